```python
import jax, jax.numpy as jnp
from jax import lax
import numpy as np

D_MODEL = 2048
BATCH = 8
SEQ = 8192
DEPTH = 2

N_MIXERS = 2
CONV_WIDTH = 3
CHUNK = 128
SG_GROUPS = 8
SG_WIDTH = D_MODEL
D_FF = 5632
RMS_EPS = 1e-5
N_A = (DEPTH + 1) // 2
N_B = DEPTH // 2

kernel_name = "hybrid_shortconv_spatialgate_convffn"


def rmsnorm(x, g):
    xf = x.astype(jnp.float32)
    inv = lax.rsqrt(jnp.mean(xf * xf, axis=-1, keepdims=True) + RMS_EPS)
    return (xf * inv).astype(x.dtype) * g


def causal_dwconv3(x, w):
    s = x.shape[1]
    xp = jnp.pad(x, ((0, 0), (CONV_WIDTH - 1, 0), (0, 0)))
    return xp[:, :s] * w[0] + xp[:, 1:s + 1] * w[1] + xp[:, 2:s + 2] * w[2]


def short_conv_mixer(h, w_in, w_conv, w_out):
    bcx = jnp.einsum('bsd,de->bse', h, w_in)
    gb, gc, xs = jnp.split(bcx, 3, axis=-1)
    y = gb * causal_dwconv3(gc * xs, w_conv)
    return jnp.einsum('bsd,de->bse', y, w_out)


def spatial_gating_mixer(h, w_in, v_norm, w_s, b_s, w_out):
    bsz, s, _ = h.shape
    z = jax.nn.gelu(jnp.einsum('bsd,de->bse', h, w_in))
    u, v = jnp.split(z, 2, axis=-1)
    v = rmsnorm(v, v_norm)
    n_chunks = s // CHUNK
    vr = v.reshape(bsz, n_chunks, CHUNK, SG_GROUPS, SG_WIDTH // SG_GROUPS)
    mask = jnp.tril(jnp.ones((CHUNK, CHUNK), dtype=w_s.dtype))
    ws = w_s * mask
    mixed = jnp.einsum('hts,bnshc->bnthc', ws, vr) + b_s.T[None, None, :, :, None]
    gate = mixed.reshape(bsz, s, SG_WIDTH)
    return jnp.einsum('bsd,de->bse', u * gate, w_out)


def conv_ffn(h, w_up, conv_w, conv_b, w_down):
    up = jnp.einsum('bsd,df->bsf', h, w_up)
    up = causal_dwconv3(up, conv_w) + conv_b
    g, a = jnp.split(up, 2, axis=-1)
    return jnp.einsum('bsf,fd->bsd', jax.nn.silu(g) * a, w_down)


def _fwd_setup_inputs(seed: int = 0) -> dict:
    key = jax.random.key(seed)
    ks = jax.random.split(key, 20)
    f32 = jnp.float32
    D = D_MODEL
    def nrm(k, shape, scale):
        return jax.random.normal(k, shape, f32) * scale
    def gain(k, shape):
        return 1.0 + 0.02 * jax.random.normal(k, shape, f32)
    return {
        "x": nrm(ks[0], (BATCH, SEQ, D), 1.0),
        "a_norm": gain(ks[1], (N_A, D)),
        "a_in": nrm(ks[2], (N_A, D, 3 * D), D ** -0.5),
        "a_conv": nrm(ks[3], (N_A, CONV_WIDTH, D), CONV_WIDTH ** -0.5),
        "a_out": nrm(ks[4], (N_A, D, D), D ** -0.5),
        "b_norm": gain(ks[5], (N_B, D)),
        "b_in": nrm(ks[6], (N_B, D, 2 * SG_WIDTH), D ** -0.5),
        "b_vnorm": gain(ks[7], (N_B, SG_WIDTH)),
        "b_ws": nrm(ks[8], (N_B, SG_GROUPS, CHUNK, CHUNK), CHUNK ** -0.5),
        "b_bs": gain(ks[9], (N_B, SG_GROUPS, CHUNK)),
        "b_out": nrm(ks[10], (N_B, SG_WIDTH, D), SG_WIDTH ** -0.5),
        "f_norm": gain(ks[11], (DEPTH, D)),
        "f_up": nrm(ks[12], (DEPTH, D, 2 * D_FF), D ** -0.5),
        "f_conv_w": nrm(ks[13], (DEPTH, CONV_WIDTH, 2 * D_FF), CONV_WIDTH ** -0.5),
        "f_conv_b": nrm(ks[14], (DEPTH, 2 * D_FF), 0.01),
        "f_down": nrm(ks[15], (DEPTH, D_FF, D), D_FF ** -0.5),
        "final_norm": gain(ks[16], (D,)),
    }


def _fwd_reference(x, a_norm, a_in, a_conv, a_out, b_norm, b_in, b_vnorm, b_ws, b_bs, b_out,
              f_norm, f_up, f_conv_w, f_conv_b, f_down, final_norm):
    for i in range(DEPTH):
        j = i // N_MIXERS
        if i % N_MIXERS == 0:
            h = rmsnorm(x, a_norm[j])
            x = x + short_conv_mixer(h, a_in[j], a_conv[j], a_out[j])
        else:
            h = rmsnorm(x, b_norm[j])
            x = x + spatial_gating_mixer(h, b_in[j], b_vnorm[j], b_ws[j], b_bs[j], b_out[j])
        h = rmsnorm(x, f_norm[i])
        x = x + conv_ffn(h, f_up[i], f_conv_w[i], f_conv_b[i], f_down[i])
    return rmsnorm(x, final_norm)


import jax as _jax
import jax.numpy as _jnp

TWIN_FORMAT = 'train_step'
FWD_PARAMS = ['x', 'a_norm', 'a_in', 'a_conv', 'a_out', 'b_norm', 'b_in', 'b_vnorm', 'b_ws', 'b_bs', 'b_out', 'f_norm', 'f_up', 'f_conv_w', 'f_conv_b', 'f_down', 'final_norm']
TWIN_WEIGHTS = ['a_norm', 'a_in', 'a_conv', 'a_out', 'b_norm', 'b_in', 'b_vnorm', 'b_ws', 'b_bs', 'b_out', 'f_norm', 'f_up', 'f_conv_w', 'f_conv_b', 'f_down', 'final_norm']
TWIN_DIFF_INPUT = 'x'
TWIN_INPUTS = ['x', 'a_norm', 'a_in', 'a_conv', 'a_out', 'b_norm', 'b_in', 'b_vnorm', 'b_ws', 'b_bs', 'b_out', 'f_norm', 'f_up', 'f_conv_w', 'f_conv_b', 'f_down', 'final_norm', 'loss_target', 'm_a_norm', 'm_a_in', 'm_a_conv', 'm_a_out', 'm_b_norm', 'm_b_in', 'm_b_vnorm', 'm_b_ws', 'm_b_bs', 'm_b_out', 'm_f_norm', 'm_f_up', 'm_f_conv_w', 'm_f_conv_b', 'm_f_down', 'm_final_norm', 'v_a_norm', 'v_a_in', 'v_a_conv', 'v_a_out', 'v_b_norm', 'v_b_in', 'v_b_vnorm', 'v_b_ws', 'v_b_bs', 'v_b_out', 'v_f_norm', 'v_f_up', 'v_f_conv_w', 'v_f_conv_b', 'v_f_down', 'v_final_norm']
TWIN_OUTPUTS = ['loss', 'grad_x', 'grad_a_norm', 'grad_a_in', 'grad_a_conv', 'grad_a_out', 'grad_b_norm', 'grad_b_in', 'grad_b_vnorm', 'grad_b_ws', 'grad_b_bs', 'grad_b_out', 'grad_f_norm', 'grad_f_up', 'grad_f_conv_w', 'grad_f_conv_b', 'grad_f_down', 'grad_final_norm', 'delta_a_norm', 'delta_a_in', 'delta_a_conv', 'delta_a_out', 'delta_b_norm', 'delta_b_in', 'delta_b_vnorm', 'delta_b_ws', 'delta_b_bs', 'delta_b_out', 'delta_f_norm', 'delta_f_up', 'delta_f_conv_w', 'delta_f_conv_b', 'delta_f_down', 'delta_final_norm', 'new_m_a_norm', 'new_m_a_in', 'new_m_a_conv', 'new_m_a_out', 'new_m_b_norm', 'new_m_b_in', 'new_m_b_vnorm', 'new_m_b_ws', 'new_m_b_bs', 'new_m_b_out', 'new_m_f_norm', 'new_m_f_up', 'new_m_f_conv_w', 'new_m_f_conv_b', 'new_m_f_down', 'new_m_final_norm', 'new_v_a_norm', 'new_v_a_in', 'new_v_a_conv', 'new_v_a_out', 'new_v_b_norm', 'new_v_b_in', 'new_v_b_vnorm', 'new_v_b_ws', 'new_v_b_bs', 'new_v_b_out', 'new_v_f_norm', 'new_v_f_up', 'new_v_f_conv_w', 'new_v_f_conv_b', 'new_v_f_down', 'new_v_final_norm']
TWIN_LEAF_KINDS = {'loss': 'loss', 'grad_x': 'grad_x', 'grad_a_norm': 'grad_w', 'grad_a_in': 'grad_w', 'grad_a_conv': 'grad_w', 'grad_a_out': 'grad_w', 'grad_b_norm': 'grad_w', 'grad_b_in': 'grad_w', 'grad_b_vnorm': 'grad_w', 'grad_b_ws': 'grad_w', 'grad_b_bs': 'grad_w', 'grad_b_out': 'grad_w', 'grad_f_norm': 'grad_w', 'grad_f_up': 'grad_w', 'grad_f_conv_w': 'grad_w', 'grad_f_conv_b': 'grad_w', 'grad_f_down': 'grad_w', 'grad_final_norm': 'grad_w', 'delta_a_norm': 'delta_w', 'delta_a_in': 'delta_w', 'delta_a_conv': 'delta_w', 'delta_a_out': 'delta_w', 'delta_b_norm': 'delta_w', 'delta_b_in': 'delta_w', 'delta_b_vnorm': 'delta_w', 'delta_b_ws': 'delta_w', 'delta_b_bs': 'delta_w', 'delta_b_out': 'delta_w', 'delta_f_norm': 'delta_w', 'delta_f_up': 'delta_w', 'delta_f_conv_w': 'delta_w', 'delta_f_conv_b': 'delta_w', 'delta_f_down': 'delta_w', 'delta_final_norm': 'delta_w', 'new_m_a_norm': 'new_m', 'new_m_a_in': 'new_m', 'new_m_a_conv': 'new_m', 'new_m_a_out': 'new_m', 'new_m_b_norm': 'new_m', 'new_m_b_in': 'new_m', 'new_m_b_vnorm': 'new_m', 'new_m_b_ws': 'new_m', 'new_m_b_bs': 'new_m', 'new_m_b_out': 'new_m', 'new_m_f_norm': 'new_m', 'new_m_f_up': 'new_m', 'new_m_f_conv_w': 'new_m', 'new_m_f_conv_b': 'new_m', 'new_m_f_down': 'new_m', 'new_m_final_norm': 'new_m', 'new_v_a_norm': 'new_v', 'new_v_a_in': 'new_v', 'new_v_a_conv': 'new_v', 'new_v_a_out': 'new_v', 'new_v_b_norm': 'new_v', 'new_v_b_in': 'new_v', 'new_v_b_vnorm': 'new_v', 'new_v_b_ws': 'new_v', 'new_v_b_bs': 'new_v', 'new_v_b_out': 'new_v', 'new_v_f_norm': 'new_v', 'new_v_f_up': 'new_v', 'new_v_f_conv_w': 'new_v', 'new_v_f_conv_b': 'new_v', 'new_v_f_down': 'new_v', 'new_v_final_norm': 'new_v'}


def _forward(args):
    return _fwd_reference(*[args[k] for k in FWD_PARAMS])


def _output_shape():
    def fwd():
        inp = _fwd_setup_inputs(0)
        return _fwd_reference(*[inp[k] for k in FWD_PARAMS])
    out = _jax.eval_shape(fwd)
    return out.shape, out.dtype

N_MICROBATCH = 1
ADAM_LR = 0.001
ADAM_B1 = 0.9
ADAM_B2 = 0.999
ADAM_EPS = 1e-08
ADAM_WD = 0.01
ADAM_STEP = 10
PER_EXAMPLE_BATCH_AXIS = {'x': 0, 'loss_target': 0}
SHARED_INPUTS = []
_WEIGHT_DTYPES = {'a_norm': _jnp.float32, 'a_in': _jnp.float32, 'a_conv': _jnp.float32, 'a_out': _jnp.float32, 'b_norm': _jnp.float32, 'b_in': _jnp.float32, 'b_vnorm': _jnp.float32, 'b_ws': _jnp.float32, 'b_bs': _jnp.float32, 'b_out': _jnp.float32, 'f_norm': _jnp.float32, 'f_up': _jnp.float32, 'f_conv_w': _jnp.float32, 'f_conv_b': _jnp.float32, 'f_down': _jnp.float32, 'final_norm': _jnp.float32}
MOMENT_SCALE = {'a_norm': 1.869250e-01, 'a_in': 1.059812e-01, 'a_conv': 1.065389e-01, 'a_out': 1.060739e-01, 'b_norm': 7.457994e-02, 'b_in': 5.112744e-02, 'b_vnorm': 3.659293e-02, 'b_ws': 4.930303e-02, 'b_bs': 7.668568e-02, 'b_out': 6.299882e-02, 'f_norm': 7.224879e-02, 'f_up': 3.021750e-02, 'f_conv_w': 3.017411e-02, 'f_conv_b': 2.954447e-02, 'f_down': 4.939856e-02, 'final_norm': 3.195564e+01}


def _to_microbatches(a, axis):
    t = _jnp.moveaxis(a, axis, 0)
    t = t.reshape((N_MICROBATCH, t.shape[0] // N_MICROBATCH) + t.shape[1:])
    return _jnp.moveaxis(t, 1, axis + 1)


def setup_inputs(seed: int = 0) -> dict:
    inp = _fwd_setup_inputs(seed)
    key = _jax.random.fold_in(_jax.random.key(seed), 7919)
    shape, _ = _output_shape()
    out = dict(inp)
    out["loss_target"] = _jax.random.normal(_jax.random.fold_in(key, 0), shape, _jnp.float32)
    for i, name in enumerate(TWIN_WEIGHTS):
        w = inp[name].astype(_jnp.float32)
        if MOMENT_SCALE is None:
            s = _jnp.sqrt(_jnp.mean(_jnp.square(w)) + 1e-30)
        else:
            s = MOMENT_SCALE[name]
        km, kv = _jax.random.split(_jax.random.fold_in(key, i + 1))
        out[name] = w
        out["m_" + name] = s * _jax.random.normal(km, w.shape, _jnp.float32)
        out["v_" + name] = (s * s) * _jax.random.uniform(kv, w.shape, _jnp.float32, 0.5, 1.5)
    if N_MICROBATCH > 1:
        for name, axis in PER_EXAMPLE_BATCH_AXIS.items():
            out[name] = _to_microbatches(out[name], axis)
    return {'x': out['x'], 'a_norm': out['a_norm'], 'a_in': out['a_in'], 'a_conv': out['a_conv'], 'a_out': out['a_out'], 'b_norm': out['b_norm'], 'b_in': out['b_in'], 'b_vnorm': out['b_vnorm'], 'b_ws': out['b_ws'], 'b_bs': out['b_bs'], 'b_out': out['b_out'], 'f_norm': out['f_norm'], 'f_up': out['f_up'], 'f_conv_w': out['f_conv_w'], 'f_conv_b': out['f_conv_b'], 'f_down': out['f_down'], 'final_norm': out['final_norm'], 'loss_target': out['loss_target'], 'm_a_norm': out['m_a_norm'], 'm_a_in': out['m_a_in'], 'm_a_conv': out['m_a_conv'], 'm_a_out': out['m_a_out'], 'm_b_norm': out['m_b_norm'], 'm_b_in': out['m_b_in'], 'm_b_vnorm': out['m_b_vnorm'], 'm_b_ws': out['m_b_ws'], 'm_b_bs': out['m_b_bs'], 'm_b_out': out['m_b_out'], 'm_f_norm': out['m_f_norm'], 'm_f_up': out['m_f_up'], 'm_f_conv_w': out['m_f_conv_w'], 'm_f_conv_b': out['m_f_conv_b'], 'm_f_down': out['m_f_down'], 'm_final_norm': out['m_final_norm'], 'v_a_norm': out['v_a_norm'], 'v_a_in': out['v_a_in'], 'v_a_conv': out['v_a_conv'], 'v_a_out': out['v_a_out'], 'v_b_norm': out['v_b_norm'], 'v_b_in': out['v_b_in'], 'v_b_vnorm': out['v_b_vnorm'], 'v_b_ws': out['v_b_ws'], 'v_b_bs': out['v_b_bs'], 'v_b_out': out['v_b_out'], 'v_f_norm': out['v_f_norm'], 'v_f_up': out['v_f_up'], 'v_f_conv_w': out['v_f_conv_w'], 'v_f_conv_b': out['v_f_conv_b'], 'v_f_down': out['v_f_down'], 'v_final_norm': out['v_final_norm']}


def _loss(weights, diff, rest, loss_target):
    with _jax.named_scope("forward"):
        args = {**rest, TWIN_DIFF_INPUT: diff, **{k: w.astype(_WEIGHT_DTYPES[k]) for k, w in weights.items()}}
        y = _forward(args)
    with _jax.named_scope("loss_head"):
        err = _jnp.square(y.astype(_jnp.float32) - loss_target)
        return 0.5 * _jnp.sum(_jnp.mean(err, axis=-1)) if err.ndim else 0.5 * err


def _adamw(w, g, m, v):
    m = ADAM_B1 * m + (1.0 - ADAM_B1) * g
    v = ADAM_B2 * v + (1.0 - ADAM_B2) * _jnp.square(g)
    m_hat = m / (1.0 - ADAM_B1 ** ADAM_STEP)
    v_hat = v / (1.0 - ADAM_B2 ** ADAM_STEP)
    delta = -ADAM_LR * (m_hat / (_jnp.sqrt(v_hat) + ADAM_EPS) + ADAM_WD * w)
    return delta, m, v


def reference(x, a_norm, a_in, a_conv, a_out, b_norm, b_in, b_vnorm, b_ws, b_bs, b_out, f_norm, f_up, f_conv_w, f_conv_b, f_down, final_norm, loss_target, m_a_norm, m_a_in, m_a_conv, m_a_out, m_b_norm, m_b_in, m_b_vnorm, m_b_ws, m_b_bs, m_b_out, m_f_norm, m_f_up, m_f_conv_w, m_f_conv_b, m_f_down, m_final_norm, v_a_norm, v_a_in, v_a_conv, v_a_out, v_b_norm, v_b_in, v_b_vnorm, v_b_ws, v_b_bs, v_b_out, v_f_norm, v_f_up, v_f_conv_w, v_f_conv_b, v_f_down, v_final_norm):
    given = dict(x=x, a_norm=a_norm, a_in=a_in, a_conv=a_conv, a_out=a_out, b_norm=b_norm, b_in=b_in, b_vnorm=b_vnorm, b_ws=b_ws, b_bs=b_bs, b_out=b_out, f_norm=f_norm, f_up=f_up, f_conv_w=f_conv_w, f_conv_b=f_conv_b, f_down=f_down, final_norm=final_norm, loss_target=loss_target, m_a_norm=m_a_norm, m_a_in=m_a_in, m_a_conv=m_a_conv, m_a_out=m_a_out, m_b_norm=m_b_norm, m_b_in=m_b_in, m_b_vnorm=m_b_vnorm, m_b_ws=m_b_ws, m_b_bs=m_b_bs, m_b_out=m_b_out, m_f_norm=m_f_norm, m_f_up=m_f_up, m_f_conv_w=m_f_conv_w, m_f_conv_b=m_f_conv_b, m_f_down=m_f_down, m_final_norm=m_final_norm, v_a_norm=v_a_norm, v_a_in=v_a_in, v_a_conv=v_a_conv, v_a_out=v_a_out, v_b_norm=v_b_norm, v_b_in=v_b_in, v_b_vnorm=v_b_vnorm, v_b_ws=v_b_ws, v_b_bs=v_b_bs, v_b_out=v_b_out, v_f_norm=v_f_norm, v_f_up=v_f_up, v_f_conv_w=v_f_conv_w, v_f_conv_b=v_f_conv_b, v_f_down=v_f_down, v_final_norm=v_final_norm)
    weights = {n: given[n] for n in TWIN_WEIGHTS}
    shared = {n: given[n] for n in SHARED_INPUTS}
    per_example = {n: given[n] for n in ['x']}
    grad_fn = _jax.value_and_grad(_loss, argnums=(0, 1))

    def one_microbatch(ex, loss_target):
        ex = dict(ex)
        diff = ex.pop(TWIN_DIFF_INPUT)
        return grad_fn(weights, diff, {**shared, **ex}, loss_target)

    if N_MICROBATCH == 1:
        loss, (grad_w, grad_x) = one_microbatch(per_example, given["loss_target"])
    else:
        def body(carry, xs):
            loss_sum, grad_sum = carry
            l_k, (gw_k, gx_k) = one_microbatch(xs[0], xs[1])
            with _jax.named_scope("update"):
                return (loss_sum + l_k, _jax.tree.map(_jnp.add, grad_sum, gw_k)), gx_k

        init = (_jnp.zeros((), _jnp.float32), _jax.tree.map(_jnp.zeros_like, weights))
        (loss, grad_w), grad_x = _jax.lax.scan(body, init, (per_example, given["loss_target"]))
    with _jax.named_scope("update"):
        delta_w, new_m, new_v = {}, {}, {}
        for n in TWIN_WEIGHTS:
            delta_w[n], new_m[n], new_v[n] = _adamw(weights[n], grad_w[n], given["m_" + n], given["v_" + n])
    return (loss, grad_x, *[grad_w[n] for n in TWIN_WEIGHTS], *[delta_w[n] for n in TWIN_WEIGHTS],
            *[new_m[n] for n in TWIN_WEIGHTS], *[new_v[n] for n in TWIN_WEIGHTS])
```

```python
import functools
import math

import jax
import jax.numpy as jnp
from jax import lax
from jax.experimental import pallas as pl
from jax.experimental.pallas import tpu as pltpu

F32 = jnp.float32
BF16 = jnp.bfloat16
MESH = pl.DeviceIdType.MESH

N_DEV = 8
RMS_EPS = 1e-5
CHUNK = 128
SG_GROUPS = 8
ADAM_LR = 0.001
ADAM_B1 = 0.9
ADAM_B2 = 0.999
ADAM_EPS = 1e-08
ADAM_WD = 0.01
ADAM_STEP = 10

LANES = 128
SLAB = 16
VMEM_LIMIT = 48 * 1024 * 1024
PACK_GRANULE = 8 * LANES


def _pick(dim, cap, mult):
    best = None
    t = mult
    while t <= min(dim, cap):
        if dim % t == 0:
            best = t
        t += mult
    return dim if best is None else best


def _params(semantics=None):
    return pltpu.CompilerParams(dimension_semantics=semantics, vmem_limit_bytes=VMEM_LIMIT)


_DIMS = {
    "nn": (((1,), (0,)), ((), ())),
    "nt": (((1,), (1,)), ((), ())),
    "tn": (((0,), (0,)), ((), ())),
}


def _matmul(a, b, mode, out_dtype, name, resid=None, tm_cap=1024, tn_cap=1024, tk_cap=512):
    if mode == "nn":
        (m, k), n = a.shape, b.shape[1]
    elif mode == "nt":
        (m, k), n = a.shape, b.shape[0]
    else:
        (k, m), n = a.shape, b.shape[1]
    tm, tn, tk = _pick(m, tm_cap, LANES), _pick(n, tn_cap, LANES), _pick(k, tk_cap, LANES)
    nk = k // tk

    def body(*refs):
        if resid is None:
            a_ref, b_ref, o_ref, acc_ref = refs
            r_ref = None
        else:
            a_ref, b_ref, r_ref, o_ref, acc_ref = refs
        kk = pl.program_id(2)
        prod = lax.dot_general(a_ref[...], b_ref[...], _DIMS[mode], preferred_element_type=F32)

        @pl.when(kk == 0)
        def _():
            acc_ref[...] = prod

        @pl.when(kk > 0)
        def _():
            acc_ref[...] += prod

        @pl.when(kk == nk - 1)
        def _():
            r = acc_ref[...]
            if r_ref is not None:
                r = r + r_ref[...]
            o_ref[...] = r.astype(out_dtype)

    a_spec = (pl.BlockSpec((tk, tm), lambda i, j, kk: (kk, i)) if mode == "tn"
              else pl.BlockSpec((tm, tk), lambda i, j, kk: (i, kk)))
    b_spec = (pl.BlockSpec((tn, tk), lambda i, j, kk: (j, kk)) if mode == "nt"
              else pl.BlockSpec((tk, tn), lambda i, j, kk: (kk, j)))
    o_spec = pl.BlockSpec((tm, tn), lambda i, j, kk: (i, j))
    in_specs = [a_spec, b_spec] + ([o_spec] if resid is not None else [])
    args = (a, b) + ((resid,) if resid is not None else ())
    return pl.pallas_call(
        body, name=name, grid=(m // tm, n // tn, nk),
        in_specs=in_specs, out_specs=o_spec,
        out_shape=jax.ShapeDtypeStruct((m, n), out_dtype),
        scratch_shapes=[pltpu.VMEM((tm, tn), F32)],
        compiler_params=_params(("parallel", "parallel", "arbitrary")),
    )(*args)


def _rms_stats(xf):
    inv = lax.rsqrt(jnp.mean(xf * xf, axis=-1, keepdims=True) + RMS_EPS)
    return inv, xf * inv


def _rmsnorm_fwd(x, g, name):
    s, d = x.shape
    tm = _pick(s, 256, SLAB)

    def body(x_ref, g_ref, h_ref):
        _, xhat = _rms_stats(x_ref[...])
        h_ref[...] = (xhat * g_ref[...]).astype(BF16)

    return pl.pallas_call(
        body, name=name, grid=(s // tm,),
        in_specs=[pl.BlockSpec((tm, d), lambda i: (i, 0)), pl.BlockSpec((1, d), lambda i: (0, 0))],
        out_specs=pl.BlockSpec((tm, d), lambda i: (i, 0)),
        out_shape=jax.ShapeDtypeStruct((s, d), BF16),
        compiler_params=_params(("parallel",)),
    )(x, g)


def _rmsnorm_bwd(x, g, dh, dx_out, name):
    s, d = x.shape
    tm = _pick(s, 256, SLAB)

    def body(x_ref, g_ref, dh_ref, dxo_ref, dxi_ref, dxib_ref, dg_ref):
        inv, xhat = _rms_stats(x_ref[...])
        dhv = dh_ref[...]
        dxhat = dhv * g_ref[...]
        proj = jnp.mean(dxhat * xhat, axis=-1, keepdims=True)
        dx = dxo_ref[...] + inv * (dxhat - xhat * proj)
        dxi_ref[...] = dx
        dxib_ref[...] = dx.astype(BF16)
        part = jnp.sum(dhv * xhat, axis=0, keepdims=True)

        @pl.when(pl.program_id(0) == 0)
        def _():
            dg_ref[...] = part

        @pl.when(pl.program_id(0) > 0)
        def _():
            dg_ref[...] += part

    row = pl.BlockSpec((tm, d), lambda i: (i, 0))
    vec = pl.BlockSpec((1, d), lambda i: (0, 0))
    return pl.pallas_call(
        body, name=name, grid=(s // tm,),
        in_specs=[row, vec, row, row], out_specs=[row, row, vec],
        out_shape=[jax.ShapeDtypeStruct((s, d), F32), jax.ShapeDtypeStruct((s, d), BF16),
                   jax.ShapeDtypeStruct((1, d), F32)],
        compiler_params=_params(("arbitrary",)),
    )(x, g, dh, dx_out)


def _final_loss(x, g, target, name):
    s, d = x.shape
    tm = _pick(s, 256, SLAB)

    def body(x_ref, g_ref, t_ref, dx_ref, dxb_ref, loss_ref, dg_ref):
        inv, xhat = _rms_stats(x_ref[...])
        gv = g_ref[...]
        err = xhat * gv - t_ref[...]
        loss = 0.5 * jnp.sum(jnp.mean(err * err, axis=-1, keepdims=True), axis=0, keepdims=True)
        dy = err * (1.0 / d)
        dxhat = dy * gv
        proj = jnp.mean(dxhat * xhat, axis=-1, keepdims=True)
        dx = inv * (dxhat - xhat * proj)
        dx_ref[...] = dx
        dxb_ref[...] = dx.astype(BF16)
        part = jnp.sum(dy * xhat, axis=0, keepdims=True)
        loss_row = jnp.broadcast_to(loss, (1, LANES))

        @pl.when(pl.program_id(0) == 0)
        def _():
            dg_ref[...] = part
            loss_ref[...] = loss_row

        @pl.when(pl.program_id(0) > 0)
        def _():
            dg_ref[...] += part
            loss_ref[...] += loss_row

    row = pl.BlockSpec((tm, d), lambda i: (i, 0))
    vec = pl.BlockSpec((1, d), lambda i: (0, 0))
    return pl.pallas_call(
        body, name=name, grid=(s // tm,),
        in_specs=[row, vec, row],
        out_specs=[row, row, pl.BlockSpec((1, LANES), lambda i: (0, 0)), vec],
        out_shape=[jax.ShapeDtypeStruct((s, d), F32), jax.ShapeDtypeStruct((s, d), BF16),
                   jax.ShapeDtypeStruct((1, LANES), F32), jax.ShapeDtypeStruct((1, d), F32)],
        compiler_params=_params(("arbitrary",)),
    )(x, g, target)


def _shift_down(prev, cur, k):
    ext = jnp.concatenate([prev, cur], axis=0)
    return pltpu.roll(ext, k, 0)[SLAB:, :]


def _shift_up(cur, nxt, k):
    ext = jnp.concatenate([cur, nxt], axis=0)
    return pltpu.roll(ext, 2 * SLAB - k, 0)[:SLAB, :]


def _conv3(w_ref, cols, prev, cur):
    s1 = _shift_down(prev, cur, 1)
    s2 = _shift_down(prev, cur, 2)
    y = w_ref[0:1, cols] * s2 + w_ref[1:2, cols] * s1 + w_ref[2:3, cols] * cur
    return y, s1, s2


def _conv3_t(w_ref, cols, cur, nxt):
    return (w_ref[2:3, cols] * cur + w_ref[1:2, cols] * _shift_up(cur, nxt, 1)
            + w_ref[0:1, cols] * _shift_up(cur, nxt, 2))


def _rows(s):
    return pl.ds(pl.multiple_of(s * SLAB, SLAB), SLAB)


def _halo_specs(tm, width, n_tiles):
    per = tm // SLAB
    prev = pl.BlockSpec((SLAB, width), lambda i: (jnp.maximum(i * per - 1, 0), 0))
    nxt = pl.BlockSpec((SLAB, width), lambda i: (jnp.minimum((i + 1) * per, n_tiles * per - 1), 0))
    return prev, nxt


def _add_rows(acc_ref, out_ref, row, cols, first):
    part = jnp.sum(acc_ref[...], axis=0, keepdims=True)

    @pl.when(first)
    def _():
        out_ref[row:row + 1, cols] = part

    @pl.when(jnp.logical_not(first))
    def _():
        out_ref[row:row + 1, cols] += part


def _mixa_fwd(bcx, wc, name):
    s, d3 = bcx.shape
    d = d3 // 3
    tm = _pick(s, 256, SLAB)
    w = _pick(d, 512, LANES)
    nslab = tm // SLAB

    def body(t_ref, prev_ref, wc_ref, y_ref):
        first_tile = pl.program_id(0) == 0
        for c in range(d // w):
            cb, cc, cx = (slice(g * d + c * w, g * d + (c + 1) * w) for g in range(3))
            cols = slice(c * w, (c + 1) * w)
            p_halo = prev_ref[:, cc].astype(F32) * prev_ref[:, cx].astype(F32)
            p_halo = jnp.where(first_tile, 0.0, p_halo)

            def slab(si, p_prev):
                r = _rows(si)
                p = t_ref[r, cc].astype(F32) * t_ref[r, cx].astype(F32)
                cv, _, _ = _conv3(wc_ref, cols, p_prev, p)
                y_ref[r, cols] = (t_ref[r, cb].astype(F32) * cv).astype(BF16)
                return p

            lax.fori_loop(0, nslab, slab, p_halo)

    prev_spec, _ = _halo_specs(tm, d3, s // tm)
    return pl.pallas_call(
        body, name=name, grid=(s // tm,),
        in_specs=[pl.BlockSpec((tm, d3), lambda i: (i, 0)), prev_spec, pl.BlockSpec((3, d), lambda i: (0, 0))],
        out_specs=pl.BlockSpec((tm, d), lambda i: (i, 0)),
        out_shape=jax.ShapeDtypeStruct((s, d), BF16),
        compiler_params=_params(("parallel",)),
    )(bcx, bcx, wc)


def _mixa_bwd(bcx, dy, wc, name):
    s, d3 = bcx.shape
    d = d3 // 3
    tm = _pick(s, 256, SLAB)
    w = _pick(d, 256, LANES)
    nslab = tm // SLAB
    n_tiles = s // tm

    def body(t_ref, prev_ref, next_ref, dy_ref, dyn_ref, wc_ref, o_ref, dwc_ref, a0, a1, a2):
        i = pl.program_id(0)
        first_tile = i == 0
        last_tile = i == n_tiles - 1
        for c in range(d // w):
            cb, cc, cx = (slice(g * d + c * w, g * d + (c + 1) * w) for g in range(3))
            cols = slice(c * w, (c + 1) * w)
            for acc in (a0, a1, a2):
                acc[...] = jnp.zeros_like(acc)
            dcv_next = jnp.where(last_tile, 0.0, dyn_ref[:, cols].astype(F32) * next_ref[:, cb].astype(F32))
            p_halo = jnp.where(first_tile, 0.0, prev_ref[:, cc].astype(F32) * prev_ref[:, cx].astype(F32))

            def one(r, p_prev, dcv_nxt):
                gb = t_ref[r, cb].astype(F32)
                gc = t_ref[r, cc].astype(F32)
                xs = t_ref[r, cx].astype(F32)
                dyv = dy_ref[r, cols].astype(F32)
                p = gc * xs
                cv, s1, s2 = _conv3(wc_ref, cols, p_prev, p)
                dcv = dyv * gb
                a2[...] += dcv * p
                a1[...] += dcv * s1
                a0[...] += dcv * s2
                dp = _conv3_t(wc_ref, cols, dcv, dcv_nxt)
                o_ref[r, cb] = (dyv * cv).astype(BF16)
                o_ref[r, cc] = (dp * xs).astype(BF16)
                o_ref[r, cx] = (dp * gc).astype(BF16)
                return dcv

            def slab(j, dcv_nxt):
                si = nslab - 1 - j
                rp = _rows(si - 1)
                p_prev = t_ref[rp, cc].astype(F32) * t_ref[rp, cx].astype(F32)
                return one(_rows(si), p_prev, dcv_nxt)

            dcv_nxt = lax.fori_loop(0, nslab - 1, slab, dcv_next)
            one(pl.ds(0, SLAB), p_halo, dcv_nxt)
            for k, acc in enumerate((a0, a1, a2)):
                _add_rows(acc, dwc_ref, k, cols, first_tile)

    prev_spec, next_spec = _halo_specs(tm, d3, n_tiles)
    _, next_dy = _halo_specs(tm, d, n_tiles)
    return pl.pallas_call(
        body, name=name, grid=(n_tiles,),
        in_specs=[pl.BlockSpec((tm, d3), lambda i: (i, 0)), prev_spec, next_spec,
                  pl.BlockSpec((tm, d), lambda i: (i, 0)), next_dy, pl.BlockSpec((3, d), lambda i: (0, 0))],
        out_specs=[pl.BlockSpec((tm, d3), lambda i: (i, 0)), pl.BlockSpec((3, d), lambda i: (0, 0))],
        out_shape=[jax.ShapeDtypeStruct((s, d3), BF16), jax.ShapeDtypeStruct((3, d), F32)],
        scratch_shapes=[pltpu.VMEM((SLAB, w), F32)] * 3,
        compiler_params=_params(("arbitrary",)),
    )(bcx, bcx, bcx, dy, dy, wc)


def _sigmoid(z):
    return 1.0 / (1.0 + jnp.exp(-z))


def _ffn_fwd(up, cw, cb, name):
    s, f2 = up.shape
    f = f2 // 2
    tm = _pick(s, 256, SLAB)
    w = _pick(f, 512, LANES)
    nslab = tm // SLAB

    def body(t_ref, prev_ref, cw_ref, cb_ref, act_ref):
        first_tile = pl.program_id(0) == 0
        for c in range(f // w):
            cg = slice(c * w, (c + 1) * w)
            ca = slice(f + c * w, f + (c + 1) * w)
            halo = tuple(jnp.where(first_tile, 0.0, prev_ref[:, cs].astype(F32)) for cs in (cg, ca))

            def slab(si, carry):
                r = _rows(si)
                g = t_ref[r, cg].astype(F32)
                a = t_ref[r, ca].astype(F32)
                gcv = _conv3(cw_ref, cg, carry[0], g)[0] + cb_ref[:, cg]
                acv = _conv3(cw_ref, ca, carry[1], a)[0] + cb_ref[:, ca]
                act_ref[r, cg] = (gcv * _sigmoid(gcv) * acv).astype(BF16)
                return g, a

            lax.fori_loop(0, nslab, slab, halo)

    prev_spec, _ = _halo_specs(tm, f2, s // tm)
    return pl.pallas_call(
        body, name=name, grid=(s // tm,),
        in_specs=[pl.BlockSpec((tm, f2), lambda i: (i, 0)), prev_spec,
                  pl.BlockSpec((3, f2), lambda i: (0, 0)), pl.BlockSpec((1, f2), lambda i: (0, 0))],
        out_specs=pl.BlockSpec((tm, f), lambda i: (i, 0)),
        out_shape=jax.ShapeDtypeStruct((s, f), BF16),
        compiler_params=_params(("parallel",)),
    )(up, up, cw, cb)


def _ffn_bwd(up, dact, cw, cb, name):
    s, f2 = up.shape
    f = f2 // 2
    tm = _pick(s, 128, SLAB)
    w = _pick(f, 256, LANES)
    nslab = tm // SLAB
    n_tiles = s // tm

    def body(t_ref, prev_ref, next_ref, da_ref, dan_ref, cw_ref, cb_ref, o_ref, dcw_ref, dcb_ref, *accs):
        i = pl.program_id(0)
        first_tile = i == 0
        last_tile = i == n_tiles - 1
        last_rows = pl.ds((nslab - 1) * SLAB, SLAB)
        for c in range(f // w):
            cg = slice(c * w, (c + 1) * w)
            ca = slice(f + c * w, f + (c + 1) * w)
            for acc in accs:
                acc[...] = jnp.zeros_like(acc)

            def grads(dav, g_prev, a_prev, g, a):
                gcv, g1, g2 = _conv3(cw_ref, cg, g_prev, g)
                acv, a1, a2 = _conv3(cw_ref, ca, a_prev, a)
                gcv = gcv + cb_ref[:, cg]
                acv = acv + cb_ref[:, ca]
                sg = _sigmoid(gcv)
                d_a = dav * (gcv * sg)
                d_g = dav * acv * (sg * (1.0 + gcv * (1.0 - sg)))
                return d_g, d_a, (g1, g2, a1, a2)

            nxt = grads(dan_ref[:, cg].astype(F32), t_ref[last_rows, cg].astype(F32),
                        t_ref[last_rows, ca].astype(F32), next_ref[:, cg].astype(F32),
                        next_ref[:, ca].astype(F32))[:2]
            nxt = tuple(jnp.where(last_tile, 0.0, v) for v in nxt)
            halo = tuple(jnp.where(first_tile, 0.0, prev_ref[:, cs].astype(F32)) for cs in (cg, ca))

            def one(r, g_prev, a_prev, carry):
                g = t_ref[r, cg].astype(F32)
                a = t_ref[r, ca].astype(F32)
                d_g, d_a, (g1, g2, a1, a2) = grads(da_ref[r, cg].astype(F32), g_prev, a_prev, g, a)
                for acc, term in zip(accs, (d_g * g2, d_g * g1, d_g * g, d_g, d_a * a2, d_a * a1, d_a * a, d_a)):
                    acc[...] += term
                o_ref[r, cg] = _conv3_t(cw_ref, cg, d_g, carry[0]).astype(BF16)
                o_ref[r, ca] = _conv3_t(cw_ref, ca, d_a, carry[1]).astype(BF16)
                return d_g, d_a

            def slab(j, carry):
                si = nslab - 1 - j
                rp = _rows(si - 1)
                return one(_rows(si), t_ref[rp, cg].astype(F32), t_ref[rp, ca].astype(F32), carry)

            carry = lax.fori_loop(0, nslab - 1, slab, nxt)
            one(pl.ds(0, SLAB), halo[0], halo[1], carry)
            for half, cs in enumerate((cg, ca)):
                for k in range(3):
                    _add_rows(accs[4 * half + k], dcw_ref, k, cs, first_tile)
                _add_rows(accs[4 * half + 3], dcb_ref, 0, cs, first_tile)

    prev_spec, next_spec = _halo_specs(tm, f2, n_tiles)
    _, next_da = _halo_specs(tm, f, n_tiles)
    return pl.pallas_call(
        body, name=name, grid=(n_tiles,),
        in_specs=[pl.BlockSpec((tm, f2), lambda i: (i, 0)), prev_spec, next_spec,
                  pl.BlockSpec((tm, f), lambda i: (i, 0)), next_da,
                  pl.BlockSpec((3, f2), lambda i: (0, 0)), pl.BlockSpec((1, f2), lambda i: (0, 0))],
        out_specs=[pl.BlockSpec((tm, f2), lambda i: (i, 0)), pl.BlockSpec((3, f2), lambda i: (0, 0)),
                   pl.BlockSpec((1, f2), lambda i: (0, 0))],
        out_shape=[jax.ShapeDtypeStruct((s, f2), BF16), jax.ShapeDtypeStruct((3, f2), F32),
                   jax.ShapeDtypeStruct((1, f2), F32)],
        scratch_shapes=[pltpu.VMEM((SLAB, w), F32)] * 8,
        compiler_params=_params(("arbitrary",)),
    )(up, up, up, dact, dact, cw, cb)


_GELU_C = math.sqrt(2.0 / math.pi)


def _gelu(x):
    th = jnp.tanh(_GELU_C * (x + 0.044715 * (x * x * x)))
    return x * (0.5 * (1.0 + th)), th


def _gelu_grad(x, th):
    return 0.5 * (1.0 + th) + 0.5 * x * (1.0 - th * th) * (_GELU_C * (1.0 + 3.0 * 0.044715 * (x * x)))


def _masked_ws(ws_ref, h):
    t = lax.broadcasted_iota(jnp.int32, (CHUNK, CHUNK), 0)
    sx = lax.broadcasted_iota(jnp.int32, (CHUNK, CHUNK), 1)
    return jnp.where(sx <= t, ws_ref[h], 0.0)


def _mixb_fwd(pre, gv, ws, bs_wide, name):
    s, w2 = pre.shape
    w = w2 // 2
    gw = w // SG_GROUPS

    def body(pre_ref, gv_ref, ws_ref, bs_ref, o_ref):
        zu, _ = _gelu(pre_ref[:, :w].astype(F32))
        zv, _ = _gelu(pre_ref[:, w:].astype(F32))
        _, vhat = _rms_stats(zv)
        vn = (vhat * gv_ref[...]).astype(BF16)
        for h in range(SG_GROUPS):
            cols = slice(h * gw, (h + 1) * gw)
            wsm = _masked_ws(ws_ref, h).astype(BF16)
            gate = jnp.dot(wsm, vn[:, cols], preferred_element_type=F32)
            gate = gate + jnp.tile(bs_ref[h], (1, gw // LANES))
            o_ref[:, cols] = (zu[:, cols] * gate).astype(BF16)

    return pl.pallas_call(
        body, name=name, grid=(s // CHUNK,),
        in_specs=[pl.BlockSpec((CHUNK, w2), lambda i: (i, 0)), pl.BlockSpec((1, w), lambda i: (0, 0)),
                  pl.BlockSpec((SG_GROUPS, CHUNK, CHUNK), lambda i: (0, 0, 0)),
                  pl.BlockSpec((SG_GROUPS, CHUNK, LANES), lambda i: (0, 0, 0))],
        out_specs=pl.BlockSpec((CHUNK, w), lambda i: (i, 0)),
        out_shape=jax.ShapeDtypeStruct((s, w), BF16),
        compiler_params=_params(("parallel",)),
    )(pre, gv, ws, bs_wide)


def _mixb_bwd(pre, dug, gv, ws, bs_wide, name):
    s, w2 = pre.shape
    w = w2 // 2
    gw = w // SG_GROUPS

    def body(pre_ref, dug_ref, gv_ref, ws_ref, bs_ref, o_ref, dws_ref, dbs_ref, dgv_ref, dvn_ref):
        first = pl.program_id(0) == 0

        @pl.when(first)
        def _():
            dws_ref[...] = jnp.zeros_like(dws_ref)
            dbs_ref[...] = jnp.zeros_like(dbs_ref)

        pu = pre_ref[:, :w].astype(F32)
        pv = pre_ref[:, w:].astype(F32)
        zu, thu = _gelu(pu)
        zv, thv = _gelu(pv)
        inv, vhat = _rms_stats(zv)
        gvv = gv_ref[...]
        vn = (vhat * gvv).astype(BF16)
        for h in range(SG_GROUPS):
            cols = slice(h * gw, (h + 1) * gw)
            wsm = _masked_ws(ws_ref, h).astype(BF16)
            gate = jnp.dot(wsm, vn[:, cols], preferred_element_type=F32)
            gate = gate + jnp.tile(bs_ref[h], (1, gw // LANES))
            dug_h = dug_ref[:, cols].astype(F32)
            dgate = dug_h * zu[:, cols]
            dgate_b = dgate.astype(BF16)
            o_ref[:, cols] = (dug_h * gate * _gelu_grad(pu[:, cols], thu[:, cols])).astype(BF16)
            dbs_ref[h] += jnp.broadcast_to(jnp.sum(dgate, axis=-1, keepdims=True), (CHUNK, LANES))
            dws = lax.dot_general(dgate_b, vn[:, cols], _DIMS["nt"], preferred_element_type=F32)
            t = lax.broadcasted_iota(jnp.int32, (CHUNK, CHUNK), 0)
            sx = lax.broadcasted_iota(jnp.int32, (CHUNK, CHUNK), 1)
            dws_ref[h] += jnp.where(sx <= t, dws, 0.0)
            dvn_ref[:, cols] = lax.dot_general(wsm, dgate_b, _DIMS["tn"], preferred_element_type=F32)
        dvn = dvn_ref[...]
        part = jnp.sum(dvn * vhat, axis=0, keepdims=True)

        @pl.when(first)
        def _():
            dgv_ref[...] = part

        @pl.when(jnp.logical_not(first))
        def _():
            dgv_ref[...] += part

        dvhat = dvn * gvv
        dzv = inv * (dvhat - vhat * jnp.mean(dvhat * vhat, axis=-1, keepdims=True))
        o_ref[:, w:] = (dzv * _gelu_grad(pv, thv)).astype(BF16)

    return pl.pallas_call(
        body, name=name, grid=(s // CHUNK,),
        in_specs=[pl.BlockSpec((CHUNK, w2), lambda i: (i, 0)), pl.BlockSpec((CHUNK, w), lambda i: (i, 0)),
                  pl.BlockSpec((1, w), lambda i: (0, 0)),
                  pl.BlockSpec((SG_GROUPS, CHUNK, CHUNK), lambda i: (0, 0, 0)),
                  pl.BlockSpec((SG_GROUPS, CHUNK, LANES), lambda i: (0, 0, 0))],
        out_specs=[pl.BlockSpec((CHUNK, w2), lambda i: (i, 0)),
                   pl.BlockSpec((SG_GROUPS, CHUNK, CHUNK), lambda i: (0, 0, 0)),
                   pl.BlockSpec((SG_GROUPS, CHUNK, LANES), lambda i: (0, 0, 0)),
                   pl.BlockSpec((1, w), lambda i: (0, 0))],
        out_shape=[jax.ShapeDtypeStruct((s, w2), BF16), jax.ShapeDtypeStruct((SG_GROUPS, CHUNK, CHUNK), F32),
                   jax.ShapeDtypeStruct((SG_GROUPS, CHUNK, LANES), F32), jax.ShapeDtypeStruct((1, w), F32)],
        scratch_shapes=[pltpu.VMEM((CHUNK, w), F32)],
        compiler_params=_params(("arbitrary",)),
    )(pre, dug, gv, ws, bs_wide)


def _cast_layer(w3, layer, name):
    _, r, c = w3.shape
    tr = _pick(r, 256, SLAB)

    def body(w_ref, o_ref):
        o_ref[...] = w_ref[...].astype(BF16)

    return pl.pallas_call(
        body, name=name, grid=(r // tr,),
        in_specs=[pl.BlockSpec((None, tr, c), lambda i: (layer, i, 0))],
        out_specs=pl.BlockSpec((tr, c), lambda i: (i, 0)),
        out_shape=jax.ShapeDtypeStruct((r, c), BF16),
        compiler_params=_params(("parallel",)),
    )(w3)


def _adamw_math(w, g, m, v):
    m = ADAM_B1 * m + (1.0 - ADAM_B1) * g
    v = ADAM_B2 * v + (1.0 - ADAM_B2) * (g * g)
    m_hat = m / (1.0 - ADAM_B1 ** ADAM_STEP)
    v_hat = v / (1.0 - ADAM_B2 ** ADAM_STEP)
    delta = -ADAM_LR * (m_hat / (jnp.sqrt(v_hat) + ADAM_EPS) + ADAM_WD * w)
    return delta, m, v


def _adamw_sharded(recv, w, m, v, name):
    nl, r, c = w.shape
    tc = _pick(c, 1024, LANES)
    tr = _pick(r, 64, SLAB)

    def body(recv_ref, w_ref, m_ref, v_ref, g_ref, d_ref, nm_ref, nv_ref):
        g = recv_ref[0].astype(F32)
        for q in range(1, N_DEV):
            g = g + recv_ref[q].astype(F32)
        delta, nm, nv = _adamw_math(w_ref[...], g, m_ref[...], v_ref[...])
        g_ref[...] = g
        d_ref[...] = delta
        nm_ref[...] = nm
        nv_ref[...] = nv

    blk = pl.BlockSpec((None, tr, tc), lambda l, i, j: (l, i, j))
    out = jax.ShapeDtypeStruct((nl, r, c), F32)
    return pl.pallas_call(
        body, name=name, grid=(nl, r // tr, c // tc),
        in_specs=[pl.BlockSpec((N_DEV, None, tr, tc), lambda l, i, j: (0, l, i, j)), blk, blk, blk],
        out_specs=[blk] * 4, out_shape=[out] * 4,
        compiler_params=_params(("parallel",) * 3),
    )(recv, w, m, v)


def _adamw_packed(w, g, m, v, name):
    r, c = w.shape
    tr = _pick(r, 256, 8)

    def body(w_ref, g_ref, m_ref, v_ref, d_ref, nm_ref, nv_ref):
        delta, nm, nv = _adamw_math(w_ref[...], g_ref[...], m_ref[...], v_ref[...])
        d_ref[...] = delta
        nm_ref[...] = nm
        nv_ref[...] = nv

    blk = pl.BlockSpec((tr, c), lambda i: (i, 0))
    out = jax.ShapeDtypeStruct((r, c), F32)
    return pl.pallas_call(
        body, name=name, grid=(r // tr,), in_specs=[blk] * 4, out_specs=[blk] * 3, out_shape=[out] * 3,
        compiler_params=_params(("parallel",)),
    )(w, g, m, v)


def _pack(arrays):
    parts = []
    for a in arrays:
        flat = a.reshape(-1).astype(F32)
        pad = (-flat.shape[0]) % PACK_GRANULE
        parts.append(jnp.pad(flat, (0, pad)) if pad else flat)
    return jnp.concatenate(parts).reshape(-1, LANES)


def _unpack(buf, shapes):
    flat = buf.reshape(-1)
    out, off = [], 0
    for shp in shapes:
        n = math.prod(shp)
        out.append(flat[off:off + n].reshape(shp))
        off += n + (-n) % PACK_GRANULE
    return out


def _mesh_pos():
    return lax.axis_index("x"), lax.axis_index("y"), lax.axis_index("c")


def _coords(q):
    return q // 4, (q // 2) % 2, q % 2


def _shard_of(ref, q, shard_shape, axis):
    r, c = shard_shape
    if axis == 0:
        return ref.at[pl.ds(pl.multiple_of(q * r, SLAB), r), :]
    return ref.at[:, pl.ds(pl.multiple_of(q * c, LANES), c)]


def _all_gather(shards, axes, name):
    n = len(shards)

    def body(*refs):
        srcs, dsts = refs[:n], refs[n:2 * n]
        send_sems, recv_sems, local_sems = refs[2 * n:]
        x, y, c = _mesh_pos()
        sibling = (x, y, 1 - c)
        chips = [(1 - x, y), (x, 1 - y), (1 - x, 1 - y)]

        def block(t, dev):
            px, py, pc = dev
            return _shard_of(dsts[t], 4 * px + 2 * py + pc, shards[t].shape, axes[t])

        def copy(t, k, owner, to, src=None):
            return pltpu.make_async_remote_copy(
                src_ref=block(t, owner) if src is None else src, dst_ref=block(t, owner),
                send_sem=send_sems.at[t, k], recv_sem=recv_sems.at[t, k], device_id=to, device_id_type=MESH)

        me = (x, y, c)
        mine = [pltpu.make_async_copy(srcs[t], block(t, me), local_sems.at[t]) for t in range(n)]
        first = []
        for t in range(n):
            mine[t].start()
            for j, chip in enumerate(chips):
                first.append(copy(t, 1 + j, me, (*chip, c), src=srcs[t]))
            first.append(copy(t, 0, me, sibling, src=srcs[t]))
        for cp in first:
            cp.start()
        passed = []
        for t in range(n):
            for j, chip in enumerate(chips):
                copy(t, 1 + j, (*chip, c), me).wait_recv()
                fwd = copy(t, 4 + j, (*chip, c), sibling)
                fwd.start()
                passed.append(fwd)
        for t in range(n):
            copy(t, 0, sibling, me).wait_recv()
            for j, chip in enumerate(chips):
                copy(t, 4 + j, (*chip, 1 - c), me).wait_recv()
        for cp in first + passed:
            cp.wait_send()
        for cp in mine:
            cp.wait()

    def full_shape(a, axis):
        r, c = a.shape
        return (r * N_DEV, c) if axis == 0 else (r, c * N_DEV)

    any_spec = pl.BlockSpec(memory_space=pl.ANY)
    return pl.pallas_call(
        body, name=name,
        in_specs=[any_spec] * n, out_specs=[any_spec] * n,
        out_shape=[jax.ShapeDtypeStruct(full_shape(a, ax), a.dtype) for a, ax in zip(shards, axes)],
        scratch_shapes=[pltpu.SemaphoreType.DMA((n, 7)), pltpu.SemaphoreType.DMA((n, 7)),
                        pltpu.SemaphoreType.DMA((n,))],
        compiler_params=pltpu.CompilerParams(has_side_effects=True),
    )(*shards)


def _reduce_scatter_send(grads, axes, name):
    flat = [(t, l) for t, g in enumerate(grads) for l in range(len(g))]
    n = len(flat)

    def shard_shape(t):
        r, c = grads[t][0].shape
        return (r // N_DEV, c) if axes[t] == 0 else (r, c // N_DEV)

    def body(*refs):
        srcs, dsts = refs[:n], refs[n:n + len(grads)]
        send_sems, recv_sems, local_sems = refs[n + len(grads):]
        x, y, c = _mesh_pos()
        me = 4 * x + 2 * y + c
        copies = []
        for i, (t, l) in enumerate(flat):
            own = pltpu.make_async_copy(_shard_of(srcs[i], me, shard_shape(t), axes[t]), dsts[t].at[me, l],
                                        local_sems.at[i])
            own.start()
            copies.append(own)
        sends = []
        for step in range(1, N_DEV):
            to = (me + step) % N_DEV
            for i, (t, l) in enumerate(flat):
                cp = pltpu.make_async_remote_copy(
                    src_ref=_shard_of(srcs[i], to, shard_shape(t), axes[t]), dst_ref=dsts[t].at[me, l],
                    send_sem=send_sems.at[i, step - 1], recv_sem=recv_sems.at[i, step - 1],
                    device_id=_coords(to), device_id_type=MESH)
                cp.start()
                sends.append(cp)
        for step in range(1, N_DEV):
            frm = (me + N_DEV - step) % N_DEV
            for i, (t, l) in enumerate(flat):
                pltpu.make_async_remote_copy(
                    src_ref=_shard_of(srcs[i], me, shard_shape(t), axes[t]), dst_ref=dsts[t].at[frm, l],
                    send_sem=send_sems.at[i, step - 1], recv_sem=recv_sems.at[i, step - 1],
                    device_id=_coords(frm), device_id_type=MESH).wait_recv()
        for cp in sends:
            cp.wait_send()
        for cp in copies:
            cp.wait()

    any_spec = pl.BlockSpec(memory_space=pl.ANY)
    return pl.pallas_call(
        body, name=name,
        in_specs=[any_spec] * n, out_specs=[any_spec] * len(grads),
        out_shape=[jax.ShapeDtypeStruct((N_DEV, len(g)) + shard_shape(t), g[0].dtype)
                   for t, g in enumerate(grads)],
        scratch_shapes=[pltpu.SemaphoreType.DMA((n, N_DEV - 1)), pltpu.SemaphoreType.DMA((n, N_DEV - 1)),
                        pltpu.SemaphoreType.DMA((n,))],
        compiler_params=pltpu.CompilerParams(has_side_effects=True),
    )(*[g for gl in grads for g in gl])


def _all_reduce_small(part, name):
    r, c = part.shape

    def body(part_ref, out_ref, slots, send_sems, recv_sems, local_sem):
        x, y, cc = _mesh_pos()
        me = 4 * x + 2 * y + cc
        own = pltpu.make_async_copy(part_ref, slots.at[me], local_sem)
        own.start()
        sends = []
        for step in range(1, N_DEV):
            to = (me + step) % N_DEV
            cp = pltpu.make_async_remote_copy(
                src_ref=part_ref, dst_ref=slots.at[me], send_sem=send_sems.at[step - 1],
                recv_sem=recv_sems.at[step - 1], device_id=_coords(to), device_id_type=MESH)
            cp.start()
            sends.append(cp)
        for step in range(1, N_DEV):
            frm = (me + N_DEV - step) % N_DEV
            pltpu.make_async_remote_copy(
                src_ref=part_ref, dst_ref=slots.at[frm], send_sem=send_sems.at[step - 1],
                recv_sem=recv_sems.at[step - 1], device_id=_coords(frm), device_id_type=MESH).wait_recv()
        for cp in sends:
            cp.wait_send()
        own.wait()
        total = slots[0]
        for q in range(1, N_DEV):
            total = total + slots[q]
        out_ref[...] = total

    vmem = pl.BlockSpec(memory_space=pltpu.VMEM)
    return pl.pallas_call(
        body, name=name, in_specs=[vmem], out_specs=vmem,
        out_shape=jax.ShapeDtypeStruct((r, c), F32),
        scratch_shapes=[pltpu.VMEM((N_DEV, r, c), F32), pltpu.SemaphoreType.DMA((N_DEV - 1,)),
                        pltpu.SemaphoreType.DMA((N_DEV - 1,)), pltpu.SemaphoreType.DMA],
        compiler_params=pltpu.CompilerParams(has_side_effects=True, vmem_limit_bytes=VMEM_LIMIT),
    )(part)


def _gather_cols(full_rows, n_rows, shard_cols):
    return full_rows.reshape(N_DEV, n_rows, shard_cols).transpose(1, 0, 2).reshape(n_rows, N_DEV * shard_cols)


def kernel(x, a_norm, a_in, a_conv, a_out, b_norm, b_in, b_vnorm, b_ws, b_bs, b_out, f_norm, f_up, f_conv_w, f_conv_b, f_down, final_norm, loss_target, m_a_norm, m_a_in, m_a_conv, m_a_out, m_b_norm, m_b_in, m_b_vnorm, m_b_ws, m_b_bs, m_b_out, m_f_norm, m_f_up, m_f_conv_w, m_f_conv_b, m_f_down, m_final_norm, v_a_norm, v_a_in, v_a_conv, v_a_out, v_b_norm, v_b_in, v_b_vnorm, v_b_ws, v_b_bs, v_b_out, v_f_norm, v_f_up, v_f_conv_w, v_f_conv_b, v_f_down, v_final_norm):
    s, d = x.shape[1], x.shape[2]
    n_ffn = f_up.shape[0]
    f2 = f_up.shape[2] * N_DEV
    me = 4 * lax.axis_index("x") + 2 * lax.axis_index("y") + lax.axis_index("c")
    x0 = x.reshape(s, d)
    target = loss_target.reshape(s, d)

    small_shard = _pack([a_conv, b_norm, b_vnorm, f_conv_w])
    shards = [_cast_layer(a_in, 0, "cast_a_in"), _cast_layer(a_out, 0, "cast_a_out")]
    axes = [1, 0]
    for l in range(n_ffn):
        shards += [_cast_layer(f_up, l, f"cast_f_up{l}"), _cast_layer(f_down, l, f"cast_f_down{l}")]
        axes += [1, 0]
    shards += [_cast_layer(b_in, 0, "cast_b_in"), _cast_layer(b_out, 0, "cast_b_out"), small_shard]
    axes += [1, 0, 0]
    gathered = _all_gather(shards, axes, "all_gather_weights")
    w_a_in, w_a_out = gathered[0], gathered[1]
    w_f_up = [gathered[2 + 2 * l] for l in range(n_ffn)]
    w_f_down = [gathered[3 + 2 * l] for l in range(n_ffn)]
    w_b_in, w_b_out, small_full = gathered[2 + 2 * n_ffn:]
    cshard = a_conv.shape[2]
    fshard = f_conv_w.shape[2]
    small_rows = small_full.reshape(N_DEV, -1)
    per_dev = _unpack_rows(small_rows, [(3, cshard), (cshard,), (cshard,), (n_ffn, 3, fshard)])
    a_conv_full = per_dev[0].transpose(1, 0, 2).reshape(3, d)
    b_norm_full = per_dev[1].reshape(1, d)
    b_vnorm_full = per_dev[2].reshape(1, d)
    f_conv_w_full = per_dev[3].transpose(1, 2, 0, 3).reshape(n_ffn, 3, f2)
    bs_wide = jnp.broadcast_to(b_bs[0][:, :, None], (SG_GROUPS, CHUNK, LANES))
    ws = b_ws[0]

    def ffn_forward(xin, l):
        h = _rmsnorm_fwd(xin, f_norm[l:l + 1], f"ffn{l}_norm")
        up = _matmul(h, w_f_up[l], "nn", BF16, f"ffn{l}_up")
        act = _ffn_fwd(up, f_conv_w_full[l], f_conv_b[l:l + 1], f"ffn{l}_mid")
        xout = _matmul(act, w_f_down[l], "nn", F32, f"ffn{l}_down", resid=xin)
        return xout, (h, up, act)

    h0 = _rmsnorm_fwd(x0, a_norm, "mixa_norm")
    bcx = _matmul(h0, w_a_in, "nn", BF16, "mixa_in")
    ya = _mixa_fwd(bcx, a_conv_full, "mixa_mid")
    x1 = _matmul(ya, w_a_out, "nn", F32, "mixa_out", resid=x0)
    x2, saved0 = ffn_forward(x1, 0)
    h2 = _rmsnorm_fwd(x2, b_norm_full, "mixb_norm")
    pre = _matmul(h2, w_b_in, "nn", BF16, "mixb_in")
    ug = _mixb_fwd(pre, b_vnorm_full, ws, bs_wide, "mixb_mid")
    x3 = _matmul(ug, w_b_out, "nn", F32, "mixb_out", resid=x2)
    x4, saved1 = ffn_forward(x3, 1)
    dx4, dx4b, loss_part, g_final = _final_loss(x4, final_norm.reshape(1, d), target, "loss_head")

    def ffn_backward(xin, l, saved, dx, dxb):
        h, up, act = saved
        dact = _matmul(dxb, w_f_down[l], "nt", BF16, f"ffn{l}_down_dx")
        g_down = _matmul(act, dxb, "tn", BF16, f"ffn{l}_down_dw")
        dup, g_cw, g_cb = _ffn_bwd(up, dact, f_conv_w_full[l], f_conv_b[l:l + 1], f"ffn{l}_mid_bwd")
        dh = _matmul(dup, w_f_up[l], "nt", F32, f"ffn{l}_up_dx")
        g_up = _matmul(h, dup, "tn", BF16, f"ffn{l}_up_dw")
        dxin, dxinb, g_norm = _rmsnorm_bwd(xin, f_norm[l:l + 1], dh, dx, f"ffn{l}_norm_bwd")
        return dxin, dxinb, (g_up, g_down, g_cw, g_cb, g_norm)

    dx3, dx3b, gf1 = ffn_backward(x3, 1, saved1, dx4, dx4b)
    dug = _matmul(dx3b, w_b_out, "nt", BF16, "mixb_out_dx")
    g_b_out = _matmul(ug, dx3b, "tn", BF16, "mixb_out_dw")
    dpre, g_ws, g_bs_wide, g_bvnorm = _mixb_bwd(pre, dug, b_vnorm_full, ws, bs_wide, "mixb_mid_bwd")
    dh2 = _matmul(dpre, w_b_in, "nt", F32, "mixb_in_dx")
    g_b_in = _matmul(h2, dpre, "tn", BF16, "mixb_in_dw")
    dx2, dx2b, g_bnorm = _rmsnorm_bwd(x2, b_norm_full, dh2, dx3, "mixb_norm_bwd")
    dx1, dx1b, gf0 = ffn_backward(x1, 0, saved0, dx2, dx2b)
    dya = _matmul(dx1b, w_a_out, "nt", BF16, "mixa_out_dx")
    g_a_out = _matmul(ya, dx1b, "tn", BF16, "mixa_out_dw")
    dbcx, g_aconv = _mixa_bwd(bcx, dya, a_conv_full, "mixa_mid_bwd")
    dh0 = _matmul(dbcx, w_a_in, "nt", F32, "mixa_in_dx")
    g_a_in = _matmul(h0, dbcx, "tn", BF16, "mixa_in_dw")
    grad_x, _, g_anorm = _rmsnorm_bwd(x0, a_norm, dh0, dx1, "mixa_norm_bwd")

    recv = _reduce_scatter_send(
        [[g_a_in], [g_a_out], [g_b_in], [g_b_out], [gf0[0], gf1[0]], [gf0[1], gf1[1]]],
        [1, 0, 1, 0, 1, 0], "reduce_scatter_grads")
    big = {}
    for name, rv, w, m, v in (("a_in", recv[0], a_in, m_a_in, v_a_in), ("a_out", recv[1], a_out, m_a_out, v_a_out),
                              ("b_in", recv[2], b_in, m_b_in, v_b_in), ("b_out", recv[3], b_out, m_b_out, v_b_out),
                              ("f_up", recv[4], f_up, m_f_up, v_f_up), ("f_down", recv[5], f_down, m_f_down, v_f_down)):
        big[name] = _adamw_sharded(rv, w, m, v, f"adamw_{name}")

    full_shapes = [(1, LANES), (1, d), (3, d), (1, d), (1, d), (SG_GROUPS, CHUNK, CHUNK), (SG_GROUPS, CHUNK),
                   (n_ffn, d), (n_ffn, 3, f2), (n_ffn, f2), (1, d)]
    parts = [loss_part, g_anorm, g_aconv, g_bnorm, g_bvnorm, g_ws, g_bs_wide[:, :, 0],
             jnp.concatenate([gf0[4], gf1[4]], axis=0), jnp.stack([gf0[2], gf1[2]]),
             jnp.concatenate([gf0[3], gf1[3]], axis=0), g_final]
    total = _all_reduce_small(_pack(parts), "all_reduce_small")
    (loss_v, r_anorm, r_aconv, r_bnorm, r_bvnorm, r_ws, r_bs, r_fnorm, r_fcw, r_fcb, r_final) = _unpack(total, full_shapes)
    small_grads = [
        r_anorm,
        lax.dynamic_slice_in_dim(r_aconv, me * cshard, cshard, axis=1).reshape(a_conv.shape),
        lax.dynamic_slice_in_dim(r_bnorm, me * cshard, cshard, axis=1),
        lax.dynamic_slice_in_dim(r_bvnorm, me * cshard, cshard, axis=1),
        r_ws.reshape(b_ws.shape), r_bs.reshape(b_bs.shape), r_fnorm,
        lax.dynamic_slice_in_dim(r_fcw, me * fshard, fshard, axis=2),
        r_fcb, r_final.reshape(final_norm.shape)]
    small_w = [a_norm, a_conv, b_norm, b_vnorm, b_ws, b_bs, f_norm, f_conv_w, f_conv_b, final_norm]
    small_m = [m_a_norm, m_a_conv, m_b_norm, m_b_vnorm, m_b_ws, m_b_bs, m_f_norm, m_f_conv_w, m_f_conv_b, m_final_norm]
    small_v = [v_a_norm, v_a_conv, v_b_norm, v_b_vnorm, v_b_ws, v_b_bs, v_f_norm, v_f_conv_w, v_f_conv_b, v_final_norm]
    shapes = [w.shape for w in small_w]
    packed = _adamw_packed(_pack(small_w), _pack(small_grads), _pack(small_m), _pack(small_v), "adamw_small")
    s_delta, s_m, s_v = (_unpack(p, shapes) for p in packed)
    small_names = ["a_norm", "a_conv", "b_norm", "b_vnorm", "b_ws", "b_bs", "f_norm", "f_conv_w", "f_conv_b", "final_norm"]
    small = {nm: (small_grads[i], s_delta[i], s_m[i], s_v[i]) for i, nm in enumerate(small_names)}

    order = ["a_norm", "a_in", "a_conv", "a_out", "b_norm", "b_in", "b_vnorm", "b_ws", "b_bs", "b_out",
             "f_norm", "f_up", "f_conv_w", "f_conv_b", "f_down", "final_norm"]
    res = {nm: (big[nm] if nm in big else small[nm]) for nm in order}
    outs = [loss_v[0, 0], grad_x.reshape(x.shape)]
    for k in range(4):
        outs += [res[nm][k] for nm in order]
    return tuple(outs)


def _unpack_rows(rows, shapes):
    out, off = [], 0
    for shp in shapes:
        n = math.prod(shp)
        out.append(rows[:, off:off + n].reshape((N_DEV,) + tuple(shp)))
        off += n + (-n) % PACK_GRANULE
    return out
```

```python
import functools
import math

import jax
import jax.numpy as jnp
from jax import lax
from jax.experimental import pallas as pl
from jax.experimental.pallas import tpu as pltpu

F32 = jnp.float32
BF16 = jnp.bfloat16
MESH = pl.DeviceIdType.MESH

N_DEV = 8
RMS_EPS = 1e-5
CHUNK = 128
SG_GROUPS = 8
ADAM_LR = 0.001
ADAM_B1 = 0.9
ADAM_B2 = 0.999
ADAM_EPS = 1e-08
ADAM_WD = 0.01
ADAM_STEP = 10

LANES = 128
SLAB = 16
VMEM_LIMIT = 48 * 1024 * 1024
PACK_GRANULE = 8 * LANES


def _pick(dim, cap, mult):
    best = None
    t = mult
    while t <= min(dim, cap):
        if dim % t == 0:
            best = t
        t += mult
    return dim if best is None else best


def _params(semantics=None):
    return pltpu.CompilerParams(dimension_semantics=semantics, vmem_limit_bytes=VMEM_LIMIT)


_DIMS = {
    "nn": (((1,), (0,)), ((), ())),
    "nt": (((1,), (1,)), ((), ())),
    "tn": (((0,), (0,)), ((), ())),
}


def _matmul(a, b, mode, out_dtype, name, resid=None, after=None, tm_cap=1024, tn_cap=1024, tk_cap=2816):
    if mode == "nn":
        (m, k), n = a.shape, b.shape[1]
    elif mode == "nt":
        (m, k), n = a.shape, b.shape[0]
    else:
        (k, m), n = a.shape, b.shape[1]
    tm, tn, tk = _pick(m, tm_cap, LANES), _pick(n, tn_cap, LANES), _pick(k, tk_cap, LANES)
    nk = k // tk
    n_in = 2 + (resid is not None) + (after is not None)

    def body(*refs):
        a_ref, b_ref = refs[:2]
        r_ref = refs[2] if resid is not None else None
        o_ref = refs[n_in]
        prod = lax.dot_general(a_ref[...], b_ref[...], _DIMS[mode], preferred_element_type=F32)

        def finish(r):
            if r_ref is not None:
                r = r + r_ref[...]
            o_ref[...] = r.astype(out_dtype)

        if nk == 1:
            finish(prod)
            return
        acc_ref = refs[n_in + 1]
        kk = pl.program_id(2)

        @pl.when(kk == 0)
        def _():
            acc_ref[...] = prod

        @pl.when(jnp.logical_and(kk > 0, kk < nk - 1))
        def _():
            acc_ref[...] += prod

        @pl.when(kk == nk - 1)
        def _():
            finish(acc_ref[...] + prod)

    a_spec = (pl.BlockSpec((tk, tm), lambda i, j, kk: (kk, i)) if mode == "tn"
              else pl.BlockSpec((tm, tk), lambda i, j, kk: (i, kk)))
    b_spec = (pl.BlockSpec((tn, tk), lambda i, j, kk: (j, kk)) if mode == "nt"
              else pl.BlockSpec((tk, tn), lambda i, j, kk: (kk, j)))
    o_spec = pl.BlockSpec((tm, tn), lambda i, j, kk: (i, j))
    in_specs = [a_spec, b_spec] + ([o_spec] if resid is not None else [])
    args = (a, b) + ((resid,) if resid is not None else ())
    if after is not None:
        in_specs.append(pl.BlockSpec(memory_space=pl.ANY))
        args += (after,)
    return pl.pallas_call(
        body, name=name, grid=(m // tm, n // tn, nk),
        in_specs=in_specs, out_specs=o_spec,
        out_shape=jax.ShapeDtypeStruct((m, n), out_dtype),
        scratch_shapes=[pltpu.VMEM((tm, tn), F32)] if nk > 1 else [],
        compiler_params=_params(("parallel", "parallel", "arbitrary")),
    )(*args)


def _rms_stats(xf):
    inv = lax.rsqrt(jnp.mean(xf * xf, axis=-1, keepdims=True) + RMS_EPS)
    return inv, xf * inv


def _rmsnorm_fwd(x, g, name):
    s, d = x.shape
    tm = _pick(s, 256, SLAB)

    def body(x_ref, g_ref, h_ref):
        _, xhat = _rms_stats(x_ref[...])
        h_ref[...] = (xhat * g_ref[...]).astype(BF16)

    return pl.pallas_call(
        body, name=name, grid=(s // tm,),
        in_specs=[pl.BlockSpec((tm, d), lambda i: (i, 0)), pl.BlockSpec((1, d), lambda i: (0, 0))],
        out_specs=pl.BlockSpec((tm, d), lambda i: (i, 0)),
        out_shape=jax.ShapeDtypeStruct((s, d), BF16),
        compiler_params=_params(("parallel",)),
    )(x, g)


def _rmsnorm_bwd(x, g, dh, dx_out, name):
    s, d = x.shape
    tm = _pick(s, 256, SLAB)

    def body(x_ref, g_ref, dh_ref, dxo_ref, dxi_ref, dxib_ref, dg_ref):
        inv, xhat = _rms_stats(x_ref[...])
        dhv = dh_ref[...]
        dxhat = dhv * g_ref[...]
        proj = jnp.mean(dxhat * xhat, axis=-1, keepdims=True)
        dx = dxo_ref[...] + inv * (dxhat - xhat * proj)
        dxi_ref[...] = dx
        dxib_ref[...] = dx.astype(BF16)
        part = jnp.sum(dhv * xhat, axis=0, keepdims=True)

        @pl.when(pl.program_id(0) == 0)
        def _():
            dg_ref[...] = part

        @pl.when(pl.program_id(0) > 0)
        def _():
            dg_ref[...] += part

    row = pl.BlockSpec((tm, d), lambda i: (i, 0))
    vec = pl.BlockSpec((1, d), lambda i: (0, 0))
    return pl.pallas_call(
        body, name=name, grid=(s // tm,),
        in_specs=[row, vec, row, row], out_specs=[row, row, vec],
        out_shape=[jax.ShapeDtypeStruct((s, d), F32), jax.ShapeDtypeStruct((s, d), BF16),
                   jax.ShapeDtypeStruct((1, d), F32)],
        compiler_params=_params(("arbitrary",)),
    )(x, g, dh, dx_out)


def _final_loss(x, g, target, name):
    s, d = x.shape
    tm = _pick(s, 256, SLAB)

    def body(x_ref, g_ref, t_ref, dx_ref, dxb_ref, loss_ref, dg_ref):
        inv, xhat = _rms_stats(x_ref[...])
        gv = g_ref[...]
        err = xhat * gv - t_ref[...]
        loss = 0.5 * jnp.sum(jnp.mean(err * err, axis=-1, keepdims=True), axis=0, keepdims=True)
        dy = err * (1.0 / d)
        dxhat = dy * gv
        proj = jnp.mean(dxhat * xhat, axis=-1, keepdims=True)
        dx = inv * (dxhat - xhat * proj)
        dx_ref[...] = dx
        dxb_ref[...] = dx.astype(BF16)
        part = jnp.sum(dy * xhat, axis=0, keepdims=True)
        loss_row = jnp.broadcast_to(loss, (1, LANES))

        @pl.when(pl.program_id(0) == 0)
        def _():
            dg_ref[...] = part
            loss_ref[...] = loss_row

        @pl.when(pl.program_id(0) > 0)
        def _():
            dg_ref[...] += part
            loss_ref[...] += loss_row

    row = pl.BlockSpec((tm, d), lambda i: (i, 0))
    vec = pl.BlockSpec((1, d), lambda i: (0, 0))
    return pl.pallas_call(
        body, name=name, grid=(s // tm,),
        in_specs=[row, vec, row],
        out_specs=[row, row, pl.BlockSpec((1, LANES), lambda i: (0, 0)), vec],
        out_shape=[jax.ShapeDtypeStruct((s, d), F32), jax.ShapeDtypeStruct((s, d), BF16),
                   jax.ShapeDtypeStruct((1, LANES), F32), jax.ShapeDtypeStruct((1, d), F32)],
        compiler_params=_params(("arbitrary",)),
    )(x, g, target)


def _shift_down(prev, cur, k):
    ext = jnp.concatenate([prev, cur], axis=0)
    return pltpu.roll(ext, k, 0)[SLAB:, :]


def _shift_up(cur, nxt, k):
    ext = jnp.concatenate([cur, nxt], axis=0)
    return pltpu.roll(ext, 2 * SLAB - k, 0)[:SLAB, :]


def _conv3(w_ref, cols, prev, cur):
    s1 = _shift_down(prev, cur, 1)
    s2 = _shift_down(prev, cur, 2)
    y = w_ref[0:1, cols] * s2 + w_ref[1:2, cols] * s1 + w_ref[2:3, cols] * cur
    return y, s1, s2


def _conv3_t(w_ref, cols, cur, nxt):
    return (w_ref[2:3, cols] * cur + w_ref[1:2, cols] * _shift_up(cur, nxt, 1)
            + w_ref[0:1, cols] * _shift_up(cur, nxt, 2))


def _rows(s):
    return pl.ds(pl.multiple_of(s * SLAB, SLAB), SLAB)


def _halo_specs(tm, width, n_tiles):
    per = tm // SLAB
    prev = pl.BlockSpec((SLAB, width), lambda i: (jnp.maximum(i * per - 1, 0), 0))
    nxt = pl.BlockSpec((SLAB, width), lambda i: (jnp.minimum((i + 1) * per, n_tiles * per - 1), 0))
    return prev, nxt


def _add_rows(acc_ref, out_ref, row, cols, first):
    part = jnp.sum(acc_ref[...], axis=0, keepdims=True)

    @pl.when(first)
    def _():
        out_ref[row:row + 1, cols] = part

    @pl.when(jnp.logical_not(first))
    def _():
        out_ref[row:row + 1, cols] += part


def _mixa_fwd(bcx, wc, name):
    s, d3 = bcx.shape
    d = d3 // 3
    tm = _pick(s, 256, SLAB)
    w = _pick(d, 512, LANES)
    nslab = tm // SLAB

    def body(t_ref, prev_ref, wc_ref, y_ref):
        first_tile = pl.program_id(0) == 0
        for c in range(d // w):
            cb, cc, cx = (slice(g * d + c * w, g * d + (c + 1) * w) for g in range(3))
            cols = slice(c * w, (c + 1) * w)
            p_halo = prev_ref[:, cc].astype(F32) * prev_ref[:, cx].astype(F32)
            p_halo = jnp.where(first_tile, 0.0, p_halo)

            def slab(si, p_prev):
                r = _rows(si)
                p = t_ref[r, cc].astype(F32) * t_ref[r, cx].astype(F32)
                cv, _, _ = _conv3(wc_ref, cols, p_prev, p)
                y_ref[r, cols] = (t_ref[r, cb].astype(F32) * cv).astype(BF16)
                return p

            lax.fori_loop(0, nslab, slab, p_halo)

    prev_spec, _ = _halo_specs(tm, d3, s // tm)
    return pl.pallas_call(
        body, name=name, grid=(s // tm,),
        in_specs=[pl.BlockSpec((tm, d3), lambda i: (i, 0)), prev_spec, pl.BlockSpec((3, d), lambda i: (0, 0))],
        out_specs=pl.BlockSpec((tm, d), lambda i: (i, 0)),
        out_shape=jax.ShapeDtypeStruct((s, d), BF16),
        compiler_params=_params(("parallel",)),
    )(bcx, bcx, wc)


def _mixa_bwd(bcx, dy, wc, name):
    s, d3 = bcx.shape
    d = d3 // 3
    tm = _pick(s, 256, SLAB)
    w = _pick(d, 256, LANES)
    nslab = tm // SLAB
    n_tiles = s // tm

    def body(t_ref, prev_ref, next_ref, dy_ref, dyn_ref, wc_ref, o_ref, dwc_ref, a0, a1, a2):
        i = pl.program_id(0)
        first_tile = i == 0
        last_tile = i == n_tiles - 1
        for c in range(d // w):
            cb, cc, cx = (slice(g * d + c * w, g * d + (c + 1) * w) for g in range(3))
            cols = slice(c * w, (c + 1) * w)
            for acc in (a0, a1, a2):
                acc[...] = jnp.zeros_like(acc)
            dcv_next = jnp.where(last_tile, 0.0, dyn_ref[:, cols].astype(F32) * next_ref[:, cb].astype(F32))
            p_halo = jnp.where(first_tile, 0.0, prev_ref[:, cc].astype(F32) * prev_ref[:, cx].astype(F32))

            def one(r, p_prev, dcv_nxt):
                gb = t_ref[r, cb].astype(F32)
                gc = t_ref[r, cc].astype(F32)
                xs = t_ref[r, cx].astype(F32)
                dyv = dy_ref[r, cols].astype(F32)
                p = gc * xs
                cv, s1, s2 = _conv3(wc_ref, cols, p_prev, p)
                dcv = dyv * gb
                a2[...] += dcv * p
                a1[...] += dcv * s1
                a0[...] += dcv * s2
                dp = _conv3_t(wc_ref, cols, dcv, dcv_nxt)
                o_ref[r, cb] = (dyv * cv).astype(BF16)
                o_ref[r, cc] = (dp * xs).astype(BF16)
                o_ref[r, cx] = (dp * gc).astype(BF16)
                return dcv

            def slab(j, dcv_nxt):
                si = nslab - 1 - j
                rp = _rows(si - 1)
                p_prev = t_ref[rp, cc].astype(F32) * t_ref[rp, cx].astype(F32)
                return one(_rows(si), p_prev, dcv_nxt)

            dcv_nxt = lax.fori_loop(0, nslab - 1, slab, dcv_next)
            one(pl.ds(0, SLAB), p_halo, dcv_nxt)
            for k, acc in enumerate((a0, a1, a2)):
                _add_rows(acc, dwc_ref, k, cols, first_tile)

    prev_spec, next_spec = _halo_specs(tm, d3, n_tiles)
    _, next_dy = _halo_specs(tm, d, n_tiles)
    return pl.pallas_call(
        body, name=name, grid=(n_tiles,),
        in_specs=[pl.BlockSpec((tm, d3), lambda i: (i, 0)), prev_spec, next_spec,
                  pl.BlockSpec((tm, d), lambda i: (i, 0)), next_dy, pl.BlockSpec((3, d), lambda i: (0, 0))],
        out_specs=[pl.BlockSpec((tm, d3), lambda i: (i, 0)), pl.BlockSpec((3, d), lambda i: (0, 0))],
        out_shape=[jax.ShapeDtypeStruct((s, d3), BF16), jax.ShapeDtypeStruct((3, d), F32)],
        scratch_shapes=[pltpu.VMEM((SLAB, w), F32)] * 3,
        compiler_params=_params(("arbitrary",)),
    )(bcx, bcx, bcx, dy, dy, wc)


def _sigmoid(z):
    return 1.0 / (1.0 + jnp.exp(-z))


def _ffn_fwd(up, cw, cb, name):
    s, f2 = up.shape
    f = f2 // 2
    tm = _pick(s, 256, SLAB)
    w = _pick(f, 512, LANES)
    nslab = tm // SLAB

    def body(t_ref, prev_ref, cw_ref, cb_ref, act_ref):
        first_tile = pl.program_id(0) == 0
        for c in range(f // w):
            cg = slice(c * w, (c + 1) * w)
            ca = slice(f + c * w, f + (c + 1) * w)
            halo = tuple(jnp.where(first_tile, 0.0, prev_ref[:, cs].astype(F32)) for cs in (cg, ca))

            def slab(si, carry):
                r = _rows(si)
                g = t_ref[r, cg].astype(F32)
                a = t_ref[r, ca].astype(F32)
                gcv = _conv3(cw_ref, cg, carry[0], g)[0] + cb_ref[:, cg]
                acv = _conv3(cw_ref, ca, carry[1], a)[0] + cb_ref[:, ca]
                act_ref[r, cg] = (gcv * _sigmoid(gcv) * acv).astype(BF16)
                return g, a

            lax.fori_loop(0, nslab, slab, halo)

    prev_spec, _ = _halo_specs(tm, f2, s // tm)
    return pl.pallas_call(
        body, name=name, grid=(s // tm,),
        in_specs=[pl.BlockSpec((tm, f2), lambda i: (i, 0)), prev_spec,
                  pl.BlockSpec((3, f2), lambda i: (0, 0)), pl.BlockSpec((1, f2), lambda i: (0, 0))],
        out_specs=pl.BlockSpec((tm, f), lambda i: (i, 0)),
        out_shape=jax.ShapeDtypeStruct((s, f), BF16),
        compiler_params=_params(("parallel",)),
    )(up, up, cw, cb)


def _ffn_bwd(up, dact, cw, cb, name):
    s, f2 = up.shape
    f = f2 // 2
    tm = _pick(s, 128, SLAB)
    w = _pick(f, 256, LANES)
    nslab = tm // SLAB
    n_tiles = s // tm

    def body(t_ref, prev_ref, next_ref, da_ref, dan_ref, cw_ref, cb_ref, o_ref, dcw_ref, dcb_ref, *accs):
        i = pl.program_id(0)
        first_tile = i == 0
        last_tile = i == n_tiles - 1
        last_rows = pl.ds((nslab - 1) * SLAB, SLAB)
        for c in range(f // w):
            cg = slice(c * w, (c + 1) * w)
            ca = slice(f + c * w, f + (c + 1) * w)
            for acc in accs:
                acc[...] = jnp.zeros_like(acc)

            def grads(dav, g_prev, a_prev, g, a):
                gcv, g1, g2 = _conv3(cw_ref, cg, g_prev, g)
                acv, a1, a2 = _conv3(cw_ref, ca, a_prev, a)
                gcv = gcv + cb_ref[:, cg]
                acv = acv + cb_ref[:, ca]
                sg = _sigmoid(gcv)
                d_a = dav * (gcv * sg)
                d_g = dav * acv * (sg * (1.0 + gcv * (1.0 - sg)))
                return d_g, d_a, (g1, g2, a1, a2)

            nxt = grads(dan_ref[:, cg].astype(F32), t_ref[last_rows, cg].astype(F32),
                        t_ref[last_rows, ca].astype(F32), next_ref[:, cg].astype(F32),
                        next_ref[:, ca].astype(F32))[:2]
            nxt = tuple(jnp.where(last_tile, 0.0, v) for v in nxt)
            halo = tuple(jnp.where(first_tile, 0.0, prev_ref[:, cs].astype(F32)) for cs in (cg, ca))

            def one(r, g_prev, a_prev, carry):
                g = t_ref[r, cg].astype(F32)
                a = t_ref[r, ca].astype(F32)
                d_g, d_a, (g1, g2, a1, a2) = grads(da_ref[r, cg].astype(F32), g_prev, a_prev, g, a)
                for acc, term in zip(accs, (d_g * g2, d_g * g1, d_g * g, d_g, d_a * a2, d_a * a1, d_a * a, d_a)):
                    acc[...] += term
                o_ref[r, cg] = _conv3_t(cw_ref, cg, d_g, carry[0]).astype(BF16)
                o_ref[r, ca] = _conv3_t(cw_ref, ca, d_a, carry[1]).astype(BF16)
                return d_g, d_a

            def slab(j, carry):
                si = nslab - 1 - j
                rp = _rows(si - 1)
                return one(_rows(si), t_ref[rp, cg].astype(F32), t_ref[rp, ca].astype(F32), carry)

            carry = lax.fori_loop(0, nslab - 1, slab, nxt)
            one(pl.ds(0, SLAB), halo[0], halo[1], carry)
            for half, cs in enumerate((cg, ca)):
                for k in range(3):
                    _add_rows(accs[4 * half + k], dcw_ref, k, cs, first_tile)
                _add_rows(accs[4 * half + 3], dcb_ref, 0, cs, first_tile)

    prev_spec, next_spec = _halo_specs(tm, f2, n_tiles)
    _, next_da = _halo_specs(tm, f, n_tiles)
    return pl.pallas_call(
        body, name=name, grid=(n_tiles,),
        in_specs=[pl.BlockSpec((tm, f2), lambda i: (i, 0)), prev_spec, next_spec,
                  pl.BlockSpec((tm, f), lambda i: (i, 0)), next_da,
                  pl.BlockSpec((3, f2), lambda i: (0, 0)), pl.BlockSpec((1, f2), lambda i: (0, 0))],
        out_specs=[pl.BlockSpec((tm, f2), lambda i: (i, 0)), pl.BlockSpec((3, f2), lambda i: (0, 0)),
                   pl.BlockSpec((1, f2), lambda i: (0, 0))],
        out_shape=[jax.ShapeDtypeStruct((s, f2), BF16), jax.ShapeDtypeStruct((3, f2), F32),
                   jax.ShapeDtypeStruct((1, f2), F32)],
        scratch_shapes=[pltpu.VMEM((SLAB, w), F32)] * 8,
        compiler_params=_params(("arbitrary",)),
    )(up, up, up, dact, dact, cw, cb)


_GELU_C = math.sqrt(2.0 / math.pi)


def _gelu(x):
    th = jnp.tanh(_GELU_C * (x + 0.044715 * (x * x * x)))
    return x * (0.5 * (1.0 + th)), th


def _gelu_grad(x, th):
    return 0.5 * (1.0 + th) + 0.5 * x * (1.0 - th * th) * (_GELU_C * (1.0 + 3.0 * 0.044715 * (x * x)))


def _masked_ws(ws_ref, h):
    t = lax.broadcasted_iota(jnp.int32, (CHUNK, CHUNK), 0)
    sx = lax.broadcasted_iota(jnp.int32, (CHUNK, CHUNK), 1)
    return jnp.where(sx <= t, ws_ref[h], 0.0)


def _mixb_fwd(pre, gv, ws, bs_wide, name):
    s, w2 = pre.shape
    w = w2 // 2
    gw = w // SG_GROUPS

    def body(pre_ref, gv_ref, ws_ref, bs_ref, o_ref):
        zu, _ = _gelu(pre_ref[:, :w].astype(F32))
        zv, _ = _gelu(pre_ref[:, w:].astype(F32))
        _, vhat = _rms_stats(zv)
        vn = (vhat * gv_ref[...]).astype(BF16)
        for h in range(SG_GROUPS):
            cols = slice(h * gw, (h + 1) * gw)
            wsm = _masked_ws(ws_ref, h).astype(BF16)
            gate = jnp.dot(wsm, vn[:, cols], preferred_element_type=F32)
            gate = gate + jnp.tile(bs_ref[h], (1, gw // LANES))
            o_ref[:, cols] = (zu[:, cols] * gate).astype(BF16)

    return pl.pallas_call(
        body, name=name, grid=(s // CHUNK,),
        in_specs=[pl.BlockSpec((CHUNK, w2), lambda i: (i, 0)), pl.BlockSpec((1, w), lambda i: (0, 0)),
                  pl.BlockSpec((SG_GROUPS, CHUNK, CHUNK), lambda i: (0, 0, 0)),
                  pl.BlockSpec((SG_GROUPS, CHUNK, LANES), lambda i: (0, 0, 0))],
        out_specs=pl.BlockSpec((CHUNK, w), lambda i: (i, 0)),
        out_shape=jax.ShapeDtypeStruct((s, w), BF16),
        compiler_params=_params(("parallel",)),
    )(pre, gv, ws, bs_wide)


def _mixb_bwd(pre, dug, gv, ws, bs_wide, name):
    s, w2 = pre.shape
    w = w2 // 2
    gw = w // SG_GROUPS

    def body(pre_ref, dug_ref, gv_ref, ws_ref, bs_ref, o_ref, dws_ref, dbs_ref, dgv_ref, dvn_ref):
        first = pl.program_id(0) == 0

        @pl.when(first)
        def _():
            dws_ref[...] = jnp.zeros_like(dws_ref)
            dbs_ref[...] = jnp.zeros_like(dbs_ref)

        pu = pre_ref[:, :w].astype(F32)
        pv = pre_ref[:, w:].astype(F32)
        zu, thu = _gelu(pu)
        zv, thv = _gelu(pv)
        inv, vhat = _rms_stats(zv)
        gvv = gv_ref[...]
        vn = (vhat * gvv).astype(BF16)
        for h in range(SG_GROUPS):
            cols = slice(h * gw, (h + 1) * gw)
            wsm = _masked_ws(ws_ref, h).astype(BF16)
            gate = jnp.dot(wsm, vn[:, cols], preferred_element_type=F32)
            gate = gate + jnp.tile(bs_ref[h], (1, gw // LANES))
            dug_h = dug_ref[:, cols].astype(F32)
            dgate = dug_h * zu[:, cols]
            dgate_b = dgate.astype(BF16)
            o_ref[:, cols] = (dug_h * gate * _gelu_grad(pu[:, cols], thu[:, cols])).astype(BF16)
            dbs_ref[h] += jnp.broadcast_to(jnp.sum(dgate, axis=-1, keepdims=True), (CHUNK, LANES))
            dws = lax.dot_general(dgate_b, vn[:, cols], _DIMS["nt"], preferred_element_type=F32)
            t = lax.broadcasted_iota(jnp.int32, (CHUNK, CHUNK), 0)
            sx = lax.broadcasted_iota(jnp.int32, (CHUNK, CHUNK), 1)
            dws_ref[h] += jnp.where(sx <= t, dws, 0.0)
            dvn_ref[:, cols] = lax.dot_general(wsm, dgate_b, _DIMS["tn"], preferred_element_type=F32)
        dvn = dvn_ref[...]
        part = jnp.sum(dvn * vhat, axis=0, keepdims=True)

        @pl.when(first)
        def _():
            dgv_ref[...] = part

        @pl.when(jnp.logical_not(first))
        def _():
            dgv_ref[...] += part

        dvhat = dvn * gvv
        dzv = inv * (dvhat - vhat * jnp.mean(dvhat * vhat, axis=-1, keepdims=True))
        o_ref[:, w:] = (dzv * _gelu_grad(pv, thv)).astype(BF16)

    return pl.pallas_call(
        body, name=name, grid=(s // CHUNK,),
        in_specs=[pl.BlockSpec((CHUNK, w2), lambda i: (i, 0)), pl.BlockSpec((CHUNK, w), lambda i: (i, 0)),
                  pl.BlockSpec((1, w), lambda i: (0, 0)),
                  pl.BlockSpec((SG_GROUPS, CHUNK, CHUNK), lambda i: (0, 0, 0)),
                  pl.BlockSpec((SG_GROUPS, CHUNK, LANES), lambda i: (0, 0, 0))],
        out_specs=[pl.BlockSpec((CHUNK, w2), lambda i: (i, 0)),
                   pl.BlockSpec((SG_GROUPS, CHUNK, CHUNK), lambda i: (0, 0, 0)),
                   pl.BlockSpec((SG_GROUPS, CHUNK, LANES), lambda i: (0, 0, 0)),
                   pl.BlockSpec((1, w), lambda i: (0, 0))],
        out_shape=[jax.ShapeDtypeStruct((s, w2), BF16), jax.ShapeDtypeStruct((SG_GROUPS, CHUNK, CHUNK), F32),
                   jax.ShapeDtypeStruct((SG_GROUPS, CHUNK, LANES), F32), jax.ShapeDtypeStruct((1, w), F32)],
        scratch_shapes=[pltpu.VMEM((CHUNK, w), F32)],
        compiler_params=_params(("arbitrary",)),
    )(pre, dug, gv, ws, bs_wide)


def _cast_layer(w3, layer, name):
    _, r, c = w3.shape
    tr = _pick(r, 256, SLAB)

    def body(w_ref, o_ref):
        o_ref[...] = w_ref[...].astype(BF16)

    return pl.pallas_call(
        body, name=name, grid=(r // tr,),
        in_specs=[pl.BlockSpec((None, tr, c), lambda i: (layer, i, 0))],
        out_specs=pl.BlockSpec((tr, c), lambda i: (i, 0)),
        out_shape=jax.ShapeDtypeStruct((r, c), BF16),
        compiler_params=_params(("parallel",)),
    )(w3)


def _adamw_math(w, g, m, v):
    m = ADAM_B1 * m + (1.0 - ADAM_B1) * g
    v = ADAM_B2 * v + (1.0 - ADAM_B2) * (g * g)
    m_hat = m / (1.0 - ADAM_B1 ** ADAM_STEP)
    v_hat = v / (1.0 - ADAM_B2 ** ADAM_STEP)
    delta = -ADAM_LR * (m_hat / (jnp.sqrt(v_hat) + ADAM_EPS) + ADAM_WD * w)
    return delta, m, v


def _adamw_sharded(recvs, w, m, v, name):
    nl, r, c = w.shape
    tc = _pick(c, 1536, LANES)
    tr = _pick(r, 64, SLAB)

    def body(*refs):
        recv_refs = refs[:nl]
        w_ref, m_ref, v_ref, g_ref, d_ref, nm_ref, nv_ref = refs[nl:]
        for layer, recv_ref in enumerate(recv_refs):
            @pl.when(pl.program_id(0) == layer)
            def _():
                g = recv_ref[0].astype(F32)
                for q in range(1, N_DEV):
                    g = g + recv_ref[q].astype(F32)
                delta, nm, nv = _adamw_math(w_ref[...], g, m_ref[...], v_ref[...])
                g_ref[...] = g
                d_ref[...] = delta
                nm_ref[...] = nm
                nv_ref[...] = nv

    def recv_spec(layer):
        return pl.BlockSpec((N_DEV, tr, tc),
                            lambda l, i, j: (0, jnp.where(l == layer, i, 0), jnp.where(l == layer, j, 0)))

    blk = pl.BlockSpec((None, tr, tc), lambda l, i, j: (l, i, j))
    out = jax.ShapeDtypeStruct((nl, r, c), F32)
    return pl.pallas_call(
        body, name=name, grid=(nl, r // tr, c // tc),
        in_specs=[recv_spec(layer) for layer in range(nl)] + [blk, blk, blk],
        out_specs=[blk] * 4, out_shape=[out] * 4,
        compiler_params=_params(("parallel",) * 3),
    )(*recvs, w, m, v)


def _adamw_packed(w, g, m, v, name):
    r, c = w.shape
    tr = _pick(r, 256, 8)

    def body(w_ref, g_ref, m_ref, v_ref, d_ref, nm_ref, nv_ref):
        delta, nm, nv = _adamw_math(w_ref[...], g_ref[...], m_ref[...], v_ref[...])
        d_ref[...] = delta
        nm_ref[...] = nm
        nv_ref[...] = nv

    blk = pl.BlockSpec((tr, c), lambda i: (i, 0))
    out = jax.ShapeDtypeStruct((r, c), F32)
    return pl.pallas_call(
        body, name=name, grid=(r // tr,), in_specs=[blk] * 4, out_specs=[blk] * 3, out_shape=[out] * 3,
        compiler_params=_params(("parallel",)),
    )(w, g, m, v)


def _pack(arrays):
    parts = []
    for a in arrays:
        flat = a.reshape(-1).astype(F32)
        pad = (-flat.shape[0]) % PACK_GRANULE
        parts.append(jnp.pad(flat, (0, pad)) if pad else flat)
    return jnp.concatenate(parts).reshape(-1, LANES)


def _unpack(buf, shapes):
    flat = buf.reshape(-1)
    out, off = [], 0
    for shp in shapes:
        n = math.prod(shp)
        out.append(flat[off:off + n].reshape(shp))
        off += n + (-n) % PACK_GRANULE
    return out


def _mesh_pos():
    return lax.axis_index("x"), lax.axis_index("y"), lax.axis_index("c")


def _coords(q):
    return q // 4, (q // 2) % 2, q % 2


def _shard_of(ref, q, shard_shape, axis):
    r, c = shard_shape
    if axis == 0:
        return ref.at[pl.ds(pl.multiple_of(q * r, SLAB), r), :]
    return ref.at[:, pl.ds(pl.multiple_of(q * c, LANES), c)]


def _all_gather(shards, axes, name):
    n = len(shards)

    def body(*refs):
        srcs, dsts = refs[:n], refs[n:2 * n]
        send_sems, recv_sems, local_sems = refs[2 * n:]
        x, y, c = _mesh_pos()
        sibling = (x, y, 1 - c)
        chips = [(1 - x, y), (x, 1 - y), (1 - x, 1 - y)]

        def block(t, dev):
            px, py, pc = dev
            return _shard_of(dsts[t], 4 * px + 2 * py + pc, shards[t].shape, axes[t])

        def copy(t, k, owner, to, src=None):
            return pltpu.make_async_remote_copy(
                src_ref=block(t, owner) if src is None else src, dst_ref=block(t, owner),
                send_sem=send_sems.at[t, k], recv_sem=recv_sems.at[t, k], device_id=to, device_id_type=MESH)

        me = (x, y, c)
        mine = [pltpu.make_async_copy(srcs[t], block(t, me), local_sems.at[t]) for t in range(n)]
        first = []
        for t in range(n):
            mine[t].start()
            for j, chip in enumerate(chips):
                first.append(copy(t, 1 + j, me, (*chip, c), src=srcs[t]))
            first.append(copy(t, 0, me, sibling, src=srcs[t]))
        for cp in first:
            cp.start()
        passed = []
        for t in range(n):
            for j, chip in enumerate(chips):
                copy(t, 1 + j, (*chip, c), me).wait_recv()
                fwd = copy(t, 4 + j, (*chip, c), sibling)
                fwd.start()
                passed.append(fwd)
        for t in range(n):
            copy(t, 0, sibling, me).wait_recv()
            for j, chip in enumerate(chips):
                copy(t, 4 + j, (*chip, 1 - c), me).wait_recv()
        for cp in first + passed:
            cp.wait_send()
        for cp in mine:
            cp.wait()

    def full_shape(a, axis):
        r, c = a.shape
        return (r * N_DEV, c) if axis == 0 else (r, c * N_DEV)

    any_spec = pl.BlockSpec(memory_space=pl.ANY)
    return pl.pallas_call(
        body, name=name,
        in_specs=[any_spec] * n, out_specs=[any_spec] * n,
        out_shape=[jax.ShapeDtypeStruct(full_shape(a, ax), a.dtype) for a, ax in zip(shards, axes)],
        scratch_shapes=[pltpu.SemaphoreType.DMA((n, 7)), pltpu.SemaphoreType.DMA((n, 7)),
                        pltpu.SemaphoreType.DMA((n,))],
        compiler_params=pltpu.CompilerParams(has_side_effects=True),
    )(*shards)


_HBM = pl.BlockSpec(memory_space=pltpu.HBM)
_SEM = pl.BlockSpec(memory_space=pltpu.SEMAPHORE)
_EFFECT = pltpu.SideEffectType.DATAFLOW_SIDE_EFFECTING


def _rs_copies(src, land, sems, axis):
    send_sems, recv_sems, own_sem = sems
    x, y, c_ = _mesh_pos()
    me = 4 * x + 2 * y + c_
    r, c = src.shape
    shard = (r // N_DEV, c) if axis == 0 else (r, c // N_DEV)
    own = pltpu.make_async_copy(_shard_of(src, me, shard, axis), land.at[me], own_sem.at[0])
    sends, arrivals = [], []
    for step in range(1, N_DEV):
        to = (me + step) % N_DEV
        frm = (me + N_DEV - step) % N_DEV
        sends.append(pltpu.make_async_remote_copy(
            src_ref=_shard_of(src, to, shard, axis), dst_ref=land.at[me],
            send_sem=send_sems.at[step - 1], recv_sem=recv_sems.at[step - 1],
            device_id=_coords(to), device_id_type=MESH))
        arrivals.append(pltpu.make_async_remote_copy(
            src_ref=_shard_of(src, me, shard, axis), dst_ref=land.at[frm],
            send_sem=send_sems.at[step - 1], recv_sem=recv_sems.at[step - 1],
            device_id=_coords(frm), device_id_type=MESH))
    return own, sends, arrivals


def _rs_start(grad, axis, name):
    r, c = grad.shape
    land_shape = (N_DEV,) + ((r // N_DEV, c) if axis == 0 else (r, c // N_DEV))

    def body(src, land, send_sems, recv_sems, own_sem, src_thru, land_thru, token):
        own, sends, _ = _rs_copies(src, land, (send_sems, recv_sems, own_sem), axis)
        own.start()
        for cp in sends:
            cp.start()
        token[...] = jnp.zeros_like(token)

    out = pl.pallas_call(
        body, name=name,
        out_shape=(pltpu.SemaphoreType.DMA((N_DEV - 1,)), pltpu.SemaphoreType.DMA((N_DEV - 1,)),
                   pltpu.SemaphoreType.DMA((1,)), pltpu.HBM(grad.shape, grad.dtype),
                   pltpu.HBM(land_shape, grad.dtype), jax.ShapeDtypeStruct((8, LANES), F32)),
        in_specs=[_HBM, _HBM],
        out_specs=(_SEM, _SEM, _SEM, _HBM, _HBM, pl.BlockSpec(memory_space=pltpu.VMEM)),
        input_output_aliases={0: 3, 1: 4},
        compiler_params=pltpu.CompilerParams(has_side_effects=_EFFECT),
    )(pltpu.with_memory_space_constraint(grad, pltpu.HBM),
      pltpu.with_memory_space_constraint(lax.empty(land_shape, grad.dtype), pltpu.HBM))
    return out[:5], out[5]


def _rs_wait(state, axis, after, name):
    send_sems, recv_sems, own_sem, src_thru, land_thru = state

    def body(src, land, send_sems, recv_sems, own_sem, after_ref, src_dead, got):
        own, sends, arrivals = _rs_copies(src, land, (send_sems, recv_sems, own_sem), axis)
        for cp in sends:
            cp.wait_send()
        for cp in arrivals:
            cp.wait_recv()
        own.wait()

    return pl.pallas_call(
        body, name=name,
        out_shape=(pltpu.HBM(src_thru.shape, src_thru.dtype), pltpu.HBM(land_thru.shape, land_thru.dtype)),
        in_specs=[_HBM, _HBM, _SEM, _SEM, _SEM, pl.BlockSpec(memory_space=pl.ANY)],
        out_specs=(_HBM, _HBM),
        input_output_aliases={0: 0, 1: 1},
        compiler_params=pltpu.CompilerParams(has_side_effects=_EFFECT),
    )(src_thru, land_thru, send_sems, recv_sems, own_sem, after)[1]


def _all_reduce_small(part, name):
    r, c = part.shape

    def body(part_ref, out_ref, slots, send_sems, recv_sems, local_sem):
        x, y, cc = _mesh_pos()
        me = 4 * x + 2 * y + cc
        own = pltpu.make_async_copy(part_ref, slots.at[me], local_sem)
        own.start()
        sends = []
        for step in range(1, N_DEV):
            to = (me + step) % N_DEV
            cp = pltpu.make_async_remote_copy(
                src_ref=part_ref, dst_ref=slots.at[me], send_sem=send_sems.at[step - 1],
                recv_sem=recv_sems.at[step - 1], device_id=_coords(to), device_id_type=MESH)
            cp.start()
            sends.append(cp)
        for step in range(1, N_DEV):
            frm = (me + N_DEV - step) % N_DEV
            pltpu.make_async_remote_copy(
                src_ref=part_ref, dst_ref=slots.at[frm], send_sem=send_sems.at[step - 1],
                recv_sem=recv_sems.at[step - 1], device_id=_coords(frm), device_id_type=MESH).wait_recv()
        for cp in sends:
            cp.wait_send()
        own.wait()
        total = slots[0]
        for q in range(1, N_DEV):
            total = total + slots[q]
        out_ref[...] = total

    vmem = pl.BlockSpec(memory_space=pltpu.VMEM)
    return pl.pallas_call(
        body, name=name, in_specs=[vmem], out_specs=vmem,
        out_shape=jax.ShapeDtypeStruct((r, c), F32),
        scratch_shapes=[pltpu.VMEM((N_DEV, r, c), F32), pltpu.SemaphoreType.DMA((N_DEV - 1,)),
                        pltpu.SemaphoreType.DMA((N_DEV - 1,)), pltpu.SemaphoreType.DMA],
        compiler_params=pltpu.CompilerParams(has_side_effects=True, vmem_limit_bytes=VMEM_LIMIT),
    )(part)


def _gather_cols(full_rows, n_rows, shard_cols):
    return full_rows.reshape(N_DEV, n_rows, shard_cols).transpose(1, 0, 2).reshape(n_rows, N_DEV * shard_cols)


def kernel(x, a_norm, a_in, a_conv, a_out, b_norm, b_in, b_vnorm, b_ws, b_bs, b_out, f_norm, f_up, f_conv_w, f_conv_b, f_down, final_norm, loss_target, m_a_norm, m_a_in, m_a_conv, m_a_out, m_b_norm, m_b_in, m_b_vnorm, m_b_ws, m_b_bs, m_b_out, m_f_norm, m_f_up, m_f_conv_w, m_f_conv_b, m_f_down, m_final_norm, v_a_norm, v_a_in, v_a_conv, v_a_out, v_b_norm, v_b_in, v_b_vnorm, v_b_ws, v_b_bs, v_b_out, v_f_norm, v_f_up, v_f_conv_w, v_f_conv_b, v_f_down, v_final_norm):
    s, d = x.shape[1], x.shape[2]
    n_ffn = f_up.shape[0]
    f2 = f_up.shape[2] * N_DEV
    me = 4 * lax.axis_index("x") + 2 * lax.axis_index("y") + lax.axis_index("c")
    x0 = x.reshape(s, d)
    target = loss_target.reshape(s, d)

    small_shard = _pack([a_conv, b_norm, b_vnorm, f_conv_w])
    shards = [_cast_layer(a_in, 0, "cast_a_in"), _cast_layer(a_out, 0, "cast_a_out")]
    axes = [1, 0]
    for l in range(n_ffn):
        shards += [_cast_layer(f_up, l, f"cast_f_up{l}"), _cast_layer(f_down, l, f"cast_f_down{l}")]
        axes += [1, 0]
    shards += [_cast_layer(b_in, 0, "cast_b_in"), _cast_layer(b_out, 0, "cast_b_out"), small_shard]
    axes += [1, 0, 0]
    gathered = _all_gather(shards, axes, "all_gather_weights")
    w_a_in, w_a_out = gathered[0], gathered[1]
    w_f_up = [gathered[2 + 2 * l] for l in range(n_ffn)]
    w_f_down = [gathered[3 + 2 * l] for l in range(n_ffn)]
    w_b_in, w_b_out, small_full = gathered[2 + 2 * n_ffn:]
    cshard = a_conv.shape[2]
    fshard = f_conv_w.shape[2]
    small_rows = small_full.reshape(N_DEV, -1)
    per_dev = _unpack_rows(small_rows, [(3, cshard), (cshard,), (cshard,), (n_ffn, 3, fshard)])
    a_conv_full = per_dev[0].transpose(1, 0, 2).reshape(3, d)
    b_norm_full = per_dev[1].reshape(1, d)
    b_vnorm_full = per_dev[2].reshape(1, d)
    f_conv_w_full = per_dev[3].transpose(1, 2, 0, 3).reshape(n_ffn, 3, f2)
    bs_wide = jnp.broadcast_to(b_bs[0][:, :, None], (SG_GROUPS, CHUNK, LANES))
    ws = b_ws[0]

    def ffn_forward(xin, l):
        h = _rmsnorm_fwd(xin, f_norm[l:l + 1], f"ffn{l}_norm")
        up = _matmul(h, w_f_up[l], "nn", BF16, f"ffn{l}_up")
        act = _ffn_fwd(up, f_conv_w_full[l], f_conv_b[l:l + 1], f"ffn{l}_mid")
        xout = _matmul(act, w_f_down[l], "nn", F32, f"ffn{l}_down", resid=xin, tm_cap=512)
        return xout, (h, up, act)

    h0 = _rmsnorm_fwd(x0, a_norm, "mixa_norm")
    bcx = _matmul(h0, w_a_in, "nn", BF16, "mixa_in")
    ya = _mixa_fwd(bcx, a_conv_full, "mixa_mid")
    x1 = _matmul(ya, w_a_out, "nn", F32, "mixa_out", resid=x0)
    x2, saved0 = ffn_forward(x1, 0)
    h2 = _rmsnorm_fwd(x2, b_norm_full, "mixb_norm")
    pre = _matmul(h2, w_b_in, "nn", BF16, "mixb_in")
    ug = _mixb_fwd(pre, b_vnorm_full, ws, bs_wide, "mixb_mid")
    x3 = _matmul(ug, w_b_out, "nn", F32, "mixb_out", resid=x2)
    x4, saved1 = ffn_forward(x3, 1)
    dx4, dx4b, loss_part, g_final = _final_loss(x4, final_norm.reshape(1, d), target, "loss_head")

    def ffn_backward(xin, l, saved, dx, dxb):
        h, up, act = saved
        g_down = _matmul(act, dxb, "tn", BF16, f"ffn{l}_down_dw", tm_cap=1408)
        rs_down, tok = _rs_start(g_down, 0, f"rs_start_f_down{l}")
        dact = _matmul(dxb, w_f_down[l], "nt", BF16, f"ffn{l}_down_dx", after=tok, tn_cap=1408)
        dup, g_cw, g_cb = _ffn_bwd(up, dact, f_conv_w_full[l], f_conv_b[l:l + 1], f"ffn{l}_mid_bwd")
        g_up = _matmul(h, dup, "tn", BF16, f"ffn{l}_up_dw")
        rs_up, tok = _rs_start(g_up, 1, f"rs_start_f_up{l}")
        dh = _matmul(dup, w_f_up[l], "nt", F32, f"ffn{l}_up_dx", after=tok)
        dxin, dxinb, g_norm = _rmsnorm_bwd(xin, f_norm[l:l + 1], dh, dx, f"ffn{l}_norm_bwd")
        return dxin, dxinb, (rs_up, rs_down, g_cw, g_cb, g_norm)

    dx3, dx3b, gf1 = ffn_backward(x3, 1, saved1, dx4, dx4b)
    g_b_out = _matmul(ug, dx3b, "tn", BF16, "mixb_out_dw")
    rs_b_out, tok = _rs_start(g_b_out, 0, "rs_start_b_out")
    dug = _matmul(dx3b, w_b_out, "nt", BF16, "mixb_out_dx", after=tok)
    dpre, g_ws, g_bs_wide, g_bvnorm = _mixb_bwd(pre, dug, b_vnorm_full, ws, bs_wide, "mixb_mid_bwd")
    g_b_in = _matmul(h2, dpre, "tn", BF16, "mixb_in_dw")
    rs_b_in, tok = _rs_start(g_b_in, 1, "rs_start_b_in")
    dh2 = _matmul(dpre, w_b_in, "nt", F32, "mixb_in_dx", after=tok)
    dx2, dx2b, g_bnorm = _rmsnorm_bwd(x2, b_norm_full, dh2, dx3, "mixb_norm_bwd")
    dx1, dx1b, gf0 = ffn_backward(x1, 0, saved0, dx2, dx2b)
    g_a_out = _matmul(ya, dx1b, "tn", BF16, "mixa_out_dw")
    rs_a_out, tok = _rs_start(g_a_out, 0, "rs_start_a_out")
    dya = _matmul(dx1b, w_a_out, "nt", BF16, "mixa_out_dx", after=tok)
    dbcx, g_aconv = _mixa_bwd(bcx, dya, a_conv_full, "mixa_mid_bwd")
    g_a_in = _matmul(h0, dbcx, "tn", BF16, "mixa_in_dw")
    rs_a_in, tok = _rs_start(g_a_in, 1, "rs_start_a_in")
    dh0 = _matmul(dbcx, w_a_in, "nt", F32, "mixa_in_dx", after=tok)
    grad_x, _, g_anorm = _rmsnorm_bwd(x0, a_norm, dh0, dx1, "mixa_norm_bwd")

    big = {}
    for name, states, axis, w, m, v in (
            ("f_down", (gf0[1], gf1[1]), 0, f_down, m_f_down, v_f_down),
            ("f_up", (gf0[0], gf1[0]), 1, f_up, m_f_up, v_f_up),
            ("b_out", (rs_b_out,), 0, b_out, m_b_out, v_b_out), ("b_in", (rs_b_in,), 1, b_in, m_b_in, v_b_in),
            ("a_out", (rs_a_out,), 0, a_out, m_a_out, v_a_out), ("a_in", (rs_a_in,), 1, a_in, m_a_in, v_a_in)):
        recvs = [_rs_wait(st, axis, grad_x, f"rs_wait_{name}{l}") for l, st in enumerate(states)]
        big[name] = _adamw_sharded(recvs, w, m, v, f"adamw_{name}")

    full_shapes = [(1, LANES), (1, d), (3, d), (1, d), (1, d), (SG_GROUPS, CHUNK, CHUNK), (SG_GROUPS, CHUNK),
                   (n_ffn, d), (n_ffn, 3, f2), (n_ffn, f2), (1, d)]
    parts = [loss_part, g_anorm, g_aconv, g_bnorm, g_bvnorm, g_ws, g_bs_wide[:, :, 0],
             jnp.concatenate([gf0[4], gf1[4]], axis=0), jnp.stack([gf0[2], gf1[2]]),
             jnp.concatenate([gf0[3], gf1[3]], axis=0), g_final]
    total = _all_reduce_small(_pack(parts), "all_reduce_small")
    (loss_v, r_anorm, r_aconv, r_bnorm, r_bvnorm, r_ws, r_bs, r_fnorm, r_fcw, r_fcb, r_final) = _unpack(total, full_shapes)
    small_grads = [
        r_anorm,
        lax.dynamic_slice_in_dim(r_aconv, me * cshard, cshard, axis=1).reshape(a_conv.shape),
        lax.dynamic_slice_in_dim(r_bnorm, me * cshard, cshard, axis=1),
        lax.dynamic_slice_in_dim(r_bvnorm, me * cshard, cshard, axis=1),
        r_ws.reshape(b_ws.shape), r_bs.reshape(b_bs.shape), r_fnorm,
        lax.dynamic_slice_in_dim(r_fcw, me * fshard, fshard, axis=2),
        r_fcb, r_final.reshape(final_norm.shape)]
    small_w = [a_norm, a_conv, b_norm, b_vnorm, b_ws, b_bs, f_norm, f_conv_w, f_conv_b, final_norm]
    small_m = [m_a_norm, m_a_conv, m_b_norm, m_b_vnorm, m_b_ws, m_b_bs, m_f_norm, m_f_conv_w, m_f_conv_b, m_final_norm]
    small_v = [v_a_norm, v_a_conv, v_b_norm, v_b_vnorm, v_b_ws, v_b_bs, v_f_norm, v_f_conv_w, v_f_conv_b, v_final_norm]
    shapes = [w.shape for w in small_w]
    packed = _adamw_packed(_pack(small_w), _pack(small_grads), _pack(small_m), _pack(small_v), "adamw_small")
    s_delta, s_m, s_v = (_unpack(p, shapes) for p in packed)
    small_names = ["a_norm", "a_conv", "b_norm", "b_vnorm", "b_ws", "b_bs", "f_norm", "f_conv_w", "f_conv_b", "final_norm"]
    small = {nm: (small_grads[i], s_delta[i], s_m[i], s_v[i]) for i, nm in enumerate(small_names)}

    order = ["a_norm", "a_in", "a_conv", "a_out", "b_norm", "b_in", "b_vnorm", "b_ws", "b_bs", "b_out",
             "f_norm", "f_up", "f_conv_w", "f_conv_b", "f_down", "final_norm"]
    res = {nm: (big[nm] if nm in big else small[nm]) for nm in order}
    outs = [loss_v[0, 0], grad_x.reshape(x.shape)]
    for k in range(4):
        outs += [res[nm][k] for nm in order]
    return tuple(outs)


def _unpack_rows(rows, shapes):
    out, off = [], 0
    for shp in shapes:
        n = math.prod(shp)
        out.append(rows[:, off:off + n].reshape((N_DEV,) + tuple(shp)))
        off += n + (-n) % PACK_GRANULE
    return out
```

```python
import functools
import math

import jax
import jax.numpy as jnp
from jax import lax
from jax.experimental import pallas as pl
from jax.experimental.pallas import tpu as pltpu

F32 = jnp.float32
BF16 = jnp.bfloat16
MESH = pl.DeviceIdType.MESH

N_DEV = 8
RMS_EPS = 1e-5
CHUNK = 128
SG_GROUPS = 8
ADAM_LR = 0.001
ADAM_B1 = 0.9
ADAM_B2 = 0.999
ADAM_EPS = 1e-08
ADAM_WD = 0.01
ADAM_STEP = 10

LANES = 128
SLAB = 16
VMEM_LIMIT = 48 * 1024 * 1024
PACK_GRANULE = 8 * LANES


def _pick(dim, cap, mult):
    best = None
    t = mult
    while t <= min(dim, cap):
        if dim % t == 0:
            best = t
        t += mult
    return dim if best is None else best


def _params(semantics=None):
    return pltpu.CompilerParams(dimension_semantics=semantics, vmem_limit_bytes=VMEM_LIMIT)


_DIMS = {
    "nn": (((1,), (0,)), ((), ())),
    "nt": (((1,), (1,)), ((), ())),
    "tn": (((0,), (0,)), ((), ())),
}


def _matmul(a, b, mode, out_dtype, name, resid=None, after=None, tm_cap=1024, tn_cap=1024, tk_cap=2816):
    if mode == "nn":
        (m, k), n = a.shape, b.shape[1]
    elif mode == "nt":
        (m, k), n = a.shape, b.shape[0]
    else:
        (k, m), n = a.shape, b.shape[1]
    tm, tn, tk = _pick(m, tm_cap, LANES), _pick(n, tn_cap, LANES), _pick(k, tk_cap, LANES)
    nk = k // tk
    n_in = 2 + (resid is not None) + (after is not None)

    def body(*refs):
        a_ref, b_ref = refs[:2]
        r_ref = refs[2] if resid is not None else None
        o_ref = refs[n_in]
        prod = lax.dot_general(a_ref[...], b_ref[...], _DIMS[mode], preferred_element_type=F32)

        def finish(r):
            if r_ref is not None:
                r = r + r_ref[...]
            o_ref[...] = r.astype(out_dtype)

        if nk == 1:
            finish(prod)
            return
        acc_ref = refs[n_in + 1]
        kk = pl.program_id(2)

        @pl.when(kk == 0)
        def _():
            acc_ref[...] = prod

        @pl.when(jnp.logical_and(kk > 0, kk < nk - 1))
        def _():
            acc_ref[...] += prod

        @pl.when(kk == nk - 1)
        def _():
            finish(acc_ref[...] + prod)

    a_spec = (pl.BlockSpec((tk, tm), lambda i, j, kk: (kk, i)) if mode == "tn"
              else pl.BlockSpec((tm, tk), lambda i, j, kk: (i, kk)))
    b_spec = (pl.BlockSpec((tn, tk), lambda i, j, kk: (j, kk)) if mode == "nt"
              else pl.BlockSpec((tk, tn), lambda i, j, kk: (kk, j)))
    o_spec = pl.BlockSpec((tm, tn), lambda i, j, kk: (i, j))
    in_specs = [a_spec, b_spec] + ([o_spec] if resid is not None else [])
    args = (a, b) + ((resid,) if resid is not None else ())
    if after is not None:
        in_specs.append(pl.BlockSpec(memory_space=pl.ANY))
        args += (after,)
    return pl.pallas_call(
        body, name=name, grid=(m // tm, n // tn, nk),
        in_specs=in_specs, out_specs=o_spec,
        out_shape=jax.ShapeDtypeStruct((m, n), out_dtype),
        scratch_shapes=[pltpu.VMEM((tm, tn), F32)] if nk > 1 else [],
        compiler_params=_params(("parallel", "parallel", "arbitrary")),
    )(*args)


def _rms_stats(xf):
    inv = lax.rsqrt(jnp.mean(xf * xf, axis=-1, keepdims=True) + RMS_EPS)
    return inv, xf * inv


def _rmsnorm_fwd(x, g, name):
    s, d = x.shape
    tm = _pick(s, 256, SLAB)

    def body(x_ref, g_ref, h_ref):
        _, xhat = _rms_stats(x_ref[...])
        h_ref[...] = (xhat * g_ref[...]).astype(BF16)

    return pl.pallas_call(
        body, name=name, grid=(s // tm,),
        in_specs=[pl.BlockSpec((tm, d), lambda i: (i, 0)), pl.BlockSpec((1, d), lambda i: (0, 0))],
        out_specs=pl.BlockSpec((tm, d), lambda i: (i, 0)),
        out_shape=jax.ShapeDtypeStruct((s, d), BF16),
        compiler_params=_params(("parallel",)),
    )(x, g)


def _rmsnorm_bwd(x, g, dh, dx_out, name):
    s, d = x.shape
    tm = _pick(s, 256, SLAB)

    def body(x_ref, g_ref, dh_ref, dxo_ref, dxi_ref, dxib_ref, dg_ref):
        inv, xhat = _rms_stats(x_ref[...])
        dhv = dh_ref[...]
        dxhat = dhv * g_ref[...]
        proj = jnp.mean(dxhat * xhat, axis=-1, keepdims=True)
        dx = dxo_ref[...] + inv * (dxhat - xhat * proj)
        dxi_ref[...] = dx
        dxib_ref[...] = dx.astype(BF16)
        part = jnp.sum(dhv * xhat, axis=0, keepdims=True)

        @pl.when(pl.program_id(0) == 0)
        def _():
            dg_ref[...] = part

        @pl.when(pl.program_id(0) > 0)
        def _():
            dg_ref[...] += part

    row = pl.BlockSpec((tm, d), lambda i: (i, 0))
    vec = pl.BlockSpec((1, d), lambda i: (0, 0))
    return pl.pallas_call(
        body, name=name, grid=(s // tm,),
        in_specs=[row, vec, row, row], out_specs=[row, row, vec],
        out_shape=[jax.ShapeDtypeStruct((s, d), F32), jax.ShapeDtypeStruct((s, d), BF16),
                   jax.ShapeDtypeStruct((1, d), F32)],
        compiler_params=_params(("arbitrary",)),
    )(x, g, dh, dx_out)


def _final_loss(x, g, target, name):
    s, d = x.shape
    tm = _pick(s, 256, SLAB)

    def body(x_ref, g_ref, t_ref, dx_ref, dxb_ref, loss_ref, dg_ref):
        inv, xhat = _rms_stats(x_ref[...])
        gv = g_ref[...]
        err = xhat * gv - t_ref[...]
        loss = 0.5 * jnp.sum(jnp.mean(err * err, axis=-1, keepdims=True), axis=0, keepdims=True)
        dy = err * (1.0 / d)
        dxhat = dy * gv
        proj = jnp.mean(dxhat * xhat, axis=-1, keepdims=True)
        dx = inv * (dxhat - xhat * proj)
        dx_ref[...] = dx
        dxb_ref[...] = dx.astype(BF16)
        part = jnp.sum(dy * xhat, axis=0, keepdims=True)
        loss_row = jnp.broadcast_to(loss, (1, LANES))

        @pl.when(pl.program_id(0) == 0)
        def _():
            dg_ref[...] = part
            loss_ref[...] = loss_row

        @pl.when(pl.program_id(0) > 0)
        def _():
            dg_ref[...] += part
            loss_ref[...] += loss_row

    row = pl.BlockSpec((tm, d), lambda i: (i, 0))
    vec = pl.BlockSpec((1, d), lambda i: (0, 0))
    return pl.pallas_call(
        body, name=name, grid=(s // tm,),
        in_specs=[row, vec, row],
        out_specs=[row, row, pl.BlockSpec((1, LANES), lambda i: (0, 0)), vec],
        out_shape=[jax.ShapeDtypeStruct((s, d), F32), jax.ShapeDtypeStruct((s, d), BF16),
                   jax.ShapeDtypeStruct((1, LANES), F32), jax.ShapeDtypeStruct((1, d), F32)],
        compiler_params=_params(("arbitrary",)),
    )(x, g, target)


def _shift_down(prev, cur, k):
    ext = jnp.concatenate([prev, cur], axis=0)
    return pltpu.roll(ext, k, 0)[SLAB:, :]


def _shift_up(cur, nxt, k):
    ext = jnp.concatenate([cur, nxt], axis=0)
    return pltpu.roll(ext, 2 * SLAB - k, 0)[:SLAB, :]


def _conv3(w_ref, cols, prev, cur):
    s1 = _shift_down(prev, cur, 1)
    s2 = _shift_down(prev, cur, 2)
    y = w_ref[0:1, cols] * s2 + w_ref[1:2, cols] * s1 + w_ref[2:3, cols] * cur
    return y, s1, s2


def _conv3_t(w_ref, cols, cur, nxt):
    return (w_ref[2:3, cols] * cur + w_ref[1:2, cols] * _shift_up(cur, nxt, 1)
            + w_ref[0:1, cols] * _shift_up(cur, nxt, 2))


def _rows(s):
    return pl.ds(pl.multiple_of(s * SLAB, SLAB), SLAB)


def _halo_specs(tm, width, n_tiles):
    per = tm // SLAB
    prev = pl.BlockSpec((SLAB, width), lambda i: (jnp.maximum(i * per - 1, 0), 0))
    nxt = pl.BlockSpec((SLAB, width), lambda i: (jnp.minimum((i + 1) * per, n_tiles * per - 1), 0))
    return prev, nxt


def _add_rows(acc_ref, out_ref, row, cols, first):
    part = jnp.sum(acc_ref[...], axis=0, keepdims=True)

    @pl.when(first)
    def _():
        out_ref[row:row + 1, cols] = part

    @pl.when(jnp.logical_not(first))
    def _():
        out_ref[row:row + 1, cols] += part


def _mixa_fwd(bcx, wc, name):
    s, d3 = bcx.shape
    d = d3 // 3
    tm = _pick(s, 256, SLAB)
    w = _pick(d, 512, LANES)
    nslab = tm // SLAB

    def body(t_ref, prev_ref, wc_ref, y_ref):
        first_tile = pl.program_id(0) == 0
        for c in range(d // w):
            cb, cc, cx = (slice(g * d + c * w, g * d + (c + 1) * w) for g in range(3))
            cols = slice(c * w, (c + 1) * w)
            p_halo = prev_ref[:, cc].astype(F32) * prev_ref[:, cx].astype(F32)
            p_halo = jnp.where(first_tile, 0.0, p_halo)

            def slab(si, p_prev):
                r = _rows(si)
                p = t_ref[r, cc].astype(F32) * t_ref[r, cx].astype(F32)
                cv, _, _ = _conv3(wc_ref, cols, p_prev, p)
                y_ref[r, cols] = (t_ref[r, cb].astype(F32) * cv).astype(BF16)
                return p

            lax.fori_loop(0, nslab, slab, p_halo)

    prev_spec, _ = _halo_specs(tm, d3, s // tm)
    return pl.pallas_call(
        body, name=name, grid=(s // tm,),
        in_specs=[pl.BlockSpec((tm, d3), lambda i: (i, 0)), prev_spec, pl.BlockSpec((3, d), lambda i: (0, 0))],
        out_specs=pl.BlockSpec((tm, d), lambda i: (i, 0)),
        out_shape=jax.ShapeDtypeStruct((s, d), BF16),
        compiler_params=_params(("parallel",)),
    )(bcx, bcx, wc)


def _mixa_bwd(bcx, dy, wc, name):
    s, d3 = bcx.shape
    d = d3 // 3
    tm = _pick(s, 256, SLAB)
    w = _pick(d, 256, LANES)
    nslab = tm // SLAB
    n_tiles = s // tm

    def body(t_ref, prev_ref, next_ref, dy_ref, dyn_ref, wc_ref, o_ref, dwc_ref, a0, a1, a2):
        i = pl.program_id(0)
        first_tile = i == 0
        last_tile = i == n_tiles - 1
        for c in range(d // w):
            cb, cc, cx = (slice(g * d + c * w, g * d + (c + 1) * w) for g in range(3))
            cols = slice(c * w, (c + 1) * w)
            for acc in (a0, a1, a2):
                acc[...] = jnp.zeros_like(acc)
            dcv_next = jnp.where(last_tile, 0.0, dyn_ref[:, cols].astype(F32) * next_ref[:, cb].astype(F32))
            p_halo = jnp.where(first_tile, 0.0, prev_ref[:, cc].astype(F32) * prev_ref[:, cx].astype(F32))

            def one(r, p_prev, dcv_nxt):
                gb = t_ref[r, cb].astype(F32)
                gc = t_ref[r, cc].astype(F32)
                xs = t_ref[r, cx].astype(F32)
                dyv = dy_ref[r, cols].astype(F32)
                p = gc * xs
                cv, s1, s2 = _conv3(wc_ref, cols, p_prev, p)
                dcv = dyv * gb
                a2[...] += dcv * p
                a1[...] += dcv * s1
                a0[...] += dcv * s2
                dp = _conv3_t(wc_ref, cols, dcv, dcv_nxt)
                o_ref[r, cb] = (dyv * cv).astype(BF16)
                o_ref[r, cc] = (dp * xs).astype(BF16)
                o_ref[r, cx] = (dp * gc).astype(BF16)
                return dcv

            def slab(j, dcv_nxt):
                si = nslab - 1 - j
                rp = _rows(si - 1)
                p_prev = t_ref[rp, cc].astype(F32) * t_ref[rp, cx].astype(F32)
                return one(_rows(si), p_prev, dcv_nxt)

            dcv_nxt = lax.fori_loop(0, nslab - 1, slab, dcv_next)
            one(pl.ds(0, SLAB), p_halo, dcv_nxt)
            for k, acc in enumerate((a0, a1, a2)):
                _add_rows(acc, dwc_ref, k, cols, first_tile)

    prev_spec, next_spec = _halo_specs(tm, d3, n_tiles)
    _, next_dy = _halo_specs(tm, d, n_tiles)
    return pl.pallas_call(
        body, name=name, grid=(n_tiles,),
        in_specs=[pl.BlockSpec((tm, d3), lambda i: (i, 0)), prev_spec, next_spec,
                  pl.BlockSpec((tm, d), lambda i: (i, 0)), next_dy, pl.BlockSpec((3, d), lambda i: (0, 0))],
        out_specs=[pl.BlockSpec((tm, d3), lambda i: (i, 0)), pl.BlockSpec((3, d), lambda i: (0, 0))],
        out_shape=[jax.ShapeDtypeStruct((s, d3), BF16), jax.ShapeDtypeStruct((3, d), F32)],
        scratch_shapes=[pltpu.VMEM((SLAB, w), F32)] * 3,
        compiler_params=_params(("arbitrary",)),
    )(bcx, bcx, bcx, dy, dy, wc)


def _sigmoid(z):
    return 1.0 / (1.0 + jnp.exp(-z))


def _ffn_fwd(up, cw, cb, name):
    s, f2 = up.shape
    f = f2 // 2
    tm = _pick(s, 256, SLAB)
    w = _pick(f, 512, LANES)
    nslab = tm // SLAB

    def body(t_ref, prev_ref, cw_ref, cb_ref, act_ref):
        first_tile = pl.program_id(0) == 0
        for c in range(f // w):
            cg = slice(c * w, (c + 1) * w)
            ca = slice(f + c * w, f + (c + 1) * w)
            halo = tuple(jnp.where(first_tile, 0.0, prev_ref[:, cs].astype(F32)) for cs in (cg, ca))

            def slab(si, carry):
                r = _rows(si)
                g = t_ref[r, cg].astype(F32)
                a = t_ref[r, ca].astype(F32)
                gcv = _conv3(cw_ref, cg, carry[0], g)[0] + cb_ref[:, cg]
                acv = _conv3(cw_ref, ca, carry[1], a)[0] + cb_ref[:, ca]
                act_ref[r, cg] = (gcv * _sigmoid(gcv) * acv).astype(BF16)
                return g, a

            lax.fori_loop(0, nslab, slab, halo)

    prev_spec, _ = _halo_specs(tm, f2, s // tm)
    return pl.pallas_call(
        body, name=name, grid=(s // tm,),
        in_specs=[pl.BlockSpec((tm, f2), lambda i: (i, 0)), prev_spec,
                  pl.BlockSpec((3, f2), lambda i: (0, 0)), pl.BlockSpec((1, f2), lambda i: (0, 0))],
        out_specs=pl.BlockSpec((tm, f), lambda i: (i, 0)),
        out_shape=jax.ShapeDtypeStruct((s, f), BF16),
        compiler_params=_params(("parallel",)),
    )(up, up, cw, cb)


def _ffn_bwd(up, dact, cw, cb, name):
    s, f2 = up.shape
    f = f2 // 2
    tm = _pick(s, 128, SLAB)
    w = _pick(f, 256, LANES)
    nslab = tm // SLAB
    n_tiles = s // tm

    def body(t_ref, prev_ref, next_ref, da_ref, dan_ref, cw_ref, cb_ref, o_ref, dcw_ref, dcb_ref, *accs):
        i = pl.program_id(0)
        first_tile = i == 0
        last_tile = i == n_tiles - 1
        last_rows = pl.ds((nslab - 1) * SLAB, SLAB)
        for c in range(f // w):
            cg = slice(c * w, (c + 1) * w)
            ca = slice(f + c * w, f + (c + 1) * w)
            for acc in accs:
                acc[...] = jnp.zeros_like(acc)

            def grads(dav, g_prev, a_prev, g, a):
                gcv, g1, g2 = _conv3(cw_ref, cg, g_prev, g)
                acv, a1, a2 = _conv3(cw_ref, ca, a_prev, a)
                gcv = gcv + cb_ref[:, cg]
                acv = acv + cb_ref[:, ca]
                sg = _sigmoid(gcv)
                d_a = dav * (gcv * sg)
                d_g = dav * acv * (sg * (1.0 + gcv * (1.0 - sg)))
                return d_g, d_a, (g1, g2, a1, a2)

            nxt = grads(dan_ref[:, cg].astype(F32), t_ref[last_rows, cg].astype(F32),
                        t_ref[last_rows, ca].astype(F32), next_ref[:, cg].astype(F32),
                        next_ref[:, ca].astype(F32))[:2]
            nxt = tuple(jnp.where(last_tile, 0.0, v) for v in nxt)
            halo = tuple(jnp.where(first_tile, 0.0, prev_ref[:, cs].astype(F32)) for cs in (cg, ca))

            def one(r, g_prev, a_prev, carry):
                g = t_ref[r, cg].astype(F32)
                a = t_ref[r, ca].astype(F32)
                d_g, d_a, (g1, g2, a1, a2) = grads(da_ref[r, cg].astype(F32), g_prev, a_prev, g, a)
                for acc, term in zip(accs, (d_g * g2, d_g * g1, d_g * g, d_g, d_a * a2, d_a * a1, d_a * a, d_a)):
                    acc[...] += term
                o_ref[r, cg] = _conv3_t(cw_ref, cg, d_g, carry[0]).astype(BF16)
                o_ref[r, ca] = _conv3_t(cw_ref, ca, d_a, carry[1]).astype(BF16)
                return d_g, d_a

            def slab(j, carry):
                si = nslab - 1 - j
                rp = _rows(si - 1)
                return one(_rows(si), t_ref[rp, cg].astype(F32), t_ref[rp, ca].astype(F32), carry)

            carry = lax.fori_loop(0, nslab - 1, slab, nxt)
            one(pl.ds(0, SLAB), halo[0], halo[1], carry)
            for half, cs in enumerate((cg, ca)):
                for k in range(3):
                    _add_rows(accs[4 * half + k], dcw_ref, k, cs, first_tile)
                _add_rows(accs[4 * half + 3], dcb_ref, 0, cs, first_tile)

    prev_spec, next_spec = _halo_specs(tm, f2, n_tiles)
    _, next_da = _halo_specs(tm, f, n_tiles)
    return pl.pallas_call(
        body, name=name, grid=(n_tiles,),
        in_specs=[pl.BlockSpec((tm, f2), lambda i: (i, 0)), prev_spec, next_spec,
                  pl.BlockSpec((tm, f), lambda i: (i, 0)), next_da,
                  pl.BlockSpec((3, f2), lambda i: (0, 0)), pl.BlockSpec((1, f2), lambda i: (0, 0))],
        out_specs=[pl.BlockSpec((tm, f2), lambda i: (i, 0)), pl.BlockSpec((3, f2), lambda i: (0, 0)),
                   pl.BlockSpec((1, f2), lambda i: (0, 0))],
        out_shape=[jax.ShapeDtypeStruct((s, f2), BF16), jax.ShapeDtypeStruct((3, f2), F32),
                   jax.ShapeDtypeStruct((1, f2), F32)],
        scratch_shapes=[pltpu.VMEM((SLAB, w), F32)] * 8,
        compiler_params=_params(("arbitrary",)),
    )(up, up, up, dact, dact, cw, cb)


_GELU_C = math.sqrt(2.0 / math.pi)


def _gelu(x):
    th = jnp.tanh(_GELU_C * (x + 0.044715 * (x * x * x)))
    return x * (0.5 * (1.0 + th)), th


def _gelu_grad(x, th):
    return 0.5 * (1.0 + th) + 0.5 * x * (1.0 - th * th) * (_GELU_C * (1.0 + 3.0 * 0.044715 * (x * x)))


def _masked_ws(ws_ref, h):
    t = lax.broadcasted_iota(jnp.int32, (CHUNK, CHUNK), 0)
    sx = lax.broadcasted_iota(jnp.int32, (CHUNK, CHUNK), 1)
    return jnp.where(sx <= t, ws_ref[h], 0.0)


def _mixb_fwd(pre, gv, ws, bs_wide, name):
    s, w2 = pre.shape
    w = w2 // 2
    gw = w // SG_GROUPS

    def body(pre_ref, gv_ref, ws_ref, bs_ref, o_ref):
        zu, _ = _gelu(pre_ref[:, :w].astype(F32))
        zv, _ = _gelu(pre_ref[:, w:].astype(F32))
        _, vhat = _rms_stats(zv)
        vn = (vhat * gv_ref[...]).astype(BF16)
        for h in range(SG_GROUPS):
            cols = slice(h * gw, (h + 1) * gw)
            wsm = _masked_ws(ws_ref, h).astype(BF16)
            gate = jnp.dot(wsm, vn[:, cols], preferred_element_type=F32)
            gate = gate + jnp.tile(bs_ref[h], (1, gw // LANES))
            o_ref[:, cols] = (zu[:, cols] * gate).astype(BF16)

    return pl.pallas_call(
        body, name=name, grid=(s // CHUNK,),
        in_specs=[pl.BlockSpec((CHUNK, w2), lambda i: (i, 0)), pl.BlockSpec((1, w), lambda i: (0, 0)),
                  pl.BlockSpec((SG_GROUPS, CHUNK, CHUNK), lambda i: (0, 0, 0)),
                  pl.BlockSpec((SG_GROUPS, CHUNK, LANES), lambda i: (0, 0, 0))],
        out_specs=pl.BlockSpec((CHUNK, w), lambda i: (i, 0)),
        out_shape=jax.ShapeDtypeStruct((s, w), BF16),
        compiler_params=_params(("parallel",)),
    )(pre, gv, ws, bs_wide)


def _mixb_bwd(pre, dug, gv, ws, bs_wide, name):
    s, w2 = pre.shape
    w = w2 // 2
    gw = w // SG_GROUPS

    def body(pre_ref, dug_ref, gv_ref, ws_ref, bs_ref, o_ref, dws_ref, dbs_ref, dgv_ref, dvn_ref):
        first = pl.program_id(0) == 0

        @pl.when(first)
        def _():
            dws_ref[...] = jnp.zeros_like(dws_ref)
            dbs_ref[...] = jnp.zeros_like(dbs_ref)

        pu = pre_ref[:, :w].astype(F32)
        pv = pre_ref[:, w:].astype(F32)
        zu, thu = _gelu(pu)
        zv, thv = _gelu(pv)
        inv, vhat = _rms_stats(zv)
        gvv = gv_ref[...]
        vn = (vhat * gvv).astype(BF16)
        for h in range(SG_GROUPS):
            cols = slice(h * gw, (h + 1) * gw)
            wsm = _masked_ws(ws_ref, h).astype(BF16)
            gate = jnp.dot(wsm, vn[:, cols], preferred_element_type=F32)
            gate = gate + jnp.tile(bs_ref[h], (1, gw // LANES))
            dug_h = dug_ref[:, cols].astype(F32)
            dgate = dug_h * zu[:, cols]
            dgate_b = dgate.astype(BF16)
            o_ref[:, cols] = (dug_h * gate * _gelu_grad(pu[:, cols], thu[:, cols])).astype(BF16)
            dbs_ref[h] += jnp.broadcast_to(jnp.sum(dgate, axis=-1, keepdims=True), (CHUNK, LANES))
            dws = lax.dot_general(dgate_b, vn[:, cols], _DIMS["nt"], preferred_element_type=F32)
            t = lax.broadcasted_iota(jnp.int32, (CHUNK, CHUNK), 0)
            sx = lax.broadcasted_iota(jnp.int32, (CHUNK, CHUNK), 1)
            dws_ref[h] += jnp.where(sx <= t, dws, 0.0)
            dvn_ref[:, cols] = lax.dot_general(wsm, dgate_b, _DIMS["tn"], preferred_element_type=F32)
        dvn = dvn_ref[...]
        part = jnp.sum(dvn * vhat, axis=0, keepdims=True)

        @pl.when(first)
        def _():
            dgv_ref[...] = part

        @pl.when(jnp.logical_not(first))
        def _():
            dgv_ref[...] += part

        dvhat = dvn * gvv
        dzv = inv * (dvhat - vhat * jnp.mean(dvhat * vhat, axis=-1, keepdims=True))
        o_ref[:, w:] = (dzv * _gelu_grad(pv, thv)).astype(BF16)

    return pl.pallas_call(
        body, name=name, grid=(s // CHUNK,),
        in_specs=[pl.BlockSpec((CHUNK, w2), lambda i: (i, 0)), pl.BlockSpec((CHUNK, w), lambda i: (i, 0)),
                  pl.BlockSpec((1, w), lambda i: (0, 0)),
                  pl.BlockSpec((SG_GROUPS, CHUNK, CHUNK), lambda i: (0, 0, 0)),
                  pl.BlockSpec((SG_GROUPS, CHUNK, LANES), lambda i: (0, 0, 0))],
        out_specs=[pl.BlockSpec((CHUNK, w2), lambda i: (i, 0)),
                   pl.BlockSpec((SG_GROUPS, CHUNK, CHUNK), lambda i: (0, 0, 0)),
                   pl.BlockSpec((SG_GROUPS, CHUNK, LANES), lambda i: (0, 0, 0)),
                   pl.BlockSpec((1, w), lambda i: (0, 0))],
        out_shape=[jax.ShapeDtypeStruct((s, w2), BF16), jax.ShapeDtypeStruct((SG_GROUPS, CHUNK, CHUNK), F32),
                   jax.ShapeDtypeStruct((SG_GROUPS, CHUNK, LANES), F32), jax.ShapeDtypeStruct((1, w), F32)],
        scratch_shapes=[pltpu.VMEM((CHUNK, w), F32)],
        compiler_params=_params(("arbitrary",)),
    )(pre, dug, gv, ws, bs_wide)


def _cast_layer(w3, layer, name):
    _, r, c = w3.shape
    tr = _pick(r, 256, SLAB)

    def body(w_ref, o_ref):
        o_ref[...] = w_ref[...].astype(BF16)

    return pl.pallas_call(
        body, name=name, grid=(r // tr,),
        in_specs=[pl.BlockSpec((None, tr, c), lambda i: (layer, i, 0))],
        out_specs=pl.BlockSpec((tr, c), lambda i: (i, 0)),
        out_shape=jax.ShapeDtypeStruct((r, c), BF16),
        compiler_params=_params(("parallel",)),
    )(w3)


def _adamw_math(w, g, m, v):
    m = ADAM_B1 * m + (1.0 - ADAM_B1) * g
    v = ADAM_B2 * v + (1.0 - ADAM_B2) * (g * g)
    m_hat = m / (1.0 - ADAM_B1 ** ADAM_STEP)
    v_hat = v / (1.0 - ADAM_B2 ** ADAM_STEP)
    delta = -ADAM_LR * (m_hat / (jnp.sqrt(v_hat) + ADAM_EPS) + ADAM_WD * w)
    return delta, m, v


def _adamw_sharded(recvs, w, m, v, name):
    nl, r, c = w.shape
    tc = _pick(c, 1536, LANES)
    tr = _pick(r, 64, SLAB)

    def body(*refs):
        recv_refs = refs[:nl]
        w_ref, m_ref, v_ref, g_ref, d_ref, nm_ref, nv_ref = refs[nl:]
        for layer, recv_ref in enumerate(recv_refs):
            @pl.when(pl.program_id(0) == layer)
            def _():
                g = recv_ref[0].astype(F32)
                for q in range(1, N_DEV):
                    g = g + recv_ref[q].astype(F32)
                delta, nm, nv = _adamw_math(w_ref[...], g, m_ref[...], v_ref[...])
                g_ref[...] = g
                d_ref[...] = delta
                nm_ref[...] = nm
                nv_ref[...] = nv

    def recv_spec(layer):
        return pl.BlockSpec((N_DEV, tr, tc),
                            lambda l, i, j: (0, jnp.where(l == layer, i, 0), jnp.where(l == layer, j, 0)))

    blk = pl.BlockSpec((None, tr, tc), lambda l, i, j: (l, i, j))
    out = jax.ShapeDtypeStruct((nl, r, c), F32)
    return pl.pallas_call(
        body, name=name, grid=(nl, r // tr, c // tc),
        in_specs=[recv_spec(layer) for layer in range(nl)] + [blk, blk, blk],
        out_specs=[blk] * 4, out_shape=[out] * 4,
        compiler_params=_params(("parallel",) * 3),
    )(*recvs, w, m, v)


def _adamw_packed(w, g, m, v, name):
    r, c = w.shape
    tr = _pick(r, 256, 8)

    def body(w_ref, g_ref, m_ref, v_ref, d_ref, nm_ref, nv_ref):
        delta, nm, nv = _adamw_math(w_ref[...], g_ref[...], m_ref[...], v_ref[...])
        d_ref[...] = delta
        nm_ref[...] = nm
        nv_ref[...] = nv

    blk = pl.BlockSpec((tr, c), lambda i: (i, 0))
    out = jax.ShapeDtypeStruct((r, c), F32)
    return pl.pallas_call(
        body, name=name, grid=(r // tr,), in_specs=[blk] * 4, out_specs=[blk] * 3, out_shape=[out] * 3,
        compiler_params=_params(("parallel",)),
    )(w, g, m, v)


def _pack(arrays):
    parts = []
    for a in arrays:
        flat = a.reshape(-1).astype(F32)
        pad = (-flat.shape[0]) % PACK_GRANULE
        parts.append(jnp.pad(flat, (0, pad)) if pad else flat)
    return jnp.concatenate(parts).reshape(-1, LANES)


def _unpack(buf, shapes):
    flat = buf.reshape(-1)
    out, off = [], 0
    for shp in shapes:
        n = math.prod(shp)
        out.append(flat[off:off + n].reshape(shp))
        off += n + (-n) % PACK_GRANULE
    return out


def _mesh_pos():
    return lax.axis_index("x"), lax.axis_index("y"), lax.axis_index("c")


def _coords(q):
    return q // 4, (q // 2) % 2, q % 2


def _shard_of(ref, q, shard_shape, axis):
    r, c = shard_shape
    if axis == 0:
        return ref.at[pl.ds(pl.multiple_of(q * r, SLAB), r), :]
    return ref.at[:, pl.ds(pl.multiple_of(q * c, LANES), c)]


_HBM = pl.BlockSpec(memory_space=pltpu.HBM)
_SEM = pl.BlockSpec(memory_space=pltpu.SEMAPHORE)
_EFFECT = pltpu.SideEffectType.DATAFLOW_SIDE_EFFECTING


def _exchange_shapes(gather, src_shape, axis):
    r, c = src_shape
    if gather:
        return (r, c), ((r * N_DEV, c) if axis == 0 else (r, c * N_DEV))
    shard = (r // N_DEV, c) if axis == 0 else (r, c // N_DEV)
    return shard, (N_DEV,) + shard


def _exchange_copies(gather, src, land, sems, axis):
    send_sems, recv_sems, own_sem = sems
    x, y, c_ = _mesh_pos()
    me = 4 * x + 2 * y + c_
    shard, _ = _exchange_shapes(gather, src.shape, axis)

    def piece(q):
        return src if gather else _shard_of(src, q, shard, axis)

    def place(q):
        return _shard_of(land, q, shard, axis) if gather else land.at[q]

    own = pltpu.make_async_copy(piece(me), place(me), own_sem.at[0])
    sends, arrivals = [], []
    for step in range(1, N_DEV):
        to = (me + step) % N_DEV
        frm = (me + N_DEV - step) % N_DEV
        sends.append(pltpu.make_async_remote_copy(
            src_ref=piece(to), dst_ref=place(me), send_sem=send_sems.at[step - 1], recv_sem=recv_sems.at[step - 1],
            device_id=_coords(to), device_id_type=MESH))
        arrivals.append(pltpu.make_async_remote_copy(
            src_ref=piece(me), dst_ref=place(frm), send_sem=send_sems.at[step - 1], recv_sem=recv_sems.at[step - 1],
            device_id=_coords(frm), device_id_type=MESH))
    return own, sends, arrivals


def _exchange_start(gather, src, axis, name, after=None):
    _, land_shape = _exchange_shapes(gather, src.shape, axis)
    extra = () if after is None else (after,)

    def body(*refs):
        src_ref, land = refs[:2]
        send_sems, recv_sems, own_sem = refs[2 + len(extra):5 + len(extra)]
        own, sends, _ = _exchange_copies(gather, src_ref, land, (send_sems, recv_sems, own_sem), axis)
        own.start()
        for cp in sends:
            cp.start()
        refs[-1][...] = jnp.zeros_like(refs[-1])

    out = pl.pallas_call(
        body, name=name,
        out_shape=(pltpu.SemaphoreType.DMA((N_DEV - 1,)), pltpu.SemaphoreType.DMA((N_DEV - 1,)),
                   pltpu.SemaphoreType.DMA((1,)), pltpu.HBM(src.shape, src.dtype),
                   pltpu.HBM(land_shape, src.dtype), jax.ShapeDtypeStruct((8, LANES), F32)),
        in_specs=[_HBM, _HBM] + [pl.BlockSpec(memory_space=pl.ANY)] * len(extra),
        out_specs=(_SEM, _SEM, _SEM, _HBM, _HBM, pl.BlockSpec(memory_space=pltpu.VMEM)),
        input_output_aliases={0: 3, 1: 4},
        compiler_params=pltpu.CompilerParams(has_side_effects=_EFFECT),
    )(pltpu.with_memory_space_constraint(src, pltpu.HBM),
      pltpu.with_memory_space_constraint(lax.empty(land_shape, src.dtype), pltpu.HBM), *extra)
    return out[:5], out[5]


def _exchange_wait(gather, state, axis, after, name):
    send_sems, recv_sems, own_sem, src_thru, land_thru = state

    def body(src, land, send_sems, recv_sems, own_sem, after_ref, src_dead, got):
        own, sends, arrivals = _exchange_copies(gather, src, land, (send_sems, recv_sems, own_sem), axis)
        for cp in sends:
            cp.wait_send()
        for cp in arrivals:
            cp.wait_recv()
        own.wait()

    return pl.pallas_call(
        body, name=name,
        out_shape=(pltpu.HBM(src_thru.shape, src_thru.dtype), pltpu.HBM(land_thru.shape, land_thru.dtype)),
        in_specs=[_HBM, _HBM, _SEM, _SEM, _SEM, pl.BlockSpec(memory_space=pl.ANY)],
        out_specs=(_HBM, _HBM),
        input_output_aliases={0: 0, 1: 1},
        compiler_params=pltpu.CompilerParams(has_side_effects=_EFFECT),
    )(src_thru, land_thru, send_sems, recv_sems, own_sem, after)[1]


def _all_reduce_small(part, name):
    r, c = part.shape

    def body(part_ref, out_ref, slots, send_sems, recv_sems, local_sem):
        x, y, cc = _mesh_pos()
        me = 4 * x + 2 * y + cc
        own = pltpu.make_async_copy(part_ref, slots.at[me], local_sem)
        own.start()
        sends = []
        for step in range(1, N_DEV):
            to = (me + step) % N_DEV
            cp = pltpu.make_async_remote_copy(
                src_ref=part_ref, dst_ref=slots.at[me], send_sem=send_sems.at[step - 1],
                recv_sem=recv_sems.at[step - 1], device_id=_coords(to), device_id_type=MESH)
            cp.start()
            sends.append(cp)
        for step in range(1, N_DEV):
            frm = (me + N_DEV - step) % N_DEV
            pltpu.make_async_remote_copy(
                src_ref=part_ref, dst_ref=slots.at[frm], send_sem=send_sems.at[step - 1],
                recv_sem=recv_sems.at[step - 1], device_id=_coords(frm), device_id_type=MESH).wait_recv()
        for cp in sends:
            cp.wait_send()
        own.wait()
        total = slots[0]
        for q in range(1, N_DEV):
            total = total + slots[q]
        out_ref[...] = total

    vmem = pl.BlockSpec(memory_space=pltpu.VMEM)
    return pl.pallas_call(
        body, name=name, in_specs=[vmem], out_specs=vmem,
        out_shape=jax.ShapeDtypeStruct((r, c), F32),
        scratch_shapes=[pltpu.VMEM((N_DEV, r, c), F32), pltpu.SemaphoreType.DMA((N_DEV - 1,)),
                        pltpu.SemaphoreType.DMA((N_DEV - 1,)), pltpu.SemaphoreType.DMA],
        compiler_params=pltpu.CompilerParams(has_side_effects=True, vmem_limit_bytes=VMEM_LIMIT),
    )(part)


def _gather_cols(full_rows, n_rows, shard_cols):
    return full_rows.reshape(N_DEV, n_rows, shard_cols).transpose(1, 0, 2).reshape(n_rows, N_DEV * shard_cols)


def kernel(x, a_norm, a_in, a_conv, a_out, b_norm, b_in, b_vnorm, b_ws, b_bs, b_out, f_norm, f_up, f_conv_w, f_conv_b, f_down, final_norm, loss_target, m_a_norm, m_a_in, m_a_conv, m_a_out, m_b_norm, m_b_in, m_b_vnorm, m_b_ws, m_b_bs, m_b_out, m_f_norm, m_f_up, m_f_conv_w, m_f_conv_b, m_f_down, m_final_norm, v_a_norm, v_a_in, v_a_conv, v_a_out, v_b_norm, v_b_in, v_b_vnorm, v_b_ws, v_b_bs, v_b_out, v_f_norm, v_f_up, v_f_conv_w, v_f_conv_b, v_f_down, v_final_norm):
    s, d = x.shape[1], x.shape[2]
    n_ffn = f_up.shape[0]
    f2 = f_up.shape[2] * N_DEV
    me = 4 * lax.axis_index("x") + 2 * lax.axis_index("y") + lax.axis_index("c")
    x0 = x.reshape(s, d)
    target = loss_target.reshape(s, d)

    wanted = [("a_in", _cast_layer(a_in, 0, "cast_a_in"), 1),
              ("small", _pack([a_conv, b_norm, b_vnorm, f_conv_w]), 0),
              ("a_out", _cast_layer(a_out, 0, "cast_a_out"), 0),
              ("f_up0", _cast_layer(f_up, 0, "cast_f_up0"), 1), ("f_down0", _cast_layer(f_down, 0, "cast_f_down0"), 0),
              ("b_in", _cast_layer(b_in, 0, "cast_b_in"), 1), ("b_out", _cast_layer(b_out, 0, "cast_b_out"), 0),
              ("f_up1", _cast_layer(f_up, 1, "cast_f_up1"), 1), ("f_down1", _cast_layer(f_down, 1, "cast_f_down1"), 0)]
    coming, tok = {}, None
    for key, shard, axis in wanted:
        state, tok = _exchange_start(True, shard, axis, f"ag_start_{key}", after=tok)
        coming[key] = (state, axis)

    def arrived(key, after):
        state, axis = coming[key]
        return _exchange_wait(True, state, axis, after, f"ag_wait_{key}")

    cshard = a_conv.shape[2]
    fshard = f_conv_w.shape[2]
    h0 = _rmsnorm_fwd(x0, a_norm, "mixa_norm")
    w_a_in = arrived("a_in", h0)
    small_rows = arrived("small", w_a_in).reshape(N_DEV, -1)
    per_dev = _unpack_rows(small_rows, [(3, cshard), (cshard,), (cshard,), (n_ffn, 3, fshard)])
    a_conv_full = per_dev[0].transpose(1, 0, 2).reshape(3, d)
    b_norm_full = per_dev[1].reshape(1, d)
    b_vnorm_full = per_dev[2].reshape(1, d)
    f_conv_w_full = per_dev[3].transpose(1, 2, 0, 3).reshape(n_ffn, 3, f2)
    bs_wide = jnp.broadcast_to(b_bs[0][:, :, None], (SG_GROUPS, CHUNK, LANES))
    ws = b_ws[0]

    w_f_up, w_f_down = {}, {}

    def ffn_forward(xin, l):
        h = _rmsnorm_fwd(xin, f_norm[l:l + 1], f"ffn{l}_norm")
        w_f_up[l] = arrived(f"f_up{l}", h)
        up = _matmul(h, w_f_up[l], "nn", BF16, f"ffn{l}_up")
        act = _ffn_fwd(up, f_conv_w_full[l], f_conv_b[l:l + 1], f"ffn{l}_mid")
        w_f_down[l] = arrived(f"f_down{l}", act)
        xout = _matmul(act, w_f_down[l], "nn", F32, f"ffn{l}_down", resid=xin, tm_cap=512)
        return xout, (h, up, act)

    bcx = _matmul(h0, w_a_in, "nn", BF16, "mixa_in")
    ya = _mixa_fwd(bcx, a_conv_full, "mixa_mid")
    w_a_out = arrived("a_out", ya)
    x1 = _matmul(ya, w_a_out, "nn", F32, "mixa_out", resid=x0)
    x2, saved0 = ffn_forward(x1, 0)
    h2 = _rmsnorm_fwd(x2, b_norm_full, "mixb_norm")
    w_b_in = arrived("b_in", h2)
    pre = _matmul(h2, w_b_in, "nn", BF16, "mixb_in")
    ug = _mixb_fwd(pre, b_vnorm_full, ws, bs_wide, "mixb_mid")
    w_b_out = arrived("b_out", ug)
    x3 = _matmul(ug, w_b_out, "nn", F32, "mixb_out", resid=x2)
    x4, saved1 = ffn_forward(x3, 1)
    dx4, dx4b, loss_part, g_final = _final_loss(x4, final_norm.reshape(1, d), target, "loss_head")

    def _rs_start(grad, axis, name):
        return _exchange_start(False, grad, axis, name)

    def ffn_backward(xin, l, saved, dx, dxb):
        h, up, act = saved
        g_down = _matmul(act, dxb, "tn", BF16, f"ffn{l}_down_dw", tm_cap=1408)
        rs_down, tok = _rs_start(g_down, 0, f"rs_start_f_down{l}")
        dact = _matmul(dxb, w_f_down[l], "nt", BF16, f"ffn{l}_down_dx", after=tok, tn_cap=1408)
        dup, g_cw, g_cb = _ffn_bwd(up, dact, f_conv_w_full[l], f_conv_b[l:l + 1], f"ffn{l}_mid_bwd")
        g_up = _matmul(h, dup, "tn", BF16, f"ffn{l}_up_dw")
        rs_up, tok = _rs_start(g_up, 1, f"rs_start_f_up{l}")
        dh = _matmul(dup, w_f_up[l], "nt", F32, f"ffn{l}_up_dx", after=tok)
        dxin, dxinb, g_norm = _rmsnorm_bwd(xin, f_norm[l:l + 1], dh, dx, f"ffn{l}_norm_bwd")
        return dxin, dxinb, (rs_up, rs_down, g_cw, g_cb, g_norm)

    dx3, dx3b, gf1 = ffn_backward(x3, 1, saved1, dx4, dx4b)
    g_b_out = _matmul(ug, dx3b, "tn", BF16, "mixb_out_dw")
    rs_b_out, tok = _rs_start(g_b_out, 0, "rs_start_b_out")
    dug = _matmul(dx3b, w_b_out, "nt", BF16, "mixb_out_dx", after=tok)
    dpre, g_ws, g_bs_wide, g_bvnorm = _mixb_bwd(pre, dug, b_vnorm_full, ws, bs_wide, "mixb_mid_bwd")
    g_b_in = _matmul(h2, dpre, "tn", BF16, "mixb_in_dw")
    rs_b_in, tok = _rs_start(g_b_in, 1, "rs_start_b_in")
    dh2 = _matmul(dpre, w_b_in, "nt", F32, "mixb_in_dx", after=tok)
    dx2, dx2b, g_bnorm = _rmsnorm_bwd(x2, b_norm_full, dh2, dx3, "mixb_norm_bwd")
    dx1, dx1b, gf0 = ffn_backward(x1, 0, saved0, dx2, dx2b)
    g_a_out = _matmul(ya, dx1b, "tn", BF16, "mixa_out_dw")
    rs_a_out, tok = _rs_start(g_a_out, 0, "rs_start_a_out")
    dya = _matmul(dx1b, w_a_out, "nt", BF16, "mixa_out_dx", after=tok)
    dbcx, g_aconv = _mixa_bwd(bcx, dya, a_conv_full, "mixa_mid_bwd")
    g_a_in = _matmul(h0, dbcx, "tn", BF16, "mixa_in_dw")
    rs_a_in, tok = _rs_start(g_a_in, 1, "rs_start_a_in")
    dh0 = _matmul(dbcx, w_a_in, "nt", F32, "mixa_in_dx", after=tok)
    grad_x, _, g_anorm = _rmsnorm_bwd(x0, a_norm, dh0, dx1, "mixa_norm_bwd")

    big = {}
    for name, states, axis, w, m, v in (
            ("f_down", (gf0[1], gf1[1]), 0, f_down, m_f_down, v_f_down),
            ("f_up", (gf0[0], gf1[0]), 1, f_up, m_f_up, v_f_up),
            ("b_out", (rs_b_out,), 0, b_out, m_b_out, v_b_out), ("b_in", (rs_b_in,), 1, b_in, m_b_in, v_b_in),
            ("a_out", (rs_a_out,), 0, a_out, m_a_out, v_a_out), ("a_in", (rs_a_in,), 1, a_in, m_a_in, v_a_in)):
        recvs = [_exchange_wait(False, st, axis, grad_x, f"rs_wait_{name}{l}") for l, st in enumerate(states)]
        big[name] = _adamw_sharded(recvs, w, m, v, f"adamw_{name}")

    full_shapes = [(1, LANES), (1, d), (3, d), (1, d), (1, d), (SG_GROUPS, CHUNK, CHUNK), (SG_GROUPS, CHUNK),
                   (n_ffn, d), (n_ffn, 3, f2), (n_ffn, f2), (1, d)]
    parts = [loss_part, g_anorm, g_aconv, g_bnorm, g_bvnorm, g_ws, g_bs_wide[:, :, 0],
             jnp.concatenate([gf0[4], gf1[4]], axis=0), jnp.stack([gf0[2], gf1[2]]),
             jnp.concatenate([gf0[3], gf1[3]], axis=0), g_final]
    total = _all_reduce_small(_pack(parts), "all_reduce_small")
    (loss_v, r_anorm, r_aconv, r_bnorm, r_bvnorm, r_ws, r_bs, r_fnorm, r_fcw, r_fcb, r_final) = _unpack(total, full_shapes)
    small_grads = [
        r_anorm,
        lax.dynamic_slice_in_dim(r_aconv, me * cshard, cshard, axis=1).reshape(a_conv.shape),
        lax.dynamic_slice_in_dim(r_bnorm, me * cshard, cshard, axis=1),
        lax.dynamic_slice_in_dim(r_bvnorm, me * cshard, cshard, axis=1),
        r_ws.reshape(b_ws.shape), r_bs.reshape(b_bs.shape), r_fnorm,
        lax.dynamic_slice_in_dim(r_fcw, me * fshard, fshard, axis=2),
        r_fcb, r_final.reshape(final_norm.shape)]
    small_w = [a_norm, a_conv, b_norm, b_vnorm, b_ws, b_bs, f_norm, f_conv_w, f_conv_b, final_norm]
    small_m = [m_a_norm, m_a_conv, m_b_norm, m_b_vnorm, m_b_ws, m_b_bs, m_f_norm, m_f_conv_w, m_f_conv_b, m_final_norm]
    small_v = [v_a_norm, v_a_conv, v_b_norm, v_b_vnorm, v_b_ws, v_b_bs, v_f_norm, v_f_conv_w, v_f_conv_b, v_final_norm]
    shapes = [w.shape for w in small_w]
    packed = _adamw_packed(_pack(small_w), _pack(small_grads), _pack(small_m), _pack(small_v), "adamw_small")
    s_delta, s_m, s_v = (_unpack(p, shapes) for p in packed)
    small_names = ["a_norm", "a_conv", "b_norm", "b_vnorm", "b_ws", "b_bs", "f_norm", "f_conv_w", "f_conv_b", "final_norm"]
    small = {nm: (small_grads[i], s_delta[i], s_m[i], s_v[i]) for i, nm in enumerate(small_names)}

    order = ["a_norm", "a_in", "a_conv", "a_out", "b_norm", "b_in", "b_vnorm", "b_ws", "b_bs", "b_out",
             "f_norm", "f_up", "f_conv_w", "f_conv_b", "f_down", "final_norm"]
    res = {nm: (big[nm] if nm in big else small[nm]) for nm in order}
    outs = [loss_v[0, 0], grad_x.reshape(x.shape)]
    for k in range(4):
        outs += [res[nm][k] for nm in order]
    return tuple(outs)


def _unpack_rows(rows, shapes):
    out, off = [], 0
    for shp in shapes:
        n = math.prod(shp)
        out.append(rows[:, off:off + n].reshape((N_DEV,) + tuple(shp)))
        off += n + (-n) % PACK_GRANULE
    return out
```

```python
import functools
import math

import jax
import jax.numpy as jnp
from jax import lax
from jax.experimental import pallas as pl
from jax.experimental.pallas import tpu as pltpu

F32 = jnp.float32
BF16 = jnp.bfloat16
MESH = pl.DeviceIdType.MESH

N_DEV = 8
RMS_EPS = 1e-5
CHUNK = 128
SG_GROUPS = 8
ADAM_LR = 0.001
ADAM_B1 = 0.9
ADAM_B2 = 0.999
ADAM_EPS = 1e-08
ADAM_WD = 0.01
ADAM_STEP = 10

LANES = 128
SLAB = 16
VMEM_LIMIT = 48 * 1024 * 1024
PACK_GRANULE = 8 * LANES


def _pick(dim, cap, mult):
    best = None
    t = mult
    while t <= min(dim, cap):
        if dim % t == 0:
            best = t
        t += mult
    return dim if best is None else best


def _params(semantics=None):
    return pltpu.CompilerParams(dimension_semantics=semantics, vmem_limit_bytes=VMEM_LIMIT)


_DIMS = {
    "nn": (((1,), (0,)), ((), ())),
    "nt": (((1,), (1,)), ((), ())),
    "tn": (((0,), (0,)), ((), ())),
}


def _matmul(a, b, mode, out_dtype, name, resid=None, after=None, tm_cap=1024, tn_cap=1024, tk_cap=2816):
    if mode == "nn":
        (m, k), n = a.shape, b.shape[1]
    elif mode == "nt":
        (m, k), n = a.shape, b.shape[0]
    else:
        (k, m), n = a.shape, b.shape[1]
    tm, tn, tk = _pick(m, tm_cap, LANES), _pick(n, tn_cap, LANES), _pick(k, tk_cap, LANES)
    nk = k // tk
    n_in = 2 + (resid is not None) + (after is not None)

    def body(*refs):
        a_ref, b_ref = refs[:2]
        r_ref = refs[2] if resid is not None else None
        o_ref = refs[n_in]
        prod = lax.dot_general(a_ref[...], b_ref[...], _DIMS[mode], preferred_element_type=F32)

        def finish(r):
            if r_ref is not None:
                r = r + r_ref[...]
            o_ref[...] = r.astype(out_dtype)

        if nk == 1:
            finish(prod)
            return
        acc_ref = refs[n_in + 1]
        kk = pl.program_id(2)

        @pl.when(kk == 0)
        def _():
            acc_ref[...] = prod

        @pl.when(jnp.logical_and(kk > 0, kk < nk - 1))
        def _():
            acc_ref[...] += prod

        @pl.when(kk == nk - 1)
        def _():
            finish(acc_ref[...] + prod)

    a_spec = (pl.BlockSpec((tk, tm), lambda i, j, kk: (kk, i)) if mode == "tn"
              else pl.BlockSpec((tm, tk), lambda i, j, kk: (i, kk)))
    b_spec = (pl.BlockSpec((tn, tk), lambda i, j, kk: (j, kk)) if mode == "nt"
              else pl.BlockSpec((tk, tn), lambda i, j, kk: (kk, j)))
    o_spec = pl.BlockSpec((tm, tn), lambda i, j, kk: (i, j))
    in_specs = [a_spec, b_spec] + ([o_spec] if resid is not None else [])
    args = (a, b) + ((resid,) if resid is not None else ())
    if after is not None:
        in_specs.append(pl.BlockSpec(memory_space=pl.ANY))
        args += (after,)
    return pl.pallas_call(
        body, name=name, grid=(m // tm, n // tn, nk),
        in_specs=in_specs, out_specs=o_spec,
        out_shape=jax.ShapeDtypeStruct((m, n), out_dtype),
        scratch_shapes=[pltpu.VMEM((tm, tn), F32)] if nk > 1 else [],
        compiler_params=_params(("parallel", "parallel", "arbitrary")),
    )(*args)


def _rms_stats(xf):
    inv = lax.rsqrt(jnp.mean(xf * xf, axis=-1, keepdims=True) + RMS_EPS)
    return inv, xf * inv


def _rmsnorm_fwd(x, g, name, after=None):
    s, d = x.shape
    tm = _pick(s, 256, SLAB)
    extra = () if after is None else (after,)

    def body(x_ref, g_ref, *rest):
        _, xhat = _rms_stats(x_ref[...])
        rest[-1][...] = (xhat * g_ref[...]).astype(BF16)

    return pl.pallas_call(
        body, name=name, grid=(s // tm,),
        in_specs=[pl.BlockSpec((tm, d), lambda i: (i, 0)), pl.BlockSpec((1, d), lambda i: (0, 0))]
        + [pl.BlockSpec(memory_space=pl.ANY)] * len(extra),
        out_specs=pl.BlockSpec((tm, d), lambda i: (i, 0)),
        out_shape=jax.ShapeDtypeStruct((s, d), BF16),
        compiler_params=_params(("parallel",)),
    )(x, g, *extra)


def _rmsnorm_bwd(x, g, dh, dx_out, name):
    s, d = x.shape
    tm = _pick(s, 256, SLAB)

    def body(x_ref, g_ref, dh_ref, dxo_ref, dxi_ref, dxib_ref, dg_ref):
        inv, xhat = _rms_stats(x_ref[...])
        dhv = dh_ref[...]
        dxhat = dhv * g_ref[...]
        proj = jnp.mean(dxhat * xhat, axis=-1, keepdims=True)
        dx = dxo_ref[...] + inv * (dxhat - xhat * proj)
        dxi_ref[...] = dx
        dxib_ref[...] = dx.astype(BF16)
        part = jnp.sum(dhv * xhat, axis=0, keepdims=True)

        @pl.when(pl.program_id(0) == 0)
        def _():
            dg_ref[...] = part

        @pl.when(pl.program_id(0) > 0)
        def _():
            dg_ref[...] += part

    row = pl.BlockSpec((tm, d), lambda i: (i, 0))
    vec = pl.BlockSpec((1, d), lambda i: (0, 0))
    return pl.pallas_call(
        body, name=name, grid=(s // tm,),
        in_specs=[row, vec, row, row], out_specs=[row, row, vec],
        out_shape=[jax.ShapeDtypeStruct((s, d), F32), jax.ShapeDtypeStruct((s, d), BF16),
                   jax.ShapeDtypeStruct((1, d), F32)],
        compiler_params=_params(("arbitrary",)),
    )(x, g, dh, dx_out)


def _final_loss(x, g, target, name):
    s, d = x.shape
    tm = _pick(s, 256, SLAB)

    def body(x_ref, g_ref, t_ref, dx_ref, dxb_ref, loss_ref, dg_ref):
        inv, xhat = _rms_stats(x_ref[...])
        gv = g_ref[...]
        err = xhat * gv - t_ref[...]
        loss = 0.5 * jnp.sum(jnp.mean(err * err, axis=-1, keepdims=True), axis=0, keepdims=True)
        dy = err * (1.0 / d)
        dxhat = dy * gv
        proj = jnp.mean(dxhat * xhat, axis=-1, keepdims=True)
        dx = inv * (dxhat - xhat * proj)
        dx_ref[...] = dx
        dxb_ref[...] = dx.astype(BF16)
        part = jnp.sum(dy * xhat, axis=0, keepdims=True)
        loss_row = jnp.broadcast_to(loss, (1, LANES))

        @pl.when(pl.program_id(0) == 0)
        def _():
            dg_ref[...] = part
            loss_ref[...] = loss_row

        @pl.when(pl.program_id(0) > 0)
        def _():
            dg_ref[...] += part
            loss_ref[...] += loss_row

    row = pl.BlockSpec((tm, d), lambda i: (i, 0))
    vec = pl.BlockSpec((1, d), lambda i: (0, 0))
    return pl.pallas_call(
        body, name=name, grid=(s // tm,),
        in_specs=[row, vec, row],
        out_specs=[row, row, pl.BlockSpec((1, LANES), lambda i: (0, 0)), vec],
        out_shape=[jax.ShapeDtypeStruct((s, d), F32), jax.ShapeDtypeStruct((s, d), BF16),
                   jax.ShapeDtypeStruct((1, LANES), F32), jax.ShapeDtypeStruct((1, d), F32)],
        compiler_params=_params(("arbitrary",)),
    )(x, g, target)


def _shift_down(prev, cur, k):
    ext = jnp.concatenate([prev, cur], axis=0)
    return pltpu.roll(ext, k, 0)[SLAB:, :]


def _shift_up(cur, nxt, k):
    ext = jnp.concatenate([cur, nxt], axis=0)
    return pltpu.roll(ext, 2 * SLAB - k, 0)[:SLAB, :]


def _conv3(w_ref, cols, prev, cur):
    s1 = _shift_down(prev, cur, 1)
    s2 = _shift_down(prev, cur, 2)
    y = w_ref[0:1, cols] * s2 + w_ref[1:2, cols] * s1 + w_ref[2:3, cols] * cur
    return y, s1, s2


def _conv3_t(w_ref, cols, cur, nxt):
    return (w_ref[2:3, cols] * cur + w_ref[1:2, cols] * _shift_up(cur, nxt, 1)
            + w_ref[0:1, cols] * _shift_up(cur, nxt, 2))


def _rows(s):
    return pl.ds(pl.multiple_of(s * SLAB, SLAB), SLAB)


def _halo_specs(tm, width, n_tiles):
    per = tm // SLAB
    prev = pl.BlockSpec((SLAB, width), lambda i: (jnp.maximum(i * per - 1, 0), 0))
    nxt = pl.BlockSpec((SLAB, width), lambda i: (jnp.minimum((i + 1) * per, n_tiles * per - 1), 0))
    return prev, nxt


def _add_rows(acc_ref, out_ref, row, cols, first):
    part = jnp.sum(acc_ref[...], axis=0, keepdims=True)

    @pl.when(first)
    def _():
        out_ref[row:row + 1, cols] = part

    @pl.when(jnp.logical_not(first))
    def _():
        out_ref[row:row + 1, cols] += part


def _mixa_fwd(bcx, wc, name):
    s, d3 = bcx.shape
    d = d3 // 3
    tm = _pick(s, 256, SLAB)
    w = _pick(d, 512, LANES)
    nslab = tm // SLAB

    def body(t_ref, prev_ref, wc_ref, y_ref):
        first_tile = pl.program_id(0) == 0
        for c in range(d // w):
            cb, cc, cx = (slice(g * d + c * w, g * d + (c + 1) * w) for g in range(3))
            cols = slice(c * w, (c + 1) * w)
            p_halo = prev_ref[:, cc].astype(F32) * prev_ref[:, cx].astype(F32)
            p_halo = jnp.where(first_tile, 0.0, p_halo)

            def slab(si, p_prev):
                r = _rows(si)
                p = t_ref[r, cc].astype(F32) * t_ref[r, cx].astype(F32)
                cv, _, _ = _conv3(wc_ref, cols, p_prev, p)
                y_ref[r, cols] = (t_ref[r, cb].astype(F32) * cv).astype(BF16)
                return p

            lax.fori_loop(0, nslab, slab, p_halo)

    prev_spec, _ = _halo_specs(tm, d3, s // tm)
    return pl.pallas_call(
        body, name=name, grid=(s // tm,),
        in_specs=[pl.BlockSpec((tm, d3), lambda i: (i, 0)), prev_spec, pl.BlockSpec((3, d), lambda i: (0, 0))],
        out_specs=pl.BlockSpec((tm, d), lambda i: (i, 0)),
        out_shape=jax.ShapeDtypeStruct((s, d), BF16),
        compiler_params=_params(("parallel",)),
    )(bcx, bcx, wc)


def _mixa_bwd(bcx, dy, wc, name):
    s, d3 = bcx.shape
    d = d3 // 3
    tm = _pick(s, 256, SLAB)
    w = _pick(d, 256, LANES)
    nslab = tm // SLAB
    n_tiles = s // tm

    def body(t_ref, prev_ref, next_ref, dy_ref, dyn_ref, wc_ref, o_ref, dwc_ref, a0, a1, a2):
        i = pl.program_id(0)
        first_tile = i == 0
        last_tile = i == n_tiles - 1
        for c in range(d // w):
            cb, cc, cx = (slice(g * d + c * w, g * d + (c + 1) * w) for g in range(3))
            cols = slice(c * w, (c + 1) * w)
            for acc in (a0, a1, a2):
                acc[...] = jnp.zeros_like(acc)
            dcv_next = jnp.where(last_tile, 0.0, dyn_ref[:, cols].astype(F32) * next_ref[:, cb].astype(F32))
            p_halo = jnp.where(first_tile, 0.0, prev_ref[:, cc].astype(F32) * prev_ref[:, cx].astype(F32))

            def one(r, p_prev, dcv_nxt):
                gb = t_ref[r, cb].astype(F32)
                gc = t_ref[r, cc].astype(F32)
                xs = t_ref[r, cx].astype(F32)
                dyv = dy_ref[r, cols].astype(F32)
                p = gc * xs
                cv, s1, s2 = _conv3(wc_ref, cols, p_prev, p)
                dcv = dyv * gb
                a2[...] += dcv * p
                a1[...] += dcv * s1
                a0[...] += dcv * s2
                dp = _conv3_t(wc_ref, cols, dcv, dcv_nxt)
                o_ref[r, cb] = (dyv * cv).astype(BF16)
                o_ref[r, cc] = (dp * xs).astype(BF16)
                o_ref[r, cx] = (dp * gc).astype(BF16)
                return dcv

            def slab(j, dcv_nxt):
                si = nslab - 1 - j
                rp = _rows(si - 1)
                p_prev = t_ref[rp, cc].astype(F32) * t_ref[rp, cx].astype(F32)
                return one(_rows(si), p_prev, dcv_nxt)

            dcv_nxt = lax.fori_loop(0, nslab - 1, slab, dcv_next)
            one(pl.ds(0, SLAB), p_halo, dcv_nxt)
            for k, acc in enumerate((a0, a1, a2)):
                _add_rows(acc, dwc_ref, k, cols, first_tile)

    prev_spec, next_spec = _halo_specs(tm, d3, n_tiles)
    _, next_dy = _halo_specs(tm, d, n_tiles)
    return pl.pallas_call(
        body, name=name, grid=(n_tiles,),
        in_specs=[pl.BlockSpec((tm, d3), lambda i: (i, 0)), prev_spec, next_spec,
                  pl.BlockSpec((tm, d), lambda i: (i, 0)), next_dy, pl.BlockSpec((3, d), lambda i: (0, 0))],
        out_specs=[pl.BlockSpec((tm, d3), lambda i: (i, 0)), pl.BlockSpec((3, d), lambda i: (0, 0))],
        out_shape=[jax.ShapeDtypeStruct((s, d3), BF16), jax.ShapeDtypeStruct((3, d), F32)],
        scratch_shapes=[pltpu.VMEM((SLAB, w), F32)] * 3,
        compiler_params=_params(("arbitrary",)),
    )(bcx, bcx, bcx, dy, dy, wc)


def _sigmoid(z):
    return 1.0 / (1.0 + jnp.exp(-z))


def _ffn_fwd(up, cw, cb, name):
    s, f2 = up.shape
    f = f2 // 2
    tm = _pick(s, 256, SLAB)
    w = _pick(f, 512, LANES)
    nslab = tm // SLAB

    def body(t_ref, prev_ref, cw_ref, cb_ref, act_ref):
        first_tile = pl.program_id(0) == 0
        for c in range(f // w):
            cg = slice(c * w, (c + 1) * w)
            ca = slice(f + c * w, f + (c + 1) * w)
            halo = tuple(jnp.where(first_tile, 0.0, prev_ref[:, cs].astype(F32)) for cs in (cg, ca))

            def slab(si, carry):
                r = _rows(si)
                g = t_ref[r, cg].astype(F32)
                a = t_ref[r, ca].astype(F32)
                gcv = _conv3(cw_ref, cg, carry[0], g)[0] + cb_ref[:, cg]
                acv = _conv3(cw_ref, ca, carry[1], a)[0] + cb_ref[:, ca]
                act_ref[r, cg] = (gcv * _sigmoid(gcv) * acv).astype(BF16)
                return g, a

            lax.fori_loop(0, nslab, slab, halo)

    prev_spec, _ = _halo_specs(tm, f2, s // tm)
    return pl.pallas_call(
        body, name=name, grid=(s // tm,),
        in_specs=[pl.BlockSpec((tm, f2), lambda i: (i, 0)), prev_spec,
                  pl.BlockSpec((3, f2), lambda i: (0, 0)), pl.BlockSpec((1, f2), lambda i: (0, 0))],
        out_specs=pl.BlockSpec((tm, f), lambda i: (i, 0)),
        out_shape=jax.ShapeDtypeStruct((s, f), BF16),
        compiler_params=_params(("parallel",)),
    )(up, up, cw, cb)


def _ffn_bwd(up, dact, cw, cb, name):
    s, f2 = up.shape
    f = f2 // 2
    tm = _pick(s, 128, SLAB)
    w = _pick(f, 256, LANES)
    nslab = tm // SLAB
    n_tiles = s // tm

    def body(t_ref, prev_ref, next_ref, da_ref, dan_ref, cw_ref, cb_ref, o_ref, dcw_ref, dcb_ref, *accs):
        i = pl.program_id(0)
        first_tile = i == 0
        last_tile = i == n_tiles - 1
        last_rows = pl.ds((nslab - 1) * SLAB, SLAB)
        for c in range(f // w):
            cg = slice(c * w, (c + 1) * w)
            ca = slice(f + c * w, f + (c + 1) * w)
            for acc in accs:
                acc[...] = jnp.zeros_like(acc)

            def grads(dav, g_prev, a_prev, g, a):
                gcv, g1, g2 = _conv3(cw_ref, cg, g_prev, g)
                acv, a1, a2 = _conv3(cw_ref, ca, a_prev, a)
                gcv = gcv + cb_ref[:, cg]
                acv = acv + cb_ref[:, ca]
                sg = _sigmoid(gcv)
                d_a = dav * (gcv * sg)
                d_g = dav * acv * (sg * (1.0 + gcv * (1.0 - sg)))
                return d_g, d_a, (g1, g2, a1, a2)

            nxt = grads(dan_ref[:, cg].astype(F32), t_ref[last_rows, cg].astype(F32),
                        t_ref[last_rows, ca].astype(F32), next_ref[:, cg].astype(F32),
                        next_ref[:, ca].astype(F32))[:2]
            nxt = tuple(jnp.where(last_tile, 0.0, v) for v in nxt)
            halo = tuple(jnp.where(first_tile, 0.0, prev_ref[:, cs].astype(F32)) for cs in (cg, ca))

            def one(r, g_prev, a_prev, carry):
                g = t_ref[r, cg].astype(F32)
                a = t_ref[r, ca].astype(F32)
                d_g, d_a, (g1, g2, a1, a2) = grads(da_ref[r, cg].astype(F32), g_prev, a_prev, g, a)
                for acc, term in zip(accs, (d_g * g2, d_g * g1, d_g * g, d_g, d_a * a2, d_a * a1, d_a * a, d_a)):
                    acc[...] += term
                o_ref[r, cg] = _conv3_t(cw_ref, cg, d_g, carry[0]).astype(BF16)
                o_ref[r, ca] = _conv3_t(cw_ref, ca, d_a, carry[1]).astype(BF16)
                return d_g, d_a

            def slab(j, carry):
                si = nslab - 1 - j
                rp = _rows(si - 1)
                return one(_rows(si), t_ref[rp, cg].astype(F32), t_ref[rp, ca].astype(F32), carry)

            carry = lax.fori_loop(0, nslab - 1, slab, nxt)
            one(pl.ds(0, SLAB), halo[0], halo[1], carry)
            for half, cs in enumerate((cg, ca)):
                for k in range(3):
                    _add_rows(accs[4 * half + k], dcw_ref, k, cs, first_tile)
                _add_rows(accs[4 * half + 3], dcb_ref, 0, cs, first_tile)

    prev_spec, next_spec = _halo_specs(tm, f2, n_tiles)
    _, next_da = _halo_specs(tm, f, n_tiles)
    return pl.pallas_call(
        body, name=name, grid=(n_tiles,),
        in_specs=[pl.BlockSpec((tm, f2), lambda i: (i, 0)), prev_spec, next_spec,
                  pl.BlockSpec((tm, f), lambda i: (i, 0)), next_da,
                  pl.BlockSpec((3, f2), lambda i: (0, 0)), pl.BlockSpec((1, f2), lambda i: (0, 0))],
        out_specs=[pl.BlockSpec((tm, f2), lambda i: (i, 0)), pl.BlockSpec((3, f2), lambda i: (0, 0)),
                   pl.BlockSpec((1, f2), lambda i: (0, 0))],
        out_shape=[jax.ShapeDtypeStruct((s, f2), BF16), jax.ShapeDtypeStruct((3, f2), F32),
                   jax.ShapeDtypeStruct((1, f2), F32)],
        scratch_shapes=[pltpu.VMEM((SLAB, w), F32)] * 8,
        compiler_params=_params(("arbitrary",)),
    )(up, up, up, dact, dact, cw, cb)


_GELU_C = math.sqrt(2.0 / math.pi)


def _gelu(x):
    th = jnp.tanh(_GELU_C * (x + 0.044715 * (x * x * x)))
    return x * (0.5 * (1.0 + th)), th


def _gelu_grad(x, th):
    return 0.5 * (1.0 + th) + 0.5 * x * (1.0 - th * th) * (_GELU_C * (1.0 + 3.0 * 0.044715 * (x * x)))


def _masked_ws(ws_ref, h):
    t = lax.broadcasted_iota(jnp.int32, (CHUNK, CHUNK), 0)
    sx = lax.broadcasted_iota(jnp.int32, (CHUNK, CHUNK), 1)
    return jnp.where(sx <= t, ws_ref[h], 0.0)


def _mixb_fwd(pre, gv, ws, bs_wide, name):
    s, w2 = pre.shape
    w = w2 // 2
    gw = w // SG_GROUPS

    def body(pre_ref, gv_ref, ws_ref, bs_ref, o_ref):
        zu, _ = _gelu(pre_ref[:, :w].astype(F32))
        zv, _ = _gelu(pre_ref[:, w:].astype(F32))
        _, vhat = _rms_stats(zv)
        vn = (vhat * gv_ref[...]).astype(BF16)
        for h in range(SG_GROUPS):
            cols = slice(h * gw, (h + 1) * gw)
            wsm = _masked_ws(ws_ref, h).astype(BF16)
            gate = jnp.dot(wsm, vn[:, cols], preferred_element_type=F32)
            gate = gate + jnp.tile(bs_ref[h], (1, gw // LANES))
            o_ref[:, cols] = (zu[:, cols] * gate).astype(BF16)

    return pl.pallas_call(
        body, name=name, grid=(s // CHUNK,),
        in_specs=[pl.BlockSpec((CHUNK, w2), lambda i: (i, 0)), pl.BlockSpec((1, w), lambda i: (0, 0)),
                  pl.BlockSpec((SG_GROUPS, CHUNK, CHUNK), lambda i: (0, 0, 0)),
                  pl.BlockSpec((SG_GROUPS, CHUNK, LANES), lambda i: (0, 0, 0))],
        out_specs=pl.BlockSpec((CHUNK, w), lambda i: (i, 0)),
        out_shape=jax.ShapeDtypeStruct((s, w), BF16),
        compiler_params=_params(("parallel",)),
    )(pre, gv, ws, bs_wide)


def _mixb_bwd(pre, dug, gv, ws, bs_wide, name):
    s, w2 = pre.shape
    w = w2 // 2
    gw = w // SG_GROUPS

    def body(pre_ref, dug_ref, gv_ref, ws_ref, bs_ref, o_ref, dws_ref, dbs_ref, dgv_ref, dvn_ref):
        first = pl.program_id(0) == 0

        @pl.when(first)
        def _():
            dws_ref[...] = jnp.zeros_like(dws_ref)
            dbs_ref[...] = jnp.zeros_like(dbs_ref)

        pu = pre_ref[:, :w].astype(F32)
        pv = pre_ref[:, w:].astype(F32)
        zu, thu = _gelu(pu)
        zv, thv = _gelu(pv)
        inv, vhat = _rms_stats(zv)
        gvv = gv_ref[...]
        vn = (vhat * gvv).astype(BF16)
        for h in range(SG_GROUPS):
            cols = slice(h * gw, (h + 1) * gw)
            wsm = _masked_ws(ws_ref, h).astype(BF16)
            gate = jnp.dot(wsm, vn[:, cols], preferred_element_type=F32)
            gate = gate + jnp.tile(bs_ref[h], (1, gw // LANES))
            dug_h = dug_ref[:, cols].astype(F32)
            dgate = dug_h * zu[:, cols]
            dgate_b = dgate.astype(BF16)
            o_ref[:, cols] = (dug_h * gate * _gelu_grad(pu[:, cols], thu[:, cols])).astype(BF16)
            dbs_ref[h] += jnp.broadcast_to(jnp.sum(dgate, axis=-1, keepdims=True), (CHUNK, LANES))
            dws = lax.dot_general(dgate_b, vn[:, cols], _DIMS["nt"], preferred_element_type=F32)
            t = lax.broadcasted_iota(jnp.int32, (CHUNK, CHUNK), 0)
            sx = lax.broadcasted_iota(jnp.int32, (CHUNK, CHUNK), 1)
            dws_ref[h] += jnp.where(sx <= t, dws, 0.0)
            dvn_ref[:, cols] = lax.dot_general(wsm, dgate_b, _DIMS["tn"], preferred_element_type=F32)
        dvn = dvn_ref[...]
        part = jnp.sum(dvn * vhat, axis=0, keepdims=True)

        @pl.when(first)
        def _():
            dgv_ref[...] = part

        @pl.when(jnp.logical_not(first))
        def _():
            dgv_ref[...] += part

        dvhat = dvn * gvv
        dzv = inv * (dvhat - vhat * jnp.mean(dvhat * vhat, axis=-1, keepdims=True))
        o_ref[:, w:] = (dzv * _gelu_grad(pv, thv)).astype(BF16)

    return pl.pallas_call(
        body, name=name, grid=(s // CHUNK,),
        in_specs=[pl.BlockSpec((CHUNK, w2), lambda i: (i, 0)), pl.BlockSpec((CHUNK, w), lambda i: (i, 0)),
                  pl.BlockSpec((1, w), lambda i: (0, 0)),
                  pl.BlockSpec((SG_GROUPS, CHUNK, CHUNK), lambda i: (0, 0, 0)),
                  pl.BlockSpec((SG_GROUPS, CHUNK, LANES), lambda i: (0, 0, 0))],
        out_specs=[pl.BlockSpec((CHUNK, w2), lambda i: (i, 0)),
                   pl.BlockSpec((SG_GROUPS, CHUNK, CHUNK), lambda i: (0, 0, 0)),
                   pl.BlockSpec((SG_GROUPS, CHUNK, LANES), lambda i: (0, 0, 0)),
                   pl.BlockSpec((1, w), lambda i: (0, 0))],
        out_shape=[jax.ShapeDtypeStruct((s, w2), BF16), jax.ShapeDtypeStruct((SG_GROUPS, CHUNK, CHUNK), F32),
                   jax.ShapeDtypeStruct((SG_GROUPS, CHUNK, LANES), F32), jax.ShapeDtypeStruct((1, w), F32)],
        scratch_shapes=[pltpu.VMEM((CHUNK, w), F32)],
        compiler_params=_params(("arbitrary",)),
    )(pre, dug, gv, ws, bs_wide)


def _cast_layer(w3, layer, name):
    _, r, c = w3.shape
    tr = _pick(r, 256, SLAB)

    def body(w_ref, o_ref):
        o_ref[...] = w_ref[...].astype(BF16)

    return pl.pallas_call(
        body, name=name, grid=(r // tr,),
        in_specs=[pl.BlockSpec((None, tr, c), lambda i: (layer, i, 0))],
        out_specs=pl.BlockSpec((tr, c), lambda i: (i, 0)),
        out_shape=jax.ShapeDtypeStruct((r, c), BF16),
        compiler_params=_params(("parallel",)),
    )(w3)


def _adamw_math(w, g, m, v):
    m = ADAM_B1 * m + (1.0 - ADAM_B1) * g
    v = ADAM_B2 * v + (1.0 - ADAM_B2) * (g * g)
    m_hat = m / (1.0 - ADAM_B1 ** ADAM_STEP)
    v_hat = v / (1.0 - ADAM_B2 ** ADAM_STEP)
    delta = -ADAM_LR * (m_hat / (jnp.sqrt(v_hat) + ADAM_EPS) + ADAM_WD * w)
    return delta, m, v


def _adamw_sharded(recvs, w, m, v, name):
    nl, r, c = w.shape
    tc = _pick(c, 1536, LANES)
    tr = _pick(r, 64, SLAB)

    def body(*refs):
        recv_refs = refs[:nl]
        w_ref, m_ref, v_ref, g_ref, d_ref, nm_ref, nv_ref = refs[nl:]
        for layer, recv_ref in enumerate(recv_refs):
            @pl.when(pl.program_id(0) == layer)
            def _():
                g = recv_ref[0].astype(F32)
                for q in range(1, N_DEV):
                    g = g + recv_ref[q].astype(F32)
                delta, nm, nv = _adamw_math(w_ref[...], g, m_ref[...], v_ref[...])
                g_ref[...] = g
                d_ref[...] = delta
                nm_ref[...] = nm
                nv_ref[...] = nv

    def recv_spec(layer):
        return pl.BlockSpec((N_DEV, tr, tc),
                            lambda l, i, j: (0, jnp.where(l == layer, i, 0), jnp.where(l == layer, j, 0)))

    blk = pl.BlockSpec((None, tr, tc), lambda l, i, j: (l, i, j))
    out = jax.ShapeDtypeStruct((nl, r, c), F32)
    return pl.pallas_call(
        body, name=name, grid=(nl, r // tr, c // tc),
        in_specs=[recv_spec(layer) for layer in range(nl)] + [blk, blk, blk],
        out_specs=[blk] * 4, out_shape=[out] * 4,
        compiler_params=_params(("parallel",) * 3),
    )(*recvs, w, m, v)


def _adamw_packed(w, g, m, v, name):
    r, c = w.shape
    tr = _pick(r, 256, 8)

    def body(w_ref, g_ref, m_ref, v_ref, d_ref, nm_ref, nv_ref):
        delta, nm, nv = _adamw_math(w_ref[...], g_ref[...], m_ref[...], v_ref[...])
        d_ref[...] = delta
        nm_ref[...] = nm
        nv_ref[...] = nv

    blk = pl.BlockSpec((tr, c), lambda i: (i, 0))
    out = jax.ShapeDtypeStruct((r, c), F32)
    return pl.pallas_call(
        body, name=name, grid=(r // tr,), in_specs=[blk] * 4, out_specs=[blk] * 3, out_shape=[out] * 3,
        compiler_params=_params(("parallel",)),
    )(w, g, m, v)


def _pack(arrays):
    parts = []
    for a in arrays:
        flat = a.reshape(-1).astype(F32)
        pad = (-flat.shape[0]) % PACK_GRANULE
        parts.append(jnp.pad(flat, (0, pad)) if pad else flat)
    return jnp.concatenate(parts).reshape(-1, LANES)


def _unpack(buf, shapes):
    flat = buf.reshape(-1)
    out, off = [], 0
    for shp in shapes:
        n = math.prod(shp)
        out.append(flat[off:off + n].reshape(shp))
        off += n + (-n) % PACK_GRANULE
    return out


def _mesh_pos():
    return lax.axis_index("x"), lax.axis_index("y"), lax.axis_index("c")


def _coords(q):
    return q // 4, (q // 2) % 2, q % 2


def _shard_of(ref, q, shard_shape, axis):
    r, c = shard_shape
    if axis == 0:
        return ref.at[pl.ds(pl.multiple_of(q * r, SLAB), r), :]
    return ref.at[:, pl.ds(pl.multiple_of(q * c, LANES), c)]


_HBM = pl.BlockSpec(memory_space=pltpu.HBM)
_SEM = pl.BlockSpec(memory_space=pltpu.SEMAPHORE)
_EFFECT = pltpu.SideEffectType.DATAFLOW_SIDE_EFFECTING


def _exchange_shapes(gather, src_shape, axis):
    r, c = src_shape
    if gather:
        return (r, c), ((r * N_DEV, c) if axis == 0 else (r, c * N_DEV))
    shard = (r // N_DEV, c) if axis == 0 else (r, c // N_DEV)
    return shard, (N_DEV,) + shard


def _exchange_copies(gather, src, land, sems, axis):
    send_sems, recv_sems, own_sem = sems
    x, y, c_ = _mesh_pos()
    me = 4 * x + 2 * y + c_
    shard, _ = _exchange_shapes(gather, src.shape, axis)

    def piece(q):
        return src if gather else _shard_of(src, q, shard, axis)

    def place(q):
        return _shard_of(land, q, shard, axis) if gather else land.at[q]

    own = pltpu.make_async_copy(piece(me), place(me), own_sem.at[0])
    sends, arrivals = [], []
    for step in range(1, N_DEV):
        to = (me + step) % N_DEV
        frm = (me + N_DEV - step) % N_DEV
        sends.append(pltpu.make_async_remote_copy(
            src_ref=piece(to), dst_ref=place(me), send_sem=send_sems.at[step - 1], recv_sem=recv_sems.at[step - 1],
            device_id=_coords(to), device_id_type=MESH))
        arrivals.append(pltpu.make_async_remote_copy(
            src_ref=piece(me), dst_ref=place(frm), send_sem=send_sems.at[step - 1], recv_sem=recv_sems.at[step - 1],
            device_id=_coords(frm), device_id_type=MESH))
    return own, sends, arrivals


def _exchange_start(gather, src, axis, name, after=None):
    _, land_shape = _exchange_shapes(gather, src.shape, axis)
    extra = () if after is None else (after,)

    def body(*refs):
        src_ref, land = refs[:2]
        send_sems, recv_sems, own_sem = refs[2 + len(extra):5 + len(extra)]
        own, sends, _ = _exchange_copies(gather, src_ref, land, (send_sems, recv_sems, own_sem), axis)
        own.start()
        for cp in sends:
            cp.start()
        refs[-1][...] = jnp.zeros_like(refs[-1])

    out = pl.pallas_call(
        body, name=name,
        out_shape=(pltpu.SemaphoreType.DMA((N_DEV - 1,)), pltpu.SemaphoreType.DMA((N_DEV - 1,)),
                   pltpu.SemaphoreType.DMA((1,)), pltpu.HBM(src.shape, src.dtype),
                   pltpu.HBM(land_shape, src.dtype), jax.ShapeDtypeStruct((8, LANES), F32)),
        in_specs=[_HBM, _HBM] + [pl.BlockSpec(memory_space=pl.ANY)] * len(extra),
        out_specs=(_SEM, _SEM, _SEM, _HBM, _HBM, pl.BlockSpec(memory_space=pltpu.VMEM)),
        input_output_aliases={0: 3, 1: 4},
        compiler_params=pltpu.CompilerParams(has_side_effects=_EFFECT),
    )(pltpu.with_memory_space_constraint(src, pltpu.HBM),
      pltpu.with_memory_space_constraint(lax.empty(land_shape, src.dtype), pltpu.HBM), *extra)
    return out[:5], out[5]


def _exchange_wait(gather, state, axis, after, name):
    send_sems, recv_sems, own_sem, src_thru, land_thru = state

    def body(src, land, send_sems, recv_sems, own_sem, after_ref, src_dead, got):
        own, sends, arrivals = _exchange_copies(gather, src, land, (send_sems, recv_sems, own_sem), axis)
        for cp in sends:
            cp.wait_send()
        for cp in arrivals:
            cp.wait_recv()
        own.wait()

    return pl.pallas_call(
        body, name=name,
        out_shape=(pltpu.HBM(src_thru.shape, src_thru.dtype), pltpu.HBM(land_thru.shape, land_thru.dtype)),
        in_specs=[_HBM, _HBM, _SEM, _SEM, _SEM, pl.BlockSpec(memory_space=pl.ANY)],
        out_specs=(_HBM, _HBM),
        input_output_aliases={0: 0, 1: 1},
        compiler_params=pltpu.CompilerParams(has_side_effects=_EFFECT),
    )(src_thru, land_thru, send_sems, recv_sems, own_sem, after)[1]


def _all_reduce_small(part, name):
    r, c = part.shape

    def body(part_ref, out_ref, slots, send_sems, recv_sems, local_sem):
        x, y, cc = _mesh_pos()
        me = 4 * x + 2 * y + cc
        own = pltpu.make_async_copy(part_ref, slots.at[me], local_sem)
        own.start()
        sends = []
        for step in range(1, N_DEV):
            to = (me + step) % N_DEV
            cp = pltpu.make_async_remote_copy(
                src_ref=part_ref, dst_ref=slots.at[me], send_sem=send_sems.at[step - 1],
                recv_sem=recv_sems.at[step - 1], device_id=_coords(to), device_id_type=MESH)
            cp.start()
            sends.append(cp)
        for step in range(1, N_DEV):
            frm = (me + N_DEV - step) % N_DEV
            pltpu.make_async_remote_copy(
                src_ref=part_ref, dst_ref=slots.at[frm], send_sem=send_sems.at[step - 1],
                recv_sem=recv_sems.at[step - 1], device_id=_coords(frm), device_id_type=MESH).wait_recv()
        for cp in sends:
            cp.wait_send()
        own.wait()
        total = slots[0]
        for q in range(1, N_DEV):
            total = total + slots[q]
        out_ref[...] = total

    vmem = pl.BlockSpec(memory_space=pltpu.VMEM)
    return pl.pallas_call(
        body, name=name, in_specs=[vmem], out_specs=vmem,
        out_shape=jax.ShapeDtypeStruct((r, c), F32),
        scratch_shapes=[pltpu.VMEM((N_DEV, r, c), F32), pltpu.SemaphoreType.DMA((N_DEV - 1,)),
                        pltpu.SemaphoreType.DMA((N_DEV - 1,)), pltpu.SemaphoreType.DMA],
        compiler_params=pltpu.CompilerParams(has_side_effects=True, vmem_limit_bytes=VMEM_LIMIT),
    )(part)


def _gather_cols(full_rows, n_rows, shard_cols):
    return full_rows.reshape(N_DEV, n_rows, shard_cols).transpose(1, 0, 2).reshape(n_rows, N_DEV * shard_cols)


def kernel(x, a_norm, a_in, a_conv, a_out, b_norm, b_in, b_vnorm, b_ws, b_bs, b_out, f_norm, f_up, f_conv_w, f_conv_b, f_down, final_norm, loss_target, m_a_norm, m_a_in, m_a_conv, m_a_out, m_b_norm, m_b_in, m_b_vnorm, m_b_ws, m_b_bs, m_b_out, m_f_norm, m_f_up, m_f_conv_w, m_f_conv_b, m_f_down, m_final_norm, v_a_norm, v_a_in, v_a_conv, v_a_out, v_b_norm, v_b_in, v_b_vnorm, v_b_ws, v_b_bs, v_b_out, v_f_norm, v_f_up, v_f_conv_w, v_f_conv_b, v_f_down, v_final_norm):
    s, d = x.shape[1], x.shape[2]
    n_ffn = f_up.shape[0]
    f2 = f_up.shape[2] * N_DEV
    me = 4 * lax.axis_index("x") + 2 * lax.axis_index("y") + lax.axis_index("c")
    x0 = x.reshape(s, d)
    target = loss_target.reshape(s, d)

    wanted = [("a_in", _cast_layer(a_in, 0, "cast_a_in"), 1),
              ("small", _pack([a_conv, b_norm, b_vnorm, f_conv_w]), 0),
              ("a_out", _cast_layer(a_out, 0, "cast_a_out"), 0),
              ("f_up0", _cast_layer(f_up, 0, "cast_f_up0"), 1), ("f_down0", _cast_layer(f_down, 0, "cast_f_down0"), 0),
              ("b_in", _cast_layer(b_in, 0, "cast_b_in"), 1), ("b_out", _cast_layer(b_out, 0, "cast_b_out"), 0),
              ("f_up1", _cast_layer(f_up, 1, "cast_f_up1"), 1), ("f_down1", _cast_layer(f_down, 1, "cast_f_down1"), 0)]
    coming, tok = {}, None
    for key, shard, axis in wanted:
        state, tok = _exchange_start(True, shard, axis, f"ag_start_{key}", after=tok)
        coming[key] = (state, axis)

    def arrived(key, after):
        state, axis = coming[key]
        return _exchange_wait(True, state, axis, after, f"ag_wait_{key}")

    cshard = a_conv.shape[2]
    fshard = f_conv_w.shape[2]
    h0 = _rmsnorm_fwd(x0, a_norm, "mixa_norm", after=tok)
    w_a_in = arrived("a_in", h0)
    small_rows = arrived("small", w_a_in).reshape(N_DEV, -1)
    per_dev = _unpack_rows(small_rows, [(3, cshard), (cshard,), (cshard,), (n_ffn, 3, fshard)])
    a_conv_full = per_dev[0].transpose(1, 0, 2).reshape(3, d)
    b_norm_full = per_dev[1].reshape(1, d)
    b_vnorm_full = per_dev[2].reshape(1, d)
    f_conv_w_full = per_dev[3].transpose(1, 2, 0, 3).reshape(n_ffn, 3, f2)
    bs_wide = jnp.broadcast_to(b_bs[0][:, :, None], (SG_GROUPS, CHUNK, LANES))
    ws = b_ws[0]

    w_f_up, w_f_down = {}, {}

    def ffn_forward(xin, l):
        h = _rmsnorm_fwd(xin, f_norm[l:l + 1], f"ffn{l}_norm")
        w_f_up[l] = arrived(f"f_up{l}", h)
        up = _matmul(h, w_f_up[l], "nn", BF16, f"ffn{l}_up")
        act = _ffn_fwd(up, f_conv_w_full[l], f_conv_b[l:l + 1], f"ffn{l}_mid")
        w_f_down[l] = arrived(f"f_down{l}", act)
        xout = _matmul(act, w_f_down[l], "nn", F32, f"ffn{l}_down", resid=xin, tm_cap=512)
        return xout, (h, up, act)

    bcx = _matmul(h0, w_a_in, "nn", BF16, "mixa_in")
    ya = _mixa_fwd(bcx, a_conv_full, "mixa_mid")
    w_a_out = arrived("a_out", ya)
    x1 = _matmul(ya, w_a_out, "nn", F32, "mixa_out", resid=x0)
    x2, saved0 = ffn_forward(x1, 0)
    h2 = _rmsnorm_fwd(x2, b_norm_full, "mixb_norm")
    w_b_in = arrived("b_in", h2)
    pre = _matmul(h2, w_b_in, "nn", BF16, "mixb_in")
    ug = _mixb_fwd(pre, b_vnorm_full, ws, bs_wide, "mixb_mid")
    w_b_out = arrived("b_out", ug)
    x3 = _matmul(ug, w_b_out, "nn", F32, "mixb_out", resid=x2)
    x4, saved1 = ffn_forward(x3, 1)
    dx4, dx4b, loss_part, g_final = _final_loss(x4, final_norm.reshape(1, d), target, "loss_head")

    def _rs_start(grad, axis, name):
        return _exchange_start(False, grad, axis, name)

    def ffn_backward(xin, l, saved, dx, dxb):
        h, up, act = saved
        g_down = _matmul(act, dxb, "tn", BF16, f"ffn{l}_down_dw", tm_cap=1408)
        rs_down, tok = _rs_start(g_down, 0, f"rs_start_f_down{l}")
        dact = _matmul(dxb, w_f_down[l], "nt", BF16, f"ffn{l}_down_dx", after=tok, tn_cap=1408)
        dup, g_cw, g_cb = _ffn_bwd(up, dact, f_conv_w_full[l], f_conv_b[l:l + 1], f"ffn{l}_mid_bwd")
        g_up = _matmul(h, dup, "tn", BF16, f"ffn{l}_up_dw")
        rs_up, tok = _rs_start(g_up, 1, f"rs_start_f_up{l}")
        dh = _matmul(dup, w_f_up[l], "nt", F32, f"ffn{l}_up_dx", after=tok)
        dxin, dxinb, g_norm = _rmsnorm_bwd(xin, f_norm[l:l + 1], dh, dx, f"ffn{l}_norm_bwd")
        return dxin, dxinb, (rs_up, rs_down, g_cw, g_cb, g_norm)

    dx3, dx3b, gf1 = ffn_backward(x3, 1, saved1, dx4, dx4b)
    g_b_out = _matmul(ug, dx3b, "tn", BF16, "mixb_out_dw")
    rs_b_out, tok = _rs_start(g_b_out, 0, "rs_start_b_out")
    dug = _matmul(dx3b, w_b_out, "nt", BF16, "mixb_out_dx", after=tok)
    dpre, g_ws, g_bs_wide, g_bvnorm = _mixb_bwd(pre, dug, b_vnorm_full, ws, bs_wide, "mixb_mid_bwd")
    g_b_in = _matmul(h2, dpre, "tn", BF16, "mixb_in_dw")
    rs_b_in, tok = _rs_start(g_b_in, 1, "rs_start_b_in")
    dh2 = _matmul(dpre, w_b_in, "nt", F32, "mixb_in_dx", after=tok)
    dx2, dx2b, g_bnorm = _rmsnorm_bwd(x2, b_norm_full, dh2, dx3, "mixb_norm_bwd")
    dx1, dx1b, gf0 = ffn_backward(x1, 0, saved0, dx2, dx2b)
    g_a_out = _matmul(ya, dx1b, "tn", BF16, "mixa_out_dw")
    rs_a_out, tok = _rs_start(g_a_out, 0, "rs_start_a_out")
    dya = _matmul(dx1b, w_a_out, "nt", BF16, "mixa_out_dx", after=tok)
    dbcx, g_aconv = _mixa_bwd(bcx, dya, a_conv_full, "mixa_mid_bwd")
    g_a_in = _matmul(h0, dbcx, "tn", BF16, "mixa_in_dw")
    rs_a_in, tok = _rs_start(g_a_in, 1, "rs_start_a_in")
    dh0 = _matmul(dbcx, w_a_in, "nt", F32, "mixa_in_dx", after=tok)
    grad_x, _, g_anorm = _rmsnorm_bwd(x0, a_norm, dh0, dx1, "mixa_norm_bwd")

    big = {}
    for name, states, axis, w, m, v in (
            ("f_down", (gf0[1], gf1[1]), 0, f_down, m_f_down, v_f_down),
            ("f_up", (gf0[0], gf1[0]), 1, f_up, m_f_up, v_f_up),
            ("b_out", (rs_b_out,), 0, b_out, m_b_out, v_b_out), ("b_in", (rs_b_in,), 1, b_in, m_b_in, v_b_in),
            ("a_out", (rs_a_out,), 0, a_out, m_a_out, v_a_out), ("a_in", (rs_a_in,), 1, a_in, m_a_in, v_a_in)):
        recvs = [_exchange_wait(False, st, axis, grad_x, f"rs_wait_{name}{l}") for l, st in enumerate(states)]
        big[name] = _adamw_sharded(recvs, w, m, v, f"adamw_{name}")

    full_shapes = [(1, LANES), (1, d), (3, d), (1, d), (1, d), (SG_GROUPS, CHUNK, CHUNK), (SG_GROUPS, CHUNK),
                   (n_ffn, d), (n_ffn, 3, f2), (n_ffn, f2), (1, d)]
    parts = [loss_part, g_anorm, g_aconv, g_bnorm, g_bvnorm, g_ws, g_bs_wide[:, :, 0],
             jnp.concatenate([gf0[4], gf1[4]], axis=0), jnp.stack([gf0[2], gf1[2]]),
             jnp.concatenate([gf0[3], gf1[3]], axis=0), g_final]
    total = _all_reduce_small(_pack(parts), "all_reduce_small")
    (loss_v, r_anorm, r_aconv, r_bnorm, r_bvnorm, r_ws, r_bs, r_fnorm, r_fcw, r_fcb, r_final) = _unpack(total, full_shapes)
    small_grads = [
        r_anorm,
        lax.dynamic_slice_in_dim(r_aconv, me * cshard, cshard, axis=1).reshape(a_conv.shape),
        lax.dynamic_slice_in_dim(r_bnorm, me * cshard, cshard, axis=1),
        lax.dynamic_slice_in_dim(r_bvnorm, me * cshard, cshard, axis=1),
        r_ws.reshape(b_ws.shape), r_bs.reshape(b_bs.shape), r_fnorm,
        lax.dynamic_slice_in_dim(r_fcw, me * fshard, fshard, axis=2),
        r_fcb, r_final.reshape(final_norm.shape)]
    small_w = [a_norm, a_conv, b_norm, b_vnorm, b_ws, b_bs, f_norm, f_conv_w, f_conv_b, final_norm]
    small_m = [m_a_norm, m_a_conv, m_b_norm, m_b_vnorm, m_b_ws, m_b_bs, m_f_norm, m_f_conv_w, m_f_conv_b, m_final_norm]
    small_v = [v_a_norm, v_a_conv, v_b_norm, v_b_vnorm, v_b_ws, v_b_bs, v_f_norm, v_f_conv_w, v_f_conv_b, v_final_norm]
    shapes = [w.shape for w in small_w]
    packed = _adamw_packed(_pack(small_w), _pack(small_grads), _pack(small_m), _pack(small_v), "adamw_small")
    s_delta, s_m, s_v = (_unpack(p, shapes) for p in packed)
    small_names = ["a_norm", "a_conv", "b_norm", "b_vnorm", "b_ws", "b_bs", "f_norm", "f_conv_w", "f_conv_b", "final_norm"]
    small = {nm: (small_grads[i], s_delta[i], s_m[i], s_v[i]) for i, nm in enumerate(small_names)}

    order = ["a_norm", "a_in", "a_conv", "a_out", "b_norm", "b_in", "b_vnorm", "b_ws", "b_bs", "b_out",
             "f_norm", "f_up", "f_conv_w", "f_conv_b", "f_down", "final_norm"]
    res = {nm: (big[nm] if nm in big else small[nm]) for nm in order}
    outs = [loss_v[0, 0], grad_x.reshape(x.shape)]
    for k in range(4):
        outs += [res[nm][k] for nm in order]
    return tuple(outs)


def _unpack_rows(rows, shapes):
    out, off = [], 0
    for shp in shapes:
        n = math.prod(shp)
        out.append(rows[:, off:off + n].reshape((N_DEV,) + tuple(shp)))
        off += n + (-n) % PACK_GRANULE
    return out
```

```python
import functools
import math

import jax
import jax.numpy as jnp
from jax import lax
from jax.experimental import pallas as pl
from jax.experimental.pallas import tpu as pltpu

F32 = jnp.float32
BF16 = jnp.bfloat16
MESH = pl.DeviceIdType.MESH

N_DEV = 8
RMS_EPS = 1e-5
CHUNK = 128
SG_GROUPS = 8
ADAM_LR = 0.001
ADAM_B1 = 0.9
ADAM_B2 = 0.999
ADAM_EPS = 1e-08
ADAM_WD = 0.01
ADAM_STEP = 10

LANES = 128
SLAB = 16
VMEM_LIMIT = 48 * 1024 * 1024
PACK_GRANULE = 8 * LANES


def _pick(dim, cap, mult):
    best = None
    t = mult
    while t <= min(dim, cap):
        if dim % t == 0:
            best = t
        t += mult
    return dim if best is None else best


def _params(semantics=None):
    return pltpu.CompilerParams(dimension_semantics=semantics, vmem_limit_bytes=VMEM_LIMIT)


_DIMS = {
    "nn": (((1,), (0,)), ((), ())),
    "nt": (((1,), (1,)), ((), ())),
    "tn": (((0,), (0,)), ((), ())),
}


def _matmul(a, b, mode, out_dtype, name, resid=None, after=None, tm_cap=1024, tn_cap=1024, tk_cap=2816):
    if mode == "nn":
        (m, k), n = a.shape, b.shape[1]
    elif mode == "nt":
        (m, k), n = a.shape, b.shape[0]
    else:
        (k, m), n = a.shape, b.shape[1]
    tm, tn, tk = _pick(m, tm_cap, LANES), _pick(n, tn_cap, LANES), _pick(k, tk_cap, LANES)
    nk = k // tk
    n_in = 2 + (resid is not None) + (after is not None)

    def body(*refs):
        a_ref, b_ref = refs[:2]
        r_ref = refs[2] if resid is not None else None
        o_ref = refs[n_in]
        prod = lax.dot_general(a_ref[...], b_ref[...], _DIMS[mode], preferred_element_type=F32)

        def finish(r):
            if r_ref is not None:
                r = r + r_ref[...]
            o_ref[...] = r.astype(out_dtype)

        if nk == 1:
            finish(prod)
            return
        acc_ref = refs[n_in + 1]
        kk = pl.program_id(2)

        @pl.when(kk == 0)
        def _():
            acc_ref[...] = prod

        @pl.when(jnp.logical_and(kk > 0, kk < nk - 1))
        def _():
            acc_ref[...] += prod

        @pl.when(kk == nk - 1)
        def _():
            finish(acc_ref[...] + prod)

    a_spec = (pl.BlockSpec((tk, tm), lambda i, j, kk: (kk, i)) if mode == "tn"
              else pl.BlockSpec((tm, tk), lambda i, j, kk: (i, kk)))
    b_spec = (pl.BlockSpec((tn, tk), lambda i, j, kk: (j, kk)) if mode == "nt"
              else pl.BlockSpec((tk, tn), lambda i, j, kk: (kk, j)))
    o_spec = pl.BlockSpec((tm, tn), lambda i, j, kk: (i, j))
    in_specs = [a_spec, b_spec] + ([o_spec] if resid is not None else [])
    args = (a, b) + ((resid,) if resid is not None else ())
    if after is not None:
        in_specs.append(pl.BlockSpec(memory_space=pl.ANY))
        args += (after,)
    return pl.pallas_call(
        body, name=name, grid=(m // tm, n // tn, nk),
        in_specs=in_specs, out_specs=o_spec,
        out_shape=jax.ShapeDtypeStruct((m, n), out_dtype),
        scratch_shapes=[pltpu.VMEM((tm, tn), F32)] if nk > 1 else [],
        compiler_params=_params(("parallel", "parallel", "arbitrary")),
    )(*args)


def _rms_stats(xf):
    inv = lax.rsqrt(jnp.mean(xf * xf, axis=-1, keepdims=True) + RMS_EPS)
    return inv, xf * inv


def _rmsnorm_fwd(x, g, name, after=None):
    s, d = x.shape
    tm = _pick(s, 256, SLAB)
    extra = () if after is None else (after,)

    def body(x_ref, g_ref, *rest):
        _, xhat = _rms_stats(x_ref[...])
        rest[-1][...] = (xhat * g_ref[...]).astype(BF16)

    return pl.pallas_call(
        body, name=name, grid=(s // tm,),
        in_specs=[pl.BlockSpec((tm, d), lambda i: (i, 0)), pl.BlockSpec((1, d), lambda i: (0, 0))]
        + [pl.BlockSpec(memory_space=pl.ANY)] * len(extra),
        out_specs=pl.BlockSpec((tm, d), lambda i: (i, 0)),
        out_shape=jax.ShapeDtypeStruct((s, d), BF16),
        compiler_params=_params(("parallel",)),
    )(x, g, *extra)


def _rmsnorm_bwd(x, g, dh, dx_out, name):
    s, d = x.shape
    tm = _pick(s, 256, SLAB)

    def body(x_ref, g_ref, dh_ref, dxo_ref, dxi_ref, dxib_ref, dg_ref):
        inv, xhat = _rms_stats(x_ref[...])
        dhv = dh_ref[...]
        dxhat = dhv * g_ref[...]
        proj = jnp.mean(dxhat * xhat, axis=-1, keepdims=True)
        dx = dxo_ref[...] + inv * (dxhat - xhat * proj)
        dxi_ref[...] = dx
        dxib_ref[...] = dx.astype(BF16)
        part = jnp.sum(dhv * xhat, axis=0, keepdims=True)

        @pl.when(pl.program_id(0) == 0)
        def _():
            dg_ref[...] = part

        @pl.when(pl.program_id(0) > 0)
        def _():
            dg_ref[...] += part

    row = pl.BlockSpec((tm, d), lambda i: (i, 0))
    vec = pl.BlockSpec((1, d), lambda i: (0, 0))
    return pl.pallas_call(
        body, name=name, grid=(s // tm,),
        in_specs=[row, vec, row, row], out_specs=[row, row, vec],
        out_shape=[jax.ShapeDtypeStruct((s, d), F32), jax.ShapeDtypeStruct((s, d), BF16),
                   jax.ShapeDtypeStruct((1, d), F32)],
        compiler_params=_params(("arbitrary",)),
    )(x, g, dh, dx_out)


def _final_loss(x, g, target, name):
    s, d = x.shape
    tm = _pick(s, 256, SLAB)

    def body(x_ref, g_ref, t_ref, dx_ref, dxb_ref, loss_ref, dg_ref):
        inv, xhat = _rms_stats(x_ref[...])
        gv = g_ref[...]
        err = xhat * gv - t_ref[...]
        loss = 0.5 * jnp.sum(jnp.mean(err * err, axis=-1, keepdims=True), axis=0, keepdims=True)
        dy = err * (1.0 / d)
        dxhat = dy * gv
        proj = jnp.mean(dxhat * xhat, axis=-1, keepdims=True)
        dx = inv * (dxhat - xhat * proj)
        dx_ref[...] = dx
        dxb_ref[...] = dx.astype(BF16)
        part = jnp.sum(dy * xhat, axis=0, keepdims=True)
        loss_row = jnp.broadcast_to(loss, (1, LANES))

        @pl.when(pl.program_id(0) == 0)
        def _():
            dg_ref[...] = part
            loss_ref[...] = loss_row

        @pl.when(pl.program_id(0) > 0)
        def _():
            dg_ref[...] += part
            loss_ref[...] += loss_row

    row = pl.BlockSpec((tm, d), lambda i: (i, 0))
    vec = pl.BlockSpec((1, d), lambda i: (0, 0))
    return pl.pallas_call(
        body, name=name, grid=(s // tm,),
        in_specs=[row, vec, row],
        out_specs=[row, row, pl.BlockSpec((1, LANES), lambda i: (0, 0)), vec],
        out_shape=[jax.ShapeDtypeStruct((s, d), F32), jax.ShapeDtypeStruct((s, d), BF16),
                   jax.ShapeDtypeStruct((1, LANES), F32), jax.ShapeDtypeStruct((1, d), F32)],
        compiler_params=_params(("arbitrary",)),
    )(x, g, target)


def _shift_down(prev, cur, k):
    ext = jnp.concatenate([prev, cur], axis=0)
    return pltpu.roll(ext, k, 0)[SLAB:, :]


def _shift_up(cur, nxt, k):
    ext = jnp.concatenate([cur, nxt], axis=0)
    return pltpu.roll(ext, 2 * SLAB - k, 0)[:SLAB, :]


def _conv3(w_ref, cols, prev, cur):
    s1 = _shift_down(prev, cur, 1)
    s2 = _shift_down(prev, cur, 2)
    y = w_ref[0:1, cols] * s2 + w_ref[1:2, cols] * s1 + w_ref[2:3, cols] * cur
    return y, s1, s2


def _conv3_t(w_ref, cols, cur, nxt):
    return (w_ref[2:3, cols] * cur + w_ref[1:2, cols] * _shift_up(cur, nxt, 1)
            + w_ref[0:1, cols] * _shift_up(cur, nxt, 2))


def _rows(s):
    return pl.ds(pl.multiple_of(s * SLAB, SLAB), SLAB)


def _halo_specs(tm, width, n_tiles):
    per = tm // SLAB
    prev = pl.BlockSpec((SLAB, width), lambda i: (jnp.maximum(i * per - 1, 0), 0))
    nxt = pl.BlockSpec((SLAB, width), lambda i: (jnp.minimum((i + 1) * per, n_tiles * per - 1), 0))
    return prev, nxt


def _add_rows(acc_ref, out_ref, row, cols, first):
    part = jnp.sum(acc_ref[...], axis=0, keepdims=True)

    @pl.when(first)
    def _():
        out_ref[row:row + 1, cols] = part

    @pl.when(jnp.logical_not(first))
    def _():
        out_ref[row:row + 1, cols] += part


def _mixa_fwd(bcx, wc, name):
    s, d3 = bcx.shape
    d = d3 // 3
    tm = _pick(s, 256, SLAB)
    w = _pick(d, 512, LANES)
    nslab = tm // SLAB

    def body(t_ref, prev_ref, wc_ref, y_ref):
        first_tile = pl.program_id(0) == 0
        for c in range(d // w):
            cb, cc, cx = (slice(g * d + c * w, g * d + (c + 1) * w) for g in range(3))
            cols = slice(c * w, (c + 1) * w)
            p_halo = prev_ref[:, cc].astype(F32) * prev_ref[:, cx].astype(F32)
            p_halo = jnp.where(first_tile, 0.0, p_halo)

            def slab(si, p_prev):
                r = _rows(si)
                p = t_ref[r, cc].astype(F32) * t_ref[r, cx].astype(F32)
                cv, _, _ = _conv3(wc_ref, cols, p_prev, p)
                y_ref[r, cols] = (t_ref[r, cb].astype(F32) * cv).astype(BF16)
                return p

            lax.fori_loop(0, nslab, slab, p_halo)

    prev_spec, _ = _halo_specs(tm, d3, s // tm)
    return pl.pallas_call(
        body, name=name, grid=(s // tm,),
        in_specs=[pl.BlockSpec((tm, d3), lambda i: (i, 0)), prev_spec, pl.BlockSpec((3, d), lambda i: (0, 0))],
        out_specs=pl.BlockSpec((tm, d), lambda i: (i, 0)),
        out_shape=jax.ShapeDtypeStruct((s, d), BF16),
        compiler_params=_params(("parallel",)),
    )(bcx, bcx, wc)


def _mixa_bwd(bcx, dy, wc, name):
    s, d3 = bcx.shape
    d = d3 // 3
    tm = _pick(s, 256, SLAB)
    w = _pick(d, 256, LANES)
    nslab = tm // SLAB
    n_tiles = s // tm

    def body(t_ref, prev_ref, next_ref, dy_ref, dyn_ref, wc_ref, o_ref, dwc_ref, a0, a1, a2):
        i = pl.program_id(0)
        first_tile = i == 0
        last_tile = i == n_tiles - 1
        for c in range(d // w):
            cb, cc, cx = (slice(g * d + c * w, g * d + (c + 1) * w) for g in range(3))
            cols = slice(c * w, (c + 1) * w)
            for acc in (a0, a1, a2):
                acc[...] = jnp.zeros_like(acc)
            dcv_next = jnp.where(last_tile, 0.0, dyn_ref[:, cols].astype(F32) * next_ref[:, cb].astype(F32))
            p_halo = jnp.where(first_tile, 0.0, prev_ref[:, cc].astype(F32) * prev_ref[:, cx].astype(F32))

            def one(r, p_prev, dcv_nxt):
                gb = t_ref[r, cb].astype(F32)
                gc = t_ref[r, cc].astype(F32)
                xs = t_ref[r, cx].astype(F32)
                dyv = dy_ref[r, cols].astype(F32)
                p = gc * xs
                cv, s1, s2 = _conv3(wc_ref, cols, p_prev, p)
                dcv = dyv * gb
                a2[...] += dcv * p
                a1[...] += dcv * s1
                a0[...] += dcv * s2
                dp = _conv3_t(wc_ref, cols, dcv, dcv_nxt)
                o_ref[r, cb] = (dyv * cv).astype(BF16)
                o_ref[r, cc] = (dp * xs).astype(BF16)
                o_ref[r, cx] = (dp * gc).astype(BF16)
                return dcv

            def slab(j, dcv_nxt):
                si = nslab - 1 - j
                rp = _rows(si - 1)
                p_prev = t_ref[rp, cc].astype(F32) * t_ref[rp, cx].astype(F32)
                return one(_rows(si), p_prev, dcv_nxt)

            dcv_nxt = lax.fori_loop(0, nslab - 1, slab, dcv_next)
            one(pl.ds(0, SLAB), p_halo, dcv_nxt)
            for k, acc in enumerate((a0, a1, a2)):
                _add_rows(acc, dwc_ref, k, cols, first_tile)

    prev_spec, next_spec = _halo_specs(tm, d3, n_tiles)
    _, next_dy = _halo_specs(tm, d, n_tiles)
    return pl.pallas_call(
        body, name=name, grid=(n_tiles,),
        in_specs=[pl.BlockSpec((tm, d3), lambda i: (i, 0)), prev_spec, next_spec,
                  pl.BlockSpec((tm, d), lambda i: (i, 0)), next_dy, pl.BlockSpec((3, d), lambda i: (0, 0))],
        out_specs=[pl.BlockSpec((tm, d3), lambda i: (i, 0)), pl.BlockSpec((3, d), lambda i: (0, 0))],
        out_shape=[jax.ShapeDtypeStruct((s, d3), BF16), jax.ShapeDtypeStruct((3, d), F32)],
        scratch_shapes=[pltpu.VMEM((SLAB, w), F32)] * 3,
        compiler_params=_params(("arbitrary",)),
    )(bcx, bcx, bcx, dy, dy, wc)


def _sigmoid(z):
    return 0.5 * jnp.tanh(0.5 * z) + 0.5


def _ffn_fwd(up, cw, cb, name):
    s, f2 = up.shape
    f = f2 // 2
    tm = _pick(s, 256, SLAB)
    w = _pick(f, 512, LANES)
    nslab = tm // SLAB

    def body(t_ref, prev_ref, cw_ref, cb_ref, act_ref, cv_ref):
        first_tile = pl.program_id(0) == 0
        for c in range(f // w):
            cg = slice(c * w, (c + 1) * w)
            ca = slice(f + c * w, f + (c + 1) * w)
            halo = tuple(jnp.where(first_tile, 0.0, prev_ref[:, cs].astype(F32)) for cs in (cg, ca))

            def slab(si, carry):
                r = _rows(si)
                g = t_ref[r, cg].astype(F32)
                a = t_ref[r, ca].astype(F32)
                gcv = _conv3(cw_ref, cg, carry[0], g)[0] + cb_ref[:, cg]
                acv = _conv3(cw_ref, ca, carry[1], a)[0] + cb_ref[:, ca]
                cv_ref[r, cg] = gcv.astype(BF16)
                cv_ref[r, ca] = acv.astype(BF16)
                act_ref[r, cg] = (gcv * _sigmoid(gcv) * acv).astype(BF16)
                return g, a

            lax.fori_loop(0, nslab, slab, halo)

    prev_spec, _ = _halo_specs(tm, f2, s // tm)
    return pl.pallas_call(
        body, name=name, grid=(s // tm,),
        in_specs=[pl.BlockSpec((tm, f2), lambda i: (i, 0)), prev_spec,
                  pl.BlockSpec((3, f2), lambda i: (0, 0)), pl.BlockSpec((1, f2), lambda i: (0, 0))],
        out_specs=[pl.BlockSpec((tm, f), lambda i: (i, 0)), pl.BlockSpec((tm, f2), lambda i: (i, 0))],
        out_shape=[jax.ShapeDtypeStruct((s, f), BF16), jax.ShapeDtypeStruct((s, f2), BF16)],
        compiler_params=_params(("parallel",)),
    )(up, up, cw, cb)


def _ffn_bwd(up, cv, dact, cw, name):
    s, f2 = up.shape
    f = f2 // 2
    tm = _pick(s, 128, SLAB)
    w = _pick(f, 256, LANES)
    nslab = tm // SLAB
    n_tiles = s // tm

    def body(t_ref, cv_ref, cvn_ref, da_ref, dan_ref, cw_ref, o_ref, dcw_ref, dcb_ref, acc_ref):
        i = pl.program_id(0)
        last_tile = i == n_tiles - 1

        @pl.when(i == 0)
        def _():
            acc_ref[...] = jnp.zeros_like(acc_ref)

        for c in range(f // w):
            cg = slice(c * w, (c + 1) * w)
            ca = slice(f + c * w, f + (c + 1) * w)

            def dconv(gcv, acv, dav):
                gcv, acv, dav = gcv.astype(F32), acv.astype(F32), dav.astype(F32)
                sg = _sigmoid(gcv)
                return dav * acv * (sg * (1.0 + gcv * (1.0 - sg))), dav * (gcv * sg)

            nxt = dconv(cvn_ref[:, cg], cvn_ref[:, ca], dan_ref[:, cg])
            nxt = tuple(jnp.where(last_tile, 0.0, v) for v in nxt)

            def slab(j, carry):
                r = _rows(nslab - 1 - j)
                d = dconv(cv_ref[r, cg], cv_ref[r, ca], da_ref[r, cg])
                for half, cs in enumerate((cg, ca)):
                    x = t_ref[r, cs].astype(F32)
                    d0 = d[half]
                    d1 = _shift_up(d0, carry[half], 1)
                    d2 = _shift_up(d0, carry[half], 2)
                    o_ref[r, cs] = (cw_ref[2:3, cs] * d0 + cw_ref[1:2, cs] * d1 + cw_ref[0:1, cs] * d2).astype(BF16)
                    for k, term in enumerate((d2 * x, d1 * x, d0 * x, d0)):
                        acc_ref[k, :, cs] += term
                return d

            lax.fori_loop(0, nslab, slab, nxt)

        @pl.when(last_tile)
        def _():
            for k in range(3):
                dcw_ref[k:k + 1, :] = jnp.sum(acc_ref[k], axis=0, keepdims=True)
            dcb_ref[...] = jnp.sum(acc_ref[3], axis=0, keepdims=True)

    _, next_cv = _halo_specs(tm, f2, n_tiles)
    _, next_da = _halo_specs(tm, f, n_tiles)
    return pl.pallas_call(
        body, name=name, grid=(n_tiles,),
        in_specs=[pl.BlockSpec((tm, f2), lambda i: (i, 0)), pl.BlockSpec((tm, f2), lambda i: (i, 0)), next_cv,
                  pl.BlockSpec((tm, f), lambda i: (i, 0)), next_da, pl.BlockSpec((3, f2), lambda i: (0, 0))],
        out_specs=[pl.BlockSpec((tm, f2), lambda i: (i, 0)), pl.BlockSpec((3, f2), lambda i: (0, 0)),
                   pl.BlockSpec((1, f2), lambda i: (0, 0))],
        out_shape=[jax.ShapeDtypeStruct((s, f2), BF16), jax.ShapeDtypeStruct((3, f2), F32),
                   jax.ShapeDtypeStruct((1, f2), F32)],
        scratch_shapes=[pltpu.VMEM((4, SLAB, f2), F32)],
        compiler_params=_params(("arbitrary",)),
    )(up, cv, cv, dact, dact, cw)


_GELU_C = math.sqrt(2.0 / math.pi)


def _gelu(x):
    th = jnp.tanh(_GELU_C * (x + 0.044715 * (x * x * x)))
    return x * (0.5 * (1.0 + th)), th


def _gelu_grad(x, th):
    return 0.5 * (1.0 + th) + 0.5 * x * (1.0 - th * th) * (_GELU_C * (1.0 + 3.0 * 0.044715 * (x * x)))


def _masked_ws(ws_ref, h):
    t = lax.broadcasted_iota(jnp.int32, (CHUNK, CHUNK), 0)
    sx = lax.broadcasted_iota(jnp.int32, (CHUNK, CHUNK), 1)
    return jnp.where(sx <= t, ws_ref[h], 0.0)


def _mixb_fwd(pre, gv, ws, bs_wide, name):
    s, w2 = pre.shape
    w = w2 // 2
    gw = w // SG_GROUPS

    def body(pre_ref, gv_ref, ws_ref, bs_ref, o_ref):
        zu, _ = _gelu(pre_ref[:, :w].astype(F32))
        zv, _ = _gelu(pre_ref[:, w:].astype(F32))
        _, vhat = _rms_stats(zv)
        vn = (vhat * gv_ref[...]).astype(BF16)
        for h in range(SG_GROUPS):
            cols = slice(h * gw, (h + 1) * gw)
            wsm = _masked_ws(ws_ref, h).astype(BF16)
            gate = jnp.dot(wsm, vn[:, cols], preferred_element_type=F32)
            gate = gate + jnp.tile(bs_ref[h], (1, gw // LANES))
            o_ref[:, cols] = (zu[:, cols] * gate).astype(BF16)

    return pl.pallas_call(
        body, name=name, grid=(s // CHUNK,),
        in_specs=[pl.BlockSpec((CHUNK, w2), lambda i: (i, 0)), pl.BlockSpec((1, w), lambda i: (0, 0)),
                  pl.BlockSpec((SG_GROUPS, CHUNK, CHUNK), lambda i: (0, 0, 0)),
                  pl.BlockSpec((SG_GROUPS, CHUNK, LANES), lambda i: (0, 0, 0))],
        out_specs=pl.BlockSpec((CHUNK, w), lambda i: (i, 0)),
        out_shape=jax.ShapeDtypeStruct((s, w), BF16),
        compiler_params=_params(("parallel",)),
    )(pre, gv, ws, bs_wide)


def _mixb_bwd(pre, dug, gv, ws, bs_wide, name):
    s, w2 = pre.shape
    w = w2 // 2
    gw = w // SG_GROUPS

    def body(pre_ref, dug_ref, gv_ref, ws_ref, bs_ref, o_ref, dws_ref, dbs_ref, dgv_ref, dvn_ref):
        first = pl.program_id(0) == 0

        @pl.when(first)
        def _():
            dws_ref[...] = jnp.zeros_like(dws_ref)
            dbs_ref[...] = jnp.zeros_like(dbs_ref)

        pu = pre_ref[:, :w].astype(F32)
        pv = pre_ref[:, w:].astype(F32)
        zu, thu = _gelu(pu)
        zv, thv = _gelu(pv)
        inv, vhat = _rms_stats(zv)
        gvv = gv_ref[...]
        vn = (vhat * gvv).astype(BF16)
        for h in range(SG_GROUPS):
            cols = slice(h * gw, (h + 1) * gw)
            wsm = _masked_ws(ws_ref, h).astype(BF16)
            gate = jnp.dot(wsm, vn[:, cols], preferred_element_type=F32)
            gate = gate + jnp.tile(bs_ref[h], (1, gw // LANES))
            dug_h = dug_ref[:, cols].astype(F32)
            dgate = dug_h * zu[:, cols]
            dgate_b = dgate.astype(BF16)
            o_ref[:, cols] = (dug_h * gate * _gelu_grad(pu[:, cols], thu[:, cols])).astype(BF16)
            dbs_ref[h] += jnp.broadcast_to(jnp.sum(dgate, axis=-1, keepdims=True), (CHUNK, LANES))
            dws = lax.dot_general(dgate_b, vn[:, cols], _DIMS["nt"], preferred_element_type=F32)
            t = lax.broadcasted_iota(jnp.int32, (CHUNK, CHUNK), 0)
            sx = lax.broadcasted_iota(jnp.int32, (CHUNK, CHUNK), 1)
            dws_ref[h] += jnp.where(sx <= t, dws, 0.0)
            dvn_ref[:, cols] = lax.dot_general(wsm, dgate_b, _DIMS["tn"], preferred_element_type=F32)
        dvn = dvn_ref[...]
        part = jnp.sum(dvn * vhat, axis=0, keepdims=True)

        @pl.when(first)
        def _():
            dgv_ref[...] = part

        @pl.when(jnp.logical_not(first))
        def _():
            dgv_ref[...] += part

        dvhat = dvn * gvv
        dzv = inv * (dvhat - vhat * jnp.mean(dvhat * vhat, axis=-1, keepdims=True))
        o_ref[:, w:] = (dzv * _gelu_grad(pv, thv)).astype(BF16)

    return pl.pallas_call(
        body, name=name, grid=(s // CHUNK,),
        in_specs=[pl.BlockSpec((CHUNK, w2), lambda i: (i, 0)), pl.BlockSpec((CHUNK, w), lambda i: (i, 0)),
                  pl.BlockSpec((1, w), lambda i: (0, 0)),
                  pl.BlockSpec((SG_GROUPS, CHUNK, CHUNK), lambda i: (0, 0, 0)),
                  pl.BlockSpec((SG_GROUPS, CHUNK, LANES), lambda i: (0, 0, 0))],
        out_specs=[pl.BlockSpec((CHUNK, w2), lambda i: (i, 0)),
                   pl.BlockSpec((SG_GROUPS, CHUNK, CHUNK), lambda i: (0, 0, 0)),
                   pl.BlockSpec((SG_GROUPS, CHUNK, LANES), lambda i: (0, 0, 0)),
                   pl.BlockSpec((1, w), lambda i: (0, 0))],
        out_shape=[jax.ShapeDtypeStruct((s, w2), BF16), jax.ShapeDtypeStruct((SG_GROUPS, CHUNK, CHUNK), F32),
                   jax.ShapeDtypeStruct((SG_GROUPS, CHUNK, LANES), F32), jax.ShapeDtypeStruct((1, w), F32)],
        scratch_shapes=[pltpu.VMEM((CHUNK, w), F32)],
        compiler_params=_params(("arbitrary",)),
    )(pre, dug, gv, ws, bs_wide)


def _cast_layer(w3, layer, name):
    _, r, c = w3.shape
    tr = _pick(r, 256, SLAB)

    def body(w_ref, o_ref):
        o_ref[...] = w_ref[...].astype(BF16)

    return pl.pallas_call(
        body, name=name, grid=(r // tr,),
        in_specs=[pl.BlockSpec((None, tr, c), lambda i: (layer, i, 0))],
        out_specs=pl.BlockSpec((tr, c), lambda i: (i, 0)),
        out_shape=jax.ShapeDtypeStruct((r, c), BF16),
        compiler_params=_params(("parallel",)),
    )(w3)


def _adamw_math(w, g, m, v):
    m = ADAM_B1 * m + (1.0 - ADAM_B1) * g
    v = ADAM_B2 * v + (1.0 - ADAM_B2) * (g * g)
    m_hat = m / (1.0 - ADAM_B1 ** ADAM_STEP)
    v_hat = v / (1.0 - ADAM_B2 ** ADAM_STEP)
    delta = -ADAM_LR * (m_hat / (jnp.sqrt(v_hat) + ADAM_EPS) + ADAM_WD * w)
    return delta, m, v


def _adamw_sharded(recvs, w, m, v, name):
    nl, r, c = w.shape
    tc = _pick(c, 1536, LANES)
    tr = _pick(r, 64, SLAB)

    def body(*refs):
        recv_refs = refs[:nl]
        w_ref, m_ref, v_ref, g_ref, d_ref, nm_ref, nv_ref = refs[nl:]
        for layer, recv_ref in enumerate(recv_refs):
            @pl.when(pl.program_id(0) == layer)
            def _():
                g = recv_ref[0].astype(F32)
                for q in range(1, N_DEV):
                    g = g + recv_ref[q].astype(F32)
                delta, nm, nv = _adamw_math(w_ref[...], g, m_ref[...], v_ref[...])
                g_ref[...] = g
                d_ref[...] = delta
                nm_ref[...] = nm
                nv_ref[...] = nv

    def recv_spec(layer):
        return pl.BlockSpec((N_DEV, tr, tc),
                            lambda l, i, j: (0, jnp.where(l == layer, i, 0), jnp.where(l == layer, j, 0)))

    blk = pl.BlockSpec((None, tr, tc), lambda l, i, j: (l, i, j))
    out = jax.ShapeDtypeStruct((nl, r, c), F32)
    return pl.pallas_call(
        body, name=name, grid=(nl, r // tr, c // tc),
        in_specs=[recv_spec(layer) for layer in range(nl)] + [blk, blk, blk],
        out_specs=[blk] * 4, out_shape=[out] * 4,
        compiler_params=_params(("parallel",) * 3),
    )(*recvs, w, m, v)


def _adamw_packed(w, g, m, v, name):
    r, c = w.shape
    tr = _pick(r, 256, 8)

    def body(w_ref, g_ref, m_ref, v_ref, d_ref, nm_ref, nv_ref):
        delta, nm, nv = _adamw_math(w_ref[...], g_ref[...], m_ref[...], v_ref[...])
        d_ref[...] = delta
        nm_ref[...] = nm
        nv_ref[...] = nv

    blk = pl.BlockSpec((tr, c), lambda i: (i, 0))
    out = jax.ShapeDtypeStruct((r, c), F32)
    return pl.pallas_call(
        body, name=name, grid=(r // tr,), in_specs=[blk] * 4, out_specs=[blk] * 3, out_shape=[out] * 3,
        compiler_params=_params(("parallel",)),
    )(w, g, m, v)


def _pack(arrays):
    parts = []
    for a in arrays:
        flat = a.reshape(-1).astype(F32)
        pad = (-flat.shape[0]) % PACK_GRANULE
        parts.append(jnp.pad(flat, (0, pad)) if pad else flat)
    return jnp.concatenate(parts).reshape(-1, LANES)


def _unpack(buf, shapes):
    flat = buf.reshape(-1)
    out, off = [], 0
    for shp in shapes:
        n = math.prod(shp)
        out.append(flat[off:off + n].reshape(shp))
        off += n + (-n) % PACK_GRANULE
    return out


def _mesh_pos():
    return lax.axis_index("x"), lax.axis_index("y"), lax.axis_index("c")


def _coords(q):
    return q // 4, (q // 2) % 2, q % 2


def _shard_of(ref, q, shard_shape, axis):
    r, c = shard_shape
    if axis == 0:
        return ref.at[pl.ds(pl.multiple_of(q * r, SLAB), r), :]
    return ref.at[:, pl.ds(pl.multiple_of(q * c, LANES), c)]


_HBM = pl.BlockSpec(memory_space=pltpu.HBM)
_SEM = pl.BlockSpec(memory_space=pltpu.SEMAPHORE)
_EFFECT = pltpu.SideEffectType.DATAFLOW_SIDE_EFFECTING


def _exchange_shapes(gather, src_shape, axis):
    r, c = src_shape
    if gather:
        return (r, c), ((r * N_DEV, c) if axis == 0 else (r, c * N_DEV))
    shard = (r // N_DEV, c) if axis == 0 else (r, c // N_DEV)
    return shard, (N_DEV,) + shard


def _exchange_copies(gather, src, land, sems, axis):
    send_sems, recv_sems, own_sem = sems
    x, y, c_ = _mesh_pos()
    me = 4 * x + 2 * y + c_
    shard, _ = _exchange_shapes(gather, src.shape, axis)

    def piece(q):
        return src if gather else _shard_of(src, q, shard, axis)

    def place(q):
        return _shard_of(land, q, shard, axis) if gather else land.at[q]

    own = pltpu.make_async_copy(piece(me), place(me), own_sem.at[0])
    sends, arrivals = [], []
    for step in range(1, N_DEV):
        to = (me + step) % N_DEV
        frm = (me + N_DEV - step) % N_DEV
        sends.append(pltpu.make_async_remote_copy(
            src_ref=piece(to), dst_ref=place(me), send_sem=send_sems.at[step - 1], recv_sem=recv_sems.at[step - 1],
            device_id=_coords(to), device_id_type=MESH))
        arrivals.append(pltpu.make_async_remote_copy(
            src_ref=piece(me), dst_ref=place(frm), send_sem=send_sems.at[step - 1], recv_sem=recv_sems.at[step - 1],
            device_id=_coords(frm), device_id_type=MESH))
    return own, sends, arrivals


def _exchange_start(gather, src, axis, name, after=None):
    _, land_shape = _exchange_shapes(gather, src.shape, axis)
    extra = () if after is None else (after,)

    def body(*refs):
        src_ref, land = refs[:2]
        send_sems, recv_sems, own_sem = refs[2 + len(extra):5 + len(extra)]
        own, sends, _ = _exchange_copies(gather, src_ref, land, (send_sems, recv_sems, own_sem), axis)
        own.start()
        for cp in sends:
            cp.start()
        refs[-1][...] = jnp.zeros_like(refs[-1])

    out = pl.pallas_call(
        body, name=name,
        out_shape=(pltpu.SemaphoreType.DMA((N_DEV - 1,)), pltpu.SemaphoreType.DMA((N_DEV - 1,)),
                   pltpu.SemaphoreType.DMA((1,)), pltpu.HBM(src.shape, src.dtype),
                   pltpu.HBM(land_shape, src.dtype), jax.ShapeDtypeStruct((8, LANES), F32)),
        in_specs=[_HBM, _HBM] + [pl.BlockSpec(memory_space=pl.ANY)] * len(extra),
        out_specs=(_SEM, _SEM, _SEM, _HBM, _HBM, pl.BlockSpec(memory_space=pltpu.VMEM)),
        input_output_aliases={0: 3, 1: 4},
        compiler_params=pltpu.CompilerParams(has_side_effects=_EFFECT),
    )(pltpu.with_memory_space_constraint(src, pltpu.HBM),
      pltpu.with_memory_space_constraint(lax.empty(land_shape, src.dtype), pltpu.HBM), *extra)
    return out[:5], out[5]


def _exchange_wait(gather, state, axis, after, name):
    send_sems, recv_sems, own_sem, src_thru, land_thru = state

    def body(src, land, send_sems, recv_sems, own_sem, after_ref, src_dead, got):
        own, sends, arrivals = _exchange_copies(gather, src, land, (send_sems, recv_sems, own_sem), axis)
        for cp in sends:
            cp.wait_send()
        for cp in arrivals:
            cp.wait_recv()
        own.wait()

    return pl.pallas_call(
        body, name=name,
        out_shape=(pltpu.HBM(src_thru.shape, src_thru.dtype), pltpu.HBM(land_thru.shape, land_thru.dtype)),
        in_specs=[_HBM, _HBM, _SEM, _SEM, _SEM, pl.BlockSpec(memory_space=pl.ANY)],
        out_specs=(_HBM, _HBM),
        input_output_aliases={0: 0, 1: 1},
        compiler_params=pltpu.CompilerParams(has_side_effects=_EFFECT),
    )(src_thru, land_thru, send_sems, recv_sems, own_sem, after)[1]


def _all_reduce_small(part, name):
    r, c = part.shape

    def body(part_ref, out_ref, slots, send_sems, recv_sems, local_sem):
        x, y, cc = _mesh_pos()
        me = 4 * x + 2 * y + cc
        own = pltpu.make_async_copy(part_ref, slots.at[me], local_sem)
        own.start()
        sends = []
        for step in range(1, N_DEV):
            to = (me + step) % N_DEV
            cp = pltpu.make_async_remote_copy(
                src_ref=part_ref, dst_ref=slots.at[me], send_sem=send_sems.at[step - 1],
                recv_sem=recv_sems.at[step - 1], device_id=_coords(to), device_id_type=MESH)
            cp.start()
            sends.append(cp)
        for step in range(1, N_DEV):
            frm = (me + N_DEV - step) % N_DEV
            pltpu.make_async_remote_copy(
                src_ref=part_ref, dst_ref=slots.at[frm], send_sem=send_sems.at[step - 1],
                recv_sem=recv_sems.at[step - 1], device_id=_coords(frm), device_id_type=MESH).wait_recv()
        for cp in sends:
            cp.wait_send()
        own.wait()
        total = slots[0]
        for q in range(1, N_DEV):
            total = total + slots[q]
        out_ref[...] = total

    vmem = pl.BlockSpec(memory_space=pltpu.VMEM)
    return pl.pallas_call(
        body, name=name, in_specs=[vmem], out_specs=vmem,
        out_shape=jax.ShapeDtypeStruct((r, c), F32),
        scratch_shapes=[pltpu.VMEM((N_DEV, r, c), F32), pltpu.SemaphoreType.DMA((N_DEV - 1,)),
                        pltpu.SemaphoreType.DMA((N_DEV - 1,)), pltpu.SemaphoreType.DMA],
        compiler_params=pltpu.CompilerParams(has_side_effects=True, vmem_limit_bytes=VMEM_LIMIT),
    )(part)


def _gather_cols(full_rows, n_rows, shard_cols):
    return full_rows.reshape(N_DEV, n_rows, shard_cols).transpose(1, 0, 2).reshape(n_rows, N_DEV * shard_cols)


def kernel(x, a_norm, a_in, a_conv, a_out, b_norm, b_in, b_vnorm, b_ws, b_bs, b_out, f_norm, f_up, f_conv_w, f_conv_b, f_down, final_norm, loss_target, m_a_norm, m_a_in, m_a_conv, m_a_out, m_b_norm, m_b_in, m_b_vnorm, m_b_ws, m_b_bs, m_b_out, m_f_norm, m_f_up, m_f_conv_w, m_f_conv_b, m_f_down, m_final_norm, v_a_norm, v_a_in, v_a_conv, v_a_out, v_b_norm, v_b_in, v_b_vnorm, v_b_ws, v_b_bs, v_b_out, v_f_norm, v_f_up, v_f_conv_w, v_f_conv_b, v_f_down, v_final_norm):
    s, d = x.shape[1], x.shape[2]
    n_ffn = f_up.shape[0]
    f2 = f_up.shape[2] * N_DEV
    me = 4 * lax.axis_index("x") + 2 * lax.axis_index("y") + lax.axis_index("c")
    x0 = x.reshape(s, d)
    target = loss_target.reshape(s, d)

    wanted = [("a_in", _cast_layer(a_in, 0, "cast_a_in"), 1),
              ("small", _pack([a_conv, b_norm, b_vnorm, f_conv_w]), 0),
              ("a_out", _cast_layer(a_out, 0, "cast_a_out"), 0),
              ("f_up0", _cast_layer(f_up, 0, "cast_f_up0"), 1), ("f_down0", _cast_layer(f_down, 0, "cast_f_down0"), 0),
              ("b_in", _cast_layer(b_in, 0, "cast_b_in"), 1), ("b_out", _cast_layer(b_out, 0, "cast_b_out"), 0),
              ("f_up1", _cast_layer(f_up, 1, "cast_f_up1"), 1), ("f_down1", _cast_layer(f_down, 1, "cast_f_down1"), 0)]
    coming, tok = {}, None
    for key, shard, axis in wanted:
        state, tok = _exchange_start(True, shard, axis, f"ag_start_{key}", after=tok)
        coming[key] = (state, axis)

    def arrived(key, after):
        state, axis = coming[key]
        return _exchange_wait(True, state, axis, after, f"ag_wait_{key}")

    cshard = a_conv.shape[2]
    fshard = f_conv_w.shape[2]
    h0 = _rmsnorm_fwd(x0, a_norm, "mixa_norm", after=tok)
    w_a_in = arrived("a_in", h0)
    small_rows = arrived("small", w_a_in).reshape(N_DEV, -1)
    per_dev = _unpack_rows(small_rows, [(3, cshard), (cshard,), (cshard,), (n_ffn, 3, fshard)])
    a_conv_full = per_dev[0].transpose(1, 0, 2).reshape(3, d)
    b_norm_full = per_dev[1].reshape(1, d)
    b_vnorm_full = per_dev[2].reshape(1, d)
    f_conv_w_full = per_dev[3].transpose(1, 2, 0, 3).reshape(n_ffn, 3, f2)
    bs_wide = jnp.broadcast_to(b_bs[0][:, :, None], (SG_GROUPS, CHUNK, LANES))
    ws = b_ws[0]

    w_f_up, w_f_down = {}, {}

    def ffn_forward(xin, l):
        h = _rmsnorm_fwd(xin, f_norm[l:l + 1], f"ffn{l}_norm")
        w_f_up[l] = arrived(f"f_up{l}", h)
        up = _matmul(h, w_f_up[l], "nn", BF16, f"ffn{l}_up")
        act, cv = _ffn_fwd(up, f_conv_w_full[l], f_conv_b[l:l + 1], f"ffn{l}_mid")
        w_f_down[l] = arrived(f"f_down{l}", act)
        xout = _matmul(act, w_f_down[l], "nn", F32, f"ffn{l}_down", resid=xin, tm_cap=512)
        return xout, (h, up, act, cv)

    bcx = _matmul(h0, w_a_in, "nn", BF16, "mixa_in")
    ya = _mixa_fwd(bcx, a_conv_full, "mixa_mid")
    w_a_out = arrived("a_out", ya)
    x1 = _matmul(ya, w_a_out, "nn", F32, "mixa_out", resid=x0)
    x2, saved0 = ffn_forward(x1, 0)
    h2 = _rmsnorm_fwd(x2, b_norm_full, "mixb_norm")
    w_b_in = arrived("b_in", h2)
    pre = _matmul(h2, w_b_in, "nn", BF16, "mixb_in")
    ug = _mixb_fwd(pre, b_vnorm_full, ws, bs_wide, "mixb_mid")
    w_b_out = arrived("b_out", ug)
    x3 = _matmul(ug, w_b_out, "nn", F32, "mixb_out", resid=x2)
    x4, saved1 = ffn_forward(x3, 1)
    dx4, dx4b, loss_part, g_final = _final_loss(x4, final_norm.reshape(1, d), target, "loss_head")

    def _rs_start(grad, axis, name):
        return _exchange_start(False, grad, axis, name)

    def ffn_backward(xin, l, saved, dx, dxb):
        h, up, act, cv = saved
        g_down = _matmul(act, dxb, "tn", BF16, f"ffn{l}_down_dw", tm_cap=1408)
        rs_down, tok = _rs_start(g_down, 0, f"rs_start_f_down{l}")
        dact = _matmul(dxb, w_f_down[l], "nt", BF16, f"ffn{l}_down_dx", after=tok, tn_cap=1408)
        dup, g_cw, g_cb = _ffn_bwd(up, cv, dact, f_conv_w_full[l], f"ffn{l}_mid_bwd")
        g_up = _matmul(h, dup, "tn", BF16, f"ffn{l}_up_dw")
        rs_up, tok = _rs_start(g_up, 1, f"rs_start_f_up{l}")
        dh = _matmul(dup, w_f_up[l], "nt", F32, f"ffn{l}_up_dx", after=tok)
        dxin, dxinb, g_norm = _rmsnorm_bwd(xin, f_norm[l:l + 1], dh, dx, f"ffn{l}_norm_bwd")
        return dxin, dxinb, (rs_up, rs_down, g_cw, g_cb, g_norm)

    dx3, dx3b, gf1 = ffn_backward(x3, 1, saved1, dx4, dx4b)
    g_b_out = _matmul(ug, dx3b, "tn", BF16, "mixb_out_dw")
    rs_b_out, tok = _rs_start(g_b_out, 0, "rs_start_b_out")
    dug = _matmul(dx3b, w_b_out, "nt", BF16, "mixb_out_dx", after=tok)
    dpre, g_ws, g_bs_wide, g_bvnorm = _mixb_bwd(pre, dug, b_vnorm_full, ws, bs_wide, "mixb_mid_bwd")
    g_b_in = _matmul(h2, dpre, "tn", BF16, "mixb_in_dw")
    rs_b_in, tok = _rs_start(g_b_in, 1, "rs_start_b_in")
    dh2 = _matmul(dpre, w_b_in, "nt", F32, "mixb_in_dx", after=tok)
    dx2, dx2b, g_bnorm = _rmsnorm_bwd(x2, b_norm_full, dh2, dx3, "mixb_norm_bwd")
    dx1, dx1b, gf0 = ffn_backward(x1, 0, saved0, dx2, dx2b)
    g_a_out = _matmul(ya, dx1b, "tn", BF16, "mixa_out_dw")
    rs_a_out, tok = _rs_start(g_a_out, 0, "rs_start_a_out")
    dya = _matmul(dx1b, w_a_out, "nt", BF16, "mixa_out_dx", after=tok)
    dbcx, g_aconv = _mixa_bwd(bcx, dya, a_conv_full, "mixa_mid_bwd")
    g_a_in = _matmul(h0, dbcx, "tn", BF16, "mixa_in_dw")
    rs_a_in, tok = _rs_start(g_a_in, 1, "rs_start_a_in")
    dh0 = _matmul(dbcx, w_a_in, "nt", F32, "mixa_in_dx", after=tok)
    grad_x, _, g_anorm = _rmsnorm_bwd(x0, a_norm, dh0, dx1, "mixa_norm_bwd")

    big = {}
    for name, states, axis, w, m, v in (
            ("f_down", (gf0[1], gf1[1]), 0, f_down, m_f_down, v_f_down),
            ("f_up", (gf0[0], gf1[0]), 1, f_up, m_f_up, v_f_up),
            ("b_out", (rs_b_out,), 0, b_out, m_b_out, v_b_out), ("b_in", (rs_b_in,), 1, b_in, m_b_in, v_b_in),
            ("a_out", (rs_a_out,), 0, a_out, m_a_out, v_a_out), ("a_in", (rs_a_in,), 1, a_in, m_a_in, v_a_in)):
        recvs = [_exchange_wait(False, st, axis, grad_x, f"rs_wait_{name}{l}") for l, st in enumerate(states)]
        big[name] = _adamw_sharded(recvs, w, m, v, f"adamw_{name}")

    full_shapes = [(1, LANES), (1, d), (3, d), (1, d), (1, d), (SG_GROUPS, CHUNK, CHUNK), (SG_GROUPS, CHUNK),
                   (n_ffn, d), (n_ffn, 3, f2), (n_ffn, f2), (1, d)]
    parts = [loss_part, g_anorm, g_aconv, g_bnorm, g_bvnorm, g_ws, g_bs_wide[:, :, 0],
             jnp.concatenate([gf0[4], gf1[4]], axis=0), jnp.stack([gf0[2], gf1[2]]),
             jnp.concatenate([gf0[3], gf1[3]], axis=0), g_final]
    total = _all_reduce_small(_pack(parts), "all_reduce_small")
    (loss_v, r_anorm, r_aconv, r_bnorm, r_bvnorm, r_ws, r_bs, r_fnorm, r_fcw, r_fcb, r_final) = _unpack(total, full_shapes)
    small_grads = [
        r_anorm,
        lax.dynamic_slice_in_dim(r_aconv, me * cshard, cshard, axis=1).reshape(a_conv.shape),
        lax.dynamic_slice_in_dim(r_bnorm, me * cshard, cshard, axis=1),
        lax.dynamic_slice_in_dim(r_bvnorm, me * cshard, cshard, axis=1),
        r_ws.reshape(b_ws.shape), r_bs.reshape(b_bs.shape), r_fnorm,
        lax.dynamic_slice_in_dim(r_fcw, me * fshard, fshard, axis=2),
        r_fcb, r_final.reshape(final_norm.shape)]
    small_w = [a_norm, a_conv, b_norm, b_vnorm, b_ws, b_bs, f_norm, f_conv_w, f_conv_b, final_norm]
    small_m = [m_a_norm, m_a_conv, m_b_norm, m_b_vnorm, m_b_ws, m_b_bs, m_f_norm, m_f_conv_w, m_f_conv_b, m_final_norm]
    small_v = [v_a_norm, v_a_conv, v_b_norm, v_b_vnorm, v_b_ws, v_b_bs, v_f_norm, v_f_conv_w, v_f_conv_b, v_final_norm]
    shapes = [w.shape for w in small_w]
    packed = _adamw_packed(_pack(small_w), _pack(small_grads), _pack(small_m), _pack(small_v), "adamw_small")
    s_delta, s_m, s_v = (_unpack(p, shapes) for p in packed)
    small_names = ["a_norm", "a_conv", "b_norm", "b_vnorm", "b_ws", "b_bs", "f_norm", "f_conv_w", "f_conv_b", "final_norm"]
    small = {nm: (small_grads[i], s_delta[i], s_m[i], s_v[i]) for i, nm in enumerate(small_names)}

    order = ["a_norm", "a_in", "a_conv", "a_out", "b_norm", "b_in", "b_vnorm", "b_ws", "b_bs", "b_out",
             "f_norm", "f_up", "f_conv_w", "f_conv_b", "f_down", "final_norm"]
    res = {nm: (big[nm] if nm in big else small[nm]) for nm in order}
    outs = [loss_v[0, 0], grad_x.reshape(x.shape)]
    for k in range(4):
        outs += [res[nm][k] for nm in order]
    return tuple(outs)


def _unpack_rows(rows, shapes):
    out, off = [], 0
    for shp in shapes:
        n = math.prod(shp)
        out.append(rows[:, off:off + n].reshape((N_DEV,) + tuple(shp)))
        off += n + (-n) % PACK_GRANULE
    return out
```

```python
import functools
import math

import jax
import jax.numpy as jnp
from jax import lax
from jax.experimental import pallas as pl
from jax.experimental.pallas import tpu as pltpu

F32 = jnp.float32
BF16 = jnp.bfloat16
MESH = pl.DeviceIdType.MESH

N_DEV = 8
RMS_EPS = 1e-5
CHUNK = 128
SG_GROUPS = 8
ADAM_LR = 0.001
ADAM_B1 = 0.9
ADAM_B2 = 0.999
ADAM_EPS = 1e-08
ADAM_WD = 0.01
ADAM_STEP = 10

LANES = 128
SLAB = 16
VMEM_LIMIT = 48 * 1024 * 1024
PACK_GRANULE = 8 * LANES


def _pick(dim, cap, mult):
    best = None
    t = mult
    while t <= min(dim, cap):
        if dim % t == 0:
            best = t
        t += mult
    return dim if best is None else best


def _params(semantics=None):
    return pltpu.CompilerParams(dimension_semantics=semantics, vmem_limit_bytes=VMEM_LIMIT)


_DIMS = {
    "nn": (((1,), (0,)), ((), ())),
    "nt": (((1,), (1,)), ((), ())),
    "tn": (((0,), (0,)), ((), ())),
}


def _matmul(a, b, mode, out_dtype, name, resid=None, after=None, tm_cap=1024, tn_cap=1024, tk_cap=2816):
    if mode == "nn":
        (m, k), n = a.shape, b.shape[1]
    elif mode == "nt":
        (m, k), n = a.shape, b.shape[0]
    else:
        (k, m), n = a.shape, b.shape[1]
    tm, tn, tk = _pick(m, tm_cap, LANES), _pick(n, tn_cap, LANES), _pick(k, tk_cap, LANES)
    nk = k // tk
    n_in = 2 + (resid is not None) + (after is not None)

    def body(*refs):
        a_ref, b_ref = refs[:2]
        r_ref = refs[2] if resid is not None else None
        o_ref = refs[n_in]
        prod = lax.dot_general(a_ref[...], b_ref[...], _DIMS[mode], preferred_element_type=F32)

        def finish(r):
            if r_ref is not None:
                r = r + r_ref[...]
            o_ref[...] = r.astype(out_dtype)

        if nk == 1:
            finish(prod)
            return
        acc_ref = refs[n_in + 1]
        kk = pl.program_id(2)

        @pl.when(kk == 0)
        def _():
            acc_ref[...] = prod

        @pl.when(jnp.logical_and(kk > 0, kk < nk - 1))
        def _():
            acc_ref[...] += prod

        @pl.when(kk == nk - 1)
        def _():
            finish(acc_ref[...] + prod)

    a_spec = (pl.BlockSpec((tk, tm), lambda i, j, kk: (kk, i)) if mode == "tn"
              else pl.BlockSpec((tm, tk), lambda i, j, kk: (i, kk)))
    b_spec = (pl.BlockSpec((tn, tk), lambda i, j, kk: (j, kk)) if mode == "nt"
              else pl.BlockSpec((tk, tn), lambda i, j, kk: (kk, j)))
    o_spec = pl.BlockSpec((tm, tn), lambda i, j, kk: (i, j))
    in_specs = [a_spec, b_spec] + ([o_spec] if resid is not None else [])
    args = (a, b) + ((resid,) if resid is not None else ())
    if after is not None:
        in_specs.append(pl.BlockSpec(memory_space=pl.ANY))
        args += (after,)
    return pl.pallas_call(
        body, name=name, grid=(m // tm, n // tn, nk),
        in_specs=in_specs, out_specs=o_spec,
        out_shape=jax.ShapeDtypeStruct((m, n), out_dtype),
        scratch_shapes=[pltpu.VMEM((tm, tn), F32)] if nk > 1 else [],
        compiler_params=_params(("parallel", "parallel", "arbitrary")),
    )(*args)


def _rms_stats(xf):
    inv = lax.rsqrt(jnp.mean(xf * xf, axis=-1, keepdims=True) + RMS_EPS)
    return inv, xf * inv


def _rmsnorm_fwd(x, g, name, after=None):
    s, d = x.shape
    tm = _pick(s, 256, SLAB)
    extra = () if after is None else (after,)

    def body(x_ref, g_ref, *rest):
        _, xhat = _rms_stats(x_ref[...])
        rest[-1][...] = (xhat * g_ref[...]).astype(BF16)

    return pl.pallas_call(
        body, name=name, grid=(s // tm,),
        in_specs=[pl.BlockSpec((tm, d), lambda i: (i, 0)), pl.BlockSpec((1, d), lambda i: (0, 0))]
        + [pl.BlockSpec(memory_space=pl.ANY)] * len(extra),
        out_specs=pl.BlockSpec((tm, d), lambda i: (i, 0)),
        out_shape=jax.ShapeDtypeStruct((s, d), BF16),
        compiler_params=_params(("parallel",)),
    )(x, g, *extra)


def _rmsnorm_bwd(x, g, dh, dx_out, name):
    s, d = x.shape
    tm = _pick(s, 256, SLAB)

    def body(x_ref, g_ref, dh_ref, dxo_ref, dxi_ref, dxib_ref, dg_ref):
        inv, xhat = _rms_stats(x_ref[...])
        dhv = dh_ref[...]
        dxhat = dhv * g_ref[...]
        proj = jnp.mean(dxhat * xhat, axis=-1, keepdims=True)
        dx = dxo_ref[...] + inv * (dxhat - xhat * proj)
        dxi_ref[...] = dx
        dxib_ref[...] = dx.astype(BF16)
        part = jnp.sum(dhv * xhat, axis=0, keepdims=True)

        @pl.when(pl.program_id(0) == 0)
        def _():
            dg_ref[...] = part

        @pl.when(pl.program_id(0) > 0)
        def _():
            dg_ref[...] += part

    row = pl.BlockSpec((tm, d), lambda i: (i, 0))
    vec = pl.BlockSpec((1, d), lambda i: (0, 0))
    return pl.pallas_call(
        body, name=name, grid=(s // tm,),
        in_specs=[row, vec, row, row], out_specs=[row, row, vec],
        out_shape=[jax.ShapeDtypeStruct((s, d), F32), jax.ShapeDtypeStruct((s, d), BF16),
                   jax.ShapeDtypeStruct((1, d), F32)],
        compiler_params=_params(("arbitrary",)),
    )(x, g, dh, dx_out)


def _final_loss(x, g, target, name):
    s, d = x.shape
    tm = _pick(s, 256, SLAB)

    def body(x_ref, g_ref, t_ref, dx_ref, dxb_ref, loss_ref, dg_ref):
        inv, xhat = _rms_stats(x_ref[...])
        gv = g_ref[...]
        err = xhat * gv - t_ref[...]
        loss = 0.5 * jnp.sum(jnp.mean(err * err, axis=-1, keepdims=True), axis=0, keepdims=True)
        dy = err * (1.0 / d)
        dxhat = dy * gv
        proj = jnp.mean(dxhat * xhat, axis=-1, keepdims=True)
        dx = inv * (dxhat - xhat * proj)
        dx_ref[...] = dx
        dxb_ref[...] = dx.astype(BF16)
        part = jnp.sum(dy * xhat, axis=0, keepdims=True)
        loss_row = jnp.broadcast_to(loss, (1, LANES))

        @pl.when(pl.program_id(0) == 0)
        def _():
            dg_ref[...] = part
            loss_ref[...] = loss_row

        @pl.when(pl.program_id(0) > 0)
        def _():
            dg_ref[...] += part
            loss_ref[...] += loss_row

    row = pl.BlockSpec((tm, d), lambda i: (i, 0))
    vec = pl.BlockSpec((1, d), lambda i: (0, 0))
    return pl.pallas_call(
        body, name=name, grid=(s // tm,),
        in_specs=[row, vec, row],
        out_specs=[row, row, pl.BlockSpec((1, LANES), lambda i: (0, 0)), vec],
        out_shape=[jax.ShapeDtypeStruct((s, d), F32), jax.ShapeDtypeStruct((s, d), BF16),
                   jax.ShapeDtypeStruct((1, LANES), F32), jax.ShapeDtypeStruct((1, d), F32)],
        compiler_params=_params(("arbitrary",)),
    )(x, g, target)


def _shift_down(prev, cur, k):
    ext = jnp.concatenate([prev, cur], axis=0)
    return pltpu.roll(ext, k, 0)[SLAB:, :]


def _shift_up(cur, nxt, k):
    ext = jnp.concatenate([cur, nxt], axis=0)
    return pltpu.roll(ext, 2 * SLAB - k, 0)[:SLAB, :]


def _conv3(w_ref, cols, prev, cur):
    s1 = _shift_down(prev, cur, 1)
    s2 = _shift_down(prev, cur, 2)
    y = w_ref[0:1, cols] * s2 + w_ref[1:2, cols] * s1 + w_ref[2:3, cols] * cur
    return y, s1, s2


def _conv3_t(w_ref, cols, cur, nxt):
    return (w_ref[2:3, cols] * cur + w_ref[1:2, cols] * _shift_up(cur, nxt, 1)
            + w_ref[0:1, cols] * _shift_up(cur, nxt, 2))


def _rows(s):
    return pl.ds(pl.multiple_of(s * SLAB, SLAB), SLAB)


def _halo_specs(tm, width, n_tiles):
    per = tm // SLAB
    prev = pl.BlockSpec((SLAB, width), lambda i: (jnp.maximum(i * per - 1, 0), 0))
    nxt = pl.BlockSpec((SLAB, width), lambda i: (jnp.minimum((i + 1) * per, n_tiles * per - 1), 0))
    return prev, nxt


def _add_rows(acc_ref, out_ref, row, cols, first):
    part = jnp.sum(acc_ref[...], axis=0, keepdims=True)

    @pl.when(first)
    def _():
        out_ref[row:row + 1, cols] = part

    @pl.when(jnp.logical_not(first))
    def _():
        out_ref[row:row + 1, cols] += part


def _mixa_fwd(bcx, wc, name):
    s, d3 = bcx.shape
    d = d3 // 3
    tm = _pick(s, 256, SLAB)
    w = _pick(d, 512, LANES)
    nslab = tm // SLAB

    def body(t_ref, prev_ref, wc_ref, y_ref):
        first_tile = pl.program_id(0) == 0
        for c in range(d // w):
            cb, cc, cx = (slice(g * d + c * w, g * d + (c + 1) * w) for g in range(3))
            cols = slice(c * w, (c + 1) * w)
            p_halo = prev_ref[:, cc].astype(F32) * prev_ref[:, cx].astype(F32)
            p_halo = jnp.where(first_tile, 0.0, p_halo)

            def slab(si, p_prev):
                r = _rows(si)
                p = t_ref[r, cc].astype(F32) * t_ref[r, cx].astype(F32)
                cv, _, _ = _conv3(wc_ref, cols, p_prev, p)
                y_ref[r, cols] = (t_ref[r, cb].astype(F32) * cv).astype(BF16)
                return p

            lax.fori_loop(0, nslab, slab, p_halo)

    prev_spec, _ = _halo_specs(tm, d3, s // tm)
    return pl.pallas_call(
        body, name=name, grid=(s // tm,),
        in_specs=[pl.BlockSpec((tm, d3), lambda i: (i, 0)), prev_spec, pl.BlockSpec((3, d), lambda i: (0, 0))],
        out_specs=pl.BlockSpec((tm, d), lambda i: (i, 0)),
        out_shape=jax.ShapeDtypeStruct((s, d), BF16),
        compiler_params=_params(("parallel",)),
    )(bcx, bcx, wc)


def _mixa_bwd(bcx, dy, wc, name):
    s, d3 = bcx.shape
    d = d3 // 3
    tm = _pick(s, 256, SLAB)
    w = _pick(d, 256, LANES)
    nslab = tm // SLAB
    n_tiles = s // tm

    def body(t_ref, prev_ref, next_ref, dy_ref, dyn_ref, wc_ref, o_ref, dwc_ref, a0, a1, a2):
        i = pl.program_id(0)
        first_tile = i == 0
        last_tile = i == n_tiles - 1
        for c in range(d // w):
            cb, cc, cx = (slice(g * d + c * w, g * d + (c + 1) * w) for g in range(3))
            cols = slice(c * w, (c + 1) * w)
            for acc in (a0, a1, a2):
                acc[...] = jnp.zeros_like(acc)
            dcv_next = jnp.where(last_tile, 0.0, dyn_ref[:, cols].astype(F32) * next_ref[:, cb].astype(F32))
            p_halo = jnp.where(first_tile, 0.0, prev_ref[:, cc].astype(F32) * prev_ref[:, cx].astype(F32))

            def one(r, p_prev, dcv_nxt):
                gb = t_ref[r, cb].astype(F32)
                gc = t_ref[r, cc].astype(F32)
                xs = t_ref[r, cx].astype(F32)
                dyv = dy_ref[r, cols].astype(F32)
                p = gc * xs
                cv, s1, s2 = _conv3(wc_ref, cols, p_prev, p)
                dcv = dyv * gb
                a2[...] += dcv * p
                a1[...] += dcv * s1
                a0[...] += dcv * s2
                dp = _conv3_t(wc_ref, cols, dcv, dcv_nxt)
                o_ref[r, cb] = (dyv * cv).astype(BF16)
                o_ref[r, cc] = (dp * xs).astype(BF16)
                o_ref[r, cx] = (dp * gc).astype(BF16)
                return dcv

            def slab(j, dcv_nxt):
                si = nslab - 1 - j
                rp = _rows(si - 1)
                p_prev = t_ref[rp, cc].astype(F32) * t_ref[rp, cx].astype(F32)
                return one(_rows(si), p_prev, dcv_nxt)

            dcv_nxt = lax.fori_loop(0, nslab - 1, slab, dcv_next)
            one(pl.ds(0, SLAB), p_halo, dcv_nxt)
            for k, acc in enumerate((a0, a1, a2)):
                _add_rows(acc, dwc_ref, k, cols, first_tile)

    prev_spec, next_spec = _halo_specs(tm, d3, n_tiles)
    _, next_dy = _halo_specs(tm, d, n_tiles)
    return pl.pallas_call(
        body, name=name, grid=(n_tiles,),
        in_specs=[pl.BlockSpec((tm, d3), lambda i: (i, 0)), prev_spec, next_spec,
                  pl.BlockSpec((tm, d), lambda i: (i, 0)), next_dy, pl.BlockSpec((3, d), lambda i: (0, 0))],
        out_specs=[pl.BlockSpec((tm, d3), lambda i: (i, 0)), pl.BlockSpec((3, d), lambda i: (0, 0))],
        out_shape=[jax.ShapeDtypeStruct((s, d3), BF16), jax.ShapeDtypeStruct((3, d), F32)],
        scratch_shapes=[pltpu.VMEM((SLAB, w), F32)] * 3,
        compiler_params=_params(("arbitrary",)),
    )(bcx, bcx, bcx, dy, dy, wc)


def _sigmoid(z):
    return 0.5 * jnp.tanh(0.5 * z) + 0.5


def _ffn_fwd(up, cw, cb, name):
    s, f2 = up.shape
    f = f2 // 2
    tm = _pick(s, 256, SLAB)
    w = _pick(f, 512, LANES)
    nslab = tm // SLAB

    def body(t_ref, prev_ref, cw_ref, cb_ref, act_ref, cv_ref):
        first_tile = pl.program_id(0) == 0
        for c in range(f // w):
            cg = slice(c * w, (c + 1) * w)
            ca = slice(f + c * w, f + (c + 1) * w)
            halo = tuple(jnp.where(first_tile, 0.0, prev_ref[:, cs].astype(F32)) for cs in (cg, ca))

            def slab(si, carry):
                r = _rows(si)
                g = t_ref[r, cg].astype(F32)
                a = t_ref[r, ca].astype(F32)
                gcv = _conv3(cw_ref, cg, carry[0], g)[0] + cb_ref[:, cg]
                acv = _conv3(cw_ref, ca, carry[1], a)[0] + cb_ref[:, ca]
                cv_ref[r, cg] = gcv.astype(BF16)
                cv_ref[r, ca] = acv.astype(BF16)
                act_ref[r, cg] = (gcv * _sigmoid(gcv) * acv).astype(BF16)
                return g, a

            lax.fori_loop(0, nslab, slab, halo)

    prev_spec, _ = _halo_specs(tm, f2, s // tm)
    return pl.pallas_call(
        body, name=name, grid=(s // tm,),
        in_specs=[pl.BlockSpec((tm, f2), lambda i: (i, 0)), prev_spec,
                  pl.BlockSpec((3, f2), lambda i: (0, 0)), pl.BlockSpec((1, f2), lambda i: (0, 0))],
        out_specs=[pl.BlockSpec((tm, f), lambda i: (i, 0)), pl.BlockSpec((tm, f2), lambda i: (i, 0))],
        out_shape=[jax.ShapeDtypeStruct((s, f), BF16), jax.ShapeDtypeStruct((s, f2), BF16)],
        compiler_params=_params(("parallel",)),
    )(up, up, cw, cb)


def _ffn_bwd(up, cv, dact, cw, name):
    s, f2 = up.shape
    f = f2 // 2
    tm = _pick(s, 128, SLAB)
    w = _pick(f, 256, LANES)
    nslab = tm // SLAB
    n_tiles = s // tm

    def body(t_ref, cv_ref, cvn_ref, da_ref, dan_ref, cw_ref, o_ref, dcw_ref, dcb_ref, acc_ref):
        i = pl.program_id(0)
        last_tile = i == n_tiles - 1

        @pl.when(i == 0)
        def _():
            acc_ref[...] = jnp.zeros_like(acc_ref)

        for c in range(f // w):
            cg = slice(c * w, (c + 1) * w)
            ca = slice(f + c * w, f + (c + 1) * w)

            def dconv(gcv, acv, dav):
                gcv, acv, dav = gcv.astype(F32), acv.astype(F32), dav.astype(F32)
                sg = _sigmoid(gcv)
                return dav * acv * (sg * (1.0 + gcv * (1.0 - sg))), dav * (gcv * sg)

            nxt = dconv(cvn_ref[:, cg], cvn_ref[:, ca], dan_ref[:, cg])
            nxt = tuple(jnp.where(last_tile, 0.0, v) for v in nxt)

            def slab(j, carry):
                r = _rows(nslab - 1 - j)
                d = dconv(cv_ref[r, cg], cv_ref[r, ca], da_ref[r, cg])
                for half, cs in enumerate((cg, ca)):
                    x = t_ref[r, cs].astype(F32)
                    d0 = d[half]
                    d1 = _shift_up(d0, carry[half], 1)
                    d2 = _shift_up(d0, carry[half], 2)
                    o_ref[r, cs] = (cw_ref[2:3, cs] * d0 + cw_ref[1:2, cs] * d1 + cw_ref[0:1, cs] * d2).astype(BF16)
                    for k, term in enumerate((d2 * x, d1 * x, d0 * x, d0)):
                        acc_ref[k, :, cs] += term
                return d

            lax.fori_loop(0, nslab, slab, nxt)

        @pl.when(last_tile)
        def _():
            for k in range(3):
                dcw_ref[k:k + 1, :] = jnp.sum(acc_ref[k], axis=0, keepdims=True)
            dcb_ref[...] = jnp.sum(acc_ref[3], axis=0, keepdims=True)

    _, next_cv = _halo_specs(tm, f2, n_tiles)
    _, next_da = _halo_specs(tm, f, n_tiles)
    return pl.pallas_call(
        body, name=name, grid=(n_tiles,),
        in_specs=[pl.BlockSpec((tm, f2), lambda i: (i, 0)), pl.BlockSpec((tm, f2), lambda i: (i, 0)), next_cv,
                  pl.BlockSpec((tm, f), lambda i: (i, 0)), next_da, pl.BlockSpec((3, f2), lambda i: (0, 0))],
        out_specs=[pl.BlockSpec((tm, f2), lambda i: (i, 0)), pl.BlockSpec((3, f2), lambda i: (0, 0)),
                   pl.BlockSpec((1, f2), lambda i: (0, 0))],
        out_shape=[jax.ShapeDtypeStruct((s, f2), BF16), jax.ShapeDtypeStruct((3, f2), F32),
                   jax.ShapeDtypeStruct((1, f2), F32)],
        scratch_shapes=[pltpu.VMEM((4, SLAB, f2), F32)],
        compiler_params=_params(("arbitrary",)),
    )(up, cv, cv, dact, dact, cw)


_GELU_C = math.sqrt(2.0 / math.pi)


def _gelu(x):
    th = jnp.tanh(_GELU_C * (x + 0.044715 * (x * x * x)))
    return x * (0.5 * (1.0 + th)), th


def _gelu_grad(x, th):
    return 0.5 * (1.0 + th) + 0.5 * x * (1.0 - th * th) * (_GELU_C * (1.0 + 3.0 * 0.044715 * (x * x)))


def _masked_ws(ws_ref, h):
    t = lax.broadcasted_iota(jnp.int32, (CHUNK, CHUNK), 0)
    sx = lax.broadcasted_iota(jnp.int32, (CHUNK, CHUNK), 1)
    return jnp.where(sx <= t, ws_ref[h], 0.0)


def _mixb_fwd(pre, gv, ws, bs_wide, name):
    s, w2 = pre.shape
    w = w2 // 2
    gw = w // SG_GROUPS

    def body(pre_ref, gv_ref, ws_ref, bs_ref, o_ref):
        zu, _ = _gelu(pre_ref[:, :w].astype(F32))
        zv, _ = _gelu(pre_ref[:, w:].astype(F32))
        _, vhat = _rms_stats(zv)
        vn = (vhat * gv_ref[...]).astype(BF16)
        for h in range(SG_GROUPS):
            cols = slice(h * gw, (h + 1) * gw)
            wsm = _masked_ws(ws_ref, h).astype(BF16)
            gate = jnp.dot(wsm, vn[:, cols], preferred_element_type=F32)
            gate = gate + jnp.tile(bs_ref[h], (1, gw // LANES))
            o_ref[:, cols] = (zu[:, cols] * gate).astype(BF16)

    return pl.pallas_call(
        body, name=name, grid=(s // CHUNK,),
        in_specs=[pl.BlockSpec((CHUNK, w2), lambda i: (i, 0)), pl.BlockSpec((1, w), lambda i: (0, 0)),
                  pl.BlockSpec((SG_GROUPS, CHUNK, CHUNK), lambda i: (0, 0, 0)),
                  pl.BlockSpec((SG_GROUPS, CHUNK, LANES), lambda i: (0, 0, 0))],
        out_specs=pl.BlockSpec((CHUNK, w), lambda i: (i, 0)),
        out_shape=jax.ShapeDtypeStruct((s, w), BF16),
        compiler_params=_params(("parallel",)),
    )(pre, gv, ws, bs_wide)


def _mixb_bwd(pre, dug, gv, ws, bs_wide, name):
    s, w2 = pre.shape
    w = w2 // 2
    gw = w // SG_GROUPS

    def body(pre_ref, dug_ref, gv_ref, ws_ref, bs_ref, o_ref, dws_ref, dbs_ref, dgv_ref, dvn_ref):
        first = pl.program_id(0) == 0

        @pl.when(first)
        def _():
            dws_ref[...] = jnp.zeros_like(dws_ref)
            dbs_ref[...] = jnp.zeros_like(dbs_ref)

        pu = pre_ref[:, :w].astype(F32)
        pv = pre_ref[:, w:].astype(F32)
        zu, thu = _gelu(pu)
        zv, thv = _gelu(pv)
        inv, vhat = _rms_stats(zv)
        gvv = gv_ref[...]
        vn = (vhat * gvv).astype(BF16)
        for h in range(SG_GROUPS):
            cols = slice(h * gw, (h + 1) * gw)
            wsm = _masked_ws(ws_ref, h).astype(BF16)
            gate = jnp.dot(wsm, vn[:, cols], preferred_element_type=F32)
            gate = gate + jnp.tile(bs_ref[h], (1, gw // LANES))
            dug_h = dug_ref[:, cols].astype(F32)
            dgate = dug_h * zu[:, cols]
            dgate_b = dgate.astype(BF16)
            o_ref[:, cols] = (dug_h * gate * _gelu_grad(pu[:, cols], thu[:, cols])).astype(BF16)
            dbs_ref[h] += jnp.broadcast_to(jnp.sum(dgate, axis=-1, keepdims=True), (CHUNK, LANES))
            dws = lax.dot_general(dgate_b, vn[:, cols], _DIMS["nt"], preferred_element_type=F32)
            t = lax.broadcasted_iota(jnp.int32, (CHUNK, CHUNK), 0)
            sx = lax.broadcasted_iota(jnp.int32, (CHUNK, CHUNK), 1)
            dws_ref[h] += jnp.where(sx <= t, dws, 0.0)
            dvn_ref[:, cols] = lax.dot_general(wsm, dgate_b, _DIMS["tn"], preferred_element_type=F32)
        dvn = dvn_ref[...]
        part = jnp.sum(dvn * vhat, axis=0, keepdims=True)

        @pl.when(first)
        def _():
            dgv_ref[...] = part

        @pl.when(jnp.logical_not(first))
        def _():
            dgv_ref[...] += part

        dvhat = dvn * gvv
        dzv = inv * (dvhat - vhat * jnp.mean(dvhat * vhat, axis=-1, keepdims=True))
        o_ref[:, w:] = (dzv * _gelu_grad(pv, thv)).astype(BF16)

    return pl.pallas_call(
        body, name=name, grid=(s // CHUNK,),
        in_specs=[pl.BlockSpec((CHUNK, w2), lambda i: (i, 0)), pl.BlockSpec((CHUNK, w), lambda i: (i, 0)),
                  pl.BlockSpec((1, w), lambda i: (0, 0)),
                  pl.BlockSpec((SG_GROUPS, CHUNK, CHUNK), lambda i: (0, 0, 0)),
                  pl.BlockSpec((SG_GROUPS, CHUNK, LANES), lambda i: (0, 0, 0))],
        out_specs=[pl.BlockSpec((CHUNK, w2), lambda i: (i, 0)),
                   pl.BlockSpec((SG_GROUPS, CHUNK, CHUNK), lambda i: (0, 0, 0)),
                   pl.BlockSpec((SG_GROUPS, CHUNK, LANES), lambda i: (0, 0, 0)),
                   pl.BlockSpec((1, w), lambda i: (0, 0))],
        out_shape=[jax.ShapeDtypeStruct((s, w2), BF16), jax.ShapeDtypeStruct((SG_GROUPS, CHUNK, CHUNK), F32),
                   jax.ShapeDtypeStruct((SG_GROUPS, CHUNK, LANES), F32), jax.ShapeDtypeStruct((1, w), F32)],
        scratch_shapes=[pltpu.VMEM((CHUNK, w), F32)],
        compiler_params=_params(("arbitrary",)),
    )(pre, dug, gv, ws, bs_wide)


def _cast_layer(w3, layer, name):
    _, r, c = w3.shape
    tr = _pick(r, 256, SLAB)

    def body(w_ref, o_ref):
        o_ref[...] = w_ref[...].astype(BF16)

    return pl.pallas_call(
        body, name=name, grid=(r // tr,),
        in_specs=[pl.BlockSpec((None, tr, c), lambda i: (layer, i, 0))],
        out_specs=pl.BlockSpec((tr, c), lambda i: (i, 0)),
        out_shape=jax.ShapeDtypeStruct((r, c), BF16),
        compiler_params=_params(("parallel",)),
    )(w3)


def _adamw_math(w, g, m, v):
    m = ADAM_B1 * m + (1.0 - ADAM_B1) * g
    v = ADAM_B2 * v + (1.0 - ADAM_B2) * (g * g)
    m_hat = m / (1.0 - ADAM_B1 ** ADAM_STEP)
    v_hat = v / (1.0 - ADAM_B2 ** ADAM_STEP)
    delta = -ADAM_LR * (m_hat / (jnp.sqrt(v_hat) + ADAM_EPS) + ADAM_WD * w)
    return delta, m, v


def _adamw_sharded(recvs, w, m, v, name):
    nl, r, c = w.shape
    tc = _pick(c, 1536, LANES)
    tr = _pick(r, 64, SLAB)

    def body(*refs):
        recv_refs = refs[:nl]
        w_ref, m_ref, v_ref, g_ref, d_ref, nm_ref, nv_ref = refs[nl:]
        for layer, recv_ref in enumerate(recv_refs):
            @pl.when(pl.program_id(0) == layer)
            def _():
                g = recv_ref[0].astype(F32)
                for q in range(1, N_DEV):
                    g = g + recv_ref[q].astype(F32)
                delta, nm, nv = _adamw_math(w_ref[...], g, m_ref[...], v_ref[...])
                g_ref[...] = g
                d_ref[...] = delta
                nm_ref[...] = nm
                nv_ref[...] = nv

    def recv_spec(layer):
        return pl.BlockSpec((N_DEV, tr, tc),
                            lambda l, i, j: (0, jnp.where(l == layer, i, 0), jnp.where(l == layer, j, 0)))

    blk = pl.BlockSpec((None, tr, tc), lambda l, i, j: (l, i, j))
    out = jax.ShapeDtypeStruct((nl, r, c), F32)
    return pl.pallas_call(
        body, name=name, grid=(nl, r // tr, c // tc),
        in_specs=[recv_spec(layer) for layer in range(nl)] + [blk, blk, blk],
        out_specs=[blk] * 4, out_shape=[out] * 4,
        compiler_params=_params(("parallel",) * 3),
    )(*recvs, w, m, v)


def _adamw_packed(w, g, m, v, name):
    r, c = w.shape
    tr = _pick(r, 256, 8)

    def body(w_ref, g_ref, m_ref, v_ref, d_ref, nm_ref, nv_ref):
        delta, nm, nv = _adamw_math(w_ref[...], g_ref[...], m_ref[...], v_ref[...])
        d_ref[...] = delta
        nm_ref[...] = nm
        nv_ref[...] = nv

    blk = pl.BlockSpec((tr, c), lambda i: (i, 0))
    out = jax.ShapeDtypeStruct((r, c), F32)
    return pl.pallas_call(
        body, name=name, grid=(r // tr,), in_specs=[blk] * 4, out_specs=[blk] * 3, out_shape=[out] * 3,
        compiler_params=_params(("parallel",)),
    )(w, g, m, v)


def _pack(arrays):
    parts = []
    for a in arrays:
        flat = a.reshape(-1).astype(F32)
        pad = (-flat.shape[0]) % PACK_GRANULE
        parts.append(jnp.pad(flat, (0, pad)) if pad else flat)
    return jnp.concatenate(parts).reshape(-1, LANES)


def _unpack(buf, shapes):
    flat = buf.reshape(-1)
    out, off = [], 0
    for shp in shapes:
        n = math.prod(shp)
        out.append(flat[off:off + n].reshape(shp))
        off += n + (-n) % PACK_GRANULE
    return out


def _mesh_pos():
    return lax.axis_index("x"), lax.axis_index("y"), lax.axis_index("c")


def _coords(q):
    return q // 4, (q // 2) % 2, q % 2


def _shard_of(ref, q, shard_shape, axis):
    r, c = shard_shape
    if axis == 0:
        return ref.at[pl.ds(pl.multiple_of(q * r, SLAB), r), :]
    return ref.at[:, pl.ds(pl.multiple_of(q * c, LANES), c)]


_HBM = pl.BlockSpec(memory_space=pltpu.HBM)
_SEM = pl.BlockSpec(memory_space=pltpu.SEMAPHORE)
_EFFECT = pltpu.SideEffectType.DATAFLOW_SIDE_EFFECTING


def _exchange_shapes(gather, src_shape, axis):
    r, c = src_shape
    if gather:
        return (r, c), ((r * N_DEV, c) if axis == 0 else (r, c * N_DEV))
    shard = (r // N_DEV, c) if axis == 0 else (r, c // N_DEV)
    return shard, (N_DEV,) + shard


def _exchange_copies(gather, src, land, sems, axis):
    send_sems, recv_sems, own_sem = sems
    x, y, c_ = _mesh_pos()
    me = 4 * x + 2 * y + c_
    shard, _ = _exchange_shapes(gather, src.shape, axis)

    def piece(q):
        return src if gather else _shard_of(src, q, shard, axis)

    def place(q):
        return _shard_of(land, q, shard, axis) if gather else land.at[q]

    own = pltpu.make_async_copy(piece(me), place(me), own_sem.at[0])
    sends, arrivals = [], []
    for step in range(1, N_DEV):
        to = (me + step) % N_DEV
        frm = (me + N_DEV - step) % N_DEV
        sends.append(pltpu.make_async_remote_copy(
            src_ref=piece(to), dst_ref=place(me), send_sem=send_sems.at[step - 1], recv_sem=recv_sems.at[step - 1],
            device_id=_coords(to), device_id_type=MESH))
        arrivals.append(pltpu.make_async_remote_copy(
            src_ref=piece(me), dst_ref=place(frm), send_sem=send_sems.at[step - 1], recv_sem=recv_sems.at[step - 1],
            device_id=_coords(frm), device_id_type=MESH))
    return own, sends, arrivals


def _exchange_start(gather, src, axis, name, after=None):
    _, land_shape = _exchange_shapes(gather, src.shape, axis)
    extra = () if after is None else (after,)

    def body(*refs):
        src_ref, land = refs[:2]
        send_sems, recv_sems, own_sem = refs[2 + len(extra):5 + len(extra)]
        own, sends, _ = _exchange_copies(gather, src_ref, land, (send_sems, recv_sems, own_sem), axis)
        own.start()
        for cp in sends:
            cp.start()
        refs[-1][...] = jnp.zeros_like(refs[-1])

    out = pl.pallas_call(
        body, name=name,
        out_shape=(pltpu.SemaphoreType.DMA((N_DEV - 1,)), pltpu.SemaphoreType.DMA((N_DEV - 1,)),
                   pltpu.SemaphoreType.DMA((1,)), pltpu.HBM(src.shape, src.dtype),
                   pltpu.HBM(land_shape, src.dtype), jax.ShapeDtypeStruct((8, LANES), F32)),
        in_specs=[_HBM, _HBM] + [pl.BlockSpec(memory_space=pl.ANY)] * len(extra),
        out_specs=(_SEM, _SEM, _SEM, _HBM, _HBM, pl.BlockSpec(memory_space=pltpu.VMEM)),
        input_output_aliases={0: 3, 1: 4},
        compiler_params=pltpu.CompilerParams(has_side_effects=_EFFECT),
    )(pltpu.with_memory_space_constraint(src, pltpu.HBM),
      pltpu.with_memory_space_constraint(lax.empty(land_shape, src.dtype), pltpu.HBM), *extra)
    return out[:5], out[5]


def _exchange_wait(gather, state, axis, after, name):
    send_sems, recv_sems, own_sem, src_thru, land_thru = state

    def body(src, land, send_sems, recv_sems, own_sem, after_ref, src_dead, got):
        own, sends, arrivals = _exchange_copies(gather, src, land, (send_sems, recv_sems, own_sem), axis)
        for cp in sends:
            cp.wait_send()
        for cp in arrivals:
            cp.wait_recv()
        own.wait()

    return pl.pallas_call(
        body, name=name,
        out_shape=(pltpu.HBM(src_thru.shape, src_thru.dtype), pltpu.HBM(land_thru.shape, land_thru.dtype)),
        in_specs=[_HBM, _HBM, _SEM, _SEM, _SEM, pl.BlockSpec(memory_space=pl.ANY)],
        out_specs=(_HBM, _HBM),
        input_output_aliases={0: 0, 1: 1},
        compiler_params=pltpu.CompilerParams(has_side_effects=_EFFECT),
    )(src_thru, land_thru, send_sems, recv_sems, own_sem, after)[1]


def _gather2_copies(shard_ref, land, sems, axis, shard_shape):
    send1, recv1, own_sem, send2, recv2 = sems
    x, y, c = _mesh_pos()
    me, sibling = (x, y, c), (x, y, 1 - c)
    chips = [(1 - x, y), (x, 1 - y), (1 - x, 1 - y)]

    def region(dev):
        px, py, pc = dev
        return _shard_of(land, 4 * px + 2 * py + pc, shard_shape, axis)

    def copy(src, block, to, send, recv):
        return pltpu.make_async_remote_copy(src_ref=src, dst_ref=region(block), send_sem=send, recv_sem=recv,
                                            device_id=to, device_id_type=MESH)

    own = pltpu.make_async_copy(shard_ref, region(me), own_sem.at[0])
    peers = [sibling] + [(*chip, c) for chip in chips]
    sends1 = [copy(shard_ref, me, to, send1.at[k], recv1.at[k]) for k, to in enumerate(peers)]
    arrivals1 = [copy(shard_ref, frm, frm, send1.at[k], recv1.at[k]) for k, frm in enumerate(peers)]
    sends2, arrivals2 = [], []
    if send2 is not None:
        for j, chip in enumerate(chips):
            sends2.append(copy(region((*chip, c)), (*chip, c), sibling, send2.at[j], recv2.at[j]))
            arrivals2.append(copy(region((*chip, 1 - c)), (*chip, 1 - c), sibling, send2.at[j], recv2.at[j]))
    return own, sends1, arrivals1, sends2, arrivals2


def _gather2_start(shard, axis, name, after=None):
    _, land_shape = _exchange_shapes(True, shard.shape, axis)
    extra = () if after is None else (after,)

    def body(*refs):
        src_ref, land = refs[:2]
        send1, recv1, own_sem = refs[2 + len(extra):5 + len(extra)]
        own, sends1, _, _, _ = _gather2_copies(src_ref, land, (send1, recv1, own_sem, None, None), axis, shard.shape)
        own.start()
        for cp in sends1[1:] + sends1[:1]:
            cp.start()
        refs[-1][...] = jnp.zeros_like(refs[-1])

    out = pl.pallas_call(
        body, name=name,
        out_shape=(pltpu.SemaphoreType.DMA((4,)), pltpu.SemaphoreType.DMA((4,)), pltpu.SemaphoreType.DMA((1,)),
                   pltpu.HBM(shard.shape, shard.dtype), pltpu.HBM(land_shape, shard.dtype),
                   jax.ShapeDtypeStruct((8, LANES), F32)),
        in_specs=[_HBM, _HBM] + [pl.BlockSpec(memory_space=pl.ANY)] * len(extra),
        out_specs=(_SEM, _SEM, _SEM, _HBM, _HBM, pl.BlockSpec(memory_space=pltpu.VMEM)),
        input_output_aliases={0: 3, 1: 4},
        compiler_params=pltpu.CompilerParams(has_side_effects=_EFFECT),
    )(pltpu.with_memory_space_constraint(shard, pltpu.HBM),
      pltpu.with_memory_space_constraint(lax.empty(land_shape, shard.dtype), pltpu.HBM), *extra)
    return out[:5], out[5]


def _gather2_pass(state, axis, after, name):
    send1, recv1, own_sem, shard_thru, land_thru = state

    def body(src_ref, land, send1, recv1, own_sem, after_ref, send2, recv2, src_out, land_out, token):
        _, _, arrivals1, sends2, _ = _gather2_copies(src_ref, land, (send1, recv1, own_sem, send2, recv2), axis,
                                                     shard_thru.shape)
        for arrival, fwd in zip(arrivals1[1:], sends2):
            arrival.wait_recv()
            fwd.start()
        token[...] = jnp.zeros_like(token)

    out = pl.pallas_call(
        body, name=name,
        out_shape=(pltpu.SemaphoreType.DMA((3,)), pltpu.SemaphoreType.DMA((3,)),
                   pltpu.HBM(shard_thru.shape, shard_thru.dtype), pltpu.HBM(land_thru.shape, land_thru.dtype),
                   jax.ShapeDtypeStruct((8, LANES), F32)),
        in_specs=[_HBM, _HBM, _SEM, _SEM, _SEM, pl.BlockSpec(memory_space=pl.ANY)],
        out_specs=(_SEM, _SEM, _HBM, _HBM, pl.BlockSpec(memory_space=pltpu.VMEM)),
        input_output_aliases={0: 2, 1: 3},
        compiler_params=pltpu.CompilerParams(has_side_effects=_EFFECT),
    )(shard_thru, land_thru, send1, recv1, own_sem, after)
    return (send1, recv1, own_sem, out[0], out[1], out[2], out[3]), out[4]


def _gather2_wait(state, axis, after, name):
    send1, recv1, own_sem, send2, recv2, shard_thru, land_thru = state

    def body(src_ref, land, send1, recv1, own_sem, send2, recv2, after_ref, src_dead, got):
        own, sends1, arrivals1, sends2, arrivals2 = _gather2_copies(
            src_ref, land, (send1, recv1, own_sem, send2, recv2), axis, shard_thru.shape)
        for cp in sends1 + sends2:
            cp.wait_send()
        for cp in arrivals1[:1] + arrivals2:
            cp.wait_recv()
        own.wait()

    return pl.pallas_call(
        body, name=name,
        out_shape=(pltpu.HBM(shard_thru.shape, shard_thru.dtype), pltpu.HBM(land_thru.shape, land_thru.dtype)),
        in_specs=[_HBM, _HBM] + [_SEM] * 5 + [pl.BlockSpec(memory_space=pl.ANY)],
        out_specs=(_HBM, _HBM),
        input_output_aliases={0: 0, 1: 1},
        compiler_params=pltpu.CompilerParams(has_side_effects=_EFFECT),
    )(shard_thru, land_thru, send1, recv1, own_sem, send2, recv2, after)[1]


def _all_reduce_small(part, name):
    r, c = part.shape

    def body(part_ref, out_ref, slots, send_sems, recv_sems, local_sem):
        x, y, cc = _mesh_pos()
        me = 4 * x + 2 * y + cc
        own = pltpu.make_async_copy(part_ref, slots.at[me], local_sem)
        own.start()
        sends = []
        for step in range(1, N_DEV):
            to = (me + step) % N_DEV
            cp = pltpu.make_async_remote_copy(
                src_ref=part_ref, dst_ref=slots.at[me], send_sem=send_sems.at[step - 1],
                recv_sem=recv_sems.at[step - 1], device_id=_coords(to), device_id_type=MESH)
            cp.start()
            sends.append(cp)
        for step in range(1, N_DEV):
            frm = (me + N_DEV - step) % N_DEV
            pltpu.make_async_remote_copy(
                src_ref=part_ref, dst_ref=slots.at[frm], send_sem=send_sems.at[step - 1],
                recv_sem=recv_sems.at[step - 1], device_id=_coords(frm), device_id_type=MESH).wait_recv()
        for cp in sends:
            cp.wait_send()
        own.wait()
        total = slots[0]
        for q in range(1, N_DEV):
            total = total + slots[q]
        out_ref[...] = total

    vmem = pl.BlockSpec(memory_space=pltpu.VMEM)
    return pl.pallas_call(
        body, name=name, in_specs=[vmem], out_specs=vmem,
        out_shape=jax.ShapeDtypeStruct((r, c), F32),
        scratch_shapes=[pltpu.VMEM((N_DEV, r, c), F32), pltpu.SemaphoreType.DMA((N_DEV - 1,)),
                        pltpu.SemaphoreType.DMA((N_DEV - 1,)), pltpu.SemaphoreType.DMA],
        compiler_params=pltpu.CompilerParams(has_side_effects=True, vmem_limit_bytes=VMEM_LIMIT),
    )(part)


def _gather_cols(full_rows, n_rows, shard_cols):
    return full_rows.reshape(N_DEV, n_rows, shard_cols).transpose(1, 0, 2).reshape(n_rows, N_DEV * shard_cols)


def kernel(x, a_norm, a_in, a_conv, a_out, b_norm, b_in, b_vnorm, b_ws, b_bs, b_out, f_norm, f_up, f_conv_w, f_conv_b, f_down, final_norm, loss_target, m_a_norm, m_a_in, m_a_conv, m_a_out, m_b_norm, m_b_in, m_b_vnorm, m_b_ws, m_b_bs, m_b_out, m_f_norm, m_f_up, m_f_conv_w, m_f_conv_b, m_f_down, m_final_norm, v_a_norm, v_a_in, v_a_conv, v_a_out, v_b_norm, v_b_in, v_b_vnorm, v_b_ws, v_b_bs, v_b_out, v_f_norm, v_f_up, v_f_conv_w, v_f_conv_b, v_f_down, v_final_norm):
    s, d = x.shape[1], x.shape[2]
    n_ffn = f_up.shape[0]
    f2 = f_up.shape[2] * N_DEV
    me = 4 * lax.axis_index("x") + 2 * lax.axis_index("y") + lax.axis_index("c")
    x0 = x.reshape(s, d)
    target = loss_target.reshape(s, d)

    wanted = [("a_in", _cast_layer(a_in, 0, "cast_a_in"), 1),
              ("small", _pack([a_conv, b_norm, b_vnorm, f_conv_w]), 0),
              ("a_out", _cast_layer(a_out, 0, "cast_a_out"), 0),
              ("f_up0", _cast_layer(f_up, 0, "cast_f_up0"), 1), ("f_down0", _cast_layer(f_down, 0, "cast_f_down0"), 0),
              ("b_in", _cast_layer(b_in, 0, "cast_b_in"), 1), ("b_out", _cast_layer(b_out, 0, "cast_b_out"), 0),
              ("f_up1", _cast_layer(f_up, 1, "cast_f_up1"), 1), ("f_down1", _cast_layer(f_down, 1, "cast_f_down1"), 0)]
    coming, tok = {}, None
    for key, shard, axis in wanted:
        state, tok = _gather2_start(shard, axis, f"ag_start_{key}", after=tok)
        coming[key] = (state, axis)

    def pass_on(keys, after):
        for key in keys:
            state, axis = coming[key]
            state, after = _gather2_pass(state, axis, after, f"ag_pass_{key}")
            coming[key] = (state, axis)
        return after

    def arrived(key, after):
        state, axis = coming[key]
        return _gather2_wait(state, axis, after, f"ag_wait_{key}")

    cshard = a_conv.shape[2]
    fshard = f_conv_w.shape[2]
    h0 = _rmsnorm_fwd(x0, a_norm, "mixa_norm", after=tok)
    w_a_in = arrived("a_in", pass_on(["a_in", "small"], h0))
    small_full = arrived("small", w_a_in)
    small_rows = small_full.reshape(N_DEV, -1)
    per_dev = _unpack_rows(small_rows, [(3, cshard), (cshard,), (cshard,), (n_ffn, 3, fshard)])
    a_conv_full = per_dev[0].transpose(1, 0, 2).reshape(3, d)
    b_norm_full = per_dev[1].reshape(1, d)
    b_vnorm_full = per_dev[2].reshape(1, d)
    f_conv_w_full = per_dev[3].transpose(1, 2, 0, 3).reshape(n_ffn, 3, f2)
    bs_wide = jnp.broadcast_to(b_bs[0][:, :, None], (SG_GROUPS, CHUNK, LANES))
    ws = b_ws[0]

    w_f_up, w_f_down = {}, {}

    def ffn_forward(xin, l, pass_early, pass_late):
        h = _rmsnorm_fwd(xin, f_norm[l:l + 1], f"ffn{l}_norm")
        w_f_up[l] = arrived(f"f_up{l}", h)
        up = _matmul(h, w_f_up[l], "nn", BF16, f"ffn{l}_up", after=pass_on(pass_early, w_f_up[l]))
        act, cv = _ffn_fwd(up, f_conv_w_full[l], f_conv_b[l:l + 1], f"ffn{l}_mid")
        w_f_down[l] = arrived(f"f_down{l}", pass_on(pass_late, act))
        xout = _matmul(act, w_f_down[l], "nn", F32, f"ffn{l}_down", resid=xin, tm_cap=512)
        return xout, (h, up, act, cv)

    bcx = _matmul(h0, w_a_in, "nn", BF16, "mixa_in", after=pass_on(["a_out"], small_full))
    ya = _mixa_fwd(bcx, a_conv_full, "mixa_mid")
    w_a_out = arrived("a_out", pass_on(["f_up0"], ya))
    x1 = _matmul(ya, w_a_out, "nn", F32, "mixa_out", resid=x0)
    x2, saved0 = ffn_forward(x1, 0, ["f_down0"], ["b_in", "b_out", "f_up1", "f_down1"])
    h2 = _rmsnorm_fwd(x2, b_norm_full, "mixb_norm")
    w_b_in = arrived("b_in", h2)
    pre = _matmul(h2, w_b_in, "nn", BF16, "mixb_in")
    ug = _mixb_fwd(pre, b_vnorm_full, ws, bs_wide, "mixb_mid")
    w_b_out = arrived("b_out", ug)
    x3 = _matmul(ug, w_b_out, "nn", F32, "mixb_out", resid=x2)
    x4, saved1 = ffn_forward(x3, 1, [], [])
    dx4, dx4b, loss_part, g_final = _final_loss(x4, final_norm.reshape(1, d), target, "loss_head")

    def _rs_start(grad, axis, name):
        return _exchange_start(False, grad, axis, name)

    def ffn_backward(xin, l, saved, dx, dxb):
        h, up, act, cv = saved
        g_down = _matmul(act, dxb, "tn", BF16, f"ffn{l}_down_dw", tm_cap=1408)
        rs_down, tok = _rs_start(g_down, 0, f"rs_start_f_down{l}")
        dact = _matmul(dxb, w_f_down[l], "nt", BF16, f"ffn{l}_down_dx", after=tok, tn_cap=1408)
        dup, g_cw, g_cb = _ffn_bwd(up, cv, dact, f_conv_w_full[l], f"ffn{l}_mid_bwd")
        g_up = _matmul(h, dup, "tn", BF16, f"ffn{l}_up_dw")
        rs_up, tok = _rs_start(g_up, 1, f"rs_start_f_up{l}")
        dh = _matmul(dup, w_f_up[l], "nt", F32, f"ffn{l}_up_dx", after=tok)
        dxin, dxinb, g_norm = _rmsnorm_bwd(xin, f_norm[l:l + 1], dh, dx, f"ffn{l}_norm_bwd")
        return dxin, dxinb, (rs_up, rs_down, g_cw, g_cb, g_norm)

    dx3, dx3b, gf1 = ffn_backward(x3, 1, saved1, dx4, dx4b)
    g_b_out = _matmul(ug, dx3b, "tn", BF16, "mixb_out_dw")
    rs_b_out, tok = _rs_start(g_b_out, 0, "rs_start_b_out")
    dug = _matmul(dx3b, w_b_out, "nt", BF16, "mixb_out_dx", after=tok)
    dpre, g_ws, g_bs_wide, g_bvnorm = _mixb_bwd(pre, dug, b_vnorm_full, ws, bs_wide, "mixb_mid_bwd")
    g_b_in = _matmul(h2, dpre, "tn", BF16, "mixb_in_dw")
    rs_b_in, tok = _rs_start(g_b_in, 1, "rs_start_b_in")
    dh2 = _matmul(dpre, w_b_in, "nt", F32, "mixb_in_dx", after=tok)
    dx2, dx2b, g_bnorm = _rmsnorm_bwd(x2, b_norm_full, dh2, dx3, "mixb_norm_bwd")
    dx1, dx1b, gf0 = ffn_backward(x1, 0, saved0, dx2, dx2b)
    g_a_out = _matmul(ya, dx1b, "tn", BF16, "mixa_out_dw")
    rs_a_out, tok = _rs_start(g_a_out, 0, "rs_start_a_out")
    dya = _matmul(dx1b, w_a_out, "nt", BF16, "mixa_out_dx", after=tok)
    dbcx, g_aconv = _mixa_bwd(bcx, dya, a_conv_full, "mixa_mid_bwd")
    g_a_in = _matmul(h0, dbcx, "tn", BF16, "mixa_in_dw")
    rs_a_in, tok = _rs_start(g_a_in, 1, "rs_start_a_in")
    dh0 = _matmul(dbcx, w_a_in, "nt", F32, "mixa_in_dx", after=tok)
    grad_x, _, g_anorm = _rmsnorm_bwd(x0, a_norm, dh0, dx1, "mixa_norm_bwd")

    big = {}
    for name, states, axis, w, m, v in (
            ("f_down", (gf0[1], gf1[1]), 0, f_down, m_f_down, v_f_down),
            ("f_up", (gf0[0], gf1[0]), 1, f_up, m_f_up, v_f_up),
            ("b_out", (rs_b_out,), 0, b_out, m_b_out, v_b_out), ("b_in", (rs_b_in,), 1, b_in, m_b_in, v_b_in),
            ("a_out", (rs_a_out,), 0, a_out, m_a_out, v_a_out), ("a_in", (rs_a_in,), 1, a_in, m_a_in, v_a_in)):
        recvs = [_exchange_wait(False, st, axis, grad_x, f"rs_wait_{name}{l}") for l, st in enumerate(states)]
        big[name] = _adamw_sharded(recvs, w, m, v, f"adamw_{name}")

    full_shapes = [(1, LANES), (1, d), (3, d), (1, d), (1, d), (SG_GROUPS, CHUNK, CHUNK), (SG_GROUPS, CHUNK),
                   (n_ffn, d), (n_ffn, 3, f2), (n_ffn, f2), (1, d)]
    parts = [loss_part, g_anorm, g_aconv, g_bnorm, g_bvnorm, g_ws, g_bs_wide[:, :, 0],
             jnp.concatenate([gf0[4], gf1[4]], axis=0), jnp.stack([gf0[2], gf1[2]]),
             jnp.concatenate([gf0[3], gf1[3]], axis=0), g_final]
    total = _all_reduce_small(_pack(parts), "all_reduce_small")
    (loss_v, r_anorm, r_aconv, r_bnorm, r_bvnorm, r_ws, r_bs, r_fnorm, r_fcw, r_fcb, r_final) = _unpack(total, full_shapes)
    small_grads = [
        r_anorm,
        lax.dynamic_slice_in_dim(r_aconv, me * cshard, cshard, axis=1).reshape(a_conv.shape),
        lax.dynamic_slice_in_dim(r_bnorm, me * cshard, cshard, axis=1),
        lax.dynamic_slice_in_dim(r_bvnorm, me * cshard, cshard, axis=1),
        r_ws.reshape(b_ws.shape), r_bs.reshape(b_bs.shape), r_fnorm,
        lax.dynamic_slice_in_dim(r_fcw, me * fshard, fshard, axis=2),
        r_fcb, r_final.reshape(final_norm.shape)]
    small_w = [a_norm, a_conv, b_norm, b_vnorm, b_ws, b_bs, f_norm, f_conv_w, f_conv_b, final_norm]
    small_m = [m_a_norm, m_a_conv, m_b_norm, m_b_vnorm, m_b_ws, m_b_bs, m_f_norm, m_f_conv_w, m_f_conv_b, m_final_norm]
    small_v = [v_a_norm, v_a_conv, v_b_norm, v_b_vnorm, v_b_ws, v_b_bs, v_f_norm, v_f_conv_w, v_f_conv_b, v_final_norm]
    shapes = [w.shape for w in small_w]
    packed = _adamw_packed(_pack(small_w), _pack(small_grads), _pack(small_m), _pack(small_v), "adamw_small")
    s_delta, s_m, s_v = (_unpack(p, shapes) for p in packed)
    small_names = ["a_norm", "a_conv", "b_norm", "b_vnorm", "b_ws", "b_bs", "f_norm", "f_conv_w", "f_conv_b", "final_norm"]
    small = {nm: (small_grads[i], s_delta[i], s_m[i], s_v[i]) for i, nm in enumerate(small_names)}

    order = ["a_norm", "a_in", "a_conv", "a_out", "b_norm", "b_in", "b_vnorm", "b_ws", "b_bs", "b_out",
             "f_norm", "f_up", "f_conv_w", "f_conv_b", "f_down", "final_norm"]
    res = {nm: (big[nm] if nm in big else small[nm]) for nm in order}
    outs = [loss_v[0, 0], grad_x.reshape(x.shape)]
    for k in range(4):
        outs += [res[nm][k] for nm in order]
    return tuple(outs)


def _unpack_rows(rows, shapes):
    out, off = [], 0
    for shp in shapes:
        n = math.prod(shp)
        out.append(rows[:, off:off + n].reshape((N_DEV,) + tuple(shp)))
        off += n + (-n) % PACK_GRANULE
    return out
```

```python
import functools
import math

import jax
import jax.numpy as jnp
from jax import lax
from jax.experimental import pallas as pl
from jax.experimental.pallas import tpu as pltpu

F32 = jnp.float32
BF16 = jnp.bfloat16
MESH = pl.DeviceIdType.MESH

N_DEV = 8
RMS_EPS = 1e-5
CHUNK = 128
SG_GROUPS = 8
ADAM_LR = 0.001
ADAM_B1 = 0.9
ADAM_B2 = 0.999
ADAM_EPS = 1e-08
ADAM_WD = 0.01
ADAM_STEP = 10

LANES = 128
SLAB = 16
VMEM_LIMIT = 48 * 1024 * 1024
PACK_GRANULE = 8 * LANES


def _pick(dim, cap, mult):
    best = None
    t = mult
    while t <= min(dim, cap):
        if dim % t == 0:
            best = t
        t += mult
    return dim if best is None else best


def _params(semantics=None):
    return pltpu.CompilerParams(dimension_semantics=semantics, vmem_limit_bytes=VMEM_LIMIT)


_DIMS = {
    "nn": (((1,), (0,)), ((), ())),
    "nt": (((1,), (1,)), ((), ())),
    "tn": (((0,), (0,)), ((), ())),
}


def _matmul(a, b, mode, out_dtype, name, resid=None, after=None, tm_cap=1024, tn_cap=1024, tk_cap=2816):
    if mode == "nn":
        (m, k), n = a.shape, b.shape[1]
    elif mode == "nt":
        (m, k), n = a.shape, b.shape[0]
    else:
        (k, m), n = a.shape, b.shape[1]
    tm, tn, tk = _pick(m, tm_cap, LANES), _pick(n, tn_cap, LANES), _pick(k, tk_cap, LANES)
    nk = k // tk
    n_in = 2 + (resid is not None) + (after is not None)

    def body(*refs):
        a_ref, b_ref = refs[:2]
        r_ref = refs[2] if resid is not None else None
        o_ref = refs[n_in]
        prod = lax.dot_general(a_ref[...], b_ref[...], _DIMS[mode], preferred_element_type=F32)

        def finish(r):
            if r_ref is not None:
                r = r + r_ref[...]
            o_ref[...] = r.astype(out_dtype)

        if nk == 1:
            finish(prod)
            return
        acc_ref = refs[n_in + 1]
        kk = pl.program_id(2)

        @pl.when(kk == 0)
        def _():
            acc_ref[...] = prod

        @pl.when(jnp.logical_and(kk > 0, kk < nk - 1))
        def _():
            acc_ref[...] += prod

        @pl.when(kk == nk - 1)
        def _():
            finish(acc_ref[...] + prod)

    a_spec = (pl.BlockSpec((tk, tm), lambda i, j, kk: (kk, i)) if mode == "tn"
              else pl.BlockSpec((tm, tk), lambda i, j, kk: (i, kk)))
    b_spec = (pl.BlockSpec((tn, tk), lambda i, j, kk: (j, kk)) if mode == "nt"
              else pl.BlockSpec((tk, tn), lambda i, j, kk: (kk, j)))
    o_spec = pl.BlockSpec((tm, tn), lambda i, j, kk: (i, j))
    in_specs = [a_spec, b_spec] + ([o_spec] if resid is not None else [])
    args = (a, b) + ((resid,) if resid is not None else ())
    if after is not None:
        in_specs.append(pl.BlockSpec(memory_space=pl.ANY))
        args += (after,)
    return pl.pallas_call(
        body, name=name, grid=(m // tm, n // tn, nk),
        in_specs=in_specs, out_specs=o_spec,
        out_shape=jax.ShapeDtypeStruct((m, n), out_dtype),
        scratch_shapes=[pltpu.VMEM((tm, tn), F32)] if nk > 1 else [],
        compiler_params=_params(("parallel", "parallel", "arbitrary")),
    )(*args)


def _rms_stats(xf):
    inv = lax.rsqrt(jnp.mean(xf * xf, axis=-1, keepdims=True) + RMS_EPS)
    return inv, xf * inv


def _rmsnorm_fwd(x, g, name, after=None):
    s, d = x.shape
    tm = _pick(s, 256, SLAB)
    extra = () if after is None else (after,)

    def body(x_ref, g_ref, *rest):
        _, xhat = _rms_stats(x_ref[...])
        rest[-1][...] = (xhat * g_ref[...]).astype(BF16)

    return pl.pallas_call(
        body, name=name, grid=(s // tm,),
        in_specs=[pl.BlockSpec((tm, d), lambda i: (i, 0)), pl.BlockSpec((1, d), lambda i: (0, 0))]
        + [pl.BlockSpec(memory_space=pl.ANY)] * len(extra),
        out_specs=pl.BlockSpec((tm, d), lambda i: (i, 0)),
        out_shape=jax.ShapeDtypeStruct((s, d), BF16),
        compiler_params=_params(("parallel",)),
    )(x, g, *extra)


def _rmsnorm_bwd(x, g, dh, dx_out, name):
    s, d = x.shape
    tm = _pick(s, 256, SLAB)

    def body(x_ref, g_ref, dh_ref, dxo_ref, dxi_ref, dxib_ref, dg_ref):
        inv, xhat = _rms_stats(x_ref[...])
        dhv = dh_ref[...]
        dxhat = dhv * g_ref[...]
        proj = jnp.mean(dxhat * xhat, axis=-1, keepdims=True)
        dx = dxo_ref[...] + inv * (dxhat - xhat * proj)
        dxi_ref[...] = dx
        dxib_ref[...] = dx.astype(BF16)
        part = jnp.sum(dhv * xhat, axis=0, keepdims=True)

        @pl.when(pl.program_id(0) == 0)
        def _():
            dg_ref[...] = part

        @pl.when(pl.program_id(0) > 0)
        def _():
            dg_ref[...] += part

    row = pl.BlockSpec((tm, d), lambda i: (i, 0))
    vec = pl.BlockSpec((1, d), lambda i: (0, 0))
    return pl.pallas_call(
        body, name=name, grid=(s // tm,),
        in_specs=[row, vec, row, row], out_specs=[row, row, vec],
        out_shape=[jax.ShapeDtypeStruct((s, d), F32), jax.ShapeDtypeStruct((s, d), BF16),
                   jax.ShapeDtypeStruct((1, d), F32)],
        compiler_params=_params(("arbitrary",)),
    )(x, g, dh, dx_out)


def _final_loss(x, g, target, name):
    s, d = x.shape
    tm = _pick(s, 256, SLAB)

    def body(x_ref, g_ref, t_ref, dx_ref, dxb_ref, loss_ref, dg_ref):
        inv, xhat = _rms_stats(x_ref[...])
        gv = g_ref[...]
        err = xhat * gv - t_ref[...]
        loss = 0.5 * jnp.sum(jnp.mean(err * err, axis=-1, keepdims=True), axis=0, keepdims=True)
        dy = err * (1.0 / d)
        dxhat = dy * gv
        proj = jnp.mean(dxhat * xhat, axis=-1, keepdims=True)
        dx = inv * (dxhat - xhat * proj)
        dx_ref[...] = dx
        dxb_ref[...] = dx.astype(BF16)
        part = jnp.sum(dy * xhat, axis=0, keepdims=True)
        loss_row = jnp.broadcast_to(loss, (1, LANES))

        @pl.when(pl.program_id(0) == 0)
        def _():
            dg_ref[...] = part
            loss_ref[...] = loss_row

        @pl.when(pl.program_id(0) > 0)
        def _():
            dg_ref[...] += part
            loss_ref[...] += loss_row

    row = pl.BlockSpec((tm, d), lambda i: (i, 0))
    vec = pl.BlockSpec((1, d), lambda i: (0, 0))
    return pl.pallas_call(
        body, name=name, grid=(s // tm,),
        in_specs=[row, vec, row],
        out_specs=[row, row, pl.BlockSpec((1, LANES), lambda i: (0, 0)), vec],
        out_shape=[jax.ShapeDtypeStruct((s, d), F32), jax.ShapeDtypeStruct((s, d), BF16),
                   jax.ShapeDtypeStruct((1, LANES), F32), jax.ShapeDtypeStruct((1, d), F32)],
        compiler_params=_params(("arbitrary",)),
    )(x, g, target)


def _shift_down(prev, cur, k):
    ext = jnp.concatenate([prev, cur], axis=0)
    return pltpu.roll(ext, k, 0)[SLAB:, :]


def _shift_up(cur, nxt, k):
    ext = jnp.concatenate([cur, nxt], axis=0)
    return pltpu.roll(ext, 2 * SLAB - k, 0)[:SLAB, :]


def _conv3(w_ref, cols, prev, cur):
    s1 = _shift_down(prev, cur, 1)
    s2 = _shift_down(prev, cur, 2)
    y = w_ref[0:1, cols] * s2 + w_ref[1:2, cols] * s1 + w_ref[2:3, cols] * cur
    return y, s1, s2


def _conv3_t(w_ref, cols, cur, nxt):
    return (w_ref[2:3, cols] * cur + w_ref[1:2, cols] * _shift_up(cur, nxt, 1)
            + w_ref[0:1, cols] * _shift_up(cur, nxt, 2))


def _rows(s):
    return pl.ds(pl.multiple_of(s * SLAB, SLAB), SLAB)


def _halo_specs(tm, width, n_tiles):
    per = tm // SLAB
    prev = pl.BlockSpec((SLAB, width), lambda i: (jnp.maximum(i * per - 1, 0), 0))
    nxt = pl.BlockSpec((SLAB, width), lambda i: (jnp.minimum((i + 1) * per, n_tiles * per - 1), 0))
    return prev, nxt


def _add_rows(acc_ref, out_ref, row, cols, first):
    part = jnp.sum(acc_ref[...], axis=0, keepdims=True)

    @pl.when(first)
    def _():
        out_ref[row:row + 1, cols] = part

    @pl.when(jnp.logical_not(first))
    def _():
        out_ref[row:row + 1, cols] += part


def _mixa_fwd(bcx, wc, name):
    s, d3 = bcx.shape
    d = d3 // 3
    tm = _pick(s, 256, SLAB)
    w = _pick(d, 512, LANES)
    nslab = tm // SLAB

    def body(t_ref, prev_ref, wc_ref, y_ref):
        first_tile = pl.program_id(0) == 0
        for c in range(d // w):
            cb, cc, cx = (slice(g * d + c * w, g * d + (c + 1) * w) for g in range(3))
            cols = slice(c * w, (c + 1) * w)
            p_halo = prev_ref[:, cc].astype(F32) * prev_ref[:, cx].astype(F32)
            p_halo = jnp.where(first_tile, 0.0, p_halo)

            def slab(si, p_prev):
                r = _rows(si)
                p = t_ref[r, cc].astype(F32) * t_ref[r, cx].astype(F32)
                cv, _, _ = _conv3(wc_ref, cols, p_prev, p)
                y_ref[r, cols] = (t_ref[r, cb].astype(F32) * cv).astype(BF16)
                return p

            lax.fori_loop(0, nslab, slab, p_halo)

    prev_spec, _ = _halo_specs(tm, d3, s // tm)
    return pl.pallas_call(
        body, name=name, grid=(s // tm,),
        in_specs=[pl.BlockSpec((tm, d3), lambda i: (i, 0)), prev_spec, pl.BlockSpec((3, d), lambda i: (0, 0))],
        out_specs=pl.BlockSpec((tm, d), lambda i: (i, 0)),
        out_shape=jax.ShapeDtypeStruct((s, d), BF16),
        compiler_params=_params(("parallel",)),
    )(bcx, bcx, wc)


def _mixa_bwd(bcx, dy, wc, name):
    s, d3 = bcx.shape
    d = d3 // 3
    tm = _pick(s, 256, SLAB)
    w = _pick(d, 256, LANES)
    nslab = tm // SLAB
    n_tiles = s // tm

    def body(t_ref, prev_ref, next_ref, dy_ref, dyn_ref, wc_ref, o_ref, dwc_ref, a0, a1, a2):
        i = pl.program_id(0)
        first_tile = i == 0
        last_tile = i == n_tiles - 1
        for c in range(d // w):
            cb, cc, cx = (slice(g * d + c * w, g * d + (c + 1) * w) for g in range(3))
            cols = slice(c * w, (c + 1) * w)
            for acc in (a0, a1, a2):
                acc[...] = jnp.zeros_like(acc)
            dcv_next = jnp.where(last_tile, 0.0, dyn_ref[:, cols].astype(F32) * next_ref[:, cb].astype(F32))
            p_halo = jnp.where(first_tile, 0.0, prev_ref[:, cc].astype(F32) * prev_ref[:, cx].astype(F32))

            def one(r, p_prev, dcv_nxt):
                gb = t_ref[r, cb].astype(F32)
                gc = t_ref[r, cc].astype(F32)
                xs = t_ref[r, cx].astype(F32)
                dyv = dy_ref[r, cols].astype(F32)
                p = gc * xs
                cv, s1, s2 = _conv3(wc_ref, cols, p_prev, p)
                dcv = dyv * gb
                a2[...] += dcv * p
                a1[...] += dcv * s1
                a0[...] += dcv * s2
                dp = _conv3_t(wc_ref, cols, dcv, dcv_nxt)
                o_ref[r, cb] = (dyv * cv).astype(BF16)
                o_ref[r, cc] = (dp * xs).astype(BF16)
                o_ref[r, cx] = (dp * gc).astype(BF16)
                return dcv

            def slab(j, dcv_nxt):
                si = nslab - 1 - j
                rp = _rows(si - 1)
                p_prev = t_ref[rp, cc].astype(F32) * t_ref[rp, cx].astype(F32)
                return one(_rows(si), p_prev, dcv_nxt)

            dcv_nxt = lax.fori_loop(0, nslab - 1, slab, dcv_next)
            one(pl.ds(0, SLAB), p_halo, dcv_nxt)
            for k, acc in enumerate((a0, a1, a2)):
                _add_rows(acc, dwc_ref, k, cols, first_tile)

    prev_spec, next_spec = _halo_specs(tm, d3, n_tiles)
    _, next_dy = _halo_specs(tm, d, n_tiles)
    return pl.pallas_call(
        body, name=name, grid=(n_tiles,),
        in_specs=[pl.BlockSpec((tm, d3), lambda i: (i, 0)), prev_spec, next_spec,
                  pl.BlockSpec((tm, d), lambda i: (i, 0)), next_dy, pl.BlockSpec((3, d), lambda i: (0, 0))],
        out_specs=[pl.BlockSpec((tm, d3), lambda i: (i, 0)), pl.BlockSpec((3, d), lambda i: (0, 0))],
        out_shape=[jax.ShapeDtypeStruct((s, d3), BF16), jax.ShapeDtypeStruct((3, d), F32)],
        scratch_shapes=[pltpu.VMEM((SLAB, w), F32)] * 3,
        compiler_params=_params(("arbitrary",)),
    )(bcx, bcx, bcx, dy, dy, wc)


def _sigmoid(z):
    return 0.5 * jnp.tanh(0.5 * z) + 0.5


def _ffn_fwd(up, cw, cb, name):
    s, f2 = up.shape
    f = f2 // 2
    tm = _pick(s, 256, SLAB)
    w = _pick(f, 512, LANES)
    nslab = tm // SLAB

    def body(t_ref, prev_ref, cw_ref, cb_ref, act_ref, cv_ref):
        first_tile = pl.program_id(0) == 0
        for c in range(f // w):
            cg = slice(c * w, (c + 1) * w)
            ca = slice(f + c * w, f + (c + 1) * w)
            halo = tuple(jnp.where(first_tile, 0.0, prev_ref[:, cs].astype(F32)) for cs in (cg, ca))

            def slab(si, carry):
                r = _rows(si)
                g = t_ref[r, cg].astype(F32)
                a = t_ref[r, ca].astype(F32)
                gcv = _conv3(cw_ref, cg, carry[0], g)[0] + cb_ref[:, cg]
                acv = _conv3(cw_ref, ca, carry[1], a)[0] + cb_ref[:, ca]
                cv_ref[r, cg] = gcv.astype(BF16)
                cv_ref[r, ca] = acv.astype(BF16)
                act_ref[r, cg] = (gcv * _sigmoid(gcv) * acv).astype(BF16)
                return g, a

            lax.fori_loop(0, nslab, slab, halo)

    prev_spec, _ = _halo_specs(tm, f2, s // tm)
    return pl.pallas_call(
        body, name=name, grid=(s // tm,),
        in_specs=[pl.BlockSpec((tm, f2), lambda i: (i, 0)), prev_spec,
                  pl.BlockSpec((3, f2), lambda i: (0, 0)), pl.BlockSpec((1, f2), lambda i: (0, 0))],
        out_specs=[pl.BlockSpec((tm, f), lambda i: (i, 0)), pl.BlockSpec((tm, f2), lambda i: (i, 0))],
        out_shape=[jax.ShapeDtypeStruct((s, f), BF16), jax.ShapeDtypeStruct((s, f2), BF16)],
        compiler_params=_params(("parallel",)),
    )(up, up, cw, cb)


def _ffn_bwd(up, cv, dact, cw, name):
    s, f2 = up.shape
    f = f2 // 2
    tm = _pick(s, 128, SLAB)
    w = _pick(f, 256, LANES)
    nslab = tm // SLAB
    n_tiles = s // tm

    def body(t_ref, cv_ref, cvn_ref, da_ref, dan_ref, cw_ref, o_ref, dcw_ref, dcb_ref, acc_ref):
        i = pl.program_id(0)
        last_tile = i == n_tiles - 1

        @pl.when(i == 0)
        def _():
            acc_ref[...] = jnp.zeros_like(acc_ref)

        for c in range(f // w):
            cg = slice(c * w, (c + 1) * w)
            ca = slice(f + c * w, f + (c + 1) * w)

            def dconv(gcv, acv, dav):
                gcv, acv, dav = gcv.astype(F32), acv.astype(F32), dav.astype(F32)
                sg = _sigmoid(gcv)
                return dav * acv * (sg * (1.0 + gcv * (1.0 - sg))), dav * (gcv * sg)

            nxt = dconv(cvn_ref[:, cg], cvn_ref[:, ca], dan_ref[:, cg])
            nxt = tuple(jnp.where(last_tile, 0.0, v) for v in nxt)

            def slab(j, carry):
                r = _rows(nslab - 1 - j)
                d = dconv(cv_ref[r, cg], cv_ref[r, ca], da_ref[r, cg])
                for half, cs in enumerate((cg, ca)):
                    x = t_ref[r, cs].astype(F32)
                    d0 = d[half]
                    d1 = _shift_up(d0, carry[half], 1)
                    d2 = _shift_up(d0, carry[half], 2)
                    o_ref[r, cs] = (cw_ref[2:3, cs] * d0 + cw_ref[1:2, cs] * d1 + cw_ref[0:1, cs] * d2).astype(BF16)
                    for k, term in enumerate((d2 * x, d1 * x, d0 * x, d0)):
                        acc_ref[k, :, cs] += term
                return d

            lax.fori_loop(0, nslab, slab, nxt)

        @pl.when(last_tile)
        def _():
            for k in range(3):
                dcw_ref[k:k + 1, :] = jnp.sum(acc_ref[k], axis=0, keepdims=True)
            dcb_ref[...] = jnp.sum(acc_ref[3], axis=0, keepdims=True)

    _, next_cv = _halo_specs(tm, f2, n_tiles)
    _, next_da = _halo_specs(tm, f, n_tiles)
    return pl.pallas_call(
        body, name=name, grid=(n_tiles,),
        in_specs=[pl.BlockSpec((tm, f2), lambda i: (i, 0)), pl.BlockSpec((tm, f2), lambda i: (i, 0)), next_cv,
                  pl.BlockSpec((tm, f), lambda i: (i, 0)), next_da, pl.BlockSpec((3, f2), lambda i: (0, 0))],
        out_specs=[pl.BlockSpec((tm, f2), lambda i: (i, 0)), pl.BlockSpec((3, f2), lambda i: (0, 0)),
                   pl.BlockSpec((1, f2), lambda i: (0, 0))],
        out_shape=[jax.ShapeDtypeStruct((s, f2), BF16), jax.ShapeDtypeStruct((3, f2), F32),
                   jax.ShapeDtypeStruct((1, f2), F32)],
        scratch_shapes=[pltpu.VMEM((4, SLAB, f2), F32)],
        compiler_params=_params(("arbitrary",)),
    )(up, cv, cv, dact, dact, cw)


_GELU_C = math.sqrt(2.0 / math.pi)


def _gelu(x):
    th = jnp.tanh(_GELU_C * (x + 0.044715 * (x * x * x)))
    return x * (0.5 * (1.0 + th)), th


def _gelu_grad(x, th):
    return 0.5 * (1.0 + th) + 0.5 * x * (1.0 - th * th) * (_GELU_C * (1.0 + 3.0 * 0.044715 * (x * x)))


def _masked_ws(ws_ref, h):
    t = lax.broadcasted_iota(jnp.int32, (CHUNK, CHUNK), 0)
    sx = lax.broadcasted_iota(jnp.int32, (CHUNK, CHUNK), 1)
    return jnp.where(sx <= t, ws_ref[h], 0.0)


def _mixb_fwd(pre, gv, ws, bs_wide, name):
    s, w2 = pre.shape
    w = w2 // 2
    gw = w // SG_GROUPS

    def body(pre_ref, gv_ref, ws_ref, bs_ref, o_ref):
        zu, _ = _gelu(pre_ref[:, :w].astype(F32))
        zv, _ = _gelu(pre_ref[:, w:].astype(F32))
        _, vhat = _rms_stats(zv)
        vn = (vhat * gv_ref[...]).astype(BF16)
        for h in range(SG_GROUPS):
            cols = slice(h * gw, (h + 1) * gw)
            wsm = _masked_ws(ws_ref, h).astype(BF16)
            gate = jnp.dot(wsm, vn[:, cols], preferred_element_type=F32)
            gate = gate + jnp.tile(bs_ref[h], (1, gw // LANES))
            o_ref[:, cols] = (zu[:, cols] * gate).astype(BF16)

    return pl.pallas_call(
        body, name=name, grid=(s // CHUNK,),
        in_specs=[pl.BlockSpec((CHUNK, w2), lambda i: (i, 0)), pl.BlockSpec((1, w), lambda i: (0, 0)),
                  pl.BlockSpec((SG_GROUPS, CHUNK, CHUNK), lambda i: (0, 0, 0)),
                  pl.BlockSpec((SG_GROUPS, CHUNK, LANES), lambda i: (0, 0, 0))],
        out_specs=pl.BlockSpec((CHUNK, w), lambda i: (i, 0)),
        out_shape=jax.ShapeDtypeStruct((s, w), BF16),
        compiler_params=_params(("parallel",)),
    )(pre, gv, ws, bs_wide)


def _mixb_bwd(pre, dug, gv, ws, bs_wide, name):
    s, w2 = pre.shape
    w = w2 // 2
    gw = w // SG_GROUPS

    def body(pre_ref, dug_ref, gv_ref, ws_ref, bs_ref, o_ref, dws_ref, dbs_ref, dgv_ref, dvn_ref):
        first = pl.program_id(0) == 0

        @pl.when(first)
        def _():
            dws_ref[...] = jnp.zeros_like(dws_ref)
            dbs_ref[...] = jnp.zeros_like(dbs_ref)

        pu = pre_ref[:, :w].astype(F32)
        pv = pre_ref[:, w:].astype(F32)
        zu, thu = _gelu(pu)
        zv, thv = _gelu(pv)
        inv, vhat = _rms_stats(zv)
        gvv = gv_ref[...]
        vn = (vhat * gvv).astype(BF16)
        for h in range(SG_GROUPS):
            cols = slice(h * gw, (h + 1) * gw)
            wsm = _masked_ws(ws_ref, h).astype(BF16)
            gate = jnp.dot(wsm, vn[:, cols], preferred_element_type=F32)
            gate = gate + jnp.tile(bs_ref[h], (1, gw // LANES))
            dug_h = dug_ref[:, cols].astype(F32)
            dgate = dug_h * zu[:, cols]
            dgate_b = dgate.astype(BF16)
            o_ref[:, cols] = (dug_h * gate * _gelu_grad(pu[:, cols], thu[:, cols])).astype(BF16)
            dbs_ref[h] += jnp.broadcast_to(jnp.sum(dgate, axis=-1, keepdims=True), (CHUNK, LANES))
            dws = lax.dot_general(dgate_b, vn[:, cols], _DIMS["nt"], preferred_element_type=F32)
            t = lax.broadcasted_iota(jnp.int32, (CHUNK, CHUNK), 0)
            sx = lax.broadcasted_iota(jnp.int32, (CHUNK, CHUNK), 1)
            dws_ref[h] += jnp.where(sx <= t, dws, 0.0)
            dvn_ref[:, cols] = lax.dot_general(wsm, dgate_b, _DIMS["tn"], preferred_element_type=F32)
        dvn = dvn_ref[...]
        part = jnp.sum(dvn * vhat, axis=0, keepdims=True)

        @pl.when(first)
        def _():
            dgv_ref[...] = part

        @pl.when(jnp.logical_not(first))
        def _():
            dgv_ref[...] += part

        dvhat = dvn * gvv
        dzv = inv * (dvhat - vhat * jnp.mean(dvhat * vhat, axis=-1, keepdims=True))
        o_ref[:, w:] = (dzv * _gelu_grad(pv, thv)).astype(BF16)

    return pl.pallas_call(
        body, name=name, grid=(s // CHUNK,),
        in_specs=[pl.BlockSpec((CHUNK, w2), lambda i: (i, 0)), pl.BlockSpec((CHUNK, w), lambda i: (i, 0)),
                  pl.BlockSpec((1, w), lambda i: (0, 0)),
                  pl.BlockSpec((SG_GROUPS, CHUNK, CHUNK), lambda i: (0, 0, 0)),
                  pl.BlockSpec((SG_GROUPS, CHUNK, LANES), lambda i: (0, 0, 0))],
        out_specs=[pl.BlockSpec((CHUNK, w2), lambda i: (i, 0)),
                   pl.BlockSpec((SG_GROUPS, CHUNK, CHUNK), lambda i: (0, 0, 0)),
                   pl.BlockSpec((SG_GROUPS, CHUNK, LANES), lambda i: (0, 0, 0)),
                   pl.BlockSpec((1, w), lambda i: (0, 0))],
        out_shape=[jax.ShapeDtypeStruct((s, w2), BF16), jax.ShapeDtypeStruct((SG_GROUPS, CHUNK, CHUNK), F32),
                   jax.ShapeDtypeStruct((SG_GROUPS, CHUNK, LANES), F32), jax.ShapeDtypeStruct((1, w), F32)],
        scratch_shapes=[pltpu.VMEM((CHUNK, w), F32)],
        compiler_params=_params(("arbitrary",)),
    )(pre, dug, gv, ws, bs_wide)


def _cast_layer(w3, layer, name):
    _, r, c = w3.shape
    tr = _pick(r, 256, SLAB)

    def body(w_ref, o_ref):
        o_ref[...] = w_ref[...].astype(BF16)

    return pl.pallas_call(
        body, name=name, grid=(r // tr,),
        in_specs=[pl.BlockSpec((None, tr, c), lambda i: (layer, i, 0))],
        out_specs=pl.BlockSpec((tr, c), lambda i: (i, 0)),
        out_shape=jax.ShapeDtypeStruct((r, c), BF16),
        compiler_params=_params(("parallel",)),
    )(w3)


def _adamw_math(w, g, m, v):
    m = ADAM_B1 * m + (1.0 - ADAM_B1) * g
    v = ADAM_B2 * v + (1.0 - ADAM_B2) * (g * g)
    m_hat = m / (1.0 - ADAM_B1 ** ADAM_STEP)
    v_hat = v / (1.0 - ADAM_B2 ** ADAM_STEP)
    delta = -ADAM_LR * (m_hat / (jnp.sqrt(v_hat) + ADAM_EPS) + ADAM_WD * w)
    return delta, m, v


def _adamw_sharded(recvs, w, m, v, name):
    nl, r, c = w.shape
    tc = _pick(c, 1536, LANES)
    tr = _pick(r, 64, SLAB)

    def body(*refs):
        recv_refs = refs[:nl]
        w_ref, m_ref, v_ref, g_ref, d_ref, nm_ref, nv_ref = refs[nl:]
        for layer, recv_ref in enumerate(recv_refs):
            @pl.when(pl.program_id(0) == layer)
            def _():
                g = recv_ref[0].astype(F32)
                for q in range(1, N_DEV):
                    g = g + recv_ref[q].astype(F32)
                delta, nm, nv = _adamw_math(w_ref[...], g, m_ref[...], v_ref[...])
                g_ref[...] = g
                d_ref[...] = delta
                nm_ref[...] = nm
                nv_ref[...] = nv

    def recv_spec(layer):
        return pl.BlockSpec((N_DEV, tr, tc),
                            lambda l, i, j: (0, jnp.where(l == layer, i, 0), jnp.where(l == layer, j, 0)))

    blk = pl.BlockSpec((None, tr, tc), lambda l, i, j: (l, i, j))
    out = jax.ShapeDtypeStruct((nl, r, c), F32)
    return pl.pallas_call(
        body, name=name, grid=(nl, r // tr, c // tc),
        in_specs=[recv_spec(layer) for layer in range(nl)] + [blk, blk, blk],
        out_specs=[blk] * 4, out_shape=[out] * 4,
        compiler_params=_params(("parallel",) * 3),
    )(*recvs, w, m, v)


def _adamw_packed(w, g, m, v, name):
    r, c = w.shape
    tr = _pick(r, 256, 8)

    def body(w_ref, g_ref, m_ref, v_ref, d_ref, nm_ref, nv_ref):
        delta, nm, nv = _adamw_math(w_ref[...], g_ref[...], m_ref[...], v_ref[...])
        d_ref[...] = delta
        nm_ref[...] = nm
        nv_ref[...] = nv

    blk = pl.BlockSpec((tr, c), lambda i: (i, 0))
    out = jax.ShapeDtypeStruct((r, c), F32)
    return pl.pallas_call(
        body, name=name, grid=(r // tr,), in_specs=[blk] * 4, out_specs=[blk] * 3, out_shape=[out] * 3,
        compiler_params=_params(("parallel",)),
    )(w, g, m, v)


def _pack(arrays):
    parts = []
    for a in arrays:
        flat = a.reshape(-1).astype(F32)
        pad = (-flat.shape[0]) % PACK_GRANULE
        parts.append(jnp.pad(flat, (0, pad)) if pad else flat)
    return jnp.concatenate(parts).reshape(-1, LANES)


def _unpack(buf, shapes):
    flat = buf.reshape(-1)
    out, off = [], 0
    for shp in shapes:
        n = math.prod(shp)
        out.append(flat[off:off + n].reshape(shp))
        off += n + (-n) % PACK_GRANULE
    return out


def _mesh_pos():
    return lax.axis_index("x"), lax.axis_index("y"), lax.axis_index("c")


def _coords(q):
    return q // 4, (q // 2) % 2, q % 2


def _shard_of(ref, q, shard_shape, axis):
    r, c = shard_shape
    if axis == 0:
        return ref.at[pl.ds(pl.multiple_of(q * r, SLAB), r), :]
    return ref.at[:, pl.ds(pl.multiple_of(q * c, LANES), c)]


_HBM = pl.BlockSpec(memory_space=pltpu.HBM)
_SEM = pl.BlockSpec(memory_space=pltpu.SEMAPHORE)
_EFFECT = pltpu.SideEffectType.DATAFLOW_SIDE_EFFECTING


def _exchange_shapes(gather, src_shape, axis):
    r, c = src_shape
    if gather:
        return (r, c), ((r * N_DEV, c) if axis == 0 else (r, c * N_DEV))
    shard = (r // N_DEV, c) if axis == 0 else (r, c // N_DEV)
    return shard, (N_DEV,) + shard


def _exchange_copies(gather, src, land, sems, axis):
    send_sems, recv_sems, own_sem = sems
    x, y, c_ = _mesh_pos()
    me = 4 * x + 2 * y + c_
    shard, _ = _exchange_shapes(gather, src.shape, axis)

    def piece(q):
        return src if gather else _shard_of(src, q, shard, axis)

    def place(q):
        return _shard_of(land, q, shard, axis) if gather else land.at[q]

    own = pltpu.make_async_copy(piece(me), place(me), own_sem.at[0])
    sends, arrivals = [], []
    for step in range(1, N_DEV):
        to = (me + step) % N_DEV
        frm = (me + N_DEV - step) % N_DEV
        sends.append(pltpu.make_async_remote_copy(
            src_ref=piece(to), dst_ref=place(me), send_sem=send_sems.at[step - 1], recv_sem=recv_sems.at[step - 1],
            device_id=_coords(to), device_id_type=MESH))
        arrivals.append(pltpu.make_async_remote_copy(
            src_ref=piece(me), dst_ref=place(frm), send_sem=send_sems.at[step - 1], recv_sem=recv_sems.at[step - 1],
            device_id=_coords(frm), device_id_type=MESH))
    return own, sends, arrivals


def _exchange_start(gather, src, axis, name, after=None):
    _, land_shape = _exchange_shapes(gather, src.shape, axis)
    extra = () if after is None else (after,)

    def body(*refs):
        src_ref, land = refs[:2]
        send_sems, recv_sems, own_sem = refs[2 + len(extra):5 + len(extra)]
        own, sends, _ = _exchange_copies(gather, src_ref, land, (send_sems, recv_sems, own_sem), axis)
        own.start()
        for cp in sends:
            cp.start()
        refs[-1][...] = jnp.zeros_like(refs[-1])

    out = pl.pallas_call(
        body, name=name,
        out_shape=(pltpu.SemaphoreType.DMA((N_DEV - 1,)), pltpu.SemaphoreType.DMA((N_DEV - 1,)),
                   pltpu.SemaphoreType.DMA((1,)), pltpu.HBM(src.shape, src.dtype),
                   pltpu.HBM(land_shape, src.dtype), jax.ShapeDtypeStruct((8, LANES), F32)),
        in_specs=[_HBM, _HBM] + [pl.BlockSpec(memory_space=pl.ANY)] * len(extra),
        out_specs=(_SEM, _SEM, _SEM, _HBM, _HBM, pl.BlockSpec(memory_space=pltpu.VMEM)),
        input_output_aliases={0: 3, 1: 4},
        compiler_params=pltpu.CompilerParams(has_side_effects=_EFFECT),
    )(pltpu.with_memory_space_constraint(src, pltpu.HBM),
      pltpu.with_memory_space_constraint(lax.empty(land_shape, src.dtype), pltpu.HBM), *extra)
    return out[:5], out[5]


def _exchange_wait(gather, state, axis, after, name):
    send_sems, recv_sems, own_sem, src_thru, land_thru = state

    def body(src, land, send_sems, recv_sems, own_sem, after_ref, src_dead, got):
        own, sends, arrivals = _exchange_copies(gather, src, land, (send_sems, recv_sems, own_sem), axis)
        for cp in sends:
            cp.wait_send()
        for cp in arrivals:
            cp.wait_recv()
        own.wait()

    return pl.pallas_call(
        body, name=name,
        out_shape=(pltpu.HBM(src_thru.shape, src_thru.dtype), pltpu.HBM(land_thru.shape, land_thru.dtype)),
        in_specs=[_HBM, _HBM, _SEM, _SEM, _SEM, pl.BlockSpec(memory_space=pl.ANY)],
        out_specs=(_HBM, _HBM),
        input_output_aliases={0: 0, 1: 1},
        compiler_params=pltpu.CompilerParams(has_side_effects=_EFFECT),
    )(src_thru, land_thru, send_sems, recv_sems, own_sem, after)[1]


def _gather2_copies(shard_ref, land, sems, axis, shard_shape):
    send1, recv1, own_sem, send2, recv2 = sems
    x, y, c = _mesh_pos()
    me, sibling = (x, y, c), (x, y, 1 - c)
    chips = [(1 - x, y), (x, 1 - y), (1 - x, 1 - y)]

    def region(dev):
        px, py, pc = dev
        return _shard_of(land, 4 * px + 2 * py + pc, shard_shape, axis)

    def copy(src, block, to, send, recv):
        return pltpu.make_async_remote_copy(src_ref=src, dst_ref=region(block), send_sem=send, recv_sem=recv,
                                            device_id=to, device_id_type=MESH)

    own = pltpu.make_async_copy(shard_ref, region(me), own_sem.at[0])
    peers = [sibling] + [(*chip, c) for chip in chips]
    sends1 = [copy(shard_ref, me, to, send1.at[k], recv1.at[k]) for k, to in enumerate(peers)]
    arrivals1 = [copy(shard_ref, frm, frm, send1.at[k], recv1.at[k]) for k, frm in enumerate(peers)]
    sends2, arrivals2 = [], []
    if send2 is not None:
        for j, chip in enumerate(chips):
            sends2.append(copy(region((*chip, c)), (*chip, c), sibling, send2.at[j], recv2.at[j]))
            arrivals2.append(copy(region((*chip, 1 - c)), (*chip, 1 - c), sibling, send2.at[j], recv2.at[j]))
    return own, sends1, arrivals1, sends2, arrivals2


def _gather2_start(shard, axis, name, after=None):
    _, land_shape = _exchange_shapes(True, shard.shape, axis)
    extra = () if after is None else (after,)

    def body(*refs):
        src_ref, land = refs[:2]
        send1, recv1, own_sem = refs[2 + len(extra):5 + len(extra)]
        own, sends1, _, _, _ = _gather2_copies(src_ref, land, (send1, recv1, own_sem, None, None), axis, shard.shape)
        own.start()
        for cp in sends1[1:] + sends1[:1]:
            cp.start()
        refs[-1][...] = jnp.zeros_like(refs[-1])

    out = pl.pallas_call(
        body, name=name,
        out_shape=(pltpu.SemaphoreType.DMA((4,)), pltpu.SemaphoreType.DMA((4,)), pltpu.SemaphoreType.DMA((1,)),
                   pltpu.HBM(shard.shape, shard.dtype), pltpu.HBM(land_shape, shard.dtype),
                   jax.ShapeDtypeStruct((8, LANES), F32)),
        in_specs=[_HBM, _HBM] + [pl.BlockSpec(memory_space=pl.ANY)] * len(extra),
        out_specs=(_SEM, _SEM, _SEM, _HBM, _HBM, pl.BlockSpec(memory_space=pltpu.VMEM)),
        input_output_aliases={0: 3, 1: 4},
        compiler_params=pltpu.CompilerParams(has_side_effects=_EFFECT),
    )(pltpu.with_memory_space_constraint(shard, pltpu.HBM),
      pltpu.with_memory_space_constraint(lax.empty(land_shape, shard.dtype), pltpu.HBM), *extra)
    return out[:5], out[5]


def _gather2_pass(state, axis, after, name):
    send1, recv1, own_sem, shard_thru, land_thru = state

    def body(src_ref, land, send1, recv1, own_sem, after_ref, send2, recv2, src_out, land_out, token):
        _, _, arrivals1, sends2, _ = _gather2_copies(src_ref, land, (send1, recv1, own_sem, send2, recv2), axis,
                                                     shard_thru.shape)
        for arrival, fwd in zip(arrivals1[1:], sends2):
            arrival.wait_recv()
            fwd.start()
        token[...] = jnp.zeros_like(token)

    out = pl.pallas_call(
        body, name=name,
        out_shape=(pltpu.SemaphoreType.DMA((3,)), pltpu.SemaphoreType.DMA((3,)),
                   pltpu.HBM(shard_thru.shape, shard_thru.dtype), pltpu.HBM(land_thru.shape, land_thru.dtype),
                   jax.ShapeDtypeStruct((8, LANES), F32)),
        in_specs=[_HBM, _HBM, _SEM, _SEM, _SEM, pl.BlockSpec(memory_space=pl.ANY)],
        out_specs=(_SEM, _SEM, _HBM, _HBM, pl.BlockSpec(memory_space=pltpu.VMEM)),
        input_output_aliases={0: 2, 1: 3},
        compiler_params=pltpu.CompilerParams(has_side_effects=_EFFECT),
    )(shard_thru, land_thru, send1, recv1, own_sem, after)
    return (send1, recv1, own_sem, out[0], out[1], out[2], out[3]), out[4]


def _gather2_wait(state, axis, after, name):
    send1, recv1, own_sem, send2, recv2, shard_thru, land_thru = state

    def body(src_ref, land, send1, recv1, own_sem, send2, recv2, after_ref, src_dead, got):
        own, sends1, arrivals1, sends2, arrivals2 = _gather2_copies(
            src_ref, land, (send1, recv1, own_sem, send2, recv2), axis, shard_thru.shape)
        for cp in sends1 + sends2:
            cp.wait_send()
        for cp in arrivals1[:1] + arrivals2:
            cp.wait_recv()
        own.wait()

    return pl.pallas_call(
        body, name=name,
        out_shape=(pltpu.HBM(shard_thru.shape, shard_thru.dtype), pltpu.HBM(land_thru.shape, land_thru.dtype)),
        in_specs=[_HBM, _HBM] + [_SEM] * 5 + [pl.BlockSpec(memory_space=pl.ANY)],
        out_specs=(_HBM, _HBM),
        input_output_aliases={0: 0, 1: 1},
        compiler_params=pltpu.CompilerParams(has_side_effects=_EFFECT),
    )(shard_thru, land_thru, send1, recv1, own_sem, send2, recv2, after)[1]


def _sum_slots(slots, name):
    _, r, c = slots.shape
    tr = _pick(r, 512, 8)

    def body(s_ref, o_ref):
        total = s_ref[0]
        for q in range(1, N_DEV):
            total = total + s_ref[q]
        o_ref[...] = total

    return pl.pallas_call(
        body, name=name, grid=(r // tr,),
        in_specs=[pl.BlockSpec((N_DEV, tr, c), lambda i: (0, i, 0))],
        out_specs=pl.BlockSpec((tr, c), lambda i: (i, 0)),
        out_shape=jax.ShapeDtypeStruct((r, c), F32),
        compiler_params=_params(("parallel",)),
    )(slots)


def kernel(x, a_norm, a_in, a_conv, a_out, b_norm, b_in, b_vnorm, b_ws, b_bs, b_out, f_norm, f_up, f_conv_w, f_conv_b, f_down, final_norm, loss_target, m_a_norm, m_a_in, m_a_conv, m_a_out, m_b_norm, m_b_in, m_b_vnorm, m_b_ws, m_b_bs, m_b_out, m_f_norm, m_f_up, m_f_conv_w, m_f_conv_b, m_f_down, m_final_norm, v_a_norm, v_a_in, v_a_conv, v_a_out, v_b_norm, v_b_in, v_b_vnorm, v_b_ws, v_b_bs, v_b_out, v_f_norm, v_f_up, v_f_conv_w, v_f_conv_b, v_f_down, v_final_norm):
    s, d = x.shape[1], x.shape[2]
    n_ffn = f_up.shape[0]
    f2 = f_up.shape[2] * N_DEV
    me = 4 * lax.axis_index("x") + 2 * lax.axis_index("y") + lax.axis_index("c")
    x0 = x.reshape(s, d)
    target = loss_target.reshape(s, d)

    wanted = [("a_in", _cast_layer(a_in, 0, "cast_a_in"), 1),
              ("small", _pack([a_conv, b_norm, b_vnorm, f_conv_w]), 0),
              ("a_out", _cast_layer(a_out, 0, "cast_a_out"), 0),
              ("f_up0", _cast_layer(f_up, 0, "cast_f_up0"), 1), ("f_down0", _cast_layer(f_down, 0, "cast_f_down0"), 0),
              ("b_in", _cast_layer(b_in, 0, "cast_b_in"), 1), ("b_out", _cast_layer(b_out, 0, "cast_b_out"), 0),
              ("f_up1", _cast_layer(f_up, 1, "cast_f_up1"), 1), ("f_down1", _cast_layer(f_down, 1, "cast_f_down1"), 0)]
    coming, tok, h0 = {}, None, None
    for n_started, (key, shard, axis) in enumerate(wanted):
        if n_started == 2:
            tok = h0 = _rmsnorm_fwd(x0, a_norm, "mixa_norm", after=tok)
        state, tok = _gather2_start(shard, axis, f"ag_start_{key}", after=tok)
        coming[key] = (state, axis)

    def pass_on(keys, after):
        for key in keys:
            state, axis = coming[key]
            state, after = _gather2_pass(state, axis, after, f"ag_pass_{key}")
            coming[key] = (state, axis)
        return after

    def arrived(key, after):
        state, axis = coming[key]
        return _gather2_wait(state, axis, after, f"ag_wait_{key}")

    cshard = a_conv.shape[2]
    fshard = f_conv_w.shape[2]
    w_a_in = arrived("a_in", pass_on(["a_in", "small"], tok))
    small_full = arrived("small", w_a_in)
    small_rows = small_full.reshape(N_DEV, -1)
    per_dev = _unpack_rows(small_rows, [(3, cshard), (cshard,), (cshard,), (n_ffn, 3, fshard)])
    a_conv_full = per_dev[0].transpose(1, 0, 2).reshape(3, d)
    b_norm_full = per_dev[1].reshape(1, d)
    b_vnorm_full = per_dev[2].reshape(1, d)
    f_conv_w_full = per_dev[3].transpose(1, 2, 0, 3).reshape(n_ffn, 3, f2)
    bs_wide = jnp.broadcast_to(b_bs[0][:, :, None], (SG_GROUPS, CHUNK, LANES))
    ws = b_ws[0]

    w_f_up, w_f_down = {}, {}

    def ffn_forward(xin, l, pass_early, pass_late):
        h = _rmsnorm_fwd(xin, f_norm[l:l + 1], f"ffn{l}_norm")
        w_f_up[l] = arrived(f"f_up{l}", h)
        up = _matmul(h, w_f_up[l], "nn", BF16, f"ffn{l}_up", after=pass_on(pass_early, w_f_up[l]))
        act, cv = _ffn_fwd(up, f_conv_w_full[l], f_conv_b[l:l + 1], f"ffn{l}_mid")
        w_f_down[l] = arrived(f"f_down{l}", pass_on(pass_late, act))
        xout = _matmul(act, w_f_down[l], "nn", F32, f"ffn{l}_down", resid=xin, tk_cap=1408)
        return xout, (h, up, act, cv)

    bcx = _matmul(h0, w_a_in, "nn", BF16, "mixa_in", after=pass_on(["a_out"], small_full))
    ya = _mixa_fwd(bcx, a_conv_full, "mixa_mid")
    w_a_out = arrived("a_out", pass_on(["f_up0"], ya))
    x1 = _matmul(ya, w_a_out, "nn", F32, "mixa_out", resid=x0)
    x2, saved0 = ffn_forward(x1, 0, ["f_down0"], ["b_in", "b_out", "f_up1", "f_down1"])
    h2 = _rmsnorm_fwd(x2, b_norm_full, "mixb_norm")
    w_b_in = arrived("b_in", h2)
    pre = _matmul(h2, w_b_in, "nn", BF16, "mixb_in")
    ug = _mixb_fwd(pre, b_vnorm_full, ws, bs_wide, "mixb_mid")
    w_b_out = arrived("b_out", ug)
    x3 = _matmul(ug, w_b_out, "nn", F32, "mixb_out", resid=x2)
    x4, saved1 = ffn_forward(x3, 1, [], [])
    dx4, dx4b, loss_part, g_final = _final_loss(x4, final_norm.reshape(1, d), target, "loss_head")

    def _rs_start(grad, axis, name):
        return _exchange_start(False, grad, axis, name)

    def ffn_backward(xin, l, saved, dx, dxb):
        h, up, act, cv = saved
        g_down = _matmul(act, dxb, "tn", BF16, f"ffn{l}_down_dw", tm_cap=1408)
        rs_down, tok = _rs_start(g_down, 0, f"rs_start_f_down{l}")
        dact = _matmul(dxb, w_f_down[l], "nt", BF16, f"ffn{l}_down_dx", after=tok, tn_cap=1408)
        dup, g_cw, g_cb = _ffn_bwd(up, cv, dact, f_conv_w_full[l], f"ffn{l}_mid_bwd")
        g_up = _matmul(h, dup, "tn", BF16, f"ffn{l}_up_dw")
        rs_up, tok = _rs_start(g_up, 1, f"rs_start_f_up{l}")
        dh = _matmul(dup, w_f_up[l], "nt", F32, f"ffn{l}_up_dx", after=tok)
        dxin, dxinb, g_norm = _rmsnorm_bwd(xin, f_norm[l:l + 1], dh, dx, f"ffn{l}_norm_bwd")
        return dxin, dxinb, (rs_up, rs_down, g_cw, g_cb, g_norm)

    dx3, dx3b, gf1 = ffn_backward(x3, 1, saved1, dx4, dx4b)
    g_b_out = _matmul(ug, dx3b, "tn", BF16, "mixb_out_dw")
    rs_b_out, tok = _rs_start(g_b_out, 0, "rs_start_b_out")
    dug = _matmul(dx3b, w_b_out, "nt", BF16, "mixb_out_dx", after=tok)
    dpre, g_ws, g_bs_wide, g_bvnorm = _mixb_bwd(pre, dug, b_vnorm_full, ws, bs_wide, "mixb_mid_bwd")
    g_b_in = _matmul(h2, dpre, "tn", BF16, "mixb_in_dw")
    rs_b_in, tok = _rs_start(g_b_in, 1, "rs_start_b_in")
    dh2 = _matmul(dpre, w_b_in, "nt", F32, "mixb_in_dx", after=tok)
    dx2, dx2b, g_bnorm = _rmsnorm_bwd(x2, b_norm_full, dh2, dx3, "mixb_norm_bwd")
    dx1, dx1b, gf0 = ffn_backward(x1, 0, saved0, dx2, dx2b)
    g_a_out = _matmul(ya, dx1b, "tn", BF16, "mixa_out_dw")
    rs_a_out, tok = _rs_start(g_a_out, 0, "rs_start_a_out")
    dya = _matmul(dx1b, w_a_out, "nt", BF16, "mixa_out_dx", after=tok)
    dbcx, g_aconv = _mixa_bwd(bcx, dya, a_conv_full, "mixa_mid_bwd")
    g_a_in = _matmul(h0, dbcx, "tn", BF16, "mixa_in_dw")
    rs_a_in, tok = _rs_start(g_a_in, 1, "rs_start_a_in")
    dh0 = _matmul(dbcx, w_a_in, "nt", F32, "mixa_in_dx", after=tok)
    grad_x, _, g_anorm = _rmsnorm_bwd(x0, a_norm, dh0, dx1, "mixa_norm_bwd")

    full_shapes = [(1, LANES), (1, d), (3, d), (1, d), (1, d), (SG_GROUPS, CHUNK, CHUNK), (SG_GROUPS, CHUNK),
                   (n_ffn, d), (n_ffn, 3, f2), (n_ffn, f2), (1, d)]
    parts = [loss_part, g_anorm, g_aconv, g_bnorm, g_bvnorm, g_ws, g_bs_wide[:, :, 0],
             jnp.concatenate([gf0[4], gf1[4]], axis=0), jnp.stack([gf0[2], gf1[2]]),
             jnp.concatenate([gf0[3], gf1[3]], axis=0), g_final]
    small_part = _pack(parts)
    small_state, small_tok = _exchange_start(True, small_part, 0, "ar_start_small", after=grad_x)

    big = {}
    for name, states, axis, w, m, v in (
            ("f_down", (gf0[1], gf1[1]), 0, f_down, m_f_down, v_f_down),
            ("f_up", (gf0[0], gf1[0]), 1, f_up, m_f_up, v_f_up),
            ("b_out", (rs_b_out,), 0, b_out, m_b_out, v_b_out), ("b_in", (rs_b_in,), 1, b_in, m_b_in, v_b_in),
            ("a_out", (rs_a_out,), 0, a_out, m_a_out, v_a_out), ("a_in", (rs_a_in,), 1, a_in, m_a_in, v_a_in)):
        recvs = [_exchange_wait(False, st, axis, small_tok, f"rs_wait_{name}{l}") for l, st in enumerate(states)]
        big[name] = _adamw_sharded(recvs, w, m, v, f"adamw_{name}")

    slots = _exchange_wait(True, small_state, 0, big["a_in"][0], "ar_wait_small")
    total = _sum_slots(slots.reshape((N_DEV,) + small_part.shape), "ar_sum_small")
    (loss_v, r_anorm, r_aconv, r_bnorm, r_bvnorm, r_ws, r_bs, r_fnorm, r_fcw, r_fcb, r_final) = _unpack(total, full_shapes)
    small_grads = [
        r_anorm,
        lax.dynamic_slice_in_dim(r_aconv, me * cshard, cshard, axis=1).reshape(a_conv.shape),
        lax.dynamic_slice_in_dim(r_bnorm, me * cshard, cshard, axis=1),
        lax.dynamic_slice_in_dim(r_bvnorm, me * cshard, cshard, axis=1),
        r_ws.reshape(b_ws.shape), r_bs.reshape(b_bs.shape), r_fnorm,
        lax.dynamic_slice_in_dim(r_fcw, me * fshard, fshard, axis=2),
        r_fcb, r_final.reshape(final_norm.shape)]
    small_w = [a_norm, a_conv, b_norm, b_vnorm, b_ws, b_bs, f_norm, f_conv_w, f_conv_b, final_norm]
    small_m = [m_a_norm, m_a_conv, m_b_norm, m_b_vnorm, m_b_ws, m_b_bs, m_f_norm, m_f_conv_w, m_f_conv_b, m_final_norm]
    small_v = [v_a_norm, v_a_conv, v_b_norm, v_b_vnorm, v_b_ws, v_b_bs, v_f_norm, v_f_conv_w, v_f_conv_b, v_final_norm]
    shapes = [w.shape for w in small_w]
    packed = _adamw_packed(_pack(small_w), _pack(small_grads), _pack(small_m), _pack(small_v), "adamw_small")
    s_delta, s_m, s_v = (_unpack(p, shapes) for p in packed)
    small_names = ["a_norm", "a_conv", "b_norm", "b_vnorm", "b_ws", "b_bs", "f_norm", "f_conv_w", "f_conv_b", "final_norm"]
    small = {nm: (small_grads[i], s_delta[i], s_m[i], s_v[i]) for i, nm in enumerate(small_names)}

    order = ["a_norm", "a_in", "a_conv", "a_out", "b_norm", "b_in", "b_vnorm", "b_ws", "b_bs", "b_out",
             "f_norm", "f_up", "f_conv_w", "f_conv_b", "f_down", "final_norm"]
    res = {nm: (big[nm] if nm in big else small[nm]) for nm in order}
    outs = [loss_v[0, 0], grad_x.reshape(x.shape)]
    for k in range(4):
        outs += [res[nm][k] for nm in order]
    return tuple(outs)


def _unpack_rows(rows, shapes):
    out, off = [], 0
    for shp in shapes:
        n = math.prod(shp)
        out.append(rows[:, off:off + n].reshape((N_DEV,) + tuple(shp)))
        off += n + (-n) % PACK_GRANULE
    return out
```

```python
import functools
import math

import jax
import jax.numpy as jnp
from jax import lax
from jax.experimental import pallas as pl
from jax.experimental.pallas import tpu as pltpu

F32 = jnp.float32
BF16 = jnp.bfloat16
MESH = pl.DeviceIdType.MESH

N_DEV = 8
RMS_EPS = 1e-5
CHUNK = 128
SG_GROUPS = 8
ADAM_LR = 0.001
ADAM_B1 = 0.9
ADAM_B2 = 0.999
ADAM_EPS = 1e-08
ADAM_WD = 0.01
ADAM_STEP = 10

LANES = 128
SLAB = 16
VMEM_LIMIT = 56 * 1024 * 1024
PACK_GRANULE = 8 * LANES


def _pick(dim, cap, mult):
    best = None
    t = mult
    while t <= min(dim, cap):
        if dim % t == 0:
            best = t
        t += mult
    return dim if best is None else best


def _params(semantics=None):
    return pltpu.CompilerParams(dimension_semantics=semantics, vmem_limit_bytes=VMEM_LIMIT)


_DIMS = {
    "nn": (((1,), (0,)), ((), ())),
    "nt": (((1,), (1,)), ((), ())),
    "tn": (((0,), (0,)), ((), ())),
}


def _matmul(a, b, mode, out_dtype, name, resid=None, after=None, tm_cap=1024, tn_cap=1024, tk_cap=2816):
    if mode == "nn":
        (m, k), n = a.shape, b.shape[1]
    elif mode == "nt":
        (m, k), n = a.shape, b.shape[0]
    else:
        (k, m), n = a.shape, b.shape[1]
    tm, tn, tk = _pick(m, tm_cap, LANES), _pick(n, tn_cap, LANES), _pick(k, tk_cap, LANES)
    nk = k // tk
    n_in = 2 + (resid is not None) + (after is not None)

    def body(*refs):
        a_ref, b_ref = refs[:2]
        r_ref = refs[2] if resid is not None else None
        o_ref = refs[n_in]
        prod = lax.dot_general(a_ref[...], b_ref[...], _DIMS[mode], preferred_element_type=F32)

        def finish(r):
            if r_ref is not None:
                r = r + r_ref[...]
            o_ref[...] = r.astype(out_dtype)

        if nk == 1:
            finish(prod)
            return
        acc_ref = refs[n_in + 1]
        kk = pl.program_id(2)

        @pl.when(kk == 0)
        def _():
            acc_ref[...] = prod

        @pl.when(jnp.logical_and(kk > 0, kk < nk - 1))
        def _():
            acc_ref[...] += prod

        @pl.when(kk == nk - 1)
        def _():
            finish(acc_ref[...] + prod)

    a_spec = (pl.BlockSpec((tk, tm), lambda i, j, kk: (kk, i)) if mode == "tn"
              else pl.BlockSpec((tm, tk), lambda i, j, kk: (i, kk)))
    b_spec = (pl.BlockSpec((tn, tk), lambda i, j, kk: (j, kk)) if mode == "nt"
              else pl.BlockSpec((tk, tn), lambda i, j, kk: (kk, j)))
    o_spec = pl.BlockSpec((tm, tn), lambda i, j, kk: (i, j))
    in_specs = [a_spec, b_spec] + ([o_spec] if resid is not None else [])
    args = (a, b) + ((resid,) if resid is not None else ())
    if after is not None:
        in_specs.append(pl.BlockSpec(memory_space=pl.ANY))
        args += (after,)
    return pl.pallas_call(
        body, name=name, grid=(m // tm, n // tn, nk),
        in_specs=in_specs, out_specs=o_spec,
        out_shape=jax.ShapeDtypeStruct((m, n), out_dtype),
        scratch_shapes=[pltpu.VMEM((tm, tn), F32)] if nk > 1 else [],
        compiler_params=_params(("parallel", "parallel", "arbitrary")),
    )(*args)


def _rms_stats(xf):
    inv = lax.rsqrt(jnp.mean(xf * xf, axis=-1, keepdims=True) + RMS_EPS)
    return inv, xf * inv


def _rmsnorm_fwd(x, g, name, after=None):
    s, d = x.shape
    tm = _pick(s, 256, SLAB)
    extra = () if after is None else (after,)

    def body(x_ref, g_ref, *rest):
        _, xhat = _rms_stats(x_ref[...])
        rest[-1][...] = (xhat * g_ref[...]).astype(BF16)

    return pl.pallas_call(
        body, name=name, grid=(s // tm,),
        in_specs=[pl.BlockSpec((tm, d), lambda i: (i, 0)), pl.BlockSpec((1, d), lambda i: (0, 0))]
        + [pl.BlockSpec(memory_space=pl.ANY)] * len(extra),
        out_specs=pl.BlockSpec((tm, d), lambda i: (i, 0)),
        out_shape=jax.ShapeDtypeStruct((s, d), BF16),
        compiler_params=_params(("parallel",)),
    )(x, g, *extra)


def _rmsnorm_bwd(x, g, dh, dx_out, name):
    s, d = x.shape
    tm = _pick(s, 256, SLAB)

    def body(x_ref, g_ref, dh_ref, dxo_ref, dxi_ref, dxib_ref, dg_ref):
        inv, xhat = _rms_stats(x_ref[...])
        dhv = dh_ref[...]
        dxhat = dhv * g_ref[...]
        proj = jnp.mean(dxhat * xhat, axis=-1, keepdims=True)
        dx = dxo_ref[...] + inv * (dxhat - xhat * proj)
        dxi_ref[...] = dx
        dxib_ref[...] = dx.astype(BF16)
        part = jnp.sum(dhv * xhat, axis=0, keepdims=True)

        @pl.when(pl.program_id(0) == 0)
        def _():
            dg_ref[...] = part

        @pl.when(pl.program_id(0) > 0)
        def _():
            dg_ref[...] += part

    row = pl.BlockSpec((tm, d), lambda i: (i, 0))
    vec = pl.BlockSpec((1, d), lambda i: (0, 0))
    return pl.pallas_call(
        body, name=name, grid=(s // tm,),
        in_specs=[row, vec, row, row], out_specs=[row, row, vec],
        out_shape=[jax.ShapeDtypeStruct((s, d), F32), jax.ShapeDtypeStruct((s, d), BF16),
                   jax.ShapeDtypeStruct((1, d), F32)],
        compiler_params=_params(("arbitrary",)),
    )(x, g, dh, dx_out)


def _final_loss(x, g, target, name):
    s, d = x.shape
    tm = _pick(s, 256, SLAB)

    def body(x_ref, g_ref, t_ref, dx_ref, dxb_ref, loss_ref, dg_ref):
        inv, xhat = _rms_stats(x_ref[...])
        gv = g_ref[...]
        err = xhat * gv - t_ref[...]
        loss = 0.5 * jnp.sum(jnp.mean(err * err, axis=-1, keepdims=True), axis=0, keepdims=True)
        dy = err * (1.0 / d)
        dxhat = dy * gv
        proj = jnp.mean(dxhat * xhat, axis=-1, keepdims=True)
        dx = inv * (dxhat - xhat * proj)
        dx_ref[...] = dx
        dxb_ref[...] = dx.astype(BF16)
        part = jnp.sum(dy * xhat, axis=0, keepdims=True)
        loss_row = jnp.broadcast_to(loss, (1, LANES))

        @pl.when(pl.program_id(0) == 0)
        def _():
            dg_ref[...] = part
            loss_ref[...] = loss_row

        @pl.when(pl.program_id(0) > 0)
        def _():
            dg_ref[...] += part
            loss_ref[...] += loss_row

    row = pl.BlockSpec((tm, d), lambda i: (i, 0))
    vec = pl.BlockSpec((1, d), lambda i: (0, 0))
    return pl.pallas_call(
        body, name=name, grid=(s // tm,),
        in_specs=[row, vec, row],
        out_specs=[row, row, pl.BlockSpec((1, LANES), lambda i: (0, 0)), vec],
        out_shape=[jax.ShapeDtypeStruct((s, d), F32), jax.ShapeDtypeStruct((s, d), BF16),
                   jax.ShapeDtypeStruct((1, LANES), F32), jax.ShapeDtypeStruct((1, d), F32)],
        compiler_params=_params(("arbitrary",)),
    )(x, g, target)


def _shift_down(prev, cur, k):
    ext = jnp.concatenate([prev, cur], axis=0)
    return pltpu.roll(ext, k, 0)[SLAB:, :]


def _shift_up(cur, nxt, k):
    ext = jnp.concatenate([cur, nxt], axis=0)
    return pltpu.roll(ext, 2 * SLAB - k, 0)[:SLAB, :]


def _conv3(w_ref, cols, prev, cur):
    s1 = _shift_down(prev, cur, 1)
    s2 = _shift_down(prev, cur, 2)
    y = w_ref[0:1, cols] * s2 + w_ref[1:2, cols] * s1 + w_ref[2:3, cols] * cur
    return y, s1, s2


def _conv3_t(w_ref, cols, cur, nxt):
    return (w_ref[2:3, cols] * cur + w_ref[1:2, cols] * _shift_up(cur, nxt, 1)
            + w_ref[0:1, cols] * _shift_up(cur, nxt, 2))


def _rows(s):
    return pl.ds(pl.multiple_of(s * SLAB, SLAB), SLAB)


def _halo_specs(tm, width, n_tiles):
    per = tm // SLAB
    prev = pl.BlockSpec((SLAB, width), lambda i: (jnp.maximum(i * per - 1, 0), 0))
    nxt = pl.BlockSpec((SLAB, width), lambda i: (jnp.minimum((i + 1) * per, n_tiles * per - 1), 0))
    return prev, nxt


def _mixa_fwd(bcx, wc, name):
    s, d3 = bcx.shape
    d = d3 // 3
    tm = _pick(s, 256, SLAB)
    w = _pick(d, 512, LANES)
    nslab = tm // SLAB

    def body(t_ref, prev_ref, wc_ref, y_ref, cv_ref):
        first_tile = pl.program_id(0) == 0
        for c in range(d // w):
            cb, cc, cx = (slice(g * d + c * w, g * d + (c + 1) * w) for g in range(3))
            cols = slice(c * w, (c + 1) * w)
            p_halo = prev_ref[:, cc].astype(F32) * prev_ref[:, cx].astype(F32)
            p_halo = jnp.where(first_tile, 0.0, p_halo)

            def slab(si, p_prev):
                r = _rows(si)
                p = t_ref[r, cc].astype(F32) * t_ref[r, cx].astype(F32)
                cv, _, _ = _conv3(wc_ref, cols, p_prev, p)
                cv_ref[r, cols] = cv.astype(BF16)
                y_ref[r, cols] = (t_ref[r, cb].astype(F32) * cv).astype(BF16)
                return p

            lax.fori_loop(0, nslab, slab, p_halo)

    prev_spec, _ = _halo_specs(tm, d3, s // tm)
    row = pl.BlockSpec((tm, d), lambda i: (i, 0))
    return pl.pallas_call(
        body, name=name, grid=(s // tm,),
        in_specs=[pl.BlockSpec((tm, d3), lambda i: (i, 0)), prev_spec, pl.BlockSpec((3, d), lambda i: (0, 0))],
        out_specs=[row, row],
        out_shape=[jax.ShapeDtypeStruct((s, d), BF16)] * 2,
        compiler_params=_params(("parallel",)),
    )(bcx, bcx, wc)


def _mixa_bwd(bcx, cv, dy, wc, name):
    s, d3 = bcx.shape
    d = d3 // 3
    tm = _pick(s, 256, SLAB)
    w = _pick(d, 256, LANES)
    nslab = tm // SLAB
    n_tiles = s // tm

    def body(t_ref, next_ref, cv_ref, dy_ref, dyn_ref, wc_ref, o_ref, dwc_ref, acc_ref):
        i = pl.program_id(0)
        last_tile = i == n_tiles - 1

        @pl.when(i == 0)
        def _():
            acc_ref[...] = jnp.zeros_like(acc_ref)

        for c in range(d // w):
            cb, cc, cx = (slice(g * d + c * w, g * d + (c + 1) * w) for g in range(3))
            cols = slice(c * w, (c + 1) * w)
            dcv_next = jnp.where(last_tile, 0.0, dyn_ref[:, cols].astype(F32) * next_ref[:, cb].astype(F32))

            def slab(j, dcv_nxt):
                r = _rows(nslab - 1 - j)
                gc = t_ref[r, cc].astype(F32)
                xs = t_ref[r, cx].astype(F32)
                dyv = dy_ref[r, cols].astype(F32)
                p = gc * xs
                d0 = dyv * t_ref[r, cb].astype(F32)
                d1 = _shift_up(d0, dcv_nxt, 1)
                d2 = _shift_up(d0, dcv_nxt, 2)
                dp = wc_ref[2:3, cols] * d0 + wc_ref[1:2, cols] * d1 + wc_ref[0:1, cols] * d2
                for k, term in enumerate((d2 * p, d1 * p, d0 * p)):
                    acc_ref[k, :, cols] += term
                o_ref[r, cb] = (dyv * cv_ref[r, cols].astype(F32)).astype(BF16)
                o_ref[r, cc] = (dp * xs).astype(BF16)
                o_ref[r, cx] = (dp * gc).astype(BF16)
                return d0

            lax.fori_loop(0, nslab, slab, dcv_next)

        @pl.when(last_tile)
        def _():
            for k in range(3):
                dwc_ref[k:k + 1, :] = jnp.sum(acc_ref[k], axis=0, keepdims=True)

    _, next_spec = _halo_specs(tm, d3, n_tiles)
    _, next_dy = _halo_specs(tm, d, n_tiles)
    row = pl.BlockSpec((tm, d), lambda i: (i, 0))
    return pl.pallas_call(
        body, name=name, grid=(n_tiles,),
        in_specs=[pl.BlockSpec((tm, d3), lambda i: (i, 0)), next_spec, row, row, next_dy,
                  pl.BlockSpec((3, d), lambda i: (0, 0))],
        out_specs=[pl.BlockSpec((tm, d3), lambda i: (i, 0)), pl.BlockSpec((3, d), lambda i: (0, 0))],
        out_shape=[jax.ShapeDtypeStruct((s, d3), BF16), jax.ShapeDtypeStruct((3, d), F32)],
        scratch_shapes=[pltpu.VMEM((3, SLAB, d), F32)],
        compiler_params=_params(("arbitrary",)),
    )(bcx, bcx, cv, dy, dy, wc)


def _sigmoid(z):
    return 0.5 * jnp.tanh(0.5 * z) + 0.5


def _ffn_fwd(up, cw, cb, name):
    s, f2 = up.shape
    f = f2 // 2
    tm = _pick(s, 256, SLAB)
    w = _pick(f, 512, LANES)
    nslab = tm // SLAB

    def body(t_ref, prev_ref, cw_ref, cb_ref, act_ref, cv_ref):
        first_tile = pl.program_id(0) == 0
        for c in range(f // w):
            cg = slice(c * w, (c + 1) * w)
            ca = slice(f + c * w, f + (c + 1) * w)
            halo = tuple(jnp.where(first_tile, 0.0, prev_ref[:, cs].astype(F32)) for cs in (cg, ca))

            def slab(si, carry):
                r = _rows(si)
                g = t_ref[r, cg].astype(F32)
                a = t_ref[r, ca].astype(F32)
                gcv = _conv3(cw_ref, cg, carry[0], g)[0] + cb_ref[:, cg]
                acv = _conv3(cw_ref, ca, carry[1], a)[0] + cb_ref[:, ca]
                cv_ref[r, cg] = gcv.astype(BF16)
                cv_ref[r, ca] = acv.astype(BF16)
                act_ref[r, cg] = (gcv * _sigmoid(gcv) * acv).astype(BF16)
                return g, a

            lax.fori_loop(0, nslab, slab, halo)

    prev_spec, _ = _halo_specs(tm, f2, s // tm)
    return pl.pallas_call(
        body, name=name, grid=(s // tm,),
        in_specs=[pl.BlockSpec((tm, f2), lambda i: (i, 0)), prev_spec,
                  pl.BlockSpec((3, f2), lambda i: (0, 0)), pl.BlockSpec((1, f2), lambda i: (0, 0))],
        out_specs=[pl.BlockSpec((tm, f), lambda i: (i, 0)), pl.BlockSpec((tm, f2), lambda i: (i, 0))],
        out_shape=[jax.ShapeDtypeStruct((s, f), BF16), jax.ShapeDtypeStruct((s, f2), BF16)],
        compiler_params=_params(("parallel",)),
    )(up, up, cw, cb)


def _ffn_bwd(up, cv, dact, cw, name):
    s, f2 = up.shape
    f = f2 // 2
    tm = _pick(s, 128, SLAB)
    w = _pick(f, 256, LANES)
    nslab = tm // SLAB
    n_tiles = s // tm

    def body(t_ref, cv_ref, cvn_ref, da_ref, dan_ref, cw_ref, o_ref, dcw_ref, dcb_ref, acc_ref):
        i = pl.program_id(0)
        last_tile = i == n_tiles - 1

        @pl.when(i == 0)
        def _():
            acc_ref[...] = jnp.zeros_like(acc_ref)

        for c in range(f // w):
            cg = slice(c * w, (c + 1) * w)
            ca = slice(f + c * w, f + (c + 1) * w)

            def dconv(gcv, acv, dav):
                gcv, acv, dav = gcv.astype(F32), acv.astype(F32), dav.astype(F32)
                sg = _sigmoid(gcv)
                return dav * acv * (sg * (1.0 + gcv * (1.0 - sg))), dav * (gcv * sg)

            nxt = dconv(cvn_ref[:, cg], cvn_ref[:, ca], dan_ref[:, cg])
            nxt = tuple(jnp.where(last_tile, 0.0, v) for v in nxt)

            def slab(j, carry):
                r = _rows(nslab - 1 - j)
                d = dconv(cv_ref[r, cg], cv_ref[r, ca], da_ref[r, cg])
                for half, cs in enumerate((cg, ca)):
                    x = t_ref[r, cs].astype(F32)
                    d0 = d[half]
                    d1 = _shift_up(d0, carry[half], 1)
                    d2 = _shift_up(d0, carry[half], 2)
                    o_ref[r, cs] = (cw_ref[2:3, cs] * d0 + cw_ref[1:2, cs] * d1 + cw_ref[0:1, cs] * d2).astype(BF16)
                    for k, term in enumerate((d2 * x, d1 * x, d0 * x, d0)):
                        acc_ref[k, :, cs] += term
                return d

            lax.fori_loop(0, nslab, slab, nxt)

        @pl.when(last_tile)
        def _():
            for k in range(3):
                dcw_ref[k:k + 1, :] = jnp.sum(acc_ref[k], axis=0, keepdims=True)
            dcb_ref[...] = jnp.sum(acc_ref[3], axis=0, keepdims=True)

    _, next_cv = _halo_specs(tm, f2, n_tiles)
    _, next_da = _halo_specs(tm, f, n_tiles)
    return pl.pallas_call(
        body, name=name, grid=(n_tiles,),
        in_specs=[pl.BlockSpec((tm, f2), lambda i: (i, 0)), pl.BlockSpec((tm, f2), lambda i: (i, 0)), next_cv,
                  pl.BlockSpec((tm, f), lambda i: (i, 0)), next_da, pl.BlockSpec((3, f2), lambda i: (0, 0))],
        out_specs=[pl.BlockSpec((tm, f2), lambda i: (i, 0)), pl.BlockSpec((3, f2), lambda i: (0, 0)),
                   pl.BlockSpec((1, f2), lambda i: (0, 0))],
        out_shape=[jax.ShapeDtypeStruct((s, f2), BF16), jax.ShapeDtypeStruct((3, f2), F32),
                   jax.ShapeDtypeStruct((1, f2), F32)],
        scratch_shapes=[pltpu.VMEM((4, SLAB, f2), F32)],
        compiler_params=_params(("arbitrary",)),
    )(up, cv, cv, dact, dact, cw)


_GELU_C = math.sqrt(2.0 / math.pi)


def _gelu(x):
    th = jnp.tanh(_GELU_C * (x + 0.044715 * (x * x * x)))
    return x * (0.5 * (1.0 + th)), th


def _gelu_grad(x, th):
    return 0.5 * (1.0 + th) + 0.5 * x * (1.0 - th * th) * (_GELU_C * (1.0 + 3.0 * 0.044715 * (x * x)))


def _masked_ws(ws_ref, h):
    t = lax.broadcasted_iota(jnp.int32, (CHUNK, CHUNK), 0)
    sx = lax.broadcasted_iota(jnp.int32, (CHUNK, CHUNK), 1)
    return jnp.where(sx <= t, ws_ref[h], 0.0)


def _mixb_fwd(pre, gv, ws, bs_wide, name):
    s, w2 = pre.shape
    w = w2 // 2
    gw = w // SG_GROUPS

    def body(pre_ref, gv_ref, ws_ref, bs_ref, o_ref):
        zu, _ = _gelu(pre_ref[:, :w].astype(F32))
        zv, _ = _gelu(pre_ref[:, w:].astype(F32))
        _, vhat = _rms_stats(zv)
        vn = (vhat * gv_ref[...]).astype(BF16)
        for h in range(SG_GROUPS):
            cols = slice(h * gw, (h + 1) * gw)
            wsm = _masked_ws(ws_ref, h).astype(BF16)
            gate = jnp.dot(wsm, vn[:, cols], preferred_element_type=F32)
            gate = gate + jnp.tile(bs_ref[h], (1, gw // LANES))
            o_ref[:, cols] = (zu[:, cols] * gate).astype(BF16)

    return pl.pallas_call(
        body, name=name, grid=(s // CHUNK,),
        in_specs=[pl.BlockSpec((CHUNK, w2), lambda i: (i, 0)), pl.BlockSpec((1, w), lambda i: (0, 0)),
                  pl.BlockSpec((SG_GROUPS, CHUNK, CHUNK), lambda i: (0, 0, 0)),
                  pl.BlockSpec((SG_GROUPS, CHUNK, LANES), lambda i: (0, 0, 0))],
        out_specs=pl.BlockSpec((CHUNK, w), lambda i: (i, 0)),
        out_shape=jax.ShapeDtypeStruct((s, w), BF16),
        compiler_params=_params(("parallel",)),
    )(pre, gv, ws, bs_wide)


def _mixb_bwd(pre, dug, gv, ws, bs_wide, name):
    s, w2 = pre.shape
    w = w2 // 2
    gw = w // SG_GROUPS

    def body(pre_ref, dug_ref, gv_ref, ws_ref, bs_ref, o_ref, dws_ref, dbs_ref, dgv_ref, dvn_ref):
        first = pl.program_id(0) == 0

        @pl.when(first)
        def _():
            dws_ref[...] = jnp.zeros_like(dws_ref)
            dbs_ref[...] = jnp.zeros_like(dbs_ref)

        pu = pre_ref[:, :w].astype(F32)
        pv = pre_ref[:, w:].astype(F32)
        zu, thu = _gelu(pu)
        zv, thv = _gelu(pv)
        inv, vhat = _rms_stats(zv)
        gvv = gv_ref[...]
        vn = (vhat * gvv).astype(BF16)
        for h in range(SG_GROUPS):
            cols = slice(h * gw, (h + 1) * gw)
            wsm = _masked_ws(ws_ref, h).astype(BF16)
            gate = jnp.dot(wsm, vn[:, cols], preferred_element_type=F32)
            gate = gate + jnp.tile(bs_ref[h], (1, gw // LANES))
            dug_h = dug_ref[:, cols].astype(F32)
            dgate = dug_h * zu[:, cols]
            dgate_b = dgate.astype(BF16)
            o_ref[:, cols] = (dug_h * gate * _gelu_grad(pu[:, cols], thu[:, cols])).astype(BF16)
            dbs_ref[h] += jnp.broadcast_to(jnp.sum(dgate, axis=-1, keepdims=True), (CHUNK, LANES))
            dws = lax.dot_general(dgate_b, vn[:, cols], _DIMS["nt"], preferred_element_type=F32)
            t = lax.broadcasted_iota(jnp.int32, (CHUNK, CHUNK), 0)
            sx = lax.broadcasted_iota(jnp.int32, (CHUNK, CHUNK), 1)
            dws_ref[h] += jnp.where(sx <= t, dws, 0.0)
            dvn_ref[:, cols] = lax.dot_general(wsm, dgate_b, _DIMS["tn"], preferred_element_type=F32)
        dvn = dvn_ref[...]
        part = jnp.sum(dvn * vhat, axis=0, keepdims=True)

        @pl.when(first)
        def _():
            dgv_ref[...] = part

        @pl.when(jnp.logical_not(first))
        def _():
            dgv_ref[...] += part

        dvhat = dvn * gvv
        dzv = inv * (dvhat - vhat * jnp.mean(dvhat * vhat, axis=-1, keepdims=True))
        o_ref[:, w:] = (dzv * _gelu_grad(pv, thv)).astype(BF16)

    return pl.pallas_call(
        body, name=name, grid=(s // CHUNK,),
        in_specs=[pl.BlockSpec((CHUNK, w2), lambda i: (i, 0)), pl.BlockSpec((CHUNK, w), lambda i: (i, 0)),
                  pl.BlockSpec((1, w), lambda i: (0, 0)),
                  pl.BlockSpec((SG_GROUPS, CHUNK, CHUNK), lambda i: (0, 0, 0)),
                  pl.BlockSpec((SG_GROUPS, CHUNK, LANES), lambda i: (0, 0, 0))],
        out_specs=[pl.BlockSpec((CHUNK, w2), lambda i: (i, 0)),
                   pl.BlockSpec((SG_GROUPS, CHUNK, CHUNK), lambda i: (0, 0, 0)),
                   pl.BlockSpec((SG_GROUPS, CHUNK, LANES), lambda i: (0, 0, 0)),
                   pl.BlockSpec((1, w), lambda i: (0, 0))],
        out_shape=[jax.ShapeDtypeStruct((s, w2), BF16), jax.ShapeDtypeStruct((SG_GROUPS, CHUNK, CHUNK), F32),
                   jax.ShapeDtypeStruct((SG_GROUPS, CHUNK, LANES), F32), jax.ShapeDtypeStruct((1, w), F32)],
        scratch_shapes=[pltpu.VMEM((CHUNK, w), F32)],
        compiler_params=_params(("arbitrary",)),
    )(pre, dug, gv, ws, bs_wide)


def _cast_layer(w3, layer, name):
    _, r, c = w3.shape
    tr = _pick(r, 256, SLAB)

    def body(w_ref, o_ref):
        o_ref[...] = w_ref[...].astype(BF16)

    return pl.pallas_call(
        body, name=name, grid=(r // tr,),
        in_specs=[pl.BlockSpec((None, tr, c), lambda i: (layer, i, 0))],
        out_specs=pl.BlockSpec((tr, c), lambda i: (i, 0)),
        out_shape=jax.ShapeDtypeStruct((r, c), BF16),
        compiler_params=_params(("parallel",)),
    )(w3)


def _adamw_math(w, g, m, v):
    m = ADAM_B1 * m + (1.0 - ADAM_B1) * g
    v = ADAM_B2 * v + (1.0 - ADAM_B2) * (g * g)
    m_hat = m / (1.0 - ADAM_B1 ** ADAM_STEP)
    v_hat = v / (1.0 - ADAM_B2 ** ADAM_STEP)
    delta = -ADAM_LR * (m_hat / (jnp.sqrt(v_hat) + ADAM_EPS) + ADAM_WD * w)
    return delta, m, v


def _adamw_sharded(recvs, w, m, v, name):
    nl, r, c = w.shape
    tc = _pick(c, 1536, LANES)
    tr = _pick(r, 64, SLAB)

    def body(*refs):
        recv_refs = refs[:nl]
        w_ref, m_ref, v_ref, g_ref, d_ref, nm_ref, nv_ref = refs[nl:]
        for layer, recv_ref in enumerate(recv_refs):
            @pl.when(pl.program_id(0) == layer)
            def _():
                g = recv_ref[0].astype(F32)
                for q in range(1, N_DEV):
                    g = g + recv_ref[q].astype(F32)
                delta, nm, nv = _adamw_math(w_ref[...], g, m_ref[...], v_ref[...])
                g_ref[...] = g
                d_ref[...] = delta
                nm_ref[...] = nm
                nv_ref[...] = nv

    def recv_spec(layer):
        return pl.BlockSpec((N_DEV, tr, tc),
                            lambda l, i, j: (0, jnp.where(l == layer, i, 0), jnp.where(l == layer, j, 0)))

    blk = pl.BlockSpec((None, tr, tc), lambda l, i, j: (l, i, j))
    out = jax.ShapeDtypeStruct((nl, r, c), F32)
    return pl.pallas_call(
        body, name=name, grid=(nl, r // tr, c // tc),
        in_specs=[recv_spec(layer) for layer in range(nl)] + [blk, blk, blk],
        out_specs=[blk] * 4, out_shape=[out] * 4,
        compiler_params=_params(("parallel",) * 3),
    )(*recvs, w, m, v)


def _adamw_packed(w, g, m, v, name):
    r, c = w.shape
    tr = _pick(r, 256, 8)

    def body(w_ref, g_ref, m_ref, v_ref, d_ref, nm_ref, nv_ref):
        delta, nm, nv = _adamw_math(w_ref[...], g_ref[...], m_ref[...], v_ref[...])
        d_ref[...] = delta
        nm_ref[...] = nm
        nv_ref[...] = nv

    blk = pl.BlockSpec((tr, c), lambda i: (i, 0))
    out = jax.ShapeDtypeStruct((r, c), F32)
    return pl.pallas_call(
        body, name=name, grid=(r // tr,), in_specs=[blk] * 4, out_specs=[blk] * 3, out_shape=[out] * 3,
        compiler_params=_params(("parallel",)),
    )(w, g, m, v)


def _pack(arrays):
    parts = []
    for a in arrays:
        flat = a.reshape(-1).astype(F32)
        pad = (-flat.shape[0]) % PACK_GRANULE
        parts.append(jnp.pad(flat, (0, pad)) if pad else flat)
    return jnp.concatenate(parts).reshape(-1, LANES)


def _unpack(buf, shapes):
    flat = buf.reshape(-1)
    out, off = [], 0
    for shp in shapes:
        n = math.prod(shp)
        out.append(flat[off:off + n].reshape(shp))
        off += n + (-n) % PACK_GRANULE
    return out


def _mesh_pos():
    return lax.axis_index("x"), lax.axis_index("y"), lax.axis_index("c")


def _coords(q):
    return q // 4, (q // 2) % 2, q % 2


def _shard_of(ref, q, shard_shape, axis):
    r, c = shard_shape
    if axis == 0:
        return ref.at[pl.ds(pl.multiple_of(q * r, SLAB), r), :]
    return ref.at[:, pl.ds(pl.multiple_of(q * c, LANES), c)]


_HBM = pl.BlockSpec(memory_space=pltpu.HBM)
_SEM = pl.BlockSpec(memory_space=pltpu.SEMAPHORE)
_EFFECT = pltpu.SideEffectType.DATAFLOW_SIDE_EFFECTING


def _exchange_shapes(gather, src_shape, axis):
    r, c = src_shape
    if gather:
        return (r, c), ((r * N_DEV, c) if axis == 0 else (r, c * N_DEV))
    shard = (r // N_DEV, c) if axis == 0 else (r, c // N_DEV)
    return shard, (N_DEV,) + shard


def _exchange_copies(gather, src, land, sems, axis):
    send_sems, recv_sems, own_sem = sems
    x, y, c_ = _mesh_pos()
    me = 4 * x + 2 * y + c_
    shard, _ = _exchange_shapes(gather, src.shape, axis)

    def piece(q):
        return src if gather else _shard_of(src, q, shard, axis)

    def place(q):
        return _shard_of(land, q, shard, axis) if gather else land.at[q]

    own = pltpu.make_async_copy(piece(me), place(me), own_sem.at[0])
    sends, arrivals = [], []
    for step in range(1, N_DEV):
        to = (me + step) % N_DEV
        frm = (me + N_DEV - step) % N_DEV
        sends.append(pltpu.make_async_remote_copy(
            src_ref=piece(to), dst_ref=place(me), send_sem=send_sems.at[step - 1], recv_sem=recv_sems.at[step - 1],
            device_id=_coords(to), device_id_type=MESH))
        arrivals.append(pltpu.make_async_remote_copy(
            src_ref=piece(me), dst_ref=place(frm), send_sem=send_sems.at[step - 1], recv_sem=recv_sems.at[step - 1],
            device_id=_coords(frm), device_id_type=MESH))
    return own, sends, arrivals


def _exchange_start(gather, src, axis, name, after=None):
    _, land_shape = _exchange_shapes(gather, src.shape, axis)
    extra = () if after is None else (after,)

    def body(*refs):
        src_ref, land = refs[:2]
        send_sems, recv_sems, own_sem = refs[2 + len(extra):5 + len(extra)]
        own, sends, _ = _exchange_copies(gather, src_ref, land, (send_sems, recv_sems, own_sem), axis)
        own.start()
        for cp in sends:
            cp.start()
        refs[-1][...] = jnp.zeros_like(refs[-1])

    out = pl.pallas_call(
        body, name=name,
        out_shape=(pltpu.SemaphoreType.DMA((N_DEV - 1,)), pltpu.SemaphoreType.DMA((N_DEV - 1,)),
                   pltpu.SemaphoreType.DMA((1,)), pltpu.HBM(src.shape, src.dtype),
                   pltpu.HBM(land_shape, src.dtype), jax.ShapeDtypeStruct((8, LANES), F32)),
        in_specs=[_HBM, _HBM] + [pl.BlockSpec(memory_space=pl.ANY)] * len(extra),
        out_specs=(_SEM, _SEM, _SEM, _HBM, _HBM, pl.BlockSpec(memory_space=pltpu.VMEM)),
        input_output_aliases={0: 3, 1: 4},
        compiler_params=pltpu.CompilerParams(has_side_effects=_EFFECT),
    )(pltpu.with_memory_space_constraint(src, pltpu.HBM),
      pltpu.with_memory_space_constraint(lax.empty(land_shape, src.dtype), pltpu.HBM), *extra)
    return out[:5], out[5]


def _exchange_wait(gather, state, axis, after, name):
    send_sems, recv_sems, own_sem, src_thru, land_thru = state
    after = tuple(after) if isinstance(after, (tuple, list)) else (after,)

    def body(src, land, send_sems, recv_sems, own_sem, *rest):
        own, sends, arrivals = _exchange_copies(gather, src, land, (send_sems, recv_sems, own_sem), axis)
        for cp in sends:
            cp.wait_send()
        for cp in arrivals:
            cp.wait_recv()
        own.wait()

    return pl.pallas_call(
        body, name=name,
        out_shape=(pltpu.HBM(src_thru.shape, src_thru.dtype), pltpu.HBM(land_thru.shape, land_thru.dtype)),
        in_specs=[_HBM, _HBM, _SEM, _SEM, _SEM] + [pl.BlockSpec(memory_space=pl.ANY)] * len(after),
        out_specs=(_HBM, _HBM),
        input_output_aliases={0: 0, 1: 1},
        compiler_params=pltpu.CompilerParams(has_side_effects=_EFFECT),
    )(src_thru, land_thru, send_sems, recv_sems, own_sem, *after)[1]


def _gather2_copies(shard_ref, land, sems, axis, shard_shape):
    send1, recv1, own_sem, send2, recv2 = sems
    x, y, c = _mesh_pos()
    me, sibling = (x, y, c), (x, y, 1 - c)
    chips = [(1 - x, y), (x, 1 - y), (1 - x, 1 - y)]

    def region(dev):
        px, py, pc = dev
        return _shard_of(land, 4 * px + 2 * py + pc, shard_shape, axis)

    def copy(src, block, to, send, recv):
        return pltpu.make_async_remote_copy(src_ref=src, dst_ref=region(block), send_sem=send, recv_sem=recv,
                                            device_id=to, device_id_type=MESH)

    own = pltpu.make_async_copy(shard_ref, region(me), own_sem.at[0])
    peers = [sibling] + [(*chip, c) for chip in chips]
    sends1 = [copy(shard_ref, me, to, send1.at[k], recv1.at[k]) for k, to in enumerate(peers)]
    arrivals1 = [copy(shard_ref, frm, frm, send1.at[k], recv1.at[k]) for k, frm in enumerate(peers)]
    sends2, arrivals2 = [], []
    if send2 is not None:
        for j, chip in enumerate(chips):
            sends2.append(copy(region((*chip, c)), (*chip, c), sibling, send2.at[j], recv2.at[j]))
            arrivals2.append(copy(region((*chip, 1 - c)), (*chip, 1 - c), sibling, send2.at[j], recv2.at[j]))
    return own, sends1, arrivals1, sends2, arrivals2


def _gather2_start(shard, axis, name, after=None):
    _, land_shape = _exchange_shapes(True, shard.shape, axis)
    extra = () if after is None else (after,)

    def body(*refs):
        src_ref, land = refs[:2]
        send1, recv1, own_sem = refs[2 + len(extra):5 + len(extra)]
        own, sends1, _, _, _ = _gather2_copies(src_ref, land, (send1, recv1, own_sem, None, None), axis, shard.shape)
        own.start()
        for cp in sends1[1:] + sends1[:1]:
            cp.start()
        refs[-1][...] = jnp.zeros_like(refs[-1])

    out = pl.pallas_call(
        body, name=name,
        out_shape=(pltpu.SemaphoreType.DMA((4,)), pltpu.SemaphoreType.DMA((4,)), pltpu.SemaphoreType.DMA((1,)),
                   pltpu.HBM(shard.shape, shard.dtype), pltpu.HBM(land_shape, shard.dtype),
                   jax.ShapeDtypeStruct((8, LANES), F32)),
        in_specs=[_HBM, _HBM] + [pl.BlockSpec(memory_space=pl.ANY)] * len(extra),
        out_specs=(_SEM, _SEM, _SEM, _HBM, _HBM, pl.BlockSpec(memory_space=pltpu.VMEM)),
        input_output_aliases={0: 3, 1: 4},
        compiler_params=pltpu.CompilerParams(has_side_effects=_EFFECT),
    )(pltpu.with_memory_space_constraint(shard, pltpu.HBM),
      pltpu.with_memory_space_constraint(lax.empty(land_shape, shard.dtype), pltpu.HBM), *extra)
    return out[:5], out[5]


def _gather2_pass(state, axis, after, name):
    send1, recv1, own_sem, shard_thru, land_thru = state

    def body(src_ref, land, send1, recv1, own_sem, after_ref, send2, recv2, src_out, land_out, token):
        _, _, arrivals1, sends2, _ = _gather2_copies(src_ref, land, (send1, recv1, own_sem, send2, recv2), axis,
                                                     shard_thru.shape)
        for arrival, fwd in zip(arrivals1[1:], sends2):
            arrival.wait_recv()
            fwd.start()
        token[...] = jnp.zeros_like(token)

    out = pl.pallas_call(
        body, name=name,
        out_shape=(pltpu.SemaphoreType.DMA((3,)), pltpu.SemaphoreType.DMA((3,)),
                   pltpu.HBM(shard_thru.shape, shard_thru.dtype), pltpu.HBM(land_thru.shape, land_thru.dtype),
                   jax.ShapeDtypeStruct((8, LANES), F32)),
        in_specs=[_HBM, _HBM, _SEM, _SEM, _SEM, pl.BlockSpec(memory_space=pl.ANY)],
        out_specs=(_SEM, _SEM, _HBM, _HBM, pl.BlockSpec(memory_space=pltpu.VMEM)),
        input_output_aliases={0: 2, 1: 3},
        compiler_params=pltpu.CompilerParams(has_side_effects=_EFFECT),
    )(shard_thru, land_thru, send1, recv1, own_sem, after)
    return (send1, recv1, own_sem, out[0], out[1], out[2], out[3]), out[4]


def _gather2_wait(state, axis, after, name):
    send1, recv1, own_sem, send2, recv2, shard_thru, land_thru = state

    def body(src_ref, land, send1, recv1, own_sem, send2, recv2, after_ref, src_dead, got):
        own, sends1, arrivals1, sends2, arrivals2 = _gather2_copies(
            src_ref, land, (send1, recv1, own_sem, send2, recv2), axis, shard_thru.shape)
        for cp in sends1 + sends2:
            cp.wait_send()
        for cp in arrivals1[:1] + arrivals2:
            cp.wait_recv()
        own.wait()

    return pl.pallas_call(
        body, name=name,
        out_shape=(pltpu.HBM(shard_thru.shape, shard_thru.dtype), pltpu.HBM(land_thru.shape, land_thru.dtype)),
        in_specs=[_HBM, _HBM] + [_SEM] * 5 + [pl.BlockSpec(memory_space=pl.ANY)],
        out_specs=(_HBM, _HBM),
        input_output_aliases={0: 0, 1: 1},
        compiler_params=pltpu.CompilerParams(has_side_effects=_EFFECT),
    )(shard_thru, land_thru, send1, recv1, own_sem, send2, recv2, after)[1]


def _sum_slots(slots, name):
    _, r, c = slots.shape
    tr = _pick(r, 512, 8)

    def body(s_ref, o_ref):
        total = s_ref[0]
        for q in range(1, N_DEV):
            total = total + s_ref[q]
        o_ref[...] = total

    return pl.pallas_call(
        body, name=name, grid=(r // tr,),
        in_specs=[pl.BlockSpec((N_DEV, tr, c), lambda i: (0, i, 0))],
        out_specs=pl.BlockSpec((tr, c), lambda i: (i, 0)),
        out_shape=jax.ShapeDtypeStruct((r, c), F32),
        compiler_params=_params(("parallel",)),
    )(slots)


def kernel(x, a_norm, a_in, a_conv, a_out, b_norm, b_in, b_vnorm, b_ws, b_bs, b_out, f_norm, f_up, f_conv_w, f_conv_b, f_down, final_norm, loss_target, m_a_norm, m_a_in, m_a_conv, m_a_out, m_b_norm, m_b_in, m_b_vnorm, m_b_ws, m_b_bs, m_b_out, m_f_norm, m_f_up, m_f_conv_w, m_f_conv_b, m_f_down, m_final_norm, v_a_norm, v_a_in, v_a_conv, v_a_out, v_b_norm, v_b_in, v_b_vnorm, v_b_ws, v_b_bs, v_b_out, v_f_norm, v_f_up, v_f_conv_w, v_f_conv_b, v_f_down, v_final_norm):
    s, d = x.shape[1], x.shape[2]
    n_ffn = f_up.shape[0]
    f2 = f_up.shape[2] * N_DEV
    me = 4 * lax.axis_index("x") + 2 * lax.axis_index("y") + lax.axis_index("c")
    x0 = x.reshape(s, d)
    target = loss_target.reshape(s, d)

    wanted = [("a_in", _cast_layer(a_in, 0, "cast_a_in"), 1),
              ("small", _pack([a_conv, b_norm, b_vnorm, f_conv_w]), 0),
              ("a_out", _cast_layer(a_out, 0, "cast_a_out"), 0),
              ("f_up0", _cast_layer(f_up, 0, "cast_f_up0"), 1), ("f_down0", _cast_layer(f_down, 0, "cast_f_down0"), 0),
              ("b_in", _cast_layer(b_in, 0, "cast_b_in"), 1), ("b_out", _cast_layer(b_out, 0, "cast_b_out"), 0),
              ("f_up1", _cast_layer(f_up, 1, "cast_f_up1"), 1), ("f_down1", _cast_layer(f_down, 1, "cast_f_down1"), 0)]
    coming, tok, h0 = {}, None, None
    for n_started, (key, shard, axis) in enumerate(wanted):
        if n_started == 2:
            tok = h0 = _rmsnorm_fwd(x0, a_norm, "mixa_norm", after=tok)
        state, tok = _gather2_start(shard, axis, f"ag_start_{key}", after=tok)
        coming[key] = (state, axis)

    def pass_on(keys, after):
        for key in keys:
            state, axis = coming[key]
            state, after = _gather2_pass(state, axis, after, f"ag_pass_{key}")
            coming[key] = (state, axis)
        return after

    def arrived(key, after):
        state, axis = coming[key]
        return _gather2_wait(state, axis, after, f"ag_wait_{key}")

    cshard = a_conv.shape[2]
    fshard = f_conv_w.shape[2]
    w_a_in = arrived("a_in", pass_on(["a_in", "small"], tok))
    small_full = arrived("small", w_a_in)
    small_rows = small_full.reshape(N_DEV, -1)
    per_dev = _unpack_rows(small_rows, [(3, cshard), (cshard,), (cshard,), (n_ffn, 3, fshard)])
    a_conv_full = per_dev[0].transpose(1, 0, 2).reshape(3, d)
    b_norm_full = per_dev[1].reshape(1, d)
    b_vnorm_full = per_dev[2].reshape(1, d)
    f_conv_w_full = per_dev[3].transpose(1, 2, 0, 3).reshape(n_ffn, 3, f2)
    bs_wide = jnp.broadcast_to(b_bs[0][:, :, None], (SG_GROUPS, CHUNK, LANES))
    ws = b_ws[0]

    w_f_up, w_f_down = {}, {}

    def ffn_forward(xin, l, pass_early, pass_late):
        h = _rmsnorm_fwd(xin, f_norm[l:l + 1], f"ffn{l}_norm")
        w_f_up[l] = arrived(f"f_up{l}", h)
        up = _matmul(h, w_f_up[l], "nn", BF16, f"ffn{l}_up", after=pass_on(pass_early, w_f_up[l]))
        act, cv = _ffn_fwd(up, f_conv_w_full[l], f_conv_b[l:l + 1], f"ffn{l}_mid")
        w_f_down[l] = arrived(f"f_down{l}", pass_on(pass_late, act))
        xout = _matmul(act, w_f_down[l], "nn", F32, f"ffn{l}_down", resid=xin, tk_cap=1408)
        return xout, (h, up, act, cv)

    bcx = _matmul(h0, w_a_in, "nn", BF16, "mixa_in")
    ya, cva = _mixa_fwd(bcx, a_conv_full, "mixa_mid")
    w_a_out = arrived("a_out", pass_on(["a_out", "f_up0"], bcx))
    x1 = _matmul(ya, w_a_out, "nn", F32, "mixa_out", resid=x0)
    x2, saved0 = ffn_forward(x1, 0, ["f_down0"], ["b_in", "b_out", "f_up1", "f_down1"])
    h2 = _rmsnorm_fwd(x2, b_norm_full, "mixb_norm")
    w_b_in = arrived("b_in", h2)
    pre = _matmul(h2, w_b_in, "nn", BF16, "mixb_in")
    ug = _mixb_fwd(pre, b_vnorm_full, ws, bs_wide, "mixb_mid")
    w_b_out = arrived("b_out", ug)
    x3 = _matmul(ug, w_b_out, "nn", F32, "mixb_out", resid=x2)
    x4, saved1 = ffn_forward(x3, 1, [], [])
    dx4, dx4b, loss_part, g_final = _final_loss(x4, final_norm.reshape(1, d), target, "loss_head")

    def _rs_start(grad, axis, name):
        return _exchange_start(False, grad, axis, name)

    def ffn_backward(xin, l, saved, dx, dxb):
        h, up, act, cv = saved
        g_down = _matmul(act, dxb, "tn", BF16, f"ffn{l}_down_dw", tm_cap=1408)
        rs_down, tok = _rs_start(g_down, 0, f"rs_start_f_down{l}")
        dact = _matmul(dxb, w_f_down[l], "nt", BF16, f"ffn{l}_down_dx", after=tok, tn_cap=1408)
        dup, g_cw, g_cb = _ffn_bwd(up, cv, dact, f_conv_w_full[l], f"ffn{l}_mid_bwd")
        g_up = _matmul(h, dup, "tn", BF16, f"ffn{l}_up_dw", tk_cap=4096)
        rs_up, tok = _rs_start(g_up, 1, f"rs_start_f_up{l}")
        dh = _matmul(dup, w_f_up[l], "nt", F32, f"ffn{l}_up_dx", after=tok)
        dxin, dxinb, g_norm = _rmsnorm_bwd(xin, f_norm[l:l + 1], dh, dx, f"ffn{l}_norm_bwd")
        return dxin, dxinb, (rs_up, rs_down, g_cw, g_cb, g_norm)

    dx3, dx3b, gf1 = ffn_backward(x3, 1, saved1, dx4, dx4b)
    g_b_out = _matmul(ug, dx3b, "tn", BF16, "mixb_out_dw", tk_cap=4096)
    rs_b_out, tok = _rs_start(g_b_out, 0, "rs_start_b_out")
    dug = _matmul(dx3b, w_b_out, "nt", BF16, "mixb_out_dx", after=tok)
    dpre, g_ws, g_bs_wide, g_bvnorm = _mixb_bwd(pre, dug, b_vnorm_full, ws, bs_wide, "mixb_mid_bwd")
    g_b_in = _matmul(h2, dpre, "tn", BF16, "mixb_in_dw", tk_cap=4096)
    rs_b_in, tok = _rs_start(g_b_in, 1, "rs_start_b_in")
    dh2 = _matmul(dpre, w_b_in, "nt", F32, "mixb_in_dx", after=tok)
    dx2, dx2b, g_bnorm = _rmsnorm_bwd(x2, b_norm_full, dh2, dx3, "mixb_norm_bwd")
    dx1, dx1b, gf0 = ffn_backward(x1, 0, saved0, dx2, dx2b)
    g_a_out = _matmul(ya, dx1b, "tn", BF16, "mixa_out_dw", tk_cap=4096)
    rs_a_out, tok = _rs_start(g_a_out, 0, "rs_start_a_out")
    dya = _matmul(dx1b, w_a_out, "nt", BF16, "mixa_out_dx", after=tok)
    dbcx, g_aconv = _mixa_bwd(bcx, cva, dya, a_conv_full, "mixa_mid_bwd")
    g_a_in = _matmul(h0, dbcx, "tn", BF16, "mixa_in_dw", tk_cap=4096)
    rs_a_in, tok = _rs_start(g_a_in, 1, "rs_start_a_in")
    dh0 = _matmul(dbcx, w_a_in, "nt", F32, "mixa_in_dx", after=tok)
    grad_x, _, g_anorm = _rmsnorm_bwd(x0, a_norm, dh0, dx1, "mixa_norm_bwd")

    full_shapes = [(1, LANES), (1, d), (3, d), (1, d), (1, d), (SG_GROUPS, CHUNK, CHUNK), (SG_GROUPS, CHUNK),
                   (n_ffn, d), (n_ffn, 3, f2), (n_ffn, f2), (1, d)]
    parts = [loss_part, g_anorm, g_aconv, g_bnorm, g_bvnorm, g_ws, g_bs_wide[:, :, 0],
             jnp.concatenate([gf0[4], gf1[4]], axis=0), jnp.stack([gf0[2], gf1[2]]),
             jnp.concatenate([gf0[3], gf1[3]], axis=0), g_final]
    small_part = _pack(parts)
    small_state, small_tok = _exchange_start(True, small_part, 0, "ar_start_small", after=grad_x)

    big = {}
    for name, states, axis, w, m, v in (
            ("f_down", (gf0[1], gf1[1]), 0, f_down, m_f_down, v_f_down),
            ("f_up", (gf0[0], gf1[0]), 1, f_up, m_f_up, v_f_up),
            ("b_out", (rs_b_out,), 0, b_out, m_b_out, v_b_out), ("b_in", (rs_b_in,), 1, b_in, m_b_in, v_b_in),
            ("a_out", (rs_a_out,), 0, a_out, m_a_out, v_a_out), ("a_in", (rs_a_in,), 1, a_in, m_a_in, v_a_in)):
        recvs = [_exchange_wait(False, st, axis, small_tok, f"rs_wait_{name}{l}") for l, st in enumerate(states)]
        big[name] = _adamw_sharded(recvs, w, m, v, f"adamw_{name}")

    slots = _exchange_wait(True, small_state, 0, [res[0] for res in big.values()], "ar_wait_small")
    total = _sum_slots(slots.reshape((N_DEV,) + small_part.shape), "ar_sum_small")
    (loss_v, r_anorm, r_aconv, r_bnorm, r_bvnorm, r_ws, r_bs, r_fnorm, r_fcw, r_fcb, r_final) = _unpack(total, full_shapes)
    small_grads = [
        r_anorm,
        lax.dynamic_slice_in_dim(r_aconv, me * cshard, cshard, axis=1).reshape(a_conv.shape),
        lax.dynamic_slice_in_dim(r_bnorm, me * cshard, cshard, axis=1),
        lax.dynamic_slice_in_dim(r_bvnorm, me * cshard, cshard, axis=1),
        r_ws.reshape(b_ws.shape), r_bs.reshape(b_bs.shape), r_fnorm,
        lax.dynamic_slice_in_dim(r_fcw, me * fshard, fshard, axis=2),
        r_fcb, r_final.reshape(final_norm.shape)]
    small_w = [a_norm, a_conv, b_norm, b_vnorm, b_ws, b_bs, f_norm, f_conv_w, f_conv_b, final_norm]
    small_m = [m_a_norm, m_a_conv, m_b_norm, m_b_vnorm, m_b_ws, m_b_bs, m_f_norm, m_f_conv_w, m_f_conv_b, m_final_norm]
    small_v = [v_a_norm, v_a_conv, v_b_norm, v_b_vnorm, v_b_ws, v_b_bs, v_f_norm, v_f_conv_w, v_f_conv_b, v_final_norm]
    shapes = [w.shape for w in small_w]
    packed = _adamw_packed(_pack(small_w), _pack(small_grads), _pack(small_m), _pack(small_v), "adamw_small")
    s_delta, s_m, s_v = (_unpack(p, shapes) for p in packed)
    small_names = ["a_norm", "a_conv", "b_norm", "b_vnorm", "b_ws", "b_bs", "f_norm", "f_conv_w", "f_conv_b", "final_norm"]
    small = {nm: (small_grads[i], s_delta[i], s_m[i], s_v[i]) for i, nm in enumerate(small_names)}

    order = ["a_norm", "a_in", "a_conv", "a_out", "b_norm", "b_in", "b_vnorm", "b_ws", "b_bs", "b_out",
             "f_norm", "f_up", "f_conv_w", "f_conv_b", "f_down", "final_norm"]
    res = {nm: (big[nm] if nm in big else small[nm]) for nm in order}
    outs = [loss_v[0, 0], grad_x.reshape(x.shape)]
    for k in range(4):
        outs += [res[nm][k] for nm in order]
    return tuple(outs)


def _unpack_rows(rows, shapes):
    out, off = [], 0
    for shp in shapes:
        n = math.prod(shp)
        out.append(rows[:, off:off + n].reshape((N_DEV,) + tuple(shp)))
        off += n + (-n) % PACK_GRANULE
    return out
```

```python
import functools
import math

import jax
import jax.numpy as jnp
from jax import lax
from jax.experimental import pallas as pl
from jax.experimental.pallas import tpu as pltpu

F32 = jnp.float32
BF16 = jnp.bfloat16
MESH = pl.DeviceIdType.MESH

N_DEV = 8
RMS_EPS = 1e-5
CHUNK = 128
SG_GROUPS = 8
ADAM_LR = 0.001
ADAM_B1 = 0.9
ADAM_B2 = 0.999
ADAM_EPS = 1e-08
ADAM_WD = 0.01
ADAM_STEP = 10

LANES = 128
SLAB = 16
VMEM_LIMIT = 56 * 1024 * 1024
PACK_GRANULE = 8 * LANES


def _pick(dim, cap, mult):
    best = None
    t = mult
    while t <= min(dim, cap):
        if dim % t == 0:
            best = t
        t += mult
    return dim if best is None else best


def _params(semantics=None):
    return pltpu.CompilerParams(dimension_semantics=semantics, vmem_limit_bytes=VMEM_LIMIT)


_DIMS = {
    "nn": (((1,), (0,)), ((), ())),
    "nt": (((1,), (1,)), ((), ())),
    "tn": (((0,), (0,)), ((), ())),
}


def _matmul(a, b, mode, out_dtype, name, resid=None, after=None, tm_cap=1024, tn_cap=1024, tk_cap=2816):
    if mode == "nn":
        (m, k), n = a.shape, b.shape[1]
    elif mode == "nt":
        (m, k), n = a.shape, b.shape[0]
    else:
        (k, m), n = a.shape, b.shape[1]
    tm, tn, tk = _pick(m, tm_cap, LANES), _pick(n, tn_cap, LANES), _pick(k, tk_cap, LANES)
    nk = k // tk
    n_in = 2 + (resid is not None) + (after is not None)

    def body(*refs):
        a_ref, b_ref = refs[:2]
        r_ref = refs[2] if resid is not None else None
        o_ref = refs[n_in]
        prod = lax.dot_general(a_ref[...], b_ref[...], _DIMS[mode], preferred_element_type=F32)

        def finish(r):
            if r_ref is not None:
                r = r + r_ref[...]
            o_ref[...] = r.astype(out_dtype)

        if nk == 1:
            finish(prod)
            return
        acc_ref = refs[n_in + 1]
        kk = pl.program_id(2)

        @pl.when(kk == 0)
        def _():
            acc_ref[...] = prod

        @pl.when(jnp.logical_and(kk > 0, kk < nk - 1))
        def _():
            acc_ref[...] += prod

        @pl.when(kk == nk - 1)
        def _():
            finish(acc_ref[...] + prod)

    a_spec = (pl.BlockSpec((tk, tm), lambda i, j, kk: (kk, i)) if mode == "tn"
              else pl.BlockSpec((tm, tk), lambda i, j, kk: (i, kk)))
    b_spec = (pl.BlockSpec((tn, tk), lambda i, j, kk: (j, kk)) if mode == "nt"
              else pl.BlockSpec((tk, tn), lambda i, j, kk: (kk, j)))
    o_spec = pl.BlockSpec((tm, tn), lambda i, j, kk: (i, j))
    in_specs = [a_spec, b_spec] + ([o_spec] if resid is not None else [])
    args = (a, b) + ((resid,) if resid is not None else ())
    if after is not None:
        in_specs.append(pl.BlockSpec(memory_space=pl.ANY))
        args += (after,)
    return pl.pallas_call(
        body, name=name, grid=(m // tm, n // tn, nk),
        in_specs=in_specs, out_specs=o_spec,
        out_shape=jax.ShapeDtypeStruct((m, n), out_dtype),
        scratch_shapes=[pltpu.VMEM((tm, tn), F32)] if nk > 1 else [],
        compiler_params=_params(("parallel", "parallel", "arbitrary")),
    )(*args)


def _rms_stats(xf):
    inv = lax.rsqrt(jnp.mean(xf * xf, axis=-1, keepdims=True) + RMS_EPS)
    return inv, xf * inv


def _rmsnorm_fwd(x, g, name, after=None):
    s, d = x.shape
    tm = _pick(s, 256, SLAB)
    extra = () if after is None else (after,)

    def body(x_ref, g_ref, *rest):
        _, xhat = _rms_stats(x_ref[...])
        rest[-1][...] = (xhat * g_ref[...]).astype(BF16)

    return pl.pallas_call(
        body, name=name, grid=(s // tm,),
        in_specs=[pl.BlockSpec((tm, d), lambda i: (i, 0)), pl.BlockSpec((1, d), lambda i: (0, 0))]
        + [pl.BlockSpec(memory_space=pl.ANY)] * len(extra),
        out_specs=pl.BlockSpec((tm, d), lambda i: (i, 0)),
        out_shape=jax.ShapeDtypeStruct((s, d), BF16),
        compiler_params=_params(("parallel",)),
    )(x, g, *extra)


def _rmsnorm_bwd(x, g, dh, dx_out, name):
    s, d = x.shape
    tm = _pick(s, 256, SLAB)

    def body(x_ref, g_ref, dh_ref, dxo_ref, dxi_ref, dxib_ref, dg_ref):
        inv, xhat = _rms_stats(x_ref[...])
        dhv = dh_ref[...]
        dxhat = dhv * g_ref[...]
        proj = jnp.mean(dxhat * xhat, axis=-1, keepdims=True)
        dx = dxo_ref[...] + inv * (dxhat - xhat * proj)
        dxi_ref[...] = dx
        dxib_ref[...] = dx.astype(BF16)
        part = jnp.sum(dhv * xhat, axis=0, keepdims=True)

        @pl.when(pl.program_id(0) == 0)
        def _():
            dg_ref[...] = part

        @pl.when(pl.program_id(0) > 0)
        def _():
            dg_ref[...] += part

    row = pl.BlockSpec((tm, d), lambda i: (i, 0))
    vec = pl.BlockSpec((1, d), lambda i: (0, 0))
    return pl.pallas_call(
        body, name=name, grid=(s // tm,),
        in_specs=[row, vec, row, row], out_specs=[row, row, vec],
        out_shape=[jax.ShapeDtypeStruct((s, d), F32), jax.ShapeDtypeStruct((s, d), BF16),
                   jax.ShapeDtypeStruct((1, d), F32)],
        compiler_params=_params(("arbitrary",)),
    )(x, g, dh, dx_out)


def _final_loss(x, g, target, name):
    s, d = x.shape
    tm = _pick(s, 256, SLAB)

    def body(x_ref, g_ref, t_ref, dx_ref, dxb_ref, loss_ref, dg_ref):
        inv, xhat = _rms_stats(x_ref[...])
        gv = g_ref[...]
        err = xhat * gv - t_ref[...]
        loss = 0.5 * jnp.sum(jnp.mean(err * err, axis=-1, keepdims=True), axis=0, keepdims=True)
        dy = err * (1.0 / d)
        dxhat = dy * gv
        proj = jnp.mean(dxhat * xhat, axis=-1, keepdims=True)
        dx = inv * (dxhat - xhat * proj)
        dx_ref[...] = dx
        dxb_ref[...] = dx.astype(BF16)
        part = jnp.sum(dy * xhat, axis=0, keepdims=True)
        loss_row = jnp.broadcast_to(loss, (1, LANES))

        @pl.when(pl.program_id(0) == 0)
        def _():
            dg_ref[...] = part
            loss_ref[...] = loss_row

        @pl.when(pl.program_id(0) > 0)
        def _():
            dg_ref[...] += part
            loss_ref[...] += loss_row

    row = pl.BlockSpec((tm, d), lambda i: (i, 0))
    vec = pl.BlockSpec((1, d), lambda i: (0, 0))
    return pl.pallas_call(
        body, name=name, grid=(s // tm,),
        in_specs=[row, vec, row],
        out_specs=[row, row, pl.BlockSpec((1, LANES), lambda i: (0, 0)), vec],
        out_shape=[jax.ShapeDtypeStruct((s, d), F32), jax.ShapeDtypeStruct((s, d), BF16),
                   jax.ShapeDtypeStruct((1, LANES), F32), jax.ShapeDtypeStruct((1, d), F32)],
        compiler_params=_params(("arbitrary",)),
    )(x, g, target)


def _shift_down(prev, cur, k):
    ext = jnp.concatenate([prev, cur], axis=0)
    return pltpu.roll(ext, k, 0)[SLAB:, :]


def _shift_up(cur, nxt, k):
    ext = jnp.concatenate([cur, nxt], axis=0)
    return pltpu.roll(ext, 2 * SLAB - k, 0)[:SLAB, :]


def _conv3(w_ref, cols, prev, cur):
    s1 = _shift_down(prev, cur, 1)
    s2 = _shift_down(prev, cur, 2)
    y = w_ref[0:1, cols] * s2 + w_ref[1:2, cols] * s1 + w_ref[2:3, cols] * cur
    return y, s1, s2


def _conv3_t(w_ref, cols, cur, nxt):
    return (w_ref[2:3, cols] * cur + w_ref[1:2, cols] * _shift_up(cur, nxt, 1)
            + w_ref[0:1, cols] * _shift_up(cur, nxt, 2))


def _rows(s):
    return pl.ds(pl.multiple_of(s * SLAB, SLAB), SLAB)


def _halo_specs(tm, width, n_tiles):
    per = tm // SLAB
    prev = pl.BlockSpec((SLAB, width), lambda i: (jnp.maximum(i * per - 1, 0), 0))
    nxt = pl.BlockSpec((SLAB, width), lambda i: (jnp.minimum((i + 1) * per, n_tiles * per - 1), 0))
    return prev, nxt


def _mixa_fwd(bcx, wc, name):
    s, d3 = bcx.shape
    d = d3 // 3
    tm = _pick(s, 256, SLAB)
    w = _pick(d, 512, LANES)
    nslab = tm // SLAB

    def body(t_ref, prev_ref, wc_ref, y_ref, cv_ref):
        first_tile = pl.program_id(0) == 0
        for c in range(d // w):
            cb, cc, cx = (slice(g * d + c * w, g * d + (c + 1) * w) for g in range(3))
            cols = slice(c * w, (c + 1) * w)
            p_halo = prev_ref[:, cc].astype(F32) * prev_ref[:, cx].astype(F32)
            p_halo = jnp.where(first_tile, 0.0, p_halo)

            def slab(si, p_prev):
                r = _rows(si)
                p = t_ref[r, cc].astype(F32) * t_ref[r, cx].astype(F32)
                cv, _, _ = _conv3(wc_ref, cols, p_prev, p)
                cv_ref[r, cols] = cv.astype(BF16)
                y_ref[r, cols] = (t_ref[r, cb].astype(F32) * cv).astype(BF16)
                return p

            lax.fori_loop(0, nslab, slab, p_halo)

    prev_spec, _ = _halo_specs(tm, d3, s // tm)
    row = pl.BlockSpec((tm, d), lambda i: (i, 0))
    return pl.pallas_call(
        body, name=name, grid=(s // tm,),
        in_specs=[pl.BlockSpec((tm, d3), lambda i: (i, 0)), prev_spec, pl.BlockSpec((3, d), lambda i: (0, 0))],
        out_specs=[row, row],
        out_shape=[jax.ShapeDtypeStruct((s, d), BF16)] * 2,
        compiler_params=_params(("parallel",)),
    )(bcx, bcx, wc)


def _mixa_bwd(bcx, cv, dy, wc, name):
    s, d3 = bcx.shape
    d = d3 // 3
    tm = _pick(s, 256, SLAB)
    w = _pick(d, 256, LANES)
    nslab = tm // SLAB
    n_tiles = s // tm

    def body(t_ref, next_ref, cv_ref, dy_ref, dyn_ref, wc_ref, o_ref, dwc_ref, acc_ref):
        i = pl.program_id(0)
        last_tile = i == n_tiles - 1

        @pl.when(i == 0)
        def _():
            acc_ref[...] = jnp.zeros_like(acc_ref)

        for c in range(d // w):
            cb, cc, cx = (slice(g * d + c * w, g * d + (c + 1) * w) for g in range(3))
            cols = slice(c * w, (c + 1) * w)
            dcv_next = jnp.where(last_tile, 0.0, dyn_ref[:, cols].astype(F32) * next_ref[:, cb].astype(F32))

            def slab(j, dcv_nxt):
                r = _rows(nslab - 1 - j)
                gc = t_ref[r, cc].astype(F32)
                xs = t_ref[r, cx].astype(F32)
                dyv = dy_ref[r, cols].astype(F32)
                p = gc * xs
                d0 = dyv * t_ref[r, cb].astype(F32)
                d1 = _shift_up(d0, dcv_nxt, 1)
                d2 = _shift_up(d0, dcv_nxt, 2)
                dp = wc_ref[2:3, cols] * d0 + wc_ref[1:2, cols] * d1 + wc_ref[0:1, cols] * d2
                for k, term in enumerate((d2 * p, d1 * p, d0 * p)):
                    acc_ref[k, :, cols] += term
                o_ref[r, cb] = (dyv * cv_ref[r, cols].astype(F32)).astype(BF16)
                o_ref[r, cc] = (dp * xs).astype(BF16)
                o_ref[r, cx] = (dp * gc).astype(BF16)
                return d0

            lax.fori_loop(0, nslab, slab, dcv_next)

        @pl.when(last_tile)
        def _():
            for k in range(3):
                dwc_ref[k:k + 1, :] = jnp.sum(acc_ref[k], axis=0, keepdims=True)

    _, next_spec = _halo_specs(tm, d3, n_tiles)
    _, next_dy = _halo_specs(tm, d, n_tiles)
    row = pl.BlockSpec((tm, d), lambda i: (i, 0))
    return pl.pallas_call(
        body, name=name, grid=(n_tiles,),
        in_specs=[pl.BlockSpec((tm, d3), lambda i: (i, 0)), next_spec, row, row, next_dy,
                  pl.BlockSpec((3, d), lambda i: (0, 0))],
        out_specs=[pl.BlockSpec((tm, d3), lambda i: (i, 0)), pl.BlockSpec((3, d), lambda i: (0, 0))],
        out_shape=[jax.ShapeDtypeStruct((s, d3), BF16), jax.ShapeDtypeStruct((3, d), F32)],
        scratch_shapes=[pltpu.VMEM((3, SLAB, d), F32)],
        compiler_params=_params(("arbitrary",)),
    )(bcx, bcx, cv, dy, dy, wc)


def _sigmoid(z):
    return 0.5 * jnp.tanh(0.5 * z) + 0.5


FUSE_STRIP = 256
FUSE_HALO = 8


def _ffn_up_fused(h, w_up, cw, cb, name):
    s, d = h.shape
    f = w_up.shape[1] // 2
    tm = _pick(s, 1024, LANES)
    tr = tm
    tn = _pick(f, 512, FUSE_STRIP)
    nj = f // tn

    def body(h_ref, wg_ref, wa_ref, cwg_ref, cwa_ref, cbg_ref, cba_ref,
             upg_ref, upa_ref, cvg_ref, cva_ref, act_ref, carry_g, carry_a):
        @pl.when(pl.program_id(1) == 0)
        def _():
            carry_g[...] = jnp.zeros_like(carry_g)
            carry_a[...] = jnp.zeros_like(carry_a)

        units = [(slice(st * FUSE_STRIP, (st + 1) * FUSE_STRIP), slice(rp * tr, (rp + 1) * tr))
                 for st in range(tn // FUSE_STRIP) for rp in range(tm // tr)]

        def matmul(unit):
            cols, rows = unit
            return tuple(jnp.dot(h_ref[rows, :], w_ref[:, cols], preferred_element_type=F32).astype(BF16)
                         for w_ref in (wg_ref, wa_ref))

        def conv(up, cw_ref, cb_ref, carry, up_ref, cv_ref, unit):
            cols, rows = unit
            up_ref[rows, cols] = up
            x = up.astype(F32)
            ext = jnp.concatenate([carry[:, cols], x], axis=0)
            s1 = pltpu.roll(ext, 1, 0)[FUSE_HALO:, :]
            s2 = pltpu.roll(ext, 2, 0)[FUSE_HALO:, :]
            carry[:, cols] = x[tr - FUSE_HALO:, :]
            cv = cw_ref[0:1, cols] * s2 + cw_ref[1:2, cols] * s1 + cw_ref[2:3, cols] * x + cb_ref[:, cols]
            cv_ref[rows, cols] = cv.astype(BF16)
            return cv

        ups = matmul(units[0])
        for n, unit in enumerate(units):
            ups_next = matmul(units[n + 1]) if n + 1 < len(units) else None
            gcv = conv(ups[0], cwg_ref, cbg_ref, carry_g, upg_ref, cvg_ref, unit)
            acv = conv(ups[1], cwa_ref, cba_ref, carry_a, upa_ref, cva_ref, unit)
            act_ref[unit[1], unit[0]] = (gcv * _sigmoid(gcv) * acv).astype(BF16)
            ups = ups_next

    def cols_of(rows, offset):
        return pl.BlockSpec((rows, tn), lambda j, i: (0, j + offset))

    tile = pl.BlockSpec((tm, tn), lambda j, i: (i, j))
    out = jax.ShapeDtypeStruct((s, f), BF16)
    return pl.pallas_call(
        body, name=name, grid=(nj, s // tm),
        in_specs=[pl.BlockSpec((tm, d), lambda j, i: (i, 0)), cols_of(d, 0), cols_of(d, nj),
                  cols_of(3, 0), cols_of(3, nj), cols_of(1, 0), cols_of(1, nj)],
        out_specs=[tile] * 5, out_shape=[out] * 5,
        scratch_shapes=[pltpu.VMEM((FUSE_HALO, tn), F32)] * 2,
        compiler_params=_params(("parallel", "arbitrary")),
    )(h, w_up, w_up, cw, cw, cb, cb)


def _ffn_bwd(up, cv, dact, cw, name):
    s, f = dact.shape
    f2 = 2 * f
    tm = _pick(s, 128, SLAB)
    w = _pick(f, 256, LANES)
    nslab = tm // SLAB
    n_tiles = s // tm

    def body(upg_ref, upa_ref, cvg_ref, cva_ref, cvgn_ref, cvan_ref, da_ref, dan_ref, cw_ref,
             o_ref, dcw_ref, dcb_ref, acc_ref):
        i = pl.program_id(0)
        last_tile = i == n_tiles - 1

        @pl.when(i == 0)
        def _():
            acc_ref[...] = jnp.zeros_like(acc_ref)

        for c in range(f // w):
            cols = slice(c * w, (c + 1) * w)

            def dconv(gcv, acv, dav):
                gcv, acv, dav = gcv.astype(F32), acv.astype(F32), dav.astype(F32)
                sg = _sigmoid(gcv)
                return dav * acv * (sg * (1.0 + gcv * (1.0 - sg))), dav * (gcv * sg)

            nxt = dconv(cvgn_ref[:, cols], cvan_ref[:, cols], dan_ref[:, cols])
            nxt = tuple(jnp.where(last_tile, 0.0, v) for v in nxt)

            def slab(j, carry):
                r = _rows(nslab - 1 - j)
                d = dconv(cvg_ref[r, cols], cva_ref[r, cols], da_ref[r, cols])
                for half, up_ref in enumerate((upg_ref, upa_ref)):
                    cs = slice(half * f + c * w, half * f + (c + 1) * w)
                    x = up_ref[r, cols].astype(F32)
                    d0 = d[half]
                    d1 = _shift_up(d0, carry[half], 1)
                    d2 = _shift_up(d0, carry[half], 2)
                    o_ref[r, cs] = (cw_ref[2:3, cs] * d0 + cw_ref[1:2, cs] * d1 + cw_ref[0:1, cs] * d2).astype(BF16)
                    for k, term in enumerate((d2 * x, d1 * x, d0 * x, d0)):
                        acc_ref[k, :, cs] += term
                return d

            lax.fori_loop(0, nslab, slab, nxt)

        @pl.when(last_tile)
        def _():
            for k in range(3):
                dcw_ref[k:k + 1, :] = jnp.sum(acc_ref[k], axis=0, keepdims=True)
            dcb_ref[...] = jnp.sum(acc_ref[3], axis=0, keepdims=True)

    _, nxt_spec = _halo_specs(tm, f, n_tiles)
    row = pl.BlockSpec((tm, f), lambda i: (i, 0))
    return pl.pallas_call(
        body, name=name, grid=(n_tiles,),
        in_specs=[row, row, row, row, nxt_spec, nxt_spec, row, nxt_spec, pl.BlockSpec((3, f2), lambda i: (0, 0))],
        out_specs=[pl.BlockSpec((tm, f2), lambda i: (i, 0)), pl.BlockSpec((3, f2), lambda i: (0, 0)),
                   pl.BlockSpec((1, f2), lambda i: (0, 0))],
        out_shape=[jax.ShapeDtypeStruct((s, f2), BF16), jax.ShapeDtypeStruct((3, f2), F32),
                   jax.ShapeDtypeStruct((1, f2), F32)],
        scratch_shapes=[pltpu.VMEM((4, SLAB, f2), F32)],
        compiler_params=_params(("arbitrary",)),
    )(up[0], up[1], cv[0], cv[1], cv[0], cv[1], dact, dact, cw)


_GELU_C = math.sqrt(2.0 / math.pi)


def _gelu(x):
    th = jnp.tanh(_GELU_C * (x + 0.044715 * (x * x * x)))
    return x * (0.5 * (1.0 + th)), th


def _gelu_grad(x, th):
    return 0.5 * (1.0 + th) + 0.5 * x * (1.0 - th * th) * (_GELU_C * (1.0 + 3.0 * 0.044715 * (x * x)))


def _masked_ws(ws_ref, h):
    t = lax.broadcasted_iota(jnp.int32, (CHUNK, CHUNK), 0)
    sx = lax.broadcasted_iota(jnp.int32, (CHUNK, CHUNK), 1)
    return jnp.where(sx <= t, ws_ref[h], 0.0)


def _mixb_fwd(pre, gv, ws, bs_wide, name):
    s, w2 = pre.shape
    w = w2 // 2
    gw = w // SG_GROUPS

    def body(pre_ref, gv_ref, ws_ref, bs_ref, o_ref):
        zu, _ = _gelu(pre_ref[:, :w].astype(F32))
        zv, _ = _gelu(pre_ref[:, w:].astype(F32))
        _, vhat = _rms_stats(zv)
        vn = (vhat * gv_ref[...]).astype(BF16)
        for h in range(SG_GROUPS):
            cols = slice(h * gw, (h + 1) * gw)
            wsm = _masked_ws(ws_ref, h).astype(BF16)
            gate = jnp.dot(wsm, vn[:, cols], preferred_element_type=F32)
            gate = gate + jnp.tile(bs_ref[h], (1, gw // LANES))
            o_ref[:, cols] = (zu[:, cols] * gate).astype(BF16)

    return pl.pallas_call(
        body, name=name, grid=(s // CHUNK,),
        in_specs=[pl.BlockSpec((CHUNK, w2), lambda i: (i, 0)), pl.BlockSpec((1, w), lambda i: (0, 0)),
                  pl.BlockSpec((SG_GROUPS, CHUNK, CHUNK), lambda i: (0, 0, 0)),
                  pl.BlockSpec((SG_GROUPS, CHUNK, LANES), lambda i: (0, 0, 0))],
        out_specs=pl.BlockSpec((CHUNK, w), lambda i: (i, 0)),
        out_shape=jax.ShapeDtypeStruct((s, w), BF16),
        compiler_params=_params(("parallel",)),
    )(pre, gv, ws, bs_wide)


def _mixb_bwd(pre, dug, gv, ws, bs_wide, name):
    s, w2 = pre.shape
    w = w2 // 2
    gw = w // SG_GROUPS

    def body(pre_ref, dug_ref, gv_ref, ws_ref, bs_ref, o_ref, dws_ref, dbs_ref, dgv_ref, dvn_ref):
        first = pl.program_id(0) == 0

        @pl.when(first)
        def _():
            dws_ref[...] = jnp.zeros_like(dws_ref)
            dbs_ref[...] = jnp.zeros_like(dbs_ref)

        pu = pre_ref[:, :w].astype(F32)
        pv = pre_ref[:, w:].astype(F32)
        zu, thu = _gelu(pu)
        zv, thv = _gelu(pv)
        inv, vhat = _rms_stats(zv)
        gvv = gv_ref[...]
        vn = (vhat * gvv).astype(BF16)
        for h in range(SG_GROUPS):
            cols = slice(h * gw, (h + 1) * gw)
            wsm = _masked_ws(ws_ref, h).astype(BF16)
            gate = jnp.dot(wsm, vn[:, cols], preferred_element_type=F32)
            gate = gate + jnp.tile(bs_ref[h], (1, gw // LANES))
            dug_h = dug_ref[:, cols].astype(F32)
            dgate = dug_h * zu[:, cols]
            dgate_b = dgate.astype(BF16)
            o_ref[:, cols] = (dug_h * gate * _gelu_grad(pu[:, cols], thu[:, cols])).astype(BF16)
            dbs_ref[h] += jnp.broadcast_to(jnp.sum(dgate, axis=-1, keepdims=True), (CHUNK, LANES))
            dws = lax.dot_general(dgate_b, vn[:, cols], _DIMS["nt"], preferred_element_type=F32)
            t = lax.broadcasted_iota(jnp.int32, (CHUNK, CHUNK), 0)
            sx = lax.broadcasted_iota(jnp.int32, (CHUNK, CHUNK), 1)
            dws_ref[h] += jnp.where(sx <= t, dws, 0.0)
            dvn_ref[:, cols] = lax.dot_general(wsm, dgate_b, _DIMS["tn"], preferred_element_type=F32)
        dvn = dvn_ref[...]
        part = jnp.sum(dvn * vhat, axis=0, keepdims=True)

        @pl.when(first)
        def _():
            dgv_ref[...] = part

        @pl.when(jnp.logical_not(first))
        def _():
            dgv_ref[...] += part

        dvhat = dvn * gvv
        dzv = inv * (dvhat - vhat * jnp.mean(dvhat * vhat, axis=-1, keepdims=True))
        o_ref[:, w:] = (dzv * _gelu_grad(pv, thv)).astype(BF16)

    return pl.pallas_call(
        body, name=name, grid=(s // CHUNK,),
        in_specs=[pl.BlockSpec((CHUNK, w2), lambda i: (i, 0)), pl.BlockSpec((CHUNK, w), lambda i: (i, 0)),
                  pl.BlockSpec((1, w), lambda i: (0, 0)),
                  pl.BlockSpec((SG_GROUPS, CHUNK, CHUNK), lambda i: (0, 0, 0)),
                  pl.BlockSpec((SG_GROUPS, CHUNK, LANES), lambda i: (0, 0, 0))],
        out_specs=[pl.BlockSpec((CHUNK, w2), lambda i: (i, 0)),
                   pl.BlockSpec((SG_GROUPS, CHUNK, CHUNK), lambda i: (0, 0, 0)),
                   pl.BlockSpec((SG_GROUPS, CHUNK, LANES), lambda i: (0, 0, 0)),
                   pl.BlockSpec((1, w), lambda i: (0, 0))],
        out_shape=[jax.ShapeDtypeStruct((s, w2), BF16), jax.ShapeDtypeStruct((SG_GROUPS, CHUNK, CHUNK), F32),
                   jax.ShapeDtypeStruct((SG_GROUPS, CHUNK, LANES), F32), jax.ShapeDtypeStruct((1, w), F32)],
        scratch_shapes=[pltpu.VMEM((CHUNK, w), F32)],
        compiler_params=_params(("arbitrary",)),
    )(pre, dug, gv, ws, bs_wide)


def _cast_layer(w3, layer, name):
    _, r, c = w3.shape
    tr = _pick(r, 256, SLAB)

    def body(w_ref, o_ref):
        o_ref[...] = w_ref[...].astype(BF16)

    return pl.pallas_call(
        body, name=name, grid=(r // tr,),
        in_specs=[pl.BlockSpec((None, tr, c), lambda i: (layer, i, 0))],
        out_specs=pl.BlockSpec((tr, c), lambda i: (i, 0)),
        out_shape=jax.ShapeDtypeStruct((r, c), BF16),
        compiler_params=_params(("parallel",)),
    )(w3)


def _adamw_math(w, g, m, v):
    m = ADAM_B1 * m + (1.0 - ADAM_B1) * g
    v = ADAM_B2 * v + (1.0 - ADAM_B2) * (g * g)
    m_hat = m / (1.0 - ADAM_B1 ** ADAM_STEP)
    v_hat = v / (1.0 - ADAM_B2 ** ADAM_STEP)
    delta = -ADAM_LR * (m_hat / (jnp.sqrt(v_hat) + ADAM_EPS) + ADAM_WD * w)
    return delta, m, v


def _adamw_sharded(recvs, w, m, v, name):
    nl, r, c = w.shape
    tc = _pick(c, 1536, LANES)
    tr = _pick(r, 64, SLAB)

    def body(*refs):
        recv_refs = refs[:nl]
        w_ref, m_ref, v_ref, g_ref, d_ref, nm_ref, nv_ref = refs[nl:]
        for layer, recv_ref in enumerate(recv_refs):
            @pl.when(pl.program_id(0) == layer)
            def _():
                g = recv_ref[0].astype(F32)
                for q in range(1, N_DEV):
                    g = g + recv_ref[q].astype(F32)
                delta, nm, nv = _adamw_math(w_ref[...], g, m_ref[...], v_ref[...])
                g_ref[...] = g
                d_ref[...] = delta
                nm_ref[...] = nm
                nv_ref[...] = nv

    def recv_spec(layer):
        return pl.BlockSpec((N_DEV, tr, tc),
                            lambda l, i, j: (0, jnp.where(l == layer, i, 0), jnp.where(l == layer, j, 0)))

    blk = pl.BlockSpec((None, tr, tc), lambda l, i, j: (l, i, j))
    out = jax.ShapeDtypeStruct((nl, r, c), F32)
    return pl.pallas_call(
        body, name=name, grid=(nl, r // tr, c // tc),
        in_specs=[recv_spec(layer) for layer in range(nl)] + [blk, blk, blk],
        out_specs=[blk] * 4, out_shape=[out] * 4,
        compiler_params=_params(("parallel",) * 3),
    )(*recvs, w, m, v)


def _adamw_packed(w, g, m, v, name):
    r, c = w.shape
    tr = _pick(r, 256, 8)

    def body(w_ref, g_ref, m_ref, v_ref, d_ref, nm_ref, nv_ref):
        delta, nm, nv = _adamw_math(w_ref[...], g_ref[...], m_ref[...], v_ref[...])
        d_ref[...] = delta
        nm_ref[...] = nm
        nv_ref[...] = nv

    blk = pl.BlockSpec((tr, c), lambda i: (i, 0))
    out = jax.ShapeDtypeStruct((r, c), F32)
    return pl.pallas_call(
        body, name=name, grid=(r // tr,), in_specs=[blk] * 4, out_specs=[blk] * 3, out_shape=[out] * 3,
        compiler_params=_params(("parallel",)),
    )(w, g, m, v)


def _pack(arrays):
    parts = []
    for a in arrays:
        flat = a.reshape(-1).astype(F32)
        pad = (-flat.shape[0]) % PACK_GRANULE
        parts.append(jnp.pad(flat, (0, pad)) if pad else flat)
    return jnp.concatenate(parts).reshape(-1, LANES)


def _unpack(buf, shapes):
    flat = buf.reshape(-1)
    out, off = [], 0
    for shp in shapes:
        n = math.prod(shp)
        out.append(flat[off:off + n].reshape(shp))
        off += n + (-n) % PACK_GRANULE
    return out


def _mesh_pos():
    return lax.axis_index("x"), lax.axis_index("y"), lax.axis_index("c")


def _coords(q):
    return q // 4, (q // 2) % 2, q % 2


def _shard_of(ref, q, shard_shape, axis):
    r, c = shard_shape
    if axis == 0:
        return ref.at[pl.ds(pl.multiple_of(q * r, SLAB), r), :]
    return ref.at[:, pl.ds(pl.multiple_of(q * c, LANES), c)]


_HBM = pl.BlockSpec(memory_space=pltpu.HBM)
_SEM = pl.BlockSpec(memory_space=pltpu.SEMAPHORE)
_EFFECT = pltpu.SideEffectType.DATAFLOW_SIDE_EFFECTING


def _exchange_shapes(gather, src_shape, axis):
    r, c = src_shape
    if gather:
        return (r, c), ((r * N_DEV, c) if axis == 0 else (r, c * N_DEV))
    shard = (r // N_DEV, c) if axis == 0 else (r, c // N_DEV)
    return shard, (N_DEV,) + shard


def _exchange_copies(gather, src, land, sems, axis):
    send_sems, recv_sems, own_sem = sems
    x, y, c_ = _mesh_pos()
    me = 4 * x + 2 * y + c_
    shard, _ = _exchange_shapes(gather, src.shape, axis)

    def piece(q):
        return src if gather else _shard_of(src, q, shard, axis)

    def place(q):
        return _shard_of(land, q, shard, axis) if gather else land.at[q]

    own = pltpu.make_async_copy(piece(me), place(me), own_sem.at[0])
    sends, arrivals = [], []
    for step in range(1, N_DEV):
        to = (me + step) % N_DEV
        frm = (me + N_DEV - step) % N_DEV
        sends.append(pltpu.make_async_remote_copy(
            src_ref=piece(to), dst_ref=place(me), send_sem=send_sems.at[step - 1], recv_sem=recv_sems.at[step - 1],
            device_id=_coords(to), device_id_type=MESH))
        arrivals.append(pltpu.make_async_remote_copy(
            src_ref=piece(me), dst_ref=place(frm), send_sem=send_sems.at[step - 1], recv_sem=recv_sems.at[step - 1],
            device_id=_coords(frm), device_id_type=MESH))
    return own, sends, arrivals


def _exchange_start(gather, src, axis, name, after=None):
    _, land_shape = _exchange_shapes(gather, src.shape, axis)
    extra = () if after is None else (after,)

    def body(*refs):
        src_ref, land = refs[:2]
        send_sems, recv_sems, own_sem = refs[2 + len(extra):5 + len(extra)]
        own, sends, _ = _exchange_copies(gather, src_ref, land, (send_sems, recv_sems, own_sem), axis)
        own.start()
        for cp in sends:
            cp.start()
        refs[-1][...] = jnp.zeros_like(refs[-1])

    out = pl.pallas_call(
        body, name=name,
        out_shape=(pltpu.SemaphoreType.DMA((N_DEV - 1,)), pltpu.SemaphoreType.DMA((N_DEV - 1,)),
                   pltpu.SemaphoreType.DMA((1,)), pltpu.HBM(src.shape, src.dtype),
                   pltpu.HBM(land_shape, src.dtype), jax.ShapeDtypeStruct((8, LANES), F32)),
        in_specs=[_HBM, _HBM] + [pl.BlockSpec(memory_space=pl.ANY)] * len(extra),
        out_specs=(_SEM, _SEM, _SEM, _HBM, _HBM, pl.BlockSpec(memory_space=pltpu.VMEM)),
        input_output_aliases={0: 3, 1: 4},
        compiler_params=pltpu.CompilerParams(has_side_effects=_EFFECT),
    )(pltpu.with_memory_space_constraint(src, pltpu.HBM),
      pltpu.with_memory_space_constraint(lax.empty(land_shape, src.dtype), pltpu.HBM), *extra)
    return out[:5], out[5]


def _exchange_wait(gather, state, axis, after, name):
    send_sems, recv_sems, own_sem, src_thru, land_thru = state
    after = tuple(after) if isinstance(after, (tuple, list)) else (after,)

    def body(src, land, send_sems, recv_sems, own_sem, *rest):
        own, sends, arrivals = _exchange_copies(gather, src, land, (send_sems, recv_sems, own_sem), axis)
        for cp in sends:
            cp.wait_send()
        for cp in arrivals:
            cp.wait_recv()
        own.wait()

    return pl.pallas_call(
        body, name=name,
        out_shape=(pltpu.HBM(src_thru.shape, src_thru.dtype), pltpu.HBM(land_thru.shape, land_thru.dtype)),
        in_specs=[_HBM, _HBM, _SEM, _SEM, _SEM] + [pl.BlockSpec(memory_space=pl.ANY)] * len(after),
        out_specs=(_HBM, _HBM),
        input_output_aliases={0: 0, 1: 1},
        compiler_params=pltpu.CompilerParams(has_side_effects=_EFFECT),
    )(src_thru, land_thru, send_sems, recv_sems, own_sem, *after)[1]


def _gather2_copies(shard_ref, land, sems, axis, shard_shape):
    send1, recv1, own_sem, send2, recv2 = sems
    x, y, c = _mesh_pos()
    me, sibling = (x, y, c), (x, y, 1 - c)
    chips = [(1 - x, y), (x, 1 - y), (1 - x, 1 - y)]

    def region(dev):
        px, py, pc = dev
        return _shard_of(land, 4 * px + 2 * py + pc, shard_shape, axis)

    def copy(src, block, to, send, recv):
        return pltpu.make_async_remote_copy(src_ref=src, dst_ref=region(block), send_sem=send, recv_sem=recv,
                                            device_id=to, device_id_type=MESH)

    own = pltpu.make_async_copy(shard_ref, region(me), own_sem.at[0])
    peers = [sibling] + [(*chip, c) for chip in chips]
    sends1 = [copy(shard_ref, me, to, send1.at[k], recv1.at[k]) for k, to in enumerate(peers)]
    arrivals1 = [copy(shard_ref, frm, frm, send1.at[k], recv1.at[k]) for k, frm in enumerate(peers)]
    sends2, arrivals2 = [], []
    if send2 is not None:
        for j, chip in enumerate(chips):
            sends2.append(copy(region((*chip, c)), (*chip, c), sibling, send2.at[j], recv2.at[j]))
            arrivals2.append(copy(region((*chip, 1 - c)), (*chip, 1 - c), sibling, send2.at[j], recv2.at[j]))
    return own, sends1, arrivals1, sends2, arrivals2


def _gather2_start(shard, axis, name, after=None):
    _, land_shape = _exchange_shapes(True, shard.shape, axis)
    extra = () if after is None else (after,)

    def body(*refs):
        src_ref, land = refs[:2]
        send1, recv1, own_sem = refs[2 + len(extra):5 + len(extra)]
        own, sends1, _, _, _ = _gather2_copies(src_ref, land, (send1, recv1, own_sem, None, None), axis, shard.shape)
        own.start()
        for cp in sends1[1:] + sends1[:1]:
            cp.start()
        refs[-1][...] = jnp.zeros_like(refs[-1])

    out = pl.pallas_call(
        body, name=name,
        out_shape=(pltpu.SemaphoreType.DMA((4,)), pltpu.SemaphoreType.DMA((4,)), pltpu.SemaphoreType.DMA((1,)),
                   pltpu.HBM(shard.shape, shard.dtype), pltpu.HBM(land_shape, shard.dtype),
                   jax.ShapeDtypeStruct((8, LANES), F32)),
        in_specs=[_HBM, _HBM] + [pl.BlockSpec(memory_space=pl.ANY)] * len(extra),
        out_specs=(_SEM, _SEM, _SEM, _HBM, _HBM, pl.BlockSpec(memory_space=pltpu.VMEM)),
        input_output_aliases={0: 3, 1: 4},
        compiler_params=pltpu.CompilerParams(has_side_effects=_EFFECT),
    )(pltpu.with_memory_space_constraint(shard, pltpu.HBM),
      pltpu.with_memory_space_constraint(lax.empty(land_shape, shard.dtype), pltpu.HBM), *extra)
    return out[:5], out[5]


def _gather2_pass(state, axis, after, name):
    send1, recv1, own_sem, shard_thru, land_thru = state

    def body(src_ref, land, send1, recv1, own_sem, after_ref, send2, recv2, src_out, land_out, token):
        _, _, arrivals1, sends2, _ = _gather2_copies(src_ref, land, (send1, recv1, own_sem, send2, recv2), axis,
                                                     shard_thru.shape)
        for arrival, fwd in zip(arrivals1[1:], sends2):
            arrival.wait_recv()
            fwd.start()
        token[...] = jnp.zeros_like(token)

    out = pl.pallas_call(
        body, name=name,
        out_shape=(pltpu.SemaphoreType.DMA((3,)), pltpu.SemaphoreType.DMA((3,)),
                   pltpu.HBM(shard_thru.shape, shard_thru.dtype), pltpu.HBM(land_thru.shape, land_thru.dtype),
                   jax.ShapeDtypeStruct((8, LANES), F32)),
        in_specs=[_HBM, _HBM, _SEM, _SEM, _SEM, pl.BlockSpec(memory_space=pl.ANY)],
        out_specs=(_SEM, _SEM, _HBM, _HBM, pl.BlockSpec(memory_space=pltpu.VMEM)),
        input_output_aliases={0: 2, 1: 3},
        compiler_params=pltpu.CompilerParams(has_side_effects=_EFFECT),
    )(shard_thru, land_thru, send1, recv1, own_sem, after)
    return (send1, recv1, own_sem, out[0], out[1], out[2], out[3]), out[4]


def _gather2_wait(state, axis, after, name):
    send1, recv1, own_sem, send2, recv2, shard_thru, land_thru = state

    def body(src_ref, land, send1, recv1, own_sem, send2, recv2, after_ref, src_dead, got):
        own, sends1, arrivals1, sends2, arrivals2 = _gather2_copies(
            src_ref, land, (send1, recv1, own_sem, send2, recv2), axis, shard_thru.shape)
        for cp in sends1 + sends2:
            cp.wait_send()
        for cp in arrivals1[:1] + arrivals2:
            cp.wait_recv()
        own.wait()

    return pl.pallas_call(
        body, name=name,
        out_shape=(pltpu.HBM(shard_thru.shape, shard_thru.dtype), pltpu.HBM(land_thru.shape, land_thru.dtype)),
        in_specs=[_HBM, _HBM] + [_SEM] * 5 + [pl.BlockSpec(memory_space=pl.ANY)],
        out_specs=(_HBM, _HBM),
        input_output_aliases={0: 0, 1: 1},
        compiler_params=pltpu.CompilerParams(has_side_effects=_EFFECT),
    )(shard_thru, land_thru, send1, recv1, own_sem, send2, recv2, after)[1]


def _sum_slots(slots, name):
    _, r, c = slots.shape
    tr = _pick(r, 512, 8)

    def body(s_ref, o_ref):
        total = s_ref[0]
        for q in range(1, N_DEV):
            total = total + s_ref[q]
        o_ref[...] = total

    return pl.pallas_call(
        body, name=name, grid=(r // tr,),
        in_specs=[pl.BlockSpec((N_DEV, tr, c), lambda i: (0, i, 0))],
        out_specs=pl.BlockSpec((tr, c), lambda i: (i, 0)),
        out_shape=jax.ShapeDtypeStruct((r, c), F32),
        compiler_params=_params(("parallel",)),
    )(slots)


def kernel(x, a_norm, a_in, a_conv, a_out, b_norm, b_in, b_vnorm, b_ws, b_bs, b_out, f_norm, f_up, f_conv_w, f_conv_b, f_down, final_norm, loss_target, m_a_norm, m_a_in, m_a_conv, m_a_out, m_b_norm, m_b_in, m_b_vnorm, m_b_ws, m_b_bs, m_b_out, m_f_norm, m_f_up, m_f_conv_w, m_f_conv_b, m_f_down, m_final_norm, v_a_norm, v_a_in, v_a_conv, v_a_out, v_b_norm, v_b_in, v_b_vnorm, v_b_ws, v_b_bs, v_b_out, v_f_norm, v_f_up, v_f_conv_w, v_f_conv_b, v_f_down, v_final_norm):
    s, d = x.shape[1], x.shape[2]
    n_ffn = f_up.shape[0]
    f2 = f_up.shape[2] * N_DEV
    me = 4 * lax.axis_index("x") + 2 * lax.axis_index("y") + lax.axis_index("c")
    x0 = x.reshape(s, d)
    target = loss_target.reshape(s, d)

    wanted = [("a_in", _cast_layer(a_in, 0, "cast_a_in"), 1),
              ("small", _pack([a_conv, b_norm, b_vnorm, f_conv_w]), 0),
              ("a_out", _cast_layer(a_out, 0, "cast_a_out"), 0),
              ("f_up0", _cast_layer(f_up, 0, "cast_f_up0"), 1), ("f_down0", _cast_layer(f_down, 0, "cast_f_down0"), 0),
              ("b_in", _cast_layer(b_in, 0, "cast_b_in"), 1), ("b_out", _cast_layer(b_out, 0, "cast_b_out"), 0),
              ("f_up1", _cast_layer(f_up, 1, "cast_f_up1"), 1), ("f_down1", _cast_layer(f_down, 1, "cast_f_down1"), 0)]
    coming, tok, h0 = {}, None, None
    for n_started, (key, shard, axis) in enumerate(wanted):
        if n_started == 2:
            tok = h0 = _rmsnorm_fwd(x0, a_norm, "mixa_norm", after=tok)
        state, tok = _gather2_start(shard, axis, f"ag_start_{key}", after=tok)
        coming[key] = (state, axis)

    def pass_on(keys, after):
        for key in keys:
            state, axis = coming[key]
            state, after = _gather2_pass(state, axis, after, f"ag_pass_{key}")
            coming[key] = (state, axis)
        return after

    def arrived(key, after):
        state, axis = coming[key]
        return _gather2_wait(state, axis, after, f"ag_wait_{key}")

    cshard = a_conv.shape[2]
    fshard = f_conv_w.shape[2]
    w_a_in = arrived("a_in", pass_on(["a_in", "small"], tok))
    small_full = arrived("small", w_a_in)
    small_rows = small_full.reshape(N_DEV, -1)
    per_dev = _unpack_rows(small_rows, [(3, cshard), (cshard,), (cshard,), (n_ffn, 3, fshard)])
    a_conv_full = per_dev[0].transpose(1, 0, 2).reshape(3, d)
    b_norm_full = per_dev[1].reshape(1, d)
    b_vnorm_full = per_dev[2].reshape(1, d)
    f_conv_w_full = per_dev[3].transpose(1, 2, 0, 3).reshape(n_ffn, 3, f2)
    bs_wide = jnp.broadcast_to(b_bs[0][:, :, None], (SG_GROUPS, CHUNK, LANES))
    ws = b_ws[0]

    w_f_up, w_f_down = {}, {}

    def ffn_forward(xin, l, pass_early, pass_late):
        h = _rmsnorm_fwd(xin, f_norm[l:l + 1], f"ffn{l}_norm")
        w_f_up[l] = arrived(f"f_up{l}", pass_on(pass_early, h))
        up_g, up_a, cv_g, cv_a, act = _ffn_up_fused(h, w_f_up[l], f_conv_w_full[l], f_conv_b[l:l + 1], f"ffn{l}_up")
        up, cv = (up_g, up_a), (cv_g, cv_a)
        w_f_down[l] = arrived(f"f_down{l}", pass_on(pass_late, act))
        xout = _matmul(act, w_f_down[l], "nn", F32, f"ffn{l}_down", resid=xin, tk_cap=1408)
        return xout, (h, up, act, cv)

    bcx = _matmul(h0, w_a_in, "nn", BF16, "mixa_in")
    ya, cva = _mixa_fwd(bcx, a_conv_full, "mixa_mid")
    w_a_out = arrived("a_out", pass_on(["a_out", "f_up0"], ya))
    x1 = _matmul(ya, w_a_out, "nn", F32, "mixa_out", resid=x0)
    x2, saved0 = ffn_forward(x1, 0, ["f_down0"], ["b_in", "b_out", "f_up1", "f_down1"])
    h2 = _rmsnorm_fwd(x2, b_norm_full, "mixb_norm")
    w_b_in = arrived("b_in", h2)
    pre = _matmul(h2, w_b_in, "nn", BF16, "mixb_in")
    ug = _mixb_fwd(pre, b_vnorm_full, ws, bs_wide, "mixb_mid")
    w_b_out = arrived("b_out", ug)
    x3 = _matmul(ug, w_b_out, "nn", F32, "mixb_out", resid=x2)
    x4, saved1 = ffn_forward(x3, 1, [], [])
    dx4, dx4b, loss_part, g_final = _final_loss(x4, final_norm.reshape(1, d), target, "loss_head")

    def _rs_start(grad, axis, name):
        return _exchange_start(False, grad, axis, name)

    def ffn_backward(xin, l, saved, dx, dxb):
        h, up, act, cv = saved
        g_down = _matmul(act, dxb, "tn", BF16, f"ffn{l}_down_dw", tm_cap=1408)
        rs_down, tok = _rs_start(g_down, 0, f"rs_start_f_down{l}")
        dact = _matmul(dxb, w_f_down[l], "nt", BF16, f"ffn{l}_down_dx", after=tok, tn_cap=1408)
        dup, g_cw, g_cb = _ffn_bwd(up, cv, dact, f_conv_w_full[l], f"ffn{l}_mid_bwd")
        g_up = _matmul(h, dup, "tn", BF16, f"ffn{l}_up_dw", tk_cap=4096)
        rs_up, tok = _rs_start(g_up, 1, f"rs_start_f_up{l}")
        dh = _matmul(dup, w_f_up[l], "nt", F32, f"ffn{l}_up_dx", after=tok)
        dxin, dxinb, g_norm = _rmsnorm_bwd(xin, f_norm[l:l + 1], dh, dx, f"ffn{l}_norm_bwd")
        return dxin, dxinb, (rs_up, rs_down, g_cw, g_cb, g_norm)

    dx3, dx3b, gf1 = ffn_backward(x3, 1, saved1, dx4, dx4b)
    g_b_out = _matmul(ug, dx3b, "tn", BF16, "mixb_out_dw", tk_cap=4096)
    rs_b_out, tok = _rs_start(g_b_out, 0, "rs_start_b_out")
    dug = _matmul(dx3b, w_b_out, "nt", BF16, "mixb_out_dx", after=tok)
    dpre, g_ws, g_bs_wide, g_bvnorm = _mixb_bwd(pre, dug, b_vnorm_full, ws, bs_wide, "mixb_mid_bwd")
    g_b_in = _matmul(h2, dpre, "tn", BF16, "mixb_in_dw", tk_cap=4096)
    rs_b_in, tok = _rs_start(g_b_in, 1, "rs_start_b_in")
    dh2 = _matmul(dpre, w_b_in, "nt", F32, "mixb_in_dx", after=tok)
    dx2, dx2b, g_bnorm = _rmsnorm_bwd(x2, b_norm_full, dh2, dx3, "mixb_norm_bwd")
    dx1, dx1b, gf0 = ffn_backward(x1, 0, saved0, dx2, dx2b)
    g_a_out = _matmul(ya, dx1b, "tn", BF16, "mixa_out_dw", tk_cap=4096)
    rs_a_out, tok = _rs_start(g_a_out, 0, "rs_start_a_out")
    dya = _matmul(dx1b, w_a_out, "nt", BF16, "mixa_out_dx", after=tok)
    dbcx, g_aconv = _mixa_bwd(bcx, cva, dya, a_conv_full, "mixa_mid_bwd")
    g_a_in = _matmul(h0, dbcx, "tn", BF16, "mixa_in_dw", tk_cap=4096)
    rs_a_in, tok = _rs_start(g_a_in, 1, "rs_start_a_in")
    dh0 = _matmul(dbcx, w_a_in, "nt", F32, "mixa_in_dx", after=tok)
    grad_x, _, g_anorm = _rmsnorm_bwd(x0, a_norm, dh0, dx1, "mixa_norm_bwd")

    full_shapes = [(1, LANES), (1, d), (3, d), (1, d), (1, d), (SG_GROUPS, CHUNK, CHUNK), (SG_GROUPS, CHUNK),
                   (n_ffn, d), (n_ffn, 3, f2), (n_ffn, f2), (1, d)]
    parts = [loss_part, g_anorm, g_aconv, g_bnorm, g_bvnorm, g_ws, g_bs_wide[:, :, 0],
             jnp.concatenate([gf0[4], gf1[4]], axis=0), jnp.stack([gf0[2], gf1[2]]),
             jnp.concatenate([gf0[3], gf1[3]], axis=0), g_final]
    small_part = _pack(parts)
    small_state, small_tok = _exchange_start(True, small_part, 0, "ar_start_small", after=grad_x)

    big = {}
    for name, states, axis, w, m, v in (
            ("f_down", (gf0[1], gf1[1]), 0, f_down, m_f_down, v_f_down),
            ("f_up", (gf0[0], gf1[0]), 1, f_up, m_f_up, v_f_up),
            ("b_out", (rs_b_out,), 0, b_out, m_b_out, v_b_out), ("b_in", (rs_b_in,), 1, b_in, m_b_in, v_b_in),
            ("a_out", (rs_a_out,), 0, a_out, m_a_out, v_a_out), ("a_in", (rs_a_in,), 1, a_in, m_a_in, v_a_in)):
        recvs = [_exchange_wait(False, st, axis, small_tok, f"rs_wait_{name}{l}") for l, st in enumerate(states)]
        big[name] = _adamw_sharded(recvs, w, m, v, f"adamw_{name}")

    slots = _exchange_wait(True, small_state, 0, [res[0] for res in big.values()], "ar_wait_small")
    total = _sum_slots(slots.reshape((N_DEV,) + small_part.shape), "ar_sum_small")
    (loss_v, r_anorm, r_aconv, r_bnorm, r_bvnorm, r_ws, r_bs, r_fnorm, r_fcw, r_fcb, r_final) = _unpack(total, full_shapes)
    small_grads = [
        r_anorm,
        lax.dynamic_slice_in_dim(r_aconv, me * cshard, cshard, axis=1).reshape(a_conv.shape),
        lax.dynamic_slice_in_dim(r_bnorm, me * cshard, cshard, axis=1),
        lax.dynamic_slice_in_dim(r_bvnorm, me * cshard, cshard, axis=1),
        r_ws.reshape(b_ws.shape), r_bs.reshape(b_bs.shape), r_fnorm,
        lax.dynamic_slice_in_dim(r_fcw, me * fshard, fshard, axis=2),
        r_fcb, r_final.reshape(final_norm.shape)]
    small_w = [a_norm, a_conv, b_norm, b_vnorm, b_ws, b_bs, f_norm, f_conv_w, f_conv_b, final_norm]
    small_m = [m_a_norm, m_a_conv, m_b_norm, m_b_vnorm, m_b_ws, m_b_bs, m_f_norm, m_f_conv_w, m_f_conv_b, m_final_norm]
    small_v = [v_a_norm, v_a_conv, v_b_norm, v_b_vnorm, v_b_ws, v_b_bs, v_f_norm, v_f_conv_w, v_f_conv_b, v_final_norm]
    shapes = [w.shape for w in small_w]
    packed = _adamw_packed(_pack(small_w), _pack(small_grads), _pack(small_m), _pack(small_v), "adamw_small")
    s_delta, s_m, s_v = (_unpack(p, shapes) for p in packed)
    small_names = ["a_norm", "a_conv", "b_norm", "b_vnorm", "b_ws", "b_bs", "f_norm", "f_conv_w", "f_conv_b", "final_norm"]
    small = {nm: (small_grads[i], s_delta[i], s_m[i], s_v[i]) for i, nm in enumerate(small_names)}

    order = ["a_norm", "a_in", "a_conv", "a_out", "b_norm", "b_in", "b_vnorm", "b_ws", "b_bs", "b_out",
             "f_norm", "f_up", "f_conv_w", "f_conv_b", "f_down", "final_norm"]
    res = {nm: (big[nm] if nm in big else small[nm]) for nm in order}
    outs = [loss_v[0, 0], grad_x.reshape(x.shape)]
    for k in range(4):
        outs += [res[nm][k] for nm in order]
    return tuple(outs)


def _unpack_rows(rows, shapes):
    out, off = [], 0
    for shp in shapes:
        n = math.prod(shp)
        out.append(rows[:, off:off + n].reshape((N_DEV,) + tuple(shp)))
        off += n + (-n) % PACK_GRANULE
    return out
```

```python
import functools
import math

import jax
import jax.numpy as jnp
from jax import lax
from jax.experimental import pallas as pl
from jax.experimental.pallas import tpu as pltpu

F32 = jnp.float32
BF16 = jnp.bfloat16
MESH = pl.DeviceIdType.MESH

N_DEV = 8
RMS_EPS = 1e-5
CHUNK = 128
SG_GROUPS = 8
ADAM_LR = 0.001
ADAM_B1 = 0.9
ADAM_B2 = 0.999
ADAM_EPS = 1e-08
ADAM_WD = 0.01
ADAM_STEP = 10

LANES = 128
SLAB = 16
VMEM_LIMIT = 56 * 1024 * 1024
PACK_GRANULE = 8 * LANES


def _pick(dim, cap, mult):
    best = None
    t = mult
    while t <= min(dim, cap):
        if dim % t == 0:
            best = t
        t += mult
    return dim if best is None else best


def _params(semantics=None):
    return pltpu.CompilerParams(dimension_semantics=semantics, vmem_limit_bytes=VMEM_LIMIT)


_DIMS = {
    "nn": (((1,), (0,)), ((), ())),
    "nt": (((1,), (1,)), ((), ())),
    "tn": (((0,), (0,)), ((), ())),
}


def _matmul(a, b, mode, out_dtype, name, resid=None, after=None, tm_cap=1024, tn_cap=1024, tk_cap=2816):
    if mode == "nn":
        (m, k), n = a.shape, b.shape[1]
    elif mode == "nt":
        (m, k), n = a.shape, b.shape[0]
    else:
        (k, m), n = a.shape, b.shape[1]
    tm, tn, tk = _pick(m, tm_cap, LANES), _pick(n, tn_cap, LANES), _pick(k, tk_cap, LANES)
    nk = k // tk
    n_in = 2 + (resid is not None) + (after is not None)

    def body(*refs):
        a_ref, b_ref = refs[:2]
        r_ref = refs[2] if resid is not None else None
        o_ref = refs[n_in]
        prod = lax.dot_general(a_ref[...], b_ref[...], _DIMS[mode], preferred_element_type=F32)

        def finish(r):
            if r_ref is not None:
                r = r + r_ref[...]
            o_ref[...] = r.astype(out_dtype)

        if nk == 1:
            finish(prod)
            return
        acc_ref = refs[n_in + 1]
        kk = pl.program_id(2)

        @pl.when(kk == 0)
        def _():
            acc_ref[...] = prod

        @pl.when(jnp.logical_and(kk > 0, kk < nk - 1))
        def _():
            acc_ref[...] += prod

        @pl.when(kk == nk - 1)
        def _():
            finish(acc_ref[...] + prod)

    a_spec = (pl.BlockSpec((tk, tm), lambda i, j, kk: (kk, i)) if mode == "tn"
              else pl.BlockSpec((tm, tk), lambda i, j, kk: (i, kk)))
    b_spec = (pl.BlockSpec((tn, tk), lambda i, j, kk: (j, kk)) if mode == "nt"
              else pl.BlockSpec((tk, tn), lambda i, j, kk: (kk, j)))
    o_spec = pl.BlockSpec((tm, tn), lambda i, j, kk: (i, j))
    in_specs = [a_spec, b_spec] + ([o_spec] if resid is not None else [])
    args = (a, b) + ((resid,) if resid is not None else ())
    if after is not None:
        in_specs.append(pl.BlockSpec(memory_space=pl.ANY))
        args += (after,)
    return pl.pallas_call(
        body, name=name, grid=(m // tm, n // tn, nk),
        in_specs=in_specs, out_specs=o_spec,
        out_shape=jax.ShapeDtypeStruct((m, n), out_dtype),
        scratch_shapes=[pltpu.VMEM((tm, tn), F32)] if nk > 1 else [],
        compiler_params=_params(("parallel", "parallel", "arbitrary")),
    )(*args)


def _rms_stats(xf):
    inv = lax.rsqrt(jnp.mean(xf * xf, axis=-1, keepdims=True) + RMS_EPS)
    return inv, xf * inv


def _rmsnorm_fwd(x, g, name, after=None):
    s, d = x.shape
    tm = _pick(s, 256, SLAB)
    extra = () if after is None else (after,)

    def body(x_ref, g_ref, *rest):
        _, xhat = _rms_stats(x_ref[...])
        rest[-1][...] = (xhat * g_ref[...]).astype(BF16)

    return pl.pallas_call(
        body, name=name, grid=(s // tm,),
        in_specs=[pl.BlockSpec((tm, d), lambda i: (i, 0)), pl.BlockSpec((1, d), lambda i: (0, 0))]
        + [pl.BlockSpec(memory_space=pl.ANY)] * len(extra),
        out_specs=pl.BlockSpec((tm, d), lambda i: (i, 0)),
        out_shape=jax.ShapeDtypeStruct((s, d), BF16),
        compiler_params=_params(("parallel",)),
    )(x, g, *extra)


def _rmsnorm_bwd(x, g, dh, dx_out, name):
    s, d = x.shape
    tm = _pick(s, 256, SLAB)

    def body(x_ref, g_ref, dh_ref, dxo_ref, dxi_ref, dxib_ref, dg_ref):
        inv, xhat = _rms_stats(x_ref[...])
        dhv = dh_ref[...]
        dxhat = dhv * g_ref[...]
        proj = jnp.mean(dxhat * xhat, axis=-1, keepdims=True)
        dx = dxo_ref[...] + inv * (dxhat - xhat * proj)
        dxi_ref[...] = dx
        dxib_ref[...] = dx.astype(BF16)
        part = jnp.sum(dhv * xhat, axis=0, keepdims=True)

        @pl.when(pl.program_id(0) == 0)
        def _():
            dg_ref[...] = part

        @pl.when(pl.program_id(0) > 0)
        def _():
            dg_ref[...] += part

    row = pl.BlockSpec((tm, d), lambda i: (i, 0))
    vec = pl.BlockSpec((1, d), lambda i: (0, 0))
    return pl.pallas_call(
        body, name=name, grid=(s // tm,),
        in_specs=[row, vec, row, row], out_specs=[row, row, vec],
        out_shape=[jax.ShapeDtypeStruct((s, d), F32), jax.ShapeDtypeStruct((s, d), BF16),
                   jax.ShapeDtypeStruct((1, d), F32)],
        compiler_params=_params(("arbitrary",)),
    )(x, g, dh, dx_out)


def _final_loss(x, g, target, name):
    s, d = x.shape
    tm = _pick(s, 256, SLAB)

    def body(x_ref, g_ref, t_ref, dx_ref, dxb_ref, loss_ref, dg_ref):
        inv, xhat = _rms_stats(x_ref[...])
        gv = g_ref[...]
        err = xhat * gv - t_ref[...]
        loss = 0.5 * jnp.sum(jnp.mean(err * err, axis=-1, keepdims=True), axis=0, keepdims=True)
        dy = err * (1.0 / d)
        dxhat = dy * gv
        proj = jnp.mean(dxhat * xhat, axis=-1, keepdims=True)
        dx = inv * (dxhat - xhat * proj)
        dx_ref[...] = dx
        dxb_ref[...] = dx.astype(BF16)
        part = jnp.sum(dy * xhat, axis=0, keepdims=True)
        loss_row = jnp.broadcast_to(loss, (1, LANES))

        @pl.when(pl.program_id(0) == 0)
        def _():
            dg_ref[...] = part
            loss_ref[...] = loss_row

        @pl.when(pl.program_id(0) > 0)
        def _():
            dg_ref[...] += part
            loss_ref[...] += loss_row

    row = pl.BlockSpec((tm, d), lambda i: (i, 0))
    vec = pl.BlockSpec((1, d), lambda i: (0, 0))
    return pl.pallas_call(
        body, name=name, grid=(s // tm,),
        in_specs=[row, vec, row],
        out_specs=[row, row, pl.BlockSpec((1, LANES), lambda i: (0, 0)), vec],
        out_shape=[jax.ShapeDtypeStruct((s, d), F32), jax.ShapeDtypeStruct((s, d), BF16),
                   jax.ShapeDtypeStruct((1, LANES), F32), jax.ShapeDtypeStruct((1, d), F32)],
        compiler_params=_params(("arbitrary",)),
    )(x, g, target)


def _shift_down(prev, cur, k):
    ext = jnp.concatenate([prev, cur], axis=0)
    return pltpu.roll(ext, k, 0)[SLAB:, :]


def _shift_up(cur, nxt, k):
    ext = jnp.concatenate([cur, nxt], axis=0)
    return pltpu.roll(ext, 2 * SLAB - k, 0)[:SLAB, :]


def _conv3(w_ref, cols, prev, cur):
    s1 = _shift_down(prev, cur, 1)
    s2 = _shift_down(prev, cur, 2)
    y = w_ref[0:1, cols] * s2 + w_ref[1:2, cols] * s1 + w_ref[2:3, cols] * cur
    return y, s1, s2


def _conv3_t(w_ref, cols, cur, nxt):
    return (w_ref[2:3, cols] * cur + w_ref[1:2, cols] * _shift_up(cur, nxt, 1)
            + w_ref[0:1, cols] * _shift_up(cur, nxt, 2))


def _rows(s):
    return pl.ds(pl.multiple_of(s * SLAB, SLAB), SLAB)


def _halo_specs(tm, width, n_tiles):
    per = tm // SLAB
    prev = pl.BlockSpec((SLAB, width), lambda i: (jnp.maximum(i * per - 1, 0), 0))
    nxt = pl.BlockSpec((SLAB, width), lambda i: (jnp.minimum((i + 1) * per, n_tiles * per - 1), 0))
    return prev, nxt


def _mixa_fwd(bcx, wc, name):
    s, d3 = bcx.shape
    d = d3 // 3
    tm = _pick(s, 256, SLAB)
    w = _pick(d, 512, LANES)
    nslab = tm // SLAB

    def body(t_ref, prev_ref, wc_ref, y_ref, cv_ref):
        first_tile = pl.program_id(0) == 0
        for c in range(d // w):
            cb, cc, cx = (slice(g * d + c * w, g * d + (c + 1) * w) for g in range(3))
            cols = slice(c * w, (c + 1) * w)
            p_halo = prev_ref[:, cc].astype(F32) * prev_ref[:, cx].astype(F32)
            p_halo = jnp.where(first_tile, 0.0, p_halo)

            def slab(si, p_prev):
                r = _rows(si)
                p = t_ref[r, cc].astype(F32) * t_ref[r, cx].astype(F32)
                cv, _, _ = _conv3(wc_ref, cols, p_prev, p)
                cv_ref[r, cols] = cv.astype(BF16)
                y_ref[r, cols] = (t_ref[r, cb].astype(F32) * cv).astype(BF16)
                return p

            lax.fori_loop(0, nslab, slab, p_halo)

    prev_spec, _ = _halo_specs(tm, d3, s // tm)
    row = pl.BlockSpec((tm, d), lambda i: (i, 0))
    return pl.pallas_call(
        body, name=name, grid=(s // tm,),
        in_specs=[pl.BlockSpec((tm, d3), lambda i: (i, 0)), prev_spec, pl.BlockSpec((3, d), lambda i: (0, 0))],
        out_specs=[row, row],
        out_shape=[jax.ShapeDtypeStruct((s, d), BF16)] * 2,
        compiler_params=_params(("parallel",)),
    )(bcx, bcx, wc)


def _mixa_bwd(bcx, cv, dy, wc, name):
    s, d3 = bcx.shape
    d = d3 // 3
    tm = _pick(s, 256, SLAB)
    w = _pick(d, 256, LANES)
    nslab = tm // SLAB
    n_tiles = s // tm

    def body(t_ref, next_ref, cv_ref, dy_ref, dyn_ref, wc_ref, o_ref, dwc_ref, acc_ref):
        i = pl.program_id(0)
        last_tile = i == n_tiles - 1

        @pl.when(i == 0)
        def _():
            acc_ref[...] = jnp.zeros_like(acc_ref)

        for c in range(d // w):
            cb, cc, cx = (slice(g * d + c * w, g * d + (c + 1) * w) for g in range(3))
            cols = slice(c * w, (c + 1) * w)
            dcv_next = jnp.where(last_tile, 0.0, dyn_ref[:, cols].astype(F32) * next_ref[:, cb].astype(F32))

            def slab(j, dcv_nxt):
                r = _rows(nslab - 1 - j)
                gc = t_ref[r, cc].astype(F32)
                xs = t_ref[r, cx].astype(F32)
                dyv = dy_ref[r, cols].astype(F32)
                p = gc * xs
                d0 = dyv * t_ref[r, cb].astype(F32)
                d1 = _shift_up(d0, dcv_nxt, 1)
                d2 = _shift_up(d0, dcv_nxt, 2)
                dp = wc_ref[2:3, cols] * d0 + wc_ref[1:2, cols] * d1 + wc_ref[0:1, cols] * d2
                for k, term in enumerate((d2 * p, d1 * p, d0 * p)):
                    acc_ref[k, :, cols] += term
                o_ref[r, cb] = (dyv * cv_ref[r, cols].astype(F32)).astype(BF16)
                o_ref[r, cc] = (dp * xs).astype(BF16)
                o_ref[r, cx] = (dp * gc).astype(BF16)
                return d0

            lax.fori_loop(0, nslab, slab, dcv_next)

        @pl.when(last_tile)
        def _():
            for k in range(3):
                dwc_ref[k:k + 1, :] = jnp.sum(acc_ref[k], axis=0, keepdims=True)

    _, next_spec = _halo_specs(tm, d3, n_tiles)
    _, next_dy = _halo_specs(tm, d, n_tiles)
    row = pl.BlockSpec((tm, d), lambda i: (i, 0))
    return pl.pallas_call(
        body, name=name, grid=(n_tiles,),
        in_specs=[pl.BlockSpec((tm, d3), lambda i: (i, 0)), next_spec, row, row, next_dy,
                  pl.BlockSpec((3, d), lambda i: (0, 0))],
        out_specs=[pl.BlockSpec((tm, d3), lambda i: (i, 0)), pl.BlockSpec((3, d), lambda i: (0, 0))],
        out_shape=[jax.ShapeDtypeStruct((s, d3), BF16), jax.ShapeDtypeStruct((3, d), F32)],
        scratch_shapes=[pltpu.VMEM((3, SLAB, d), F32)],
        compiler_params=_params(("arbitrary",)),
    )(bcx, bcx, cv, dy, dy, wc)


def _sigmoid(z):
    return 0.5 * jnp.tanh(0.5 * z) + 0.5


FUSE_STRIP = 256
FUSE_HALO = 8


def _ffn_up_fused(h, w_up, cw, cb, name):
    s, d = h.shape
    f = w_up.shape[1] // 2
    tm = _pick(s, 1024, LANES)
    tr = tm
    tn = _pick(f, 512, FUSE_STRIP)
    nj = f // tn

    def body(h_ref, wg_ref, wa_ref, cwg_ref, cwa_ref, cbg_ref, cba_ref,
             upg_ref, upa_ref, cvg_ref, cva_ref, act_ref, carry_g, carry_a):
        @pl.when(pl.program_id(1) == 0)
        def _():
            carry_g[...] = jnp.zeros_like(carry_g)
            carry_a[...] = jnp.zeros_like(carry_a)

        units = [(slice(st * FUSE_STRIP, (st + 1) * FUSE_STRIP), slice(rp * tr, (rp + 1) * tr))
                 for st in range(tn // FUSE_STRIP) for rp in range(tm // tr)]

        def matmul(unit):
            cols, rows = unit
            return tuple(jnp.dot(h_ref[rows, :], w_ref[:, cols], preferred_element_type=F32).astype(BF16)
                         for w_ref in (wg_ref, wa_ref))

        def conv(up, cw_ref, cb_ref, carry, up_ref, cv_ref, unit):
            cols, rows = unit
            up_ref[rows, cols] = up
            x = up.astype(F32)
            ext = jnp.concatenate([carry[:, cols], x], axis=0)
            s1 = pltpu.roll(ext, 1, 0)[FUSE_HALO:, :]
            s2 = pltpu.roll(ext, 2, 0)[FUSE_HALO:, :]
            carry[:, cols] = x[tr - FUSE_HALO:, :]
            cv = cw_ref[0:1, cols] * s2 + cw_ref[1:2, cols] * s1 + cw_ref[2:3, cols] * x + cb_ref[:, cols]
            cv_ref[rows, cols] = cv.astype(BF16)
            return cv

        ups = matmul(units[0])
        for n, unit in enumerate(units):
            ups_next = matmul(units[n + 1]) if n + 1 < len(units) else None
            gcv = conv(ups[0], cwg_ref, cbg_ref, carry_g, upg_ref, cvg_ref, unit)
            acv = conv(ups[1], cwa_ref, cba_ref, carry_a, upa_ref, cva_ref, unit)
            act_ref[unit[1], unit[0]] = (gcv * _sigmoid(gcv) * acv).astype(BF16)
            ups = ups_next

    def cols_of(rows, offset):
        return pl.BlockSpec((rows, tn), lambda j, i: (0, j + offset))

    tile = pl.BlockSpec((tm, tn), lambda j, i: (i, j))
    out = jax.ShapeDtypeStruct((s, f), BF16)
    return pl.pallas_call(
        body, name=name, grid=(nj, s // tm),
        in_specs=[pl.BlockSpec((tm, d), lambda j, i: (i, 0)), cols_of(d, 0), cols_of(d, nj),
                  cols_of(3, 0), cols_of(3, nj), cols_of(1, 0), cols_of(1, nj)],
        out_specs=[tile] * 5, out_shape=[out] * 5,
        scratch_shapes=[pltpu.VMEM((FUSE_HALO, tn), F32)] * 2,
        compiler_params=_params(("parallel", "arbitrary")),
    )(h, w_up, w_up, cw, cw, cb, cb)


def _ffn_down_dx_fused(dxb, w_down, up, cv, cw, after, name):
    s, d = dxb.shape
    f = w_down.shape[0]
    tm = _pick(s, 1024, LANES)
    tn = _pick(f, 512, FUSE_STRIP)
    nj, ni = f // tn, s // tm
    n_steps = nj * ni
    sub = tm // FUSE_HALO

    def body(dx_ref, w_ref, upg_ref, upa_ref, cvg_ref, cva_ref, cwg_ref, cwa_ref, after_ref,
             dup_hbm, dcwg_ref, dcwa_ref, dcbg_ref, dcba_ref, out_buf, out_sem, carry, acc):
        j, i = pl.program_id(0), pl.program_id(1)
        step = j * ni + i
        slot = lax.rem(step, 2)
        row0 = pl.multiple_of((ni - 1 - i) * tm, tm)

        def out_copy(half):
            col0 = pl.multiple_of(half * f + j * tn, LANES)
            return pltpu.make_async_copy(out_buf.at[slot, half], dup_hbm.at[pl.ds(row0, tm), pl.ds(col0, tn)],
                                         out_sem.at[slot, half])

        @pl.when(step >= 2)
        def _():
            for half in range(2):
                out_copy(half).wait()

        @pl.when(i == 0)
        def _():
            carry[...] = jnp.zeros_like(carry)
            acc[...] = jnp.zeros_like(acc)

        for st in range(tn // FUSE_STRIP):
            cols = slice(st * FUSE_STRIP, (st + 1) * FUSE_STRIP)
            dact = lax.dot_general(dx_ref[...], w_ref[cols, :], _DIMS["nt"], preferred_element_type=F32)
            gcv = cvg_ref[:, cols].astype(F32)
            acv = cva_ref[:, cols].astype(F32)
            sg = _sigmoid(gcv)
            dd = (dact * acv * (sg * (1.0 + gcv * (1.0 - sg))), dact * (gcv * sg))
            for half, (up_ref, cw_ref) in enumerate(((upg_ref, cwg_ref), (upa_ref, cwa_ref))):
                x = up_ref[:, cols].astype(F32)
                d0 = dd[half]
                ext = jnp.concatenate([d0, carry[half, :, cols]], axis=0)
                d1 = pltpu.roll(ext, tm + FUSE_HALO - 1, 0)[:tm, :]
                d2 = pltpu.roll(ext, tm + FUSE_HALO - 2, 0)[:tm, :]
                carry[half, :, cols] = d0[:FUSE_HALO, :]
                out_buf[slot, half, :, cols] = (cw_ref[2:3, cols] * d0 + cw_ref[1:2, cols] * d1
                                                + cw_ref[0:1, cols] * d2).astype(BF16)
                for k, term in enumerate((d2 * x, d1 * x, d0 * x, d0)):
                    acc[half, k, :, cols] += jnp.sum(term.reshape(sub, FUSE_HALO, FUSE_STRIP), axis=0)

        for half in range(2):
            out_copy(half).start()

        @pl.when(i == ni - 1)
        def _():
            for half, (dcw_ref, dcb_ref) in enumerate(((dcwg_ref, dcbg_ref), (dcwa_ref, dcba_ref))):
                for k in range(3):
                    dcw_ref[k:k + 1, :] = jnp.sum(acc[half, k], axis=0, keepdims=True)
                dcb_ref[...] = jnp.sum(acc[half, 3], axis=0, keepdims=True)

        @pl.when(step == n_steps - 1)
        def _():
            for half in range(2):
                out_copy(half).wait()
                if n_steps > 1:
                    pltpu.make_async_copy(out_buf.at[1 - slot, half], dup_hbm.at[pl.ds(row0, tm), pl.ds(0, tn)],
                                          out_sem.at[1 - slot, half]).wait()

    tile = pl.BlockSpec((tm, tn), lambda j, i: (ni - 1 - i, j))

    def cols_of(rows, offset):
        return pl.BlockSpec((rows, tn), lambda j, i: (0, j + offset))

    small = pl.BlockSpec((3, tn), lambda j, i: (0, j)), pl.BlockSpec((1, tn), lambda j, i: (0, j))
    return pl.pallas_call(
        body, name=name, grid=(nj, ni),
        in_specs=[pl.BlockSpec((tm, d), lambda j, i: (ni - 1 - i, 0)), pl.BlockSpec((tn, d), lambda j, i: (j, 0)),
                  tile, tile, tile, tile, cols_of(3, 0), cols_of(3, nj), pl.BlockSpec(memory_space=pl.ANY)],
        out_specs=[pl.BlockSpec(memory_space=pl.ANY), small[0], small[0], small[1], small[1]],
        out_shape=[jax.ShapeDtypeStruct((s, 2 * f), BF16), jax.ShapeDtypeStruct((3, f), F32),
                   jax.ShapeDtypeStruct((3, f), F32), jax.ShapeDtypeStruct((1, f), F32),
                   jax.ShapeDtypeStruct((1, f), F32)],
        scratch_shapes=[pltpu.VMEM((2, 2, tm, tn), BF16), pltpu.SemaphoreType.DMA((2, 2)),
                        pltpu.VMEM((2, FUSE_HALO, tn), F32), pltpu.VMEM((2, 4, FUSE_HALO, tn), F32)],
        compiler_params=_params(("arbitrary", "arbitrary")),
    )(dxb, w_down, up[0], up[1], cv[0], cv[1], cw, cw, after)


_GELU_C = math.sqrt(2.0 / math.pi)


def _gelu(x):
    th = jnp.tanh(_GELU_C * (x + 0.044715 * (x * x * x)))
    return x * (0.5 * (1.0 + th)), th


def _gelu_grad(x, th):
    return 0.5 * (1.0 + th) + 0.5 * x * (1.0 - th * th) * (_GELU_C * (1.0 + 3.0 * 0.044715 * (x * x)))


def _masked_ws(ws_ref, h):
    t = lax.broadcasted_iota(jnp.int32, (CHUNK, CHUNK), 0)
    sx = lax.broadcasted_iota(jnp.int32, (CHUNK, CHUNK), 1)
    return jnp.where(sx <= t, ws_ref[h], 0.0)


def _mixb_fwd(pre, gv, ws, bs_wide, name):
    s, w2 = pre.shape
    w = w2 // 2
    gw = w // SG_GROUPS

    def body(pre_ref, gv_ref, ws_ref, bs_ref, o_ref):
        zu, _ = _gelu(pre_ref[:, :w].astype(F32))
        zv, _ = _gelu(pre_ref[:, w:].astype(F32))
        _, vhat = _rms_stats(zv)
        vn = (vhat * gv_ref[...]).astype(BF16)
        for h in range(SG_GROUPS):
            cols = slice(h * gw, (h + 1) * gw)
            wsm = _masked_ws(ws_ref, h).astype(BF16)
            gate = jnp.dot(wsm, vn[:, cols], preferred_element_type=F32)
            gate = gate + jnp.tile(bs_ref[h], (1, gw // LANES))
            o_ref[:, cols] = (zu[:, cols] * gate).astype(BF16)

    return pl.pallas_call(
        body, name=name, grid=(s // CHUNK,),
        in_specs=[pl.BlockSpec((CHUNK, w2), lambda i: (i, 0)), pl.BlockSpec((1, w), lambda i: (0, 0)),
                  pl.BlockSpec((SG_GROUPS, CHUNK, CHUNK), lambda i: (0, 0, 0)),
                  pl.BlockSpec((SG_GROUPS, CHUNK, LANES), lambda i: (0, 0, 0))],
        out_specs=pl.BlockSpec((CHUNK, w), lambda i: (i, 0)),
        out_shape=jax.ShapeDtypeStruct((s, w), BF16),
        compiler_params=_params(("parallel",)),
    )(pre, gv, ws, bs_wide)


def _mixb_bwd(pre, dug, gv, ws, bs_wide, name):
    s, w2 = pre.shape
    w = w2 // 2
    gw = w // SG_GROUPS

    def body(pre_ref, dug_ref, gv_ref, ws_ref, bs_ref, o_ref, dws_ref, dbs_ref, dgv_ref, dvn_ref):
        first = pl.program_id(0) == 0

        @pl.when(first)
        def _():
            dws_ref[...] = jnp.zeros_like(dws_ref)
            dbs_ref[...] = jnp.zeros_like(dbs_ref)

        pu = pre_ref[:, :w].astype(F32)
        pv = pre_ref[:, w:].astype(F32)
        zu, thu = _gelu(pu)
        zv, thv = _gelu(pv)
        inv, vhat = _rms_stats(zv)
        gvv = gv_ref[...]
        vn = (vhat * gvv).astype(BF16)
        for h in range(SG_GROUPS):
            cols = slice(h * gw, (h + 1) * gw)
            wsm = _masked_ws(ws_ref, h).astype(BF16)
            gate = jnp.dot(wsm, vn[:, cols], preferred_element_type=F32)
            gate = gate + jnp.tile(bs_ref[h], (1, gw // LANES))
            dug_h = dug_ref[:, cols].astype(F32)
            dgate = dug_h * zu[:, cols]
            dgate_b = dgate.astype(BF16)
            o_ref[:, cols] = (dug_h * gate * _gelu_grad(pu[:, cols], thu[:, cols])).astype(BF16)
            dbs_ref[h] += jnp.broadcast_to(jnp.sum(dgate, axis=-1, keepdims=True), (CHUNK, LANES))
            dws = lax.dot_general(dgate_b, vn[:, cols], _DIMS["nt"], preferred_element_type=F32)
            t = lax.broadcasted_iota(jnp.int32, (CHUNK, CHUNK), 0)
            sx = lax.broadcasted_iota(jnp.int32, (CHUNK, CHUNK), 1)
            dws_ref[h] += jnp.where(sx <= t, dws, 0.0)
            dvn_ref[:, cols] = lax.dot_general(wsm, dgate_b, _DIMS["tn"], preferred_element_type=F32)
        dvn = dvn_ref[...]
        part = jnp.sum(dvn * vhat, axis=0, keepdims=True)

        @pl.when(first)
        def _():
            dgv_ref[...] = part

        @pl.when(jnp.logical_not(first))
        def _():
            dgv_ref[...] += part

        dvhat = dvn * gvv
        dzv = inv * (dvhat - vhat * jnp.mean(dvhat * vhat, axis=-1, keepdims=True))
        o_ref[:, w:] = (dzv * _gelu_grad(pv, thv)).astype(BF16)

    return pl.pallas_call(
        body, name=name, grid=(s // CHUNK,),
        in_specs=[pl.BlockSpec((CHUNK, w2), lambda i: (i, 0)), pl.BlockSpec((CHUNK, w), lambda i: (i, 0)),
                  pl.BlockSpec((1, w), lambda i: (0, 0)),
                  pl.BlockSpec((SG_GROUPS, CHUNK, CHUNK), lambda i: (0, 0, 0)),
                  pl.BlockSpec((SG_GROUPS, CHUNK, LANES), lambda i: (0, 0, 0))],
        out_specs=[pl.BlockSpec((CHUNK, w2), lambda i: (i, 0)),
                   pl.BlockSpec((SG_GROUPS, CHUNK, CHUNK), lambda i: (0, 0, 0)),
                   pl.BlockSpec((SG_GROUPS, CHUNK, LANES), lambda i: (0, 0, 0)),
                   pl.BlockSpec((1, w), lambda i: (0, 0))],
        out_shape=[jax.ShapeDtypeStruct((s, w2), BF16), jax.ShapeDtypeStruct((SG_GROUPS, CHUNK, CHUNK), F32),
                   jax.ShapeDtypeStruct((SG_GROUPS, CHUNK, LANES), F32), jax.ShapeDtypeStruct((1, w), F32)],
        scratch_shapes=[pltpu.VMEM((CHUNK, w), F32)],
        compiler_params=_params(("arbitrary",)),
    )(pre, dug, gv, ws, bs_wide)


def _cast_layer(w3, layer, name):
    _, r, c = w3.shape
    tr = _pick(r, 256, SLAB)

    def body(w_ref, o_ref):
        o_ref[...] = w_ref[...].astype(BF16)

    return pl.pallas_call(
        body, name=name, grid=(r // tr,),
        in_specs=[pl.BlockSpec((None, tr, c), lambda i: (layer, i, 0))],
        out_specs=pl.BlockSpec((tr, c), lambda i: (i, 0)),
        out_shape=jax.ShapeDtypeStruct((r, c), BF16),
        compiler_params=_params(("parallel",)),
    )(w3)


def _adamw_math(w, g, m, v):
    m = ADAM_B1 * m + (1.0 - ADAM_B1) * g
    v = ADAM_B2 * v + (1.0 - ADAM_B2) * (g * g)
    m_hat = m / (1.0 - ADAM_B1 ** ADAM_STEP)
    v_hat = v / (1.0 - ADAM_B2 ** ADAM_STEP)
    delta = -ADAM_LR * (m_hat / (jnp.sqrt(v_hat) + ADAM_EPS) + ADAM_WD * w)
    return delta, m, v


def _adamw_sharded(recvs, w, m, v, name):
    nl, r, c = w.shape
    tc = _pick(c, 1536, LANES)
    tr = _pick(r, 64, SLAB)

    def body(*refs):
        recv_refs = refs[:nl]
        w_ref, m_ref, v_ref, g_ref, d_ref, nm_ref, nv_ref = refs[nl:]
        for layer, recv_ref in enumerate(recv_refs):
            @pl.when(pl.program_id(0) == layer)
            def _():
                g = recv_ref[0].astype(F32)
                for q in range(1, N_DEV):
                    g = g + recv_ref[q].astype(F32)
                delta, nm, nv = _adamw_math(w_ref[...], g, m_ref[...], v_ref[...])
                g_ref[...] = g
                d_ref[...] = delta
                nm_ref[...] = nm
                nv_ref[...] = nv

    def recv_spec(layer):
        return pl.BlockSpec((N_DEV, tr, tc),
                            lambda l, i, j: (0, jnp.where(l == layer, i, 0), jnp.where(l == layer, j, 0)))

    blk = pl.BlockSpec((None, tr, tc), lambda l, i, j: (l, i, j))
    out = jax.ShapeDtypeStruct((nl, r, c), F32)
    return pl.pallas_call(
        body, name=name, grid=(nl, r // tr, c // tc),
        in_specs=[recv_spec(layer) for layer in range(nl)] + [blk, blk, blk],
        out_specs=[blk] * 4, out_shape=[out] * 4,
        compiler_params=_params(("parallel",) * 3),
    )(*recvs, w, m, v)


def _adamw_packed(w, g, m, v, name):
    r, c = w.shape
    tr = _pick(r, 256, 8)

    def body(w_ref, g_ref, m_ref, v_ref, d_ref, nm_ref, nv_ref):
        delta, nm, nv = _adamw_math(w_ref[...], g_ref[...], m_ref[...], v_ref[...])
        d_ref[...] = delta
        nm_ref[...] = nm
        nv_ref[...] = nv

    blk = pl.BlockSpec((tr, c), lambda i: (i, 0))
    out = jax.ShapeDtypeStruct((r, c), F32)
    return pl.pallas_call(
        body, name=name, grid=(r // tr,), in_specs=[blk] * 4, out_specs=[blk] * 3, out_shape=[out] * 3,
        compiler_params=_params(("parallel",)),
    )(w, g, m, v)


def _pack(arrays):
    parts = []
    for a in arrays:
        flat = a.reshape(-1).astype(F32)
        pad = (-flat.shape[0]) % PACK_GRANULE
        parts.append(jnp.pad(flat, (0, pad)) if pad else flat)
    return jnp.concatenate(parts).reshape(-1, LANES)


def _unpack(buf, shapes):
    flat = buf.reshape(-1)
    out, off = [], 0
    for shp in shapes:
        n = math.prod(shp)
        out.append(flat[off:off + n].reshape(shp))
        off += n + (-n) % PACK_GRANULE
    return out


def _mesh_pos():
    return lax.axis_index("x"), lax.axis_index("y"), lax.axis_index("c")


def _coords(q):
    return q // 4, (q // 2) % 2, q % 2


def _shard_of(ref, q, shard_shape, axis):
    r, c = shard_shape
    if axis == 0:
        return ref.at[pl.ds(pl.multiple_of(q * r, SLAB), r), :]
    return ref.at[:, pl.ds(pl.multiple_of(q * c, LANES), c)]


_HBM = pl.BlockSpec(memory_space=pltpu.HBM)
_SEM = pl.BlockSpec(memory_space=pltpu.SEMAPHORE)
_EFFECT = pltpu.SideEffectType.DATAFLOW_SIDE_EFFECTING


def _exchange_shapes(gather, src_shape, axis):
    r, c = src_shape
    if gather:
        return (r, c), ((r * N_DEV, c) if axis == 0 else (r, c * N_DEV))
    shard = (r // N_DEV, c) if axis == 0 else (r, c // N_DEV)
    return shard, (N_DEV,) + shard


def _exchange_copies(gather, src, land, sems, axis):
    send_sems, recv_sems, own_sem = sems
    x, y, c_ = _mesh_pos()
    me = 4 * x + 2 * y + c_
    shard, _ = _exchange_shapes(gather, src.shape, axis)

    def piece(q):
        return src if gather else _shard_of(src, q, shard, axis)

    def place(q):
        return _shard_of(land, q, shard, axis) if gather else land.at[q]

    own = pltpu.make_async_copy(piece(me), place(me), own_sem.at[0])
    sends, arrivals = [], []
    for step in range(1, N_DEV):
        to = (me + step) % N_DEV
        frm = (me + N_DEV - step) % N_DEV
        sends.append(pltpu.make_async_remote_copy(
            src_ref=piece(to), dst_ref=place(me), send_sem=send_sems.at[step - 1], recv_sem=recv_sems.at[step - 1],
            device_id=_coords(to), device_id_type=MESH))
        arrivals.append(pltpu.make_async_remote_copy(
            src_ref=piece(me), dst_ref=place(frm), send_sem=send_sems.at[step - 1], recv_sem=recv_sems.at[step - 1],
            device_id=_coords(frm), device_id_type=MESH))
    return own, sends, arrivals


def _exchange_start(gather, src, axis, name, after=None):
    _, land_shape = _exchange_shapes(gather, src.shape, axis)
    extra = () if after is None else (after,)

    def body(*refs):
        src_ref, land = refs[:2]
        send_sems, recv_sems, own_sem = refs[2 + len(extra):5 + len(extra)]
        own, sends, _ = _exchange_copies(gather, src_ref, land, (send_sems, recv_sems, own_sem), axis)
        own.start()
        for cp in sends:
            cp.start()
        refs[-1][...] = jnp.zeros_like(refs[-1])

    out = pl.pallas_call(
        body, name=name,
        out_shape=(pltpu.SemaphoreType.DMA((N_DEV - 1,)), pltpu.SemaphoreType.DMA((N_DEV - 1,)),
                   pltpu.SemaphoreType.DMA((1,)), pltpu.HBM(src.shape, src.dtype),
                   pltpu.HBM(land_shape, src.dtype), jax.ShapeDtypeStruct((8, LANES), F32)),
        in_specs=[_HBM, _HBM] + [pl.BlockSpec(memory_space=pl.ANY)] * len(extra),
        out_specs=(_SEM, _SEM, _SEM, _HBM, _HBM, pl.BlockSpec(memory_space=pltpu.VMEM)),
        input_output_aliases={0: 3, 1: 4},
        compiler_params=pltpu.CompilerParams(has_side_effects=_EFFECT),
    )(pltpu.with_memory_space_constraint(src, pltpu.HBM),
      pltpu.with_memory_space_constraint(lax.empty(land_shape, src.dtype), pltpu.HBM), *extra)
    return out[:5], out[5]


def _exchange_wait(gather, state, axis, after, name):
    send_sems, recv_sems, own_sem, src_thru, land_thru = state
    after = tuple(after) if isinstance(after, (tuple, list)) else (after,)

    def body(src, land, send_sems, recv_sems, own_sem, *rest):
        own, sends, arrivals = _exchange_copies(gather, src, land, (send_sems, recv_sems, own_sem), axis)
        for cp in sends:
            cp.wait_send()
        for cp in arrivals:
            cp.wait_recv()
        own.wait()

    return pl.pallas_call(
        body, name=name,
        out_shape=(pltpu.HBM(src_thru.shape, src_thru.dtype), pltpu.HBM(land_thru.shape, land_thru.dtype)),
        in_specs=[_HBM, _HBM, _SEM, _SEM, _SEM] + [pl.BlockSpec(memory_space=pl.ANY)] * len(after),
        out_specs=(_HBM, _HBM),
        input_output_aliases={0: 0, 1: 1},
        compiler_params=pltpu.CompilerParams(has_side_effects=_EFFECT),
    )(src_thru, land_thru, send_sems, recv_sems, own_sem, *after)[1]


def _gather2_copies(shard_ref, land, sems, axis, shard_shape):
    send1, recv1, own_sem, send2, recv2 = sems
    x, y, c = _mesh_pos()
    me, sibling = (x, y, c), (x, y, 1 - c)
    chips = [(1 - x, y), (x, 1 - y), (1 - x, 1 - y)]

    def region(dev):
        px, py, pc = dev
        return _shard_of(land, 4 * px + 2 * py + pc, shard_shape, axis)

    def copy(src, block, to, send, recv):
        return pltpu.make_async_remote_copy(src_ref=src, dst_ref=region(block), send_sem=send, recv_sem=recv,
                                            device_id=to, device_id_type=MESH)

    own = pltpu.make_async_copy(shard_ref, region(me), own_sem.at[0])
    peers = [sibling] + [(*chip, c) for chip in chips]
    sends1 = [copy(shard_ref, me, to, send1.at[k], recv1.at[k]) for k, to in enumerate(peers)]
    arrivals1 = [copy(shard_ref, frm, frm, send1.at[k], recv1.at[k]) for k, frm in enumerate(peers)]
    sends2, arrivals2 = [], []
    if send2 is not None:
        for j, chip in enumerate(chips):
            sends2.append(copy(region((*chip, c)), (*chip, c), sibling, send2.at[j], recv2.at[j]))
            arrivals2.append(copy(region((*chip, 1 - c)), (*chip, 1 - c), sibling, send2.at[j], recv2.at[j]))
    return own, sends1, arrivals1, sends2, arrivals2


def _gather2_start(shard, axis, name, after=None):
    _, land_shape = _exchange_shapes(True, shard.shape, axis)
    extra = () if after is None else (after,)

    def body(*refs):
        src_ref, land = refs[:2]
        send1, recv1, own_sem = refs[2 + len(extra):5 + len(extra)]
        own, sends1, _, _, _ = _gather2_copies(src_ref, land, (send1, recv1, own_sem, None, None), axis, shard.shape)
        own.start()
        for cp in sends1[1:] + sends1[:1]:
            cp.start()
        refs[-1][...] = jnp.zeros_like(refs[-1])

    out = pl.pallas_call(
        body, name=name,
        out_shape=(pltpu.SemaphoreType.DMA((4,)), pltpu.SemaphoreType.DMA((4,)), pltpu.SemaphoreType.DMA((1,)),
                   pltpu.HBM(shard.shape, shard.dtype), pltpu.HBM(land_shape, shard.dtype),
                   jax.ShapeDtypeStruct((8, LANES), F32)),
        in_specs=[_HBM, _HBM] + [pl.BlockSpec(memory_space=pl.ANY)] * len(extra),
        out_specs=(_SEM, _SEM, _SEM, _HBM, _HBM, pl.BlockSpec(memory_space=pltpu.VMEM)),
        input_output_aliases={0: 3, 1: 4},
        compiler_params=pltpu.CompilerParams(has_side_effects=_EFFECT),
    )(pltpu.with_memory_space_constraint(shard, pltpu.HBM),
      pltpu.with_memory_space_constraint(lax.empty(land_shape, shard.dtype), pltpu.HBM), *extra)
    return out[:5], out[5]


def _gather2_pass(state, axis, after, name):
    send1, recv1, own_sem, shard_thru, land_thru = state

    def body(src_ref, land, send1, recv1, own_sem, after_ref, send2, recv2, src_out, land_out, token):
        _, _, arrivals1, sends2, _ = _gather2_copies(src_ref, land, (send1, recv1, own_sem, send2, recv2), axis,
                                                     shard_thru.shape)
        for arrival, fwd in zip(arrivals1[1:], sends2):
            arrival.wait_recv()
            fwd.start()
        token[...] = jnp.zeros_like(token)

    out = pl.pallas_call(
        body, name=name,
        out_shape=(pltpu.SemaphoreType.DMA((3,)), pltpu.SemaphoreType.DMA((3,)),
                   pltpu.HBM(shard_thru.shape, shard_thru.dtype), pltpu.HBM(land_thru.shape, land_thru.dtype),
                   jax.ShapeDtypeStruct((8, LANES), F32)),
        in_specs=[_HBM, _HBM, _SEM, _SEM, _SEM, pl.BlockSpec(memory_space=pl.ANY)],
        out_specs=(_SEM, _SEM, _HBM, _HBM, pl.BlockSpec(memory_space=pltpu.VMEM)),
        input_output_aliases={0: 2, 1: 3},
        compiler_params=pltpu.CompilerParams(has_side_effects=_EFFECT),
    )(shard_thru, land_thru, send1, recv1, own_sem, after)
    return (send1, recv1, own_sem, out[0], out[1], out[2], out[3]), out[4]


def _gather2_wait(state, axis, after, name):
    send1, recv1, own_sem, send2, recv2, shard_thru, land_thru = state

    def body(src_ref, land, send1, recv1, own_sem, send2, recv2, after_ref, src_dead, got):
        own, sends1, arrivals1, sends2, arrivals2 = _gather2_copies(
            src_ref, land, (send1, recv1, own_sem, send2, recv2), axis, shard_thru.shape)
        for cp in sends1 + sends2:
            cp.wait_send()
        for cp in arrivals1[:1] + arrivals2:
            cp.wait_recv()
        own.wait()

    return pl.pallas_call(
        body, name=name,
        out_shape=(pltpu.HBM(shard_thru.shape, shard_thru.dtype), pltpu.HBM(land_thru.shape, land_thru.dtype)),
        in_specs=[_HBM, _HBM] + [_SEM] * 5 + [pl.BlockSpec(memory_space=pl.ANY)],
        out_specs=(_HBM, _HBM),
        input_output_aliases={0: 0, 1: 1},
        compiler_params=pltpu.CompilerParams(has_side_effects=_EFFECT),
    )(shard_thru, land_thru, send1, recv1, own_sem, send2, recv2, after)[1]


def _sum_slots(slots, name):
    _, r, c = slots.shape
    tr = _pick(r, 512, 8)

    def body(s_ref, o_ref):
        total = s_ref[0]
        for q in range(1, N_DEV):
            total = total + s_ref[q]
        o_ref[...] = total

    return pl.pallas_call(
        body, name=name, grid=(r // tr,),
        in_specs=[pl.BlockSpec((N_DEV, tr, c), lambda i: (0, i, 0))],
        out_specs=pl.BlockSpec((tr, c), lambda i: (i, 0)),
        out_shape=jax.ShapeDtypeStruct((r, c), F32),
        compiler_params=_params(("parallel",)),
    )(slots)


def kernel(x, a_norm, a_in, a_conv, a_out, b_norm, b_in, b_vnorm, b_ws, b_bs, b_out, f_norm, f_up, f_conv_w, f_conv_b, f_down, final_norm, loss_target, m_a_norm, m_a_in, m_a_conv, m_a_out, m_b_norm, m_b_in, m_b_vnorm, m_b_ws, m_b_bs, m_b_out, m_f_norm, m_f_up, m_f_conv_w, m_f_conv_b, m_f_down, m_final_norm, v_a_norm, v_a_in, v_a_conv, v_a_out, v_b_norm, v_b_in, v_b_vnorm, v_b_ws, v_b_bs, v_b_out, v_f_norm, v_f_up, v_f_conv_w, v_f_conv_b, v_f_down, v_final_norm):
    s, d = x.shape[1], x.shape[2]
    n_ffn = f_up.shape[0]
    f2 = f_up.shape[2] * N_DEV
    me = 4 * lax.axis_index("x") + 2 * lax.axis_index("y") + lax.axis_index("c")
    x0 = x.reshape(s, d)
    target = loss_target.reshape(s, d)

    wanted = [("a_in", _cast_layer(a_in, 0, "cast_a_in"), 1),
              ("small", _pack([a_conv, b_norm, b_vnorm, f_conv_w]), 0),
              ("a_out", _cast_layer(a_out, 0, "cast_a_out"), 0),
              ("f_up0", _cast_layer(f_up, 0, "cast_f_up0"), 1), ("f_down0", _cast_layer(f_down, 0, "cast_f_down0"), 0),
              ("b_in", _cast_layer(b_in, 0, "cast_b_in"), 1), ("b_out", _cast_layer(b_out, 0, "cast_b_out"), 0),
              ("f_up1", _cast_layer(f_up, 1, "cast_f_up1"), 1), ("f_down1", _cast_layer(f_down, 1, "cast_f_down1"), 0)]
    coming, tok, h0 = {}, None, None
    for n_started, (key, shard, axis) in enumerate(wanted):
        if n_started == 2:
            tok = h0 = _rmsnorm_fwd(x0, a_norm, "mixa_norm", after=tok)
        state, tok = _gather2_start(shard, axis, f"ag_start_{key}", after=tok)
        coming[key] = (state, axis)

    def pass_on(keys, after):
        for key in keys:
            state, axis = coming[key]
            state, after = _gather2_pass(state, axis, after, f"ag_pass_{key}")
            coming[key] = (state, axis)
        return after

    def arrived(key, after):
        state, axis = coming[key]
        return _gather2_wait(state, axis, after, f"ag_wait_{key}")

    cshard = a_conv.shape[2]
    fshard = f_conv_w.shape[2]
    w_a_in = arrived("a_in", pass_on(["a_in", "small"], tok))
    small_full = arrived("small", w_a_in)
    small_rows = small_full.reshape(N_DEV, -1)
    per_dev = _unpack_rows(small_rows, [(3, cshard), (cshard,), (cshard,), (n_ffn, 3, fshard)])
    a_conv_full = per_dev[0].transpose(1, 0, 2).reshape(3, d)
    b_norm_full = per_dev[1].reshape(1, d)
    b_vnorm_full = per_dev[2].reshape(1, d)
    f_conv_w_full = per_dev[3].transpose(1, 2, 0, 3).reshape(n_ffn, 3, f2)
    bs_wide = jnp.broadcast_to(b_bs[0][:, :, None], (SG_GROUPS, CHUNK, LANES))
    ws = b_ws[0]

    w_f_up, w_f_down = {}, {}

    def ffn_forward(xin, l, pass_early, pass_late):
        h = _rmsnorm_fwd(xin, f_norm[l:l + 1], f"ffn{l}_norm")
        w_f_up[l] = arrived(f"f_up{l}", pass_on(pass_early, h))
        up_g, up_a, cv_g, cv_a, act = _ffn_up_fused(h, w_f_up[l], f_conv_w_full[l], f_conv_b[l:l + 1], f"ffn{l}_up")
        up, cv = (up_g, up_a), (cv_g, cv_a)
        w_f_down[l] = arrived(f"f_down{l}", pass_on(pass_late, act))
        xout = _matmul(act, w_f_down[l], "nn", F32, f"ffn{l}_down", resid=xin, tk_cap=1408)
        return xout, (h, up, act, cv)

    bcx = _matmul(h0, w_a_in, "nn", BF16, "mixa_in")
    ya, cva = _mixa_fwd(bcx, a_conv_full, "mixa_mid")
    w_a_out = arrived("a_out", pass_on(["a_out", "f_up0"], ya))
    x1 = _matmul(ya, w_a_out, "nn", F32, "mixa_out", resid=x0)
    x2, saved0 = ffn_forward(x1, 0, ["f_down0"], ["b_in", "b_out", "f_up1", "f_down1"])
    h2 = _rmsnorm_fwd(x2, b_norm_full, "mixb_norm")
    w_b_in = arrived("b_in", h2)
    pre = _matmul(h2, w_b_in, "nn", BF16, "mixb_in")
    ug = _mixb_fwd(pre, b_vnorm_full, ws, bs_wide, "mixb_mid")
    w_b_out = arrived("b_out", ug)
    x3 = _matmul(ug, w_b_out, "nn", F32, "mixb_out", resid=x2)
    x4, saved1 = ffn_forward(x3, 1, [], [])
    dx4, dx4b, loss_part, g_final = _final_loss(x4, final_norm.reshape(1, d), target, "loss_head")

    def _rs_start(grad, axis, name):
        return _exchange_start(False, grad, axis, name)

    def ffn_backward(xin, l, saved, dx, dxb):
        h, up, act, cv = saved
        g_down = _matmul(act, dxb, "tn", BF16, f"ffn{l}_down_dw", tm_cap=1408)
        rs_down, tok = _rs_start(g_down, 0, f"rs_start_f_down{l}")
        dup, cwg, cwa, cbg, cba = _ffn_down_dx_fused(dxb, w_f_down[l], up, cv, f_conv_w_full[l], tok,
                                                     f"ffn{l}_down_dx")
        g_cw, g_cb = jnp.concatenate([cwg, cwa], axis=1), jnp.concatenate([cbg, cba], axis=1)
        g_up = _matmul(h, dup, "tn", BF16, f"ffn{l}_up_dw", tk_cap=4096)
        rs_up, tok = _rs_start(g_up, 1, f"rs_start_f_up{l}")
        dh = _matmul(dup, w_f_up[l], "nt", F32, f"ffn{l}_up_dx", after=tok)
        dxin, dxinb, g_norm = _rmsnorm_bwd(xin, f_norm[l:l + 1], dh, dx, f"ffn{l}_norm_bwd")
        return dxin, dxinb, (rs_up, rs_down, g_cw, g_cb, g_norm)

    dx3, dx3b, gf1 = ffn_backward(x3, 1, saved1, dx4, dx4b)
    g_b_out = _matmul(ug, dx3b, "tn", BF16, "mixb_out_dw", tk_cap=4096)
    rs_b_out, tok = _rs_start(g_b_out, 0, "rs_start_b_out")
    dug = _matmul(dx3b, w_b_out, "nt", BF16, "mixb_out_dx", after=tok)
    dpre, g_ws, g_bs_wide, g_bvnorm = _mixb_bwd(pre, dug, b_vnorm_full, ws, bs_wide, "mixb_mid_bwd")
    g_b_in = _matmul(h2, dpre, "tn", BF16, "mixb_in_dw", tk_cap=4096)
    rs_b_in, tok = _rs_start(g_b_in, 1, "rs_start_b_in")
    dh2 = _matmul(dpre, w_b_in, "nt", F32, "mixb_in_dx", after=tok)
    dx2, dx2b, g_bnorm = _rmsnorm_bwd(x2, b_norm_full, dh2, dx3, "mixb_norm_bwd")
    dx1, dx1b, gf0 = ffn_backward(x1, 0, saved0, dx2, dx2b)
    g_a_out = _matmul(ya, dx1b, "tn", BF16, "mixa_out_dw", tk_cap=4096)
    rs_a_out, tok = _rs_start(g_a_out, 0, "rs_start_a_out")
    dya = _matmul(dx1b, w_a_out, "nt", BF16, "mixa_out_dx", after=tok)
    dbcx, g_aconv = _mixa_bwd(bcx, cva, dya, a_conv_full, "mixa_mid_bwd")
    g_a_in = _matmul(h0, dbcx, "tn", BF16, "mixa_in_dw", tk_cap=4096)
    rs_a_in, tok = _rs_start(g_a_in, 1, "rs_start_a_in")
    dh0 = _matmul(dbcx, w_a_in, "nt", F32, "mixa_in_dx", after=tok)
    grad_x, _, g_anorm = _rmsnorm_bwd(x0, a_norm, dh0, dx1, "mixa_norm_bwd")

    full_shapes = [(1, LANES), (1, d), (3, d), (1, d), (1, d), (SG_GROUPS, CHUNK, CHUNK), (SG_GROUPS, CHUNK),
                   (n_ffn, d), (n_ffn, 3, f2), (n_ffn, f2), (1, d)]
    parts = [loss_part, g_anorm, g_aconv, g_bnorm, g_bvnorm, g_ws, g_bs_wide[:, :, 0],
             jnp.concatenate([gf0[4], gf1[4]], axis=0), jnp.stack([gf0[2], gf1[2]]),
             jnp.concatenate([gf0[3], gf1[3]], axis=0), g_final]
    small_part = _pack(parts)
    small_state, small_tok = _exchange_start(True, small_part, 0, "ar_start_small", after=grad_x)

    big = {}
    for name, states, axis, w, m, v in (
            ("f_down", (gf0[1], gf1[1]), 0, f_down, m_f_down, v_f_down),
            ("f_up", (gf0[0], gf1[0]), 1, f_up, m_f_up, v_f_up),
            ("b_out", (rs_b_out,), 0, b_out, m_b_out, v_b_out), ("b_in", (rs_b_in,), 1, b_in, m_b_in, v_b_in),
            ("a_out", (rs_a_out,), 0, a_out, m_a_out, v_a_out), ("a_in", (rs_a_in,), 1, a_in, m_a_in, v_a_in)):
        recvs = [_exchange_wait(False, st, axis, small_tok, f"rs_wait_{name}{l}") for l, st in enumerate(states)]
        big[name] = _adamw_sharded(recvs, w, m, v, f"adamw_{name}")

    slots = _exchange_wait(True, small_state, 0, [res[0] for res in big.values()], "ar_wait_small")
    total = _sum_slots(slots.reshape((N_DEV,) + small_part.shape), "ar_sum_small")
    (loss_v, r_anorm, r_aconv, r_bnorm, r_bvnorm, r_ws, r_bs, r_fnorm, r_fcw, r_fcb, r_final) = _unpack(total, full_shapes)
    small_grads = [
        r_anorm,
        lax.dynamic_slice_in_dim(r_aconv, me * cshard, cshard, axis=1).reshape(a_conv.shape),
        lax.dynamic_slice_in_dim(r_bnorm, me * cshard, cshard, axis=1),
        lax.dynamic_slice_in_dim(r_bvnorm, me * cshard, cshard, axis=1),
        r_ws.reshape(b_ws.shape), r_bs.reshape(b_bs.shape), r_fnorm,
        lax.dynamic_slice_in_dim(r_fcw, me * fshard, fshard, axis=2),
        r_fcb, r_final.reshape(final_norm.shape)]
    small_w = [a_norm, a_conv, b_norm, b_vnorm, b_ws, b_bs, f_norm, f_conv_w, f_conv_b, final_norm]
    small_m = [m_a_norm, m_a_conv, m_b_norm, m_b_vnorm, m_b_ws, m_b_bs, m_f_norm, m_f_conv_w, m_f_conv_b, m_final_norm]
    small_v = [v_a_norm, v_a_conv, v_b_norm, v_b_vnorm, v_b_ws, v_b_bs, v_f_norm, v_f_conv_w, v_f_conv_b, v_final_norm]
    shapes = [w.shape for w in small_w]
    packed = _adamw_packed(_pack(small_w), _pack(small_grads), _pack(small_m), _pack(small_v), "adamw_small")
    s_delta, s_m, s_v = (_unpack(p, shapes) for p in packed)
    small_names = ["a_norm", "a_conv", "b_norm", "b_vnorm", "b_ws", "b_bs", "f_norm", "f_conv_w", "f_conv_b", "final_norm"]
    small = {nm: (small_grads[i], s_delta[i], s_m[i], s_v[i]) for i, nm in enumerate(small_names)}

    order = ["a_norm", "a_in", "a_conv", "a_out", "b_norm", "b_in", "b_vnorm", "b_ws", "b_bs", "b_out",
             "f_norm", "f_up", "f_conv_w", "f_conv_b", "f_down", "final_norm"]
    res = {nm: (big[nm] if nm in big else small[nm]) for nm in order}
    outs = [loss_v[0, 0], grad_x.reshape(x.shape)]
    for k in range(4):
        outs += [res[nm][k] for nm in order]
    return tuple(outs)


def _unpack_rows(rows, shapes):
    out, off = [], 0
    for shp in shapes:
        n = math.prod(shp)
        out.append(rows[:, off:off + n].reshape((N_DEV,) + tuple(shp)))
        off += n + (-n) % PACK_GRANULE
    return out
```

```python
import functools
import math

import jax
import jax.numpy as jnp
from jax import lax
from jax.experimental import pallas as pl
from jax.experimental.pallas import tpu as pltpu

F32 = jnp.float32
BF16 = jnp.bfloat16
MESH = pl.DeviceIdType.MESH

N_DEV = 8
RMS_EPS = 1e-5
CHUNK = 128
SG_GROUPS = 8
ADAM_LR = 0.001
ADAM_B1 = 0.9
ADAM_B2 = 0.999
ADAM_EPS = 1e-08
ADAM_WD = 0.01
ADAM_STEP = 10

LANES = 128
SLAB = 16
VMEM_LIMIT = 56 * 1024 * 1024
PACK_GRANULE = 8 * LANES


def _pick(dim, cap, mult):
    best = None
    t = mult
    while t <= min(dim, cap):
        if dim % t == 0:
            best = t
        t += mult
    return dim if best is None else best


def _params(semantics=None):
    return pltpu.CompilerParams(dimension_semantics=semantics, vmem_limit_bytes=VMEM_LIMIT)


_DIMS = {
    "nn": (((1,), (0,)), ((), ())),
    "nt": (((1,), (1,)), ((), ())),
    "tn": (((0,), (0,)), ((), ())),
}


def _matmul(a, b, mode, out_dtype, name, resid=None, after=None, tm_cap=1024, tn_cap=1024, tk_cap=2816):
    if mode == "nn":
        (m, k), n = a.shape, b.shape[1]
    elif mode == "nt":
        (m, k), n = a.shape, b.shape[0]
    else:
        (k, m), n = a.shape, b.shape[1]
    tm, tn, tk = _pick(m, tm_cap, LANES), _pick(n, tn_cap, LANES), _pick(k, tk_cap, LANES)
    nk = k // tk
    n_in = 2 + (resid is not None) + (after is not None)

    def body(*refs):
        a_ref, b_ref = refs[:2]
        r_ref = refs[2] if resid is not None else None
        o_ref = refs[n_in]
        prod = lax.dot_general(a_ref[...], b_ref[...], _DIMS[mode], preferred_element_type=F32)

        def finish(r):
            if r_ref is not None:
                r = r + r_ref[...]
            o_ref[...] = r.astype(out_dtype)

        if nk == 1:
            finish(prod)
            return
        acc_ref = refs[n_in + 1]
        kk = pl.program_id(2)

        @pl.when(kk == 0)
        def _():
            acc_ref[...] = prod

        @pl.when(jnp.logical_and(kk > 0, kk < nk - 1))
        def _():
            acc_ref[...] += prod

        @pl.when(kk == nk - 1)
        def _():
            finish(acc_ref[...] + prod)

    a_spec = (pl.BlockSpec((tk, tm), lambda i, j, kk: (kk, i)) if mode == "tn"
              else pl.BlockSpec((tm, tk), lambda i, j, kk: (i, kk)))
    b_spec = (pl.BlockSpec((tn, tk), lambda i, j, kk: (j, kk)) if mode == "nt"
              else pl.BlockSpec((tk, tn), lambda i, j, kk: (kk, j)))
    o_spec = pl.BlockSpec((tm, tn), lambda i, j, kk: (i, j))
    in_specs = [a_spec, b_spec] + ([o_spec] if resid is not None else [])
    args = (a, b) + ((resid,) if resid is not None else ())
    if after is not None:
        in_specs.append(pl.BlockSpec(memory_space=pl.ANY))
        args += (after,)
    return pl.pallas_call(
        body, name=name, grid=(m // tm, n // tn, nk),
        in_specs=in_specs, out_specs=o_spec,
        out_shape=jax.ShapeDtypeStruct((m, n), out_dtype),
        scratch_shapes=[pltpu.VMEM((tm, tn), F32)] if nk > 1 else [],
        compiler_params=_params(("parallel", "parallel", "arbitrary")),
    )(*args)


def _rms_stats(xf):
    inv = lax.rsqrt(jnp.mean(xf * xf, axis=-1, keepdims=True) + RMS_EPS)
    return inv, xf * inv


def _rmsnorm_fwd(x, g, name, after=None):
    s, d = x.shape
    tm = _pick(s, 256, SLAB)
    extra = () if after is None else (after,)

    def body(x_ref, g_ref, *rest):
        _, xhat = _rms_stats(x_ref[...])
        rest[-1][...] = (xhat * g_ref[...]).astype(BF16)

    return pl.pallas_call(
        body, name=name, grid=(s // tm,),
        in_specs=[pl.BlockSpec((tm, d), lambda i: (i, 0)), pl.BlockSpec((1, d), lambda i: (0, 0))]
        + [pl.BlockSpec(memory_space=pl.ANY)] * len(extra),
        out_specs=pl.BlockSpec((tm, d), lambda i: (i, 0)),
        out_shape=jax.ShapeDtypeStruct((s, d), BF16),
        compiler_params=_params(("parallel",)),
    )(x, g, *extra)


def _rmsnorm_bwd(x, g, dh, dx_out, name):
    s, d = x.shape
    tm = _pick(s, 256, SLAB)

    def body(x_ref, g_ref, dh_ref, dxo_ref, dxi_ref, dxib_ref, dg_ref):
        inv, xhat = _rms_stats(x_ref[...])
        dhv = dh_ref[...].astype(F32)
        dxhat = dhv * g_ref[...]
        proj = jnp.mean(dxhat * xhat, axis=-1, keepdims=True)
        dx = dxo_ref[...] + inv * (dxhat - xhat * proj)
        dxi_ref[...] = dx
        dxib_ref[...] = dx.astype(BF16)
        part = jnp.sum(dhv * xhat, axis=0, keepdims=True)

        @pl.when(pl.program_id(0) == 0)
        def _():
            dg_ref[...] = part

        @pl.when(pl.program_id(0) > 0)
        def _():
            dg_ref[...] += part

    row = pl.BlockSpec((tm, d), lambda i: (i, 0))
    vec = pl.BlockSpec((1, d), lambda i: (0, 0))
    return pl.pallas_call(
        body, name=name, grid=(s // tm,),
        in_specs=[row, vec, row, row], out_specs=[row, row, vec],
        out_shape=[jax.ShapeDtypeStruct((s, d), F32), jax.ShapeDtypeStruct((s, d), BF16),
                   jax.ShapeDtypeStruct((1, d), F32)],
        compiler_params=_params(("arbitrary",)),
    )(x, g, dh, dx_out)


def _final_loss(x, g, target, name):
    s, d = x.shape
    tm = _pick(s, 256, SLAB)

    def body(x_ref, g_ref, t_ref, dx_ref, dxb_ref, loss_ref, dg_ref):
        inv, xhat = _rms_stats(x_ref[...])
        gv = g_ref[...]
        err = xhat * gv - t_ref[...]
        loss = 0.5 * jnp.sum(jnp.mean(err * err, axis=-1, keepdims=True), axis=0, keepdims=True)
        dy = err * (1.0 / d)
        dxhat = dy * gv
        proj = jnp.mean(dxhat * xhat, axis=-1, keepdims=True)
        dx = inv * (dxhat - xhat * proj)
        dx_ref[...] = dx
        dxb_ref[...] = dx.astype(BF16)
        part = jnp.sum(dy * xhat, axis=0, keepdims=True)
        loss_row = jnp.broadcast_to(loss, (1, LANES))

        @pl.when(pl.program_id(0) == 0)
        def _():
            dg_ref[...] = part
            loss_ref[...] = loss_row

        @pl.when(pl.program_id(0) > 0)
        def _():
            dg_ref[...] += part
            loss_ref[...] += loss_row

    row = pl.BlockSpec((tm, d), lambda i: (i, 0))
    vec = pl.BlockSpec((1, d), lambda i: (0, 0))
    return pl.pallas_call(
        body, name=name, grid=(s // tm,),
        in_specs=[row, vec, row],
        out_specs=[row, row, pl.BlockSpec((1, LANES), lambda i: (0, 0)), vec],
        out_shape=[jax.ShapeDtypeStruct((s, d), F32), jax.ShapeDtypeStruct((s, d), BF16),
                   jax.ShapeDtypeStruct((1, LANES), F32), jax.ShapeDtypeStruct((1, d), F32)],
        compiler_params=_params(("arbitrary",)),
    )(x, g, target)


def _shift_up(cur, nxt, k):
    ext = jnp.concatenate([cur, nxt], axis=0)
    return pltpu.roll(ext, 2 * SLAB - k, 0)[:SLAB, :]


def _rows(s):
    return pl.ds(pl.multiple_of(s * SLAB, SLAB), SLAB)


def _next_halo_spec(tm, width, n_tiles):
    per = tm // SLAB
    return pl.BlockSpec((SLAB, width), lambda i: (jnp.minimum((i + 1) * per, n_tiles * per - 1), 0))


def _mixa_in_fused(h, w_in, wc, name):
    s, d = h.shape
    tm = _pick(s, 1024, LANES)
    tn = _pick(d, 512, FUSE_STRIP)
    nj = d // tn

    def body(h_ref, wb_ref, wg_ref, wx_ref, wc_ref, gb_ref, gc_ref, xs_ref, cv_ref, y_ref, carry):
        @pl.when(pl.program_id(1) == 0)
        def _():
            carry[...] = jnp.zeros_like(carry)

        def matmul(st):
            cols = slice(st * FUSE_STRIP, (st + 1) * FUSE_STRIP)
            return tuple(jnp.dot(h_ref[...], w_ref[:, cols], preferred_element_type=F32).astype(BF16)
                         for w_ref in (wb_ref, wg_ref, wx_ref))

        n_strips = tn // FUSE_STRIP
        parts = matmul(0)
        for st in range(n_strips):
            parts_next = matmul(st + 1) if st + 1 < n_strips else None
            cols = slice(st * FUSE_STRIP, (st + 1) * FUSE_STRIP)
            for ref, part in zip((gb_ref, gc_ref, xs_ref), parts):
                ref[:, cols] = part
            p = parts[1].astype(F32) * parts[2].astype(F32)
            ext = jnp.concatenate([carry[:, cols], p], axis=0)
            s1 = pltpu.roll(ext, 1, 0)[FUSE_HALO:, :]
            s2 = pltpu.roll(ext, 2, 0)[FUSE_HALO:, :]
            carry[:, cols] = p[tm - FUSE_HALO:, :]
            cv = wc_ref[0:1, cols] * s2 + wc_ref[1:2, cols] * s1 + wc_ref[2:3, cols] * p
            cv_ref[:, cols] = cv.astype(BF16)
            y_ref[:, cols] = (parts[0].astype(F32) * cv).astype(BF16)
            parts = parts_next

    def cols_of(rows, offset):
        return pl.BlockSpec((rows, tn), lambda j, i: (0, j + offset))

    tile = pl.BlockSpec((tm, tn), lambda j, i: (i, j))
    return pl.pallas_call(
        body, name=name, grid=(nj, s // tm),
        in_specs=[pl.BlockSpec((tm, d), lambda j, i: (i, 0)), cols_of(d, 0), cols_of(d, nj), cols_of(d, 2 * nj),
                  cols_of(3, 0)],
        out_specs=[tile] * 5, out_shape=[jax.ShapeDtypeStruct((s, d), BF16)] * 5,
        scratch_shapes=[pltpu.VMEM((FUSE_HALO, tn), F32)],
        compiler_params=_params(("parallel", "arbitrary")),
    )(h, w_in, w_in, w_in, wc)


def _mixa_bwd(bcx, cv, dy, wc, name):
    s, d = dy.shape
    d3 = 3 * d
    tm = _pick(s, 256, SLAB)
    w = _pick(d, 256, LANES)
    nslab = tm // SLAB
    n_tiles = s // tm

    def body(gb_ref, gc_ref, xs_ref, gbn_ref, cv_ref, dy_ref, dyn_ref, wc_ref, o_ref, dwc_ref, acc_ref):
        i = pl.program_id(0)
        last_tile = i == n_tiles - 1

        @pl.when(i == 0)
        def _():
            acc_ref[...] = jnp.zeros_like(acc_ref)

        for c in range(d // w):
            cb, cc, cx = (slice(g * d + c * w, g * d + (c + 1) * w) for g in range(3))
            cols = slice(c * w, (c + 1) * w)
            dcv_next = jnp.where(last_tile, 0.0, dyn_ref[:, cols].astype(F32) * gbn_ref[:, cols].astype(F32))

            def slab(j, dcv_nxt):
                r = _rows(nslab - 1 - j)
                gc = gc_ref[r, cols].astype(F32)
                xs = xs_ref[r, cols].astype(F32)
                dyv = dy_ref[r, cols].astype(F32)
                p = gc * xs
                d0 = dyv * gb_ref[r, cols].astype(F32)
                d1 = _shift_up(d0, dcv_nxt, 1)
                d2 = _shift_up(d0, dcv_nxt, 2)
                dp = wc_ref[2:3, cols] * d0 + wc_ref[1:2, cols] * d1 + wc_ref[0:1, cols] * d2
                for k, term in enumerate((d2 * p, d1 * p, d0 * p)):
                    acc_ref[k, :, cols] += term
                o_ref[r, cb] = (dyv * cv_ref[r, cols].astype(F32)).astype(BF16)
                o_ref[r, cc] = (dp * xs).astype(BF16)
                o_ref[r, cx] = (dp * gc).astype(BF16)
                return d0

            lax.fori_loop(0, nslab, slab, dcv_next)

        @pl.when(last_tile)
        def _():
            for k in range(3):
                dwc_ref[k:k + 1, :] = jnp.sum(acc_ref[k], axis=0, keepdims=True)

    nxt = _next_halo_spec(tm, d, n_tiles)
    row = pl.BlockSpec((tm, d), lambda i: (i, 0))
    return pl.pallas_call(
        body, name=name, grid=(n_tiles,),
        in_specs=[row, row, row, nxt, row, row, nxt, pl.BlockSpec((3, d), lambda i: (0, 0))],
        out_specs=[pl.BlockSpec((tm, d3), lambda i: (i, 0)), pl.BlockSpec((3, d), lambda i: (0, 0))],
        out_shape=[jax.ShapeDtypeStruct((s, d3), BF16), jax.ShapeDtypeStruct((3, d), F32)],
        scratch_shapes=[pltpu.VMEM((3, SLAB, d), F32)],
        compiler_params=_params(("arbitrary",)),
    )(bcx[0], bcx[1], bcx[2], bcx[0], cv, dy, dy, wc)


def _sigmoid(z):
    return 0.5 * jnp.tanh(0.5 * z) + 0.5


FUSE_STRIP = 256
FUSE_HALO = 8


def _ffn_up_fused(h, w_up, cw, cb, name):
    s, d = h.shape
    f = w_up.shape[1] // 2
    tm = _pick(s, 1024, LANES)
    tr = tm
    tn = _pick(f, 512, FUSE_STRIP)
    nj = f // tn

    def body(h_ref, wg_ref, wa_ref, cwg_ref, cwa_ref, cbg_ref, cba_ref,
             upg_ref, upa_ref, cvg_ref, cva_ref, act_ref, carry_g, carry_a):
        @pl.when(pl.program_id(1) == 0)
        def _():
            carry_g[...] = jnp.zeros_like(carry_g)
            carry_a[...] = jnp.zeros_like(carry_a)

        units = [(slice(st * FUSE_STRIP, (st + 1) * FUSE_STRIP), slice(rp * tr, (rp + 1) * tr))
                 for st in range(tn // FUSE_STRIP) for rp in range(tm // tr)]

        def matmul(unit):
            cols, rows = unit
            return tuple(jnp.dot(h_ref[rows, :], w_ref[:, cols], preferred_element_type=F32).astype(BF16)
                         for w_ref in (wg_ref, wa_ref))

        def conv(up, cw_ref, cb_ref, carry, up_ref, cv_ref, unit):
            cols, rows = unit
            up_ref[rows, cols] = up
            x = up.astype(F32)
            ext = jnp.concatenate([carry[:, cols], x], axis=0)
            s1 = pltpu.roll(ext, 1, 0)[FUSE_HALO:, :]
            s2 = pltpu.roll(ext, 2, 0)[FUSE_HALO:, :]
            carry[:, cols] = x[tr - FUSE_HALO:, :]
            cv = cw_ref[0:1, cols] * s2 + cw_ref[1:2, cols] * s1 + cw_ref[2:3, cols] * x + cb_ref[:, cols]
            cv_ref[rows, cols] = cv.astype(BF16)
            return cv

        ups = matmul(units[0])
        for n, unit in enumerate(units):
            ups_next = matmul(units[n + 1]) if n + 1 < len(units) else None
            gcv = conv(ups[0], cwg_ref, cbg_ref, carry_g, upg_ref, cvg_ref, unit)
            acv = conv(ups[1], cwa_ref, cba_ref, carry_a, upa_ref, cva_ref, unit)
            act_ref[unit[1], unit[0]] = (gcv * _sigmoid(gcv) * acv).astype(BF16)
            ups = ups_next

    def cols_of(rows, offset):
        return pl.BlockSpec((rows, tn), lambda j, i: (0, j + offset))

    tile = pl.BlockSpec((tm, tn), lambda j, i: (i, j))
    out = jax.ShapeDtypeStruct((s, f), BF16)
    return pl.pallas_call(
        body, name=name, grid=(nj, s // tm),
        in_specs=[pl.BlockSpec((tm, d), lambda j, i: (i, 0)), cols_of(d, 0), cols_of(d, nj),
                  cols_of(3, 0), cols_of(3, nj), cols_of(1, 0), cols_of(1, nj)],
        out_specs=[tile] * 5, out_shape=[out] * 5,
        scratch_shapes=[pltpu.VMEM((FUSE_HALO, tn), F32)] * 2,
        compiler_params=_params(("parallel", "arbitrary")),
    )(h, w_up, w_up, cw, cw, cb, cb)


def _ffn_down_dx_fused(dxb, w_down, up, cv, cw, after, name):
    s, d = dxb.shape
    f = w_down.shape[0]
    tm = _pick(s, 1024, LANES)
    tn = _pick(f, 512, FUSE_STRIP)
    nj, ni = f // tn, s // tm
    n_steps = nj * ni
    sub = tm // FUSE_HALO

    def body(dx_ref, w_ref, upg_ref, upa_ref, cvg_ref, cva_ref, cwg_ref, cwa_ref, after_ref,
             dup_hbm, dcwg_ref, dcwa_ref, dcbg_ref, dcba_ref, out_buf, out_sem, carry, acc):
        j, i = pl.program_id(0), pl.program_id(1)
        step = j * ni + i
        slot = lax.rem(step, 2)
        row0 = pl.multiple_of((ni - 1 - i) * tm, tm)

        def out_copy(half):
            col0 = pl.multiple_of(half * f + j * tn, LANES)
            return pltpu.make_async_copy(out_buf.at[slot, half], dup_hbm.at[pl.ds(row0, tm), pl.ds(col0, tn)],
                                         out_sem.at[slot, half])

        @pl.when(step >= 2)
        def _():
            for half in range(2):
                out_copy(half).wait()

        @pl.when(i == 0)
        def _():
            carry[...] = jnp.zeros_like(carry)
            acc[...] = jnp.zeros_like(acc)

        for st in range(tn // FUSE_STRIP):
            cols = slice(st * FUSE_STRIP, (st + 1) * FUSE_STRIP)
            dact = lax.dot_general(dx_ref[...], w_ref[cols, :], _DIMS["nt"], preferred_element_type=F32)
            gcv = cvg_ref[:, cols].astype(F32)
            acv = cva_ref[:, cols].astype(F32)
            sg = _sigmoid(gcv)
            dd = (dact * acv * (sg * (1.0 + gcv * (1.0 - sg))), dact * (gcv * sg))
            for half, (up_ref, cw_ref) in enumerate(((upg_ref, cwg_ref), (upa_ref, cwa_ref))):
                x = up_ref[:, cols].astype(F32)
                d0 = dd[half]
                ext = jnp.concatenate([d0, carry[half, :, cols]], axis=0)
                d1 = pltpu.roll(ext, tm + FUSE_HALO - 1, 0)[:tm, :]
                d2 = pltpu.roll(ext, tm + FUSE_HALO - 2, 0)[:tm, :]
                carry[half, :, cols] = d0[:FUSE_HALO, :]
                out_buf[slot, half, :, cols] = (cw_ref[2:3, cols] * d0 + cw_ref[1:2, cols] * d1
                                                + cw_ref[0:1, cols] * d2).astype(BF16)
                for k, term in enumerate((d2 * x, d1 * x, d0 * x, d0)):
                    acc[half, k, :, cols] += jnp.sum(term.reshape(sub, FUSE_HALO, FUSE_STRIP), axis=0)

        for half in range(2):
            out_copy(half).start()

        @pl.when(i == ni - 1)
        def _():
            for half, (dcw_ref, dcb_ref) in enumerate(((dcwg_ref, dcbg_ref), (dcwa_ref, dcba_ref))):
                for k in range(3):
                    dcw_ref[k:k + 1, :] = jnp.sum(acc[half, k], axis=0, keepdims=True)
                dcb_ref[...] = jnp.sum(acc[half, 3], axis=0, keepdims=True)

        @pl.when(step == n_steps - 1)
        def _():
            for half in range(2):
                out_copy(half).wait()
                if n_steps > 1:
                    pltpu.make_async_copy(out_buf.at[1 - slot, half], dup_hbm.at[pl.ds(row0, tm), pl.ds(0, tn)],
                                          out_sem.at[1 - slot, half]).wait()

    tile = pl.BlockSpec((tm, tn), lambda j, i: (ni - 1 - i, j))

    def cols_of(rows, offset):
        return pl.BlockSpec((rows, tn), lambda j, i: (0, j + offset))

    small = pl.BlockSpec((3, tn), lambda j, i: (0, j)), pl.BlockSpec((1, tn), lambda j, i: (0, j))
    return pl.pallas_call(
        body, name=name, grid=(nj, ni),
        in_specs=[pl.BlockSpec((tm, d), lambda j, i: (ni - 1 - i, 0)), pl.BlockSpec((tn, d), lambda j, i: (j, 0)),
                  tile, tile, tile, tile, cols_of(3, 0), cols_of(3, nj), pl.BlockSpec(memory_space=pl.ANY)],
        out_specs=[pl.BlockSpec(memory_space=pl.ANY), small[0], small[0], small[1], small[1]],
        out_shape=[jax.ShapeDtypeStruct((s, 2 * f), BF16), jax.ShapeDtypeStruct((3, f), F32),
                   jax.ShapeDtypeStruct((3, f), F32), jax.ShapeDtypeStruct((1, f), F32),
                   jax.ShapeDtypeStruct((1, f), F32)],
        scratch_shapes=[pltpu.VMEM((2, 2, tm, tn), BF16), pltpu.SemaphoreType.DMA((2, 2)),
                        pltpu.VMEM((2, FUSE_HALO, tn), F32), pltpu.VMEM((2, 4, FUSE_HALO, tn), F32)],
        compiler_params=_params(("arbitrary", "arbitrary")),
    )(dxb, w_down, up[0], up[1], cv[0], cv[1], cw, cw, after)


_GELU_C = math.sqrt(2.0 / math.pi)


def _gelu(x):
    th = jnp.tanh(_GELU_C * (x + 0.044715 * (x * x * x)))
    return x * (0.5 * (1.0 + th)), th


def _gelu_grad(x, th):
    return 0.5 * (1.0 + th) + 0.5 * x * (1.0 - th * th) * (_GELU_C * (1.0 + 3.0 * 0.044715 * (x * x)))


def _masked_ws(ws_ref, h):
    t = lax.broadcasted_iota(jnp.int32, (CHUNK, CHUNK), 0)
    sx = lax.broadcasted_iota(jnp.int32, (CHUNK, CHUNK), 1)
    return jnp.where(sx <= t, ws_ref[h], 0.0)


def _mixb_fwd(pre, gv, ws, bs_wide, name):
    s, w2 = pre.shape
    w = w2 // 2
    gw = w // SG_GROUPS

    def body(pre_ref, gv_ref, ws_ref, bs_ref, o_ref):
        zu, _ = _gelu(pre_ref[:, :w].astype(F32))
        zv, _ = _gelu(pre_ref[:, w:].astype(F32))
        _, vhat = _rms_stats(zv)
        vn = (vhat * gv_ref[...]).astype(BF16)
        for h in range(SG_GROUPS):
            cols = slice(h * gw, (h + 1) * gw)
            wsm = _masked_ws(ws_ref, h).astype(BF16)
            gate = jnp.dot(wsm, vn[:, cols], preferred_element_type=F32)
            gate = gate + jnp.tile(bs_ref[h], (1, gw // LANES))
            o_ref[:, cols] = (zu[:, cols] * gate).astype(BF16)

    return pl.pallas_call(
        body, name=name, grid=(s // CHUNK,),
        in_specs=[pl.BlockSpec((CHUNK, w2), lambda i: (i, 0)), pl.BlockSpec((1, w), lambda i: (0, 0)),
                  pl.BlockSpec((SG_GROUPS, CHUNK, CHUNK), lambda i: (0, 0, 0)),
                  pl.BlockSpec((SG_GROUPS, CHUNK, LANES), lambda i: (0, 0, 0))],
        out_specs=pl.BlockSpec((CHUNK, w), lambda i: (i, 0)),
        out_shape=jax.ShapeDtypeStruct((s, w), BF16),
        compiler_params=_params(("parallel",)),
    )(pre, gv, ws, bs_wide)


def _mixb_bwd(pre, dug, gv, ws, bs_wide, name):
    s, w2 = pre.shape
    w = w2 // 2
    gw = w // SG_GROUPS

    def body(pre_ref, dug_ref, gv_ref, ws_ref, bs_ref, o_ref, dws_ref, dbs_ref, dgv_ref, dvn_ref):
        first = pl.program_id(0) == 0

        @pl.when(first)
        def _():
            dws_ref[...] = jnp.zeros_like(dws_ref)
            dbs_ref[...] = jnp.zeros_like(dbs_ref)

        pu = pre_ref[:, :w].astype(F32)
        pv = pre_ref[:, w:].astype(F32)
        zu, thu = _gelu(pu)
        zv, thv = _gelu(pv)
        inv, vhat = _rms_stats(zv)
        gvv = gv_ref[...]
        vn = (vhat * gvv).astype(BF16)
        for h in range(SG_GROUPS):
            cols = slice(h * gw, (h + 1) * gw)
            wsm = _masked_ws(ws_ref, h).astype(BF16)
            gate = jnp.dot(wsm, vn[:, cols], preferred_element_type=F32)
            gate = gate + jnp.tile(bs_ref[h], (1, gw // LANES))
            dug_h = dug_ref[:, cols].astype(F32)
            dgate = dug_h * zu[:, cols]
            dgate_b = dgate.astype(BF16)
            o_ref[:, cols] = (dug_h * gate * _gelu_grad(pu[:, cols], thu[:, cols])).astype(BF16)
            dbs_ref[h] += jnp.broadcast_to(jnp.sum(dgate, axis=-1, keepdims=True), (CHUNK, LANES))
            dws = lax.dot_general(dgate_b, vn[:, cols], _DIMS["nt"], preferred_element_type=F32)
            t = lax.broadcasted_iota(jnp.int32, (CHUNK, CHUNK), 0)
            sx = lax.broadcasted_iota(jnp.int32, (CHUNK, CHUNK), 1)
            dws_ref[h] += jnp.where(sx <= t, dws, 0.0)
            dvn_ref[:, cols] = lax.dot_general(wsm, dgate_b, _DIMS["tn"], preferred_element_type=F32)
        dvn = dvn_ref[...]
        part = jnp.sum(dvn * vhat, axis=0, keepdims=True)

        @pl.when(first)
        def _():
            dgv_ref[...] = part

        @pl.when(jnp.logical_not(first))
        def _():
            dgv_ref[...] += part

        dvhat = dvn * gvv
        dzv = inv * (dvhat - vhat * jnp.mean(dvhat * vhat, axis=-1, keepdims=True))
        o_ref[:, w:] = (dzv * _gelu_grad(pv, thv)).astype(BF16)

    return pl.pallas_call(
        body, name=name, grid=(s // CHUNK,),
        in_specs=[pl.BlockSpec((CHUNK, w2), lambda i: (i, 0)), pl.BlockSpec((CHUNK, w), lambda i: (i, 0)),
                  pl.BlockSpec((1, w), lambda i: (0, 0)),
                  pl.BlockSpec((SG_GROUPS, CHUNK, CHUNK), lambda i: (0, 0, 0)),
                  pl.BlockSpec((SG_GROUPS, CHUNK, LANES), lambda i: (0, 0, 0))],
        out_specs=[pl.BlockSpec((CHUNK, w2), lambda i: (i, 0)),
                   pl.BlockSpec((SG_GROUPS, CHUNK, CHUNK), lambda i: (0, 0, 0)),
                   pl.BlockSpec((SG_GROUPS, CHUNK, LANES), lambda i: (0, 0, 0)),
                   pl.BlockSpec((1, w), lambda i: (0, 0))],
        out_shape=[jax.ShapeDtypeStruct((s, w2), BF16), jax.ShapeDtypeStruct((SG_GROUPS, CHUNK, CHUNK), F32),
                   jax.ShapeDtypeStruct((SG_GROUPS, CHUNK, LANES), F32), jax.ShapeDtypeStruct((1, w), F32)],
        scratch_shapes=[pltpu.VMEM((CHUNK, w), F32)],
        compiler_params=_params(("arbitrary",)),
    )(pre, dug, gv, ws, bs_wide)


def _cast_layer(w3, layer, name):
    _, r, c = w3.shape
    tr = _pick(r, 256, SLAB)

    def body(w_ref, o_ref):
        o_ref[...] = w_ref[...].astype(BF16)

    return pl.pallas_call(
        body, name=name, grid=(r // tr,),
        in_specs=[pl.BlockSpec((None, tr, c), lambda i: (layer, i, 0))],
        out_specs=pl.BlockSpec((tr, c), lambda i: (i, 0)),
        out_shape=jax.ShapeDtypeStruct((r, c), BF16),
        compiler_params=_params(("parallel",)),
    )(w3)


def _adamw_math(w, g, m, v):
    m = ADAM_B1 * m + (1.0 - ADAM_B1) * g
    v = ADAM_B2 * v + (1.0 - ADAM_B2) * (g * g)
    m_hat = m / (1.0 - ADAM_B1 ** ADAM_STEP)
    v_hat = v / (1.0 - ADAM_B2 ** ADAM_STEP)
    delta = -ADAM_LR * (m_hat / (jnp.sqrt(v_hat) + ADAM_EPS) + ADAM_WD * w)
    return delta, m, v


def _adamw_sharded(recvs, w, m, v, name):
    nl, r, c = w.shape
    tc = _pick(c, 1536, LANES)
    tr = _pick(r, 64, SLAB)

    def body(*refs):
        recv_refs = refs[:nl]
        w_ref, m_ref, v_ref, g_ref, d_ref, nm_ref, nv_ref = refs[nl:]
        for layer, recv_ref in enumerate(recv_refs):
            @pl.when(pl.program_id(0) == layer)
            def _():
                g = recv_ref[0].astype(F32)
                for q in range(1, N_DEV):
                    g = g + recv_ref[q].astype(F32)
                delta, nm, nv = _adamw_math(w_ref[...], g, m_ref[...], v_ref[...])
                g_ref[...] = g
                d_ref[...] = delta
                nm_ref[...] = nm
                nv_ref[...] = nv

    def recv_spec(layer):
        return pl.BlockSpec((N_DEV, tr, tc),
                            lambda l, i, j: (0, jnp.where(l == layer, i, 0), jnp.where(l == layer, j, 0)))

    blk = pl.BlockSpec((None, tr, tc), lambda l, i, j: (l, i, j))
    out = jax.ShapeDtypeStruct((nl, r, c), F32)
    return pl.pallas_call(
        body, name=name, grid=(nl, r // tr, c // tc),
        in_specs=[recv_spec(layer) for layer in range(nl)] + [blk, blk, blk],
        out_specs=[blk] * 4, out_shape=[out] * 4,
        compiler_params=_params(("parallel",) * 3),
    )(*recvs, w, m, v)


def _adamw_packed(w, g, m, v, name):
    r, c = w.shape
    tr = _pick(r, 256, 8)

    def body(w_ref, g_ref, m_ref, v_ref, d_ref, nm_ref, nv_ref):
        delta, nm, nv = _adamw_math(w_ref[...], g_ref[...], m_ref[...], v_ref[...])
        d_ref[...] = delta
        nm_ref[...] = nm
        nv_ref[...] = nv

    blk = pl.BlockSpec((tr, c), lambda i: (i, 0))
    out = jax.ShapeDtypeStruct((r, c), F32)
    return pl.pallas_call(
        body, name=name, grid=(r // tr,), in_specs=[blk] * 4, out_specs=[blk] * 3, out_shape=[out] * 3,
        compiler_params=_params(("parallel",)),
    )(w, g, m, v)


def _pack(arrays):
    parts = []
    for a in arrays:
        flat = a.reshape(-1).astype(F32)
        pad = (-flat.shape[0]) % PACK_GRANULE
        parts.append(jnp.pad(flat, (0, pad)) if pad else flat)
    return jnp.concatenate(parts).reshape(-1, LANES)


def _unpack(buf, shapes):
    flat = buf.reshape(-1)
    out, off = [], 0
    for shp in shapes:
        n = math.prod(shp)
        out.append(flat[off:off + n].reshape(shp))
        off += n + (-n) % PACK_GRANULE
    return out


def _mesh_pos():
    return lax.axis_index("x"), lax.axis_index("y"), lax.axis_index("c")


def _coords(q):
    return q // 4, (q // 2) % 2, q % 2


def _shard_of(ref, q, shard_shape, axis):
    r, c = shard_shape
    if axis == 0:
        return ref.at[pl.ds(pl.multiple_of(q * r, SLAB), r), :]
    return ref.at[:, pl.ds(pl.multiple_of(q * c, LANES), c)]


_HBM = pl.BlockSpec(memory_space=pltpu.HBM)
_SEM = pl.BlockSpec(memory_space=pltpu.SEMAPHORE)
_EFFECT = pltpu.SideEffectType.DATAFLOW_SIDE_EFFECTING


def _exchange_shapes(gather, src_shape, axis):
    r, c = src_shape
    if gather:
        return (r, c), ((r * N_DEV, c) if axis == 0 else (r, c * N_DEV))
    shard = (r // N_DEV, c) if axis == 0 else (r, c // N_DEV)
    return shard, (N_DEV,) + shard


def _exchange_copies(gather, src, land, sems, axis):
    send_sems, recv_sems, own_sem = sems
    x, y, c_ = _mesh_pos()
    me = 4 * x + 2 * y + c_
    shard, _ = _exchange_shapes(gather, src.shape, axis)

    def piece(q):
        return src if gather else _shard_of(src, q, shard, axis)

    def place(q):
        return _shard_of(land, q, shard, axis) if gather else land.at[q]

    own = pltpu.make_async_copy(piece(me), place(me), own_sem.at[0])
    sends, arrivals = [], []
    for step in range(1, N_DEV):
        to = (me + step) % N_DEV
        frm = (me + N_DEV - step) % N_DEV
        sends.append(pltpu.make_async_remote_copy(
            src_ref=piece(to), dst_ref=place(me), send_sem=send_sems.at[step - 1], recv_sem=recv_sems.at[step - 1],
            device_id=_coords(to), device_id_type=MESH))
        arrivals.append(pltpu.make_async_remote_copy(
            src_ref=piece(me), dst_ref=place(frm), send_sem=send_sems.at[step - 1], recv_sem=recv_sems.at[step - 1],
            device_id=_coords(frm), device_id_type=MESH))
    return own, sends, arrivals


def _exchange_start(gather, src, axis, name, after=None):
    _, land_shape = _exchange_shapes(gather, src.shape, axis)
    extra = () if after is None else (after,)

    def body(*refs):
        src_ref, land = refs[:2]
        send_sems, recv_sems, own_sem = refs[2 + len(extra):5 + len(extra)]
        own, sends, _ = _exchange_copies(gather, src_ref, land, (send_sems, recv_sems, own_sem), axis)
        own.start()
        for cp in sends:
            cp.start()
        refs[-1][...] = jnp.zeros_like(refs[-1])

    out = pl.pallas_call(
        body, name=name,
        out_shape=(pltpu.SemaphoreType.DMA((N_DEV - 1,)), pltpu.SemaphoreType.DMA((N_DEV - 1,)),
                   pltpu.SemaphoreType.DMA((1,)), pltpu.HBM(src.shape, src.dtype),
                   pltpu.HBM(land_shape, src.dtype), jax.ShapeDtypeStruct((8, LANES), F32)),
        in_specs=[_HBM, _HBM] + [pl.BlockSpec(memory_space=pl.ANY)] * len(extra),
        out_specs=(_SEM, _SEM, _SEM, _HBM, _HBM, pl.BlockSpec(memory_space=pltpu.VMEM)),
        input_output_aliases={0: 3, 1: 4},
        compiler_params=pltpu.CompilerParams(has_side_effects=_EFFECT),
    )(pltpu.with_memory_space_constraint(src, pltpu.HBM),
      pltpu.with_memory_space_constraint(lax.empty(land_shape, src.dtype), pltpu.HBM), *extra)
    return out[:5], out[5]


def _exchange_wait(gather, state, axis, after, name):
    send_sems, recv_sems, own_sem, src_thru, land_thru = state
    after = tuple(after) if isinstance(after, (tuple, list)) else (after,)

    def body(src, land, send_sems, recv_sems, own_sem, *rest):
        own, sends, arrivals = _exchange_copies(gather, src, land, (send_sems, recv_sems, own_sem), axis)
        for cp in sends:
            cp.wait_send()
        for cp in arrivals:
            cp.wait_recv()
        own.wait()

    return pl.pallas_call(
        body, name=name,
        out_shape=(pltpu.HBM(src_thru.shape, src_thru.dtype), pltpu.HBM(land_thru.shape, land_thru.dtype)),
        in_specs=[_HBM, _HBM, _SEM, _SEM, _SEM] + [pl.BlockSpec(memory_space=pl.ANY)] * len(after),
        out_specs=(_HBM, _HBM),
        input_output_aliases={0: 0, 1: 1},
        compiler_params=pltpu.CompilerParams(has_side_effects=_EFFECT),
    )(src_thru, land_thru, send_sems, recv_sems, own_sem, *after)[1]


def _gather2_copies(shard_ref, land, sems, axis, shard_shape):
    send1, recv1, own_sem, send2, recv2 = sems
    x, y, c = _mesh_pos()
    me, sibling = (x, y, c), (x, y, 1 - c)
    chips = [(1 - x, y), (x, 1 - y), (1 - x, 1 - y)]

    def region(dev):
        px, py, pc = dev
        return _shard_of(land, 4 * px + 2 * py + pc, shard_shape, axis)

    def copy(src, block, to, send, recv):
        return pltpu.make_async_remote_copy(src_ref=src, dst_ref=region(block), send_sem=send, recv_sem=recv,
                                            device_id=to, device_id_type=MESH)

    own = pltpu.make_async_copy(shard_ref, region(me), own_sem.at[0])
    peers = [sibling] + [(*chip, c) for chip in chips]
    sends1 = [copy(shard_ref, me, to, send1.at[k], recv1.at[k]) for k, to in enumerate(peers)]
    arrivals1 = [copy(shard_ref, frm, frm, send1.at[k], recv1.at[k]) for k, frm in enumerate(peers)]
    sends2, arrivals2 = [], []
    if send2 is not None:
        for j, chip in enumerate(chips):
            sends2.append(copy(region((*chip, c)), (*chip, c), sibling, send2.at[j], recv2.at[j]))
            arrivals2.append(copy(region((*chip, 1 - c)), (*chip, 1 - c), sibling, send2.at[j], recv2.at[j]))
    return own, sends1, arrivals1, sends2, arrivals2


def _gather2_start(shard, axis, name, after=None):
    _, land_shape = _exchange_shapes(True, shard.shape, axis)
    extra = () if after is None else (after,)

    def body(*refs):
        src_ref, land = refs[:2]
        send1, recv1, own_sem = refs[2 + len(extra):5 + len(extra)]
        own, sends1, _, _, _ = _gather2_copies(src_ref, land, (send1, recv1, own_sem, None, None), axis, shard.shape)
        own.start()
        for cp in sends1[1:] + sends1[:1]:
            cp.start()
        refs[-1][...] = jnp.zeros_like(refs[-1])

    out = pl.pallas_call(
        body, name=name,
        out_shape=(pltpu.SemaphoreType.DMA((4,)), pltpu.SemaphoreType.DMA((4,)), pltpu.SemaphoreType.DMA((1,)),
                   pltpu.HBM(shard.shape, shard.dtype), pltpu.HBM(land_shape, shard.dtype),
                   jax.ShapeDtypeStruct((8, LANES), F32)),
        in_specs=[_HBM, _HBM] + [pl.BlockSpec(memory_space=pl.ANY)] * len(extra),
        out_specs=(_SEM, _SEM, _SEM, _HBM, _HBM, pl.BlockSpec(memory_space=pltpu.VMEM)),
        input_output_aliases={0: 3, 1: 4},
        compiler_params=pltpu.CompilerParams(has_side_effects=_EFFECT),
    )(pltpu.with_memory_space_constraint(shard, pltpu.HBM),
      pltpu.with_memory_space_constraint(lax.empty(land_shape, shard.dtype), pltpu.HBM), *extra)
    return out[:5], out[5]


def _gather2_pass(state, axis, after, name):
    send1, recv1, own_sem, shard_thru, land_thru = state

    def body(src_ref, land, send1, recv1, own_sem, after_ref, send2, recv2, src_out, land_out, token):
        _, _, arrivals1, sends2, _ = _gather2_copies(src_ref, land, (send1, recv1, own_sem, send2, recv2), axis,
                                                     shard_thru.shape)
        for arrival, fwd in zip(arrivals1[1:], sends2):
            arrival.wait_recv()
            fwd.start()
        token[...] = jnp.zeros_like(token)

    out = pl.pallas_call(
        body, name=name,
        out_shape=(pltpu.SemaphoreType.DMA((3,)), pltpu.SemaphoreType.DMA((3,)),
                   pltpu.HBM(shard_thru.shape, shard_thru.dtype), pltpu.HBM(land_thru.shape, land_thru.dtype),
                   jax.ShapeDtypeStruct((8, LANES), F32)),
        in_specs=[_HBM, _HBM, _SEM, _SEM, _SEM, pl.BlockSpec(memory_space=pl.ANY)],
        out_specs=(_SEM, _SEM, _HBM, _HBM, pl.BlockSpec(memory_space=pltpu.VMEM)),
        input_output_aliases={0: 2, 1: 3},
        compiler_params=pltpu.CompilerParams(has_side_effects=_EFFECT),
    )(shard_thru, land_thru, send1, recv1, own_sem, after)
    return (send1, recv1, own_sem, out[0], out[1], out[2], out[3]), out[4]


def _gather2_wait(state, axis, after, name):
    send1, recv1, own_sem, send2, recv2, shard_thru, land_thru = state

    def body(src_ref, land, send1, recv1, own_sem, send2, recv2, after_ref, src_dead, got):
        own, sends1, arrivals1, sends2, arrivals2 = _gather2_copies(
            src_ref, land, (send1, recv1, own_sem, send2, recv2), axis, shard_thru.shape)
        for cp in sends1 + sends2:
            cp.wait_send()
        for cp in arrivals1[:1] + arrivals2:
            cp.wait_recv()
        own.wait()

    return pl.pallas_call(
        body, name=name,
        out_shape=(pltpu.HBM(shard_thru.shape, shard_thru.dtype), pltpu.HBM(land_thru.shape, land_thru.dtype)),
        in_specs=[_HBM, _HBM] + [_SEM] * 5 + [pl.BlockSpec(memory_space=pl.ANY)],
        out_specs=(_HBM, _HBM),
        input_output_aliases={0: 0, 1: 1},
        compiler_params=pltpu.CompilerParams(has_side_effects=_EFFECT),
    )(shard_thru, land_thru, send1, recv1, own_sem, send2, recv2, after)[1]


def _sum_slots(slots, name):
    _, r, c = slots.shape
    tr = _pick(r, 512, 8)

    def body(s_ref, o_ref):
        total = s_ref[0]
        for q in range(1, N_DEV):
            total = total + s_ref[q]
        o_ref[...] = total

    return pl.pallas_call(
        body, name=name, grid=(r // tr,),
        in_specs=[pl.BlockSpec((N_DEV, tr, c), lambda i: (0, i, 0))],
        out_specs=pl.BlockSpec((tr, c), lambda i: (i, 0)),
        out_shape=jax.ShapeDtypeStruct((r, c), F32),
        compiler_params=_params(("parallel",)),
    )(slots)


def kernel(x, a_norm, a_in, a_conv, a_out, b_norm, b_in, b_vnorm, b_ws, b_bs, b_out, f_norm, f_up, f_conv_w, f_conv_b, f_down, final_norm, loss_target, m_a_norm, m_a_in, m_a_conv, m_a_out, m_b_norm, m_b_in, m_b_vnorm, m_b_ws, m_b_bs, m_b_out, m_f_norm, m_f_up, m_f_conv_w, m_f_conv_b, m_f_down, m_final_norm, v_a_norm, v_a_in, v_a_conv, v_a_out, v_b_norm, v_b_in, v_b_vnorm, v_b_ws, v_b_bs, v_b_out, v_f_norm, v_f_up, v_f_conv_w, v_f_conv_b, v_f_down, v_final_norm):
    s, d = x.shape[1], x.shape[2]
    n_ffn = f_up.shape[0]
    f2 = f_up.shape[2] * N_DEV
    me = 4 * lax.axis_index("x") + 2 * lax.axis_index("y") + lax.axis_index("c")
    x0 = x.reshape(s, d)
    target = loss_target.reshape(s, d)

    wanted = [("a_in", _cast_layer(a_in, 0, "cast_a_in"), 1),
              ("small", _pack([a_conv, b_norm, b_vnorm, f_conv_w]), 0),
              ("a_out", _cast_layer(a_out, 0, "cast_a_out"), 0),
              ("f_up0", _cast_layer(f_up, 0, "cast_f_up0"), 1), ("f_down0", _cast_layer(f_down, 0, "cast_f_down0"), 0),
              ("b_in", _cast_layer(b_in, 0, "cast_b_in"), 1), ("b_out", _cast_layer(b_out, 0, "cast_b_out"), 0),
              ("f_up1", _cast_layer(f_up, 1, "cast_f_up1"), 1), ("f_down1", _cast_layer(f_down, 1, "cast_f_down1"), 0)]
    coming, tok, h0 = {}, None, None
    for n_started, (key, shard, axis) in enumerate(wanted):
        if n_started == 2:
            tok = h0 = _rmsnorm_fwd(x0, a_norm, "mixa_norm", after=tok)
        state, tok = _gather2_start(shard, axis, f"ag_start_{key}", after=tok)
        coming[key] = (state, axis)

    def pass_on(keys, after):
        for key in keys:
            state, axis = coming[key]
            state, after = _gather2_pass(state, axis, after, f"ag_pass_{key}")
            coming[key] = (state, axis)
        return after

    def arrived(key, after):
        state, axis = coming[key]
        return _gather2_wait(state, axis, after, f"ag_wait_{key}")

    cshard = a_conv.shape[2]
    fshard = f_conv_w.shape[2]
    w_a_in = arrived("a_in", pass_on(["a_in", "small"], tok))
    small_full = arrived("small", w_a_in)
    small_rows = small_full.reshape(N_DEV, -1)
    per_dev = _unpack_rows(small_rows, [(3, cshard), (cshard,), (cshard,), (n_ffn, 3, fshard)])
    a_conv_full = per_dev[0].transpose(1, 0, 2).reshape(3, d)
    b_norm_full = per_dev[1].reshape(1, d)
    b_vnorm_full = per_dev[2].reshape(1, d)
    f_conv_w_full = per_dev[3].transpose(1, 2, 0, 3).reshape(n_ffn, 3, f2)
    bs_wide = jnp.broadcast_to(b_bs[0][:, :, None], (SG_GROUPS, CHUNK, LANES))
    ws = b_ws[0]

    w_f_up, w_f_down = {}, {}

    def ffn_forward(xin, l, pass_early, pass_late):
        h = _rmsnorm_fwd(xin, f_norm[l:l + 1], f"ffn{l}_norm")
        w_f_up[l] = arrived(f"f_up{l}", pass_on(pass_early, h))
        up_g, up_a, cv_g, cv_a, act = _ffn_up_fused(h, w_f_up[l], f_conv_w_full[l], f_conv_b[l:l + 1], f"ffn{l}_up")
        up, cv = (up_g, up_a), (cv_g, cv_a)
        w_f_down[l] = arrived(f"f_down{l}", pass_on(pass_late, act))
        xout = _matmul(act, w_f_down[l], "nn", F32, f"ffn{l}_down", resid=xin, tk_cap=1408)
        return xout, (h, up, act, cv)

    gb, gc, xs, cva, ya = _mixa_in_fused(h0, w_a_in, a_conv_full, "mixa_in")
    bcx = (gb, gc, xs)
    w_a_out = arrived("a_out", pass_on(["a_out", "f_up0"], ya))
    x1 = _matmul(ya, w_a_out, "nn", F32, "mixa_out", resid=x0)
    x2, saved0 = ffn_forward(x1, 0, ["f_down0"], ["b_in", "b_out", "f_up1", "f_down1"])
    h2 = _rmsnorm_fwd(x2, b_norm_full, "mixb_norm")
    w_b_in = arrived("b_in", h2)
    pre = _matmul(h2, w_b_in, "nn", BF16, "mixb_in")
    ug = _mixb_fwd(pre, b_vnorm_full, ws, bs_wide, "mixb_mid")
    w_b_out = arrived("b_out", ug)
    x3 = _matmul(ug, w_b_out, "nn", F32, "mixb_out", resid=x2)
    x4, saved1 = ffn_forward(x3, 1, [], [])
    dx4, dx4b, loss_part, g_final = _final_loss(x4, final_norm.reshape(1, d), target, "loss_head")

    def _rs_start(grad, axis, name):
        return _exchange_start(False, grad, axis, name)

    def ffn_backward(xin, l, saved, dx, dxb):
        h, up, act, cv = saved
        g_down = _matmul(act, dxb, "tn", BF16, f"ffn{l}_down_dw", tm_cap=1408)
        rs_down, tok = _rs_start(g_down, 0, f"rs_start_f_down{l}")
        dup, cwg, cwa, cbg, cba = _ffn_down_dx_fused(dxb, w_f_down[l], up, cv, f_conv_w_full[l], tok,
                                                     f"ffn{l}_down_dx")
        g_cw, g_cb = jnp.concatenate([cwg, cwa], axis=1), jnp.concatenate([cbg, cba], axis=1)
        g_up = _matmul(h, dup, "tn", BF16, f"ffn{l}_up_dw", tk_cap=4096)
        rs_up, tok = _rs_start(g_up, 1, f"rs_start_f_up{l}")
        dh = _matmul(dup, w_f_up[l], "nt", BF16, f"ffn{l}_up_dx", after=tok)
        dxin, dxinb, g_norm = _rmsnorm_bwd(xin, f_norm[l:l + 1], dh, dx, f"ffn{l}_norm_bwd")
        return dxin, dxinb, (rs_up, rs_down, g_cw, g_cb, g_norm)

    dx3, dx3b, gf1 = ffn_backward(x3, 1, saved1, dx4, dx4b)
    g_b_out = _matmul(ug, dx3b, "tn", BF16, "mixb_out_dw", tk_cap=4096)
    rs_b_out, tok = _rs_start(g_b_out, 0, "rs_start_b_out")
    dug = _matmul(dx3b, w_b_out, "nt", BF16, "mixb_out_dx", after=tok)
    dpre, g_ws, g_bs_wide, g_bvnorm = _mixb_bwd(pre, dug, b_vnorm_full, ws, bs_wide, "mixb_mid_bwd")
    g_b_in = _matmul(h2, dpre, "tn", BF16, "mixb_in_dw", tk_cap=4096)
    rs_b_in, tok = _rs_start(g_b_in, 1, "rs_start_b_in")
    dh2 = _matmul(dpre, w_b_in, "nt", BF16, "mixb_in_dx", after=tok)
    dx2, dx2b, g_bnorm = _rmsnorm_bwd(x2, b_norm_full, dh2, dx3, "mixb_norm_bwd")
    dx1, dx1b, gf0 = ffn_backward(x1, 0, saved0, dx2, dx2b)
    g_a_out = _matmul(ya, dx1b, "tn", BF16, "mixa_out_dw", tk_cap=4096)
    rs_a_out, tok = _rs_start(g_a_out, 0, "rs_start_a_out")
    dya = _matmul(dx1b, w_a_out, "nt", BF16, "mixa_out_dx", after=tok)
    dbcx, g_aconv = _mixa_bwd(bcx, cva, dya, a_conv_full, "mixa_mid_bwd")
    g_a_in = _matmul(h0, dbcx, "tn", BF16, "mixa_in_dw", tk_cap=4096)
    rs_a_in, tok = _rs_start(g_a_in, 1, "rs_start_a_in")
    dh0 = _matmul(dbcx, w_a_in, "nt", BF16, "mixa_in_dx", after=tok)
    grad_x, _, g_anorm = _rmsnorm_bwd(x0, a_norm, dh0, dx1, "mixa_norm_bwd")

    full_shapes = [(1, LANES), (1, d), (3, d), (1, d), (1, d), (SG_GROUPS, CHUNK, CHUNK), (SG_GROUPS, CHUNK),
                   (n_ffn, d), (n_ffn, 3, f2), (n_ffn, f2), (1, d)]
    parts = [loss_part, g_anorm, g_aconv, g_bnorm, g_bvnorm, g_ws, g_bs_wide[:, :, 0],
             jnp.concatenate([gf0[4], gf1[4]], axis=0), jnp.stack([gf0[2], gf1[2]]),
             jnp.concatenate([gf0[3], gf1[3]], axis=0), g_final]
    small_part = _pack(parts)
    small_state, small_tok = _exchange_start(True, small_part, 0, "ar_start_small", after=grad_x)

    big = {}
    for name, states, axis, w, m, v in (
            ("f_down", (gf0[1], gf1[1]), 0, f_down, m_f_down, v_f_down),
            ("f_up", (gf0[0], gf1[0]), 1, f_up, m_f_up, v_f_up),
            ("b_out", (rs_b_out,), 0, b_out, m_b_out, v_b_out), ("b_in", (rs_b_in,), 1, b_in, m_b_in, v_b_in),
            ("a_out", (rs_a_out,), 0, a_out, m_a_out, v_a_out), ("a_in", (rs_a_in,), 1, a_in, m_a_in, v_a_in)):
        recvs = [_exchange_wait(False, st, axis, small_tok, f"rs_wait_{name}{l}") for l, st in enumerate(states)]
        big[name] = _adamw_sharded(recvs, w, m, v, f"adamw_{name}")

    slots = _exchange_wait(True, small_state, 0, [res[0] for res in big.values()], "ar_wait_small")
    total = _sum_slots(slots.reshape((N_DEV,) + small_part.shape), "ar_sum_small")
    (loss_v, r_anorm, r_aconv, r_bnorm, r_bvnorm, r_ws, r_bs, r_fnorm, r_fcw, r_fcb, r_final) = _unpack(total, full_shapes)
    small_grads = [
        r_anorm,
        lax.dynamic_slice_in_dim(r_aconv, me * cshard, cshard, axis=1).reshape(a_conv.shape),
        lax.dynamic_slice_in_dim(r_bnorm, me * cshard, cshard, axis=1),
        lax.dynamic_slice_in_dim(r_bvnorm, me * cshard, cshard, axis=1),
        r_ws.reshape(b_ws.shape), r_bs.reshape(b_bs.shape), r_fnorm,
        lax.dynamic_slice_in_dim(r_fcw, me * fshard, fshard, axis=2),
        r_fcb, r_final.reshape(final_norm.shape)]
    small_w = [a_norm, a_conv, b_norm, b_vnorm, b_ws, b_bs, f_norm, f_conv_w, f_conv_b, final_norm]
    small_m = [m_a_norm, m_a_conv, m_b_norm, m_b_vnorm, m_b_ws, m_b_bs, m_f_norm, m_f_conv_w, m_f_conv_b, m_final_norm]
    small_v = [v_a_norm, v_a_conv, v_b_norm, v_b_vnorm, v_b_ws, v_b_bs, v_f_norm, v_f_conv_w, v_f_conv_b, v_final_norm]
    shapes = [w.shape for w in small_w]
    packed = _adamw_packed(_pack(small_w), _pack(small_grads), _pack(small_m), _pack(small_v), "adamw_small")
    s_delta, s_m, s_v = (_unpack(p, shapes) for p in packed)
    small_names = ["a_norm", "a_conv", "b_norm", "b_vnorm", "b_ws", "b_bs", "f_norm", "f_conv_w", "f_conv_b", "final_norm"]
    small = {nm: (small_grads[i], s_delta[i], s_m[i], s_v[i]) for i, nm in enumerate(small_names)}

    order = ["a_norm", "a_in", "a_conv", "a_out", "b_norm", "b_in", "b_vnorm", "b_ws", "b_bs", "b_out",
             "f_norm", "f_up", "f_conv_w", "f_conv_b", "f_down", "final_norm"]
    res = {nm: (big[nm] if nm in big else small[nm]) for nm in order}
    outs = [loss_v[0, 0], grad_x.reshape(x.shape)]
    for k in range(4):
        outs += [res[nm][k] for nm in order]
    return tuple(outs)


def _unpack_rows(rows, shapes):
    out, off = [], 0
    for shp in shapes:
        n = math.prod(shp)
        out.append(rows[:, off:off + n].reshape((N_DEV,) + tuple(shp)))
        off += n + (-n) % PACK_GRANULE
    return out
```

```python
import functools
import math

import jax
import jax.numpy as jnp
from jax import lax
from jax.experimental import pallas as pl
from jax.experimental.pallas import tpu as pltpu

F32 = jnp.float32
BF16 = jnp.bfloat16
MESH = pl.DeviceIdType.MESH

N_DEV = 8
RMS_EPS = 1e-5
CHUNK = 128
SG_GROUPS = 8
ADAM_LR = 0.001
ADAM_B1 = 0.9
ADAM_B2 = 0.999
ADAM_EPS = 1e-08
ADAM_WD = 0.01
ADAM_STEP = 10

LANES = 128
SLAB = 16
FUSE_STRIP = 256
FUSE_HALO = 8
VMEM_LIMIT = 56 * 1024 * 1024
PACK_GRANULE = 8 * LANES


def _pick(dim, cap, mult):
    best = None
    t = mult
    while t <= min(dim, cap):
        if dim % t == 0:
            best = t
        t += mult
    return dim if best is None else best


def _params(semantics=None):
    return pltpu.CompilerParams(dimension_semantics=semantics, vmem_limit_bytes=VMEM_LIMIT)


_DIMS = {
    "nn": (((1,), (0,)), ((), ())),
    "nt": (((1,), (1,)), ((), ())),
    "tn": (((0,), (0,)), ((), ())),
}


def _matmul(a, b, mode, out_dtype, name, resid=None, after=None, tm_cap=1024, tn_cap=1024, tk_cap=2816):
    if mode == "nn":
        (m, k), n = a.shape, b.shape[1]
    elif mode == "nt":
        (m, k), n = a.shape, b.shape[0]
    else:
        (k, m), n = a.shape, b.shape[1]
    tm, tn, tk = _pick(m, tm_cap, LANES), _pick(n, tn_cap, LANES), _pick(k, tk_cap, LANES)
    nk = k // tk
    n_in = 2 + (resid is not None) + (after is not None)

    def body(*refs):
        a_ref, b_ref = refs[:2]
        r_ref = refs[2] if resid is not None else None
        o_ref = refs[n_in]
        prod = lax.dot_general(a_ref[...], b_ref[...], _DIMS[mode], preferred_element_type=F32)

        def finish(r):
            if r_ref is not None:
                r = r + r_ref[...]
            o_ref[...] = r.astype(out_dtype)

        if nk == 1:
            finish(prod)
            return
        acc_ref = refs[n_in + 1]
        kk = pl.program_id(2)

        @pl.when(kk == 0)
        def _():
            acc_ref[...] = prod

        @pl.when(jnp.logical_and(kk > 0, kk < nk - 1))
        def _():
            acc_ref[...] += prod

        @pl.when(kk == nk - 1)
        def _():
            finish(acc_ref[...] + prod)

    a_spec = (pl.BlockSpec((tk, tm), lambda i, j, kk: (kk, i)) if mode == "tn"
              else pl.BlockSpec((tm, tk), lambda i, j, kk: (i, kk)))
    b_spec = (pl.BlockSpec((tn, tk), lambda i, j, kk: (j, kk)) if mode == "nt"
              else pl.BlockSpec((tk, tn), lambda i, j, kk: (kk, j)))
    o_spec = pl.BlockSpec((tm, tn), lambda i, j, kk: (i, j))
    in_specs = [a_spec, b_spec] + ([o_spec] if resid is not None else [])
    args = (a, b) + ((resid,) if resid is not None else ())
    if after is not None:
        in_specs.append(pl.BlockSpec(memory_space=pl.ANY))
        args += (after,)
    return pl.pallas_call(
        body, name=name, grid=(m // tm, n // tn, nk),
        in_specs=in_specs, out_specs=o_spec,
        out_shape=jax.ShapeDtypeStruct((m, n), out_dtype),
        scratch_shapes=[pltpu.VMEM((tm, tn), F32)] if nk > 1 else [],
        compiler_params=_params(("parallel", "parallel", "arbitrary")),
    )(*args)


def _rms_stats(xf):
    inv = lax.rsqrt(jnp.mean(xf * xf, axis=-1, keepdims=True) + RMS_EPS)
    return inv, xf * inv


def _rmsnorm_fwd(x, g, name, after=None):
    s, d = x.shape
    tm = _pick(s, 256, SLAB)
    extra = () if after is None else (after,)

    def body(x_ref, g_ref, *rest):
        _, xhat = _rms_stats(x_ref[...])
        rest[-1][...] = (xhat * g_ref[...]).astype(BF16)

    return pl.pallas_call(
        body, name=name, grid=(s // tm,),
        in_specs=[pl.BlockSpec((tm, d), lambda i: (i, 0)), pl.BlockSpec((1, d), lambda i: (0, 0))]
        + [pl.BlockSpec(memory_space=pl.ANY)] * len(extra),
        out_specs=pl.BlockSpec((tm, d), lambda i: (i, 0)),
        out_shape=jax.ShapeDtypeStruct((s, d), BF16),
        compiler_params=_params(("parallel",)),
    )(x, g, *extra)


def _rmsnorm_bwd(x, g, dh, dx_out, name):
    s, d = x.shape
    tm = _pick(s, 256, SLAB)

    def body(x_ref, g_ref, dh_ref, dxo_ref, dxi_ref, dxib_ref, dg_ref):
        inv, xhat = _rms_stats(x_ref[...])
        dhv = dh_ref[...].astype(F32)
        dxhat = dhv * g_ref[...]
        proj = jnp.mean(dxhat * xhat, axis=-1, keepdims=True)
        dx = dxo_ref[...] + inv * (dxhat - xhat * proj)
        dxi_ref[...] = dx
        dxib_ref[...] = dx.astype(BF16)
        part = jnp.sum(dhv * xhat, axis=0, keepdims=True)

        @pl.when(pl.program_id(0) == 0)
        def _():
            dg_ref[...] = part

        @pl.when(pl.program_id(0) > 0)
        def _():
            dg_ref[...] += part

    row = pl.BlockSpec((tm, d), lambda i: (i, 0))
    vec = pl.BlockSpec((1, d), lambda i: (0, 0))
    return pl.pallas_call(
        body, name=name, grid=(s // tm,),
        in_specs=[row, vec, row, row], out_specs=[row, row, vec],
        out_shape=[jax.ShapeDtypeStruct((s, d), F32), jax.ShapeDtypeStruct((s, d), BF16),
                   jax.ShapeDtypeStruct((1, d), F32)],
        compiler_params=_params(("arbitrary",)),
    )(x, g, dh, dx_out)


def _final_loss(x, g, target, name):
    s, d = x.shape
    tm = _pick(s, 256, SLAB)

    def body(x_ref, g_ref, t_ref, dx_ref, dxb_ref, loss_ref, dg_ref):
        inv, xhat = _rms_stats(x_ref[...])
        gv = g_ref[...]
        err = xhat * gv - t_ref[...]
        loss = 0.5 * jnp.sum(jnp.mean(err * err, axis=-1, keepdims=True), axis=0, keepdims=True)
        dy = err * (1.0 / d)
        dxhat = dy * gv
        proj = jnp.mean(dxhat * xhat, axis=-1, keepdims=True)
        dx = inv * (dxhat - xhat * proj)
        dx_ref[...] = dx
        dxb_ref[...] = dx.astype(BF16)
        part = jnp.sum(dy * xhat, axis=0, keepdims=True)
        loss_row = jnp.broadcast_to(loss, (1, LANES))

        @pl.when(pl.program_id(0) == 0)
        def _():
            dg_ref[...] = part
            loss_ref[...] = loss_row

        @pl.when(pl.program_id(0) > 0)
        def _():
            dg_ref[...] += part
            loss_ref[...] += loss_row

    row = pl.BlockSpec((tm, d), lambda i: (i, 0))
    vec = pl.BlockSpec((1, d), lambda i: (0, 0))
    return pl.pallas_call(
        body, name=name, grid=(s // tm,),
        in_specs=[row, vec, row],
        out_specs=[row, row, pl.BlockSpec((1, LANES), lambda i: (0, 0)), vec],
        out_shape=[jax.ShapeDtypeStruct((s, d), F32), jax.ShapeDtypeStruct((s, d), BF16),
                   jax.ShapeDtypeStruct((1, LANES), F32), jax.ShapeDtypeStruct((1, d), F32)],
        compiler_params=_params(("arbitrary",)),
    )(x, g, target)


def _mixa_in_fused(h, w_in, wc, name):
    s, d = h.shape
    tm = _pick(s, 1024, LANES)
    tn = _pick(d, 512, FUSE_STRIP)
    nj = d // tn

    def body(h_ref, wb_ref, wg_ref, wx_ref, wc_ref, gb_ref, gc_ref, xs_ref, cv_ref, y_ref, carry):
        @pl.when(pl.program_id(1) == 0)
        def _():
            carry[...] = jnp.zeros_like(carry)

        def matmul(st):
            cols = slice(st * FUSE_STRIP, (st + 1) * FUSE_STRIP)
            return tuple(jnp.dot(h_ref[...], w_ref[:, cols], preferred_element_type=F32).astype(BF16)
                         for w_ref in (wb_ref, wg_ref, wx_ref))

        n_strips = tn // FUSE_STRIP
        parts = matmul(0)
        for st in range(n_strips):
            parts_next = matmul(st + 1) if st + 1 < n_strips else None
            cols = slice(st * FUSE_STRIP, (st + 1) * FUSE_STRIP)
            for ref, part in zip((gb_ref, gc_ref, xs_ref), parts):
                ref[:, cols] = part
            p = parts[1].astype(F32) * parts[2].astype(F32)
            ext = jnp.concatenate([carry[:, cols], p], axis=0)
            s1 = pltpu.roll(ext, 1, 0)[FUSE_HALO:, :]
            s2 = pltpu.roll(ext, 2, 0)[FUSE_HALO:, :]
            carry[:, cols] = p[tm - FUSE_HALO:, :]
            cv = wc_ref[0:1, cols] * s2 + wc_ref[1:2, cols] * s1 + wc_ref[2:3, cols] * p
            cv_ref[:, cols] = cv.astype(BF16)
            y_ref[:, cols] = (parts[0].astype(F32) * cv).astype(BF16)
            parts = parts_next

    def cols_of(rows, offset):
        return pl.BlockSpec((rows, tn), lambda j, i: (0, j + offset))

    tile = pl.BlockSpec((tm, tn), lambda j, i: (i, j))
    return pl.pallas_call(
        body, name=name, grid=(nj, s // tm),
        in_specs=[pl.BlockSpec((tm, d), lambda j, i: (i, 0)), cols_of(d, 0), cols_of(d, nj), cols_of(d, 2 * nj),
                  cols_of(3, 0)],
        out_specs=[tile] * 5, out_shape=[jax.ShapeDtypeStruct((s, d), BF16)] * 5,
        scratch_shapes=[pltpu.VMEM((FUSE_HALO, tn), F32)],
        compiler_params=_params(("parallel", "arbitrary")),
    )(h, w_in, w_in, w_in, wc)


def _mixa_out_dx_fused(dxb, w_out, bcx, cv, wc, after, name):
    s, d = dxb.shape
    tm = _pick(s, 1024, LANES)
    tn = _pick(d, 512, FUSE_STRIP)
    nj, ni = d // tn, s // tm
    n_steps = nj * ni
    sub = tm // FUSE_HALO

    def body(dx_ref, w_ref, gb_ref, gc_ref, xs_ref, cv_ref, wc_ref, after_ref,
             dbcx_hbm, dwc_ref, out_buf, out_sem, carry, acc):
        j, i = pl.program_id(0), pl.program_id(1)
        step = j * ni + i
        slot = lax.rem(step, 2)
        row0 = pl.multiple_of((ni - 1 - i) * tm, tm)

        def out_copy(part, slot_=None):
            slot_ = slot if slot_ is None else slot_
            col0 = pl.multiple_of(part * d + j * tn, LANES)
            return pltpu.make_async_copy(out_buf.at[slot_, part], dbcx_hbm.at[pl.ds(row0, tm), pl.ds(col0, tn)],
                                         out_sem.at[slot_, part])

        @pl.when(step >= 2)
        def _():
            for part in range(3):
                out_copy(part).wait()

        @pl.when(i == 0)
        def _():
            carry[...] = jnp.zeros_like(carry)
            acc[...] = jnp.zeros_like(acc)

        for st in range(tn // FUSE_STRIP):
            cols = slice(st * FUSE_STRIP, (st + 1) * FUSE_STRIP)
            dyv = lax.dot_general(dx_ref[...], w_ref[cols, :], _DIMS["nt"], preferred_element_type=F32)
            gc = gc_ref[:, cols].astype(F32)
            xs = xs_ref[:, cols].astype(F32)
            d0 = dyv * gb_ref[:, cols].astype(F32)
            ext = jnp.concatenate([d0, carry[:, cols]], axis=0)
            d1 = pltpu.roll(ext, tm + FUSE_HALO - 1, 0)[:tm, :]
            d2 = pltpu.roll(ext, tm + FUSE_HALO - 2, 0)[:tm, :]
            carry[:, cols] = d0[:FUSE_HALO, :]
            dp = wc_ref[2:3, cols] * d0 + wc_ref[1:2, cols] * d1 + wc_ref[0:1, cols] * d2
            out_buf[slot, 0, :, cols] = (dyv * cv_ref[:, cols].astype(F32)).astype(BF16)
            out_buf[slot, 1, :, cols] = (dp * xs).astype(BF16)
            out_buf[slot, 2, :, cols] = (dp * gc).astype(BF16)
            p = gc * xs
            for k, term in enumerate((d2 * p, d1 * p, d0 * p)):
                acc[k, :, cols] += jnp.sum(term.reshape(sub, FUSE_HALO, FUSE_STRIP), axis=0)

        for part in range(3):
            out_copy(part).start()

        @pl.when(i == ni - 1)
        def _():
            for k in range(3):
                dwc_ref[k:k + 1, :] = jnp.sum(acc[k], axis=0, keepdims=True)

        @pl.when(step == n_steps - 1)
        def _():
            for part in range(3):
                out_copy(part).wait()
                if n_steps > 1:
                    out_copy(part, 1 - slot).wait()

    tile = pl.BlockSpec((tm, tn), lambda j, i: (ni - 1 - i, j))
    return pl.pallas_call(
        body, name=name, grid=(nj, ni),
        in_specs=[pl.BlockSpec((tm, d), lambda j, i: (ni - 1 - i, 0)), pl.BlockSpec((tn, d), lambda j, i: (j, 0)),
                  tile, tile, tile, tile, pl.BlockSpec((3, tn), lambda j, i: (0, j)),
                  pl.BlockSpec(memory_space=pl.ANY)],
        out_specs=[pl.BlockSpec(memory_space=pl.ANY), pl.BlockSpec((3, tn), lambda j, i: (0, j))],
        out_shape=[jax.ShapeDtypeStruct((s, 3 * d), BF16), jax.ShapeDtypeStruct((3, d), F32)],
        scratch_shapes=[pltpu.VMEM((2, 3, tm, tn), BF16), pltpu.SemaphoreType.DMA((2, 3)),
                        pltpu.VMEM((FUSE_HALO, tn), F32), pltpu.VMEM((3, FUSE_HALO, tn), F32)],
        compiler_params=_params(("arbitrary", "arbitrary")),
    )(dxb, w_out, bcx[0], bcx[1], bcx[2], cv, wc, after)


def _sigmoid(z):
    return 0.5 * jnp.tanh(0.5 * z) + 0.5


def _ffn_up_fused(h, w_up, cw, cb, name):
    s, d = h.shape
    f = w_up.shape[1] // 2
    tm = _pick(s, 1024, LANES)
    tr = tm
    tn = _pick(f, 512, FUSE_STRIP)
    nj = f // tn

    def body(h_ref, wg_ref, wa_ref, cwg_ref, cwa_ref, cbg_ref, cba_ref,
             upg_ref, upa_ref, cvg_ref, cva_ref, act_ref, carry_g, carry_a):
        @pl.when(pl.program_id(1) == 0)
        def _():
            carry_g[...] = jnp.zeros_like(carry_g)
            carry_a[...] = jnp.zeros_like(carry_a)

        units = [(slice(st * FUSE_STRIP, (st + 1) * FUSE_STRIP), slice(rp * tr, (rp + 1) * tr))
                 for st in range(tn // FUSE_STRIP) for rp in range(tm // tr)]

        def matmul(unit):
            cols, rows = unit
            return tuple(jnp.dot(h_ref[rows, :], w_ref[:, cols], preferred_element_type=F32).astype(BF16)
                         for w_ref in (wg_ref, wa_ref))

        def conv(up, cw_ref, cb_ref, carry, up_ref, cv_ref, unit):
            cols, rows = unit
            up_ref[rows, cols] = up
            x = up.astype(F32)
            ext = jnp.concatenate([carry[:, cols], x], axis=0)
            s1 = pltpu.roll(ext, 1, 0)[FUSE_HALO:, :]
            s2 = pltpu.roll(ext, 2, 0)[FUSE_HALO:, :]
            carry[:, cols] = x[tr - FUSE_HALO:, :]
            cv = cw_ref[0:1, cols] * s2 + cw_ref[1:2, cols] * s1 + cw_ref[2:3, cols] * x + cb_ref[:, cols]
            cv_ref[rows, cols] = cv.astype(BF16)
            return cv

        ups = matmul(units[0])
        for n, unit in enumerate(units):
            ups_next = matmul(units[n + 1]) if n + 1 < len(units) else None
            gcv = conv(ups[0], cwg_ref, cbg_ref, carry_g, upg_ref, cvg_ref, unit)
            acv = conv(ups[1], cwa_ref, cba_ref, carry_a, upa_ref, cva_ref, unit)
            act_ref[unit[1], unit[0]] = (gcv * _sigmoid(gcv) * acv).astype(BF16)
            ups = ups_next

    def cols_of(rows, offset):
        return pl.BlockSpec((rows, tn), lambda j, i: (0, j + offset))

    tile = pl.BlockSpec((tm, tn), lambda j, i: (i, j))
    out = jax.ShapeDtypeStruct((s, f), BF16)
    return pl.pallas_call(
        body, name=name, grid=(nj, s // tm),
        in_specs=[pl.BlockSpec((tm, d), lambda j, i: (i, 0)), cols_of(d, 0), cols_of(d, nj),
                  cols_of(3, 0), cols_of(3, nj), cols_of(1, 0), cols_of(1, nj)],
        out_specs=[tile] * 5, out_shape=[out] * 5,
        scratch_shapes=[pltpu.VMEM((FUSE_HALO, tn), F32)] * 2,
        compiler_params=_params(("parallel", "arbitrary")),
    )(h, w_up, w_up, cw, cw, cb, cb)


def _ffn_down_dx_fused(dxb, w_down, up, cv, cw, after, name):
    s, d = dxb.shape
    f = w_down.shape[0]
    tm = _pick(s, 1024, LANES)
    tn = _pick(f, 512, FUSE_STRIP)
    nj, ni = f // tn, s // tm
    n_steps = nj * ni
    sub = tm // FUSE_HALO

    def body(dx_ref, w_ref, upg_ref, upa_ref, cvg_ref, cva_ref, cwg_ref, cwa_ref, after_ref,
             dup_hbm, dcwg_ref, dcwa_ref, dcbg_ref, dcba_ref, out_buf, out_sem, carry, acc):
        j, i = pl.program_id(0), pl.program_id(1)
        step = j * ni + i
        slot = lax.rem(step, 2)
        row0 = pl.multiple_of((ni - 1 - i) * tm, tm)

        def out_copy(half):
            col0 = pl.multiple_of(half * f + j * tn, LANES)
            return pltpu.make_async_copy(out_buf.at[slot, half], dup_hbm.at[pl.ds(row0, tm), pl.ds(col0, tn)],
                                         out_sem.at[slot, half])

        @pl.when(step >= 2)
        def _():
            for half in range(2):
                out_copy(half).wait()

        @pl.when(i == 0)
        def _():
            carry[...] = jnp.zeros_like(carry)
            acc[...] = jnp.zeros_like(acc)

        for st in range(tn // FUSE_STRIP):
            cols = slice(st * FUSE_STRIP, (st + 1) * FUSE_STRIP)
            dact = lax.dot_general(dx_ref[...], w_ref[cols, :], _DIMS["nt"], preferred_element_type=F32)
            gcv = cvg_ref[:, cols].astype(F32)
            acv = cva_ref[:, cols].astype(F32)
            sg = _sigmoid(gcv)
            dd = (dact * acv * (sg * (1.0 + gcv * (1.0 - sg))), dact * (gcv * sg))
            for half, (up_ref, cw_ref) in enumerate(((upg_ref, cwg_ref), (upa_ref, cwa_ref))):
                x = up_ref[:, cols].astype(F32)
                d0 = dd[half]
                ext = jnp.concatenate([d0, carry[half, :, cols]], axis=0)
                d1 = pltpu.roll(ext, tm + FUSE_HALO - 1, 0)[:tm, :]
                d2 = pltpu.roll(ext, tm + FUSE_HALO - 2, 0)[:tm, :]
                carry[half, :, cols] = d0[:FUSE_HALO, :]
                out_buf[slot, half, :, cols] = (cw_ref[2:3, cols] * d0 + cw_ref[1:2, cols] * d1
                                                + cw_ref[0:1, cols] * d2).astype(BF16)
                for k, term in enumerate((d2 * x, d1 * x, d0 * x, d0)):
                    acc[half, k, :, cols] += jnp.sum(term.reshape(sub, FUSE_HALO, FUSE_STRIP), axis=0)

        for half in range(2):
            out_copy(half).start()

        @pl.when(i == ni - 1)
        def _():
            for half, (dcw_ref, dcb_ref) in enumerate(((dcwg_ref, dcbg_ref), (dcwa_ref, dcba_ref))):
                for k in range(3):
                    dcw_ref[k:k + 1, :] = jnp.sum(acc[half, k], axis=0, keepdims=True)
                dcb_ref[...] = jnp.sum(acc[half, 3], axis=0, keepdims=True)

        @pl.when(step == n_steps - 1)
        def _():
            for half in range(2):
                out_copy(half).wait()
                if n_steps > 1:
                    pltpu.make_async_copy(out_buf.at[1 - slot, half], dup_hbm.at[pl.ds(row0, tm), pl.ds(0, tn)],
                                          out_sem.at[1 - slot, half]).wait()

    tile = pl.BlockSpec((tm, tn), lambda j, i: (ni - 1 - i, j))

    def cols_of(rows, offset):
        return pl.BlockSpec((rows, tn), lambda j, i: (0, j + offset))

    small = pl.BlockSpec((3, tn), lambda j, i: (0, j)), pl.BlockSpec((1, tn), lambda j, i: (0, j))
    return pl.pallas_call(
        body, name=name, grid=(nj, ni),
        in_specs=[pl.BlockSpec((tm, d), lambda j, i: (ni - 1 - i, 0)), pl.BlockSpec((tn, d), lambda j, i: (j, 0)),
                  tile, tile, tile, tile, cols_of(3, 0), cols_of(3, nj), pl.BlockSpec(memory_space=pl.ANY)],
        out_specs=[pl.BlockSpec(memory_space=pl.ANY), small[0], small[0], small[1], small[1]],
        out_shape=[jax.ShapeDtypeStruct((s, 2 * f), BF16), jax.ShapeDtypeStruct((3, f), F32),
                   jax.ShapeDtypeStruct((3, f), F32), jax.ShapeDtypeStruct((1, f), F32),
                   jax.ShapeDtypeStruct((1, f), F32)],
        scratch_shapes=[pltpu.VMEM((2, 2, tm, tn), BF16), pltpu.SemaphoreType.DMA((2, 2)),
                        pltpu.VMEM((2, FUSE_HALO, tn), F32), pltpu.VMEM((2, 4, FUSE_HALO, tn), F32)],
        compiler_params=_params(("arbitrary", "arbitrary")),
    )(dxb, w_down, up[0], up[1], cv[0], cv[1], cw, cw, after)


_GELU_C = math.sqrt(2.0 / math.pi)


def _gelu(x):
    th = jnp.tanh(_GELU_C * (x + 0.044715 * (x * x * x)))
    return x * (0.5 * (1.0 + th)), th


def _gelu_grad(x, th):
    return 0.5 * (1.0 + th) + 0.5 * x * (1.0 - th * th) * (_GELU_C * (1.0 + 3.0 * 0.044715 * (x * x)))


def _masked_ws(ws_ref, h):
    t = lax.broadcasted_iota(jnp.int32, (CHUNK, CHUNK), 0)
    sx = lax.broadcasted_iota(jnp.int32, (CHUNK, CHUNK), 1)
    return jnp.where(sx <= t, ws_ref[h], 0.0)


def _mixb_fwd(pre, gv, ws, bs_wide, name):
    s, w2 = pre.shape
    w = w2 // 2
    gw = w // SG_GROUPS

    def body(pre_ref, gv_ref, ws_ref, bs_ref, o_ref):
        zu, _ = _gelu(pre_ref[:, :w].astype(F32))
        zv, _ = _gelu(pre_ref[:, w:].astype(F32))
        _, vhat = _rms_stats(zv)
        vn = (vhat * gv_ref[...]).astype(BF16)
        for h in range(SG_GROUPS):
            cols = slice(h * gw, (h + 1) * gw)
            wsm = _masked_ws(ws_ref, h).astype(BF16)
            gate = jnp.dot(wsm, vn[:, cols], preferred_element_type=F32)
            gate = gate + jnp.tile(bs_ref[h], (1, gw // LANES))
            o_ref[:, cols] = (zu[:, cols] * gate).astype(BF16)

    return pl.pallas_call(
        body, name=name, grid=(s // CHUNK,),
        in_specs=[pl.BlockSpec((CHUNK, w2), lambda i: (i, 0)), pl.BlockSpec((1, w), lambda i: (0, 0)),
                  pl.BlockSpec((SG_GROUPS, CHUNK, CHUNK), lambda i: (0, 0, 0)),
                  pl.BlockSpec((SG_GROUPS, CHUNK, LANES), lambda i: (0, 0, 0))],
        out_specs=pl.BlockSpec((CHUNK, w), lambda i: (i, 0)),
        out_shape=jax.ShapeDtypeStruct((s, w), BF16),
        compiler_params=_params(("parallel",)),
    )(pre, gv, ws, bs_wide)


def _mixb_bwd(pre, dug, gv, ws, bs_wide, name):
    s, w2 = pre.shape
    w = w2 // 2
    gw = w // SG_GROUPS

    def body(pre_ref, dug_ref, gv_ref, ws_ref, bs_ref, o_ref, dws_ref, dbs_ref, dgv_ref, dvn_ref):
        first = pl.program_id(0) == 0

        @pl.when(first)
        def _():
            dws_ref[...] = jnp.zeros_like(dws_ref)
            dbs_ref[...] = jnp.zeros_like(dbs_ref)

        pu = pre_ref[:, :w].astype(F32)
        pv = pre_ref[:, w:].astype(F32)
        zu, thu = _gelu(pu)
        zv, thv = _gelu(pv)
        inv, vhat = _rms_stats(zv)
        gvv = gv_ref[...]
        vn = (vhat * gvv).astype(BF16)
        for h in range(SG_GROUPS):
            cols = slice(h * gw, (h + 1) * gw)
            wsm = _masked_ws(ws_ref, h).astype(BF16)
            gate = jnp.dot(wsm, vn[:, cols], preferred_element_type=F32)
            gate = gate + jnp.tile(bs_ref[h], (1, gw // LANES))
            dug_h = dug_ref[:, cols].astype(F32)
            dgate = dug_h * zu[:, cols]
            dgate_b = dgate.astype(BF16)
            o_ref[:, cols] = (dug_h * gate * _gelu_grad(pu[:, cols], thu[:, cols])).astype(BF16)
            dbs_ref[h] += jnp.broadcast_to(jnp.sum(dgate, axis=-1, keepdims=True), (CHUNK, LANES))
            dws = lax.dot_general(dgate_b, vn[:, cols], _DIMS["nt"], preferred_element_type=F32)
            t = lax.broadcasted_iota(jnp.int32, (CHUNK, CHUNK), 0)
            sx = lax.broadcasted_iota(jnp.int32, (CHUNK, CHUNK), 1)
            dws_ref[h] += jnp.where(sx <= t, dws, 0.0)
            dvn_ref[:, cols] = lax.dot_general(wsm, dgate_b, _DIMS["tn"], preferred_element_type=F32)
        dvn = dvn_ref[...]
        part = jnp.sum(dvn * vhat, axis=0, keepdims=True)

        @pl.when(first)
        def _():
            dgv_ref[...] = part

        @pl.when(jnp.logical_not(first))
        def _():
            dgv_ref[...] += part

        dvhat = dvn * gvv
        dzv = inv * (dvhat - vhat * jnp.mean(dvhat * vhat, axis=-1, keepdims=True))
        o_ref[:, w:] = (dzv * _gelu_grad(pv, thv)).astype(BF16)

    return pl.pallas_call(
        body, name=name, grid=(s // CHUNK,),
        in_specs=[pl.BlockSpec((CHUNK, w2), lambda i: (i, 0)), pl.BlockSpec((CHUNK, w), lambda i: (i, 0)),
                  pl.BlockSpec((1, w), lambda i: (0, 0)),
                  pl.BlockSpec((SG_GROUPS, CHUNK, CHUNK), lambda i: (0, 0, 0)),
                  pl.BlockSpec((SG_GROUPS, CHUNK, LANES), lambda i: (0, 0, 0))],
        out_specs=[pl.BlockSpec((CHUNK, w2), lambda i: (i, 0)),
                   pl.BlockSpec((SG_GROUPS, CHUNK, CHUNK), lambda i: (0, 0, 0)),
                   pl.BlockSpec((SG_GROUPS, CHUNK, LANES), lambda i: (0, 0, 0)),
                   pl.BlockSpec((1, w), lambda i: (0, 0))],
        out_shape=[jax.ShapeDtypeStruct((s, w2), BF16), jax.ShapeDtypeStruct((SG_GROUPS, CHUNK, CHUNK), F32),
                   jax.ShapeDtypeStruct((SG_GROUPS, CHUNK, LANES), F32), jax.ShapeDtypeStruct((1, w), F32)],
        scratch_shapes=[pltpu.VMEM((CHUNK, w), F32)],
        compiler_params=_params(("arbitrary",)),
    )(pre, dug, gv, ws, bs_wide)


def _cast_layer(w3, layer, name):
    _, r, c = w3.shape
    tr = _pick(r, 256, SLAB)

    def body(w_ref, o_ref):
        o_ref[...] = w_ref[...].astype(BF16)

    return pl.pallas_call(
        body, name=name, grid=(r // tr,),
        in_specs=[pl.BlockSpec((None, tr, c), lambda i: (layer, i, 0))],
        out_specs=pl.BlockSpec((tr, c), lambda i: (i, 0)),
        out_shape=jax.ShapeDtypeStruct((r, c), BF16),
        compiler_params=_params(("parallel",)),
    )(w3)


def _adamw_math(w, g, m, v):
    m = ADAM_B1 * m + (1.0 - ADAM_B1) * g
    v = ADAM_B2 * v + (1.0 - ADAM_B2) * (g * g)
    m_hat = m / (1.0 - ADAM_B1 ** ADAM_STEP)
    v_hat = v / (1.0 - ADAM_B2 ** ADAM_STEP)
    delta = -ADAM_LR * (m_hat / (jnp.sqrt(v_hat) + ADAM_EPS) + ADAM_WD * w)
    return delta, m, v


def _adamw_sharded(recvs, w, m, v, name):
    nl, r, c = w.shape
    tc = _pick(c, 1536, LANES)
    tr = _pick(r, 64, SLAB)

    def body(*refs):
        recv_refs = refs[:nl]
        w_ref, m_ref, v_ref, g_ref, d_ref, nm_ref, nv_ref = refs[nl:]
        for layer, recv_ref in enumerate(recv_refs):
            @pl.when(pl.program_id(0) == layer)
            def _():
                g = recv_ref[0].astype(F32)
                for q in range(1, N_DEV):
                    g = g + recv_ref[q].astype(F32)
                delta, nm, nv = _adamw_math(w_ref[...], g, m_ref[...], v_ref[...])
                g_ref[...] = g
                d_ref[...] = delta
                nm_ref[...] = nm
                nv_ref[...] = nv

    def recv_spec(layer):
        return pl.BlockSpec((N_DEV, tr, tc),
                            lambda l, i, j: (0, jnp.where(l == layer, i, 0), jnp.where(l == layer, j, 0)))

    blk = pl.BlockSpec((None, tr, tc), lambda l, i, j: (l, i, j))
    out = jax.ShapeDtypeStruct((nl, r, c), F32)
    return pl.pallas_call(
        body, name=name, grid=(nl, r // tr, c // tc),
        in_specs=[recv_spec(layer) for layer in range(nl)] + [blk, blk, blk],
        out_specs=[blk] * 4, out_shape=[out] * 4,
        compiler_params=_params(("parallel",) * 3),
    )(*recvs, w, m, v)


def _adamw_packed(w, g, m, v, name):
    r, c = w.shape
    tr = _pick(r, 256, 8)

    def body(w_ref, g_ref, m_ref, v_ref, d_ref, nm_ref, nv_ref):
        delta, nm, nv = _adamw_math(w_ref[...], g_ref[...], m_ref[...], v_ref[...])
        d_ref[...] = delta
        nm_ref[...] = nm
        nv_ref[...] = nv

    blk = pl.BlockSpec((tr, c), lambda i: (i, 0))
    out = jax.ShapeDtypeStruct((r, c), F32)
    return pl.pallas_call(
        body, name=name, grid=(r // tr,), in_specs=[blk] * 4, out_specs=[blk] * 3, out_shape=[out] * 3,
        compiler_params=_params(("parallel",)),
    )(w, g, m, v)


def _pack(arrays):
    parts = []
    for a in arrays:
        flat = a.reshape(-1).astype(F32)
        pad = (-flat.shape[0]) % PACK_GRANULE
        parts.append(jnp.pad(flat, (0, pad)) if pad else flat)
    return jnp.concatenate(parts).reshape(-1, LANES)


def _unpack(buf, shapes):
    flat = buf.reshape(-1)
    out, off = [], 0
    for shp in shapes:
        n = math.prod(shp)
        out.append(flat[off:off + n].reshape(shp))
        off += n + (-n) % PACK_GRANULE
    return out


def _mesh_pos():
    return lax.axis_index("x"), lax.axis_index("y"), lax.axis_index("c")


def _coords(q):
    return q // 4, (q // 2) % 2, q % 2


def _shard_of(ref, q, shard_shape, axis):
    r, c = shard_shape
    if axis == 0:
        return ref.at[pl.ds(pl.multiple_of(q * r, SLAB), r), :]
    return ref.at[:, pl.ds(pl.multiple_of(q * c, LANES), c)]


_HBM = pl.BlockSpec(memory_space=pltpu.HBM)
_SEM = pl.BlockSpec(memory_space=pltpu.SEMAPHORE)
_EFFECT = pltpu.SideEffectType.DATAFLOW_SIDE_EFFECTING


def _exchange_shapes(gather, src_shape, axis):
    r, c = src_shape
    if gather:
        return (r, c), ((r * N_DEV, c) if axis == 0 else (r, c * N_DEV))
    shard = (r // N_DEV, c) if axis == 0 else (r, c // N_DEV)
    return shard, (N_DEV,) + shard


def _exchange_copies(gather, src, land, sems, axis):
    send_sems, recv_sems, own_sem = sems
    x, y, c_ = _mesh_pos()
    me = 4 * x + 2 * y + c_
    shard, _ = _exchange_shapes(gather, src.shape, axis)

    def piece(q):
        return src if gather else _shard_of(src, q, shard, axis)

    def place(q):
        return _shard_of(land, q, shard, axis) if gather else land.at[q]

    own = pltpu.make_async_copy(piece(me), place(me), own_sem.at[0])
    sends, arrivals = [], []
    for step in range(1, N_DEV):
        to = (me + step) % N_DEV
        frm = (me + N_DEV - step) % N_DEV
        sends.append(pltpu.make_async_remote_copy(
            src_ref=piece(to), dst_ref=place(me), send_sem=send_sems.at[step - 1], recv_sem=recv_sems.at[step - 1],
            device_id=_coords(to), device_id_type=MESH))
        arrivals.append(pltpu.make_async_remote_copy(
            src_ref=piece(me), dst_ref=place(frm), send_sem=send_sems.at[step - 1], recv_sem=recv_sems.at[step - 1],
            device_id=_coords(frm), device_id_type=MESH))
    return own, sends, arrivals


def _exchange_start(gather, src, axis, name, after=None):
    _, land_shape = _exchange_shapes(gather, src.shape, axis)
    extra = () if after is None else (after,)

    def body(*refs):
        src_ref, land = refs[:2]
        send_sems, recv_sems, own_sem = refs[2 + len(extra):5 + len(extra)]
        own, sends, _ = _exchange_copies(gather, src_ref, land, (send_sems, recv_sems, own_sem), axis)
        own.start()
        for cp in sends:
            cp.start()
        refs[-1][...] = jnp.zeros_like(refs[-1])

    out = pl.pallas_call(
        body, name=name,
        out_shape=(pltpu.SemaphoreType.DMA((N_DEV - 1,)), pltpu.SemaphoreType.DMA((N_DEV - 1,)),
                   pltpu.SemaphoreType.DMA((1,)), pltpu.HBM(src.shape, src.dtype),
                   pltpu.HBM(land_shape, src.dtype), jax.ShapeDtypeStruct((8, LANES), F32)),
        in_specs=[_HBM, _HBM] + [pl.BlockSpec(memory_space=pl.ANY)] * len(extra),
        out_specs=(_SEM, _SEM, _SEM, _HBM, _HBM, pl.BlockSpec(memory_space=pltpu.VMEM)),
        input_output_aliases={0: 3, 1: 4},
        compiler_params=pltpu.CompilerParams(has_side_effects=_EFFECT),
    )(pltpu.with_memory_space_constraint(src, pltpu.HBM),
      pltpu.with_memory_space_constraint(lax.empty(land_shape, src.dtype), pltpu.HBM), *extra)
    return out[:5], out[5]


def _exchange_wait(gather, state, axis, after, name):
    send_sems, recv_sems, own_sem, src_thru, land_thru = state
    after = tuple(after) if isinstance(after, (tuple, list)) else (after,)

    def body(src, land, send_sems, recv_sems, own_sem, *rest):
        own, sends, arrivals = _exchange_copies(gather, src, land, (send_sems, recv_sems, own_sem), axis)
        for cp in sends:
            cp.wait_send()
        for cp in arrivals:
            cp.wait_recv()
        own.wait()

    return pl.pallas_call(
        body, name=name,
        out_shape=(pltpu.HBM(src_thru.shape, src_thru.dtype), pltpu.HBM(land_thru.shape, land_thru.dtype)),
        in_specs=[_HBM, _HBM, _SEM, _SEM, _SEM] + [pl.BlockSpec(memory_space=pl.ANY)] * len(after),
        out_specs=(_HBM, _HBM),
        input_output_aliases={0: 0, 1: 1},
        compiler_params=pltpu.CompilerParams(has_side_effects=_EFFECT),
    )(src_thru, land_thru, send_sems, recv_sems, own_sem, *after)[1]


def _gather2_copies(shard_ref, land, sems, axis, shard_shape):
    send1, recv1, own_sem, send2, recv2 = sems
    x, y, c = _mesh_pos()
    me, sibling = (x, y, c), (x, y, 1 - c)
    chips = [(1 - x, y), (x, 1 - y), (1 - x, 1 - y)]

    def region(dev):
        px, py, pc = dev
        return _shard_of(land, 4 * px + 2 * py + pc, shard_shape, axis)

    def copy(src, block, to, send, recv):
        return pltpu.make_async_remote_copy(src_ref=src, dst_ref=region(block), send_sem=send, recv_sem=recv,
                                            device_id=to, device_id_type=MESH)

    own = pltpu.make_async_copy(shard_ref, region(me), own_sem.at[0])
    peers = [sibling] + [(*chip, c) for chip in chips]
    sends1 = [copy(shard_ref, me, to, send1.at[k], recv1.at[k]) for k, to in enumerate(peers)]
    arrivals1 = [copy(shard_ref, frm, frm, send1.at[k], recv1.at[k]) for k, frm in enumerate(peers)]
    sends2, arrivals2 = [], []
    if send2 is not None:
        for j, chip in enumerate(chips):
            sends2.append(copy(region((*chip, c)), (*chip, c), sibling, send2.at[j], recv2.at[j]))
            arrivals2.append(copy(region((*chip, 1 - c)), (*chip, 1 - c), sibling, send2.at[j], recv2.at[j]))
    return own, sends1, arrivals1, sends2, arrivals2


def _gather2_start(shard, axis, name, after=None):
    _, land_shape = _exchange_shapes(True, shard.shape, axis)
    extra = () if after is None else (after,)

    def body(*refs):
        src_ref, land = refs[:2]
        send1, recv1, own_sem = refs[2 + len(extra):5 + len(extra)]
        own, sends1, _, _, _ = _gather2_copies(src_ref, land, (send1, recv1, own_sem, None, None), axis, shard.shape)
        own.start()
        for cp in sends1[1:] + sends1[:1]:
            cp.start()
        refs[-1][...] = jnp.zeros_like(refs[-1])

    out = pl.pallas_call(
        body, name=name,
        out_shape=(pltpu.SemaphoreType.DMA((4,)), pltpu.SemaphoreType.DMA((4,)), pltpu.SemaphoreType.DMA((1,)),
                   pltpu.HBM(shard.shape, shard.dtype), pltpu.HBM(land_shape, shard.dtype),
                   jax.ShapeDtypeStruct((8, LANES), F32)),
        in_specs=[_HBM, _HBM] + [pl.BlockSpec(memory_space=pl.ANY)] * len(extra),
        out_specs=(_SEM, _SEM, _SEM, _HBM, _HBM, pl.BlockSpec(memory_space=pltpu.VMEM)),
        input_output_aliases={0: 3, 1: 4},
        compiler_params=pltpu.CompilerParams(has_side_effects=_EFFECT),
    )(pltpu.with_memory_space_constraint(shard, pltpu.HBM),
      pltpu.with_memory_space_constraint(lax.empty(land_shape, shard.dtype), pltpu.HBM), *extra)
    return out[:5], out[5]


def _gather2_pass(state, axis, after, name):
    send1, recv1, own_sem, shard_thru, land_thru = state

    def body(src_ref, land, send1, recv1, own_sem, after_ref, send2, recv2, src_out, land_out, token):
        _, _, arrivals1, sends2, _ = _gather2_copies(src_ref, land, (send1, recv1, own_sem, send2, recv2), axis,
                                                     shard_thru.shape)
        for arrival, fwd in zip(arrivals1[1:], sends2):
            arrival.wait_recv()
            fwd.start()
        token[...] = jnp.zeros_like(token)

    out = pl.pallas_call(
        body, name=name,
        out_shape=(pltpu.SemaphoreType.DMA((3,)), pltpu.SemaphoreType.DMA((3,)),
                   pltpu.HBM(shard_thru.shape, shard_thru.dtype), pltpu.HBM(land_thru.shape, land_thru.dtype),
                   jax.ShapeDtypeStruct((8, LANES), F32)),
        in_specs=[_HBM, _HBM, _SEM, _SEM, _SEM, pl.BlockSpec(memory_space=pl.ANY)],
        out_specs=(_SEM, _SEM, _HBM, _HBM, pl.BlockSpec(memory_space=pltpu.VMEM)),
        input_output_aliases={0: 2, 1: 3},
        compiler_params=pltpu.CompilerParams(has_side_effects=_EFFECT),
    )(shard_thru, land_thru, send1, recv1, own_sem, after)
    return (send1, recv1, own_sem, out[0], out[1], out[2], out[3]), out[4]


def _gather2_wait(state, axis, after, name):
    send1, recv1, own_sem, send2, recv2, shard_thru, land_thru = state

    def body(src_ref, land, send1, recv1, own_sem, send2, recv2, after_ref, src_dead, got):
        own, sends1, arrivals1, sends2, arrivals2 = _gather2_copies(
            src_ref, land, (send1, recv1, own_sem, send2, recv2), axis, shard_thru.shape)
        for cp in sends1 + sends2:
            cp.wait_send()
        for cp in arrivals1[:1] + arrivals2:
            cp.wait_recv()
        own.wait()

    return pl.pallas_call(
        body, name=name,
        out_shape=(pltpu.HBM(shard_thru.shape, shard_thru.dtype), pltpu.HBM(land_thru.shape, land_thru.dtype)),
        in_specs=[_HBM, _HBM] + [_SEM] * 5 + [pl.BlockSpec(memory_space=pl.ANY)],
        out_specs=(_HBM, _HBM),
        input_output_aliases={0: 0, 1: 1},
        compiler_params=pltpu.CompilerParams(has_side_effects=_EFFECT),
    )(shard_thru, land_thru, send1, recv1, own_sem, send2, recv2, after)[1]


def _sum_slots(slots, name):
    _, r, c = slots.shape
    tr = _pick(r, 512, 8)

    def body(s_ref, o_ref):
        total = s_ref[0]
        for q in range(1, N_DEV):
            total = total + s_ref[q]
        o_ref[...] = total

    return pl.pallas_call(
        body, name=name, grid=(r // tr,),
        in_specs=[pl.BlockSpec((N_DEV, tr, c), lambda i: (0, i, 0))],
        out_specs=pl.BlockSpec((tr, c), lambda i: (i, 0)),
        out_shape=jax.ShapeDtypeStruct((r, c), F32),
        compiler_params=_params(("parallel",)),
    )(slots)


def kernel(x, a_norm, a_in, a_conv, a_out, b_norm, b_in, b_vnorm, b_ws, b_bs, b_out, f_norm, f_up, f_conv_w, f_conv_b, f_down, final_norm, loss_target, m_a_norm, m_a_in, m_a_conv, m_a_out, m_b_norm, m_b_in, m_b_vnorm, m_b_ws, m_b_bs, m_b_out, m_f_norm, m_f_up, m_f_conv_w, m_f_conv_b, m_f_down, m_final_norm, v_a_norm, v_a_in, v_a_conv, v_a_out, v_b_norm, v_b_in, v_b_vnorm, v_b_ws, v_b_bs, v_b_out, v_f_norm, v_f_up, v_f_conv_w, v_f_conv_b, v_f_down, v_final_norm):
    s, d = x.shape[1], x.shape[2]
    n_ffn = f_up.shape[0]
    f2 = f_up.shape[2] * N_DEV
    me = 4 * lax.axis_index("x") + 2 * lax.axis_index("y") + lax.axis_index("c")
    x0 = x.reshape(s, d)
    target = loss_target.reshape(s, d)

    wanted = [("a_in", _cast_layer(a_in, 0, "cast_a_in"), 1),
              ("small", _pack([a_conv, b_norm, b_vnorm, f_conv_w]), 0),
              ("a_out", _cast_layer(a_out, 0, "cast_a_out"), 0),
              ("f_up0", _cast_layer(f_up, 0, "cast_f_up0"), 1), ("f_down0", _cast_layer(f_down, 0, "cast_f_down0"), 0),
              ("b_in", _cast_layer(b_in, 0, "cast_b_in"), 1), ("b_out", _cast_layer(b_out, 0, "cast_b_out"), 0),
              ("f_up1", _cast_layer(f_up, 1, "cast_f_up1"), 1), ("f_down1", _cast_layer(f_down, 1, "cast_f_down1"), 0)]
    coming, tok, h0 = {}, None, None
    for n_started, (key, shard, axis) in enumerate(wanted):
        if n_started == 2:
            tok = h0 = _rmsnorm_fwd(x0, a_norm, "mixa_norm", after=tok)
        state, tok = _gather2_start(shard, axis, f"ag_start_{key}", after=tok)
        coming[key] = (state, axis)

    def pass_on(keys, after):
        for key in keys:
            state, axis = coming[key]
            state, after = _gather2_pass(state, axis, after, f"ag_pass_{key}")
            coming[key] = (state, axis)
        return after

    def arrived(key, after):
        state, axis = coming[key]
        return _gather2_wait(state, axis, after, f"ag_wait_{key}")

    cshard = a_conv.shape[2]
    fshard = f_conv_w.shape[2]
    w_a_in = arrived("a_in", pass_on(["a_in", "small"], tok))
    small_full = arrived("small", w_a_in)
    small_rows = small_full.reshape(N_DEV, -1)
    per_dev = _unpack_rows(small_rows, [(3, cshard), (cshard,), (cshard,), (n_ffn, 3, fshard)])
    a_conv_full = per_dev[0].transpose(1, 0, 2).reshape(3, d)
    b_norm_full = per_dev[1].reshape(1, d)
    b_vnorm_full = per_dev[2].reshape(1, d)
    f_conv_w_full = per_dev[3].transpose(1, 2, 0, 3).reshape(n_ffn, 3, f2)
    bs_wide = jnp.broadcast_to(b_bs[0][:, :, None], (SG_GROUPS, CHUNK, LANES))
    ws = b_ws[0]

    w_f_up, w_f_down = {}, {}

    def ffn_forward(xin, l, pass_first, pass_early, pass_late):
        h = _rmsnorm_fwd(xin, f_norm[l:l + 1], f"ffn{l}_norm", after=pass_on(pass_first, xin))
        w_f_up[l] = arrived(f"f_up{l}", pass_on(pass_early, h))
        up_g, up_a, cv_g, cv_a, act = _ffn_up_fused(h, w_f_up[l], f_conv_w_full[l], f_conv_b[l:l + 1], f"ffn{l}_up")
        up, cv = (up_g, up_a), (cv_g, cv_a)
        w_f_down[l] = arrived(f"f_down{l}", pass_on(pass_late, act))
        xout = _matmul(act, w_f_down[l], "nn", F32, f"ffn{l}_down", resid=xin, tk_cap=1408)
        return xout, (h, up, act, cv)

    gb, gc, xs, cva, ya = _mixa_in_fused(h0, w_a_in, a_conv_full, "mixa_in")
    bcx = (gb, gc, xs)
    w_a_out = arrived("a_out", pass_on(["a_out"], ya))
    x1 = _matmul(ya, w_a_out, "nn", F32, "mixa_out", resid=x0)
    x2, saved0 = ffn_forward(x1, 0, ["f_up0"], ["f_down0"], ["b_in", "b_out", "f_up1", "f_down1"])
    h2 = _rmsnorm_fwd(x2, b_norm_full, "mixb_norm")
    w_b_in = arrived("b_in", h2)
    pre = _matmul(h2, w_b_in, "nn", BF16, "mixb_in")
    ug = _mixb_fwd(pre, b_vnorm_full, ws, bs_wide, "mixb_mid")
    w_b_out = arrived("b_out", ug)
    x3 = _matmul(ug, w_b_out, "nn", F32, "mixb_out", resid=x2)
    x4, saved1 = ffn_forward(x3, 1, [], [], [])
    dx4, dx4b, loss_part, g_final = _final_loss(x4, final_norm.reshape(1, d), target, "loss_head")

    def _rs_start(grad, axis, name):
        return _exchange_start(False, grad, axis, name)

    def ffn_backward(xin, l, saved, dx, dxb):
        h, up, act, cv = saved
        g_down = _matmul(act, dxb, "tn", BF16, f"ffn{l}_down_dw", tm_cap=1408)
        rs_down, tok = _rs_start(g_down, 0, f"rs_start_f_down{l}")
        dup, cwg, cwa, cbg, cba = _ffn_down_dx_fused(dxb, w_f_down[l], up, cv, f_conv_w_full[l], tok,
                                                     f"ffn{l}_down_dx")
        g_cw, g_cb = jnp.concatenate([cwg, cwa], axis=1), jnp.concatenate([cbg, cba], axis=1)
        g_up = _matmul(h, dup, "tn", BF16, f"ffn{l}_up_dw", tk_cap=4096)
        rs_up, tok = _rs_start(g_up, 1, f"rs_start_f_up{l}")
        dh = _matmul(dup, w_f_up[l], "nt", BF16, f"ffn{l}_up_dx", after=tok)
        dxin, dxinb, g_norm = _rmsnorm_bwd(xin, f_norm[l:l + 1], dh, dx, f"ffn{l}_norm_bwd")
        return dxin, dxinb, (rs_up, rs_down, g_cw, g_cb, g_norm)

    dx3, dx3b, gf1 = ffn_backward(x3, 1, saved1, dx4, dx4b)
    g_b_out = _matmul(ug, dx3b, "tn", BF16, "mixb_out_dw", tk_cap=4096)
    rs_b_out, tok = _rs_start(g_b_out, 0, "rs_start_b_out")
    dug = _matmul(dx3b, w_b_out, "nt", BF16, "mixb_out_dx", after=tok)
    dpre, g_ws, g_bs_wide, g_bvnorm = _mixb_bwd(pre, dug, b_vnorm_full, ws, bs_wide, "mixb_mid_bwd")
    g_b_in = _matmul(h2, dpre, "tn", BF16, "mixb_in_dw", tk_cap=4096)
    rs_b_in, tok = _rs_start(g_b_in, 1, "rs_start_b_in")
    dh2 = _matmul(dpre, w_b_in, "nt", BF16, "mixb_in_dx", after=tok)
    dx2, dx2b, g_bnorm = _rmsnorm_bwd(x2, b_norm_full, dh2, dx3, "mixb_norm_bwd")
    dx1, dx1b, gf0 = ffn_backward(x1, 0, saved0, dx2, dx2b)
    g_a_out = _matmul(ya, dx1b, "tn", BF16, "mixa_out_dw", tk_cap=4096)
    rs_a_out, tok = _rs_start(g_a_out, 0, "rs_start_a_out")
    dbcx, g_aconv = _mixa_out_dx_fused(dx1b, w_a_out, bcx, cva, a_conv_full, tok, "mixa_out_dx")
    g_a_in = _matmul(h0, dbcx, "tn", BF16, "mixa_in_dw", tk_cap=4096)
    rs_a_in, tok = _rs_start(g_a_in, 1, "rs_start_a_in")
    dh0 = _matmul(dbcx, w_a_in, "nt", BF16, "mixa_in_dx", after=tok)
    grad_x, _, g_anorm = _rmsnorm_bwd(x0, a_norm, dh0, dx1, "mixa_norm_bwd")

    full_shapes = [(1, LANES), (1, d), (3, d), (1, d), (1, d), (SG_GROUPS, CHUNK, CHUNK), (SG_GROUPS, CHUNK),
                   (n_ffn, d), (n_ffn, 3, f2), (n_ffn, f2), (1, d)]
    parts = [loss_part, g_anorm, g_aconv, g_bnorm, g_bvnorm, g_ws, g_bs_wide[:, :, 0],
             jnp.concatenate([gf0[4], gf1[4]], axis=0), jnp.stack([gf0[2], gf1[2]]),
             jnp.concatenate([gf0[3], gf1[3]], axis=0), g_final]
    small_part = _pack(parts)
    small_state, small_tok = _exchange_start(True, small_part, 0, "ar_start_small", after=grad_x)

    big = {}
    for name, states, axis, w, m, v in (
            ("f_down", (gf0[1], gf1[1]), 0, f_down, m_f_down, v_f_down),
            ("f_up", (gf0[0], gf1[0]), 1, f_up, m_f_up, v_f_up),
            ("b_out", (rs_b_out,), 0, b_out, m_b_out, v_b_out), ("b_in", (rs_b_in,), 1, b_in, m_b_in, v_b_in),
            ("a_out", (rs_a_out,), 0, a_out, m_a_out, v_a_out), ("a_in", (rs_a_in,), 1, a_in, m_a_in, v_a_in)):
        recvs = [_exchange_wait(False, st, axis, small_tok, f"rs_wait_{name}{l}") for l, st in enumerate(states)]
        big[name] = _adamw_sharded(recvs, w, m, v, f"adamw_{name}")

    slots = _exchange_wait(True, small_state, 0, [res[0] for res in big.values()], "ar_wait_small")
    total = _sum_slots(slots.reshape((N_DEV,) + small_part.shape), "ar_sum_small")
    (loss_v, r_anorm, r_aconv, r_bnorm, r_bvnorm, r_ws, r_bs, r_fnorm, r_fcw, r_fcb, r_final) = _unpack(total, full_shapes)
    small_grads = [
        r_anorm,
        lax.dynamic_slice_in_dim(r_aconv, me * cshard, cshard, axis=1).reshape(a_conv.shape),
        lax.dynamic_slice_in_dim(r_bnorm, me * cshard, cshard, axis=1),
        lax.dynamic_slice_in_dim(r_bvnorm, me * cshard, cshard, axis=1),
        r_ws.reshape(b_ws.shape), r_bs.reshape(b_bs.shape), r_fnorm,
        lax.dynamic_slice_in_dim(r_fcw, me * fshard, fshard, axis=2),
        r_fcb, r_final.reshape(final_norm.shape)]
    small_w = [a_norm, a_conv, b_norm, b_vnorm, b_ws, b_bs, f_norm, f_conv_w, f_conv_b, final_norm]
    small_m = [m_a_norm, m_a_conv, m_b_norm, m_b_vnorm, m_b_ws, m_b_bs, m_f_norm, m_f_conv_w, m_f_conv_b, m_final_norm]
    small_v = [v_a_norm, v_a_conv, v_b_norm, v_b_vnorm, v_b_ws, v_b_bs, v_f_norm, v_f_conv_w, v_f_conv_b, v_final_norm]
    shapes = [w.shape for w in small_w]
    packed = _adamw_packed(_pack(small_w), _pack(small_grads), _pack(small_m), _pack(small_v), "adamw_small")
    s_delta, s_m, s_v = (_unpack(p, shapes) for p in packed)
    small_names = ["a_norm", "a_conv", "b_norm", "b_vnorm", "b_ws", "b_bs", "f_norm", "f_conv_w", "f_conv_b", "final_norm"]
    small = {nm: (small_grads[i], s_delta[i], s_m[i], s_v[i]) for i, nm in enumerate(small_names)}

    order = ["a_norm", "a_in", "a_conv", "a_out", "b_norm", "b_in", "b_vnorm", "b_ws", "b_bs", "b_out",
             "f_norm", "f_up", "f_conv_w", "f_conv_b", "f_down", "final_norm"]
    res = {nm: (big[nm] if nm in big else small[nm]) for nm in order}
    outs = [loss_v[0, 0], grad_x.reshape(x.shape)]
    for k in range(4):
        outs += [res[nm][k] for nm in order]
    return tuple(outs)


def _unpack_rows(rows, shapes):
    out, off = [], 0
    for shp in shapes:
        n = math.prod(shp)
        out.append(rows[:, off:off + n].reshape((N_DEV,) + tuple(shp)))
        off += n + (-n) % PACK_GRANULE
    return out
```

```python
import math

import jax
import jax.numpy as jnp
from jax import lax
from jax.experimental import pallas as pl
from jax.experimental.pallas import tpu as pltpu

F32 = jnp.float32
BF16 = jnp.bfloat16
MESH = pl.DeviceIdType.MESH

N_DEV = 8
RMS_EPS = 1e-5
CHUNK = 128
SG_GROUPS = 8
ADAM_LR = 0.001
ADAM_B1 = 0.9
ADAM_B2 = 0.999
ADAM_EPS = 1e-08
ADAM_WD = 0.01
ADAM_STEP = 10

LANES = 128
SLAB = 16
FUSE_STRIP = 256
FUSE_HALO = 8
VMEM_LIMIT = 56 * 1024 * 1024
PACK_GRANULE = 8 * LANES


def _pick(dim, cap, mult):
    best = None
    t = mult
    while t <= min(dim, cap):
        if dim % t == 0:
            best = t
        t += mult
    return dim if best is None else best


def _params(semantics=None):
    return pltpu.CompilerParams(dimension_semantics=semantics, vmem_limit_bytes=VMEM_LIMIT)


_DIMS = {
    "nn": (((1,), (0,)), ((), ())),
    "nt": (((1,), (1,)), ((), ())),
    "tn": (((0,), (0,)), ((), ())),
}


def _matmul(a, b, mode, out_dtype, name, resid=None, after=None, tm_cap=1024, tn_cap=1024, tk_cap=2816):
    if mode == "nn":
        (m, k), n = a.shape, b.shape[1]
    elif mode == "nt":
        (m, k), n = a.shape, b.shape[0]
    else:
        (k, m), n = a.shape, b.shape[1]
    tm, tn, tk = _pick(m, tm_cap, LANES), _pick(n, tn_cap, LANES), _pick(k, tk_cap, LANES)
    nk = k // tk
    n_in = 2 + (resid is not None) + (after is not None)

    def body(*refs):
        a_ref, b_ref = refs[:2]
        r_ref = refs[2] if resid is not None else None
        o_ref = refs[n_in]
        prod = lax.dot_general(a_ref[...], b_ref[...], _DIMS[mode], preferred_element_type=F32)

        def finish(r):
            if r_ref is not None:
                r = r + r_ref[...]
            o_ref[...] = r.astype(out_dtype)

        if nk == 1:
            finish(prod)
            return
        acc_ref = refs[n_in + 1]
        kk = pl.program_id(2)

        @pl.when(kk == 0)
        def _():
            acc_ref[...] = prod

        @pl.when(jnp.logical_and(kk > 0, kk < nk - 1))
        def _():
            acc_ref[...] += prod

        @pl.when(kk == nk - 1)
        def _():
            finish(acc_ref[...] + prod)

    a_spec = (pl.BlockSpec((tk, tm), lambda i, j, kk: (kk, i)) if mode == "tn"
              else pl.BlockSpec((tm, tk), lambda i, j, kk: (i, kk)))
    b_spec = (pl.BlockSpec((tn, tk), lambda i, j, kk: (j, kk)) if mode == "nt"
              else pl.BlockSpec((tk, tn), lambda i, j, kk: (kk, j)))
    o_spec = pl.BlockSpec((tm, tn), lambda i, j, kk: (i, j))
    in_specs = [a_spec, b_spec] + ([o_spec] if resid is not None else [])
    args = (a, b) + ((resid,) if resid is not None else ())
    if after is not None:
        in_specs.append(pl.BlockSpec(memory_space=pl.ANY))
        args += (after,)
    return pl.pallas_call(
        body, name=name, grid=(m // tm, n // tn, nk),
        in_specs=in_specs, out_specs=o_spec,
        out_shape=jax.ShapeDtypeStruct((m, n), out_dtype),
        scratch_shapes=[pltpu.VMEM((tm, tn), F32)] if nk > 1 else [],
        compiler_params=_params(("parallel", "parallel", "arbitrary")),
    )(*args)


def _rms_stats(xf):
    inv = lax.rsqrt(jnp.mean(xf * xf, axis=-1, keepdims=True) + RMS_EPS)
    return inv, xf * inv


def _rmsnorm_fwd(x, g, name, after=None):
    s, d = x.shape
    tm = _pick(s, 512, SLAB)
    extra = () if after is None else (after,)

    def body(x_ref, g_ref, *rest):
        _, xhat = _rms_stats(x_ref[...])
        rest[-1][...] = (xhat * g_ref[...]).astype(BF16)

    return pl.pallas_call(
        body, name=name, grid=(s // tm,),
        in_specs=[pl.BlockSpec((tm, d), lambda i: (i, 0)), pl.BlockSpec((1, d), lambda i: (0, 0))]
        + [pl.BlockSpec(memory_space=pl.ANY)] * len(extra),
        out_specs=pl.BlockSpec((tm, d), lambda i: (i, 0)),
        out_shape=jax.ShapeDtypeStruct((s, d), BF16),
        compiler_params=_params(("parallel",)),
    )(x, g, *extra)


def _rmsnorm_bwd(x, g, dh, dx_out, name):
    s, d = x.shape
    tm = _pick(s, 256, SLAB)

    def body(x_ref, g_ref, dh_ref, dxo_ref, dxi_ref, dxib_ref, dg_ref):
        inv, xhat = _rms_stats(x_ref[...])
        dhv = dh_ref[...].astype(F32)
        dxhat = dhv * g_ref[...]
        proj = jnp.mean(dxhat * xhat, axis=-1, keepdims=True)
        dx = dxo_ref[...] + inv * (dxhat - xhat * proj)
        dxi_ref[...] = dx
        dxib_ref[...] = dx.astype(BF16)
        part = jnp.sum(dhv * xhat, axis=0, keepdims=True)

        @pl.when(pl.program_id(0) == 0)
        def _():
            dg_ref[...] = part

        @pl.when(pl.program_id(0) > 0)
        def _():
            dg_ref[...] += part

    row = pl.BlockSpec((tm, d), lambda i: (i, 0))
    vec = pl.BlockSpec((1, d), lambda i: (0, 0))
    return pl.pallas_call(
        body, name=name, grid=(s // tm,),
        in_specs=[row, vec, row, row], out_specs=[row, row, vec],
        out_shape=[jax.ShapeDtypeStruct((s, d), F32), jax.ShapeDtypeStruct((s, d), BF16),
                   jax.ShapeDtypeStruct((1, d), F32)],
        compiler_params=_params(("arbitrary",)),
    )(x, g, dh, dx_out)


def _final_loss(x, g, target, name):
    s, d = x.shape
    tm = _pick(s, 256, SLAB)

    def body(x_ref, g_ref, t_ref, dx_ref, dxb_ref, loss_ref, dg_ref):
        inv, xhat = _rms_stats(x_ref[...])
        gv = g_ref[...]
        err = xhat * gv - t_ref[...]
        loss = 0.5 * jnp.sum(jnp.mean(err * err, axis=-1, keepdims=True), axis=0, keepdims=True)
        dy = err * (1.0 / d)
        dxhat = dy * gv
        proj = jnp.mean(dxhat * xhat, axis=-1, keepdims=True)
        dx = inv * (dxhat - xhat * proj)
        dx_ref[...] = dx
        dxb_ref[...] = dx.astype(BF16)
        part = jnp.sum(dy * xhat, axis=0, keepdims=True)
        loss_row = jnp.broadcast_to(loss, (1, LANES))

        @pl.when(pl.program_id(0) == 0)
        def _():
            dg_ref[...] = part
            loss_ref[...] = loss_row

        @pl.when(pl.program_id(0) > 0)
        def _():
            dg_ref[...] += part
            loss_ref[...] += loss_row

    row = pl.BlockSpec((tm, d), lambda i: (i, 0))
    vec = pl.BlockSpec((1, d), lambda i: (0, 0))
    return pl.pallas_call(
        body, name=name, grid=(s // tm,),
        in_specs=[row, vec, row],
        out_specs=[row, row, pl.BlockSpec((1, LANES), lambda i: (0, 0)), vec],
        out_shape=[jax.ShapeDtypeStruct((s, d), F32), jax.ShapeDtypeStruct((s, d), BF16),
                   jax.ShapeDtypeStruct((1, LANES), F32), jax.ShapeDtypeStruct((1, d), F32)],
        compiler_params=_params(("arbitrary",)),
    )(x, g, target)


def _mixa_in_fused(h, w_in, wc, name):
    s, d = h.shape
    tm = _pick(s, 1024, LANES)
    tn = _pick(d, 512, FUSE_STRIP)
    nj = d // tn

    def body(h_ref, wb_ref, wg_ref, wx_ref, wc_ref, gb_ref, gc_ref, xs_ref, cv_ref, y_ref, carry):
        @pl.when(pl.program_id(1) == 0)
        def _():
            carry[...] = jnp.zeros_like(carry)

        def matmul(st):
            cols = slice(st * FUSE_STRIP, (st + 1) * FUSE_STRIP)
            return tuple(jnp.dot(h_ref[...], w_ref[:, cols], preferred_element_type=F32).astype(BF16)
                         for w_ref in (wb_ref, wg_ref, wx_ref))

        n_strips = tn // FUSE_STRIP
        parts = matmul(0)
        for st in range(n_strips):
            parts_next = matmul(st + 1) if st + 1 < n_strips else None
            cols = slice(st * FUSE_STRIP, (st + 1) * FUSE_STRIP)
            for ref, part in zip((gb_ref, gc_ref, xs_ref), parts):
                ref[:, cols] = part
            p = parts[1].astype(F32) * parts[2].astype(F32)
            ext = jnp.concatenate([carry[:, cols], p], axis=0)
            s1 = pltpu.roll(ext, 1, 0)[FUSE_HALO:, :]
            s2 = pltpu.roll(ext, 2, 0)[FUSE_HALO:, :]
            carry[:, cols] = p[tm - FUSE_HALO:, :]
            cv = wc_ref[0:1, cols] * s2 + wc_ref[1:2, cols] * s1 + wc_ref[2:3, cols] * p
            cv_ref[:, cols] = cv.astype(BF16)
            y_ref[:, cols] = (parts[0].astype(F32) * cv).astype(BF16)
            parts = parts_next

    def cols_of(rows, offset):
        return pl.BlockSpec((rows, tn), lambda j, i: (0, j + offset))

    tile = pl.BlockSpec((tm, tn), lambda j, i: (i, j))
    return pl.pallas_call(
        body, name=name, grid=(nj, s // tm),
        in_specs=[pl.BlockSpec((tm, d), lambda j, i: (i, 0)), cols_of(d, 0), cols_of(d, nj), cols_of(d, 2 * nj),
                  cols_of(3, 0)],
        out_specs=[tile] * 5, out_shape=[jax.ShapeDtypeStruct((s, d), BF16)] * 5,
        scratch_shapes=[pltpu.VMEM((FUSE_HALO, tn), F32)],
        compiler_params=_params(("parallel", "arbitrary")),
    )(h, w_in, w_in, w_in, wc)


def _mixa_out_dx_fused(dxb, w_out, bcx, cv, wc, after, name):
    s, d = dxb.shape
    tm = _pick(s, 1024, LANES)
    tn = _pick(d, 512, FUSE_STRIP)
    nj, ni = d // tn, s // tm
    n_steps = nj * ni
    sub = tm // FUSE_HALO

    def body(dx_ref, w_ref, gb_ref, gc_ref, xs_ref, cv_ref, wc_ref, after_ref,
             dbcx_hbm, dwc_ref, out_buf, out_sem, carry, acc):
        j, i = pl.program_id(0), pl.program_id(1)
        step = j * ni + i
        slot = lax.rem(step, 2)
        row0 = pl.multiple_of((ni - 1 - i) * tm, tm)

        def out_copy(part, slot_=None):
            slot_ = slot if slot_ is None else slot_
            col0 = pl.multiple_of(part * d + j * tn, LANES)
            return pltpu.make_async_copy(out_buf.at[slot_, part], dbcx_hbm.at[pl.ds(row0, tm), pl.ds(col0, tn)],
                                         out_sem.at[slot_, part])

        @pl.when(step >= 2)
        def _():
            for part in range(3):
                out_copy(part).wait()

        @pl.when(i == 0)
        def _():
            carry[...] = jnp.zeros_like(carry)
            acc[...] = jnp.zeros_like(acc)

        for st in range(tn // FUSE_STRIP):
            cols = slice(st * FUSE_STRIP, (st + 1) * FUSE_STRIP)
            dyv = lax.dot_general(dx_ref[...], w_ref[cols, :], _DIMS["nt"], preferred_element_type=F32)
            gc = gc_ref[:, cols].astype(F32)
            xs = xs_ref[:, cols].astype(F32)
            d0 = dyv * gb_ref[:, cols].astype(F32)
            ext = jnp.concatenate([d0, carry[:, cols]], axis=0)
            d1 = pltpu.roll(ext, tm + FUSE_HALO - 1, 0)[:tm, :]
            d2 = pltpu.roll(ext, tm + FUSE_HALO - 2, 0)[:tm, :]
            carry[:, cols] = d0[:FUSE_HALO, :]
            dp = wc_ref[2:3, cols] * d0 + wc_ref[1:2, cols] * d1 + wc_ref[0:1, cols] * d2
            out_buf[slot, 0, :, cols] = (dyv * cv_ref[:, cols].astype(F32)).astype(BF16)
            out_buf[slot, 1, :, cols] = (dp * xs).astype(BF16)
            out_buf[slot, 2, :, cols] = (dp * gc).astype(BF16)
            p = gc * xs
            for k, term in enumerate((d2 * p, d1 * p, d0 * p)):
                acc[k, :, cols] += jnp.sum(term.reshape(sub, FUSE_HALO, FUSE_STRIP), axis=0)

        for part in range(3):
            out_copy(part).start()

        @pl.when(i == ni - 1)
        def _():
            for k in range(3):
                dwc_ref[k:k + 1, :] = jnp.sum(acc[k], axis=0, keepdims=True)

        @pl.when(step == n_steps - 1)
        def _():
            for part in range(3):
                out_copy(part).wait()
                if n_steps > 1:
                    out_copy(part, 1 - slot).wait()

    tile = pl.BlockSpec((tm, tn), lambda j, i: (ni - 1 - i, j))
    return pl.pallas_call(
        body, name=name, grid=(nj, ni),
        in_specs=[pl.BlockSpec((tm, d), lambda j, i: (ni - 1 - i, 0)), pl.BlockSpec((tn, d), lambda j, i: (j, 0)),
                  tile, tile, tile, tile, pl.BlockSpec((3, tn), lambda j, i: (0, j)),
                  pl.BlockSpec(memory_space=pl.ANY)],
        out_specs=[pl.BlockSpec(memory_space=pl.ANY), pl.BlockSpec((3, tn), lambda j, i: (0, j))],
        out_shape=[jax.ShapeDtypeStruct((s, 3 * d), BF16), jax.ShapeDtypeStruct((3, d), F32)],
        scratch_shapes=[pltpu.VMEM((2, 3, tm, tn), BF16), pltpu.SemaphoreType.DMA((2, 3)),
                        pltpu.VMEM((FUSE_HALO, tn), F32), pltpu.VMEM((3, FUSE_HALO, tn), F32)],
        compiler_params=_params(("arbitrary", "arbitrary")),
    )(dxb, w_out, bcx[0], bcx[1], bcx[2], cv, wc, after)


def _sigmoid(z):
    return 0.5 * jnp.tanh(0.5 * z) + 0.5


def _ffn_up_fused(h, w_up, cw, cb, name):
    s, d = h.shape
    f = w_up.shape[1] // 2
    tm = _pick(s, 1024, LANES)
    tn = _pick(f, 512, FUSE_STRIP)
    nj = f // tn

    def body(h_ref, wg_ref, wa_ref, cwg_ref, cwa_ref, cbg_ref, cba_ref,
             upg_ref, upa_ref, cvg_ref, cva_ref, act_ref, carry_g, carry_a):
        @pl.when(pl.program_id(1) == 0)
        def _():
            carry_g[...] = jnp.zeros_like(carry_g)
            carry_a[...] = jnp.zeros_like(carry_a)

        def matmul(st):
            cols = slice(st * FUSE_STRIP, (st + 1) * FUSE_STRIP)
            return tuple(jnp.dot(h_ref[...], w_ref[:, cols], preferred_element_type=F32).astype(BF16)
                         for w_ref in (wg_ref, wa_ref))

        def conv(up, cw_ref, cb_ref, carry, up_ref, cv_ref, cols):
            up_ref[:, cols] = up
            x = up.astype(F32)
            ext = jnp.concatenate([carry[:, cols], x], axis=0)
            s1 = pltpu.roll(ext, 1, 0)[FUSE_HALO:, :]
            s2 = pltpu.roll(ext, 2, 0)[FUSE_HALO:, :]
            carry[:, cols] = x[tm - FUSE_HALO:, :]
            cv = cw_ref[0:1, cols] * s2 + cw_ref[1:2, cols] * s1 + cw_ref[2:3, cols] * x + cb_ref[:, cols]
            cv_ref[:, cols] = cv.astype(BF16)
            return cv

        n_strips = tn // FUSE_STRIP
        ups = matmul(0)
        for st in range(n_strips):
            ups_next = matmul(st + 1) if st + 1 < n_strips else None
            cols = slice(st * FUSE_STRIP, (st + 1) * FUSE_STRIP)
            gcv = conv(ups[0], cwg_ref, cbg_ref, carry_g, upg_ref, cvg_ref, cols)
            acv = conv(ups[1], cwa_ref, cba_ref, carry_a, upa_ref, cva_ref, cols)
            act_ref[:, cols] = (gcv * _sigmoid(gcv) * acv).astype(BF16)
            ups = ups_next

    def cols_of(rows, offset):
        return pl.BlockSpec((rows, tn), lambda j, i: (0, j + offset))

    tile = pl.BlockSpec((tm, tn), lambda j, i: (i, j))
    out = jax.ShapeDtypeStruct((s, f), BF16)
    return pl.pallas_call(
        body, name=name, grid=(nj, s // tm),
        in_specs=[pl.BlockSpec((tm, d), lambda j, i: (i, 0)), cols_of(d, 0), cols_of(d, nj),
                  cols_of(3, 0), cols_of(3, nj), cols_of(1, 0), cols_of(1, nj)],
        out_specs=[tile] * 5, out_shape=[out] * 5,
        scratch_shapes=[pltpu.VMEM((FUSE_HALO, tn), F32)] * 2,
        compiler_params=_params(("parallel", "arbitrary")),
    )(h, w_up, w_up, cw, cw, cb, cb)


def _ffn_down_dx_fused(dxb, w_down, up, cv, cw, after, name):
    s, d = dxb.shape
    f = w_down.shape[0]
    tm = _pick(s, 1024, LANES)
    tn = _pick(f, 512, FUSE_STRIP)
    nj, ni = f // tn, s // tm
    n_steps = nj * ni
    sub = tm // FUSE_HALO

    def body(dx_ref, w_ref, upg_ref, upa_ref, cvg_ref, cva_ref, cwg_ref, cwa_ref, after_ref,
             dup_hbm, dcwg_ref, dcwa_ref, dcbg_ref, dcba_ref, out_buf, out_sem, carry, acc):
        j, i = pl.program_id(0), pl.program_id(1)
        step = j * ni + i
        slot = lax.rem(step, 2)
        row0 = pl.multiple_of((ni - 1 - i) * tm, tm)

        def out_copy(half):
            col0 = pl.multiple_of(half * f + j * tn, LANES)
            return pltpu.make_async_copy(out_buf.at[slot, half], dup_hbm.at[pl.ds(row0, tm), pl.ds(col0, tn)],
                                         out_sem.at[slot, half])

        @pl.when(step >= 2)
        def _():
            for half in range(2):
                out_copy(half).wait()

        @pl.when(i == 0)
        def _():
            carry[...] = jnp.zeros_like(carry)
            acc[...] = jnp.zeros_like(acc)

        for st in range(tn // FUSE_STRIP):
            cols = slice(st * FUSE_STRIP, (st + 1) * FUSE_STRIP)
            dact = lax.dot_general(dx_ref[...], w_ref[cols, :], _DIMS["nt"], preferred_element_type=F32)
            gcv = cvg_ref[:, cols].astype(F32)
            acv = cva_ref[:, cols].astype(F32)
            sg = _sigmoid(gcv)
            dd = (dact * acv * (sg * (1.0 + gcv * (1.0 - sg))), dact * (gcv * sg))
            for half, (up_ref, cw_ref) in enumerate(((upg_ref, cwg_ref), (upa_ref, cwa_ref))):
                x = up_ref[:, cols].astype(F32)
                d0 = dd[half]
                ext = jnp.concatenate([d0, carry[half, :, cols]], axis=0)
                d1 = pltpu.roll(ext, tm + FUSE_HALO - 1, 0)[:tm, :]
                d2 = pltpu.roll(ext, tm + FUSE_HALO - 2, 0)[:tm, :]
                carry[half, :, cols] = d0[:FUSE_HALO, :]
                out_buf[slot, half, :, cols] = (cw_ref[2:3, cols] * d0 + cw_ref[1:2, cols] * d1
                                                + cw_ref[0:1, cols] * d2).astype(BF16)
                for k, term in enumerate((d2 * x, d1 * x, d0 * x, d0)):
                    acc[half, k, :, cols] += jnp.sum(term.reshape(sub, FUSE_HALO, FUSE_STRIP), axis=0)

        for half in range(2):
            out_copy(half).start()

        @pl.when(i == ni - 1)
        def _():
            for half, (dcw_ref, dcb_ref) in enumerate(((dcwg_ref, dcbg_ref), (dcwa_ref, dcba_ref))):
                for k in range(3):
                    dcw_ref[k:k + 1, :] = jnp.sum(acc[half, k], axis=0, keepdims=True)
                dcb_ref[...] = jnp.sum(acc[half, 3], axis=0, keepdims=True)

        @pl.when(step == n_steps - 1)
        def _():
            for half in range(2):
                out_copy(half).wait()
                if n_steps > 1:
                    pltpu.make_async_copy(out_buf.at[1 - slot, half], dup_hbm.at[pl.ds(row0, tm), pl.ds(0, tn)],
                                          out_sem.at[1 - slot, half]).wait()

    tile = pl.BlockSpec((tm, tn), lambda j, i: (ni - 1 - i, j))

    def cols_of(rows, offset):
        return pl.BlockSpec((rows, tn), lambda j, i: (0, j + offset))

    small = pl.BlockSpec((3, tn), lambda j, i: (0, j)), pl.BlockSpec((1, tn), lambda j, i: (0, j))
    return pl.pallas_call(
        body, name=name, grid=(nj, ni),
        in_specs=[pl.BlockSpec((tm, d), lambda j, i: (ni - 1 - i, 0)), pl.BlockSpec((tn, d), lambda j, i: (j, 0)),
                  tile, tile, tile, tile, cols_of(3, 0), cols_of(3, nj), pl.BlockSpec(memory_space=pl.ANY)],
        out_specs=[pl.BlockSpec(memory_space=pl.ANY), small[0], small[0], small[1], small[1]],
        out_shape=[jax.ShapeDtypeStruct((s, 2 * f), BF16), jax.ShapeDtypeStruct((3, f), F32),
                   jax.ShapeDtypeStruct((3, f), F32), jax.ShapeDtypeStruct((1, f), F32),
                   jax.ShapeDtypeStruct((1, f), F32)],
        scratch_shapes=[pltpu.VMEM((2, 2, tm, tn), BF16), pltpu.SemaphoreType.DMA((2, 2)),
                        pltpu.VMEM((2, FUSE_HALO, tn), F32), pltpu.VMEM((2, 4, FUSE_HALO, tn), F32)],
        compiler_params=_params(("arbitrary", "arbitrary")),
    )(dxb, w_down, up[0], up[1], cv[0], cv[1], cw, cw, after)


_GELU_C = math.sqrt(2.0 / math.pi)


def _gelu(x):
    th = jnp.tanh(_GELU_C * (x + 0.044715 * (x * x * x)))
    return x * (0.5 * (1.0 + th)), th


def _gelu_grad(x, th):
    return 0.5 * (1.0 + th) + 0.5 * x * (1.0 - th * th) * (_GELU_C * (1.0 + 3.0 * 0.044715 * (x * x)))


def _masked_ws(ws_ref, h):
    t = lax.broadcasted_iota(jnp.int32, (CHUNK, CHUNK), 0)
    sx = lax.broadcasted_iota(jnp.int32, (CHUNK, CHUNK), 1)
    return jnp.where(sx <= t, ws_ref[h], 0.0)


def _mixb_fwd(pre, gv, ws, bs_wide, name):
    s, w2 = pre.shape
    w = w2 // 2
    gw = w // SG_GROUPS

    def body(pre_ref, gv_ref, ws_ref, bs_ref, o_ref):
        zu, _ = _gelu(pre_ref[:, :w].astype(F32))
        zv, _ = _gelu(pre_ref[:, w:].astype(F32))
        _, vhat = _rms_stats(zv)
        vn = (vhat * gv_ref[...]).astype(BF16)
        for h in range(SG_GROUPS):
            cols = slice(h * gw, (h + 1) * gw)
            wsm = _masked_ws(ws_ref, h).astype(BF16)
            gate = jnp.dot(wsm, vn[:, cols], preferred_element_type=F32)
            gate = gate + jnp.tile(bs_ref[h], (1, gw // LANES))
            o_ref[:, cols] = (zu[:, cols] * gate).astype(BF16)

    return pl.pallas_call(
        body, name=name, grid=(s // CHUNK,),
        in_specs=[pl.BlockSpec((CHUNK, w2), lambda i: (i, 0)), pl.BlockSpec((1, w), lambda i: (0, 0)),
                  pl.BlockSpec((SG_GROUPS, CHUNK, CHUNK), lambda i: (0, 0, 0)),
                  pl.BlockSpec((SG_GROUPS, CHUNK, LANES), lambda i: (0, 0, 0))],
        out_specs=pl.BlockSpec((CHUNK, w), lambda i: (i, 0)),
        out_shape=jax.ShapeDtypeStruct((s, w), BF16),
        compiler_params=_params(("parallel",)),
    )(pre, gv, ws, bs_wide)


def _mixb_bwd(pre, dug, gv, ws, bs_wide, name):
    s, w2 = pre.shape
    w = w2 // 2
    gw = w // SG_GROUPS

    def body(pre_ref, dug_ref, gv_ref, ws_ref, bs_ref, o_ref, dws_ref, dbs_ref, dgv_ref, dvn_ref):
        first = pl.program_id(0) == 0

        @pl.when(first)
        def _():
            dws_ref[...] = jnp.zeros_like(dws_ref)
            dbs_ref[...] = jnp.zeros_like(dbs_ref)

        pu = pre_ref[:, :w].astype(F32)
        pv = pre_ref[:, w:].astype(F32)
        zu, thu = _gelu(pu)
        zv, thv = _gelu(pv)
        inv, vhat = _rms_stats(zv)
        gvv = gv_ref[...]
        vn = (vhat * gvv).astype(BF16)
        for h in range(SG_GROUPS):
            cols = slice(h * gw, (h + 1) * gw)
            wsm = _masked_ws(ws_ref, h).astype(BF16)
            gate = jnp.dot(wsm, vn[:, cols], preferred_element_type=F32)
            gate = gate + jnp.tile(bs_ref[h], (1, gw // LANES))
            dug_h = dug_ref[:, cols].astype(F32)
            dgate = dug_h * zu[:, cols]
            dgate_b = dgate.astype(BF16)
            o_ref[:, cols] = (dug_h * gate * _gelu_grad(pu[:, cols], thu[:, cols])).astype(BF16)
            dbs_ref[h] += jnp.broadcast_to(jnp.sum(dgate, axis=-1, keepdims=True), (CHUNK, LANES))
            dws = lax.dot_general(dgate_b, vn[:, cols], _DIMS["nt"], preferred_element_type=F32)
            t = lax.broadcasted_iota(jnp.int32, (CHUNK, CHUNK), 0)
            sx = lax.broadcasted_iota(jnp.int32, (CHUNK, CHUNK), 1)
            dws_ref[h] += jnp.where(sx <= t, dws, 0.0)
            dvn_ref[:, cols] = lax.dot_general(wsm, dgate_b, _DIMS["tn"], preferred_element_type=F32)
        dvn = dvn_ref[...]
        part = jnp.sum(dvn * vhat, axis=0, keepdims=True)

        @pl.when(first)
        def _():
            dgv_ref[...] = part

        @pl.when(jnp.logical_not(first))
        def _():
            dgv_ref[...] += part

        dvhat = dvn * gvv
        dzv = inv * (dvhat - vhat * jnp.mean(dvhat * vhat, axis=-1, keepdims=True))
        o_ref[:, w:] = (dzv * _gelu_grad(pv, thv)).astype(BF16)

    return pl.pallas_call(
        body, name=name, grid=(s // CHUNK,),
        in_specs=[pl.BlockSpec((CHUNK, w2), lambda i: (i, 0)), pl.BlockSpec((CHUNK, w), lambda i: (i, 0)),
                  pl.BlockSpec((1, w), lambda i: (0, 0)),
                  pl.BlockSpec((SG_GROUPS, CHUNK, CHUNK), lambda i: (0, 0, 0)),
                  pl.BlockSpec((SG_GROUPS, CHUNK, LANES), lambda i: (0, 0, 0))],
        out_specs=[pl.BlockSpec((CHUNK, w2), lambda i: (i, 0)),
                   pl.BlockSpec((SG_GROUPS, CHUNK, CHUNK), lambda i: (0, 0, 0)),
                   pl.BlockSpec((SG_GROUPS, CHUNK, LANES), lambda i: (0, 0, 0)),
                   pl.BlockSpec((1, w), lambda i: (0, 0))],
        out_shape=[jax.ShapeDtypeStruct((s, w2), BF16), jax.ShapeDtypeStruct((SG_GROUPS, CHUNK, CHUNK), F32),
                   jax.ShapeDtypeStruct((SG_GROUPS, CHUNK, LANES), F32), jax.ShapeDtypeStruct((1, w), F32)],
        scratch_shapes=[pltpu.VMEM((CHUNK, w), F32)],
        compiler_params=_params(("arbitrary",)),
    )(pre, dug, gv, ws, bs_wide)


def _cast_layer(w3, layer, name):
    _, r, c = w3.shape
    tr = _pick(r, 256, SLAB)

    def body(w_ref, o_ref):
        o_ref[...] = w_ref[...].astype(BF16)

    return pl.pallas_call(
        body, name=name, grid=(r // tr,),
        in_specs=[pl.BlockSpec((None, tr, c), lambda i: (layer, i, 0))],
        out_specs=pl.BlockSpec((tr, c), lambda i: (i, 0)),
        out_shape=jax.ShapeDtypeStruct((r, c), BF16),
        compiler_params=_params(("parallel",)),
    )(w3)


def _adamw_math(w, g, m, v):
    m = ADAM_B1 * m + (1.0 - ADAM_B1) * g
    v = ADAM_B2 * v + (1.0 - ADAM_B2) * (g * g)
    m_hat = m / (1.0 - ADAM_B1 ** ADAM_STEP)
    v_hat = v / (1.0 - ADAM_B2 ** ADAM_STEP)
    delta = -ADAM_LR * (m_hat / (jnp.sqrt(v_hat) + ADAM_EPS) + ADAM_WD * w)
    return delta, m, v


def _adamw_sharded(recvs, w, m, v, name):
    nl, r, c = w.shape
    tc = _pick(c, 1536, LANES)
    tr = _pick(r, 64, SLAB)

    def body(*refs):
        recv_refs = refs[:nl]
        w_ref, m_ref, v_ref, g_ref, d_ref, nm_ref, nv_ref = refs[nl:]
        for layer, recv_ref in enumerate(recv_refs):
            @pl.when(pl.program_id(0) == layer)
            def _():
                g = recv_ref[0].astype(F32)
                for q in range(1, N_DEV):
                    g = g + recv_ref[q].astype(F32)
                delta, nm, nv = _adamw_math(w_ref[...], g, m_ref[...], v_ref[...])
                g_ref[...] = g
                d_ref[...] = delta
                nm_ref[...] = nm
                nv_ref[...] = nv

    def recv_spec(layer):
        return pl.BlockSpec((N_DEV, tr, tc),
                            lambda l, i, j: (0, jnp.where(l == layer, i, 0), jnp.where(l == layer, j, 0)))

    blk = pl.BlockSpec((None, tr, tc), lambda l, i, j: (l, i, j))
    out = jax.ShapeDtypeStruct((nl, r, c), F32)
    return pl.pallas_call(
        body, name=name, grid=(nl, r // tr, c // tc),
        in_specs=[recv_spec(layer) for layer in range(nl)] + [blk, blk, blk],
        out_specs=[blk] * 4, out_shape=[out] * 4,
        compiler_params=_params(("parallel",) * 3),
    )(*recvs, w, m, v)


def _adamw_packed(w, g, m, v, name):
    r, c = w.shape
    tr = _pick(r, 256, 8)

    def body(w_ref, g_ref, m_ref, v_ref, d_ref, nm_ref, nv_ref):
        delta, nm, nv = _adamw_math(w_ref[...], g_ref[...], m_ref[...], v_ref[...])
        d_ref[...] = delta
        nm_ref[...] = nm
        nv_ref[...] = nv

    blk = pl.BlockSpec((tr, c), lambda i: (i, 0))
    out = jax.ShapeDtypeStruct((r, c), F32)
    return pl.pallas_call(
        body, name=name, grid=(r // tr,), in_specs=[blk] * 4, out_specs=[blk] * 3, out_shape=[out] * 3,
        compiler_params=_params(("parallel",)),
    )(w, g, m, v)


def _pack(arrays):
    parts = []
    for a in arrays:
        flat = a.reshape(-1).astype(F32)
        pad = (-flat.shape[0]) % PACK_GRANULE
        parts.append(jnp.pad(flat, (0, pad)) if pad else flat)
    return jnp.concatenate(parts).reshape(-1, LANES)


def _unpack(buf, shapes):
    flat = buf.reshape(-1)
    out, off = [], 0
    for shp in shapes:
        n = math.prod(shp)
        out.append(flat[off:off + n].reshape(shp))
        off += n + (-n) % PACK_GRANULE
    return out


def _mesh_pos():
    return lax.axis_index("x"), lax.axis_index("y"), lax.axis_index("c")


def _coords(q):
    return q // 4, (q // 2) % 2, q % 2


def _shard_of(ref, q, shard_shape, axis):
    r, c = shard_shape
    if axis == 0:
        return ref.at[pl.ds(pl.multiple_of(q * r, SLAB), r), :]
    return ref.at[:, pl.ds(pl.multiple_of(q * c, LANES), c)]


_HBM = pl.BlockSpec(memory_space=pltpu.HBM)
_SEM = pl.BlockSpec(memory_space=pltpu.SEMAPHORE)
_EFFECT = pltpu.SideEffectType.DATAFLOW_SIDE_EFFECTING


def _exchange_shapes(gather, src_shape, axis):
    r, c = src_shape
    if gather:
        return (r, c), ((r * N_DEV, c) if axis == 0 else (r, c * N_DEV))
    shard = (r // N_DEV, c) if axis == 0 else (r, c // N_DEV)
    return shard, (N_DEV,) + shard


def _exchange_copies(gather, src, land, sems, axis):
    send_sems, recv_sems, own_sem = sems
    x, y, c_ = _mesh_pos()
    me = 4 * x + 2 * y + c_
    shard, _ = _exchange_shapes(gather, src.shape, axis)

    def piece(q):
        return src if gather else _shard_of(src, q, shard, axis)

    def place(q):
        return _shard_of(land, q, shard, axis) if gather else land.at[q]

    own = pltpu.make_async_copy(piece(me), place(me), own_sem.at[0])
    sends, arrivals = [], []
    for step in range(1, N_DEV):
        to = (me + step) % N_DEV
        frm = (me + N_DEV - step) % N_DEV
        sends.append(pltpu.make_async_remote_copy(
            src_ref=piece(to), dst_ref=place(me), send_sem=send_sems.at[step - 1], recv_sem=recv_sems.at[step - 1],
            device_id=_coords(to), device_id_type=MESH))
        arrivals.append(pltpu.make_async_remote_copy(
            src_ref=piece(me), dst_ref=place(frm), send_sem=send_sems.at[step - 1], recv_sem=recv_sems.at[step - 1],
            device_id=_coords(frm), device_id_type=MESH))
    return own, sends, arrivals


def _exchange_start(gather, src, axis, name, after=None):
    _, land_shape = _exchange_shapes(gather, src.shape, axis)
    extra = () if after is None else (after,)

    def body(*refs):
        src_ref, land = refs[:2]
        send_sems, recv_sems, own_sem = refs[2 + len(extra):5 + len(extra)]
        own, sends, _ = _exchange_copies(gather, src_ref, land, (send_sems, recv_sems, own_sem), axis)
        own.start()
        for cp in sends:
            cp.start()
        refs[-1][...] = jnp.zeros_like(refs[-1])

    out = pl.pallas_call(
        body, name=name,
        out_shape=(pltpu.SemaphoreType.DMA((N_DEV - 1,)), pltpu.SemaphoreType.DMA((N_DEV - 1,)),
                   pltpu.SemaphoreType.DMA((1,)), pltpu.HBM(src.shape, src.dtype),
                   pltpu.HBM(land_shape, src.dtype), jax.ShapeDtypeStruct((8, LANES), F32)),
        in_specs=[_HBM, _HBM] + [pl.BlockSpec(memory_space=pl.ANY)] * len(extra),
        out_specs=(_SEM, _SEM, _SEM, _HBM, _HBM, pl.BlockSpec(memory_space=pltpu.VMEM)),
        input_output_aliases={0: 3, 1: 4},
        compiler_params=pltpu.CompilerParams(has_side_effects=_EFFECT),
    )(pltpu.with_memory_space_constraint(src, pltpu.HBM),
      pltpu.with_memory_space_constraint(lax.empty(land_shape, src.dtype), pltpu.HBM), *extra)
    return out[:5], out[5]


def _exchange_wait(gather, state, axis, after, name):
    send_sems, recv_sems, own_sem, src_thru, land_thru = state
    after = tuple(after) if isinstance(after, (tuple, list)) else (after,)

    def body(src, land, send_sems, recv_sems, own_sem, *rest):
        own, sends, arrivals = _exchange_copies(gather, src, land, (send_sems, recv_sems, own_sem), axis)
        for cp in sends:
            cp.wait_send()
        for cp in arrivals:
            cp.wait_recv()
        own.wait()

    return pl.pallas_call(
        body, name=name,
        out_shape=(pltpu.HBM(src_thru.shape, src_thru.dtype), pltpu.HBM(land_thru.shape, land_thru.dtype)),
        in_specs=[_HBM, _HBM, _SEM, _SEM, _SEM] + [pl.BlockSpec(memory_space=pl.ANY)] * len(after),
        out_specs=(_HBM, _HBM),
        input_output_aliases={0: 0, 1: 1},
        compiler_params=pltpu.CompilerParams(has_side_effects=_EFFECT),
    )(src_thru, land_thru, send_sems, recv_sems, own_sem, *after)[1]


def _gather2_copies(shard_ref, land, sems, axis, shard_shape):
    send1, recv1, own_sem, send2, recv2 = sems
    x, y, c = _mesh_pos()
    me, sibling = (x, y, c), (x, y, 1 - c)
    chips = [(1 - x, y), (x, 1 - y), (1 - x, 1 - y)]

    def region(dev):
        px, py, pc = dev
        return _shard_of(land, 4 * px + 2 * py + pc, shard_shape, axis)

    def copy(src, block, to, send, recv):
        return pltpu.make_async_remote_copy(src_ref=src, dst_ref=region(block), send_sem=send, recv_sem=recv,
                                            device_id=to, device_id_type=MESH)

    own = pltpu.make_async_copy(shard_ref, region(me), own_sem.at[0])
    peers = [sibling] + [(*chip, c) for chip in chips]
    sends1 = [copy(shard_ref, me, to, send1.at[k], recv1.at[k]) for k, to in enumerate(peers)]
    arrivals1 = [copy(shard_ref, frm, frm, send1.at[k], recv1.at[k]) for k, frm in enumerate(peers)]
    sends2, arrivals2 = [], []
    if send2 is not None:
        for j, chip in enumerate(chips):
            sends2.append(copy(region((*chip, c)), (*chip, c), sibling, send2.at[j], recv2.at[j]))
            arrivals2.append(copy(region((*chip, 1 - c)), (*chip, 1 - c), sibling, send2.at[j], recv2.at[j]))
    return own, sends1, arrivals1, sends2, arrivals2


def _gather2_start(shard, axis, name, after=None):
    _, land_shape = _exchange_shapes(True, shard.shape, axis)
    extra = () if after is None else (after,)

    def body(*refs):
        src_ref, land = refs[:2]
        send1, recv1, own_sem = refs[2 + len(extra):5 + len(extra)]
        own, sends1, _, _, _ = _gather2_copies(src_ref, land, (send1, recv1, own_sem, None, None), axis, shard.shape)
        own.start()
        for cp in sends1[1:] + sends1[:1]:
            cp.start()
        refs[-1][...] = jnp.zeros_like(refs[-1])

    out = pl.pallas_call(
        body, name=name,
        out_shape=(pltpu.SemaphoreType.DMA((4,)), pltpu.SemaphoreType.DMA((4,)), pltpu.SemaphoreType.DMA((1,)),
                   pltpu.HBM(shard.shape, shard.dtype), pltpu.HBM(land_shape, shard.dtype),
                   jax.ShapeDtypeStruct((8, LANES), F32)),
        in_specs=[_HBM, _HBM] + [pl.BlockSpec(memory_space=pl.ANY)] * len(extra),
        out_specs=(_SEM, _SEM, _SEM, _HBM, _HBM, pl.BlockSpec(memory_space=pltpu.VMEM)),
        input_output_aliases={0: 3, 1: 4},
        compiler_params=pltpu.CompilerParams(has_side_effects=_EFFECT),
    )(pltpu.with_memory_space_constraint(shard, pltpu.HBM),
      pltpu.with_memory_space_constraint(lax.empty(land_shape, shard.dtype), pltpu.HBM), *extra)
    return out[:5], out[5]


def _gather2_pass(state, axis, after, name):
    send1, recv1, own_sem, shard_thru, land_thru = state

    def body(src_ref, land, send1, recv1, own_sem, after_ref, send2, recv2, src_out, land_out, token):
        _, _, arrivals1, sends2, _ = _gather2_copies(src_ref, land, (send1, recv1, own_sem, send2, recv2), axis,
                                                     shard_thru.shape)
        for arrival, fwd in zip(arrivals1[1:], sends2):
            arrival.wait_recv()
            fwd.start()
        token[...] = jnp.zeros_like(token)

    out = pl.pallas_call(
        body, name=name,
        out_shape=(pltpu.SemaphoreType.DMA((3,)), pltpu.SemaphoreType.DMA((3,)),
                   pltpu.HBM(shard_thru.shape, shard_thru.dtype), pltpu.HBM(land_thru.shape, land_thru.dtype),
                   jax.ShapeDtypeStruct((8, LANES), F32)),
        in_specs=[_HBM, _HBM, _SEM, _SEM, _SEM, pl.BlockSpec(memory_space=pl.ANY)],
        out_specs=(_SEM, _SEM, _HBM, _HBM, pl.BlockSpec(memory_space=pltpu.VMEM)),
        input_output_aliases={0: 2, 1: 3},
        compiler_params=pltpu.CompilerParams(has_side_effects=_EFFECT),
    )(shard_thru, land_thru, send1, recv1, own_sem, after)
    return (send1, recv1, own_sem, out[0], out[1], out[2], out[3]), out[4]


def _gather2_wait(state, axis, after, name):
    send1, recv1, own_sem, send2, recv2, shard_thru, land_thru = state

    def body(src_ref, land, send1, recv1, own_sem, send2, recv2, after_ref, src_dead, got):
        own, sends1, arrivals1, sends2, arrivals2 = _gather2_copies(
            src_ref, land, (send1, recv1, own_sem, send2, recv2), axis, shard_thru.shape)
        for cp in sends1 + sends2:
            cp.wait_send()
        for cp in arrivals1[:1] + arrivals2:
            cp.wait_recv()
        own.wait()

    return pl.pallas_call(
        body, name=name,
        out_shape=(pltpu.HBM(shard_thru.shape, shard_thru.dtype), pltpu.HBM(land_thru.shape, land_thru.dtype)),
        in_specs=[_HBM, _HBM] + [_SEM] * 5 + [pl.BlockSpec(memory_space=pl.ANY)],
        out_specs=(_HBM, _HBM),
        input_output_aliases={0: 0, 1: 1},
        compiler_params=pltpu.CompilerParams(has_side_effects=_EFFECT),
    )(shard_thru, land_thru, send1, recv1, own_sem, send2, recv2, after)[1]


def _sum_slots(slots, name):
    _, r, c = slots.shape
    tr = _pick(r, 512, 8)

    def body(s_ref, o_ref):
        total = s_ref[0]
        for q in range(1, N_DEV):
            total = total + s_ref[q]
        o_ref[...] = total

    return pl.pallas_call(
        body, name=name, grid=(r // tr,),
        in_specs=[pl.BlockSpec((N_DEV, tr, c), lambda i: (0, i, 0))],
        out_specs=pl.BlockSpec((tr, c), lambda i: (i, 0)),
        out_shape=jax.ShapeDtypeStruct((r, c), F32),
        compiler_params=_params(("parallel",)),
    )(slots)


def kernel(x, a_norm, a_in, a_conv, a_out, b_norm, b_in, b_vnorm, b_ws, b_bs, b_out, f_norm, f_up, f_conv_w, f_conv_b, f_down, final_norm, loss_target, m_a_norm, m_a_in, m_a_conv, m_a_out, m_b_norm, m_b_in, m_b_vnorm, m_b_ws, m_b_bs, m_b_out, m_f_norm, m_f_up, m_f_conv_w, m_f_conv_b, m_f_down, m_final_norm, v_a_norm, v_a_in, v_a_conv, v_a_out, v_b_norm, v_b_in, v_b_vnorm, v_b_ws, v_b_bs, v_b_out, v_f_norm, v_f_up, v_f_conv_w, v_f_conv_b, v_f_down, v_final_norm):
    s, d = x.shape[1], x.shape[2]
    n_ffn = f_up.shape[0]
    f2 = f_up.shape[2] * N_DEV
    me = 4 * lax.axis_index("x") + 2 * lax.axis_index("y") + lax.axis_index("c")
    x0 = x.reshape(s, d)
    target = loss_target.reshape(s, d)

    wanted = [("a_in", _cast_layer(a_in, 0, "cast_a_in"), 1),
              ("small", _pack([a_conv, b_norm, b_vnorm, f_conv_w]), 0),
              ("a_out", _cast_layer(a_out, 0, "cast_a_out"), 0),
              ("f_up0", _cast_layer(f_up, 0, "cast_f_up0"), 1), ("f_down0", _cast_layer(f_down, 0, "cast_f_down0"), 0),
              ("b_in", _cast_layer(b_in, 0, "cast_b_in"), 1), ("b_out", _cast_layer(b_out, 0, "cast_b_out"), 0),
              ("f_up1", _cast_layer(f_up, 1, "cast_f_up1"), 1), ("f_down1", _cast_layer(f_down, 1, "cast_f_down1"), 0)]
    coming, tok, h0 = {}, None, None
    for n_started, (key, shard, axis) in enumerate(wanted):
        if n_started == 2:
            tok = h0 = _rmsnorm_fwd(x0, a_norm, "mixa_norm", after=tok)
        state, tok = _gather2_start(shard, axis, f"ag_start_{key}", after=tok)
        coming[key] = (state, axis)

    def pass_on(keys, after):
        for key in keys:
            state, axis = coming[key]
            state, after = _gather2_pass(state, axis, after, f"ag_pass_{key}")
            coming[key] = (state, axis)
        return after

    def arrived(key, after):
        state, axis = coming[key]
        return _gather2_wait(state, axis, after, f"ag_wait_{key}")

    cshard = a_conv.shape[2]
    fshard = f_conv_w.shape[2]
    w_a_in = arrived("a_in", pass_on(["a_in", "small"], tok))
    small_full = arrived("small", w_a_in)
    small_rows = small_full.reshape(N_DEV, -1)
    per_dev = _unpack_rows(small_rows, [(3, cshard), (cshard,), (cshard,), (n_ffn, 3, fshard)])
    a_conv_full = per_dev[0].transpose(1, 0, 2).reshape(3, d)
    b_norm_full = per_dev[1].reshape(1, d)
    b_vnorm_full = per_dev[2].reshape(1, d)
    f_conv_w_full = per_dev[3].transpose(1, 2, 0, 3).reshape(n_ffn, 3, f2)
    bs_wide = jnp.broadcast_to(b_bs[0][:, :, None], (SG_GROUPS, CHUNK, LANES))
    ws = b_ws[0]

    w_f_up, w_f_down = {}, {}

    def ffn_forward(xin, l, pass_first, pass_early, pass_late):
        h = _rmsnorm_fwd(xin, f_norm[l:l + 1], f"ffn{l}_norm", after=pass_on(pass_first, xin))
        w_f_up[l] = arrived(f"f_up{l}", pass_on(pass_early, h))
        up_g, up_a, cv_g, cv_a, act = _ffn_up_fused(h, w_f_up[l], f_conv_w_full[l], f_conv_b[l:l + 1], f"ffn{l}_up")
        up, cv = (up_g, up_a), (cv_g, cv_a)
        w_f_down[l] = arrived(f"f_down{l}", pass_on(pass_late, act))
        xout = _matmul(act, w_f_down[l], "nn", F32, f"ffn{l}_down", resid=xin, tm_cap=512)
        return xout, (h, up, act, cv)

    gb, gc, xs, cva, ya = _mixa_in_fused(h0, w_a_in, a_conv_full, "mixa_in")
    bcx = (gb, gc, xs)
    w_a_out = arrived("a_out", pass_on(["a_out"], ya))
    x1 = _matmul(ya, w_a_out, "nn", F32, "mixa_out", resid=x0)
    x2, saved0 = ffn_forward(x1, 0, ["f_up0"], ["f_down0"], ["b_in", "b_out", "f_up1", "f_down1"])
    h2 = _rmsnorm_fwd(x2, b_norm_full, "mixb_norm")
    w_b_in = arrived("b_in", h2)
    pre = _matmul(h2, w_b_in, "nn", BF16, "mixb_in")
    ug = _mixb_fwd(pre, b_vnorm_full, ws, bs_wide, "mixb_mid")
    w_b_out = arrived("b_out", ug)
    x3 = _matmul(ug, w_b_out, "nn", F32, "mixb_out", resid=x2)
    x4, saved1 = ffn_forward(x3, 1, [], [], [])
    dx4, dx4b, loss_part, g_final = _final_loss(x4, final_norm.reshape(1, d), target, "loss_head")

    def _rs_start(grad, axis, name):
        return _exchange_start(False, grad, axis, name)

    def ffn_backward(xin, l, saved, dx, dxb):
        h, up, act, cv = saved
        g_down = _matmul(act, dxb, "tn", BF16, f"ffn{l}_down_dw", tm_cap=1408)
        rs_down, tok = _rs_start(g_down, 0, f"rs_start_f_down{l}")
        dup, cwg, cwa, cbg, cba = _ffn_down_dx_fused(dxb, w_f_down[l], up, cv, f_conv_w_full[l], tok,
                                                     f"ffn{l}_down_dx")
        g_cw, g_cb = jnp.concatenate([cwg, cwa], axis=1), jnp.concatenate([cbg, cba], axis=1)
        g_up = _matmul(h, dup, "tn", BF16, f"ffn{l}_up_dw", tk_cap=4096)
        rs_up, tok = _rs_start(g_up, 1, f"rs_start_f_up{l}")
        dh = _matmul(dup, w_f_up[l], "nt", BF16, f"ffn{l}_up_dx", after=tok)
        dxin, dxinb, g_norm = _rmsnorm_bwd(xin, f_norm[l:l + 1], dh, dx, f"ffn{l}_norm_bwd")
        return dxin, dxinb, (rs_up, rs_down, g_cw, g_cb, g_norm)

    dx3, dx3b, gf1 = ffn_backward(x3, 1, saved1, dx4, dx4b)
    g_b_out = _matmul(ug, dx3b, "tn", BF16, "mixb_out_dw", tk_cap=4096)
    rs_b_out, tok = _rs_start(g_b_out, 0, "rs_start_b_out")
    dug = _matmul(dx3b, w_b_out, "nt", BF16, "mixb_out_dx", after=tok)
    dpre, g_ws, g_bs_wide, g_bvnorm = _mixb_bwd(pre, dug, b_vnorm_full, ws, bs_wide, "mixb_mid_bwd")
    g_b_in = _matmul(h2, dpre, "tn", BF16, "mixb_in_dw", tk_cap=4096)
    rs_b_in, tok = _rs_start(g_b_in, 1, "rs_start_b_in")
    dh2 = _matmul(dpre, w_b_in, "nt", BF16, "mixb_in_dx", after=tok)
    dx2, dx2b, g_bnorm = _rmsnorm_bwd(x2, b_norm_full, dh2, dx3, "mixb_norm_bwd")
    dx1, dx1b, gf0 = ffn_backward(x1, 0, saved0, dx2, dx2b)
    g_a_out = _matmul(ya, dx1b, "tn", BF16, "mixa_out_dw", tk_cap=4096)
    rs_a_out, tok = _rs_start(g_a_out, 0, "rs_start_a_out")
    dbcx, g_aconv = _mixa_out_dx_fused(dx1b, w_a_out, bcx, cva, a_conv_full, tok, "mixa_out_dx")
    g_a_in = _matmul(h0, dbcx, "tn", BF16, "mixa_in_dw", tk_cap=4096)
    rs_a_in, tok = _rs_start(g_a_in, 1, "rs_start_a_in")
    dh0 = _matmul(dbcx, w_a_in, "nt", BF16, "mixa_in_dx", after=tok)
    grad_x, _, g_anorm = _rmsnorm_bwd(x0, a_norm, dh0, dx1, "mixa_norm_bwd")

    full_shapes = [(1, LANES), (1, d), (3, d), (1, d), (1, d), (SG_GROUPS, CHUNK, CHUNK), (SG_GROUPS, CHUNK),
                   (n_ffn, d), (n_ffn, 3, f2), (n_ffn, f2), (1, d)]
    parts = [loss_part, g_anorm, g_aconv, g_bnorm, g_bvnorm, g_ws, g_bs_wide[:, :, 0],
             jnp.concatenate([gf0[4], gf1[4]], axis=0), jnp.stack([gf0[2], gf1[2]]),
             jnp.concatenate([gf0[3], gf1[3]], axis=0), g_final]
    small_part = _pack(parts)
    small_state, small_tok = _exchange_start(True, small_part, 0, "ar_start_small", after=grad_x)

    big = {}
    for name, states, axis, w, m, v in (
            ("f_down", (gf0[1], gf1[1]), 0, f_down, m_f_down, v_f_down),
            ("f_up", (gf0[0], gf1[0]), 1, f_up, m_f_up, v_f_up),
            ("b_out", (rs_b_out,), 0, b_out, m_b_out, v_b_out), ("b_in", (rs_b_in,), 1, b_in, m_b_in, v_b_in),
            ("a_out", (rs_a_out,), 0, a_out, m_a_out, v_a_out), ("a_in", (rs_a_in,), 1, a_in, m_a_in, v_a_in)):
        recvs = [_exchange_wait(False, st, axis, small_tok, f"rs_wait_{name}{l}") for l, st in enumerate(states)]
        big[name] = _adamw_sharded(recvs, w, m, v, f"adamw_{name}")

    slots = _exchange_wait(True, small_state, 0, [res[0] for res in big.values()], "ar_wait_small")
    total = _sum_slots(slots.reshape((N_DEV,) + small_part.shape), "ar_sum_small")
    (loss_v, r_anorm, r_aconv, r_bnorm, r_bvnorm, r_ws, r_bs, r_fnorm, r_fcw, r_fcb, r_final) = _unpack(total, full_shapes)
    small_grads = [
        r_anorm,
        lax.dynamic_slice_in_dim(r_aconv, me * cshard, cshard, axis=1).reshape(a_conv.shape),
        lax.dynamic_slice_in_dim(r_bnorm, me * cshard, cshard, axis=1),
        lax.dynamic_slice_in_dim(r_bvnorm, me * cshard, cshard, axis=1),
        r_ws.reshape(b_ws.shape), r_bs.reshape(b_bs.shape), r_fnorm,
        lax.dynamic_slice_in_dim(r_fcw, me * fshard, fshard, axis=2),
        r_fcb, r_final.reshape(final_norm.shape)]
    small_w = [a_norm, a_conv, b_norm, b_vnorm, b_ws, b_bs, f_norm, f_conv_w, f_conv_b, final_norm]
    small_m = [m_a_norm, m_a_conv, m_b_norm, m_b_vnorm, m_b_ws, m_b_bs, m_f_norm, m_f_conv_w, m_f_conv_b, m_final_norm]
    small_v = [v_a_norm, v_a_conv, v_b_norm, v_b_vnorm, v_b_ws, v_b_bs, v_f_norm, v_f_conv_w, v_f_conv_b, v_final_norm]
    shapes = [w.shape for w in small_w]
    packed = _adamw_packed(_pack(small_w), _pack(small_grads), _pack(small_m), _pack(small_v), "adamw_small")
    s_delta, s_m, s_v = (_unpack(p, shapes) for p in packed)
    small_names = ["a_norm", "a_conv", "b_norm", "b_vnorm", "b_ws", "b_bs", "f_norm", "f_conv_w", "f_conv_b", "final_norm"]
    small = {nm: (small_grads[i], s_delta[i], s_m[i], s_v[i]) for i, nm in enumerate(small_names)}

    order = ["a_norm", "a_in", "a_conv", "a_out", "b_norm", "b_in", "b_vnorm", "b_ws", "b_bs", "b_out",
             "f_norm", "f_up", "f_conv_w", "f_conv_b", "f_down", "final_norm"]
    res = {nm: (big[nm] if nm in big else small[nm]) for nm in order}
    outs = [loss_v[0, 0], grad_x.reshape(x.shape)]
    for k in range(4):
        outs += [res[nm][k] for nm in order]
    return tuple(outs)


def _unpack_rows(rows, shapes):
    out, off = [], 0
    for shp in shapes:
        n = math.prod(shp)
        out.append(rows[:, off:off + n].reshape((N_DEV,) + tuple(shp)))
        off += n + (-n) % PACK_GRANULE
    return out
```

```python
import math

import jax
import jax.numpy as jnp
from jax import lax
from jax.experimental import pallas as pl
from jax.experimental.pallas import tpu as pltpu

F32 = jnp.float32
BF16 = jnp.bfloat16
MESH = pl.DeviceIdType.MESH

N_DEV = 8
RMS_EPS = 1e-5
CHUNK = 128
SG_GROUPS = 8
ADAM_LR = 0.001
ADAM_B1 = 0.9
ADAM_B2 = 0.999
ADAM_EPS = 1e-08
ADAM_WD = 0.01
ADAM_STEP = 10

LANES = 128
SLAB = 16
FUSE_STRIP = 256
FUSE_HALO = 8
VMEM_LIMIT = 56 * 1024 * 1024
PACK_GRANULE = 8 * LANES


def _pick(dim, cap, mult):
    best = None
    t = mult
    while t <= min(dim, cap):
        if dim % t == 0:
            best = t
        t += mult
    return dim if best is None else best


def _params(semantics=None):
    return pltpu.CompilerParams(dimension_semantics=semantics, vmem_limit_bytes=VMEM_LIMIT)


_DIMS = {
    "nn": (((1,), (0,)), ((), ())),
    "nt": (((1,), (1,)), ((), ())),
    "tn": (((0,), (0,)), ((), ())),
}


def _matmul(a, b, mode, out_dtype, name, resid=None, after=None, tm_cap=1024, tn_cap=1024, tk_cap=2816,
            hold_b=False):
    if mode == "nn":
        (m, k), n = a.shape, b.shape[1]
    elif mode == "nt":
        (m, k), n = a.shape, b.shape[0]
    else:
        (k, m), n = a.shape, b.shape[1]
    tm, tn, tk = _pick(m, tm_cap, LANES), _pick(n, tn_cap, LANES), _pick(k, tk_cap, LANES)
    nk = k // tk
    n_in = 2 + (resid is not None) + (after is not None)

    def body(*refs):
        a_ref, b_ref = refs[:2]
        r_ref = refs[2] if resid is not None else None
        o_ref = refs[n_in]
        prod = lax.dot_general(a_ref[...], b_ref[...], _DIMS[mode], preferred_element_type=F32)

        def finish(r):
            if r_ref is not None:
                r = r + r_ref[...]
            o_ref[...] = r.astype(out_dtype)

        if nk == 1:
            finish(prod)
            return
        acc_ref = refs[n_in + 1]
        kk = pl.program_id(2)

        @pl.when(kk == 0)
        def _():
            acc_ref[...] = prod

        @pl.when(jnp.logical_and(kk > 0, kk < nk - 1))
        def _():
            acc_ref[...] += prod

        @pl.when(kk == nk - 1)
        def _():
            finish(acc_ref[...] + prod)

    def spec(block, index):
        if hold_b:
            return pl.BlockSpec(block, lambda j, i, kk: index(i, j, kk))
        return pl.BlockSpec(block, index)

    a_spec = (spec((tk, tm), lambda i, j, kk: (kk, i)) if mode == "tn"
              else spec((tm, tk), lambda i, j, kk: (i, kk)))
    b_spec = (spec((tn, tk), lambda i, j, kk: (j, kk)) if mode == "nt"
              else spec((tk, tn), lambda i, j, kk: (kk, j)))
    o_spec = spec((tm, tn), lambda i, j, kk: (i, j))
    in_specs = [a_spec, b_spec] + ([o_spec] if resid is not None else [])
    args = (a, b) + ((resid,) if resid is not None else ())
    if after is not None:
        in_specs.append(pl.BlockSpec(memory_space=pl.ANY))
        args += (after,)
    return pl.pallas_call(
        body, name=name, grid=(n // tn, m // tm, nk) if hold_b else (m // tm, n // tn, nk),
        in_specs=in_specs, out_specs=o_spec,
        out_shape=jax.ShapeDtypeStruct((m, n), out_dtype),
        scratch_shapes=[pltpu.VMEM((tm, tn), F32)] if nk > 1 else [],
        compiler_params=_params(("parallel", "parallel", "arbitrary")),
    )(*args)


def _rms_stats(xf):
    inv = lax.rsqrt(jnp.mean(xf * xf, axis=-1, keepdims=True) + RMS_EPS)
    return inv, xf * inv


def _rmsnorm_fwd(x, g, name, after=None):
    s, d = x.shape
    tm = _pick(s, 512, SLAB)
    extra = () if after is None else (after,)

    def body(x_ref, g_ref, *rest):
        _, xhat = _rms_stats(x_ref[...])
        rest[-1][...] = (xhat * g_ref[...]).astype(BF16)

    return pl.pallas_call(
        body, name=name, grid=(s // tm,),
        in_specs=[pl.BlockSpec((tm, d), lambda i: (i, 0)), pl.BlockSpec((1, d), lambda i: (0, 0))]
        + [pl.BlockSpec(memory_space=pl.ANY)] * len(extra),
        out_specs=pl.BlockSpec((tm, d), lambda i: (i, 0)),
        out_shape=jax.ShapeDtypeStruct((s, d), BF16),
        compiler_params=_params(("parallel",)),
    )(x, g, *extra)


def _rmsnorm_bwd(x, g, dh, dx_out, name):
    s, d = x.shape
    tm = _pick(s, 256, SLAB)

    def body(x_ref, g_ref, dh_ref, dxo_ref, dxi_ref, dxib_ref, dg_ref):
        inv, xhat = _rms_stats(x_ref[...])
        dhv = dh_ref[...].astype(F32)
        dxhat = dhv * g_ref[...]
        proj = jnp.mean(dxhat * xhat, axis=-1, keepdims=True)
        dx = dxo_ref[...] + inv * (dxhat - xhat * proj)
        dxi_ref[...] = dx
        dxib_ref[...] = dx.astype(BF16)
        part = jnp.sum(dhv * xhat, axis=0, keepdims=True)

        @pl.when(pl.program_id(0) == 0)
        def _():
            dg_ref[...] = part

        @pl.when(pl.program_id(0) > 0)
        def _():
            dg_ref[...] += part

    row = pl.BlockSpec((tm, d), lambda i: (i, 0))
    vec = pl.BlockSpec((1, d), lambda i: (0, 0))
    return pl.pallas_call(
        body, name=name, grid=(s // tm,),
        in_specs=[row, vec, row, row], out_specs=[row, row, vec],
        out_shape=[jax.ShapeDtypeStruct((s, d), F32), jax.ShapeDtypeStruct((s, d), BF16),
                   jax.ShapeDtypeStruct((1, d), F32)],
        compiler_params=_params(("arbitrary",)),
    )(x, g, dh, dx_out)


def _final_loss(x, g, target, name):
    s, d = x.shape
    tm = _pick(s, 256, SLAB)

    def body(x_ref, g_ref, t_ref, dx_ref, dxb_ref, loss_ref, dg_ref):
        inv, xhat = _rms_stats(x_ref[...])
        gv = g_ref[...]
        err = xhat * gv - t_ref[...]
        loss = 0.5 * jnp.sum(jnp.mean(err * err, axis=-1, keepdims=True), axis=0, keepdims=True)
        dy = err * (1.0 / d)
        dxhat = dy * gv
        proj = jnp.mean(dxhat * xhat, axis=-1, keepdims=True)
        dx = inv * (dxhat - xhat * proj)
        dx_ref[...] = dx
        dxb_ref[...] = dx.astype(BF16)
        part = jnp.sum(dy * xhat, axis=0, keepdims=True)
        loss_row = jnp.broadcast_to(loss, (1, LANES))

        @pl.when(pl.program_id(0) == 0)
        def _():
            dg_ref[...] = part
            loss_ref[...] = loss_row

        @pl.when(pl.program_id(0) > 0)
        def _():
            dg_ref[...] += part
            loss_ref[...] += loss_row

    row = pl.BlockSpec((tm, d), lambda i: (i, 0))
    vec = pl.BlockSpec((1, d), lambda i: (0, 0))
    return pl.pallas_call(
        body, name=name, grid=(s // tm,),
        in_specs=[row, vec, row],
        out_specs=[row, row, pl.BlockSpec((1, LANES), lambda i: (0, 0)), vec],
        out_shape=[jax.ShapeDtypeStruct((s, d), F32), jax.ShapeDtypeStruct((s, d), BF16),
                   jax.ShapeDtypeStruct((1, LANES), F32), jax.ShapeDtypeStruct((1, d), F32)],
        compiler_params=_params(("arbitrary",)),
    )(x, g, target)


def _mixa_in_fused(h, w_in, wc, name):
    s, d = h.shape
    tm = _pick(s, 1024, LANES)
    tn = _pick(d, 512, FUSE_STRIP)
    nj = d // tn

    def body(h_ref, wb_ref, wg_ref, wx_ref, wc_ref, gb_ref, gc_ref, xs_ref, cv_ref, y_ref, carry):
        @pl.when(pl.program_id(1) == 0)
        def _():
            carry[...] = jnp.zeros_like(carry)

        def matmul(st):
            cols = slice(st * FUSE_STRIP, (st + 1) * FUSE_STRIP)
            return tuple(jnp.dot(h_ref[...], w_ref[:, cols], preferred_element_type=F32).astype(BF16)
                         for w_ref in (wb_ref, wg_ref, wx_ref))

        n_strips = tn // FUSE_STRIP
        parts = matmul(0)
        for st in range(n_strips):
            parts_next = matmul(st + 1) if st + 1 < n_strips else None
            cols = slice(st * FUSE_STRIP, (st + 1) * FUSE_STRIP)
            for ref, part in zip((gb_ref, gc_ref, xs_ref), parts):
                ref[:, cols] = part
            p = parts[1].astype(F32) * parts[2].astype(F32)
            ext = jnp.concatenate([carry[:, cols], p], axis=0)
            s1 = pltpu.roll(ext, 1, 0)[FUSE_HALO:, :]
            s2 = pltpu.roll(ext, 2, 0)[FUSE_HALO:, :]
            carry[:, cols] = p[tm - FUSE_HALO:, :]
            cv = wc_ref[0:1, cols] * s2 + wc_ref[1:2, cols] * s1 + wc_ref[2:3, cols] * p
            cv_ref[:, cols] = cv.astype(BF16)
            y_ref[:, cols] = (parts[0].astype(F32) * cv).astype(BF16)
            parts = parts_next

    def cols_of(rows, offset):
        return pl.BlockSpec((rows, tn), lambda j, i: (0, j + offset))

    tile = pl.BlockSpec((tm, tn), lambda j, i: (i, j))
    return pl.pallas_call(
        body, name=name, grid=(nj, s // tm),
        in_specs=[pl.BlockSpec((tm, d), lambda j, i: (i, 0)), cols_of(d, 0), cols_of(d, nj), cols_of(d, 2 * nj),
                  cols_of(3, 0)],
        out_specs=[tile] * 5, out_shape=[jax.ShapeDtypeStruct((s, d), BF16)] * 5,
        scratch_shapes=[pltpu.VMEM((FUSE_HALO, tn), F32)],
        compiler_params=_params(("parallel", "arbitrary")),
    )(h, w_in, w_in, w_in, wc)


def _mixa_out_dx_fused(dxb, w_out, bcx, cv, wc, after, name):
    s, d = dxb.shape
    tm = _pick(s, 1024, LANES)
    tn = _pick(d, 512, FUSE_STRIP)
    nj, ni = d // tn, s // tm
    n_steps = nj * ni
    sub = tm // FUSE_HALO

    def body(dx_ref, w_ref, gb_ref, gc_ref, xs_ref, cv_ref, wc_ref, after_ref,
             dbcx_hbm, dwc_ref, out_buf, out_sem, carry, acc):
        j, i = pl.program_id(0), pl.program_id(1)
        step = j * ni + i
        slot = lax.rem(step, 2)
        row0 = pl.multiple_of((ni - 1 - i) * tm, tm)

        def out_copy(part, slot_=None):
            slot_ = slot if slot_ is None else slot_
            col0 = pl.multiple_of(part * d + j * tn, LANES)
            return pltpu.make_async_copy(out_buf.at[slot_, part], dbcx_hbm.at[pl.ds(row0, tm), pl.ds(col0, tn)],
                                         out_sem.at[slot_, part])

        @pl.when(step >= 2)
        def _():
            for part in range(3):
                out_copy(part).wait()

        @pl.when(i == 0)
        def _():
            carry[...] = jnp.zeros_like(carry)
            acc[...] = jnp.zeros_like(acc)

        for st in range(tn // FUSE_STRIP):
            cols = slice(st * FUSE_STRIP, (st + 1) * FUSE_STRIP)
            dyv = lax.dot_general(dx_ref[...], w_ref[cols, :], _DIMS["nt"], preferred_element_type=F32)
            gc = gc_ref[:, cols].astype(F32)
            xs = xs_ref[:, cols].astype(F32)
            d0 = dyv * gb_ref[:, cols].astype(F32)
            ext = jnp.concatenate([d0, carry[:, cols]], axis=0)
            d1 = pltpu.roll(ext, tm + FUSE_HALO - 1, 0)[:tm, :]
            d2 = pltpu.roll(ext, tm + FUSE_HALO - 2, 0)[:tm, :]
            carry[:, cols] = d0[:FUSE_HALO, :]
            dp = wc_ref[2:3, cols] * d0 + wc_ref[1:2, cols] * d1 + wc_ref[0:1, cols] * d2
            out_buf[slot, 0, :, cols] = (dyv * cv_ref[:, cols].astype(F32)).astype(BF16)
            out_buf[slot, 1, :, cols] = (dp * xs).astype(BF16)
            out_buf[slot, 2, :, cols] = (dp * gc).astype(BF16)
            p = gc * xs
            for k, term in enumerate((d2 * p, d1 * p, d0 * p)):
                acc[k, :, cols] += jnp.sum(term.reshape(sub, FUSE_HALO, FUSE_STRIP), axis=0)

        for part in range(3):
            out_copy(part).start()

        @pl.when(i == ni - 1)
        def _():
            for k in range(3):
                dwc_ref[k:k + 1, :] = jnp.sum(acc[k], axis=0, keepdims=True)

        @pl.when(step == n_steps - 1)
        def _():
            for part in range(3):
                out_copy(part).wait()
                if n_steps > 1:
                    out_copy(part, 1 - slot).wait()

    tile = pl.BlockSpec((tm, tn), lambda j, i: (ni - 1 - i, j))
    return pl.pallas_call(
        body, name=name, grid=(nj, ni),
        in_specs=[pl.BlockSpec((tm, d), lambda j, i: (ni - 1 - i, 0)), pl.BlockSpec((tn, d), lambda j, i: (j, 0)),
                  tile, tile, tile, tile, pl.BlockSpec((3, tn), lambda j, i: (0, j)),
                  pl.BlockSpec(memory_space=pl.ANY)],
        out_specs=[pl.BlockSpec(memory_space=pl.ANY), pl.BlockSpec((3, tn), lambda j, i: (0, j))],
        out_shape=[jax.ShapeDtypeStruct((s, 3 * d), BF16), jax.ShapeDtypeStruct((3, d), F32)],
        scratch_shapes=[pltpu.VMEM((2, 3, tm, tn), BF16), pltpu.SemaphoreType.DMA((2, 3)),
                        pltpu.VMEM((FUSE_HALO, tn), F32), pltpu.VMEM((3, FUSE_HALO, tn), F32)],
        compiler_params=_params(("arbitrary", "arbitrary")),
    )(dxb, w_out, bcx[0], bcx[1], bcx[2], cv, wc, after)


def _sigmoid(z):
    return 0.5 * jnp.tanh(0.5 * z) + 0.5


def _ffn_up_fused(h, w_up, cw, cb, name):
    s, d = h.shape
    f = w_up.shape[1] // 2
    tm = _pick(s, 1024, LANES)
    tn = _pick(f, 512, FUSE_STRIP)
    nj = f // tn

    def body(h_ref, wg_ref, wa_ref, cwg_ref, cwa_ref, cbg_ref, cba_ref,
             upg_ref, upa_ref, cvg_ref, cva_ref, act_ref, carry_g, carry_a):
        @pl.when(pl.program_id(1) == 0)
        def _():
            carry_g[...] = jnp.zeros_like(carry_g)
            carry_a[...] = jnp.zeros_like(carry_a)

        def matmul(st):
            cols = slice(st * FUSE_STRIP, (st + 1) * FUSE_STRIP)
            return tuple(jnp.dot(h_ref[...], w_ref[:, cols], preferred_element_type=F32).astype(BF16)
                         for w_ref in (wg_ref, wa_ref))

        def conv(up, cw_ref, cb_ref, carry, up_ref, cv_ref, cols):
            up_ref[:, cols] = up
            x = up.astype(F32)
            ext = jnp.concatenate([carry[:, cols], x], axis=0)
            s1 = pltpu.roll(ext, 1, 0)[FUSE_HALO:, :]
            s2 = pltpu.roll(ext, 2, 0)[FUSE_HALO:, :]
            carry[:, cols] = x[tm - FUSE_HALO:, :]
            cv = cw_ref[0:1, cols] * s2 + cw_ref[1:2, cols] * s1 + cw_ref[2:3, cols] * x + cb_ref[:, cols]
            cv_ref[:, cols] = cv.astype(BF16)
            return cv

        n_strips = tn // FUSE_STRIP
        ups = matmul(0)
        for st in range(n_strips):
            ups_next = matmul(st + 1) if st + 1 < n_strips else None
            cols = slice(st * FUSE_STRIP, (st + 1) * FUSE_STRIP)
            gcv = conv(ups[0], cwg_ref, cbg_ref, carry_g, upg_ref, cvg_ref, cols)
            acv = conv(ups[1], cwa_ref, cba_ref, carry_a, upa_ref, cva_ref, cols)
            act_ref[:, cols] = (gcv * _sigmoid(gcv) * acv).astype(BF16)
            ups = ups_next

    def cols_of(rows, offset):
        return pl.BlockSpec((rows, tn), lambda j, i: (0, j + offset))

    tile = pl.BlockSpec((tm, tn), lambda j, i: (i, j))
    out = jax.ShapeDtypeStruct((s, f), BF16)
    return pl.pallas_call(
        body, name=name, grid=(nj, s // tm),
        in_specs=[pl.BlockSpec((tm, d), lambda j, i: (i, 0)), cols_of(d, 0), cols_of(d, nj),
                  cols_of(3, 0), cols_of(3, nj), cols_of(1, 0), cols_of(1, nj)],
        out_specs=[tile] * 5, out_shape=[out] * 5,
        scratch_shapes=[pltpu.VMEM((FUSE_HALO, tn), F32)] * 2,
        compiler_params=_params(("parallel", "arbitrary")),
    )(h, w_up, w_up, cw, cw, cb, cb)


def _ffn_down_dx_fused(dxb, w_down, up, cv, cw, after, name):
    s, d = dxb.shape
    f = w_down.shape[0]
    tm = _pick(s, 1024, LANES)
    tn = _pick(f, 512, FUSE_STRIP)
    nj, ni = f // tn, s // tm
    n_steps = nj * ni
    sub = tm // FUSE_HALO

    def body(dx_ref, w_ref, upg_ref, upa_ref, cvg_ref, cva_ref, cwg_ref, cwa_ref, after_ref,
             dup_hbm, dcwg_ref, dcwa_ref, dcbg_ref, dcba_ref, out_buf, out_sem, carry, acc):
        j, i = pl.program_id(0), pl.program_id(1)
        step = j * ni + i
        slot = lax.rem(step, 2)
        row0 = pl.multiple_of((ni - 1 - i) * tm, tm)

        def out_copy(half):
            col0 = pl.multiple_of(half * f + j * tn, LANES)
            return pltpu.make_async_copy(out_buf.at[slot, half], dup_hbm.at[pl.ds(row0, tm), pl.ds(col0, tn)],
                                         out_sem.at[slot, half])

        @pl.when(step >= 2)
        def _():
            for half in range(2):
                out_copy(half).wait()

        @pl.when(i == 0)
        def _():
            carry[...] = jnp.zeros_like(carry)
            acc[...] = jnp.zeros_like(acc)

        for st in range(tn // FUSE_STRIP):
            cols = slice(st * FUSE_STRIP, (st + 1) * FUSE_STRIP)
            dact = lax.dot_general(dx_ref[...], w_ref[cols, :], _DIMS["nt"], preferred_element_type=F32)
            gcv = cvg_ref[:, cols].astype(F32)
            acv = cva_ref[:, cols].astype(F32)
            sg = _sigmoid(gcv)
            dd = (dact * acv * (sg * (1.0 + gcv * (1.0 - sg))), dact * (gcv * sg))
            for half, (up_ref, cw_ref) in enumerate(((upg_ref, cwg_ref), (upa_ref, cwa_ref))):
                x = up_ref[:, cols].astype(F32)
                d0 = dd[half]
                ext = jnp.concatenate([d0, carry[half, :, cols]], axis=0)
                d1 = pltpu.roll(ext, tm + FUSE_HALO - 1, 0)[:tm, :]
                d2 = pltpu.roll(ext, tm + FUSE_HALO - 2, 0)[:tm, :]
                carry[half, :, cols] = d0[:FUSE_HALO, :]
                out_buf[slot, half, :, cols] = (cw_ref[2:3, cols] * d0 + cw_ref[1:2, cols] * d1
                                                + cw_ref[0:1, cols] * d2).astype(BF16)
                for k, term in enumerate((d2 * x, d1 * x, d0 * x, d0)):
                    acc[half, k, :, cols] += jnp.sum(term.reshape(sub, FUSE_HALO, FUSE_STRIP), axis=0)

        for half in range(2):
            out_copy(half).start()

        @pl.when(i == ni - 1)
        def _():
            for half, (dcw_ref, dcb_ref) in enumerate(((dcwg_ref, dcbg_ref), (dcwa_ref, dcba_ref))):
                for k in range(3):
                    dcw_ref[k:k + 1, :] = jnp.sum(acc[half, k], axis=0, keepdims=True)
                dcb_ref[...] = jnp.sum(acc[half, 3], axis=0, keepdims=True)

        @pl.when(step == n_steps - 1)
        def _():
            for half in range(2):
                out_copy(half).wait()
                if n_steps > 1:
                    pltpu.make_async_copy(out_buf.at[1 - slot, half], dup_hbm.at[pl.ds(row0, tm), pl.ds(0, tn)],
                                          out_sem.at[1 - slot, half]).wait()

    tile = pl.BlockSpec((tm, tn), lambda j, i: (ni - 1 - i, j))

    def cols_of(rows, offset):
        return pl.BlockSpec((rows, tn), lambda j, i: (0, j + offset))

    small = pl.BlockSpec((3, tn), lambda j, i: (0, j)), pl.BlockSpec((1, tn), lambda j, i: (0, j))
    return pl.pallas_call(
        body, name=name, grid=(nj, ni),
        in_specs=[pl.BlockSpec((tm, d), lambda j, i: (ni - 1 - i, 0)), pl.BlockSpec((tn, d), lambda j, i: (j, 0)),
                  tile, tile, tile, tile, cols_of(3, 0), cols_of(3, nj), pl.BlockSpec(memory_space=pl.ANY)],
        out_specs=[pl.BlockSpec(memory_space=pl.ANY), small[0], small[0], small[1], small[1]],
        out_shape=[jax.ShapeDtypeStruct((s, 2 * f), BF16), jax.ShapeDtypeStruct((3, f), F32),
                   jax.ShapeDtypeStruct((3, f), F32), jax.ShapeDtypeStruct((1, f), F32),
                   jax.ShapeDtypeStruct((1, f), F32)],
        scratch_shapes=[pltpu.VMEM((2, 2, tm, tn), BF16), pltpu.SemaphoreType.DMA((2, 2)),
                        pltpu.VMEM((2, FUSE_HALO, tn), F32), pltpu.VMEM((2, 4, FUSE_HALO, tn), F32)],
        compiler_params=_params(("arbitrary", "arbitrary")),
    )(dxb, w_down, up[0], up[1], cv[0], cv[1], cw, cw, after)


_GELU_C = math.sqrt(2.0 / math.pi)


def _gelu(x):
    th = jnp.tanh(_GELU_C * (x + 0.044715 * (x * x * x)))
    return x * (0.5 * (1.0 + th)), th


def _gelu_grad(x, th):
    return 0.5 * (1.0 + th) + 0.5 * x * (1.0 - th * th) * (_GELU_C * (1.0 + 3.0 * 0.044715 * (x * x)))


def _masked_ws(ws_ref, h):
    t = lax.broadcasted_iota(jnp.int32, (CHUNK, CHUNK), 0)
    sx = lax.broadcasted_iota(jnp.int32, (CHUNK, CHUNK), 1)
    return jnp.where(sx <= t, ws_ref[h], 0.0)


def _mixb_fwd(pre, gv, ws, bs_wide, name):
    s, w2 = pre.shape
    w = w2 // 2
    gw = w // SG_GROUPS

    def body(pre_ref, gv_ref, ws_ref, bs_ref, o_ref):
        zu, _ = _gelu(pre_ref[:, :w].astype(F32))
        zv, _ = _gelu(pre_ref[:, w:].astype(F32))
        _, vhat = _rms_stats(zv)
        vn = (vhat * gv_ref[...]).astype(BF16)
        for h in range(SG_GROUPS):
            cols = slice(h * gw, (h + 1) * gw)
            wsm = _masked_ws(ws_ref, h).astype(BF16)
            gate = jnp.dot(wsm, vn[:, cols], preferred_element_type=F32)
            gate = gate + jnp.tile(bs_ref[h], (1, gw // LANES))
            o_ref[:, cols] = (zu[:, cols] * gate).astype(BF16)

    return pl.pallas_call(
        body, name=name, grid=(s // CHUNK,),
        in_specs=[pl.BlockSpec((CHUNK, w2), lambda i: (i, 0)), pl.BlockSpec((1, w), lambda i: (0, 0)),
                  pl.BlockSpec((SG_GROUPS, CHUNK, CHUNK), lambda i: (0, 0, 0)),
                  pl.BlockSpec((SG_GROUPS, CHUNK, LANES), lambda i: (0, 0, 0))],
        out_specs=pl.BlockSpec((CHUNK, w), lambda i: (i, 0)),
        out_shape=jax.ShapeDtypeStruct((s, w), BF16),
        compiler_params=_params(("parallel",)),
    )(pre, gv, ws, bs_wide)


def _mixb_bwd(pre, dug, gv, ws, bs_wide, name):
    s, w2 = pre.shape
    w = w2 // 2
    gw = w // SG_GROUPS

    def body(pre_ref, dug_ref, gv_ref, ws_ref, bs_ref, o_ref, dws_ref, dbs_ref, dgv_ref, dvn_ref):
        first = pl.program_id(0) == 0

        @pl.when(first)
        def _():
            dws_ref[...] = jnp.zeros_like(dws_ref)
            dbs_ref[...] = jnp.zeros_like(dbs_ref)

        pu = pre_ref[:, :w].astype(F32)
        pv = pre_ref[:, w:].astype(F32)
        zu, thu = _gelu(pu)
        zv, thv = _gelu(pv)
        inv, vhat = _rms_stats(zv)
        gvv = gv_ref[...]
        vn = (vhat * gvv).astype(BF16)
        for h in range(SG_GROUPS):
            cols = slice(h * gw, (h + 1) * gw)
            wsm = _masked_ws(ws_ref, h).astype(BF16)
            gate = jnp.dot(wsm, vn[:, cols], preferred_element_type=F32)
            gate = gate + jnp.tile(bs_ref[h], (1, gw // LANES))
            dug_h = dug_ref[:, cols].astype(F32)
            dgate = dug_h * zu[:, cols]
            dgate_b = dgate.astype(BF16)
            o_ref[:, cols] = (dug_h * gate * _gelu_grad(pu[:, cols], thu[:, cols])).astype(BF16)
            dbs_ref[h] += jnp.broadcast_to(jnp.sum(dgate, axis=-1, keepdims=True), (CHUNK, LANES))
            dws = lax.dot_general(dgate_b, vn[:, cols], _DIMS["nt"], preferred_element_type=F32)
            t = lax.broadcasted_iota(jnp.int32, (CHUNK, CHUNK), 0)
            sx = lax.broadcasted_iota(jnp.int32, (CHUNK, CHUNK), 1)
            dws_ref[h] += jnp.where(sx <= t, dws, 0.0)
            dvn_ref[:, cols] = lax.dot_general(wsm, dgate_b, _DIMS["tn"], preferred_element_type=F32)
        dvn = dvn_ref[...]
        part = jnp.sum(dvn * vhat, axis=0, keepdims=True)

        @pl.when(first)
        def _():
            dgv_ref[...] = part

        @pl.when(jnp.logical_not(first))
        def _():
            dgv_ref[...] += part

        dvhat = dvn * gvv
        dzv = inv * (dvhat - vhat * jnp.mean(dvhat * vhat, axis=-1, keepdims=True))
        o_ref[:, w:] = (dzv * _gelu_grad(pv, thv)).astype(BF16)

    return pl.pallas_call(
        body, name=name, grid=(s // CHUNK,),
        in_specs=[pl.BlockSpec((CHUNK, w2), lambda i: (i, 0)), pl.BlockSpec((CHUNK, w), lambda i: (i, 0)),
                  pl.BlockSpec((1, w), lambda i: (0, 0)),
                  pl.BlockSpec((SG_GROUPS, CHUNK, CHUNK), lambda i: (0, 0, 0)),
                  pl.BlockSpec((SG_GROUPS, CHUNK, LANES), lambda i: (0, 0, 0))],
        out_specs=[pl.BlockSpec((CHUNK, w2), lambda i: (i, 0)),
                   pl.BlockSpec((SG_GROUPS, CHUNK, CHUNK), lambda i: (0, 0, 0)),
                   pl.BlockSpec((SG_GROUPS, CHUNK, LANES), lambda i: (0, 0, 0)),
                   pl.BlockSpec((1, w), lambda i: (0, 0))],
        out_shape=[jax.ShapeDtypeStruct((s, w2), BF16), jax.ShapeDtypeStruct((SG_GROUPS, CHUNK, CHUNK), F32),
                   jax.ShapeDtypeStruct((SG_GROUPS, CHUNK, LANES), F32), jax.ShapeDtypeStruct((1, w), F32)],
        scratch_shapes=[pltpu.VMEM((CHUNK, w), F32)],
        compiler_params=_params(("arbitrary",)),
    )(pre, dug, gv, ws, bs_wide)


def _cast_layer(w3, layer, name):
    _, r, c = w3.shape
    tr = _pick(r, 256, SLAB)

    def body(w_ref, o_ref):
        o_ref[...] = w_ref[...].astype(BF16)

    return pl.pallas_call(
        body, name=name, grid=(r // tr,),
        in_specs=[pl.BlockSpec((None, tr, c), lambda i: (layer, i, 0))],
        out_specs=pl.BlockSpec((tr, c), lambda i: (i, 0)),
        out_shape=jax.ShapeDtypeStruct((r, c), BF16),
        compiler_params=_params(("parallel",)),
    )(w3)


def _adamw_math(w, g, m, v):
    m = ADAM_B1 * m + (1.0 - ADAM_B1) * g
    v = ADAM_B2 * v + (1.0 - ADAM_B2) * (g * g)
    m_hat = m / (1.0 - ADAM_B1 ** ADAM_STEP)
    v_hat = v / (1.0 - ADAM_B2 ** ADAM_STEP)
    delta = -ADAM_LR * (m_hat / (jnp.sqrt(v_hat) + ADAM_EPS) + ADAM_WD * w)
    return delta, m, v


def _adamw_sharded(recvs, w, m, v, name):
    nl, r, c = w.shape
    tc = _pick(c, 1536, LANES)
    tr = _pick(r, 64, SLAB)

    def body(*refs):
        recv_refs = refs[:nl]
        w_ref, m_ref, v_ref, g_ref, d_ref, nm_ref, nv_ref = refs[nl:]
        for layer, recv_ref in enumerate(recv_refs):
            @pl.when(pl.program_id(0) == layer)
            def _():
                g = recv_ref[0].astype(F32)
                for q in range(1, N_DEV):
                    g = g + recv_ref[q].astype(F32)
                delta, nm, nv = _adamw_math(w_ref[...], g, m_ref[...], v_ref[...])
                g_ref[...] = g
                d_ref[...] = delta
                nm_ref[...] = nm
                nv_ref[...] = nv

    def recv_spec(layer):
        return pl.BlockSpec((N_DEV, tr, tc),
                            lambda l, i, j: (0, jnp.where(l == layer, i, 0), jnp.where(l == layer, j, 0)))

    blk = pl.BlockSpec((None, tr, tc), lambda l, i, j: (l, i, j))
    out = jax.ShapeDtypeStruct((nl, r, c), F32)
    return pl.pallas_call(
        body, name=name, grid=(nl, r // tr, c // tc),
        in_specs=[recv_spec(layer) for layer in range(nl)] + [blk, blk, blk],
        out_specs=[blk] * 4, out_shape=[out] * 4,
        compiler_params=_params(("parallel",) * 3),
    )(*recvs, w, m, v)


def _adamw_packed(w, g, m, v, name):
    r, c = w.shape
    tr = _pick(r, 256, 8)

    def body(w_ref, g_ref, m_ref, v_ref, d_ref, nm_ref, nv_ref):
        delta, nm, nv = _adamw_math(w_ref[...], g_ref[...], m_ref[...], v_ref[...])
        d_ref[...] = delta
        nm_ref[...] = nm
        nv_ref[...] = nv

    blk = pl.BlockSpec((tr, c), lambda i: (i, 0))
    out = jax.ShapeDtypeStruct((r, c), F32)
    return pl.pallas_call(
        body, name=name, grid=(r // tr,), in_specs=[blk] * 4, out_specs=[blk] * 3, out_shape=[out] * 3,
        compiler_params=_params(("parallel",)),
    )(w, g, m, v)


def _pack(arrays):
    parts = []
    for a in arrays:
        flat = a.reshape(-1).astype(F32)
        pad = (-flat.shape[0]) % PACK_GRANULE
        parts.append(jnp.pad(flat, (0, pad)) if pad else flat)
    return jnp.concatenate(parts).reshape(-1, LANES)


def _unpack(buf, shapes):
    flat = buf.reshape(-1)
    out, off = [], 0
    for shp in shapes:
        n = math.prod(shp)
        out.append(flat[off:off + n].reshape(shp))
        off += n + (-n) % PACK_GRANULE
    return out


def _mesh_pos():
    return lax.axis_index("x"), lax.axis_index("y"), lax.axis_index("c")


def _coords(q):
    return q // 4, (q // 2) % 2, q % 2


def _shard_of(ref, q, shard_shape, axis):
    r, c = shard_shape
    if axis == 0:
        return ref.at[pl.ds(pl.multiple_of(q * r, SLAB), r), :]
    return ref.at[:, pl.ds(pl.multiple_of(q * c, LANES), c)]


_HBM = pl.BlockSpec(memory_space=pltpu.HBM)
_SEM = pl.BlockSpec(memory_space=pltpu.SEMAPHORE)
_EFFECT = pltpu.SideEffectType.DATAFLOW_SIDE_EFFECTING


def _exchange_shapes(gather, src_shape, axis):
    r, c = src_shape
    if gather:
        return (r, c), ((r * N_DEV, c) if axis == 0 else (r, c * N_DEV))
    shard = (r // N_DEV, c) if axis == 0 else (r, c // N_DEV)
    return shard, (N_DEV,) + shard


def _exchange_copies(gather, src, land, sems, axis):
    send_sems, recv_sems, own_sem = sems
    x, y, c_ = _mesh_pos()
    me = 4 * x + 2 * y + c_
    shard, _ = _exchange_shapes(gather, src.shape, axis)

    def piece(q):
        return src if gather else _shard_of(src, q, shard, axis)

    def place(q):
        return _shard_of(land, q, shard, axis) if gather else land.at[q]

    own = pltpu.make_async_copy(piece(me), place(me), own_sem.at[0])
    sends, arrivals = [], []
    for step in range(1, N_DEV):
        to = (me + step) % N_DEV
        frm = (me + N_DEV - step) % N_DEV
        sends.append(pltpu.make_async_remote_copy(
            src_ref=piece(to), dst_ref=place(me), send_sem=send_sems.at[step - 1], recv_sem=recv_sems.at[step - 1],
            device_id=_coords(to), device_id_type=MESH))
        arrivals.append(pltpu.make_async_remote_copy(
            src_ref=piece(me), dst_ref=place(frm), send_sem=send_sems.at[step - 1], recv_sem=recv_sems.at[step - 1],
            device_id=_coords(frm), device_id_type=MESH))
    return own, sends, arrivals


def _exchange_start(gather, src, axis, name, after=None):
    _, land_shape = _exchange_shapes(gather, src.shape, axis)
    extra = () if after is None else (after,)

    def body(*refs):
        src_ref, land = refs[:2]
        send_sems, recv_sems, own_sem = refs[2 + len(extra):5 + len(extra)]
        own, sends, _ = _exchange_copies(gather, src_ref, land, (send_sems, recv_sems, own_sem), axis)
        own.start()
        for cp in sends:
            cp.start()
        refs[-1][...] = jnp.zeros_like(refs[-1])

    out = pl.pallas_call(
        body, name=name,
        out_shape=(pltpu.SemaphoreType.DMA((N_DEV - 1,)), pltpu.SemaphoreType.DMA((N_DEV - 1,)),
                   pltpu.SemaphoreType.DMA((1,)), pltpu.HBM(src.shape, src.dtype),
                   pltpu.HBM(land_shape, src.dtype), jax.ShapeDtypeStruct((8, LANES), F32)),
        in_specs=[_HBM, _HBM] + [pl.BlockSpec(memory_space=pl.ANY)] * len(extra),
        out_specs=(_SEM, _SEM, _SEM, _HBM, _HBM, pl.BlockSpec(memory_space=pltpu.VMEM)),
        input_output_aliases={0: 3, 1: 4},
        compiler_params=pltpu.CompilerParams(has_side_effects=_EFFECT),
    )(pltpu.with_memory_space_constraint(src, pltpu.HBM),
      pltpu.with_memory_space_constraint(lax.empty(land_shape, src.dtype), pltpu.HBM), *extra)
    return out[:5], out[5]


def _exchange_wait(gather, state, axis, after, name):
    send_sems, recv_sems, own_sem, src_thru, land_thru = state
    after = tuple(after) if isinstance(after, (tuple, list)) else (after,)

    def body(src, land, send_sems, recv_sems, own_sem, *rest):
        own, sends, arrivals = _exchange_copies(gather, src, land, (send_sems, recv_sems, own_sem), axis)
        for cp in sends:
            cp.wait_send()
        for cp in arrivals:
            cp.wait_recv()
        own.wait()

    return pl.pallas_call(
        body, name=name,
        out_shape=(pltpu.HBM(src_thru.shape, src_thru.dtype), pltpu.HBM(land_thru.shape, land_thru.dtype)),
        in_specs=[_HBM, _HBM, _SEM, _SEM, _SEM] + [pl.BlockSpec(memory_space=pl.ANY)] * len(after),
        out_specs=(_HBM, _HBM),
        input_output_aliases={0: 0, 1: 1},
        compiler_params=pltpu.CompilerParams(has_side_effects=_EFFECT),
    )(src_thru, land_thru, send_sems, recv_sems, own_sem, *after)[1]


def _gather2_copies(shard_ref, land, sems, axis, shard_shape):
    send1, recv1, own_sem, send2, recv2 = sems
    x, y, c = _mesh_pos()
    me, sibling = (x, y, c), (x, y, 1 - c)
    chips = [(1 - x, y), (x, 1 - y), (1 - x, 1 - y)]

    def region(dev):
        px, py, pc = dev
        return _shard_of(land, 4 * px + 2 * py + pc, shard_shape, axis)

    def copy(src, block, to, send, recv):
        return pltpu.make_async_remote_copy(src_ref=src, dst_ref=region(block), send_sem=send, recv_sem=recv,
                                            device_id=to, device_id_type=MESH)

    own = pltpu.make_async_copy(shard_ref, region(me), own_sem.at[0])
    peers = [sibling] + [(*chip, c) for chip in chips]
    sends1 = [copy(shard_ref, me, to, send1.at[k], recv1.at[k]) for k, to in enumerate(peers)]
    arrivals1 = [copy(shard_ref, frm, frm, send1.at[k], recv1.at[k]) for k, frm in enumerate(peers)]
    sends2, arrivals2 = [], []
    if send2 is not None:
        for j, chip in enumerate(chips):
            sends2.append(copy(region((*chip, c)), (*chip, c), sibling, send2.at[j], recv2.at[j]))
            arrivals2.append(copy(region((*chip, 1 - c)), (*chip, 1 - c), sibling, send2.at[j], recv2.at[j]))
    return own, sends1, arrivals1, sends2, arrivals2


def _gather2_start(shard, axis, name, after=None):
    _, land_shape = _exchange_shapes(True, shard.shape, axis)
    extra = () if after is None else (after,)

    def body(*refs):
        src_ref, land = refs[:2]
        send1, recv1, own_sem = refs[2 + len(extra):5 + len(extra)]
        own, sends1, _, _, _ = _gather2_copies(src_ref, land, (send1, recv1, own_sem, None, None), axis, shard.shape)
        own.start()
        for cp in sends1[1:] + sends1[:1]:
            cp.start()
        refs[-1][...] = jnp.zeros_like(refs[-1])

    out = pl.pallas_call(
        body, name=name,
        out_shape=(pltpu.SemaphoreType.DMA((4,)), pltpu.SemaphoreType.DMA((4,)), pltpu.SemaphoreType.DMA((1,)),
                   pltpu.HBM(shard.shape, shard.dtype), pltpu.HBM(land_shape, shard.dtype),
                   jax.ShapeDtypeStruct((8, LANES), F32)),
        in_specs=[_HBM, _HBM] + [pl.BlockSpec(memory_space=pl.ANY)] * len(extra),
        out_specs=(_SEM, _SEM, _SEM, _HBM, _HBM, pl.BlockSpec(memory_space=pltpu.VMEM)),
        input_output_aliases={0: 3, 1: 4},
        compiler_params=pltpu.CompilerParams(has_side_effects=_EFFECT),
    )(pltpu.with_memory_space_constraint(shard, pltpu.HBM),
      pltpu.with_memory_space_constraint(lax.empty(land_shape, shard.dtype), pltpu.HBM), *extra)
    return out[:5], out[5]


def _gather2_pass(state, axis, after, name):
    send1, recv1, own_sem, shard_thru, land_thru = state

    def body(src_ref, land, send1, recv1, own_sem, after_ref, send2, recv2, src_out, land_out, token):
        _, _, arrivals1, sends2, _ = _gather2_copies(src_ref, land, (send1, recv1, own_sem, send2, recv2), axis,
                                                     shard_thru.shape)
        for arrival, fwd in zip(arrivals1[1:], sends2):
            arrival.wait_recv()
            fwd.start()
        token[...] = jnp.zeros_like(token)

    out = pl.pallas_call(
        body, name=name,
        out_shape=(pltpu.SemaphoreType.DMA((3,)), pltpu.SemaphoreType.DMA((3,)),
                   pltpu.HBM(shard_thru.shape, shard_thru.dtype), pltpu.HBM(land_thru.shape, land_thru.dtype),
                   jax.ShapeDtypeStruct((8, LANES), F32)),
        in_specs=[_HBM, _HBM, _SEM, _SEM, _SEM, pl.BlockSpec(memory_space=pl.ANY)],
        out_specs=(_SEM, _SEM, _HBM, _HBM, pl.BlockSpec(memory_space=pltpu.VMEM)),
        input_output_aliases={0: 2, 1: 3},
        compiler_params=pltpu.CompilerParams(has_side_effects=_EFFECT),
    )(shard_thru, land_thru, send1, recv1, own_sem, after)
    return (send1, recv1, own_sem, out[0], out[1], out[2], out[3]), out[4]


def _gather2_wait(state, axis, after, name):
    send1, recv1, own_sem, send2, recv2, shard_thru, land_thru = state

    def body(src_ref, land, send1, recv1, own_sem, send2, recv2, after_ref, src_dead, got):
        own, sends1, arrivals1, sends2, arrivals2 = _gather2_copies(
            src_ref, land, (send1, recv1, own_sem, send2, recv2), axis, shard_thru.shape)
        for cp in sends1 + sends2:
            cp.wait_send()
        for cp in arrivals1[:1] + arrivals2:
            cp.wait_recv()
        own.wait()

    return pl.pallas_call(
        body, name=name,
        out_shape=(pltpu.HBM(shard_thru.shape, shard_thru.dtype), pltpu.HBM(land_thru.shape, land_thru.dtype)),
        in_specs=[_HBM, _HBM] + [_SEM] * 5 + [pl.BlockSpec(memory_space=pl.ANY)],
        out_specs=(_HBM, _HBM),
        input_output_aliases={0: 0, 1: 1},
        compiler_params=pltpu.CompilerParams(has_side_effects=_EFFECT),
    )(shard_thru, land_thru, send1, recv1, own_sem, send2, recv2, after)[1]


def _sum_slots(slots, name):
    _, r, c = slots.shape
    tr = _pick(r, 512, 8)

    def body(s_ref, o_ref):
        total = s_ref[0]
        for q in range(1, N_DEV):
            total = total + s_ref[q]
        o_ref[...] = total

    return pl.pallas_call(
        body, name=name, grid=(r // tr,),
        in_specs=[pl.BlockSpec((N_DEV, tr, c), lambda i: (0, i, 0))],
        out_specs=pl.BlockSpec((tr, c), lambda i: (i, 0)),
        out_shape=jax.ShapeDtypeStruct((r, c), F32),
        compiler_params=_params(("parallel",)),
    )(slots)


def kernel(x, a_norm, a_in, a_conv, a_out, b_norm, b_in, b_vnorm, b_ws, b_bs, b_out, f_norm, f_up, f_conv_w, f_conv_b, f_down, final_norm, loss_target, m_a_norm, m_a_in, m_a_conv, m_a_out, m_b_norm, m_b_in, m_b_vnorm, m_b_ws, m_b_bs, m_b_out, m_f_norm, m_f_up, m_f_conv_w, m_f_conv_b, m_f_down, m_final_norm, v_a_norm, v_a_in, v_a_conv, v_a_out, v_b_norm, v_b_in, v_b_vnorm, v_b_ws, v_b_bs, v_b_out, v_f_norm, v_f_up, v_f_conv_w, v_f_conv_b, v_f_down, v_final_norm):
    s, d = x.shape[1], x.shape[2]
    n_ffn = f_up.shape[0]
    f2 = f_up.shape[2] * N_DEV
    me = 4 * lax.axis_index("x") + 2 * lax.axis_index("y") + lax.axis_index("c")
    x0 = x.reshape(s, d)
    target = loss_target.reshape(s, d)

    wanted = [("a_in", _cast_layer(a_in, 0, "cast_a_in"), 1),
              ("small", _pack([a_conv, b_norm, b_vnorm, f_conv_w]), 0),
              ("a_out", _cast_layer(a_out, 0, "cast_a_out"), 0),
              ("f_up0", _cast_layer(f_up, 0, "cast_f_up0"), 1), ("f_down0", _cast_layer(f_down, 0, "cast_f_down0"), 0),
              ("b_in", _cast_layer(b_in, 0, "cast_b_in"), 1), ("b_out", _cast_layer(b_out, 0, "cast_b_out"), 0),
              ("f_up1", _cast_layer(f_up, 1, "cast_f_up1"), 1), ("f_down1", _cast_layer(f_down, 1, "cast_f_down1"), 0)]
    coming, tok, h0 = {}, None, None
    for n_started, (key, shard, axis) in enumerate(wanted):
        if n_started == 2:
            tok = h0 = _rmsnorm_fwd(x0, a_norm, "mixa_norm", after=tok)
        state, tok = _gather2_start(shard, axis, f"ag_start_{key}", after=tok)
        coming[key] = (state, axis)

    def pass_on(keys, after):
        for key in keys:
            state, axis = coming[key]
            state, after = _gather2_pass(state, axis, after, f"ag_pass_{key}")
            coming[key] = (state, axis)
        return after

    def arrived(key, after):
        state, axis = coming[key]
        return _gather2_wait(state, axis, after, f"ag_wait_{key}")

    cshard = a_conv.shape[2]
    fshard = f_conv_w.shape[2]
    w_a_in = arrived("a_in", pass_on(["a_in", "small"], tok))
    small_full = arrived("small", w_a_in)
    small_rows = small_full.reshape(N_DEV, -1)
    per_dev = _unpack_rows(small_rows, [(3, cshard), (cshard,), (cshard,), (n_ffn, 3, fshard)])
    a_conv_full = per_dev[0].transpose(1, 0, 2).reshape(3, d)
    b_norm_full = per_dev[1].reshape(1, d)
    b_vnorm_full = per_dev[2].reshape(1, d)
    f_conv_w_full = per_dev[3].transpose(1, 2, 0, 3).reshape(n_ffn, 3, f2)
    bs_wide = jnp.broadcast_to(b_bs[0][:, :, None], (SG_GROUPS, CHUNK, LANES))
    ws = b_ws[0]

    w_f_up, w_f_down = {}, {}

    def ffn_forward(xin, l, pass_first, pass_early, pass_late):
        h = _rmsnorm_fwd(xin, f_norm[l:l + 1], f"ffn{l}_norm", after=pass_on(pass_first, xin))
        w_f_up[l] = arrived(f"f_up{l}", pass_on(pass_early, h))
        up_g, up_a, cv_g, cv_a, act = _ffn_up_fused(h, w_f_up[l], f_conv_w_full[l], f_conv_b[l:l + 1], f"ffn{l}_up")
        up, cv = (up_g, up_a), (cv_g, cv_a)
        w_f_down[l] = arrived(f"f_down{l}", pass_on(pass_late, act))
        xout = _matmul(act, w_f_down[l], "nn", F32, f"ffn{l}_down", resid=xin, tm_cap=512, tk_cap=act.shape[1],
                       hold_b=True)
        return xout, (h, up, act, cv)

    gb, gc, xs, cva, ya = _mixa_in_fused(h0, w_a_in, a_conv_full, "mixa_in")
    bcx = (gb, gc, xs)
    w_a_out = arrived("a_out", pass_on(["a_out"], ya))
    x1 = _matmul(ya, w_a_out, "nn", F32, "mixa_out", resid=x0)
    x2, saved0 = ffn_forward(x1, 0, ["f_up0"], ["f_down0"], ["b_in", "b_out", "f_up1", "f_down1"])
    h2 = _rmsnorm_fwd(x2, b_norm_full, "mixb_norm")
    w_b_in = arrived("b_in", h2)
    pre = _matmul(h2, w_b_in, "nn", BF16, "mixb_in")
    ug = _mixb_fwd(pre, b_vnorm_full, ws, bs_wide, "mixb_mid")
    w_b_out = arrived("b_out", ug)
    x3 = _matmul(ug, w_b_out, "nn", F32, "mixb_out", resid=x2)
    x4, saved1 = ffn_forward(x3, 1, [], [], [])
    dx4, dx4b, loss_part, g_final = _final_loss(x4, final_norm.reshape(1, d), target, "loss_head")

    def _rs_start(grad, axis, name):
        return _exchange_start(False, grad, axis, name)

    def ffn_backward(xin, l, saved, dx, dxb):
        h, up, act, cv = saved
        g_down = _matmul(act, dxb, "tn", BF16, f"ffn{l}_down_dw", tm_cap=1408)
        rs_down, tok = _rs_start(g_down, 0, f"rs_start_f_down{l}")
        dup, cwg, cwa, cbg, cba = _ffn_down_dx_fused(dxb, w_f_down[l], up, cv, f_conv_w_full[l], tok,
                                                     f"ffn{l}_down_dx")
        g_cw, g_cb = jnp.concatenate([cwg, cwa], axis=1), jnp.concatenate([cbg, cba], axis=1)
        g_up = _matmul(h, dup, "tn", BF16, f"ffn{l}_up_dw", tk_cap=4096)
        rs_up, tok = _rs_start(g_up, 1, f"rs_start_f_up{l}")
        dh = _matmul(dup, w_f_up[l], "nt", BF16, f"ffn{l}_up_dx", after=tok)
        dxin, dxinb, g_norm = _rmsnorm_bwd(xin, f_norm[l:l + 1], dh, dx, f"ffn{l}_norm_bwd")
        return dxin, dxinb, (rs_up, rs_down, g_cw, g_cb, g_norm)

    dx3, dx3b, gf1 = ffn_backward(x3, 1, saved1, dx4, dx4b)
    g_b_out = _matmul(ug, dx3b, "tn", BF16, "mixb_out_dw", tk_cap=4096)
    rs_b_out, tok = _rs_start(g_b_out, 0, "rs_start_b_out")
    dug = _matmul(dx3b, w_b_out, "nt", BF16, "mixb_out_dx", after=tok)
    dpre, g_ws, g_bs_wide, g_bvnorm = _mixb_bwd(pre, dug, b_vnorm_full, ws, bs_wide, "mixb_mid_bwd")
    g_b_in = _matmul(h2, dpre, "tn", BF16, "mixb_in_dw", tk_cap=4096)
    rs_b_in, tok = _rs_start(g_b_in, 1, "rs_start_b_in")
    dh2 = _matmul(dpre, w_b_in, "nt", BF16, "mixb_in_dx", after=tok, tk_cap=dpre.shape[1], hold_b=True)
    dx2, dx2b, g_bnorm = _rmsnorm_bwd(x2, b_norm_full, dh2, dx3, "mixb_norm_bwd")
    dx1, dx1b, gf0 = ffn_backward(x1, 0, saved0, dx2, dx2b)
    g_a_out = _matmul(ya, dx1b, "tn", BF16, "mixa_out_dw", tk_cap=4096)
    rs_a_out, tok = _rs_start(g_a_out, 0, "rs_start_a_out")
    dbcx, g_aconv = _mixa_out_dx_fused(dx1b, w_a_out, bcx, cva, a_conv_full, tok, "mixa_out_dx")
    g_a_in = _matmul(h0, dbcx, "tn", BF16, "mixa_in_dw", tk_cap=4096)
    rs_a_in, tok = _rs_start(g_a_in, 1, "rs_start_a_in")
    dh0 = _matmul(dbcx, w_a_in, "nt", BF16, "mixa_in_dx", after=tok, tm_cap=512, tk_cap=dbcx.shape[1], hold_b=True)
    grad_x, _, g_anorm = _rmsnorm_bwd(x0, a_norm, dh0, dx1, "mixa_norm_bwd")

    full_shapes = [(1, LANES), (1, d), (3, d), (1, d), (1, d), (SG_GROUPS, CHUNK, CHUNK), (SG_GROUPS, CHUNK),
                   (n_ffn, d), (n_ffn, 3, f2), (n_ffn, f2), (1, d)]
    parts = [loss_part, g_anorm, g_aconv, g_bnorm, g_bvnorm, g_ws, g_bs_wide[:, :, 0],
             jnp.concatenate([gf0[4], gf1[4]], axis=0), jnp.stack([gf0[2], gf1[2]]),
             jnp.concatenate([gf0[3], gf1[3]], axis=0), g_final]
    small_part = _pack(parts)
    small_state, small_tok = _exchange_start(True, small_part, 0, "ar_start_small", after=grad_x)

    big = {}
    for name, states, axis, w, m, v in (
            ("f_down", (gf0[1], gf1[1]), 0, f_down, m_f_down, v_f_down),
            ("f_up", (gf0[0], gf1[0]), 1, f_up, m_f_up, v_f_up),
            ("b_out", (rs_b_out,), 0, b_out, m_b_out, v_b_out), ("b_in", (rs_b_in,), 1, b_in, m_b_in, v_b_in),
            ("a_out", (rs_a_out,), 0, a_out, m_a_out, v_a_out), ("a_in", (rs_a_in,), 1, a_in, m_a_in, v_a_in)):
        recvs = [_exchange_wait(False, st, axis, small_tok, f"rs_wait_{name}{l}") for l, st in enumerate(states)]
        big[name] = _adamw_sharded(recvs, w, m, v, f"adamw_{name}")

    slots = _exchange_wait(True, small_state, 0, [res[0] for res in big.values()], "ar_wait_small")
    total = _sum_slots(slots.reshape((N_DEV,) + small_part.shape), "ar_sum_small")
    (loss_v, r_anorm, r_aconv, r_bnorm, r_bvnorm, r_ws, r_bs, r_fnorm, r_fcw, r_fcb, r_final) = _unpack(total, full_shapes)
    small_grads = [
        r_anorm,
        lax.dynamic_slice_in_dim(r_aconv, me * cshard, cshard, axis=1).reshape(a_conv.shape),
        lax.dynamic_slice_in_dim(r_bnorm, me * cshard, cshard, axis=1),
        lax.dynamic_slice_in_dim(r_bvnorm, me * cshard, cshard, axis=1),
        r_ws.reshape(b_ws.shape), r_bs.reshape(b_bs.shape), r_fnorm,
        lax.dynamic_slice_in_dim(r_fcw, me * fshard, fshard, axis=2),
        r_fcb, r_final.reshape(final_norm.shape)]
    small_w = [a_norm, a_conv, b_norm, b_vnorm, b_ws, b_bs, f_norm, f_conv_w, f_conv_b, final_norm]
    small_m = [m_a_norm, m_a_conv, m_b_norm, m_b_vnorm, m_b_ws, m_b_bs, m_f_norm, m_f_conv_w, m_f_conv_b, m_final_norm]
    small_v = [v_a_norm, v_a_conv, v_b_norm, v_b_vnorm, v_b_ws, v_b_bs, v_f_norm, v_f_conv_w, v_f_conv_b, v_final_norm]
    shapes = [w.shape for w in small_w]
    packed = _adamw_packed(_pack(small_w), _pack(small_grads), _pack(small_m), _pack(small_v), "adamw_small")
    s_delta, s_m, s_v = (_unpack(p, shapes) for p in packed)
    small_names = ["a_norm", "a_conv", "b_norm", "b_vnorm", "b_ws", "b_bs", "f_norm", "f_conv_w", "f_conv_b", "final_norm"]
    small = {nm: (small_grads[i], s_delta[i], s_m[i], s_v[i]) for i, nm in enumerate(small_names)}

    order = ["a_norm", "a_in", "a_conv", "a_out", "b_norm", "b_in", "b_vnorm", "b_ws", "b_bs", "b_out",
             "f_norm", "f_up", "f_conv_w", "f_conv_b", "f_down", "final_norm"]
    res = {nm: (big[nm] if nm in big else small[nm]) for nm in order}
    outs = [loss_v[0, 0], grad_x.reshape(x.shape)]
    for k in range(4):
        outs += [res[nm][k] for nm in order]
    return tuple(outs)


def _unpack_rows(rows, shapes):
    out, off = [], 0
    for shp in shapes:
        n = math.prod(shp)
        out.append(rows[:, off:off + n].reshape((N_DEV,) + tuple(shp)))
        off += n + (-n) % PACK_GRANULE
    return out
```

```python
import math

import jax
import jax.numpy as jnp
from jax import lax
from jax.experimental import pallas as pl
from jax.experimental.pallas import tpu as pltpu

F32 = jnp.float32
BF16 = jnp.bfloat16
MESH = pl.DeviceIdType.MESH

N_DEV = 8
RMS_EPS = 1e-5
CHUNK = 128
SG_GROUPS = 8
ADAM_LR = 0.001
ADAM_B1 = 0.9
ADAM_B2 = 0.999
ADAM_EPS = 1e-08
ADAM_WD = 0.01
ADAM_STEP = 10

LANES = 128
SLAB = 16
FUSE_STRIP = 256
FUSE_HALO = 8
VMEM_LIMIT = 56 * 1024 * 1024
PACK_GRANULE = 8 * LANES


def _pick(dim, cap, mult):
    best = None
    t = mult
    while t <= min(dim, cap):
        if dim % t == 0:
            best = t
        t += mult
    return dim if best is None else best


def _params(semantics=None):
    return pltpu.CompilerParams(dimension_semantics=semantics, vmem_limit_bytes=VMEM_LIMIT)


_DIMS = {
    "nn": (((1,), (0,)), ((), ())),
    "nt": (((1,), (1,)), ((), ())),
    "tn": (((0,), (0,)), ((), ())),
}


def _matmul(a, b, mode, out_dtype, name, resid=None, after=None, tm_cap=1024, tn_cap=1024, tk_cap=2816,
            hold_b=False):
    if mode == "nn":
        (m, k), n = a.shape, b.shape[1]
    elif mode == "nt":
        (m, k), n = a.shape, b.shape[0]
    else:
        (k, m), n = a.shape, b.shape[1]
    tm, tn, tk = _pick(m, tm_cap, LANES), _pick(n, tn_cap, LANES), _pick(k, tk_cap, LANES)
    nk = k // tk
    n_in = 2 + (resid is not None) + (after is not None)

    def body(*refs):
        a_ref, b_ref = refs[:2]
        r_ref = refs[2] if resid is not None else None
        o_ref = refs[n_in]
        prod = lax.dot_general(a_ref[...], b_ref[...], _DIMS[mode], preferred_element_type=F32)

        def finish(r):
            if r_ref is not None:
                r = r + r_ref[...]
            o_ref[...] = r.astype(out_dtype)

        if nk == 1:
            finish(prod)
            return
        acc_ref = refs[n_in + 1]
        kk = pl.program_id(2)

        @pl.when(kk == 0)
        def _():
            acc_ref[...] = prod

        @pl.when(jnp.logical_and(kk > 0, kk < nk - 1))
        def _():
            acc_ref[...] += prod

        @pl.when(kk == nk - 1)
        def _():
            finish(acc_ref[...] + prod)

    def spec(block, index):
        if hold_b:
            return pl.BlockSpec(block, lambda j, i, kk: index(i, j, kk))
        return pl.BlockSpec(block, index)

    a_spec = (spec((tk, tm), lambda i, j, kk: (kk, i)) if mode == "tn"
              else spec((tm, tk), lambda i, j, kk: (i, kk)))
    b_spec = (spec((tn, tk), lambda i, j, kk: (j, kk)) if mode == "nt"
              else spec((tk, tn), lambda i, j, kk: (kk, j)))
    o_spec = spec((tm, tn), lambda i, j, kk: (i, j))
    in_specs = [a_spec, b_spec] + ([o_spec] if resid is not None else [])
    args = (a, b) + ((resid,) if resid is not None else ())
    if after is not None:
        in_specs.append(pl.BlockSpec(memory_space=pl.ANY))
        args += (after,)
    return pl.pallas_call(
        body, name=name, grid=(n // tn, m // tm, nk) if hold_b else (m // tm, n // tn, nk),
        in_specs=in_specs, out_specs=o_spec,
        out_shape=jax.ShapeDtypeStruct((m, n), out_dtype),
        scratch_shapes=[pltpu.VMEM((tm, tn), F32)] if nk > 1 else [],
        compiler_params=_params(("parallel", "parallel", "arbitrary")),
    )(*args)


def _rms_stats(xf):
    inv = lax.rsqrt(jnp.mean(xf * xf, axis=-1, keepdims=True) + RMS_EPS)
    return inv, xf * inv


def _rmsnorm_fwd(x, g, name, after=None):
    s, d = x.shape
    tm = _pick(s, 512, SLAB)
    extra = () if after is None else (after,)

    def body(x_ref, g_ref, *rest):
        _, xhat = _rms_stats(x_ref[...])
        rest[-1][...] = (xhat * g_ref[...]).astype(BF16)

    return pl.pallas_call(
        body, name=name, grid=(s // tm,),
        in_specs=[pl.BlockSpec((tm, d), lambda i: (i, 0)), pl.BlockSpec((1, d), lambda i: (0, 0))]
        + [pl.BlockSpec(memory_space=pl.ANY)] * len(extra),
        out_specs=pl.BlockSpec((tm, d), lambda i: (i, 0)),
        out_shape=jax.ShapeDtypeStruct((s, d), BF16),
        compiler_params=_params(("parallel",)),
    )(x, g, *extra)


def _rmsnorm_bwd(x, g, dh, dx_out, name):
    s, d = x.shape
    tm = _pick(s, 256, SLAB)

    def body(x_ref, g_ref, dh_ref, dxo_ref, dxi_ref, dxib_ref, dg_ref):
        inv, xhat = _rms_stats(x_ref[...])
        dhv = dh_ref[...].astype(F32)
        dxhat = dhv * g_ref[...]
        proj = jnp.mean(dxhat * xhat, axis=-1, keepdims=True)
        dx = dxo_ref[...] + inv * (dxhat - xhat * proj)
        dxi_ref[...] = dx
        dxib_ref[...] = dx.astype(BF16)
        part = jnp.sum(dhv * xhat, axis=0, keepdims=True)

        @pl.when(pl.program_id(0) == 0)
        def _():
            dg_ref[...] = part

        @pl.when(pl.program_id(0) > 0)
        def _():
            dg_ref[...] += part

    row = pl.BlockSpec((tm, d), lambda i: (i, 0))
    vec = pl.BlockSpec((1, d), lambda i: (0, 0))
    return pl.pallas_call(
        body, name=name, grid=(s // tm,),
        in_specs=[row, vec, row, row], out_specs=[row, row, vec],
        out_shape=[jax.ShapeDtypeStruct((s, d), F32), jax.ShapeDtypeStruct((s, d), BF16),
                   jax.ShapeDtypeStruct((1, d), F32)],
        compiler_params=_params(("arbitrary",)),
    )(x, g, dh, dx_out)


def _final_loss(x, g, target, name):
    s, d = x.shape
    tm = _pick(s, 256, SLAB)

    def body(x_ref, g_ref, t_ref, dx_ref, dxb_ref, loss_ref, dg_ref):
        inv, xhat = _rms_stats(x_ref[...])
        gv = g_ref[...]
        err = xhat * gv - t_ref[...]
        loss = 0.5 * jnp.sum(jnp.mean(err * err, axis=-1, keepdims=True), axis=0, keepdims=True)
        dy = err * (1.0 / d)
        dxhat = dy * gv
        proj = jnp.mean(dxhat * xhat, axis=-1, keepdims=True)
        dx = inv * (dxhat - xhat * proj)
        dx_ref[...] = dx
        dxb_ref[...] = dx.astype(BF16)
        part = jnp.sum(dy * xhat, axis=0, keepdims=True)
        loss_row = jnp.broadcast_to(loss, (1, LANES))

        @pl.when(pl.program_id(0) == 0)
        def _():
            dg_ref[...] = part
            loss_ref[...] = loss_row

        @pl.when(pl.program_id(0) > 0)
        def _():
            dg_ref[...] += part
            loss_ref[...] += loss_row

    row = pl.BlockSpec((tm, d), lambda i: (i, 0))
    vec = pl.BlockSpec((1, d), lambda i: (0, 0))
    return pl.pallas_call(
        body, name=name, grid=(s // tm,),
        in_specs=[row, vec, row],
        out_specs=[row, row, pl.BlockSpec((1, LANES), lambda i: (0, 0)), vec],
        out_shape=[jax.ShapeDtypeStruct((s, d), F32), jax.ShapeDtypeStruct((s, d), BF16),
                   jax.ShapeDtypeStruct((1, LANES), F32), jax.ShapeDtypeStruct((1, d), F32)],
        compiler_params=_params(("arbitrary",)),
    )(x, g, target)


def _mixa_in_fused(h, w_in, wc, name):
    s, d = h.shape
    tm = _pick(s, 1024, LANES)
    tn = _pick(d, 512, FUSE_STRIP)
    nj = d // tn

    def body(h_ref, wb_ref, wg_ref, wx_ref, wc_ref, gb_ref, gc_ref, xs_ref, cv_ref, y_ref, carry):
        @pl.when(pl.program_id(1) == 0)
        def _():
            carry[...] = jnp.zeros_like(carry)

        def matmul(st):
            cols = slice(st * FUSE_STRIP, (st + 1) * FUSE_STRIP)
            return tuple(jnp.dot(h_ref[...], w_ref[:, cols], preferred_element_type=F32).astype(BF16)
                         for w_ref in (wb_ref, wg_ref, wx_ref))

        n_strips = tn // FUSE_STRIP
        parts = matmul(0)
        for st in range(n_strips):
            parts_next = matmul(st + 1) if st + 1 < n_strips else None
            cols = slice(st * FUSE_STRIP, (st + 1) * FUSE_STRIP)
            for ref, part in zip((gb_ref, gc_ref, xs_ref), parts):
                ref[:, cols] = part
            p = parts[1].astype(F32) * parts[2].astype(F32)
            ext = jnp.concatenate([carry[:, cols], p], axis=0)
            s1 = pltpu.roll(ext, 1, 0)[FUSE_HALO:, :]
            s2 = pltpu.roll(ext, 2, 0)[FUSE_HALO:, :]
            carry[:, cols] = p[tm - FUSE_HALO:, :]
            cv = wc_ref[0:1, cols] * s2 + wc_ref[1:2, cols] * s1 + wc_ref[2:3, cols] * p
            cv_ref[:, cols] = cv.astype(BF16)
            y_ref[:, cols] = (parts[0].astype(F32) * cv).astype(BF16)
            parts = parts_next

    def cols_of(rows, offset):
        return pl.BlockSpec((rows, tn), lambda j, i: (0, j + offset))

    tile = pl.BlockSpec((tm, tn), lambda j, i: (i, j))
    return pl.pallas_call(
        body, name=name, grid=(nj, s // tm),
        in_specs=[pl.BlockSpec((tm, d), lambda j, i: (i, 0)), cols_of(d, 0), cols_of(d, nj), cols_of(d, 2 * nj),
                  cols_of(3, 0)],
        out_specs=[tile] * 5, out_shape=[jax.ShapeDtypeStruct((s, d), BF16)] * 5,
        scratch_shapes=[pltpu.VMEM((FUSE_HALO, tn), F32)],
        compiler_params=_params(("parallel", "arbitrary")),
    )(h, w_in, w_in, w_in, wc)


def _mixa_out_dx_fused(dxb, w_out, bcx, cv, wc, after, name):
    s, d = dxb.shape
    tm = _pick(s, 1024, LANES)
    tn = _pick(d, 512, FUSE_STRIP)
    nj, ni = d // tn, s // tm
    n_steps = nj * ni
    sub = tm // FUSE_HALO

    def body(dx_ref, w_ref, gb_ref, gc_ref, xs_ref, cv_ref, wc_ref, after_ref,
             dbcx_hbm, dwc_ref, out_buf, out_sem, carry, acc):
        j, i = pl.program_id(0), pl.program_id(1)
        step = j * ni + i
        slot = lax.rem(step, 2)
        row0 = pl.multiple_of((ni - 1 - i) * tm, tm)

        def out_copy(part, slot_=None):
            slot_ = slot if slot_ is None else slot_
            col0 = pl.multiple_of(part * d + j * tn, LANES)
            return pltpu.make_async_copy(out_buf.at[slot_, part], dbcx_hbm.at[pl.ds(row0, tm), pl.ds(col0, tn)],
                                         out_sem.at[slot_, part])

        @pl.when(step >= 2)
        def _():
            for part in range(3):
                out_copy(part).wait()

        @pl.when(i == 0)
        def _():
            carry[...] = jnp.zeros_like(carry)
            acc[...] = jnp.zeros_like(acc)

        for st in range(tn // FUSE_STRIP):
            cols = slice(st * FUSE_STRIP, (st + 1) * FUSE_STRIP)
            dyv = lax.dot_general(dx_ref[...], w_ref[cols, :], _DIMS["nt"], preferred_element_type=F32)
            gc = gc_ref[:, cols].astype(F32)
            xs = xs_ref[:, cols].astype(F32)
            d0 = dyv * gb_ref[:, cols].astype(F32)
            ext = jnp.concatenate([d0, carry[:, cols]], axis=0)
            d1 = pltpu.roll(ext, tm + FUSE_HALO - 1, 0)[:tm, :]
            d2 = pltpu.roll(ext, tm + FUSE_HALO - 2, 0)[:tm, :]
            carry[:, cols] = d0[:FUSE_HALO, :]
            dp = wc_ref[2:3, cols] * d0 + wc_ref[1:2, cols] * d1 + wc_ref[0:1, cols] * d2
            out_buf[slot, 0, :, cols] = (dyv * cv_ref[:, cols].astype(F32)).astype(BF16)
            out_buf[slot, 1, :, cols] = (dp * xs).astype(BF16)
            out_buf[slot, 2, :, cols] = (dp * gc).astype(BF16)
            p = gc * xs
            for k, term in enumerate((d2 * p, d1 * p, d0 * p)):
                acc[k, :, cols] += jnp.sum(term.reshape(sub, FUSE_HALO, FUSE_STRIP), axis=0)

        for part in range(3):
            out_copy(part).start()

        @pl.when(i == ni - 1)
        def _():
            for k in range(3):
                dwc_ref[k:k + 1, :] = jnp.sum(acc[k], axis=0, keepdims=True)

        @pl.when(step == n_steps - 1)
        def _():
            for part in range(3):
                out_copy(part).wait()
                if n_steps > 1:
                    out_copy(part, 1 - slot).wait()

    tile = pl.BlockSpec((tm, tn), lambda j, i: (ni - 1 - i, j))
    return pl.pallas_call(
        body, name=name, grid=(nj, ni),
        in_specs=[pl.BlockSpec((tm, d), lambda j, i: (ni - 1 - i, 0)), pl.BlockSpec((tn, d), lambda j, i: (j, 0)),
                  tile, tile, tile, tile, pl.BlockSpec((3, tn), lambda j, i: (0, j)),
                  pl.BlockSpec(memory_space=pl.ANY)],
        out_specs=[pl.BlockSpec(memory_space=pl.ANY), pl.BlockSpec((3, tn), lambda j, i: (0, j))],
        out_shape=[jax.ShapeDtypeStruct((s, 3 * d), BF16), jax.ShapeDtypeStruct((3, d), F32)],
        scratch_shapes=[pltpu.VMEM((2, 3, tm, tn), BF16), pltpu.SemaphoreType.DMA((2, 3)),
                        pltpu.VMEM((FUSE_HALO, tn), F32), pltpu.VMEM((3, FUSE_HALO, tn), F32)],
        compiler_params=_params(("arbitrary", "arbitrary")),
    )(dxb, w_out, bcx[0], bcx[1], bcx[2], cv, wc, after)


def _sigmoid(z):
    return 0.5 * jnp.tanh(0.5 * z) + 0.5


def _ffn_up_fused(h, w_up, cw, cb, name):
    s, d = h.shape
    f = w_up.shape[1] // 2
    tm = _pick(s, 1024, LANES)
    tn = _pick(f, 512, FUSE_STRIP)
    nj = f // tn

    def body(h_ref, wg_ref, wa_ref, cwg_ref, cwa_ref, cbg_ref, cba_ref,
             upg_ref, upa_ref, cvg_ref, cva_ref, act_ref, carry_g, carry_a):
        @pl.when(pl.program_id(1) == 0)
        def _():
            carry_g[...] = jnp.zeros_like(carry_g)
            carry_a[...] = jnp.zeros_like(carry_a)

        def matmul(st):
            cols = slice(st * FUSE_STRIP, (st + 1) * FUSE_STRIP)
            return tuple(jnp.dot(h_ref[...], w_ref[:, cols], preferred_element_type=F32).astype(BF16)
                         for w_ref in (wg_ref, wa_ref))

        def conv(up, cw_ref, cb_ref, carry, up_ref, cv_ref, cols):
            up_ref[:, cols] = up
            x = up.astype(F32)
            ext = jnp.concatenate([carry[:, cols], x], axis=0)
            s1 = pltpu.roll(ext, 1, 0)[FUSE_HALO:, :]
            s2 = pltpu.roll(ext, 2, 0)[FUSE_HALO:, :]
            carry[:, cols] = x[tm - FUSE_HALO:, :]
            cv = cw_ref[0:1, cols] * s2 + cw_ref[1:2, cols] * s1 + cw_ref[2:3, cols] * x + cb_ref[:, cols]
            cv_ref[:, cols] = cv.astype(BF16)
            return cv

        n_strips = tn // FUSE_STRIP
        ups = matmul(0)
        for st in range(n_strips):
            ups_next = matmul(st + 1) if st + 1 < n_strips else None
            cols = slice(st * FUSE_STRIP, (st + 1) * FUSE_STRIP)
            gcv = conv(ups[0], cwg_ref, cbg_ref, carry_g, upg_ref, cvg_ref, cols)
            acv = conv(ups[1], cwa_ref, cba_ref, carry_a, upa_ref, cva_ref, cols)
            act_ref[:, cols] = (gcv * _sigmoid(gcv) * acv).astype(BF16)
            ups = ups_next

    def cols_of(rows, offset):
        return pl.BlockSpec((rows, tn), lambda j, i: (0, j + offset))

    tile = pl.BlockSpec((tm, tn), lambda j, i: (i, j))
    out = jax.ShapeDtypeStruct((s, f), BF16)
    return pl.pallas_call(
        body, name=name, grid=(nj, s // tm),
        in_specs=[pl.BlockSpec((tm, d), lambda j, i: (i, 0)), cols_of(d, 0), cols_of(d, nj),
                  cols_of(3, 0), cols_of(3, nj), cols_of(1, 0), cols_of(1, nj)],
        out_specs=[tile] * 5, out_shape=[out] * 5,
        scratch_shapes=[pltpu.VMEM((FUSE_HALO, tn), F32)] * 2,
        compiler_params=_params(("parallel", "arbitrary")),
    )(h, w_up, w_up, cw, cw, cb, cb)


def _ffn_down_dx_fused(dxb, w_down, up, cv, cw, after, name):
    s, d = dxb.shape
    f = w_down.shape[0]
    tm = _pick(s, 1024, LANES)
    tn = _pick(f, 512, FUSE_STRIP)
    nj, ni = f // tn, s // tm
    n_steps = nj * ni
    sub = tm // FUSE_HALO

    def body(dx_ref, w_ref, upg_ref, upa_ref, cvg_ref, cva_ref, cwg_ref, cwa_ref, after_ref,
             dup_hbm, dcwg_ref, dcwa_ref, dcbg_ref, dcba_ref, out_buf, out_sem, carry, acc):
        j, i = pl.program_id(0), pl.program_id(1)
        step = j * ni + i
        slot = lax.rem(step, 2)
        row0 = pl.multiple_of((ni - 1 - i) * tm, tm)

        def out_copy(half):
            col0 = pl.multiple_of(half * f + j * tn, LANES)
            return pltpu.make_async_copy(out_buf.at[slot, half], dup_hbm.at[pl.ds(row0, tm), pl.ds(col0, tn)],
                                         out_sem.at[slot, half])

        @pl.when(step >= 2)
        def _():
            for half in range(2):
                out_copy(half).wait()

        @pl.when(i == 0)
        def _():
            carry[...] = jnp.zeros_like(carry)
            acc[...] = jnp.zeros_like(acc)

        for st in range(tn // FUSE_STRIP):
            cols = slice(st * FUSE_STRIP, (st + 1) * FUSE_STRIP)
            dact = lax.dot_general(dx_ref[...], w_ref[cols, :], _DIMS["nt"], preferred_element_type=F32)
            gcv = cvg_ref[:, cols].astype(F32)
            acv = cva_ref[:, cols].astype(F32)
            sg = _sigmoid(gcv)
            dd = (dact * acv * (sg * (1.0 + gcv * (1.0 - sg))), dact * (gcv * sg))
            for half, (up_ref, cw_ref) in enumerate(((upg_ref, cwg_ref), (upa_ref, cwa_ref))):
                x = up_ref[:, cols].astype(F32)
                d0 = dd[half]
                ext = jnp.concatenate([d0, carry[half, :, cols]], axis=0)
                d1 = pltpu.roll(ext, tm + FUSE_HALO - 1, 0)[:tm, :]
                d2 = pltpu.roll(ext, tm + FUSE_HALO - 2, 0)[:tm, :]
                carry[half, :, cols] = d0[:FUSE_HALO, :]
                out_buf[slot, half, :, cols] = (cw_ref[2:3, cols] * d0 + cw_ref[1:2, cols] * d1
                                                + cw_ref[0:1, cols] * d2).astype(BF16)
                for k, term in enumerate((d2 * x, d1 * x, d0 * x, d0)):
                    acc[half, k, :, cols] += jnp.sum(term.reshape(sub, FUSE_HALO, FUSE_STRIP), axis=0)

        for half in range(2):
            out_copy(half).start()

        @pl.when(i == ni - 1)
        def _():
            for half, (dcw_ref, dcb_ref) in enumerate(((dcwg_ref, dcbg_ref), (dcwa_ref, dcba_ref))):
                for k in range(3):
                    dcw_ref[k:k + 1, :] = jnp.sum(acc[half, k], axis=0, keepdims=True)
                dcb_ref[...] = jnp.sum(acc[half, 3], axis=0, keepdims=True)

        @pl.when(step == n_steps - 1)
        def _():
            for half in range(2):
                out_copy(half).wait()
                if n_steps > 1:
                    pltpu.make_async_copy(out_buf.at[1 - slot, half], dup_hbm.at[pl.ds(row0, tm), pl.ds(0, tn)],
                                          out_sem.at[1 - slot, half]).wait()

    tile = pl.BlockSpec((tm, tn), lambda j, i: (ni - 1 - i, j))

    def cols_of(rows, offset):
        return pl.BlockSpec((rows, tn), lambda j, i: (0, j + offset))

    small = pl.BlockSpec((3, tn), lambda j, i: (0, j)), pl.BlockSpec((1, tn), lambda j, i: (0, j))
    return pl.pallas_call(
        body, name=name, grid=(nj, ni),
        in_specs=[pl.BlockSpec((tm, d), lambda j, i: (ni - 1 - i, 0)), pl.BlockSpec((tn, d), lambda j, i: (j, 0)),
                  tile, tile, tile, tile, cols_of(3, 0), cols_of(3, nj), pl.BlockSpec(memory_space=pl.ANY)],
        out_specs=[pl.BlockSpec(memory_space=pl.ANY), small[0], small[0], small[1], small[1]],
        out_shape=[jax.ShapeDtypeStruct((s, 2 * f), BF16), jax.ShapeDtypeStruct((3, f), F32),
                   jax.ShapeDtypeStruct((3, f), F32), jax.ShapeDtypeStruct((1, f), F32),
                   jax.ShapeDtypeStruct((1, f), F32)],
        scratch_shapes=[pltpu.VMEM((2, 2, tm, tn), BF16), pltpu.SemaphoreType.DMA((2, 2)),
                        pltpu.VMEM((2, FUSE_HALO, tn), F32), pltpu.VMEM((2, 4, FUSE_HALO, tn), F32)],
        compiler_params=_params(("arbitrary", "arbitrary")),
    )(dxb, w_down, up[0], up[1], cv[0], cv[1], cw, cw, after)


_GELU_C = math.sqrt(2.0 / math.pi)


def _gelu(x):
    th = jnp.tanh(_GELU_C * (x + 0.044715 * (x * x * x)))
    return x * (0.5 * (1.0 + th)), th


def _gelu_grad(x, th):
    return 0.5 * (1.0 + th) + 0.5 * x * (1.0 - th * th) * (_GELU_C * (1.0 + 3.0 * 0.044715 * (x * x)))


def _masked_ws(ws_ref, h):
    t = lax.broadcasted_iota(jnp.int32, (CHUNK, CHUNK), 0)
    sx = lax.broadcasted_iota(jnp.int32, (CHUNK, CHUNK), 1)
    return jnp.where(sx <= t, ws_ref[h], 0.0)


def _mixb_fwd(pre, gv, ws, bs_wide, name):
    s, w2 = pre.shape
    w = w2 // 2
    gw = w // SG_GROUPS

    def body(pre_ref, gv_ref, ws_ref, bs_ref, o_ref):
        zu, _ = _gelu(pre_ref[:, :w].astype(F32))
        zv, _ = _gelu(pre_ref[:, w:].astype(F32))
        _, vhat = _rms_stats(zv)
        vn = (vhat * gv_ref[...]).astype(BF16)
        for h in range(SG_GROUPS):
            cols = slice(h * gw, (h + 1) * gw)
            wsm = _masked_ws(ws_ref, h).astype(BF16)
            gate = jnp.dot(wsm, vn[:, cols], preferred_element_type=F32)
            gate = gate + jnp.tile(bs_ref[h], (1, gw // LANES))
            o_ref[:, cols] = (zu[:, cols] * gate).astype(BF16)

    return pl.pallas_call(
        body, name=name, grid=(s // CHUNK,),
        in_specs=[pl.BlockSpec((CHUNK, w2), lambda i: (i, 0)), pl.BlockSpec((1, w), lambda i: (0, 0)),
                  pl.BlockSpec((SG_GROUPS, CHUNK, CHUNK), lambda i: (0, 0, 0)),
                  pl.BlockSpec((SG_GROUPS, CHUNK, LANES), lambda i: (0, 0, 0))],
        out_specs=pl.BlockSpec((CHUNK, w), lambda i: (i, 0)),
        out_shape=jax.ShapeDtypeStruct((s, w), BF16),
        compiler_params=_params(("parallel",)),
    )(pre, gv, ws, bs_wide)


def _mixb_bwd(pre, dug, gv, ws, bs_wide, name):
    s, w2 = pre.shape
    w = w2 // 2
    gw = w // SG_GROUPS

    def body(pre_ref, dug_ref, gv_ref, ws_ref, bs_ref, o_ref, dws_ref, dbs_ref, dgv_ref, dvn_ref):
        first = pl.program_id(0) == 0

        @pl.when(first)
        def _():
            dws_ref[...] = jnp.zeros_like(dws_ref)
            dbs_ref[...] = jnp.zeros_like(dbs_ref)

        pu = pre_ref[:, :w].astype(F32)
        pv = pre_ref[:, w:].astype(F32)
        zu, thu = _gelu(pu)
        zv, thv = _gelu(pv)
        inv, vhat = _rms_stats(zv)
        gvv = gv_ref[...]
        vn = (vhat * gvv).astype(BF16)
        for h in range(SG_GROUPS):
            cols = slice(h * gw, (h + 1) * gw)
            wsm = _masked_ws(ws_ref, h).astype(BF16)
            gate = jnp.dot(wsm, vn[:, cols], preferred_element_type=F32)
            gate = gate + jnp.tile(bs_ref[h], (1, gw // LANES))
            dug_h = dug_ref[:, cols].astype(F32)
            dgate = dug_h * zu[:, cols]
            dgate_b = dgate.astype(BF16)
            o_ref[:, cols] = (dug_h * gate * _gelu_grad(pu[:, cols], thu[:, cols])).astype(BF16)
            dbs_ref[h] += jnp.broadcast_to(jnp.sum(dgate, axis=-1, keepdims=True), (CHUNK, LANES))
            dws = lax.dot_general(dgate_b, vn[:, cols], _DIMS["nt"], preferred_element_type=F32)
            t = lax.broadcasted_iota(jnp.int32, (CHUNK, CHUNK), 0)
            sx = lax.broadcasted_iota(jnp.int32, (CHUNK, CHUNK), 1)
            dws_ref[h] += jnp.where(sx <= t, dws, 0.0)
            dvn_ref[:, cols] = lax.dot_general(wsm, dgate_b, _DIMS["tn"], preferred_element_type=F32)
        dvn = dvn_ref[...]
        part = jnp.sum(dvn * vhat, axis=0, keepdims=True)

        @pl.when(first)
        def _():
            dgv_ref[...] = part

        @pl.when(jnp.logical_not(first))
        def _():
            dgv_ref[...] += part

        dvhat = dvn * gvv
        dzv = inv * (dvhat - vhat * jnp.mean(dvhat * vhat, axis=-1, keepdims=True))
        o_ref[:, w:] = (dzv * _gelu_grad(pv, thv)).astype(BF16)

    return pl.pallas_call(
        body, name=name, grid=(s // CHUNK,),
        in_specs=[pl.BlockSpec((CHUNK, w2), lambda i: (i, 0)), pl.BlockSpec((CHUNK, w), lambda i: (i, 0)),
                  pl.BlockSpec((1, w), lambda i: (0, 0)),
                  pl.BlockSpec((SG_GROUPS, CHUNK, CHUNK), lambda i: (0, 0, 0)),
                  pl.BlockSpec((SG_GROUPS, CHUNK, LANES), lambda i: (0, 0, 0))],
        out_specs=[pl.BlockSpec((CHUNK, w2), lambda i: (i, 0)),
                   pl.BlockSpec((SG_GROUPS, CHUNK, CHUNK), lambda i: (0, 0, 0)),
                   pl.BlockSpec((SG_GROUPS, CHUNK, LANES), lambda i: (0, 0, 0)),
                   pl.BlockSpec((1, w), lambda i: (0, 0))],
        out_shape=[jax.ShapeDtypeStruct((s, w2), BF16), jax.ShapeDtypeStruct((SG_GROUPS, CHUNK, CHUNK), F32),
                   jax.ShapeDtypeStruct((SG_GROUPS, CHUNK, LANES), F32), jax.ShapeDtypeStruct((1, w), F32)],
        scratch_shapes=[pltpu.VMEM((CHUNK, w), F32)],
        compiler_params=_params(("arbitrary",)),
    )(pre, dug, gv, ws, bs_wide)


def _cast_layer(w3, layer, name):
    _, r, c = w3.shape
    tr = _pick(r, 256, SLAB)

    def body(w_ref, o_ref):
        o_ref[...] = w_ref[...].astype(BF16)

    return pl.pallas_call(
        body, name=name, grid=(r // tr,),
        in_specs=[pl.BlockSpec((None, tr, c), lambda i: (layer, i, 0))],
        out_specs=pl.BlockSpec((tr, c), lambda i: (i, 0)),
        out_shape=jax.ShapeDtypeStruct((r, c), BF16),
        compiler_params=_params(("parallel",)),
    )(w3)


def _adamw_math(w, g, m, v):
    m = ADAM_B1 * m + (1.0 - ADAM_B1) * g
    v = ADAM_B2 * v + (1.0 - ADAM_B2) * (g * g)
    m_hat = m / (1.0 - ADAM_B1 ** ADAM_STEP)
    v_hat = v / (1.0 - ADAM_B2 ** ADAM_STEP)
    delta = -ADAM_LR * (m_hat / (jnp.sqrt(v_hat) + ADAM_EPS) + ADAM_WD * w)
    return delta, m, v


def _adamw_sharded(recvs, w, m, v, name):
    nl, r, c = w.shape
    tc = _pick(c, 1536, LANES)
    tr = _pick(r, 64, SLAB)

    def body(*refs):
        recv_refs = refs[:nl]
        w_ref, m_ref, v_ref, g_ref, d_ref, nm_ref, nv_ref = refs[nl:]
        for layer, recv_ref in enumerate(recv_refs):
            @pl.when(pl.program_id(0) == layer)
            def _():
                g = recv_ref[0].astype(F32)
                for q in range(1, N_DEV):
                    g = g + recv_ref[q].astype(F32)
                delta, nm, nv = _adamw_math(w_ref[...], g, m_ref[...], v_ref[...])
                g_ref[...] = g
                d_ref[...] = delta
                nm_ref[...] = nm
                nv_ref[...] = nv

    def recv_spec(layer):
        return pl.BlockSpec((N_DEV, tr, tc),
                            lambda l, i, j: (0, jnp.where(l == layer, i, 0), jnp.where(l == layer, j, 0)))

    blk = pl.BlockSpec((None, tr, tc), lambda l, i, j: (l, i, j))
    out = jax.ShapeDtypeStruct((nl, r, c), F32)
    return pl.pallas_call(
        body, name=name, grid=(nl, r // tr, c // tc),
        in_specs=[recv_spec(layer) for layer in range(nl)] + [blk, blk, blk],
        out_specs=[blk] * 4, out_shape=[out] * 4,
        compiler_params=_params(("parallel",) * 3),
    )(*recvs, w, m, v)


def _adamw_packed(w, g, m, v, name):
    r, c = w.shape
    tr = _pick(r, 256, 8)

    def body(w_ref, g_ref, m_ref, v_ref, d_ref, nm_ref, nv_ref):
        delta, nm, nv = _adamw_math(w_ref[...], g_ref[...], m_ref[...], v_ref[...])
        d_ref[...] = delta
        nm_ref[...] = nm
        nv_ref[...] = nv

    blk = pl.BlockSpec((tr, c), lambda i: (i, 0))
    out = jax.ShapeDtypeStruct((r, c), F32)
    return pl.pallas_call(
        body, name=name, grid=(r // tr,), in_specs=[blk] * 4, out_specs=[blk] * 3, out_shape=[out] * 3,
        compiler_params=_params(("parallel",)),
    )(w, g, m, v)


def _pack(arrays):
    parts = []
    for a in arrays:
        flat = a.reshape(-1).astype(F32)
        pad = (-flat.shape[0]) % PACK_GRANULE
        parts.append(jnp.pad(flat, (0, pad)) if pad else flat)
    return jnp.concatenate(parts).reshape(-1, LANES)


def _unpack(buf, shapes):
    flat = buf.reshape(-1)
    out, off = [], 0
    for shp in shapes:
        n = math.prod(shp)
        out.append(flat[off:off + n].reshape(shp))
        off += n + (-n) % PACK_GRANULE
    return out


def _mesh_pos():
    return lax.axis_index("x"), lax.axis_index("y"), lax.axis_index("c")


def _coords(q):
    return q // 4, (q // 2) % 2, q % 2


def _shard_of(ref, q, shard_shape, axis):
    r, c = shard_shape
    if axis == 0:
        return ref.at[pl.ds(pl.multiple_of(q * r, SLAB), r), :]
    return ref.at[:, pl.ds(pl.multiple_of(q * c, LANES), c)]


_HBM = pl.BlockSpec(memory_space=pltpu.HBM)
_SEM = pl.BlockSpec(memory_space=pltpu.SEMAPHORE)
_EFFECT = pltpu.SideEffectType.DATAFLOW_SIDE_EFFECTING


def _exchange_shapes(gather, src_shape, axis):
    r, c = src_shape
    if gather:
        return (r, c), ((r * N_DEV, c) if axis == 0 else (r, c * N_DEV))
    shard = (r // N_DEV, c) if axis == 0 else (r, c // N_DEV)
    return shard, (N_DEV,) + shard


def _exchange_copies(gather, src, land, sems, axis):
    send_sems, recv_sems, own_sem = sems
    x, y, c_ = _mesh_pos()
    me = 4 * x + 2 * y + c_
    shard, _ = _exchange_shapes(gather, src.shape, axis)

    def piece(q):
        return src if gather else _shard_of(src, q, shard, axis)

    def place(q):
        return _shard_of(land, q, shard, axis) if gather else land.at[q]

    own = pltpu.make_async_copy(piece(me), place(me), own_sem.at[0])
    sends, arrivals = [], []
    for step in range(1, N_DEV):
        to = (me + step) % N_DEV
        frm = (me + N_DEV - step) % N_DEV
        sends.append(pltpu.make_async_remote_copy(
            src_ref=piece(to), dst_ref=place(me), send_sem=send_sems.at[step - 1], recv_sem=recv_sems.at[step - 1],
            device_id=_coords(to), device_id_type=MESH))
        arrivals.append(pltpu.make_async_remote_copy(
            src_ref=piece(me), dst_ref=place(frm), send_sem=send_sems.at[step - 1], recv_sem=recv_sems.at[step - 1],
            device_id=_coords(frm), device_id_type=MESH))
    return own, sends, arrivals


def _exchange_start(gather, src, axis, name, after=None):
    _, land_shape = _exchange_shapes(gather, src.shape, axis)
    extra = () if after is None else (after,)

    def body(*refs):
        src_ref, land = refs[:2]
        send_sems, recv_sems, own_sem = refs[2 + len(extra):5 + len(extra)]
        own, sends, _ = _exchange_copies(gather, src_ref, land, (send_sems, recv_sems, own_sem), axis)
        own.start()
        for cp in sends:
            cp.start()
        refs[-1][...] = jnp.zeros_like(refs[-1])

    out = pl.pallas_call(
        body, name=name,
        out_shape=(pltpu.SemaphoreType.DMA((N_DEV - 1,)), pltpu.SemaphoreType.DMA((N_DEV - 1,)),
                   pltpu.SemaphoreType.DMA((1,)), pltpu.HBM(src.shape, src.dtype),
                   pltpu.HBM(land_shape, src.dtype), jax.ShapeDtypeStruct((8, LANES), F32)),
        in_specs=[_HBM, _HBM] + [pl.BlockSpec(memory_space=pl.ANY)] * len(extra),
        out_specs=(_SEM, _SEM, _SEM, _HBM, _HBM, pl.BlockSpec(memory_space=pltpu.VMEM)),
        input_output_aliases={0: 3, 1: 4},
        compiler_params=pltpu.CompilerParams(has_side_effects=_EFFECT),
    )(pltpu.with_memory_space_constraint(src, pltpu.HBM),
      pltpu.with_memory_space_constraint(lax.empty(land_shape, src.dtype), pltpu.HBM), *extra)
    return out[:5], out[5]


def _exchange_wait(gather, state, axis, after, name):
    send_sems, recv_sems, own_sem, src_thru, land_thru = state
    after = tuple(after) if isinstance(after, (tuple, list)) else (after,)

    def body(src, land, send_sems, recv_sems, own_sem, *rest):
        own, sends, arrivals = _exchange_copies(gather, src, land, (send_sems, recv_sems, own_sem), axis)
        for cp in sends:
            cp.wait_send()
        for cp in arrivals:
            cp.wait_recv()
        own.wait()

    return pl.pallas_call(
        body, name=name,
        out_shape=(pltpu.HBM(src_thru.shape, src_thru.dtype), pltpu.HBM(land_thru.shape, land_thru.dtype)),
        in_specs=[_HBM, _HBM, _SEM, _SEM, _SEM] + [pl.BlockSpec(memory_space=pl.ANY)] * len(after),
        out_specs=(_HBM, _HBM),
        input_output_aliases={0: 0, 1: 1},
        compiler_params=pltpu.CompilerParams(has_side_effects=_EFFECT),
    )(src_thru, land_thru, send_sems, recv_sems, own_sem, *after)[1]


def _gather2_copies(shard_ref, land, sems, axis, shard_shape):
    send1, recv1, own_sem, send2, recv2 = sems
    x, y, c = _mesh_pos()
    me, sibling = (x, y, c), (x, y, 1 - c)
    chips = [(1 - x, y), (x, 1 - y), (1 - x, 1 - y)]

    def region(dev):
        px, py, pc = dev
        return _shard_of(land, 4 * px + 2 * py + pc, shard_shape, axis)

    def copy(src, block, to, send, recv):
        return pltpu.make_async_remote_copy(src_ref=src, dst_ref=region(block), send_sem=send, recv_sem=recv,
                                            device_id=to, device_id_type=MESH)

    own = pltpu.make_async_copy(shard_ref, region(me), own_sem.at[0])
    peers = [sibling] + [(*chip, c) for chip in chips]
    sends1 = [copy(shard_ref, me, to, send1.at[k], recv1.at[k]) for k, to in enumerate(peers)]
    arrivals1 = [copy(shard_ref, frm, frm, send1.at[k], recv1.at[k]) for k, frm in enumerate(peers)]
    sends2, arrivals2 = [], []
    if send2 is not None:
        for j, chip in enumerate(chips):
            sends2.append(copy(region((*chip, c)), (*chip, c), sibling, send2.at[j], recv2.at[j]))
            arrivals2.append(copy(region((*chip, 1 - c)), (*chip, 1 - c), sibling, send2.at[j], recv2.at[j]))
    return own, sends1, arrivals1, sends2, arrivals2


def _gather2_start(shard, axis, name, after=None):
    _, land_shape = _exchange_shapes(True, shard.shape, axis)
    extra = () if after is None else (after,)

    def body(*refs):
        src_ref, land = refs[:2]
        send1, recv1, own_sem = refs[2 + len(extra):5 + len(extra)]
        own, sends1, _, _, _ = _gather2_copies(src_ref, land, (send1, recv1, own_sem, None, None), axis, shard.shape)
        own.start()
        for cp in sends1[1:] + sends1[:1]:
            cp.start()
        refs[-1][...] = jnp.zeros_like(refs[-1])

    out = pl.pallas_call(
        body, name=name,
        out_shape=(pltpu.SemaphoreType.DMA((4,)), pltpu.SemaphoreType.DMA((4,)), pltpu.SemaphoreType.DMA((1,)),
                   pltpu.HBM(shard.shape, shard.dtype), pltpu.HBM(land_shape, shard.dtype),
                   jax.ShapeDtypeStruct((8, LANES), F32)),
        in_specs=[_HBM, _HBM] + [pl.BlockSpec(memory_space=pl.ANY)] * len(extra),
        out_specs=(_SEM, _SEM, _SEM, _HBM, _HBM, pl.BlockSpec(memory_space=pltpu.VMEM)),
        input_output_aliases={0: 3, 1: 4},
        compiler_params=pltpu.CompilerParams(has_side_effects=_EFFECT),
    )(pltpu.with_memory_space_constraint(shard, pltpu.HBM),
      pltpu.with_memory_space_constraint(lax.empty(land_shape, shard.dtype), pltpu.HBM), *extra)
    return out[:5], out[5]


def _gather2_pass(state, axis, after, name):
    send1, recv1, own_sem, shard_thru, land_thru = state

    def body(src_ref, land, send1, recv1, own_sem, after_ref, send2, recv2, src_out, land_out, token):
        _, _, arrivals1, sends2, _ = _gather2_copies(src_ref, land, (send1, recv1, own_sem, send2, recv2), axis,
                                                     shard_thru.shape)
        for arrival, fwd in zip(arrivals1[1:], sends2):
            arrival.wait_recv()
            fwd.start()
        token[...] = jnp.zeros_like(token)

    out = pl.pallas_call(
        body, name=name,
        out_shape=(pltpu.SemaphoreType.DMA((3,)), pltpu.SemaphoreType.DMA((3,)),
                   pltpu.HBM(shard_thru.shape, shard_thru.dtype), pltpu.HBM(land_thru.shape, land_thru.dtype),
                   jax.ShapeDtypeStruct((8, LANES), F32)),
        in_specs=[_HBM, _HBM, _SEM, _SEM, _SEM, pl.BlockSpec(memory_space=pl.ANY)],
        out_specs=(_SEM, _SEM, _HBM, _HBM, pl.BlockSpec(memory_space=pltpu.VMEM)),
        input_output_aliases={0: 2, 1: 3},
        compiler_params=pltpu.CompilerParams(has_side_effects=_EFFECT),
    )(shard_thru, land_thru, send1, recv1, own_sem, after)
    return (send1, recv1, own_sem, out[0], out[1], out[2], out[3]), out[4]


def _gather2_wait(state, axis, after, name):
    send1, recv1, own_sem, send2, recv2, shard_thru, land_thru = state

    def body(src_ref, land, send1, recv1, own_sem, send2, recv2, after_ref, src_dead, got):
        own, sends1, arrivals1, sends2, arrivals2 = _gather2_copies(
            src_ref, land, (send1, recv1, own_sem, send2, recv2), axis, shard_thru.shape)
        for cp in sends1 + sends2:
            cp.wait_send()
        for cp in arrivals1[:1] + arrivals2:
            cp.wait_recv()
        own.wait()

    return pl.pallas_call(
        body, name=name,
        out_shape=(pltpu.HBM(shard_thru.shape, shard_thru.dtype), pltpu.HBM(land_thru.shape, land_thru.dtype)),
        in_specs=[_HBM, _HBM] + [_SEM] * 5 + [pl.BlockSpec(memory_space=pl.ANY)],
        out_specs=(_HBM, _HBM),
        input_output_aliases={0: 0, 1: 1},
        compiler_params=pltpu.CompilerParams(has_side_effects=_EFFECT),
    )(shard_thru, land_thru, send1, recv1, own_sem, send2, recv2, after)[1]


def _sum_slots(slots, name):
    _, r, c = slots.shape
    tr = _pick(r, 512, 8)

    def body(s_ref, o_ref):
        total = s_ref[0]
        for q in range(1, N_DEV):
            total = total + s_ref[q]
        o_ref[...] = total

    return pl.pallas_call(
        body, name=name, grid=(r // tr,),
        in_specs=[pl.BlockSpec((N_DEV, tr, c), lambda i: (0, i, 0))],
        out_specs=pl.BlockSpec((tr, c), lambda i: (i, 0)),
        out_shape=jax.ShapeDtypeStruct((r, c), F32),
        compiler_params=_params(("parallel",)),
    )(slots)


def kernel(x, a_norm, a_in, a_conv, a_out, b_norm, b_in, b_vnorm, b_ws, b_bs, b_out, f_norm, f_up, f_conv_w, f_conv_b, f_down, final_norm, loss_target, m_a_norm, m_a_in, m_a_conv, m_a_out, m_b_norm, m_b_in, m_b_vnorm, m_b_ws, m_b_bs, m_b_out, m_f_norm, m_f_up, m_f_conv_w, m_f_conv_b, m_f_down, m_final_norm, v_a_norm, v_a_in, v_a_conv, v_a_out, v_b_norm, v_b_in, v_b_vnorm, v_b_ws, v_b_bs, v_b_out, v_f_norm, v_f_up, v_f_conv_w, v_f_conv_b, v_f_down, v_final_norm):
    s, d = x.shape[1], x.shape[2]
    n_ffn = f_up.shape[0]
    f2 = f_up.shape[2] * N_DEV
    me = 4 * lax.axis_index("x") + 2 * lax.axis_index("y") + lax.axis_index("c")
    x0 = x.reshape(s, d)
    target = loss_target.reshape(s, d)

    wanted = [("a_in", _cast_layer(a_in, 0, "cast_a_in"), 1),
              ("small", _pack([a_conv, b_norm, b_vnorm, f_conv_w]), 0),
              ("a_out", _cast_layer(a_out, 0, "cast_a_out"), 0),
              ("f_up0", _cast_layer(f_up, 0, "cast_f_up0"), 1), ("f_down0", _cast_layer(f_down, 0, "cast_f_down0"), 0),
              ("b_in", _cast_layer(b_in, 0, "cast_b_in"), 1), ("b_out", _cast_layer(b_out, 0, "cast_b_out"), 0),
              ("f_up1", _cast_layer(f_up, 1, "cast_f_up1"), 1), ("f_down1", _cast_layer(f_down, 1, "cast_f_down1"), 0)]
    coming, tok, h0 = {}, None, None
    for n_started, (key, shard, axis) in enumerate(wanted):
        if n_started == 2:
            tok = h0 = _rmsnorm_fwd(x0, a_norm, "mixa_norm", after=tok)
        state, tok = _gather2_start(shard, axis, f"ag_start_{key}", after=tok)
        coming[key] = (state, axis)

    def pass_on(keys, after):
        for key in keys:
            state, axis = coming[key]
            state, after = _gather2_pass(state, axis, after, f"ag_pass_{key}")
            coming[key] = (state, axis)
        return after

    def arrived(key, after):
        state, axis = coming[key]
        return _gather2_wait(state, axis, after, f"ag_wait_{key}")

    cshard = a_conv.shape[2]
    fshard = f_conv_w.shape[2]
    w_a_in = arrived("a_in", pass_on(["a_in", "small"], tok))
    small_full = arrived("small", w_a_in)
    small_rows = small_full.reshape(N_DEV, -1)
    per_dev = _unpack_rows(small_rows, [(3, cshard), (cshard,), (cshard,), (n_ffn, 3, fshard)])
    a_conv_full = per_dev[0].transpose(1, 0, 2).reshape(3, d)
    b_norm_full = per_dev[1].reshape(1, d)
    b_vnorm_full = per_dev[2].reshape(1, d)
    f_conv_w_full = per_dev[3].transpose(1, 2, 0, 3).reshape(n_ffn, 3, f2)
    bs_wide = jnp.broadcast_to(b_bs[0][:, :, None], (SG_GROUPS, CHUNK, LANES))
    ws = b_ws[0]

    w_f_up, w_f_down = {}, {}

    def ffn_forward(xin, l, pass_first, pass_early, pass_late):
        h = _rmsnorm_fwd(xin, f_norm[l:l + 1], f"ffn{l}_norm", after=pass_on(pass_first, xin))
        w_f_up[l] = arrived(f"f_up{l}", pass_on(pass_early, h))
        up_g, up_a, cv_g, cv_a, act = _ffn_up_fused(h, w_f_up[l], f_conv_w_full[l], f_conv_b[l:l + 1], f"ffn{l}_up")
        up, cv = (up_g, up_a), (cv_g, cv_a)
        w_f_down[l] = arrived(f"f_down{l}", pass_on(pass_late, act))
        xout = _matmul(act, w_f_down[l], "nn", F32, f"ffn{l}_down", resid=xin, tm_cap=512, tk_cap=act.shape[1],
                       hold_b=True)
        return xout, (h, up, act, cv)

    gb, gc, xs, cva, ya = _mixa_in_fused(h0, w_a_in, a_conv_full, "mixa_in")
    bcx = (gb, gc, xs)
    w_a_out = arrived("a_out", pass_on(["a_out"], ya))
    x1 = _matmul(ya, w_a_out, "nn", F32, "mixa_out", resid=x0)
    x2, saved0 = ffn_forward(x1, 0, ["f_up0"], ["f_down0"], ["b_in", "b_out", "f_up1", "f_down1"])
    h2 = _rmsnorm_fwd(x2, b_norm_full, "mixb_norm")
    w_b_in = arrived("b_in", h2)
    pre = _matmul(h2, w_b_in, "nn", BF16, "mixb_in")
    ug = _mixb_fwd(pre, b_vnorm_full, ws, bs_wide, "mixb_mid")
    w_b_out = arrived("b_out", ug)
    x3 = _matmul(ug, w_b_out, "nn", F32, "mixb_out", resid=x2)
    x4, saved1 = ffn_forward(x3, 1, [], [], [])
    dx4, dx4b, loss_part, g_final = _final_loss(x4, final_norm.reshape(1, d), target, "loss_head")

    def _rs_start(grad, axis, name):
        return _exchange_start(False, grad, axis, name)

    whole_k = dict(tm_cap=512, tn_cap=512, tk_cap=s, hold_b=True)

    def ffn_backward(xin, l, saved, dx, dxb):
        h, up, act, cv = saved
        g_down = _matmul(act, dxb, "tn", BF16, f"ffn{l}_down_dw", **whole_k)
        rs_down, tok = _rs_start(g_down, 0, f"rs_start_f_down{l}")
        dup, cwg, cwa, cbg, cba = _ffn_down_dx_fused(dxb, w_f_down[l], up, cv, f_conv_w_full[l], tok,
                                                     f"ffn{l}_down_dx")
        g_cw, g_cb = jnp.concatenate([cwg, cwa], axis=1), jnp.concatenate([cbg, cba], axis=1)
        g_up = _matmul(h, dup, "tn", BF16, f"ffn{l}_up_dw", tk_cap=4096)
        rs_up, tok = _rs_start(g_up, 1, f"rs_start_f_up{l}")
        dh = _matmul(dup, w_f_up[l], "nt", BF16, f"ffn{l}_up_dx", after=tok, tm_cap=512, tn_cap=512,
                     tk_cap=dup.shape[1], hold_b=True)
        dxin, dxinb, g_norm = _rmsnorm_bwd(xin, f_norm[l:l + 1], dh, dx, f"ffn{l}_norm_bwd")
        return dxin, dxinb, (rs_up, rs_down, g_cw, g_cb, g_norm)

    dx3, dx3b, gf1 = ffn_backward(x3, 1, saved1, dx4, dx4b)
    g_b_out = _matmul(ug, dx3b, "tn", BF16, "mixb_out_dw", **whole_k)
    rs_b_out, tok = _rs_start(g_b_out, 0, "rs_start_b_out")
    dug = _matmul(dx3b, w_b_out, "nt", BF16, "mixb_out_dx", after=tok)
    dpre, g_ws, g_bs_wide, g_bvnorm = _mixb_bwd(pre, dug, b_vnorm_full, ws, bs_wide, "mixb_mid_bwd")
    g_b_in = _matmul(h2, dpre, "tn", BF16, "mixb_in_dw", **whole_k)
    rs_b_in, tok = _rs_start(g_b_in, 1, "rs_start_b_in")
    dh2 = _matmul(dpre, w_b_in, "nt", BF16, "mixb_in_dx", after=tok, tk_cap=dpre.shape[1], hold_b=True)
    dx2, dx2b, g_bnorm = _rmsnorm_bwd(x2, b_norm_full, dh2, dx3, "mixb_norm_bwd")
    dx1, dx1b, gf0 = ffn_backward(x1, 0, saved0, dx2, dx2b)
    g_a_out = _matmul(ya, dx1b, "tn", BF16, "mixa_out_dw", **whole_k)
    rs_a_out, tok = _rs_start(g_a_out, 0, "rs_start_a_out")
    dbcx, g_aconv = _mixa_out_dx_fused(dx1b, w_a_out, bcx, cva, a_conv_full, tok, "mixa_out_dx")
    g_a_in = _matmul(h0, dbcx, "tn", BF16, "mixa_in_dw", **whole_k)
    rs_a_in, tok = _rs_start(g_a_in, 1, "rs_start_a_in")
    dh0 = _matmul(dbcx, w_a_in, "nt", BF16, "mixa_in_dx", after=tok, tm_cap=512, tk_cap=dbcx.shape[1], hold_b=True)
    grad_x, _, g_anorm = _rmsnorm_bwd(x0, a_norm, dh0, dx1, "mixa_norm_bwd")

    full_shapes = [(1, LANES), (1, d), (3, d), (1, d), (1, d), (SG_GROUPS, CHUNK, CHUNK), (SG_GROUPS, CHUNK),
                   (n_ffn, d), (n_ffn, 3, f2), (n_ffn, f2), (1, d)]
    parts = [loss_part, g_anorm, g_aconv, g_bnorm, g_bvnorm, g_ws, g_bs_wide[:, :, 0],
             jnp.concatenate([gf0[4], gf1[4]], axis=0), jnp.stack([gf0[2], gf1[2]]),
             jnp.concatenate([gf0[3], gf1[3]], axis=0), g_final]
    small_part = _pack(parts)
    small_state, small_tok = _exchange_start(True, small_part, 0, "ar_start_small", after=grad_x)

    big = {}
    for name, states, axis, w, m, v in (
            ("f_down", (gf0[1], gf1[1]), 0, f_down, m_f_down, v_f_down),
            ("f_up", (gf0[0], gf1[0]), 1, f_up, m_f_up, v_f_up),
            ("b_out", (rs_b_out,), 0, b_out, m_b_out, v_b_out), ("b_in", (rs_b_in,), 1, b_in, m_b_in, v_b_in),
            ("a_out", (rs_a_out,), 0, a_out, m_a_out, v_a_out), ("a_in", (rs_a_in,), 1, a_in, m_a_in, v_a_in)):
        recvs = [_exchange_wait(False, st, axis, small_tok, f"rs_wait_{name}{l}") for l, st in enumerate(states)]
        big[name] = _adamw_sharded(recvs, w, m, v, f"adamw_{name}")

    slots = _exchange_wait(True, small_state, 0, [res[0] for res in big.values()], "ar_wait_small")
    total = _sum_slots(slots.reshape((N_DEV,) + small_part.shape), "ar_sum_small")
    (loss_v, r_anorm, r_aconv, r_bnorm, r_bvnorm, r_ws, r_bs, r_fnorm, r_fcw, r_fcb, r_final) = _unpack(total, full_shapes)
    small_grads = [
        r_anorm,
        lax.dynamic_slice_in_dim(r_aconv, me * cshard, cshard, axis=1).reshape(a_conv.shape),
        lax.dynamic_slice_in_dim(r_bnorm, me * cshard, cshard, axis=1),
        lax.dynamic_slice_in_dim(r_bvnorm, me * cshard, cshard, axis=1),
        r_ws.reshape(b_ws.shape), r_bs.reshape(b_bs.shape), r_fnorm,
        lax.dynamic_slice_in_dim(r_fcw, me * fshard, fshard, axis=2),
        r_fcb, r_final.reshape(final_norm.shape)]
    small_w = [a_norm, a_conv, b_norm, b_vnorm, b_ws, b_bs, f_norm, f_conv_w, f_conv_b, final_norm]
    small_m = [m_a_norm, m_a_conv, m_b_norm, m_b_vnorm, m_b_ws, m_b_bs, m_f_norm, m_f_conv_w, m_f_conv_b, m_final_norm]
    small_v = [v_a_norm, v_a_conv, v_b_norm, v_b_vnorm, v_b_ws, v_b_bs, v_f_norm, v_f_conv_w, v_f_conv_b, v_final_norm]
    shapes = [w.shape for w in small_w]
    packed = _adamw_packed(_pack(small_w), _pack(small_grads), _pack(small_m), _pack(small_v), "adamw_small")
    s_delta, s_m, s_v = (_unpack(p, shapes) for p in packed)
    small_names = ["a_norm", "a_conv", "b_norm", "b_vnorm", "b_ws", "b_bs", "f_norm", "f_conv_w", "f_conv_b", "final_norm"]
    small = {nm: (small_grads[i], s_delta[i], s_m[i], s_v[i]) for i, nm in enumerate(small_names)}

    order = ["a_norm", "a_in", "a_conv", "a_out", "b_norm", "b_in", "b_vnorm", "b_ws", "b_bs", "b_out",
             "f_norm", "f_up", "f_conv_w", "f_conv_b", "f_down", "final_norm"]
    res = {nm: (big[nm] if nm in big else small[nm]) for nm in order}
    outs = [loss_v[0, 0], grad_x.reshape(x.shape)]
    for k in range(4):
        outs += [res[nm][k] for nm in order]
    return tuple(outs)


def _unpack_rows(rows, shapes):
    out, off = [], 0
    for shp in shapes:
        n = math.prod(shp)
        out.append(rows[:, off:off + n].reshape((N_DEV,) + tuple(shp)))
        off += n + (-n) % PACK_GRANULE
    return out
```

```python
import math

import jax
import jax.numpy as jnp
from jax import lax
from jax.experimental import pallas as pl
from jax.experimental.pallas import tpu as pltpu

F32 = jnp.float32
BF16 = jnp.bfloat16
MESH = pl.DeviceIdType.MESH

N_DEV = 8
RMS_EPS = 1e-5
CHUNK = 128
SG_GROUPS = 8
ADAM_LR = 0.001
ADAM_B1 = 0.9
ADAM_B2 = 0.999
ADAM_EPS = 1e-08
ADAM_WD = 0.01
ADAM_STEP = 10

LANES = 128
SLAB = 16
FUSE_STRIP = 256
FUSE_HALO = 8
VMEM_LIMIT = 56 * 1024 * 1024
PACK_GRANULE = 8 * LANES


def _pick(dim, cap, mult):
    best = None
    t = mult
    while t <= min(dim, cap):
        if dim % t == 0:
            best = t
        t += mult
    return dim if best is None else best


def _params(semantics=None):
    return pltpu.CompilerParams(dimension_semantics=semantics, vmem_limit_bytes=VMEM_LIMIT)


_DIMS = {
    "nn": (((1,), (0,)), ((), ())),
    "nt": (((1,), (1,)), ((), ())),
    "tn": (((0,), (0,)), ((), ())),
}


def _matmul(a, b, mode, out_dtype, name, resid=None, after=None, tm_cap=1024, tn_cap=1024, tk_cap=2816,
            hold_b=False):
    if mode == "nn":
        (m, k), n = a.shape, b.shape[1]
    elif mode == "nt":
        (m, k), n = a.shape, b.shape[0]
    else:
        (k, m), n = a.shape, b.shape[1]
    tm, tn, tk = _pick(m, tm_cap, LANES), _pick(n, tn_cap, LANES), _pick(k, tk_cap, LANES)
    nk = k // tk
    n_in = 2 + (resid is not None) + (after is not None)

    def body(*refs):
        a_ref, b_ref = refs[:2]
        r_ref = refs[2] if resid is not None else None
        o_ref = refs[n_in]
        prod = lax.dot_general(a_ref[...], b_ref[...], _DIMS[mode], preferred_element_type=F32)

        def finish(r):
            if r_ref is not None:
                r = r + r_ref[...]
            o_ref[...] = r.astype(out_dtype)

        if nk == 1:
            finish(prod)
            return
        acc_ref = refs[n_in + 1]
        kk = pl.program_id(2)

        @pl.when(kk == 0)
        def _():
            acc_ref[...] = prod

        @pl.when(jnp.logical_and(kk > 0, kk < nk - 1))
        def _():
            acc_ref[...] += prod

        @pl.when(kk == nk - 1)
        def _():
            finish(acc_ref[...] + prod)

    def spec(block, index):
        if hold_b:
            return pl.BlockSpec(block, lambda j, i, kk: index(i, j, kk))
        return pl.BlockSpec(block, index)

    a_spec = (spec((tk, tm), lambda i, j, kk: (kk, i)) if mode == "tn"
              else spec((tm, tk), lambda i, j, kk: (i, kk)))
    b_spec = (spec((tn, tk), lambda i, j, kk: (j, kk)) if mode == "nt"
              else spec((tk, tn), lambda i, j, kk: (kk, j)))
    o_spec = spec((tm, tn), lambda i, j, kk: (i, j))
    in_specs = [a_spec, b_spec] + ([o_spec] if resid is not None else [])
    args = (a, b) + ((resid,) if resid is not None else ())
    if after is not None:
        in_specs.append(pl.BlockSpec(memory_space=pl.ANY))
        args += (after,)
    return pl.pallas_call(
        body, name=name, grid=(n // tn, m // tm, nk) if hold_b else (m // tm, n // tn, nk),
        in_specs=in_specs, out_specs=o_spec,
        out_shape=jax.ShapeDtypeStruct((m, n), out_dtype),
        scratch_shapes=[pltpu.VMEM((tm, tn), F32)] if nk > 1 else [],
        compiler_params=_params(("parallel", "parallel", "arbitrary")),
    )(*args)


def _rms_stats(xf):
    inv = lax.rsqrt(jnp.mean(xf * xf, axis=-1, keepdims=True) + RMS_EPS)
    return inv, xf * inv


def _rmsnorm_fwd(x, g, name, after=None):
    s, d = x.shape
    tm = _pick(s, 512, SLAB)
    extra = () if after is None else (after,)

    def body(x_ref, g_ref, *rest):
        _, xhat = _rms_stats(x_ref[...])
        rest[-1][...] = (xhat * g_ref[...]).astype(BF16)

    return pl.pallas_call(
        body, name=name, grid=(s // tm,),
        in_specs=[pl.BlockSpec((tm, d), lambda i: (i, 0)), pl.BlockSpec((1, d), lambda i: (0, 0))]
        + [pl.BlockSpec(memory_space=pl.ANY)] * len(extra),
        out_specs=pl.BlockSpec((tm, d), lambda i: (i, 0)),
        out_shape=jax.ShapeDtypeStruct((s, d), BF16),
        compiler_params=_params(("parallel",)),
    )(x, g, *extra)


def _rmsnorm_bwd(x, g, dh, dx_out, name):
    s, d = x.shape
    tm = _pick(s, 256, SLAB)

    def body(x_ref, g_ref, dh_ref, dxo_ref, dxi_ref, dxib_ref, dg_ref):
        inv, xhat = _rms_stats(x_ref[...])
        dhv = dh_ref[...].astype(F32)
        dxhat = dhv * g_ref[...]
        proj = jnp.mean(dxhat * xhat, axis=-1, keepdims=True)
        dx = dxo_ref[...] + inv * (dxhat - xhat * proj)
        dxi_ref[...] = dx
        dxib_ref[...] = dx.astype(BF16)
        part = jnp.sum(dhv * xhat, axis=0, keepdims=True)

        @pl.when(pl.program_id(0) == 0)
        def _():
            dg_ref[...] = part

        @pl.when(pl.program_id(0) > 0)
        def _():
            dg_ref[...] += part

    row = pl.BlockSpec((tm, d), lambda i: (i, 0))
    vec = pl.BlockSpec((1, d), lambda i: (0, 0))
    return pl.pallas_call(
        body, name=name, grid=(s // tm,),
        in_specs=[row, vec, row, row], out_specs=[row, row, vec],
        out_shape=[jax.ShapeDtypeStruct((s, d), F32), jax.ShapeDtypeStruct((s, d), BF16),
                   jax.ShapeDtypeStruct((1, d), F32)],
        compiler_params=_params(("arbitrary",)),
    )(x, g, dh, dx_out)


def _final_loss(x, g, target, name):
    s, d = x.shape
    tm = _pick(s, 256, SLAB)

    def body(x_ref, g_ref, t_ref, dx_ref, dxb_ref, loss_ref, dg_ref):
        inv, xhat = _rms_stats(x_ref[...])
        gv = g_ref[...]
        err = xhat * gv - t_ref[...]
        loss = 0.5 * jnp.sum(jnp.mean(err * err, axis=-1, keepdims=True), axis=0, keepdims=True)
        dy = err * (1.0 / d)
        dxhat = dy * gv
        proj = jnp.mean(dxhat * xhat, axis=-1, keepdims=True)
        dx = inv * (dxhat - xhat * proj)
        dx_ref[...] = dx
        dxb_ref[...] = dx.astype(BF16)
        part = jnp.sum(dy * xhat, axis=0, keepdims=True)
        loss_row = jnp.broadcast_to(loss, (1, LANES))

        @pl.when(pl.program_id(0) == 0)
        def _():
            dg_ref[...] = part
            loss_ref[...] = loss_row

        @pl.when(pl.program_id(0) > 0)
        def _():
            dg_ref[...] += part
            loss_ref[...] += loss_row

    row = pl.BlockSpec((tm, d), lambda i: (i, 0))
    vec = pl.BlockSpec((1, d), lambda i: (0, 0))
    return pl.pallas_call(
        body, name=name, grid=(s // tm,),
        in_specs=[row, vec, row],
        out_specs=[row, row, pl.BlockSpec((1, LANES), lambda i: (0, 0)), vec],
        out_shape=[jax.ShapeDtypeStruct((s, d), F32), jax.ShapeDtypeStruct((s, d), BF16),
                   jax.ShapeDtypeStruct((1, LANES), F32), jax.ShapeDtypeStruct((1, d), F32)],
        compiler_params=_params(("arbitrary",)),
    )(x, g, target)


def _mixa_in_fused(h, w_in, wc, name):
    s, d = h.shape
    tm = _pick(s, 1024, LANES)
    tn = _pick(d, 512, FUSE_STRIP)
    nj = d // tn

    def body(h_ref, wb_ref, wg_ref, wx_ref, wc_ref, gb_ref, gc_ref, xs_ref, cv_ref, y_ref, carry):
        @pl.when(pl.program_id(1) == 0)
        def _():
            carry[...] = jnp.zeros_like(carry)

        def matmul(st):
            cols = slice(st * FUSE_STRIP, (st + 1) * FUSE_STRIP)
            return tuple(jnp.dot(h_ref[...], w_ref[:, cols], preferred_element_type=F32).astype(BF16)
                         for w_ref in (wb_ref, wg_ref, wx_ref))

        n_strips = tn // FUSE_STRIP
        parts = matmul(0)
        for st in range(n_strips):
            parts_next = matmul(st + 1) if st + 1 < n_strips else None
            cols = slice(st * FUSE_STRIP, (st + 1) * FUSE_STRIP)
            for ref, part in zip((gb_ref, gc_ref, xs_ref), parts):
                ref[:, cols] = part
            p = parts[1].astype(F32) * parts[2].astype(F32)
            ext = jnp.concatenate([carry[:, cols], p], axis=0)
            s1 = pltpu.roll(ext, 1, 0)[FUSE_HALO:, :]
            s2 = pltpu.roll(ext, 2, 0)[FUSE_HALO:, :]
            carry[:, cols] = p[tm - FUSE_HALO:, :]
            cv = wc_ref[0:1, cols] * s2 + wc_ref[1:2, cols] * s1 + wc_ref[2:3, cols] * p
            cv_ref[:, cols] = cv.astype(BF16)
            y_ref[:, cols] = (parts[0].astype(F32) * cv).astype(BF16)
            parts = parts_next

    def cols_of(rows, offset):
        return pl.BlockSpec((rows, tn), lambda j, i: (0, j + offset))

    tile = pl.BlockSpec((tm, tn), lambda j, i: (i, j))
    return pl.pallas_call(
        body, name=name, grid=(nj, s // tm),
        in_specs=[pl.BlockSpec((tm, d), lambda j, i: (i, 0)), cols_of(d, 0), cols_of(d, nj), cols_of(d, 2 * nj),
                  cols_of(3, 0)],
        out_specs=[tile] * 5, out_shape=[jax.ShapeDtypeStruct((s, d), BF16)] * 5,
        scratch_shapes=[pltpu.VMEM((FUSE_HALO, tn), F32)],
        compiler_params=_params(("parallel", "arbitrary")),
    )(h, w_in, w_in, w_in, wc)


def _mixa_out_dx_fused(dxb, w_out, bcx, cv, wc, after, name):
    s, d = dxb.shape
    tm = _pick(s, 1024, LANES)
    tn = _pick(d, 512, FUSE_STRIP)
    nj, ni = d // tn, s // tm
    n_steps = nj * ni
    sub = tm // FUSE_HALO

    def body(dx_ref, w_ref, gb_ref, gc_ref, xs_ref, cv_ref, wc_ref, after_ref,
             dbcx_hbm, dwc_ref, out_buf, out_sem, carry, acc):
        j, i = pl.program_id(0), pl.program_id(1)
        step = j * ni + i
        slot = lax.rem(step, 2)
        row0 = pl.multiple_of((ni - 1 - i) * tm, tm)

        def out_copy(part, slot_=None):
            slot_ = slot if slot_ is None else slot_
            col0 = pl.multiple_of(part * d + j * tn, LANES)
            return pltpu.make_async_copy(out_buf.at[slot_, part], dbcx_hbm.at[pl.ds(row0, tm), pl.ds(col0, tn)],
                                         out_sem.at[slot_, part])

        @pl.when(step >= 2)
        def _():
            for part in range(3):
                out_copy(part).wait()

        @pl.when(i == 0)
        def _():
            carry[...] = jnp.zeros_like(carry)
            acc[...] = jnp.zeros_like(acc)

        for st in range(tn // FUSE_STRIP):
            cols = slice(st * FUSE_STRIP, (st + 1) * FUSE_STRIP)
            dyv = lax.dot_general(dx_ref[...], w_ref[cols, :], _DIMS["nt"], preferred_element_type=F32)
            gc = gc_ref[:, cols].astype(F32)
            xs = xs_ref[:, cols].astype(F32)
            d0 = dyv * gb_ref[:, cols].astype(F32)
            ext = jnp.concatenate([d0, carry[:, cols]], axis=0)
            d1 = pltpu.roll(ext, tm + FUSE_HALO - 1, 0)[:tm, :]
            d2 = pltpu.roll(ext, tm + FUSE_HALO - 2, 0)[:tm, :]
            carry[:, cols] = d0[:FUSE_HALO, :]
            dp = wc_ref[2:3, cols] * d0 + wc_ref[1:2, cols] * d1 + wc_ref[0:1, cols] * d2
            out_buf[slot, 0, :, cols] = (dyv * cv_ref[:, cols].astype(F32)).astype(BF16)
            out_buf[slot, 1, :, cols] = (dp * xs).astype(BF16)
            out_buf[slot, 2, :, cols] = (dp * gc).astype(BF16)
            p = gc * xs
            for k, term in enumerate((d2 * p, d1 * p, d0 * p)):
                acc[k, :, cols] += jnp.sum(term.reshape(sub, FUSE_HALO, FUSE_STRIP), axis=0)

        for part in range(3):
            out_copy(part).start()

        @pl.when(i == ni - 1)
        def _():
            for k in range(3):
                dwc_ref[k:k + 1, :] = jnp.sum(acc[k], axis=0, keepdims=True)

        @pl.when(step == n_steps - 1)
        def _():
            for part in range(3):
                out_copy(part).wait()
                if n_steps > 1:
                    out_copy(part, 1 - slot).wait()

    tile = pl.BlockSpec((tm, tn), lambda j, i: (ni - 1 - i, j))
    return pl.pallas_call(
        body, name=name, grid=(nj, ni),
        in_specs=[pl.BlockSpec((tm, d), lambda j, i: (ni - 1 - i, 0)), pl.BlockSpec((tn, d), lambda j, i: (j, 0)),
                  tile, tile, tile, tile, pl.BlockSpec((3, tn), lambda j, i: (0, j)),
                  pl.BlockSpec(memory_space=pl.ANY)],
        out_specs=[pl.BlockSpec(memory_space=pl.ANY), pl.BlockSpec((3, tn), lambda j, i: (0, j))],
        out_shape=[jax.ShapeDtypeStruct((s, 3 * d), BF16), jax.ShapeDtypeStruct((3, d), F32)],
        scratch_shapes=[pltpu.VMEM((2, 3, tm, tn), BF16), pltpu.SemaphoreType.DMA((2, 3)),
                        pltpu.VMEM((FUSE_HALO, tn), F32), pltpu.VMEM((3, FUSE_HALO, tn), F32)],
        compiler_params=_params(("arbitrary", "arbitrary")),
    )(dxb, w_out, bcx[0], bcx[1], bcx[2], cv, wc, after)


def _sigmoid(z):
    return 0.5 * jnp.tanh(0.5 * z) + 0.5


def _ffn_up_fused(h, w_up, cw, cb, name):
    s, d = h.shape
    f = w_up.shape[1] // 2
    tm = _pick(s, 1024, LANES)
    tn = _pick(f, 512, FUSE_STRIP)
    nj = f // tn

    def body(h_ref, wg_ref, wa_ref, cwg_ref, cwa_ref, cbg_ref, cba_ref,
             upg_ref, upa_ref, cvg_ref, cva_ref, act_ref, carry_g, carry_a):
        @pl.when(pl.program_id(1) == 0)
        def _():
            carry_g[...] = jnp.zeros_like(carry_g)
            carry_a[...] = jnp.zeros_like(carry_a)

        def matmul(st):
            cols = slice(st * FUSE_STRIP, (st + 1) * FUSE_STRIP)
            return tuple(jnp.dot(h_ref[...], w_ref[:, cols], preferred_element_type=F32).astype(BF16)
                         for w_ref in (wg_ref, wa_ref))

        def conv(up, cw_ref, cb_ref, carry, up_ref, cv_ref, cols):
            up_ref[:, cols] = up
            x = up.astype(F32)
            ext = jnp.concatenate([carry[:, cols], x], axis=0)
            s1 = pltpu.roll(ext, 1, 0)[FUSE_HALO:, :]
            s2 = pltpu.roll(ext, 2, 0)[FUSE_HALO:, :]
            carry[:, cols] = x[tm - FUSE_HALO:, :]
            cv = cw_ref[0:1, cols] * s2 + cw_ref[1:2, cols] * s1 + cw_ref[2:3, cols] * x + cb_ref[:, cols]
            cv_ref[:, cols] = cv.astype(BF16)
            return cv

        n_strips = tn // FUSE_STRIP
        ups = matmul(0)
        for st in range(n_strips):
            ups_next = matmul(st + 1) if st + 1 < n_strips else None
            cols = slice(st * FUSE_STRIP, (st + 1) * FUSE_STRIP)
            gcv = conv(ups[0], cwg_ref, cbg_ref, carry_g, upg_ref, cvg_ref, cols)
            acv = conv(ups[1], cwa_ref, cba_ref, carry_a, upa_ref, cva_ref, cols)
            act_ref[:, cols] = (gcv * _sigmoid(gcv) * acv).astype(BF16)
            ups = ups_next

    def cols_of(rows, offset):
        return pl.BlockSpec((rows, tn), lambda j, i: (0, j + offset))

    tile = pl.BlockSpec((tm, tn), lambda j, i: (i, j))
    out = jax.ShapeDtypeStruct((s, f), BF16)
    return pl.pallas_call(
        body, name=name, grid=(nj, s // tm),
        in_specs=[pl.BlockSpec((tm, d), lambda j, i: (i, 0)), cols_of(d, 0), cols_of(d, nj),
                  cols_of(3, 0), cols_of(3, nj), cols_of(1, 0), cols_of(1, nj)],
        out_specs=[tile] * 5, out_shape=[out] * 5,
        scratch_shapes=[pltpu.VMEM((FUSE_HALO, tn), F32)] * 2,
        compiler_params=_params(("parallel", "arbitrary")),
    )(h, w_up, w_up, cw, cw, cb, cb)


def _ffn_down_dx_fused(dxb, w_down, up, cv, cw, after, name):
    s, d = dxb.shape
    f = w_down.shape[0]
    tm = _pick(s, 1024, LANES)
    tn = _pick(f, 512, FUSE_STRIP)
    nj, ni = f // tn, s // tm
    n_steps = nj * ni
    sub = tm // FUSE_HALO

    def body(dx_ref, w_ref, upg_ref, upa_ref, cvg_ref, cva_ref, cwg_ref, cwa_ref, after_ref,
             dup_hbm, dcwg_ref, dcwa_ref, dcbg_ref, dcba_ref, out_buf, out_sem, carry, acc):
        j, i = pl.program_id(0), pl.program_id(1)
        step = j * ni + i
        slot = lax.rem(step, 2)
        row0 = pl.multiple_of((ni - 1 - i) * tm, tm)

        def out_copy(half):
            col0 = pl.multiple_of(half * f + j * tn, LANES)
            return pltpu.make_async_copy(out_buf.at[slot, half], dup_hbm.at[pl.ds(row0, tm), pl.ds(col0, tn)],
                                         out_sem.at[slot, half])

        @pl.when(step >= 2)
        def _():
            for half in range(2):
                out_copy(half).wait()

        @pl.when(i == 0)
        def _():
            carry[...] = jnp.zeros_like(carry)
            acc[...] = jnp.zeros_like(acc)

        for st in range(tn // FUSE_STRIP):
            cols = slice(st * FUSE_STRIP, (st + 1) * FUSE_STRIP)
            dact = lax.dot_general(dx_ref[...], w_ref[cols, :], _DIMS["nt"], preferred_element_type=F32)
            gcv = cvg_ref[:, cols].astype(F32)
            acv = cva_ref[:, cols].astype(F32)
            sg = _sigmoid(gcv)
            dd = (dact * acv * (sg * (1.0 + gcv * (1.0 - sg))), dact * (gcv * sg))
            for half, (up_ref, cw_ref) in enumerate(((upg_ref, cwg_ref), (upa_ref, cwa_ref))):
                x = up_ref[:, cols].astype(F32)
                d0 = dd[half]
                ext = jnp.concatenate([d0, carry[half, :, cols]], axis=0)
                d1 = pltpu.roll(ext, tm + FUSE_HALO - 1, 0)[:tm, :]
                d2 = pltpu.roll(ext, tm + FUSE_HALO - 2, 0)[:tm, :]
                carry[half, :, cols] = d0[:FUSE_HALO, :]
                out_buf[slot, half, :, cols] = (cw_ref[2:3, cols] * d0 + cw_ref[1:2, cols] * d1
                                                + cw_ref[0:1, cols] * d2).astype(BF16)
                for k, term in enumerate((d2 * x, d1 * x, d0 * x, d0)):
                    acc[half, k, :, cols] += jnp.sum(term.reshape(sub, FUSE_HALO, FUSE_STRIP), axis=0)

        for half in range(2):
            out_copy(half).start()

        @pl.when(i == ni - 1)
        def _():
            for half, (dcw_ref, dcb_ref) in enumerate(((dcwg_ref, dcbg_ref), (dcwa_ref, dcba_ref))):
                for k in range(3):
                    dcw_ref[k:k + 1, :] = jnp.sum(acc[half, k], axis=0, keepdims=True)
                dcb_ref[...] = jnp.sum(acc[half, 3], axis=0, keepdims=True)

        @pl.when(step == n_steps - 1)
        def _():
            for half in range(2):
                out_copy(half).wait()
                if n_steps > 1:
                    pltpu.make_async_copy(out_buf.at[1 - slot, half], dup_hbm.at[pl.ds(row0, tm), pl.ds(0, tn)],
                                          out_sem.at[1 - slot, half]).wait()

    tile = pl.BlockSpec((tm, tn), lambda j, i: (ni - 1 - i, j))

    def cols_of(rows, offset):
        return pl.BlockSpec((rows, tn), lambda j, i: (0, j + offset))

    small = pl.BlockSpec((3, tn), lambda j, i: (0, j)), pl.BlockSpec((1, tn), lambda j, i: (0, j))
    return pl.pallas_call(
        body, name=name, grid=(nj, ni),
        in_specs=[pl.BlockSpec((tm, d), lambda j, i: (ni - 1 - i, 0)), pl.BlockSpec((tn, d), lambda j, i: (j, 0)),
                  tile, tile, tile, tile, cols_of(3, 0), cols_of(3, nj), pl.BlockSpec(memory_space=pl.ANY)],
        out_specs=[pl.BlockSpec(memory_space=pl.ANY), small[0], small[0], small[1], small[1]],
        out_shape=[jax.ShapeDtypeStruct((s, 2 * f), BF16), jax.ShapeDtypeStruct((3, f), F32),
                   jax.ShapeDtypeStruct((3, f), F32), jax.ShapeDtypeStruct((1, f), F32),
                   jax.ShapeDtypeStruct((1, f), F32)],
        scratch_shapes=[pltpu.VMEM((2, 2, tm, tn), BF16), pltpu.SemaphoreType.DMA((2, 2)),
                        pltpu.VMEM((2, FUSE_HALO, tn), F32), pltpu.VMEM((2, 4, FUSE_HALO, tn), F32)],
        compiler_params=_params(("arbitrary", "arbitrary")),
    )(dxb, w_down, up[0], up[1], cv[0], cv[1], cw, cw, after)


_GELU_C = math.sqrt(2.0 / math.pi)


def _gelu(x):
    th = jnp.tanh(_GELU_C * (x + 0.044715 * (x * x * x)))
    return x * (0.5 * (1.0 + th)), th


def _gelu_grad(x, th):
    return 0.5 * (1.0 + th) + 0.5 * x * (1.0 - th * th) * (_GELU_C * (1.0 + 3.0 * 0.044715 * (x * x)))


def _masked_ws(ws_ref, h):
    t = lax.broadcasted_iota(jnp.int32, (CHUNK, CHUNK), 0)
    sx = lax.broadcasted_iota(jnp.int32, (CHUNK, CHUNK), 1)
    return jnp.where(sx <= t, ws_ref[h], 0.0)


def _mixb_fwd(pre, gv, ws, bs_wide, name):
    s, w2 = pre.shape
    w = w2 // 2
    gw = w // SG_GROUPS

    def body(pre_ref, gv_ref, ws_ref, bs_ref, o_ref):
        zu, _ = _gelu(pre_ref[:, :w].astype(F32))
        zv, _ = _gelu(pre_ref[:, w:].astype(F32))
        _, vhat = _rms_stats(zv)
        vn = (vhat * gv_ref[...]).astype(BF16)
        for h in range(SG_GROUPS):
            cols = slice(h * gw, (h + 1) * gw)
            wsm = _masked_ws(ws_ref, h).astype(BF16)
            gate = jnp.dot(wsm, vn[:, cols], preferred_element_type=F32)
            gate = gate + jnp.tile(bs_ref[h], (1, gw // LANES))
            o_ref[:, cols] = (zu[:, cols] * gate).astype(BF16)

    return pl.pallas_call(
        body, name=name, grid=(s // CHUNK,),
        in_specs=[pl.BlockSpec((CHUNK, w2), lambda i: (i, 0)), pl.BlockSpec((1, w), lambda i: (0, 0)),
                  pl.BlockSpec((SG_GROUPS, CHUNK, CHUNK), lambda i: (0, 0, 0)),
                  pl.BlockSpec((SG_GROUPS, CHUNK, LANES), lambda i: (0, 0, 0))],
        out_specs=pl.BlockSpec((CHUNK, w), lambda i: (i, 0)),
        out_shape=jax.ShapeDtypeStruct((s, w), BF16),
        compiler_params=_params(("parallel",)),
    )(pre, gv, ws, bs_wide)


def _mixb_bwd(pre, dug, gv, ws, bs_wide, name):
    s, w2 = pre.shape
    w = w2 // 2
    gw = w // SG_GROUPS

    def body(pre_ref, dug_ref, gv_ref, ws_ref, bs_ref, o_ref, dws_ref, dbs_ref, dgv_ref, dvn_ref):
        first = pl.program_id(0) == 0

        @pl.when(first)
        def _():
            dws_ref[...] = jnp.zeros_like(dws_ref)
            dbs_ref[...] = jnp.zeros_like(dbs_ref)

        pu = pre_ref[:, :w].astype(F32)
        pv = pre_ref[:, w:].astype(F32)
        zu, thu = _gelu(pu)
        zv, thv = _gelu(pv)
        inv, vhat = _rms_stats(zv)
        gvv = gv_ref[...]
        vn = (vhat * gvv).astype(BF16)
        for h in range(SG_GROUPS):
            cols = slice(h * gw, (h + 1) * gw)
            wsm = _masked_ws(ws_ref, h).astype(BF16)
            gate = jnp.dot(wsm, vn[:, cols], preferred_element_type=F32)
            gate = gate + jnp.tile(bs_ref[h], (1, gw // LANES))
            dug_h = dug_ref[:, cols].astype(F32)
            dgate = dug_h * zu[:, cols]
            dgate_b = dgate.astype(BF16)
            o_ref[:, cols] = (dug_h * gate * _gelu_grad(pu[:, cols], thu[:, cols])).astype(BF16)
            dbs_ref[h] += jnp.broadcast_to(jnp.sum(dgate, axis=-1, keepdims=True), (CHUNK, LANES))
            dws = lax.dot_general(dgate_b, vn[:, cols], _DIMS["nt"], preferred_element_type=F32)
            t = lax.broadcasted_iota(jnp.int32, (CHUNK, CHUNK), 0)
            sx = lax.broadcasted_iota(jnp.int32, (CHUNK, CHUNK), 1)
            dws_ref[h] += jnp.where(sx <= t, dws, 0.0)
            dvn_ref[:, cols] = lax.dot_general(wsm, dgate_b, _DIMS["tn"], preferred_element_type=F32)
        dvn = dvn_ref[...]
        part = jnp.sum(dvn * vhat, axis=0, keepdims=True)

        @pl.when(first)
        def _():
            dgv_ref[...] = part

        @pl.when(jnp.logical_not(first))
        def _():
            dgv_ref[...] += part

        dvhat = dvn * gvv
        dzv = inv * (dvhat - vhat * jnp.mean(dvhat * vhat, axis=-1, keepdims=True))
        o_ref[:, w:] = (dzv * _gelu_grad(pv, thv)).astype(BF16)

    return pl.pallas_call(
        body, name=name, grid=(s // CHUNK,),
        in_specs=[pl.BlockSpec((CHUNK, w2), lambda i: (i, 0)), pl.BlockSpec((CHUNK, w), lambda i: (i, 0)),
                  pl.BlockSpec((1, w), lambda i: (0, 0)),
                  pl.BlockSpec((SG_GROUPS, CHUNK, CHUNK), lambda i: (0, 0, 0)),
                  pl.BlockSpec((SG_GROUPS, CHUNK, LANES), lambda i: (0, 0, 0))],
        out_specs=[pl.BlockSpec((CHUNK, w2), lambda i: (i, 0)),
                   pl.BlockSpec((SG_GROUPS, CHUNK, CHUNK), lambda i: (0, 0, 0)),
                   pl.BlockSpec((SG_GROUPS, CHUNK, LANES), lambda i: (0, 0, 0)),
                   pl.BlockSpec((1, w), lambda i: (0, 0))],
        out_shape=[jax.ShapeDtypeStruct((s, w2), BF16), jax.ShapeDtypeStruct((SG_GROUPS, CHUNK, CHUNK), F32),
                   jax.ShapeDtypeStruct((SG_GROUPS, CHUNK, LANES), F32), jax.ShapeDtypeStruct((1, w), F32)],
        scratch_shapes=[pltpu.VMEM((CHUNK, w), F32)],
        compiler_params=_params(("arbitrary",)),
    )(pre, dug, gv, ws, bs_wide)


def _cast_layer(w3, layer, name):
    _, r, c = w3.shape
    tr = _pick(r, 256, SLAB)

    def body(w_ref, o_ref):
        o_ref[...] = w_ref[...].astype(BF16)

    return pl.pallas_call(
        body, name=name, grid=(r // tr,),
        in_specs=[pl.BlockSpec((None, tr, c), lambda i: (layer, i, 0))],
        out_specs=pl.BlockSpec((tr, c), lambda i: (i, 0)),
        out_shape=jax.ShapeDtypeStruct((r, c), BF16),
        compiler_params=_params(("parallel",)),
    )(w3)


def _adamw_math(w, g, m, v):
    m = ADAM_B1 * m + (1.0 - ADAM_B1) * g
    v = ADAM_B2 * v + (1.0 - ADAM_B2) * (g * g)
    m_hat = m / (1.0 - ADAM_B1 ** ADAM_STEP)
    v_hat = v / (1.0 - ADAM_B2 ** ADAM_STEP)
    delta = -ADAM_LR * (m_hat / (jnp.sqrt(v_hat) + ADAM_EPS) + ADAM_WD * w)
    return delta, m, v


def _adamw_sharded(recvs, w, m, v, name):
    nl, r, c = w.shape
    tc = _pick(c, 1536, LANES)
    tr = _pick(r, 64, SLAB)

    def body(*refs):
        recv_refs = refs[:nl]
        w_ref, m_ref, v_ref, g_ref, d_ref, nm_ref, nv_ref = refs[nl:]
        for layer, recv_ref in enumerate(recv_refs):
            @pl.when(pl.program_id(0) == layer)
            def _():
                g = recv_ref[0].astype(F32)
                for q in range(1, N_DEV):
                    g = g + recv_ref[q].astype(F32)
                delta, nm, nv = _adamw_math(w_ref[...], g, m_ref[...], v_ref[...])
                g_ref[...] = g
                d_ref[...] = delta
                nm_ref[...] = nm
                nv_ref[...] = nv

    def recv_spec(layer):
        return pl.BlockSpec((N_DEV, tr, tc),
                            lambda l, i, j: (0, jnp.where(l == layer, i, 0), jnp.where(l == layer, j, 0)))

    blk = pl.BlockSpec((None, tr, tc), lambda l, i, j: (l, i, j))
    out = jax.ShapeDtypeStruct((nl, r, c), F32)
    return pl.pallas_call(
        body, name=name, grid=(nl, r // tr, c // tc),
        in_specs=[recv_spec(layer) for layer in range(nl)] + [blk, blk, blk],
        out_specs=[blk] * 4, out_shape=[out] * 4,
        compiler_params=_params(("parallel",) * 3),
    )(*recvs, w, m, v)


def _adamw_packed(w, g, m, v, name):
    r, c = w.shape
    tr = _pick(r, 256, 8)

    def body(w_ref, g_ref, m_ref, v_ref, d_ref, nm_ref, nv_ref):
        delta, nm, nv = _adamw_math(w_ref[...], g_ref[...], m_ref[...], v_ref[...])
        d_ref[...] = delta
        nm_ref[...] = nm
        nv_ref[...] = nv

    blk = pl.BlockSpec((tr, c), lambda i: (i, 0))
    out = jax.ShapeDtypeStruct((r, c), F32)
    return pl.pallas_call(
        body, name=name, grid=(r // tr,), in_specs=[blk] * 4, out_specs=[blk] * 3, out_shape=[out] * 3,
        compiler_params=_params(("parallel",)),
    )(w, g, m, v)


def _pack(arrays):
    parts = []
    for a in arrays:
        flat = a.reshape(-1).astype(F32)
        pad = (-flat.shape[0]) % PACK_GRANULE
        parts.append(jnp.pad(flat, (0, pad)) if pad else flat)
    return jnp.concatenate(parts).reshape(-1, LANES)


def _unpack(buf, shapes):
    flat = buf.reshape(-1)
    out, off = [], 0
    for shp in shapes:
        n = math.prod(shp)
        out.append(flat[off:off + n].reshape(shp))
        off += n + (-n) % PACK_GRANULE
    return out


def _mesh_pos():
    return lax.axis_index("x"), lax.axis_index("y"), lax.axis_index("c")


def _coords(q):
    return q // 4, (q // 2) % 2, q % 2


def _shard_of(ref, q, shard_shape, axis):
    r, c = shard_shape
    if axis == 0:
        return ref.at[pl.ds(pl.multiple_of(q * r, SLAB), r), :]
    return ref.at[:, pl.ds(pl.multiple_of(q * c, LANES), c)]


_HBM = pl.BlockSpec(memory_space=pltpu.HBM)
_SEM = pl.BlockSpec(memory_space=pltpu.SEMAPHORE)
_EFFECT = pltpu.SideEffectType.DATAFLOW_SIDE_EFFECTING


def _exchange_shapes(gather, src_shape, axis):
    r, c = src_shape
    if gather:
        return (r, c), ((r * N_DEV, c) if axis == 0 else (r, c * N_DEV))
    shard = (r // N_DEV, c) if axis == 0 else (r, c // N_DEV)
    return shard, (N_DEV,) + shard


def _exchange_copies(gather, src, land, sems, axis):
    send_sems, recv_sems, own_sem = sems
    x, y, c_ = _mesh_pos()
    me = 4 * x + 2 * y + c_
    shard, _ = _exchange_shapes(gather, src.shape, axis)

    def piece(q):
        return src if gather else _shard_of(src, q, shard, axis)

    def place(q):
        return _shard_of(land, q, shard, axis) if gather else land.at[q]

    own = pltpu.make_async_copy(piece(me), place(me), own_sem.at[0])
    sends, arrivals = [], []
    for step in range(1, N_DEV):
        to = (me + step) % N_DEV
        frm = (me + N_DEV - step) % N_DEV
        sends.append(pltpu.make_async_remote_copy(
            src_ref=piece(to), dst_ref=place(me), send_sem=send_sems.at[step - 1], recv_sem=recv_sems.at[step - 1],
            device_id=_coords(to), device_id_type=MESH))
        arrivals.append(pltpu.make_async_remote_copy(
            src_ref=piece(me), dst_ref=place(frm), send_sem=send_sems.at[step - 1], recv_sem=recv_sems.at[step - 1],
            device_id=_coords(frm), device_id_type=MESH))
    return own, sends, arrivals


def _exchange_start(gather, src, axis, name, after=None):
    _, land_shape = _exchange_shapes(gather, src.shape, axis)
    extra = () if after is None else (after,)

    def body(*refs):
        src_ref, land = refs[:2]
        send_sems, recv_sems, own_sem = refs[2 + len(extra):5 + len(extra)]
        own, sends, _ = _exchange_copies(gather, src_ref, land, (send_sems, recv_sems, own_sem), axis)
        own.start()
        for cp in sends:
            cp.start()
        refs[-1][...] = jnp.zeros_like(refs[-1])

    out = pl.pallas_call(
        body, name=name,
        out_shape=(pltpu.SemaphoreType.DMA((N_DEV - 1,)), pltpu.SemaphoreType.DMA((N_DEV - 1,)),
                   pltpu.SemaphoreType.DMA((1,)), pltpu.HBM(src.shape, src.dtype),
                   pltpu.HBM(land_shape, src.dtype), jax.ShapeDtypeStruct((8, LANES), F32)),
        in_specs=[_HBM, _HBM] + [pl.BlockSpec(memory_space=pl.ANY)] * len(extra),
        out_specs=(_SEM, _SEM, _SEM, _HBM, _HBM, pl.BlockSpec(memory_space=pltpu.VMEM)),
        input_output_aliases={0: 3, 1: 4},
        compiler_params=pltpu.CompilerParams(has_side_effects=_EFFECT),
    )(pltpu.with_memory_space_constraint(src, pltpu.HBM),
      pltpu.with_memory_space_constraint(lax.empty(land_shape, src.dtype), pltpu.HBM), *extra)
    return out[:5], out[5]


def _exchange_wait(gather, state, axis, after, name):
    send_sems, recv_sems, own_sem, src_thru, land_thru = state
    after = tuple(after) if isinstance(after, (tuple, list)) else (after,)

    def body(src, land, send_sems, recv_sems, own_sem, *rest):
        own, sends, arrivals = _exchange_copies(gather, src, land, (send_sems, recv_sems, own_sem), axis)
        for cp in sends:
            cp.wait_send()
        for cp in arrivals:
            cp.wait_recv()
        own.wait()

    return pl.pallas_call(
        body, name=name,
        out_shape=(pltpu.HBM(src_thru.shape, src_thru.dtype), pltpu.HBM(land_thru.shape, land_thru.dtype)),
        in_specs=[_HBM, _HBM, _SEM, _SEM, _SEM] + [pl.BlockSpec(memory_space=pl.ANY)] * len(after),
        out_specs=(_HBM, _HBM),
        input_output_aliases={0: 0, 1: 1},
        compiler_params=pltpu.CompilerParams(has_side_effects=_EFFECT),
    )(src_thru, land_thru, send_sems, recv_sems, own_sem, *after)[1]


def _gather2_copies(shard_ref, land, sems, axis, shard_shape):
    send1, recv1, own_sem, send2, recv2 = sems
    x, y, c = _mesh_pos()
    me, sibling = (x, y, c), (x, y, 1 - c)
    chips = [(1 - x, y), (x, 1 - y), (1 - x, 1 - y)]

    def region(dev):
        px, py, pc = dev
        return _shard_of(land, 4 * px + 2 * py + pc, shard_shape, axis)

    def copy(src, block, to, send, recv):
        return pltpu.make_async_remote_copy(src_ref=src, dst_ref=region(block), send_sem=send, recv_sem=recv,
                                            device_id=to, device_id_type=MESH)

    own = pltpu.make_async_copy(shard_ref, region(me), own_sem.at[0])
    peers = [sibling] + [(*chip, c) for chip in chips]
    sends1 = [copy(shard_ref, me, to, send1.at[k], recv1.at[k]) for k, to in enumerate(peers)]
    arrivals1 = [copy(shard_ref, frm, frm, send1.at[k], recv1.at[k]) for k, frm in enumerate(peers)]
    sends2, arrivals2 = [], []
    if send2 is not None:
        for j, chip in enumerate(chips):
            sends2.append(copy(region((*chip, c)), (*chip, c), sibling, send2.at[j], recv2.at[j]))
            arrivals2.append(copy(region((*chip, 1 - c)), (*chip, 1 - c), sibling, send2.at[j], recv2.at[j]))
    return own, sends1, arrivals1, sends2, arrivals2


def _gather2_start(shard, axis, name, after=None):
    _, land_shape = _exchange_shapes(True, shard.shape, axis)
    extra = () if after is None else (after,)

    def body(*refs):
        src_ref, land = refs[:2]
        send1, recv1, own_sem = refs[2 + len(extra):5 + len(extra)]
        own, sends1, _, _, _ = _gather2_copies(src_ref, land, (send1, recv1, own_sem, None, None), axis, shard.shape)
        own.start()
        for cp in sends1[1:] + sends1[:1]:
            cp.start()
        refs[-1][...] = jnp.zeros_like(refs[-1])

    out = pl.pallas_call(
        body, name=name,
        out_shape=(pltpu.SemaphoreType.DMA((4,)), pltpu.SemaphoreType.DMA((4,)), pltpu.SemaphoreType.DMA((1,)),
                   pltpu.HBM(shard.shape, shard.dtype), pltpu.HBM(land_shape, shard.dtype),
                   jax.ShapeDtypeStruct((8, LANES), F32)),
        in_specs=[_HBM, _HBM] + [pl.BlockSpec(memory_space=pl.ANY)] * len(extra),
        out_specs=(_SEM, _SEM, _SEM, _HBM, _HBM, pl.BlockSpec(memory_space=pltpu.VMEM)),
        input_output_aliases={0: 3, 1: 4},
        compiler_params=pltpu.CompilerParams(has_side_effects=_EFFECT),
    )(pltpu.with_memory_space_constraint(shard, pltpu.HBM),
      pltpu.with_memory_space_constraint(lax.empty(land_shape, shard.dtype), pltpu.HBM), *extra)
    return out[:5], out[5]


def _gather2_pass(state, axis, after, name):
    send1, recv1, own_sem, shard_thru, land_thru = state

    def body(src_ref, land, send1, recv1, own_sem, after_ref, send2, recv2, src_out, land_out, token):
        _, _, arrivals1, sends2, _ = _gather2_copies(src_ref, land, (send1, recv1, own_sem, send2, recv2), axis,
                                                     shard_thru.shape)
        for arrival, fwd in zip(arrivals1[1:], sends2):
            arrival.wait_recv()
            fwd.start()
        token[...] = jnp.zeros_like(token)

    out = pl.pallas_call(
        body, name=name,
        out_shape=(pltpu.SemaphoreType.DMA((3,)), pltpu.SemaphoreType.DMA((3,)),
                   pltpu.HBM(shard_thru.shape, shard_thru.dtype), pltpu.HBM(land_thru.shape, land_thru.dtype),
                   jax.ShapeDtypeStruct((8, LANES), F32)),
        in_specs=[_HBM, _HBM, _SEM, _SEM, _SEM, pl.BlockSpec(memory_space=pl.ANY)],
        out_specs=(_SEM, _SEM, _HBM, _HBM, pl.BlockSpec(memory_space=pltpu.VMEM)),
        input_output_aliases={0: 2, 1: 3},
        compiler_params=pltpu.CompilerParams(has_side_effects=_EFFECT),
    )(shard_thru, land_thru, send1, recv1, own_sem, after)
    return (send1, recv1, own_sem, out[0], out[1], out[2], out[3]), out[4]


def _gather2_wait(state, axis, after, name):
    send1, recv1, own_sem, send2, recv2, shard_thru, land_thru = state

    def body(src_ref, land, send1, recv1, own_sem, send2, recv2, after_ref, src_dead, got):
        own, sends1, arrivals1, sends2, arrivals2 = _gather2_copies(
            src_ref, land, (send1, recv1, own_sem, send2, recv2), axis, shard_thru.shape)
        for cp in sends1 + sends2:
            cp.wait_send()
        for cp in arrivals1[:1] + arrivals2:
            cp.wait_recv()
        own.wait()

    return pl.pallas_call(
        body, name=name,
        out_shape=(pltpu.HBM(shard_thru.shape, shard_thru.dtype), pltpu.HBM(land_thru.shape, land_thru.dtype)),
        in_specs=[_HBM, _HBM] + [_SEM] * 5 + [pl.BlockSpec(memory_space=pl.ANY)],
        out_specs=(_HBM, _HBM),
        input_output_aliases={0: 0, 1: 1},
        compiler_params=pltpu.CompilerParams(has_side_effects=_EFFECT),
    )(shard_thru, land_thru, send1, recv1, own_sem, send2, recv2, after)[1]


def _sum_slots(slots, name):
    _, r, c = slots.shape
    tr = _pick(r, 512, 8)

    def body(s_ref, o_ref):
        total = s_ref[0]
        for q in range(1, N_DEV):
            total = total + s_ref[q]
        o_ref[...] = total

    return pl.pallas_call(
        body, name=name, grid=(r // tr,),
        in_specs=[pl.BlockSpec((N_DEV, tr, c), lambda i: (0, i, 0))],
        out_specs=pl.BlockSpec((tr, c), lambda i: (i, 0)),
        out_shape=jax.ShapeDtypeStruct((r, c), F32),
        compiler_params=_params(("parallel",)),
    )(slots)


def kernel(x, a_norm, a_in, a_conv, a_out, b_norm, b_in, b_vnorm, b_ws, b_bs, b_out, f_norm, f_up, f_conv_w, f_conv_b, f_down, final_norm, loss_target, m_a_norm, m_a_in, m_a_conv, m_a_out, m_b_norm, m_b_in, m_b_vnorm, m_b_ws, m_b_bs, m_b_out, m_f_norm, m_f_up, m_f_conv_w, m_f_conv_b, m_f_down, m_final_norm, v_a_norm, v_a_in, v_a_conv, v_a_out, v_b_norm, v_b_in, v_b_vnorm, v_b_ws, v_b_bs, v_b_out, v_f_norm, v_f_up, v_f_conv_w, v_f_conv_b, v_f_down, v_final_norm):
    s, d = x.shape[1], x.shape[2]
    n_ffn = f_up.shape[0]
    f2 = f_up.shape[2] * N_DEV
    me = 4 * lax.axis_index("x") + 2 * lax.axis_index("y") + lax.axis_index("c")
    x0 = x.reshape(s, d)
    target = loss_target.reshape(s, d)

    wanted = [("a_in", _cast_layer(a_in, 0, "cast_a_in"), 1),
              ("small", _pack([a_conv, b_norm, b_vnorm, f_conv_w]), 0),
              ("a_out", _cast_layer(a_out, 0, "cast_a_out"), 0),
              ("f_up0", _cast_layer(f_up, 0, "cast_f_up0"), 1), ("f_down0", _cast_layer(f_down, 0, "cast_f_down0"), 0),
              ("b_in", _cast_layer(b_in, 0, "cast_b_in"), 1), ("b_out", _cast_layer(b_out, 0, "cast_b_out"), 0),
              ("f_up1", _cast_layer(f_up, 1, "cast_f_up1"), 1), ("f_down1", _cast_layer(f_down, 1, "cast_f_down1"), 0)]
    coming, tok, h0 = {}, None, None
    for n_started, (key, shard, axis) in enumerate(wanted):
        if n_started == 2:
            tok = h0 = _rmsnorm_fwd(x0, a_norm, "mixa_norm", after=tok)
        state, tok = _gather2_start(shard, axis, f"ag_start_{key}", after=tok)
        coming[key] = (state, axis)

    def pass_on(keys, after):
        for key in keys:
            state, axis = coming[key]
            state, after = _gather2_pass(state, axis, after, f"ag_pass_{key}")
            coming[key] = (state, axis)
        return after

    def arrived(key, after):
        state, axis = coming[key]
        return _gather2_wait(state, axis, after, f"ag_wait_{key}")

    cshard = a_conv.shape[2]
    fshard = f_conv_w.shape[2]
    w_a_in = arrived("a_in", pass_on(["a_in", "small"], tok))
    small_full = arrived("small", w_a_in)
    small_rows = small_full.reshape(N_DEV, -1)
    per_dev = _unpack_rows(small_rows, [(3, cshard), (cshard,), (cshard,), (n_ffn, 3, fshard)])
    a_conv_full = per_dev[0].transpose(1, 0, 2).reshape(3, d)
    b_norm_full = per_dev[1].reshape(1, d)
    b_vnorm_full = per_dev[2].reshape(1, d)
    f_conv_w_full = per_dev[3].transpose(1, 2, 0, 3).reshape(n_ffn, 3, f2)
    bs_wide = jnp.broadcast_to(b_bs[0][:, :, None], (SG_GROUPS, CHUNK, LANES))
    ws = b_ws[0]

    w_f_up, w_f_down = {}, {}

    def ffn_forward(xin, l, pass_first, pass_early, pass_late):
        h = _rmsnorm_fwd(xin, f_norm[l:l + 1], f"ffn{l}_norm", after=pass_on(pass_first, xin))
        w_f_up[l] = arrived(f"f_up{l}", pass_on(pass_early, h))
        up_g, up_a, cv_g, cv_a, act = _ffn_up_fused(h, w_f_up[l], f_conv_w_full[l], f_conv_b[l:l + 1], f"ffn{l}_up")
        up, cv = (up_g, up_a), (cv_g, cv_a)
        w_f_down[l] = arrived(f"f_down{l}", pass_on(pass_late, act))
        xout = _matmul(act, w_f_down[l], "nn", F32, f"ffn{l}_down", resid=xin, tm_cap=512, tk_cap=act.shape[1],
                       hold_b=True)
        return xout, (h, up, act, cv)

    gb, gc, xs, cva, ya = _mixa_in_fused(h0, w_a_in, a_conv_full, "mixa_in")
    bcx = (gb, gc, xs)
    w_a_out = arrived("a_out", pass_on(["a_out"], ya))
    x1 = _matmul(ya, w_a_out, "nn", F32, "mixa_out", resid=x0, tm_cap=512, tn_cap=d, hold_b=True)
    x2, saved0 = ffn_forward(x1, 0, ["f_up0"], ["f_down0"], ["b_in", "b_out", "f_up1", "f_down1"])
    h2 = _rmsnorm_fwd(x2, b_norm_full, "mixb_norm")
    w_b_in = arrived("b_in", h2)
    pre = _matmul(h2, w_b_in, "nn", BF16, "mixb_in")
    ug = _mixb_fwd(pre, b_vnorm_full, ws, bs_wide, "mixb_mid")
    w_b_out = arrived("b_out", ug)
    x3 = _matmul(ug, w_b_out, "nn", F32, "mixb_out", resid=x2, tm_cap=512, tn_cap=d, hold_b=True)
    x4, saved1 = ffn_forward(x3, 1, [], [], [])
    dx4, dx4b, loss_part, g_final = _final_loss(x4, final_norm.reshape(1, d), target, "loss_head")

    def _rs_start(grad, axis, name):
        return _exchange_start(False, grad, axis, name)

    def ffn_backward(xin, l, saved, dx, dxb):
        h, up, act, cv = saved
        g_down = _matmul(act, dxb, "tn", BF16, f"ffn{l}_down_dw", tm_cap=512, tn_cap=512, tk_cap=s, hold_b=True)
        rs_down, tok = _rs_start(g_down, 0, f"rs_start_f_down{l}")
        dup, cwg, cwa, cbg, cba = _ffn_down_dx_fused(dxb, w_f_down[l], up, cv, f_conv_w_full[l], tok,
                                                     f"ffn{l}_down_dx")
        g_cw, g_cb = jnp.concatenate([cwg, cwa], axis=1), jnp.concatenate([cbg, cba], axis=1)
        g_up = _matmul(h, dup, "tn", BF16, f"ffn{l}_up_dw", tk_cap=4096)
        rs_up, tok = _rs_start(g_up, 1, f"rs_start_f_up{l}")
        dh = _matmul(dup, w_f_up[l], "nt", BF16, f"ffn{l}_up_dx", after=tok, tm_cap=512, tn_cap=512,
                     tk_cap=dup.shape[1], hold_b=True)
        dxin, dxinb, g_norm = _rmsnorm_bwd(xin, f_norm[l:l + 1], dh, dx, f"ffn{l}_norm_bwd")
        return dxin, dxinb, (rs_up, rs_down, g_cw, g_cb, g_norm)

    dx3, dx3b, gf1 = ffn_backward(x3, 1, saved1, dx4, dx4b)
    g_b_out = _matmul(ug, dx3b, "tn", BF16, "mixb_out_dw", tk_cap=4096)
    rs_b_out, tok = _rs_start(g_b_out, 0, "rs_start_b_out")
    dug = _matmul(dx3b, w_b_out, "nt", BF16, "mixb_out_dx", after=tok)
    dpre, g_ws, g_bs_wide, g_bvnorm = _mixb_bwd(pre, dug, b_vnorm_full, ws, bs_wide, "mixb_mid_bwd")
    g_b_in = _matmul(h2, dpre, "tn", BF16, "mixb_in_dw", tk_cap=4096)
    rs_b_in, tok = _rs_start(g_b_in, 1, "rs_start_b_in")
    dh2 = _matmul(dpre, w_b_in, "nt", BF16, "mixb_in_dx", after=tok, tk_cap=dpre.shape[1], hold_b=True)
    dx2, dx2b, g_bnorm = _rmsnorm_bwd(x2, b_norm_full, dh2, dx3, "mixb_norm_bwd")
    dx1, dx1b, gf0 = ffn_backward(x1, 0, saved0, dx2, dx2b)
    g_a_out = _matmul(ya, dx1b, "tn", BF16, "mixa_out_dw", tk_cap=4096)
    rs_a_out, tok = _rs_start(g_a_out, 0, "rs_start_a_out")
    dbcx, g_aconv = _mixa_out_dx_fused(dx1b, w_a_out, bcx, cva, a_conv_full, tok, "mixa_out_dx")
    g_a_in = _matmul(h0, dbcx, "tn", BF16, "mixa_in_dw", tk_cap=4096)
    rs_a_in, tok = _rs_start(g_a_in, 1, "rs_start_a_in")
    dh0 = _matmul(dbcx, w_a_in, "nt", BF16, "mixa_in_dx", after=tok, tm_cap=512, tk_cap=dbcx.shape[1], hold_b=True)
    grad_x, _, g_anorm = _rmsnorm_bwd(x0, a_norm, dh0, dx1, "mixa_norm_bwd")

    full_shapes = [(1, LANES), (1, d), (3, d), (1, d), (1, d), (SG_GROUPS, CHUNK, CHUNK), (SG_GROUPS, CHUNK),
                   (n_ffn, d), (n_ffn, 3, f2), (n_ffn, f2), (1, d)]
    parts = [loss_part, g_anorm, g_aconv, g_bnorm, g_bvnorm, g_ws, g_bs_wide[:, :, 0],
             jnp.concatenate([gf0[4], gf1[4]], axis=0), jnp.stack([gf0[2], gf1[2]]),
             jnp.concatenate([gf0[3], gf1[3]], axis=0), g_final]
    small_part = _pack(parts)
    small_state, small_tok = _exchange_start(True, small_part, 0, "ar_start_small", after=grad_x)

    big = {}
    for name, states, axis, w, m, v in (
            ("f_down", (gf0[1], gf1[1]), 0, f_down, m_f_down, v_f_down),
            ("f_up", (gf0[0], gf1[0]), 1, f_up, m_f_up, v_f_up),
            ("b_out", (rs_b_out,), 0, b_out, m_b_out, v_b_out), ("b_in", (rs_b_in,), 1, b_in, m_b_in, v_b_in),
            ("a_out", (rs_a_out,), 0, a_out, m_a_out, v_a_out), ("a_in", (rs_a_in,), 1, a_in, m_a_in, v_a_in)):
        recvs = [_exchange_wait(False, st, axis, small_tok, f"rs_wait_{name}{l}") for l, st in enumerate(states)]
        big[name] = _adamw_sharded(recvs, w, m, v, f"adamw_{name}")

    slots = _exchange_wait(True, small_state, 0, [res[0] for res in big.values()], "ar_wait_small")
    total = _sum_slots(slots.reshape((N_DEV,) + small_part.shape), "ar_sum_small")
    (loss_v, r_anorm, r_aconv, r_bnorm, r_bvnorm, r_ws, r_bs, r_fnorm, r_fcw, r_fcb, r_final) = _unpack(total, full_shapes)
    small_grads = [
        r_anorm,
        lax.dynamic_slice_in_dim(r_aconv, me * cshard, cshard, axis=1).reshape(a_conv.shape),
        lax.dynamic_slice_in_dim(r_bnorm, me * cshard, cshard, axis=1),
        lax.dynamic_slice_in_dim(r_bvnorm, me * cshard, cshard, axis=1),
        r_ws.reshape(b_ws.shape), r_bs.reshape(b_bs.shape), r_fnorm,
        lax.dynamic_slice_in_dim(r_fcw, me * fshard, fshard, axis=2),
        r_fcb, r_final.reshape(final_norm.shape)]
    small_w = [a_norm, a_conv, b_norm, b_vnorm, b_ws, b_bs, f_norm, f_conv_w, f_conv_b, final_norm]
    small_m = [m_a_norm, m_a_conv, m_b_norm, m_b_vnorm, m_b_ws, m_b_bs, m_f_norm, m_f_conv_w, m_f_conv_b, m_final_norm]
    small_v = [v_a_norm, v_a_conv, v_b_norm, v_b_vnorm, v_b_ws, v_b_bs, v_f_norm, v_f_conv_w, v_f_conv_b, v_final_norm]
    shapes = [w.shape for w in small_w]
    packed = _adamw_packed(_pack(small_w), _pack(small_grads), _pack(small_m), _pack(small_v), "adamw_small")
    s_delta, s_m, s_v = (_unpack(p, shapes) for p in packed)
    small_names = ["a_norm", "a_conv", "b_norm", "b_vnorm", "b_ws", "b_bs", "f_norm", "f_conv_w", "f_conv_b", "final_norm"]
    small = {nm: (small_grads[i], s_delta[i], s_m[i], s_v[i]) for i, nm in enumerate(small_names)}

    order = ["a_norm", "a_in", "a_conv", "a_out", "b_norm", "b_in", "b_vnorm", "b_ws", "b_bs", "b_out",
             "f_norm", "f_up", "f_conv_w", "f_conv_b", "f_down", "final_norm"]
    res = {nm: (big[nm] if nm in big else small[nm]) for nm in order}
    outs = [loss_v[0, 0], grad_x.reshape(x.shape)]
    for k in range(4):
        outs += [res[nm][k] for nm in order]
    return tuple(outs)


def _unpack_rows(rows, shapes):
    out, off = [], 0
    for shp in shapes:
        n = math.prod(shp)
        out.append(rows[:, off:off + n].reshape((N_DEV,) + tuple(shp)))
        off += n + (-n) % PACK_GRANULE
    return out
```

```python
import math

import jax
import jax.numpy as jnp
from jax import lax
from jax.experimental import pallas as pl
from jax.experimental.pallas import tpu as pltpu

F32 = jnp.float32
BF16 = jnp.bfloat16
MESH = pl.DeviceIdType.MESH

N_DEV = 8
RMS_EPS = 1e-5
CHUNK = 128
SG_GROUPS = 8
ADAM_LR = 0.001
ADAM_B1 = 0.9
ADAM_B2 = 0.999
ADAM_EPS = 1e-08
ADAM_WD = 0.01
ADAM_STEP = 10

LANES = 128
SLAB = 16
FUSE_STRIP = 256
FUSE_HALO = 8
VMEM_LIMIT = 60 * 1024 * 1024
PACK_GRANULE = 8 * LANES


def _pick(dim, cap, mult):
    best = None
    t = mult
    while t <= min(dim, cap):
        if dim % t == 0:
            best = t
        t += mult
    return dim if best is None else best


def _params(semantics=None):
    return pltpu.CompilerParams(dimension_semantics=semantics, vmem_limit_bytes=VMEM_LIMIT)


_DIMS = {
    "nn": (((1,), (0,)), ((), ())),
    "nt": (((1,), (1,)), ((), ())),
    "tn": (((0,), (0,)), ((), ())),
}


def _matmul(a, b, mode, out_dtype, name, resid=None, after=None, tm_cap=1024, tn_cap=1024, tk_cap=2816,
            hold_b=False):
    if mode == "nn":
        (m, k), n = a.shape, b.shape[1]
    elif mode == "nt":
        (m, k), n = a.shape, b.shape[0]
    else:
        (k, m), n = a.shape, b.shape[1]
    tm, tn, tk = _pick(m, tm_cap, LANES), _pick(n, tn_cap, LANES), _pick(k, tk_cap, LANES)
    nk = k // tk
    n_in = 2 + (resid is not None) + (after is not None)

    def body(*refs):
        a_ref, b_ref = refs[:2]
        r_ref = refs[2] if resid is not None else None
        o_ref = refs[n_in]
        prod = lax.dot_general(a_ref[...], b_ref[...], _DIMS[mode], preferred_element_type=F32)

        def finish(r):
            if r_ref is not None:
                r = r + r_ref[...]
            o_ref[...] = r.astype(out_dtype)

        if nk == 1:
            finish(prod)
            return
        acc_ref = refs[n_in + 1]
        kk = pl.program_id(2)

        @pl.when(kk == 0)
        def _():
            acc_ref[...] = prod

        @pl.when(jnp.logical_and(kk > 0, kk < nk - 1))
        def _():
            acc_ref[...] += prod

        @pl.when(kk == nk - 1)
        def _():
            finish(acc_ref[...] + prod)

    def spec(block, index):
        if hold_b:
            return pl.BlockSpec(block, lambda j, i, kk: index(i, j, kk))
        return pl.BlockSpec(block, index)

    a_spec = (spec((tk, tm), lambda i, j, kk: (kk, i)) if mode == "tn"
              else spec((tm, tk), lambda i, j, kk: (i, kk)))
    b_spec = (spec((tn, tk), lambda i, j, kk: (j, kk)) if mode == "nt"
              else spec((tk, tn), lambda i, j, kk: (kk, j)))
    o_spec = spec((tm, tn), lambda i, j, kk: (i, j))
    in_specs = [a_spec, b_spec] + ([o_spec] if resid is not None else [])
    args = (a, b) + ((resid,) if resid is not None else ())
    if after is not None:
        in_specs.append(pl.BlockSpec(memory_space=pl.ANY))
        args += (after,)
    return pl.pallas_call(
        body, name=name, grid=(n // tn, m // tm, nk) if hold_b else (m // tm, n // tn, nk),
        in_specs=in_specs, out_specs=o_spec,
        out_shape=jax.ShapeDtypeStruct((m, n), out_dtype),
        scratch_shapes=[pltpu.VMEM((tm, tn), F32)] if nk > 1 else [],
        compiler_params=_params(("parallel", "parallel", "arbitrary")),
    )(*args)


def _rms_stats(xf):
    inv = lax.rsqrt(jnp.mean(xf * xf, axis=-1, keepdims=True) + RMS_EPS)
    return inv, xf * inv


def _rmsnorm_fwd(x, g, name, after=None):
    s, d = x.shape
    tm = _pick(s, 512, SLAB)
    extra = () if after is None else (after,)

    def body(x_ref, g_ref, *rest):
        _, xhat = _rms_stats(x_ref[...])
        rest[-1][...] = (xhat * g_ref[...]).astype(BF16)

    return pl.pallas_call(
        body, name=name, grid=(s // tm,),
        in_specs=[pl.BlockSpec((tm, d), lambda i: (i, 0)), pl.BlockSpec((1, d), lambda i: (0, 0))]
        + [pl.BlockSpec(memory_space=pl.ANY)] * len(extra),
        out_specs=pl.BlockSpec((tm, d), lambda i: (i, 0)),
        out_shape=jax.ShapeDtypeStruct((s, d), BF16),
        compiler_params=_params(("parallel",)),
    )(x, g, *extra)


def _rmsnorm_bwd(x, g, dh, dx_out, name):
    s, d = x.shape
    tm = _pick(s, 256, SLAB)

    def body(x_ref, g_ref, dh_ref, dxo_ref, dxi_ref, dxib_ref, dg_ref):
        inv, xhat = _rms_stats(x_ref[...])
        dhv = dh_ref[...].astype(F32)
        dxhat = dhv * g_ref[...]
        proj = jnp.mean(dxhat * xhat, axis=-1, keepdims=True)
        dx = dxo_ref[...] + inv * (dxhat - xhat * proj)
        dxi_ref[...] = dx
        dxib_ref[...] = dx.astype(BF16)
        part = jnp.sum(dhv * xhat, axis=0, keepdims=True)

        @pl.when(pl.program_id(0) == 0)
        def _():
            dg_ref[...] = part

        @pl.when(pl.program_id(0) > 0)
        def _():
            dg_ref[...] += part

    row = pl.BlockSpec((tm, d), lambda i: (i, 0))
    vec = pl.BlockSpec((1, d), lambda i: (0, 0))
    return pl.pallas_call(
        body, name=name, grid=(s // tm,),
        in_specs=[row, vec, row, row], out_specs=[row, row, vec],
        out_shape=[jax.ShapeDtypeStruct((s, d), F32), jax.ShapeDtypeStruct((s, d), BF16),
                   jax.ShapeDtypeStruct((1, d), F32)],
        compiler_params=_params(("arbitrary",)),
    )(x, g, dh, dx_out)


def _final_loss(x, g, target, name):
    s, d = x.shape
    tm = _pick(s, 256, SLAB)

    def body(x_ref, g_ref, t_ref, dx_ref, dxb_ref, loss_ref, dg_ref):
        inv, xhat = _rms_stats(x_ref[...])
        gv = g_ref[...]
        err = xhat * gv - t_ref[...]
        loss = 0.5 * jnp.sum(jnp.mean(err * err, axis=-1, keepdims=True), axis=0, keepdims=True)
        dy = err * (1.0 / d)
        dxhat = dy * gv
        proj = jnp.mean(dxhat * xhat, axis=-1, keepdims=True)
        dx = inv * (dxhat - xhat * proj)
        dx_ref[...] = dx
        dxb_ref[...] = dx.astype(BF16)
        part = jnp.sum(dy * xhat, axis=0, keepdims=True)
        loss_row = jnp.broadcast_to(loss, (1, LANES))

        @pl.when(pl.program_id(0) == 0)
        def _():
            dg_ref[...] = part
            loss_ref[...] = loss_row

        @pl.when(pl.program_id(0) > 0)
        def _():
            dg_ref[...] += part
            loss_ref[...] += loss_row

    row = pl.BlockSpec((tm, d), lambda i: (i, 0))
    vec = pl.BlockSpec((1, d), lambda i: (0, 0))
    return pl.pallas_call(
        body, name=name, grid=(s // tm,),
        in_specs=[row, vec, row],
        out_specs=[row, row, pl.BlockSpec((1, LANES), lambda i: (0, 0)), vec],
        out_shape=[jax.ShapeDtypeStruct((s, d), F32), jax.ShapeDtypeStruct((s, d), BF16),
                   jax.ShapeDtypeStruct((1, LANES), F32), jax.ShapeDtypeStruct((1, d), F32)],
        compiler_params=_params(("arbitrary",)),
    )(x, g, target)


def _mixa_in_fused(h, w_in, wc, name):
    s, d = h.shape
    tm = _pick(s, 1024, LANES)
    tn = _pick(d, 512, FUSE_STRIP)
    nj = d // tn

    def body(h_ref, wb_ref, wg_ref, wx_ref, wc_ref, gb_ref, gc_ref, xs_ref, cv_ref, y_ref, carry):
        @pl.when(pl.program_id(1) == 0)
        def _():
            carry[...] = jnp.zeros_like(carry)

        def matmul(st):
            cols = slice(st * FUSE_STRIP, (st + 1) * FUSE_STRIP)
            return tuple(jnp.dot(h_ref[...], w_ref[:, cols], preferred_element_type=F32).astype(BF16)
                         for w_ref in (wb_ref, wg_ref, wx_ref))

        n_strips = tn // FUSE_STRIP
        parts = matmul(0)
        for st in range(n_strips):
            parts_next = matmul(st + 1) if st + 1 < n_strips else None
            cols = slice(st * FUSE_STRIP, (st + 1) * FUSE_STRIP)
            for ref, part in zip((gb_ref, gc_ref, xs_ref), parts):
                ref[:, cols] = part
            p = parts[1].astype(F32) * parts[2].astype(F32)
            ext = jnp.concatenate([carry[:, cols], p], axis=0)
            s1 = pltpu.roll(ext, 1, 0)[FUSE_HALO:, :]
            s2 = pltpu.roll(ext, 2, 0)[FUSE_HALO:, :]
            carry[:, cols] = p[tm - FUSE_HALO:, :]
            cv = wc_ref[0:1, cols] * s2 + wc_ref[1:2, cols] * s1 + wc_ref[2:3, cols] * p
            cv_ref[:, cols] = cv.astype(BF16)
            y_ref[:, cols] = (parts[0].astype(F32) * cv).astype(BF16)
            parts = parts_next

    def cols_of(rows, offset):
        return pl.BlockSpec((rows, tn), lambda j, i: (0, j + offset))

    tile = pl.BlockSpec((tm, tn), lambda j, i: (i, j))
    return pl.pallas_call(
        body, name=name, grid=(nj, s // tm),
        in_specs=[pl.BlockSpec((tm, d), lambda j, i: (i, 0)), cols_of(d, 0), cols_of(d, nj), cols_of(d, 2 * nj),
                  cols_of(3, 0)],
        out_specs=[tile] * 5, out_shape=[jax.ShapeDtypeStruct((s, d), BF16)] * 5,
        scratch_shapes=[pltpu.VMEM((FUSE_HALO, tn), F32)],
        compiler_params=_params(("parallel", "arbitrary")),
    )(h, w_in, w_in, w_in, wc)


def _mixa_out_dx_fused(dxb, w_out, bcx, cv, wc, after, name):
    s, d = dxb.shape
    tm = _pick(s, 1024, LANES)
    tn = _pick(d, 512, FUSE_STRIP)
    nj, ni = d // tn, s // tm
    n_steps = nj * ni
    sub = tm // FUSE_HALO

    def body(dx_ref, w_ref, gb_ref, gc_ref, xs_ref, cv_ref, wc_ref, after_ref,
             dbcx_hbm, dwc_ref, out_buf, out_sem, carry, acc):
        j, i = pl.program_id(0), pl.program_id(1)
        step = j * ni + i
        slot = lax.rem(step, 2)
        row0 = pl.multiple_of((ni - 1 - i) * tm, tm)

        def out_copy(part, slot_=None):
            slot_ = slot if slot_ is None else slot_
            col0 = pl.multiple_of(part * d + j * tn, LANES)
            return pltpu.make_async_copy(out_buf.at[slot_, part], dbcx_hbm.at[pl.ds(row0, tm), pl.ds(col0, tn)],
                                         out_sem.at[slot_, part])

        @pl.when(step >= 2)
        def _():
            for part in range(3):
                out_copy(part).wait()

        @pl.when(i == 0)
        def _():
            carry[...] = jnp.zeros_like(carry)
            acc[...] = jnp.zeros_like(acc)

        for st in range(tn // FUSE_STRIP):
            cols = slice(st * FUSE_STRIP, (st + 1) * FUSE_STRIP)
            dyv = lax.dot_general(dx_ref[...], w_ref[cols, :], _DIMS["nt"], preferred_element_type=F32)
            gc = gc_ref[:, cols].astype(F32)
            xs = xs_ref[:, cols].astype(F32)
            d0 = dyv * gb_ref[:, cols].astype(F32)
            ext = jnp.concatenate([d0, carry[:, cols]], axis=0)
            d1 = pltpu.roll(ext, tm + FUSE_HALO - 1, 0)[:tm, :]
            d2 = pltpu.roll(ext, tm + FUSE_HALO - 2, 0)[:tm, :]
            carry[:, cols] = d0[:FUSE_HALO, :]
            dp = wc_ref[2:3, cols] * d0 + wc_ref[1:2, cols] * d1 + wc_ref[0:1, cols] * d2
            out_buf[slot, 0, :, cols] = (dyv * cv_ref[:, cols].astype(F32)).astype(BF16)
            out_buf[slot, 1, :, cols] = (dp * xs).astype(BF16)
            out_buf[slot, 2, :, cols] = (dp * gc).astype(BF16)
            p = gc * xs
            for k, term in enumerate((d2 * p, d1 * p, d0 * p)):
                acc[k, :, cols] += jnp.sum(term.reshape(sub, FUSE_HALO, FUSE_STRIP), axis=0)

        for part in range(3):
            out_copy(part).start()

        @pl.when(i == ni - 1)
        def _():
            for k in range(3):
                dwc_ref[k:k + 1, :] = jnp.sum(acc[k], axis=0, keepdims=True)

        @pl.when(step == n_steps - 1)
        def _():
            for part in range(3):
                out_copy(part).wait()
                if n_steps > 1:
                    out_copy(part, 1 - slot).wait()

    tile = pl.BlockSpec((tm, tn), lambda j, i: (ni - 1 - i, j))
    return pl.pallas_call(
        body, name=name, grid=(nj, ni),
        in_specs=[pl.BlockSpec((tm, d), lambda j, i: (ni - 1 - i, 0)), pl.BlockSpec((tn, d), lambda j, i: (j, 0)),
                  tile, tile, tile, tile, pl.BlockSpec((3, tn), lambda j, i: (0, j)),
                  pl.BlockSpec(memory_space=pl.ANY)],
        out_specs=[pl.BlockSpec(memory_space=pl.ANY), pl.BlockSpec((3, tn), lambda j, i: (0, j))],
        out_shape=[jax.ShapeDtypeStruct((s, 3 * d), BF16), jax.ShapeDtypeStruct((3, d), F32)],
        scratch_shapes=[pltpu.VMEM((2, 3, tm, tn), BF16), pltpu.SemaphoreType.DMA((2, 3)),
                        pltpu.VMEM((FUSE_HALO, tn), F32), pltpu.VMEM((3, FUSE_HALO, tn), F32)],
        compiler_params=_params(("arbitrary", "arbitrary")),
    )(dxb, w_out, bcx[0], bcx[1], bcx[2], cv, wc, after)


def _sigmoid(z):
    return 0.5 * jnp.tanh(0.5 * z) + 0.5


def _ffn_up_fused(h, w_up, cw, cb, name):
    s, d = h.shape
    f = w_up.shape[1] // 2
    tm = _pick(s, 1024, LANES)
    tn = _pick(f, 512, FUSE_STRIP)
    nj = f // tn

    def body(h_ref, wg_ref, wa_ref, cwg_ref, cwa_ref, cbg_ref, cba_ref,
             upg_ref, upa_ref, cvg_ref, cva_ref, act_ref, carry_g, carry_a):
        @pl.when(pl.program_id(1) == 0)
        def _():
            carry_g[...] = jnp.zeros_like(carry_g)
            carry_a[...] = jnp.zeros_like(carry_a)

        def matmul(st):
            cols = slice(st * FUSE_STRIP, (st + 1) * FUSE_STRIP)
            return tuple(jnp.dot(h_ref[...], w_ref[:, cols], preferred_element_type=F32).astype(BF16)
                         for w_ref in (wg_ref, wa_ref))

        def conv(up, cw_ref, cb_ref, carry, up_ref, cv_ref, cols):
            up_ref[:, cols] = up
            x = up.astype(F32)
            ext = jnp.concatenate([carry[:, cols], x], axis=0)
            s1 = pltpu.roll(ext, 1, 0)[FUSE_HALO:, :]
            s2 = pltpu.roll(ext, 2, 0)[FUSE_HALO:, :]
            carry[:, cols] = x[tm - FUSE_HALO:, :]
            cv = cw_ref[0:1, cols] * s2 + cw_ref[1:2, cols] * s1 + cw_ref[2:3, cols] * x + cb_ref[:, cols]
            cv_ref[:, cols] = cv.astype(BF16)
            return cv

        n_strips = tn // FUSE_STRIP
        ups = matmul(0)
        for st in range(n_strips):
            ups_next = matmul(st + 1) if st + 1 < n_strips else None
            cols = slice(st * FUSE_STRIP, (st + 1) * FUSE_STRIP)
            gcv = conv(ups[0], cwg_ref, cbg_ref, carry_g, upg_ref, cvg_ref, cols)
            acv = conv(ups[1], cwa_ref, cba_ref, carry_a, upa_ref, cva_ref, cols)
            act_ref[:, cols] = (gcv * _sigmoid(gcv) * acv).astype(BF16)
            ups = ups_next

    def cols_of(rows, offset):
        return pl.BlockSpec((rows, tn), lambda j, i: (0, j + offset))

    tile = pl.BlockSpec((tm, tn), lambda j, i: (i, j))
    out = jax.ShapeDtypeStruct((s, f), BF16)
    return pl.pallas_call(
        body, name=name, grid=(nj, s // tm),
        in_specs=[pl.BlockSpec((tm, d), lambda j, i: (i, 0)), cols_of(d, 0), cols_of(d, nj),
                  cols_of(3, 0), cols_of(3, nj), cols_of(1, 0), cols_of(1, nj)],
        out_specs=[tile] * 5, out_shape=[out] * 5,
        scratch_shapes=[pltpu.VMEM((FUSE_HALO, tn), F32)] * 2,
        compiler_params=_params(("parallel", "arbitrary")),
    )(h, w_up, w_up, cw, cw, cb, cb)


def _ffn_down_dx_fused(dxb, w_down, up, cv, cw, after, name):
    s, d = dxb.shape
    f = w_down.shape[0]
    tm = _pick(s, 1024, LANES)
    tn = _pick(f, 512, FUSE_STRIP)
    nj, ni = f // tn, s // tm
    n_steps = nj * ni
    sub = tm // FUSE_HALO

    def body(dx_ref, w_ref, upg_ref, upa_ref, cvg_ref, cva_ref, cwg_ref, cwa_ref, after_ref,
             dup_hbm, dcwg_ref, dcwa_ref, dcbg_ref, dcba_ref, out_buf, out_sem, carry, acc):
        j, i = pl.program_id(0), pl.program_id(1)
        step = j * ni + i
        slot = lax.rem(step, 2)
        row0 = pl.multiple_of((ni - 1 - i) * tm, tm)

        def out_copy(half):
            col0 = pl.multiple_of(half * f + j * tn, LANES)
            return pltpu.make_async_copy(out_buf.at[slot, half], dup_hbm.at[pl.ds(row0, tm), pl.ds(col0, tn)],
                                         out_sem.at[slot, half])

        @pl.when(step >= 2)
        def _():
            for half in range(2):
                out_copy(half).wait()

        @pl.when(i == 0)
        def _():
            carry[...] = jnp.zeros_like(carry)
            acc[...] = jnp.zeros_like(acc)

        for st in range(tn // FUSE_STRIP):
            cols = slice(st * FUSE_STRIP, (st + 1) * FUSE_STRIP)
            dact = lax.dot_general(dx_ref[...], w_ref[cols, :], _DIMS["nt"], preferred_element_type=F32)
            gcv = cvg_ref[:, cols].astype(F32)
            acv = cva_ref[:, cols].astype(F32)
            sg = _sigmoid(gcv)
            dd = (dact * acv * (sg * (1.0 + gcv * (1.0 - sg))), dact * (gcv * sg))
            for half, (up_ref, cw_ref) in enumerate(((upg_ref, cwg_ref), (upa_ref, cwa_ref))):
                x = up_ref[:, cols].astype(F32)
                d0 = dd[half]
                ext = jnp.concatenate([d0, carry[half, :, cols]], axis=0)
                d1 = pltpu.roll(ext, tm + FUSE_HALO - 1, 0)[:tm, :]
                d2 = pltpu.roll(ext, tm + FUSE_HALO - 2, 0)[:tm, :]
                carry[half, :, cols] = d0[:FUSE_HALO, :]
                out_buf[slot, half, :, cols] = (cw_ref[2:3, cols] * d0 + cw_ref[1:2, cols] * d1
                                                + cw_ref[0:1, cols] * d2).astype(BF16)
                for k, term in enumerate((d2 * x, d1 * x, d0 * x, d0)):
                    acc[half, k, :, cols] += jnp.sum(term.reshape(sub, FUSE_HALO, FUSE_STRIP), axis=0)

        for half in range(2):
            out_copy(half).start()

        @pl.when(i == ni - 1)
        def _():
            for half, (dcw_ref, dcb_ref) in enumerate(((dcwg_ref, dcbg_ref), (dcwa_ref, dcba_ref))):
                for k in range(3):
                    dcw_ref[k:k + 1, :] = jnp.sum(acc[half, k], axis=0, keepdims=True)
                dcb_ref[...] = jnp.sum(acc[half, 3], axis=0, keepdims=True)

        @pl.when(step == n_steps - 1)
        def _():
            for half in range(2):
                out_copy(half).wait()
                if n_steps > 1:
                    pltpu.make_async_copy(out_buf.at[1 - slot, half], dup_hbm.at[pl.ds(row0, tm), pl.ds(0, tn)],
                                          out_sem.at[1 - slot, half]).wait()

    tile = pl.BlockSpec((tm, tn), lambda j, i: (ni - 1 - i, j))

    def cols_of(rows, offset):
        return pl.BlockSpec((rows, tn), lambda j, i: (0, j + offset))

    small = pl.BlockSpec((3, tn), lambda j, i: (0, j)), pl.BlockSpec((1, tn), lambda j, i: (0, j))
    return pl.pallas_call(
        body, name=name, grid=(nj, ni),
        in_specs=[pl.BlockSpec((tm, d), lambda j, i: (ni - 1 - i, 0)), pl.BlockSpec((tn, d), lambda j, i: (j, 0)),
                  tile, tile, tile, tile, cols_of(3, 0), cols_of(3, nj), pl.BlockSpec(memory_space=pl.ANY)],
        out_specs=[pl.BlockSpec(memory_space=pl.ANY), small[0], small[0], small[1], small[1]],
        out_shape=[jax.ShapeDtypeStruct((s, 2 * f), BF16), jax.ShapeDtypeStruct((3, f), F32),
                   jax.ShapeDtypeStruct((3, f), F32), jax.ShapeDtypeStruct((1, f), F32),
                   jax.ShapeDtypeStruct((1, f), F32)],
        scratch_shapes=[pltpu.VMEM((2, 2, tm, tn), BF16), pltpu.SemaphoreType.DMA((2, 2)),
                        pltpu.VMEM((2, FUSE_HALO, tn), F32), pltpu.VMEM((2, 4, FUSE_HALO, tn), F32)],
        compiler_params=_params(("arbitrary", "arbitrary")),
    )(dxb, w_down, up[0], up[1], cv[0], cv[1], cw, cw, after)


_GELU_C = math.sqrt(2.0 / math.pi)


def _gelu(x):
    th = jnp.tanh(_GELU_C * (x + 0.044715 * (x * x * x)))
    return x * (0.5 * (1.0 + th)), th


def _gelu_grad(x, th):
    return 0.5 * (1.0 + th) + 0.5 * x * (1.0 - th * th) * (_GELU_C * (1.0 + 3.0 * 0.044715 * (x * x)))


def _masked_ws(ws_ref, h):
    t = lax.broadcasted_iota(jnp.int32, (CHUNK, CHUNK), 0)
    sx = lax.broadcasted_iota(jnp.int32, (CHUNK, CHUNK), 1)
    return jnp.where(sx <= t, ws_ref[h], 0.0)


def _mixb_fwd(pre, gv, ws, bs_wide, name):
    s, w2 = pre.shape
    w = w2 // 2
    gw = w // SG_GROUPS

    def body(pre_ref, gv_ref, ws_ref, bs_ref, o_ref):
        zu, _ = _gelu(pre_ref[:, :w].astype(F32))
        zv, _ = _gelu(pre_ref[:, w:].astype(F32))
        _, vhat = _rms_stats(zv)
        vn = (vhat * gv_ref[...]).astype(BF16)
        for h in range(SG_GROUPS):
            cols = slice(h * gw, (h + 1) * gw)
            wsm = _masked_ws(ws_ref, h).astype(BF16)
            gate = jnp.dot(wsm, vn[:, cols], preferred_element_type=F32)
            gate = gate + jnp.tile(bs_ref[h], (1, gw // LANES))
            o_ref[:, cols] = (zu[:, cols] * gate).astype(BF16)

    return pl.pallas_call(
        body, name=name, grid=(s // CHUNK,),
        in_specs=[pl.BlockSpec((CHUNK, w2), lambda i: (i, 0)), pl.BlockSpec((1, w), lambda i: (0, 0)),
                  pl.BlockSpec((SG_GROUPS, CHUNK, CHUNK), lambda i: (0, 0, 0)),
                  pl.BlockSpec((SG_GROUPS, CHUNK, LANES), lambda i: (0, 0, 0))],
        out_specs=pl.BlockSpec((CHUNK, w), lambda i: (i, 0)),
        out_shape=jax.ShapeDtypeStruct((s, w), BF16),
        compiler_params=_params(("parallel",)),
    )(pre, gv, ws, bs_wide)


def _mixb_bwd(pre, dug, gv, ws, bs_wide, name):
    s, w2 = pre.shape
    w = w2 // 2
    gw = w // SG_GROUPS

    def body(pre_ref, dug_ref, gv_ref, ws_ref, bs_ref, o_ref, dws_ref, dbs_ref, dgv_ref, dvn_ref):
        first = pl.program_id(0) == 0

        @pl.when(first)
        def _():
            dws_ref[...] = jnp.zeros_like(dws_ref)
            dbs_ref[...] = jnp.zeros_like(dbs_ref)

        pu = pre_ref[:, :w].astype(F32)
        pv = pre_ref[:, w:].astype(F32)
        zu, thu = _gelu(pu)
        zv, thv = _gelu(pv)
        inv, vhat = _rms_stats(zv)
        gvv = gv_ref[...]
        vn = (vhat * gvv).astype(BF16)
        for h in range(SG_GROUPS):
            cols = slice(h * gw, (h + 1) * gw)
            wsm = _masked_ws(ws_ref, h).astype(BF16)
            gate = jnp.dot(wsm, vn[:, cols], preferred_element_type=F32)
            gate = gate + jnp.tile(bs_ref[h], (1, gw // LANES))
            dug_h = dug_ref[:, cols].astype(F32)
            dgate = dug_h * zu[:, cols]
            dgate_b = dgate.astype(BF16)
            o_ref[:, cols] = (dug_h * gate * _gelu_grad(pu[:, cols], thu[:, cols])).astype(BF16)
            dbs_ref[h] += jnp.broadcast_to(jnp.sum(dgate, axis=-1, keepdims=True), (CHUNK, LANES))
            dws = lax.dot_general(dgate_b, vn[:, cols], _DIMS["nt"], preferred_element_type=F32)
            t = lax.broadcasted_iota(jnp.int32, (CHUNK, CHUNK), 0)
            sx = lax.broadcasted_iota(jnp.int32, (CHUNK, CHUNK), 1)
            dws_ref[h] += jnp.where(sx <= t, dws, 0.0)
            dvn_ref[:, cols] = lax.dot_general(wsm, dgate_b, _DIMS["tn"], preferred_element_type=F32)
        dvn = dvn_ref[...]
        part = jnp.sum(dvn * vhat, axis=0, keepdims=True)

        @pl.when(first)
        def _():
            dgv_ref[...] = part

        @pl.when(jnp.logical_not(first))
        def _():
            dgv_ref[...] += part

        dvhat = dvn * gvv
        dzv = inv * (dvhat - vhat * jnp.mean(dvhat * vhat, axis=-1, keepdims=True))
        o_ref[:, w:] = (dzv * _gelu_grad(pv, thv)).astype(BF16)

    return pl.pallas_call(
        body, name=name, grid=(s // CHUNK,),
        in_specs=[pl.BlockSpec((CHUNK, w2), lambda i: (i, 0)), pl.BlockSpec((CHUNK, w), lambda i: (i, 0)),
                  pl.BlockSpec((1, w), lambda i: (0, 0)),
                  pl.BlockSpec((SG_GROUPS, CHUNK, CHUNK), lambda i: (0, 0, 0)),
                  pl.BlockSpec((SG_GROUPS, CHUNK, LANES), lambda i: (0, 0, 0))],
        out_specs=[pl.BlockSpec((CHUNK, w2), lambda i: (i, 0)),
                   pl.BlockSpec((SG_GROUPS, CHUNK, CHUNK), lambda i: (0, 0, 0)),
                   pl.BlockSpec((SG_GROUPS, CHUNK, LANES), lambda i: (0, 0, 0)),
                   pl.BlockSpec((1, w), lambda i: (0, 0))],
        out_shape=[jax.ShapeDtypeStruct((s, w2), BF16), jax.ShapeDtypeStruct((SG_GROUPS, CHUNK, CHUNK), F32),
                   jax.ShapeDtypeStruct((SG_GROUPS, CHUNK, LANES), F32), jax.ShapeDtypeStruct((1, w), F32)],
        scratch_shapes=[pltpu.VMEM((CHUNK, w), F32)],
        compiler_params=_params(("arbitrary",)),
    )(pre, dug, gv, ws, bs_wide)


def _cast_layer(w3, layer, name):
    _, r, c = w3.shape
    tr = _pick(r, 256, SLAB)

    def body(w_ref, o_ref):
        o_ref[...] = w_ref[...].astype(BF16)

    return pl.pallas_call(
        body, name=name, grid=(r // tr,),
        in_specs=[pl.BlockSpec((None, tr, c), lambda i: (layer, i, 0))],
        out_specs=pl.BlockSpec((tr, c), lambda i: (i, 0)),
        out_shape=jax.ShapeDtypeStruct((r, c), BF16),
        compiler_params=_params(("parallel",)),
    )(w3)


def _adamw_math(w, g, m, v):
    m = ADAM_B1 * m + (1.0 - ADAM_B1) * g
    v = ADAM_B2 * v + (1.0 - ADAM_B2) * (g * g)
    m_hat = m / (1.0 - ADAM_B1 ** ADAM_STEP)
    v_hat = v / (1.0 - ADAM_B2 ** ADAM_STEP)
    delta = -ADAM_LR * (m_hat / (jnp.sqrt(v_hat) + ADAM_EPS) + ADAM_WD * w)
    return delta, m, v


def _adamw_sharded(recvs, w, m, v, name):
    nl, r, c = w.shape
    tc = _pick(c, 1536, LANES)
    tr = _pick(r, 64, SLAB)

    def body(*refs):
        recv_refs = refs[:nl]
        w_ref, m_ref, v_ref, g_ref, d_ref, nm_ref, nv_ref = refs[nl:]
        for layer, recv_ref in enumerate(recv_refs):
            @pl.when(pl.program_id(0) == layer)
            def _():
                g = recv_ref[0].astype(F32)
                for q in range(1, N_DEV):
                    g = g + recv_ref[q].astype(F32)
                delta, nm, nv = _adamw_math(w_ref[...], g, m_ref[...], v_ref[...])
                g_ref[...] = g
                d_ref[...] = delta
                nm_ref[...] = nm
                nv_ref[...] = nv

    def recv_spec(layer):
        return pl.BlockSpec((N_DEV, tr, tc),
                            lambda l, i, j: (0, jnp.where(l == layer, i, 0), jnp.where(l == layer, j, 0)))

    blk = pl.BlockSpec((None, tr, tc), lambda l, i, j: (l, i, j))
    out = jax.ShapeDtypeStruct((nl, r, c), F32)
    return pl.pallas_call(
        body, name=name, grid=(nl, r // tr, c // tc),
        in_specs=[recv_spec(layer) for layer in range(nl)] + [blk, blk, blk],
        out_specs=[blk] * 4, out_shape=[out] * 4,
        compiler_params=_params(("parallel",) * 3),
    )(*recvs, w, m, v)


def _adamw_packed(w, g, m, v, name):
    r, c = w.shape
    tr = _pick(r, 256, 8)

    def body(w_ref, g_ref, m_ref, v_ref, d_ref, nm_ref, nv_ref):
        delta, nm, nv = _adamw_math(w_ref[...], g_ref[...], m_ref[...], v_ref[...])
        d_ref[...] = delta
        nm_ref[...] = nm
        nv_ref[...] = nv

    blk = pl.BlockSpec((tr, c), lambda i: (i, 0))
    out = jax.ShapeDtypeStruct((r, c), F32)
    return pl.pallas_call(
        body, name=name, grid=(r // tr,), in_specs=[blk] * 4, out_specs=[blk] * 3, out_shape=[out] * 3,
        compiler_params=_params(("parallel",)),
    )(w, g, m, v)


def _pack(arrays):
    parts = []
    for a in arrays:
        flat = a.reshape(-1).astype(F32)
        pad = (-flat.shape[0]) % PACK_GRANULE
        parts.append(jnp.pad(flat, (0, pad)) if pad else flat)
    return jnp.concatenate(parts).reshape(-1, LANES)


def _unpack(buf, shapes):
    flat = buf.reshape(-1)
    out, off = [], 0
    for shp in shapes:
        n = math.prod(shp)
        out.append(flat[off:off + n].reshape(shp))
        off += n + (-n) % PACK_GRANULE
    return out


def _mesh_pos():
    return lax.axis_index("x"), lax.axis_index("y"), lax.axis_index("c")


def _coords(q):
    return q // 4, (q // 2) % 2, q % 2


def _shard_of(ref, q, shard_shape, axis):
    r, c = shard_shape
    if axis == 0:
        return ref.at[pl.ds(pl.multiple_of(q * r, SLAB), r), :]
    return ref.at[:, pl.ds(pl.multiple_of(q * c, LANES), c)]


_HBM = pl.BlockSpec(memory_space=pltpu.HBM)
_SEM = pl.BlockSpec(memory_space=pltpu.SEMAPHORE)
_EFFECT = pltpu.SideEffectType.DATAFLOW_SIDE_EFFECTING


def _exchange_shapes(gather, src_shape, axis):
    r, c = src_shape
    if gather:
        return (r, c), ((r * N_DEV, c) if axis == 0 else (r, c * N_DEV))
    shard = (r // N_DEV, c) if axis == 0 else (r, c // N_DEV)
    return shard, (N_DEV,) + shard


def _exchange_copies(gather, src, land, sems, axis):
    send_sems, recv_sems, own_sem = sems
    x, y, c_ = _mesh_pos()
    me = 4 * x + 2 * y + c_
    shard, _ = _exchange_shapes(gather, src.shape, axis)

    def piece(q):
        return src if gather else _shard_of(src, q, shard, axis)

    def place(q):
        return _shard_of(land, q, shard, axis) if gather else land.at[q]

    own = pltpu.make_async_copy(piece(me), place(me), own_sem.at[0])
    sends, arrivals = [], []
    for step in range(1, N_DEV):
        to = (me + step) % N_DEV
        frm = (me + N_DEV - step) % N_DEV
        sends.append(pltpu.make_async_remote_copy(
            src_ref=piece(to), dst_ref=place(me), send_sem=send_sems.at[step - 1], recv_sem=recv_sems.at[step - 1],
            device_id=_coords(to), device_id_type=MESH))
        arrivals.append(pltpu.make_async_remote_copy(
            src_ref=piece(me), dst_ref=place(frm), send_sem=send_sems.at[step - 1], recv_sem=recv_sems.at[step - 1],
            device_id=_coords(frm), device_id_type=MESH))
    return own, sends, arrivals


def _exchange_start(gather, src, axis, name, after=None):
    _, land_shape = _exchange_shapes(gather, src.shape, axis)
    extra = () if after is None else (after,)

    def body(*refs):
        src_ref, land = refs[:2]
        send_sems, recv_sems, own_sem = refs[2 + len(extra):5 + len(extra)]
        own, sends, _ = _exchange_copies(gather, src_ref, land, (send_sems, recv_sems, own_sem), axis)
        own.start()
        for cp in sends:
            cp.start()
        refs[-1][...] = jnp.zeros_like(refs[-1])

    out = pl.pallas_call(
        body, name=name,
        out_shape=(pltpu.SemaphoreType.DMA((N_DEV - 1,)), pltpu.SemaphoreType.DMA((N_DEV - 1,)),
                   pltpu.SemaphoreType.DMA((1,)), pltpu.HBM(src.shape, src.dtype),
                   pltpu.HBM(land_shape, src.dtype), jax.ShapeDtypeStruct((8, LANES), F32)),
        in_specs=[_HBM, _HBM] + [pl.BlockSpec(memory_space=pl.ANY)] * len(extra),
        out_specs=(_SEM, _SEM, _SEM, _HBM, _HBM, pl.BlockSpec(memory_space=pltpu.VMEM)),
        input_output_aliases={0: 3, 1: 4},
        compiler_params=pltpu.CompilerParams(has_side_effects=_EFFECT),
    )(pltpu.with_memory_space_constraint(src, pltpu.HBM),
      pltpu.with_memory_space_constraint(lax.empty(land_shape, src.dtype), pltpu.HBM), *extra)
    return out[:5], out[5]


def _exchange_wait(gather, state, axis, after, name):
    send_sems, recv_sems, own_sem, src_thru, land_thru = state
    after = tuple(after) if isinstance(after, (tuple, list)) else (after,)

    def body(src, land, send_sems, recv_sems, own_sem, *rest):
        own, sends, arrivals = _exchange_copies(gather, src, land, (send_sems, recv_sems, own_sem), axis)
        for cp in sends:
            cp.wait_send()
        for cp in arrivals:
            cp.wait_recv()
        own.wait()

    return pl.pallas_call(
        body, name=name,
        out_shape=(pltpu.HBM(src_thru.shape, src_thru.dtype), pltpu.HBM(land_thru.shape, land_thru.dtype)),
        in_specs=[_HBM, _HBM, _SEM, _SEM, _SEM] + [pl.BlockSpec(memory_space=pl.ANY)] * len(after),
        out_specs=(_HBM, _HBM),
        input_output_aliases={0: 0, 1: 1},
        compiler_params=pltpu.CompilerParams(has_side_effects=_EFFECT),
    )(src_thru, land_thru, send_sems, recv_sems, own_sem, *after)[1]


def _gather2_copies(shard_ref, land, sems, axis, shard_shape):
    send1, recv1, own_sem, send2, recv2 = sems
    x, y, c = _mesh_pos()
    me, sibling = (x, y, c), (x, y, 1 - c)
    chips = [(1 - x, y), (x, 1 - y), (1 - x, 1 - y)]

    def region(dev):
        px, py, pc = dev
        return _shard_of(land, 4 * px + 2 * py + pc, shard_shape, axis)

    def copy(src, block, to, send, recv):
        return pltpu.make_async_remote_copy(src_ref=src, dst_ref=region(block), send_sem=send, recv_sem=recv,
                                            device_id=to, device_id_type=MESH)

    own = pltpu.make_async_copy(shard_ref, region(me), own_sem.at[0])
    peers = [sibling] + [(*chip, c) for chip in chips]
    sends1 = [copy(shard_ref, me, to, send1.at[k], recv1.at[k]) for k, to in enumerate(peers)]
    arrivals1 = [copy(shard_ref, frm, frm, send1.at[k], recv1.at[k]) for k, frm in enumerate(peers)]
    sends2, arrivals2 = [], []
    if send2 is not None:
        for j, chip in enumerate(chips):
            sends2.append(copy(region((*chip, c)), (*chip, c), sibling, send2.at[j], recv2.at[j]))
            arrivals2.append(copy(region((*chip, 1 - c)), (*chip, 1 - c), sibling, send2.at[j], recv2.at[j]))
    return own, sends1, arrivals1, sends2, arrivals2


def _gather2_start(shard, axis, name, after=None):
    _, land_shape = _exchange_shapes(True, shard.shape, axis)
    extra = () if after is None else (after,)

    def body(*refs):
        src_ref, land = refs[:2]
        send1, recv1, own_sem = refs[2 + len(extra):5 + len(extra)]
        own, sends1, _, _, _ = _gather2_copies(src_ref, land, (send1, recv1, own_sem, None, None), axis, shard.shape)
        own.start()
        for cp in sends1[1:] + sends1[:1]:
            cp.start()
        refs[-1][...] = jnp.zeros_like(refs[-1])

    out = pl.pallas_call(
        body, name=name,
        out_shape=(pltpu.SemaphoreType.DMA((4,)), pltpu.SemaphoreType.DMA((4,)), pltpu.SemaphoreType.DMA((1,)),
                   pltpu.HBM(shard.shape, shard.dtype), pltpu.HBM(land_shape, shard.dtype),
                   jax.ShapeDtypeStruct((8, LANES), F32)),
        in_specs=[_HBM, _HBM] + [pl.BlockSpec(memory_space=pl.ANY)] * len(extra),
        out_specs=(_SEM, _SEM, _SEM, _HBM, _HBM, pl.BlockSpec(memory_space=pltpu.VMEM)),
        input_output_aliases={0: 3, 1: 4},
        compiler_params=pltpu.CompilerParams(has_side_effects=_EFFECT),
    )(pltpu.with_memory_space_constraint(shard, pltpu.HBM),
      pltpu.with_memory_space_constraint(lax.empty(land_shape, shard.dtype), pltpu.HBM), *extra)
    return out[:5], out[5]


def _gather2_pass(state, axis, after, name):
    send1, recv1, own_sem, shard_thru, land_thru = state

    def body(src_ref, land, send1, recv1, own_sem, after_ref, send2, recv2, src_out, land_out, token):
        _, _, arrivals1, sends2, _ = _gather2_copies(src_ref, land, (send1, recv1, own_sem, send2, recv2), axis,
                                                     shard_thru.shape)
        for arrival, fwd in zip(arrivals1[1:], sends2):
            arrival.wait_recv()
            fwd.start()
        token[...] = jnp.zeros_like(token)

    out = pl.pallas_call(
        body, name=name,
        out_shape=(pltpu.SemaphoreType.DMA((3,)), pltpu.SemaphoreType.DMA((3,)),
                   pltpu.HBM(shard_thru.shape, shard_thru.dtype), pltpu.HBM(land_thru.shape, land_thru.dtype),
                   jax.ShapeDtypeStruct((8, LANES), F32)),
        in_specs=[_HBM, _HBM, _SEM, _SEM, _SEM, pl.BlockSpec(memory_space=pl.ANY)],
        out_specs=(_SEM, _SEM, _HBM, _HBM, pl.BlockSpec(memory_space=pltpu.VMEM)),
        input_output_aliases={0: 2, 1: 3},
        compiler_params=pltpu.CompilerParams(has_side_effects=_EFFECT),
    )(shard_thru, land_thru, send1, recv1, own_sem, after)
    return (send1, recv1, own_sem, out[0], out[1], out[2], out[3]), out[4]


def _gather2_wait(state, axis, after, name):
    send1, recv1, own_sem, send2, recv2, shard_thru, land_thru = state

    def body(src_ref, land, send1, recv1, own_sem, send2, recv2, after_ref, src_dead, got):
        own, sends1, arrivals1, sends2, arrivals2 = _gather2_copies(
            src_ref, land, (send1, recv1, own_sem, send2, recv2), axis, shard_thru.shape)
        for cp in sends1 + sends2:
            cp.wait_send()
        for cp in arrivals1[:1] + arrivals2:
            cp.wait_recv()
        own.wait()

    return pl.pallas_call(
        body, name=name,
        out_shape=(pltpu.HBM(shard_thru.shape, shard_thru.dtype), pltpu.HBM(land_thru.shape, land_thru.dtype)),
        in_specs=[_HBM, _HBM] + [_SEM] * 5 + [pl.BlockSpec(memory_space=pl.ANY)],
        out_specs=(_HBM, _HBM),
        input_output_aliases={0: 0, 1: 1},
        compiler_params=pltpu.CompilerParams(has_side_effects=_EFFECT),
    )(shard_thru, land_thru, send1, recv1, own_sem, send2, recv2, after)[1]


def _sum_slots(slots, name):
    _, r, c = slots.shape
    tr = _pick(r, 512, 8)

    def body(s_ref, o_ref):
        total = s_ref[0]
        for q in range(1, N_DEV):
            total = total + s_ref[q]
        o_ref[...] = total

    return pl.pallas_call(
        body, name=name, grid=(r // tr,),
        in_specs=[pl.BlockSpec((N_DEV, tr, c), lambda i: (0, i, 0))],
        out_specs=pl.BlockSpec((tr, c), lambda i: (i, 0)),
        out_shape=jax.ShapeDtypeStruct((r, c), F32),
        compiler_params=_params(("parallel",)),
    )(slots)


def kernel(x, a_norm, a_in, a_conv, a_out, b_norm, b_in, b_vnorm, b_ws, b_bs, b_out, f_norm, f_up, f_conv_w, f_conv_b, f_down, final_norm, loss_target, m_a_norm, m_a_in, m_a_conv, m_a_out, m_b_norm, m_b_in, m_b_vnorm, m_b_ws, m_b_bs, m_b_out, m_f_norm, m_f_up, m_f_conv_w, m_f_conv_b, m_f_down, m_final_norm, v_a_norm, v_a_in, v_a_conv, v_a_out, v_b_norm, v_b_in, v_b_vnorm, v_b_ws, v_b_bs, v_b_out, v_f_norm, v_f_up, v_f_conv_w, v_f_conv_b, v_f_down, v_final_norm):
    s, d = x.shape[1], x.shape[2]
    n_ffn = f_up.shape[0]
    f2 = f_up.shape[2] * N_DEV
    me = 4 * lax.axis_index("x") + 2 * lax.axis_index("y") + lax.axis_index("c")
    x0 = x.reshape(s, d)
    target = loss_target.reshape(s, d)

    wanted = [("a_in", _cast_layer(a_in, 0, "cast_a_in"), 1),
              ("small", _pack([a_conv, b_norm, b_vnorm, f_conv_w]), 0),
              ("a_out", _cast_layer(a_out, 0, "cast_a_out"), 0),
              ("f_up0", _cast_layer(f_up, 0, "cast_f_up0"), 1), ("f_down0", _cast_layer(f_down, 0, "cast_f_down0"), 0),
              ("b_in", _cast_layer(b_in, 0, "cast_b_in"), 1), ("b_out", _cast_layer(b_out, 0, "cast_b_out"), 0),
              ("f_up1", _cast_layer(f_up, 1, "cast_f_up1"), 1), ("f_down1", _cast_layer(f_down, 1, "cast_f_down1"), 0)]
    coming, tok, h0 = {}, None, None
    for n_started, (key, shard, axis) in enumerate(wanted):
        if n_started == 2:
            tok = h0 = _rmsnorm_fwd(x0, a_norm, "mixa_norm", after=tok)
        state, tok = _gather2_start(shard, axis, f"ag_start_{key}", after=tok)
        coming[key] = (state, axis)

    def pass_on(keys, after):
        for key in keys:
            state, axis = coming[key]
            state, after = _gather2_pass(state, axis, after, f"ag_pass_{key}")
            coming[key] = (state, axis)
        return after

    def arrived(key, after):
        state, axis = coming[key]
        return _gather2_wait(state, axis, after, f"ag_wait_{key}")

    cshard = a_conv.shape[2]
    fshard = f_conv_w.shape[2]
    w_a_in = arrived("a_in", pass_on(["a_in", "small"], tok))
    small_full = arrived("small", w_a_in)
    small_rows = small_full.reshape(N_DEV, -1)
    per_dev = _unpack_rows(small_rows, [(3, cshard), (cshard,), (cshard,), (n_ffn, 3, fshard)])
    a_conv_full = per_dev[0].transpose(1, 0, 2).reshape(3, d)
    b_norm_full = per_dev[1].reshape(1, d)
    b_vnorm_full = per_dev[2].reshape(1, d)
    f_conv_w_full = per_dev[3].transpose(1, 2, 0, 3).reshape(n_ffn, 3, f2)
    bs_wide = jnp.broadcast_to(b_bs[0][:, :, None], (SG_GROUPS, CHUNK, LANES))
    ws = b_ws[0]

    w_f_up, w_f_down = {}, {}

    def ffn_forward(xin, l, pass_first, pass_early, pass_late):
        h = _rmsnorm_fwd(xin, f_norm[l:l + 1], f"ffn{l}_norm", after=pass_on(pass_first, xin))
        w_f_up[l] = arrived(f"f_up{l}", pass_on(pass_early, h))
        up_g, up_a, cv_g, cv_a, act = _ffn_up_fused(h, w_f_up[l], f_conv_w_full[l], f_conv_b[l:l + 1], f"ffn{l}_up")
        up, cv = (up_g, up_a), (cv_g, cv_a)
        w_f_down[l] = arrived(f"f_down{l}", pass_on(pass_late, act))
        xout = _matmul(act, w_f_down[l], "nn", F32, f"ffn{l}_down", resid=xin, tm_cap=512, tk_cap=act.shape[1],
                       hold_b=True)
        return xout, (h, up, act, cv)

    gb, gc, xs, cva, ya = _mixa_in_fused(h0, w_a_in, a_conv_full, "mixa_in")
    bcx = (gb, gc, xs)
    w_a_out = arrived("a_out", pass_on(["a_out"], ya))
    x1 = _matmul(ya, w_a_out, "nn", F32, "mixa_out", resid=x0, tm_cap=512, tn_cap=d, hold_b=True)
    x2, saved0 = ffn_forward(x1, 0, ["f_up0"], ["f_down0"], ["b_in", "b_out", "f_up1", "f_down1"])
    h2 = _rmsnorm_fwd(x2, b_norm_full, "mixb_norm")
    w_b_in = arrived("b_in", h2)
    pre = _matmul(h2, w_b_in, "nn", BF16, "mixb_in")
    ug = _mixb_fwd(pre, b_vnorm_full, ws, bs_wide, "mixb_mid")
    w_b_out = arrived("b_out", ug)
    x3 = _matmul(ug, w_b_out, "nn", F32, "mixb_out", resid=x2, tm_cap=512, tn_cap=d, hold_b=True)
    x4, saved1 = ffn_forward(x3, 1, [], [], [])
    dx4, dx4b, loss_part, g_final = _final_loss(x4, final_norm.reshape(1, d), target, "loss_head")

    def _rs_start(grad, axis, name):
        return _exchange_start(False, grad, axis, name)

    def ffn_backward(xin, l, saved, dx, dxb):
        h, up, act, cv = saved
        g_down = _matmul(act, dxb, "tn", BF16, f"ffn{l}_down_dw", tm_cap=512, tn_cap=512, tk_cap=s, hold_b=True)
        rs_down, tok = _rs_start(g_down, 0, f"rs_start_f_down{l}")
        dup, cwg, cwa, cbg, cba = _ffn_down_dx_fused(dxb, w_f_down[l], up, cv, f_conv_w_full[l], tok,
                                                     f"ffn{l}_down_dx")
        g_cw, g_cb = jnp.concatenate([cwg, cwa], axis=1), jnp.concatenate([cbg, cba], axis=1)
        g_up = _matmul(h, dup, "tn", BF16, f"ffn{l}_up_dw", tm_cap=512, tk_cap=s, hold_b=True)
        rs_up, tok = _rs_start(g_up, 1, f"rs_start_f_up{l}")
        dh = _matmul(dup, w_f_up[l], "nt", BF16, f"ffn{l}_up_dx", after=tok, tm_cap=512, tn_cap=512,
                     tk_cap=dup.shape[1], hold_b=True)
        dxin, dxinb, g_norm = _rmsnorm_bwd(xin, f_norm[l:l + 1], dh, dx, f"ffn{l}_norm_bwd")
        return dxin, dxinb, (rs_up, rs_down, g_cw, g_cb, g_norm)

    dx3, dx3b, gf1 = ffn_backward(x3, 1, saved1, dx4, dx4b)
    g_b_out = _matmul(ug, dx3b, "tn", BF16, "mixb_out_dw", tk_cap=4096)
    rs_b_out, tok = _rs_start(g_b_out, 0, "rs_start_b_out")
    dug = _matmul(dx3b, w_b_out, "nt", BF16, "mixb_out_dx", after=tok, tm_cap=512, tn_cap=d, hold_b=True)
    dpre, g_ws, g_bs_wide, g_bvnorm = _mixb_bwd(pre, dug, b_vnorm_full, ws, bs_wide, "mixb_mid_bwd")
    g_b_in = _matmul(h2, dpre, "tn", BF16, "mixb_in_dw", tk_cap=4096)
    rs_b_in, tok = _rs_start(g_b_in, 1, "rs_start_b_in")
    dh2 = _matmul(dpre, w_b_in, "nt", BF16, "mixb_in_dx", after=tok, tk_cap=dpre.shape[1], hold_b=True)
    dx2, dx2b, g_bnorm = _rmsnorm_bwd(x2, b_norm_full, dh2, dx3, "mixb_norm_bwd")
    dx1, dx1b, gf0 = ffn_backward(x1, 0, saved0, dx2, dx2b)
    g_a_out = _matmul(ya, dx1b, "tn", BF16, "mixa_out_dw", tk_cap=4096)
    rs_a_out, tok = _rs_start(g_a_out, 0, "rs_start_a_out")
    dbcx, g_aconv = _mixa_out_dx_fused(dx1b, w_a_out, bcx, cva, a_conv_full, tok, "mixa_out_dx")
    g_a_in = _matmul(h0, dbcx, "tn", BF16, "mixa_in_dw", tk_cap=4096)
    rs_a_in, tok = _rs_start(g_a_in, 1, "rs_start_a_in")
    dh0 = _matmul(dbcx, w_a_in, "nt", BF16, "mixa_in_dx", after=tok, tm_cap=512, tk_cap=dbcx.shape[1], hold_b=True)
    grad_x, _, g_anorm = _rmsnorm_bwd(x0, a_norm, dh0, dx1, "mixa_norm_bwd")

    full_shapes = [(1, LANES), (1, d), (3, d), (1, d), (1, d), (SG_GROUPS, CHUNK, CHUNK), (SG_GROUPS, CHUNK),
                   (n_ffn, d), (n_ffn, 3, f2), (n_ffn, f2), (1, d)]
    parts = [loss_part, g_anorm, g_aconv, g_bnorm, g_bvnorm, g_ws, g_bs_wide[:, :, 0],
             jnp.concatenate([gf0[4], gf1[4]], axis=0), jnp.stack([gf0[2], gf1[2]]),
             jnp.concatenate([gf0[3], gf1[3]], axis=0), g_final]
    small_part = _pack(parts)
    small_state, small_tok = _exchange_start(True, small_part, 0, "ar_start_small", after=grad_x)

    big = {}
    for name, states, axis, w, m, v in (
            ("f_down", (gf0[1], gf1[1]), 0, f_down, m_f_down, v_f_down),
            ("f_up", (gf0[0], gf1[0]), 1, f_up, m_f_up, v_f_up),
            ("b_out", (rs_b_out,), 0, b_out, m_b_out, v_b_out), ("b_in", (rs_b_in,), 1, b_in, m_b_in, v_b_in),
            ("a_out", (rs_a_out,), 0, a_out, m_a_out, v_a_out), ("a_in", (rs_a_in,), 1, a_in, m_a_in, v_a_in)):
        recvs = [_exchange_wait(False, st, axis, small_tok, f"rs_wait_{name}{l}") for l, st in enumerate(states)]
        big[name] = _adamw_sharded(recvs, w, m, v, f"adamw_{name}")

    slots = _exchange_wait(True, small_state, 0, [res[0] for res in big.values()], "ar_wait_small")
    total = _sum_slots(slots.reshape((N_DEV,) + small_part.shape), "ar_sum_small")
    (loss_v, r_anorm, r_aconv, r_bnorm, r_bvnorm, r_ws, r_bs, r_fnorm, r_fcw, r_fcb, r_final) = _unpack(total, full_shapes)
    small_grads = [
        r_anorm,
        lax.dynamic_slice_in_dim(r_aconv, me * cshard, cshard, axis=1).reshape(a_conv.shape),
        lax.dynamic_slice_in_dim(r_bnorm, me * cshard, cshard, axis=1),
        lax.dynamic_slice_in_dim(r_bvnorm, me * cshard, cshard, axis=1),
        r_ws.reshape(b_ws.shape), r_bs.reshape(b_bs.shape), r_fnorm,
        lax.dynamic_slice_in_dim(r_fcw, me * fshard, fshard, axis=2),
        r_fcb, r_final.reshape(final_norm.shape)]
    small_w = [a_norm, a_conv, b_norm, b_vnorm, b_ws, b_bs, f_norm, f_conv_w, f_conv_b, final_norm]
    small_m = [m_a_norm, m_a_conv, m_b_norm, m_b_vnorm, m_b_ws, m_b_bs, m_f_norm, m_f_conv_w, m_f_conv_b, m_final_norm]
    small_v = [v_a_norm, v_a_conv, v_b_norm, v_b_vnorm, v_b_ws, v_b_bs, v_f_norm, v_f_conv_w, v_f_conv_b, v_final_norm]
    shapes = [w.shape for w in small_w]
    packed = _adamw_packed(_pack(small_w), _pack(small_grads), _pack(small_m), _pack(small_v), "adamw_small")
    s_delta, s_m, s_v = (_unpack(p, shapes) for p in packed)
    small_names = ["a_norm", "a_conv", "b_norm", "b_vnorm", "b_ws", "b_bs", "f_norm", "f_conv_w", "f_conv_b", "final_norm"]
    small = {nm: (small_grads[i], s_delta[i], s_m[i], s_v[i]) for i, nm in enumerate(small_names)}

    order = ["a_norm", "a_in", "a_conv", "a_out", "b_norm", "b_in", "b_vnorm", "b_ws", "b_bs", "b_out",
             "f_norm", "f_up", "f_conv_w", "f_conv_b", "f_down", "final_norm"]
    res = {nm: (big[nm] if nm in big else small[nm]) for nm in order}
    outs = [loss_v[0, 0], grad_x.reshape(x.shape)]
    for k in range(4):
        outs += [res[nm][k] for nm in order]
    return tuple(outs)


def _unpack_rows(rows, shapes):
    out, off = [], 0
    for shp in shapes:
        n = math.prod(shp)
        out.append(rows[:, off:off + n].reshape((N_DEV,) + tuple(shp)))
        off += n + (-n) % PACK_GRANULE
    return out
```

```python
import math

import jax
import jax.numpy as jnp
from jax import lax
from jax.experimental import pallas as pl
from jax.experimental.pallas import tpu as pltpu

F32 = jnp.float32
BF16 = jnp.bfloat16
MESH = pl.DeviceIdType.MESH

N_DEV = 8
RMS_EPS = 1e-5
CHUNK = 128
SG_GROUPS = 8
ADAM_LR = 0.001
ADAM_B1 = 0.9
ADAM_B2 = 0.999
ADAM_EPS = 1e-08
ADAM_WD = 0.01
ADAM_STEP = 10

LANES = 128
SLAB = 16
FUSE_STRIP = 256
FUSE_HALO = 8
VMEM_LIMIT = 60 * 1024 * 1024
PACK_GRANULE = 8 * LANES


def _pick(dim, cap, mult):
    best = None
    t = mult
    while t <= min(dim, cap):
        if dim % t == 0:
            best = t
        t += mult
    return dim if best is None else best


def _params(semantics=None):
    return pltpu.CompilerParams(dimension_semantics=semantics, vmem_limit_bytes=VMEM_LIMIT)


_DIMS = {
    "nn": (((1,), (0,)), ((), ())),
    "nt": (((1,), (1,)), ((), ())),
    "tn": (((0,), (0,)), ((), ())),
}


def _matmul(a, b, mode, out_dtype, name, resid=None, after=None, tm_cap=1024, tn_cap=1024, tk_cap=2816,
            hold_b=False):
    if mode == "nn":
        (m, k), n = a.shape, b.shape[1]
    elif mode == "nt":
        (m, k), n = a.shape, b.shape[0]
    else:
        (k, m), n = a.shape, b.shape[1]
    tm, tn, tk = _pick(m, tm_cap, LANES), _pick(n, tn_cap, LANES), _pick(k, tk_cap, LANES)
    nk = k // tk
    n_in = 2 + (resid is not None) + (after is not None)

    def body(*refs):
        a_ref, b_ref = refs[:2]
        r_ref = refs[2] if resid is not None else None
        o_ref = refs[n_in]
        prod = lax.dot_general(a_ref[...], b_ref[...], _DIMS[mode], preferred_element_type=F32)

        def finish(r):
            if r_ref is not None:
                r = r + r_ref[...]
            o_ref[...] = r.astype(out_dtype)

        if nk == 1:
            finish(prod)
            return
        acc_ref = refs[n_in + 1]
        kk = pl.program_id(2)

        @pl.when(kk == 0)
        def _():
            acc_ref[...] = prod

        @pl.when(jnp.logical_and(kk > 0, kk < nk - 1))
        def _():
            acc_ref[...] += prod

        @pl.when(kk == nk - 1)
        def _():
            finish(acc_ref[...] + prod)

    def spec(block, index):
        if hold_b:
            return pl.BlockSpec(block, lambda j, i, kk: index(i, j, kk))
        return pl.BlockSpec(block, index)

    a_spec = (spec((tk, tm), lambda i, j, kk: (kk, i)) if mode == "tn"
              else spec((tm, tk), lambda i, j, kk: (i, kk)))
    b_spec = (spec((tn, tk), lambda i, j, kk: (j, kk)) if mode == "nt"
              else spec((tk, tn), lambda i, j, kk: (kk, j)))
    o_spec = spec((tm, tn), lambda i, j, kk: (i, j))
    in_specs = [a_spec, b_spec] + ([o_spec] if resid is not None else [])
    args = (a, b) + ((resid,) if resid is not None else ())
    if after is not None:
        in_specs.append(pl.BlockSpec(memory_space=pl.ANY))
        args += (after,)
    return pl.pallas_call(
        body, name=name, grid=(n // tn, m // tm, nk) if hold_b else (m // tm, n // tn, nk),
        in_specs=in_specs, out_specs=o_spec,
        out_shape=jax.ShapeDtypeStruct((m, n), out_dtype),
        scratch_shapes=[pltpu.VMEM((tm, tn), F32)] if nk > 1 else [],
        compiler_params=_params(("parallel", "parallel", "arbitrary")),
    )(*args)


def _rms_stats(xf):
    inv = lax.rsqrt(jnp.mean(xf * xf, axis=-1, keepdims=True) + RMS_EPS)
    return inv, xf * inv


def _rmsnorm_fwd(x, g, name, after=None):
    s, d = x.shape
    tm = _pick(s, 512, SLAB)
    extra = () if after is None else (after,)

    def body(x_ref, g_ref, *rest):
        _, xhat = _rms_stats(x_ref[...])
        rest[-1][...] = (xhat * g_ref[...]).astype(BF16)

    return pl.pallas_call(
        body, name=name, grid=(s // tm,),
        in_specs=[pl.BlockSpec((tm, d), lambda i: (i, 0)), pl.BlockSpec((1, d), lambda i: (0, 0))]
        + [pl.BlockSpec(memory_space=pl.ANY)] * len(extra),
        out_specs=pl.BlockSpec((tm, d), lambda i: (i, 0)),
        out_shape=jax.ShapeDtypeStruct((s, d), BF16),
        compiler_params=_params(("parallel",)),
    )(x, g, *extra)


def _rmsnorm_bwd(x, g, dh, dx_out, name):
    s, d = x.shape
    tm = _pick(s, 256, SLAB)

    def body(x_ref, g_ref, dh_ref, dxo_ref, dxi_ref, dxib_ref, dg_ref):
        inv, xhat = _rms_stats(x_ref[...])
        dhv = dh_ref[...].astype(F32)
        dxhat = dhv * g_ref[...]
        proj = jnp.mean(dxhat * xhat, axis=-1, keepdims=True)
        dx = dxo_ref[...] + inv * (dxhat - xhat * proj)
        dxi_ref[...] = dx
        dxib_ref[...] = dx.astype(BF16)
        part = jnp.sum(dhv * xhat, axis=0, keepdims=True)

        @pl.when(pl.program_id(0) == 0)
        def _():
            dg_ref[...] = part

        @pl.when(pl.program_id(0) > 0)
        def _():
            dg_ref[...] += part

    row = pl.BlockSpec((tm, d), lambda i: (i, 0))
    vec = pl.BlockSpec((1, d), lambda i: (0, 0))
    return pl.pallas_call(
        body, name=name, grid=(s // tm,),
        in_specs=[row, vec, row, row], out_specs=[row, row, vec],
        out_shape=[jax.ShapeDtypeStruct((s, d), F32), jax.ShapeDtypeStruct((s, d), BF16),
                   jax.ShapeDtypeStruct((1, d), F32)],
        compiler_params=_params(("arbitrary",)),
    )(x, g, dh, dx_out)


def _final_loss(x, g, target, name):
    s, d = x.shape
    tm = _pick(s, 256, SLAB)

    def body(x_ref, g_ref, t_ref, dx_ref, dxb_ref, loss_ref, dg_ref):
        inv, xhat = _rms_stats(x_ref[...])
        gv = g_ref[...]
        err = xhat * gv - t_ref[...]
        loss = 0.5 * jnp.sum(jnp.mean(err * err, axis=-1, keepdims=True), axis=0, keepdims=True)
        dy = err * (1.0 / d)
        dxhat = dy * gv
        proj = jnp.mean(dxhat * xhat, axis=-1, keepdims=True)
        dx = inv * (dxhat - xhat * proj)
        dx_ref[...] = dx
        dxb_ref[...] = dx.astype(BF16)
        part = jnp.sum(dy * xhat, axis=0, keepdims=True)
        loss_row = jnp.broadcast_to(loss, (1, LANES))

        @pl.when(pl.program_id(0) == 0)
        def _():
            dg_ref[...] = part
            loss_ref[...] = loss_row

        @pl.when(pl.program_id(0) > 0)
        def _():
            dg_ref[...] += part
            loss_ref[...] += loss_row

    row = pl.BlockSpec((tm, d), lambda i: (i, 0))
    vec = pl.BlockSpec((1, d), lambda i: (0, 0))
    return pl.pallas_call(
        body, name=name, grid=(s // tm,),
        in_specs=[row, vec, row],
        out_specs=[row, row, pl.BlockSpec((1, LANES), lambda i: (0, 0)), vec],
        out_shape=[jax.ShapeDtypeStruct((s, d), F32), jax.ShapeDtypeStruct((s, d), BF16),
                   jax.ShapeDtypeStruct((1, LANES), F32), jax.ShapeDtypeStruct((1, d), F32)],
        compiler_params=_params(("arbitrary",)),
    )(x, g, target)


def _mixa_in_fused(h, w_in, wc, name):
    s, d = h.shape
    tm = _pick(s, 1024, LANES)
    tn = _pick(d, 512, FUSE_STRIP)
    nj = d // tn

    def body(h_ref, wb_ref, wg_ref, wx_ref, wc_ref, gb_ref, gc_ref, xs_ref, cv_ref, y_ref, carry):
        @pl.when(pl.program_id(1) == 0)
        def _():
            carry[...] = jnp.zeros_like(carry)

        def matmul(st):
            cols = slice(st * FUSE_STRIP, (st + 1) * FUSE_STRIP)
            return tuple(jnp.dot(h_ref[...], w_ref[:, cols], preferred_element_type=F32).astype(BF16)
                         for w_ref in (wb_ref, wg_ref, wx_ref))

        n_strips = tn // FUSE_STRIP
        parts = matmul(0)
        for st in range(n_strips):
            parts_next = matmul(st + 1) if st + 1 < n_strips else None
            cols = slice(st * FUSE_STRIP, (st + 1) * FUSE_STRIP)
            for ref, part in zip((gb_ref, gc_ref, xs_ref), parts):
                ref[:, cols] = part
            p = parts[1].astype(F32) * parts[2].astype(F32)
            ext = jnp.concatenate([carry[:, cols], p], axis=0)
            s1 = pltpu.roll(ext, 1, 0)[FUSE_HALO:, :]
            s2 = pltpu.roll(ext, 2, 0)[FUSE_HALO:, :]
            carry[:, cols] = p[tm - FUSE_HALO:, :]
            cv = wc_ref[0:1, cols] * s2 + wc_ref[1:2, cols] * s1 + wc_ref[2:3, cols] * p
            cv_ref[:, cols] = cv.astype(BF16)
            y_ref[:, cols] = (parts[0].astype(F32) * cv).astype(BF16)
            parts = parts_next

    def cols_of(rows, offset):
        return pl.BlockSpec((rows, tn), lambda j, i: (0, j + offset))

    tile = pl.BlockSpec((tm, tn), lambda j, i: (i, j))
    return pl.pallas_call(
        body, name=name, grid=(nj, s // tm),
        in_specs=[pl.BlockSpec((tm, d), lambda j, i: (i, 0)), cols_of(d, 0), cols_of(d, nj), cols_of(d, 2 * nj),
                  cols_of(3, 0)],
        out_specs=[tile] * 5, out_shape=[jax.ShapeDtypeStruct((s, d), BF16)] * 5,
        scratch_shapes=[pltpu.VMEM((FUSE_HALO, tn), F32)],
        compiler_params=_params(("parallel", "arbitrary")),
    )(h, w_in, w_in, w_in, wc)


def _mixa_out_dx_fused(dxb, w_out, bcx, cv, wc, after, name):
    s, d = dxb.shape
    tm = _pick(s, 1024, LANES)
    tn = _pick(d, 512, FUSE_STRIP)
    nj, ni = d // tn, s // tm
    n_steps = nj * ni
    sub = tm // FUSE_HALO

    def body(dx_ref, w_ref, gb_ref, gc_ref, xs_ref, cv_ref, wc_ref, after_ref,
             dbcx_hbm, dwc_ref, out_buf, out_sem, carry, acc):
        j, i = pl.program_id(0), pl.program_id(1)
        step = j * ni + i
        slot = lax.rem(step, 2)
        row0 = pl.multiple_of((ni - 1 - i) * tm, tm)

        def out_copy(part, slot_=None):
            slot_ = slot if slot_ is None else slot_
            col0 = pl.multiple_of(part * d + j * tn, LANES)
            return pltpu.make_async_copy(out_buf.at[slot_, part], dbcx_hbm.at[pl.ds(row0, tm), pl.ds(col0, tn)],
                                         out_sem.at[slot_, part])

        @pl.when(step >= 2)
        def _():
            for part in range(3):
                out_copy(part).wait()

        @pl.when(i == 0)
        def _():
            carry[...] = jnp.zeros_like(carry)
            acc[...] = jnp.zeros_like(acc)

        for st in range(tn // FUSE_STRIP):
            cols = slice(st * FUSE_STRIP, (st + 1) * FUSE_STRIP)
            dyv = lax.dot_general(dx_ref[...], w_ref[cols, :], _DIMS["nt"], preferred_element_type=F32)
            gc = gc_ref[:, cols].astype(F32)
            xs = xs_ref[:, cols].astype(F32)
            d0 = dyv * gb_ref[:, cols].astype(F32)
            ext = jnp.concatenate([d0, carry[:, cols]], axis=0)
            d1 = pltpu.roll(ext, tm + FUSE_HALO - 1, 0)[:tm, :]
            d2 = pltpu.roll(ext, tm + FUSE_HALO - 2, 0)[:tm, :]
            carry[:, cols] = d0[:FUSE_HALO, :]
            dp = wc_ref[2:3, cols] * d0 + wc_ref[1:2, cols] * d1 + wc_ref[0:1, cols] * d2
            out_buf[slot, 0, :, cols] = (dyv * cv_ref[:, cols].astype(F32)).astype(BF16)
            out_buf[slot, 1, :, cols] = (dp * xs).astype(BF16)
            out_buf[slot, 2, :, cols] = (dp * gc).astype(BF16)
            p = gc * xs
            for k, term in enumerate((d2 * p, d1 * p, d0 * p)):
                acc[k, :, cols] += jnp.sum(term.reshape(sub, FUSE_HALO, FUSE_STRIP), axis=0)

        for part in range(3):
            out_copy(part).start()

        @pl.when(i == ni - 1)
        def _():
            for k in range(3):
                dwc_ref[k:k + 1, :] = jnp.sum(acc[k], axis=0, keepdims=True)

        @pl.when(step == n_steps - 1)
        def _():
            for part in range(3):
                out_copy(part).wait()
                if n_steps > 1:
                    out_copy(part, 1 - slot).wait()

    tile = pl.BlockSpec((tm, tn), lambda j, i: (ni - 1 - i, j))
    return pl.pallas_call(
        body, name=name, grid=(nj, ni),
        in_specs=[pl.BlockSpec((tm, d), lambda j, i: (ni - 1 - i, 0)), pl.BlockSpec((tn, d), lambda j, i: (j, 0)),
                  tile, tile, tile, tile, pl.BlockSpec((3, tn), lambda j, i: (0, j)),
                  pl.BlockSpec(memory_space=pl.ANY)],
        out_specs=[pl.BlockSpec(memory_space=pl.ANY), pl.BlockSpec((3, tn), lambda j, i: (0, j))],
        out_shape=[jax.ShapeDtypeStruct((s, 3 * d), BF16), jax.ShapeDtypeStruct((3, d), F32)],
        scratch_shapes=[pltpu.VMEM((2, 3, tm, tn), BF16), pltpu.SemaphoreType.DMA((2, 3)),
                        pltpu.VMEM((FUSE_HALO, tn), F32), pltpu.VMEM((3, FUSE_HALO, tn), F32)],
        compiler_params=_params(("arbitrary", "arbitrary")),
    )(dxb, w_out, bcx[0], bcx[1], bcx[2], cv, wc, after)


def _sigmoid(z):
    return 0.5 * jnp.tanh(0.5 * z) + 0.5


def _ffn_up_fused(h, w_up, cw, cb, name):
    s, d = h.shape
    f = w_up.shape[1] // 2
    tm = _pick(s, 1024, LANES)
    tn = _pick(f, 512, FUSE_STRIP)
    nj = f // tn

    def body(h_ref, wg_ref, wa_ref, cwg_ref, cwa_ref, cbg_ref, cba_ref,
             upg_ref, upa_ref, cvg_ref, cva_ref, act_ref, carry_g, carry_a):
        @pl.when(pl.program_id(1) == 0)
        def _():
            carry_g[...] = jnp.zeros_like(carry_g)
            carry_a[...] = jnp.zeros_like(carry_a)

        def matmul(st):
            cols = slice(st * FUSE_STRIP, (st + 1) * FUSE_STRIP)
            return tuple(jnp.dot(h_ref[...], w_ref[:, cols], preferred_element_type=F32).astype(BF16)
                         for w_ref in (wg_ref, wa_ref))

        def conv(up, cw_ref, cb_ref, carry, up_ref, cv_ref, cols):
            up_ref[:, cols] = up
            x = up.astype(F32)
            ext = jnp.concatenate([carry[:, cols], x], axis=0)
            s1 = pltpu.roll(ext, 1, 0)[FUSE_HALO:, :]
            s2 = pltpu.roll(ext, 2, 0)[FUSE_HALO:, :]
            carry[:, cols] = x[tm - FUSE_HALO:, :]
            cv = cw_ref[0:1, cols] * s2 + cw_ref[1:2, cols] * s1 + cw_ref[2:3, cols] * x + cb_ref[:, cols]
            cv_ref[:, cols] = cv.astype(BF16)
            return cv

        n_strips = tn // FUSE_STRIP
        ups = matmul(0)
        for st in range(n_strips):
            ups_next = matmul(st + 1) if st + 1 < n_strips else None
            cols = slice(st * FUSE_STRIP, (st + 1) * FUSE_STRIP)
            gcv = conv(ups[0], cwg_ref, cbg_ref, carry_g, upg_ref, cvg_ref, cols)
            acv = conv(ups[1], cwa_ref, cba_ref, carry_a, upa_ref, cva_ref, cols)
            act_ref[:, cols] = (gcv * _sigmoid(gcv) * acv).astype(BF16)
            ups = ups_next

    def cols_of(rows, offset):
        return pl.BlockSpec((rows, tn), lambda j, i: (0, j + offset))

    tile = pl.BlockSpec((tm, tn), lambda j, i: (i, j))
    out = jax.ShapeDtypeStruct((s, f), BF16)
    return pl.pallas_call(
        body, name=name, grid=(nj, s // tm),
        in_specs=[pl.BlockSpec((tm, d), lambda j, i: (i, 0)), cols_of(d, 0), cols_of(d, nj),
                  cols_of(3, 0), cols_of(3, nj), cols_of(1, 0), cols_of(1, nj)],
        out_specs=[tile] * 5, out_shape=[out] * 5,
        scratch_shapes=[pltpu.VMEM((FUSE_HALO, tn), F32)] * 2,
        compiler_params=_params(("parallel", "arbitrary")),
    )(h, w_up, w_up, cw, cw, cb, cb)


def _ffn_down_dx_fused(dxb, w_down, up, cv, cw, after, name):
    s, d = dxb.shape
    f = w_down.shape[0]
    tm = _pick(s, 1024, LANES)
    tn = _pick(f, 512, FUSE_STRIP)
    nj, ni = f // tn, s // tm
    n_steps = nj * ni
    sub = tm // FUSE_HALO

    def body(dx_ref, w_ref, upg_ref, upa_ref, cvg_ref, cva_ref, cwg_ref, cwa_ref, after_ref,
             dup_hbm, dcwg_ref, dcwa_ref, dcbg_ref, dcba_ref, out_buf, out_sem, carry, acc):
        j, i = pl.program_id(0), pl.program_id(1)
        step = j * ni + i
        slot = lax.rem(step, 2)
        row0 = pl.multiple_of((ni - 1 - i) * tm, tm)

        def out_copy(half):
            col0 = pl.multiple_of(half * f + j * tn, LANES)
            return pltpu.make_async_copy(out_buf.at[slot, half], dup_hbm.at[pl.ds(row0, tm), pl.ds(col0, tn)],
                                         out_sem.at[slot, half])

        @pl.when(step >= 2)
        def _():
            for half in range(2):
                out_copy(half).wait()

        @pl.when(i == 0)
        def _():
            carry[...] = jnp.zeros_like(carry)
            acc[...] = jnp.zeros_like(acc)

        for st in range(tn // FUSE_STRIP):
            cols = slice(st * FUSE_STRIP, (st + 1) * FUSE_STRIP)
            dact = lax.dot_general(dx_ref[...], w_ref[cols, :], _DIMS["nt"], preferred_element_type=F32)
            gcv = cvg_ref[:, cols].astype(F32)
            acv = cva_ref[:, cols].astype(F32)
            sg = _sigmoid(gcv)
            dd = (dact * acv * (sg * (1.0 + gcv * (1.0 - sg))), dact * (gcv * sg))
            for half, (up_ref, cw_ref) in enumerate(((upg_ref, cwg_ref), (upa_ref, cwa_ref))):
                x = up_ref[:, cols].astype(F32)
                d0 = dd[half]
                ext = jnp.concatenate([d0, carry[half, :, cols]], axis=0)
                d1 = pltpu.roll(ext, tm + FUSE_HALO - 1, 0)[:tm, :]
                d2 = pltpu.roll(ext, tm + FUSE_HALO - 2, 0)[:tm, :]
                carry[half, :, cols] = d0[:FUSE_HALO, :]
                out_buf[slot, half, :, cols] = (cw_ref[2:3, cols] * d0 + cw_ref[1:2, cols] * d1
                                                + cw_ref[0:1, cols] * d2).astype(BF16)
                for k, term in enumerate((d2 * x, d1 * x, d0 * x, d0)):
                    acc[half, k, :, cols] += jnp.sum(term.reshape(sub, FUSE_HALO, FUSE_STRIP), axis=0)

        for half in range(2):
            out_copy(half).start()

        @pl.when(i == ni - 1)
        def _():
            for half, (dcw_ref, dcb_ref) in enumerate(((dcwg_ref, dcbg_ref), (dcwa_ref, dcba_ref))):
                for k in range(3):
                    dcw_ref[k:k + 1, :] = jnp.sum(acc[half, k], axis=0, keepdims=True)
                dcb_ref[...] = jnp.sum(acc[half, 3], axis=0, keepdims=True)

        @pl.when(step == n_steps - 1)
        def _():
            for half in range(2):
                out_copy(half).wait()
                if n_steps > 1:
                    pltpu.make_async_copy(out_buf.at[1 - slot, half], dup_hbm.at[pl.ds(row0, tm), pl.ds(0, tn)],
                                          out_sem.at[1 - slot, half]).wait()

    tile = pl.BlockSpec((tm, tn), lambda j, i: (ni - 1 - i, j))

    def cols_of(rows, offset):
        return pl.BlockSpec((rows, tn), lambda j, i: (0, j + offset))

    small = pl.BlockSpec((3, tn), lambda j, i: (0, j)), pl.BlockSpec((1, tn), lambda j, i: (0, j))
    return pl.pallas_call(
        body, name=name, grid=(nj, ni),
        in_specs=[pl.BlockSpec((tm, d), lambda j, i: (ni - 1 - i, 0)), pl.BlockSpec((tn, d), lambda j, i: (j, 0)),
                  tile, tile, tile, tile, cols_of(3, 0), cols_of(3, nj), pl.BlockSpec(memory_space=pl.ANY)],
        out_specs=[pl.BlockSpec(memory_space=pl.ANY), small[0], small[0], small[1], small[1]],
        out_shape=[jax.ShapeDtypeStruct((s, 2 * f), BF16), jax.ShapeDtypeStruct((3, f), F32),
                   jax.ShapeDtypeStruct((3, f), F32), jax.ShapeDtypeStruct((1, f), F32),
                   jax.ShapeDtypeStruct((1, f), F32)],
        scratch_shapes=[pltpu.VMEM((2, 2, tm, tn), BF16), pltpu.SemaphoreType.DMA((2, 2)),
                        pltpu.VMEM((2, FUSE_HALO, tn), F32), pltpu.VMEM((2, 4, FUSE_HALO, tn), F32)],
        compiler_params=_params(("arbitrary", "arbitrary")),
    )(dxb, w_down, up[0], up[1], cv[0], cv[1], cw, cw, after)


_GELU_C = math.sqrt(2.0 / math.pi)


def _gelu(x):
    th = jnp.tanh(_GELU_C * (x + 0.044715 * (x * x * x)))
    return x * (0.5 * (1.0 + th)), th


def _gelu_grad(x, th):
    return 0.5 * (1.0 + th) + 0.5 * x * (1.0 - th * th) * (_GELU_C * (1.0 + 3.0 * 0.044715 * (x * x)))


def _masked_ws(ws_ref, h):
    t = lax.broadcasted_iota(jnp.int32, (CHUNK, CHUNK), 0)
    sx = lax.broadcasted_iota(jnp.int32, (CHUNK, CHUNK), 1)
    return jnp.where(sx <= t, ws_ref[h], 0.0)


def _mixb_fwd(pre, gv, ws, bs_wide, name):
    s, w2 = pre.shape
    w = w2 // 2
    gw = w // SG_GROUPS

    def body(pre_ref, gv_ref, ws_ref, bs_ref, o_ref):
        zu, _ = _gelu(pre_ref[:, :w].astype(F32))
        zv, _ = _gelu(pre_ref[:, w:].astype(F32))
        _, vhat = _rms_stats(zv)
        vn = (vhat * gv_ref[...]).astype(BF16)
        for h in range(SG_GROUPS):
            cols = slice(h * gw, (h + 1) * gw)
            wsm = _masked_ws(ws_ref, h).astype(BF16)
            gate = jnp.dot(wsm, vn[:, cols], preferred_element_type=F32)
            gate = gate + jnp.tile(bs_ref[h], (1, gw // LANES))
            o_ref[:, cols] = (zu[:, cols] * gate).astype(BF16)

    return pl.pallas_call(
        body, name=name, grid=(s // CHUNK,),
        in_specs=[pl.BlockSpec((CHUNK, w2), lambda i: (i, 0)), pl.BlockSpec((1, w), lambda i: (0, 0)),
                  pl.BlockSpec((SG_GROUPS, CHUNK, CHUNK), lambda i: (0, 0, 0)),
                  pl.BlockSpec((SG_GROUPS, CHUNK, LANES), lambda i: (0, 0, 0))],
        out_specs=pl.BlockSpec((CHUNK, w), lambda i: (i, 0)),
        out_shape=jax.ShapeDtypeStruct((s, w), BF16),
        compiler_params=_params(("parallel",)),
    )(pre, gv, ws, bs_wide)


def _mixb_bwd(pre, dug, gv, ws, bs_wide, name):
    s, w2 = pre.shape
    w = w2 // 2
    gw = w // SG_GROUPS

    def body(pre_ref, dug_ref, gv_ref, ws_ref, bs_ref, o_ref, dws_ref, dbs_ref, dgv_ref, dvn_ref):
        first = pl.program_id(0) == 0

        @pl.when(first)
        def _():
            dws_ref[...] = jnp.zeros_like(dws_ref)
            dbs_ref[...] = jnp.zeros_like(dbs_ref)

        pu = pre_ref[:, :w].astype(F32)
        pv = pre_ref[:, w:].astype(F32)
        zu, thu = _gelu(pu)
        zv, thv = _gelu(pv)
        inv, vhat = _rms_stats(zv)
        gvv = gv_ref[...]
        vn = (vhat * gvv).astype(BF16)
        for h in range(SG_GROUPS):
            cols = slice(h * gw, (h + 1) * gw)
            wsm = _masked_ws(ws_ref, h).astype(BF16)
            gate = jnp.dot(wsm, vn[:, cols], preferred_element_type=F32)
            gate = gate + jnp.tile(bs_ref[h], (1, gw // LANES))
            dug_h = dug_ref[:, cols].astype(F32)
            dgate = dug_h * zu[:, cols]
            dgate_b = dgate.astype(BF16)
            o_ref[:, cols] = (dug_h * gate * _gelu_grad(pu[:, cols], thu[:, cols])).astype(BF16)
            dbs_ref[h] += jnp.broadcast_to(jnp.sum(dgate, axis=-1, keepdims=True), (CHUNK, LANES))
            dws = lax.dot_general(dgate_b, vn[:, cols], _DIMS["nt"], preferred_element_type=F32)
            t = lax.broadcasted_iota(jnp.int32, (CHUNK, CHUNK), 0)
            sx = lax.broadcasted_iota(jnp.int32, (CHUNK, CHUNK), 1)
            dws_ref[h] += jnp.where(sx <= t, dws, 0.0)
            dvn_ref[:, cols] = lax.dot_general(wsm, dgate_b, _DIMS["tn"], preferred_element_type=F32)
        dvn = dvn_ref[...]
        part = jnp.sum(dvn * vhat, axis=0, keepdims=True)

        @pl.when(first)
        def _():
            dgv_ref[...] = part

        @pl.when(jnp.logical_not(first))
        def _():
            dgv_ref[...] += part

        dvhat = dvn * gvv
        dzv = inv * (dvhat - vhat * jnp.mean(dvhat * vhat, axis=-1, keepdims=True))
        o_ref[:, w:] = (dzv * _gelu_grad(pv, thv)).astype(BF16)

    return pl.pallas_call(
        body, name=name, grid=(s // CHUNK,),
        in_specs=[pl.BlockSpec((CHUNK, w2), lambda i: (i, 0)), pl.BlockSpec((CHUNK, w), lambda i: (i, 0)),
                  pl.BlockSpec((1, w), lambda i: (0, 0)),
                  pl.BlockSpec((SG_GROUPS, CHUNK, CHUNK), lambda i: (0, 0, 0)),
                  pl.BlockSpec((SG_GROUPS, CHUNK, LANES), lambda i: (0, 0, 0))],
        out_specs=[pl.BlockSpec((CHUNK, w2), lambda i: (i, 0)),
                   pl.BlockSpec((SG_GROUPS, CHUNK, CHUNK), lambda i: (0, 0, 0)),
                   pl.BlockSpec((SG_GROUPS, CHUNK, LANES), lambda i: (0, 0, 0)),
                   pl.BlockSpec((1, w), lambda i: (0, 0))],
        out_shape=[jax.ShapeDtypeStruct((s, w2), BF16), jax.ShapeDtypeStruct((SG_GROUPS, CHUNK, CHUNK), F32),
                   jax.ShapeDtypeStruct((SG_GROUPS, CHUNK, LANES), F32), jax.ShapeDtypeStruct((1, w), F32)],
        scratch_shapes=[pltpu.VMEM((CHUNK, w), F32)],
        compiler_params=_params(("arbitrary",)),
    )(pre, dug, gv, ws, bs_wide)


def _cast_layer(w3, layer, name):
    _, r, c = w3.shape
    tr = _pick(r, 256, SLAB)

    def body(w_ref, o_ref):
        o_ref[...] = w_ref[...].astype(BF16)

    return pl.pallas_call(
        body, name=name, grid=(r // tr,),
        in_specs=[pl.BlockSpec((None, tr, c), lambda i: (layer, i, 0))],
        out_specs=pl.BlockSpec((tr, c), lambda i: (i, 0)),
        out_shape=jax.ShapeDtypeStruct((r, c), BF16),
        compiler_params=_params(("parallel",)),
    )(w3)


def _adamw_math(w, g, m, v):
    m = ADAM_B1 * m + (1.0 - ADAM_B1) * g
    v = ADAM_B2 * v + (1.0 - ADAM_B2) * (g * g)
    m_hat = m / (1.0 - ADAM_B1 ** ADAM_STEP)
    v_hat = v / (1.0 - ADAM_B2 ** ADAM_STEP)
    delta = -ADAM_LR * (m_hat / (jnp.sqrt(v_hat) + ADAM_EPS) + ADAM_WD * w)
    return delta, m, v


def _adamw_sharded(recvs, w, m, v, name):
    nl, r, c = w.shape
    tc = _pick(c, 1536, LANES)
    tr = _pick(r, 64, SLAB)

    def body(*refs):
        recv_refs = refs[:nl]
        w_ref, m_ref, v_ref, g_ref, d_ref, nm_ref, nv_ref = refs[nl:]
        for layer, recv_ref in enumerate(recv_refs):
            @pl.when(pl.program_id(0) == layer)
            def _():
                g = recv_ref[0].astype(F32)
                for q in range(1, N_DEV):
                    g = g + recv_ref[q].astype(F32)
                delta, nm, nv = _adamw_math(w_ref[...], g, m_ref[...], v_ref[...])
                g_ref[...] = g
                d_ref[...] = delta
                nm_ref[...] = nm
                nv_ref[...] = nv

    def recv_spec(layer):
        return pl.BlockSpec((N_DEV, tr, tc),
                            lambda l, i, j: (0, jnp.where(l == layer, i, 0), jnp.where(l == layer, j, 0)))

    blk = pl.BlockSpec((None, tr, tc), lambda l, i, j: (l, i, j))
    out = jax.ShapeDtypeStruct((nl, r, c), F32)
    return pl.pallas_call(
        body, name=name, grid=(nl, r // tr, c // tc),
        in_specs=[recv_spec(layer) for layer in range(nl)] + [blk, blk, blk],
        out_specs=[blk] * 4, out_shape=[out] * 4,
        compiler_params=_params(("parallel",) * 3),
    )(*recvs, w, m, v)


def _adamw_packed(w, g, m, v, name):
    r, c = w.shape
    tr = _pick(r, 256, 8)

    def body(w_ref, g_ref, m_ref, v_ref, d_ref, nm_ref, nv_ref):
        delta, nm, nv = _adamw_math(w_ref[...], g_ref[...], m_ref[...], v_ref[...])
        d_ref[...] = delta
        nm_ref[...] = nm
        nv_ref[...] = nv

    blk = pl.BlockSpec((tr, c), lambda i: (i, 0))
    out = jax.ShapeDtypeStruct((r, c), F32)
    return pl.pallas_call(
        body, name=name, grid=(r // tr,), in_specs=[blk] * 4, out_specs=[blk] * 3, out_shape=[out] * 3,
        compiler_params=_params(("parallel",)),
    )(w, g, m, v)


def _pack(arrays):
    parts = []
    for a in arrays:
        flat = a.reshape(-1).astype(F32)
        pad = (-flat.shape[0]) % PACK_GRANULE
        parts.append(jnp.pad(flat, (0, pad)) if pad else flat)
    return jnp.concatenate(parts).reshape(-1, LANES)


def _unpack(buf, shapes):
    flat = buf.reshape(-1)
    out, off = [], 0
    for shp in shapes:
        n = math.prod(shp)
        out.append(flat[off:off + n].reshape(shp))
        off += n + (-n) % PACK_GRANULE
    return out


def _mesh_pos():
    return lax.axis_index("x"), lax.axis_index("y"), lax.axis_index("c")


def _coords(q):
    return q // 4, (q // 2) % 2, q % 2


def _shard_of(ref, q, shard_shape, axis):
    r, c = shard_shape
    if axis == 0:
        return ref.at[pl.ds(pl.multiple_of(q * r, SLAB), r), :]
    return ref.at[:, pl.ds(pl.multiple_of(q * c, LANES), c)]


_HBM = pl.BlockSpec(memory_space=pltpu.HBM)
_SEM = pl.BlockSpec(memory_space=pltpu.SEMAPHORE)
_EFFECT = pltpu.SideEffectType.DATAFLOW_SIDE_EFFECTING


def _exchange_shapes(gather, src_shape, axis):
    r, c = src_shape
    if gather:
        return (r, c), ((r * N_DEV, c) if axis == 0 else (r, c * N_DEV))
    shard = (r // N_DEV, c) if axis == 0 else (r, c // N_DEV)
    return shard, (N_DEV,) + shard


def _exchange_copies(gather, src, land, sems, axis):
    send_sems, recv_sems, own_sem = sems
    x, y, c_ = _mesh_pos()
    me = 4 * x + 2 * y + c_
    shard, _ = _exchange_shapes(gather, src.shape, axis)

    def piece(q):
        return src if gather else _shard_of(src, q, shard, axis)

    def place(q):
        return _shard_of(land, q, shard, axis) if gather else land.at[q]

    own = pltpu.make_async_copy(piece(me), place(me), own_sem.at[0])
    sends, arrivals = [], []
    for step in range(1, N_DEV):
        to = (me + step) % N_DEV
        frm = (me + N_DEV - step) % N_DEV
        sends.append(pltpu.make_async_remote_copy(
            src_ref=piece(to), dst_ref=place(me), send_sem=send_sems.at[step - 1], recv_sem=recv_sems.at[step - 1],
            device_id=_coords(to), device_id_type=MESH))
        arrivals.append(pltpu.make_async_remote_copy(
            src_ref=piece(me), dst_ref=place(frm), send_sem=send_sems.at[step - 1], recv_sem=recv_sems.at[step - 1],
            device_id=_coords(frm), device_id_type=MESH))
    return own, sends, arrivals


def _exchange_start(gather, src, axis, name, after=None):
    _, land_shape = _exchange_shapes(gather, src.shape, axis)
    extra = () if after is None else (after,)

    def body(*refs):
        src_ref, land = refs[:2]
        send_sems, recv_sems, own_sem = refs[2 + len(extra):5 + len(extra)]
        own, sends, _ = _exchange_copies(gather, src_ref, land, (send_sems, recv_sems, own_sem), axis)
        own.start()
        for cp in sends:
            cp.start()
        refs[-1][...] = jnp.zeros_like(refs[-1])

    out = pl.pallas_call(
        body, name=name,
        out_shape=(pltpu.SemaphoreType.DMA((N_DEV - 1,)), pltpu.SemaphoreType.DMA((N_DEV - 1,)),
                   pltpu.SemaphoreType.DMA((1,)), pltpu.HBM(src.shape, src.dtype),
                   pltpu.HBM(land_shape, src.dtype), jax.ShapeDtypeStruct((8, LANES), F32)),
        in_specs=[_HBM, _HBM] + [pl.BlockSpec(memory_space=pl.ANY)] * len(extra),
        out_specs=(_SEM, _SEM, _SEM, _HBM, _HBM, pl.BlockSpec(memory_space=pltpu.VMEM)),
        input_output_aliases={0: 3, 1: 4},
        compiler_params=pltpu.CompilerParams(has_side_effects=_EFFECT),
    )(pltpu.with_memory_space_constraint(src, pltpu.HBM),
      pltpu.with_memory_space_constraint(lax.empty(land_shape, src.dtype), pltpu.HBM), *extra)
    return out[:5], out[5]


def _exchange_wait(gather, state, axis, after, name):
    send_sems, recv_sems, own_sem, src_thru, land_thru = state
    after = tuple(after) if isinstance(after, (tuple, list)) else (after,)

    def body(src, land, send_sems, recv_sems, own_sem, *rest):
        own, sends, arrivals = _exchange_copies(gather, src, land, (send_sems, recv_sems, own_sem), axis)
        for cp in sends:
            cp.wait_send()
        for cp in arrivals:
            cp.wait_recv()
        own.wait()

    return pl.pallas_call(
        body, name=name,
        out_shape=(pltpu.HBM(src_thru.shape, src_thru.dtype), pltpu.HBM(land_thru.shape, land_thru.dtype)),
        in_specs=[_HBM, _HBM, _SEM, _SEM, _SEM] + [pl.BlockSpec(memory_space=pl.ANY)] * len(after),
        out_specs=(_HBM, _HBM),
        input_output_aliases={0: 0, 1: 1},
        compiler_params=pltpu.CompilerParams(has_side_effects=_EFFECT),
    )(src_thru, land_thru, send_sems, recv_sems, own_sem, *after)[1]


def _gather2_copies(shard_ref, land, sems, axis, shard_shape):
    send1, recv1, own_sem, send2, recv2 = sems
    x, y, c = _mesh_pos()
    me, sibling = (x, y, c), (x, y, 1 - c)
    chips = [(1 - x, y), (x, 1 - y), (1 - x, 1 - y)]

    def region(dev):
        px, py, pc = dev
        return _shard_of(land, 4 * px + 2 * py + pc, shard_shape, axis)

    def copy(src, block, to, send, recv):
        return pltpu.make_async_remote_copy(src_ref=src, dst_ref=region(block), send_sem=send, recv_sem=recv,
                                            device_id=to, device_id_type=MESH)

    own = pltpu.make_async_copy(shard_ref, region(me), own_sem.at[0])
    peers = [sibling] + [(*chip, c) for chip in chips]
    sends1 = [copy(shard_ref, me, to, send1.at[k], recv1.at[k]) for k, to in enumerate(peers)]
    arrivals1 = [copy(shard_ref, frm, frm, send1.at[k], recv1.at[k]) for k, frm in enumerate(peers)]
    sends2, arrivals2 = [], []
    if send2 is not None:
        for j, chip in enumerate(chips):
            sends2.append(copy(region((*chip, c)), (*chip, c), sibling, send2.at[j], recv2.at[j]))
            arrivals2.append(copy(region((*chip, 1 - c)), (*chip, 1 - c), sibling, send2.at[j], recv2.at[j]))
    return own, sends1, arrivals1, sends2, arrivals2


def _gather2_start(shard, axis, name, after=None):
    _, land_shape = _exchange_shapes(True, shard.shape, axis)
    extra = () if after is None else (after,)

    def body(*refs):
        src_ref, land = refs[:2]
        send1, recv1, own_sem = refs[2 + len(extra):5 + len(extra)]
        own, sends1, _, _, _ = _gather2_copies(src_ref, land, (send1, recv1, own_sem, None, None), axis, shard.shape)
        own.start()
        for cp in sends1[1:] + sends1[:1]:
            cp.start()
        refs[-1][...] = jnp.zeros_like(refs[-1])

    out = pl.pallas_call(
        body, name=name,
        out_shape=(pltpu.SemaphoreType.DMA((4,)), pltpu.SemaphoreType.DMA((4,)), pltpu.SemaphoreType.DMA((1,)),
                   pltpu.HBM(shard.shape, shard.dtype), pltpu.HBM(land_shape, shard.dtype),
                   jax.ShapeDtypeStruct((8, LANES), F32)),
        in_specs=[_HBM, _HBM] + [pl.BlockSpec(memory_space=pl.ANY)] * len(extra),
        out_specs=(_SEM, _SEM, _SEM, _HBM, _HBM, pl.BlockSpec(memory_space=pltpu.VMEM)),
        input_output_aliases={0: 3, 1: 4},
        compiler_params=pltpu.CompilerParams(has_side_effects=_EFFECT),
    )(pltpu.with_memory_space_constraint(shard, pltpu.HBM),
      pltpu.with_memory_space_constraint(lax.empty(land_shape, shard.dtype), pltpu.HBM), *extra)
    return out[:5], out[5]


def _gather2_pass(state, axis, after, name):
    send1, recv1, own_sem, shard_thru, land_thru = state

    def body(src_ref, land, send1, recv1, own_sem, after_ref, send2, recv2, src_out, land_out, token):
        _, _, arrivals1, sends2, _ = _gather2_copies(src_ref, land, (send1, recv1, own_sem, send2, recv2), axis,
                                                     shard_thru.shape)
        for arrival, fwd in zip(arrivals1[1:], sends2):
            arrival.wait_recv()
            fwd.start()
        token[...] = jnp.zeros_like(token)

    out = pl.pallas_call(
        body, name=name,
        out_shape=(pltpu.SemaphoreType.DMA((3,)), pltpu.SemaphoreType.DMA((3,)),
                   pltpu.HBM(shard_thru.shape, shard_thru.dtype), pltpu.HBM(land_thru.shape, land_thru.dtype),
                   jax.ShapeDtypeStruct((8, LANES), F32)),
        in_specs=[_HBM, _HBM, _SEM, _SEM, _SEM, pl.BlockSpec(memory_space=pl.ANY)],
        out_specs=(_SEM, _SEM, _HBM, _HBM, pl.BlockSpec(memory_space=pltpu.VMEM)),
        input_output_aliases={0: 2, 1: 3},
        compiler_params=pltpu.CompilerParams(has_side_effects=_EFFECT),
    )(shard_thru, land_thru, send1, recv1, own_sem, after)
    return (send1, recv1, own_sem, out[0], out[1], out[2], out[3]), out[4]


def _gather2_wait(state, axis, after, name):
    send1, recv1, own_sem, send2, recv2, shard_thru, land_thru = state

    def body(src_ref, land, send1, recv1, own_sem, send2, recv2, after_ref, src_dead, got):
        own, sends1, arrivals1, sends2, arrivals2 = _gather2_copies(
            src_ref, land, (send1, recv1, own_sem, send2, recv2), axis, shard_thru.shape)
        for cp in sends1 + sends2:
            cp.wait_send()
        for cp in arrivals1[:1] + arrivals2:
            cp.wait_recv()
        own.wait()

    return pl.pallas_call(
        body, name=name,
        out_shape=(pltpu.HBM(shard_thru.shape, shard_thru.dtype), pltpu.HBM(land_thru.shape, land_thru.dtype)),
        in_specs=[_HBM, _HBM] + [_SEM] * 5 + [pl.BlockSpec(memory_space=pl.ANY)],
        out_specs=(_HBM, _HBM),
        input_output_aliases={0: 0, 1: 1},
        compiler_params=pltpu.CompilerParams(has_side_effects=_EFFECT),
    )(shard_thru, land_thru, send1, recv1, own_sem, send2, recv2, after)[1]


def _sum_slots(slots, name):
    _, r, c = slots.shape
    tr = _pick(r, 512, 8)

    def body(s_ref, o_ref):
        total = s_ref[0]
        for q in range(1, N_DEV):
            total = total + s_ref[q]
        o_ref[...] = total

    return pl.pallas_call(
        body, name=name, grid=(r // tr,),
        in_specs=[pl.BlockSpec((N_DEV, tr, c), lambda i: (0, i, 0))],
        out_specs=pl.BlockSpec((tr, c), lambda i: (i, 0)),
        out_shape=jax.ShapeDtypeStruct((r, c), F32),
        compiler_params=_params(("parallel",)),
    )(slots)


def kernel(x, a_norm, a_in, a_conv, a_out, b_norm, b_in, b_vnorm, b_ws, b_bs, b_out, f_norm, f_up, f_conv_w, f_conv_b, f_down, final_norm, loss_target, m_a_norm, m_a_in, m_a_conv, m_a_out, m_b_norm, m_b_in, m_b_vnorm, m_b_ws, m_b_bs, m_b_out, m_f_norm, m_f_up, m_f_conv_w, m_f_conv_b, m_f_down, m_final_norm, v_a_norm, v_a_in, v_a_conv, v_a_out, v_b_norm, v_b_in, v_b_vnorm, v_b_ws, v_b_bs, v_b_out, v_f_norm, v_f_up, v_f_conv_w, v_f_conv_b, v_f_down, v_final_norm):
    s, d = x.shape[1], x.shape[2]
    n_ffn = f_up.shape[0]
    f2 = f_up.shape[2] * N_DEV
    me = 4 * lax.axis_index("x") + 2 * lax.axis_index("y") + lax.axis_index("c")
    x0 = x.reshape(s, d)
    target = loss_target.reshape(s, d)

    wanted = [("a_in", _cast_layer(a_in, 0, "cast_a_in"), 1),
              ("small", _pack([a_conv, b_norm, b_vnorm, f_conv_w]), 0),
              ("a_out", _cast_layer(a_out, 0, "cast_a_out"), 0),
              ("f_up0", _cast_layer(f_up, 0, "cast_f_up0"), 1), ("f_down0", _cast_layer(f_down, 0, "cast_f_down0"), 0),
              ("b_in", _cast_layer(b_in, 0, "cast_b_in"), 1), ("b_out", _cast_layer(b_out, 0, "cast_b_out"), 0),
              ("f_up1", _cast_layer(f_up, 1, "cast_f_up1"), 1), ("f_down1", _cast_layer(f_down, 1, "cast_f_down1"), 0)]
    coming, tok, h0 = {}, None, None
    for n_started, (key, shard, axis) in enumerate(wanted):
        if n_started == 2:
            tok = h0 = _rmsnorm_fwd(x0, a_norm, "mixa_norm", after=tok)
        state, tok = _gather2_start(shard, axis, f"ag_start_{key}", after=tok)
        coming[key] = (state, axis)

    def pass_on(keys, after):
        for key in keys:
            state, axis = coming[key]
            state, after = _gather2_pass(state, axis, after, f"ag_pass_{key}")
            coming[key] = (state, axis)
        return after

    def arrived(key, after):
        state, axis = coming[key]
        return _gather2_wait(state, axis, after, f"ag_wait_{key}")

    cshard = a_conv.shape[2]
    fshard = f_conv_w.shape[2]
    w_a_in = arrived("a_in", pass_on(["a_in", "small"], tok))
    small_full = arrived("small", w_a_in)
    small_rows = small_full.reshape(N_DEV, -1)
    per_dev = _unpack_rows(small_rows, [(3, cshard), (cshard,), (cshard,), (n_ffn, 3, fshard)])
    a_conv_full = per_dev[0].transpose(1, 0, 2).reshape(3, d)
    b_norm_full = per_dev[1].reshape(1, d)
    b_vnorm_full = per_dev[2].reshape(1, d)
    f_conv_w_full = per_dev[3].transpose(1, 2, 0, 3).reshape(n_ffn, 3, f2)
    bs_wide = jnp.broadcast_to(b_bs[0][:, :, None], (SG_GROUPS, CHUNK, LANES))
    ws = b_ws[0]

    w_f_up, w_f_down = {}, {}

    def ffn_forward(xin, l, pass_first, pass_early, pass_late):
        h = _rmsnorm_fwd(xin, f_norm[l:l + 1], f"ffn{l}_norm", after=pass_on(pass_first, xin))
        w_f_up[l] = arrived(f"f_up{l}", pass_on(pass_early, h))
        up_g, up_a, cv_g, cv_a, act = _ffn_up_fused(h, w_f_up[l], f_conv_w_full[l], f_conv_b[l:l + 1], f"ffn{l}_up")
        up, cv = (up_g, up_a), (cv_g, cv_a)
        w_f_down[l] = arrived(f"f_down{l}", pass_on(pass_late, act))
        xout = _matmul(act, w_f_down[l], "nn", F32, f"ffn{l}_down", resid=xin, tm_cap=512, tk_cap=act.shape[1],
                       hold_b=True)
        return xout, (h, up, act, cv)

    gb, gc, xs, cva, ya = _mixa_in_fused(h0, w_a_in, a_conv_full, "mixa_in")
    bcx = (gb, gc, xs)
    w_a_out = arrived("a_out", pass_on(["a_out"], ya))
    x1 = _matmul(ya, w_a_out, "nn", F32, "mixa_out", resid=x0, tm_cap=512, tn_cap=d, hold_b=True)
    x2, saved0 = ffn_forward(x1, 0, ["f_up0"], ["f_down0"], ["b_in", "b_out", "f_up1", "f_down1"])
    h2 = _rmsnorm_fwd(x2, b_norm_full, "mixb_norm")
    w_b_in = arrived("b_in", h2)
    pre = _matmul(h2, w_b_in, "nn", BF16, "mixb_in")
    ug = _mixb_fwd(pre, b_vnorm_full, ws, bs_wide, "mixb_mid")
    w_b_out = arrived("b_out", ug)
    x3 = _matmul(ug, w_b_out, "nn", F32, "mixb_out", resid=x2, tm_cap=512, tn_cap=d, hold_b=True)
    x4, saved1 = ffn_forward(x3, 1, [], [], [])
    dx4, dx4b, loss_part, g_final = _final_loss(x4, final_norm.reshape(1, d), target, "loss_head")

    def _rs_start(grad, axis, name):
        return _exchange_start(False, grad, axis, name)

    whole_s = dict(tm_cap=512, tk_cap=s, hold_b=True)

    def ffn_backward(xin, l, saved, dx, dxb):
        h, up, act, cv = saved
        g_down = _matmul(act, dxb, "tn", BF16, f"ffn{l}_down_dw", **whole_s)
        rs_down, tok = _rs_start(g_down, 0, f"rs_start_f_down{l}")
        dup, cwg, cwa, cbg, cba = _ffn_down_dx_fused(dxb, w_f_down[l], up, cv, f_conv_w_full[l], tok,
                                                     f"ffn{l}_down_dx")
        g_cw, g_cb = jnp.concatenate([cwg, cwa], axis=1), jnp.concatenate([cbg, cba], axis=1)
        g_up = _matmul(h, dup, "tn", BF16, f"ffn{l}_up_dw", **whole_s)
        rs_up, tok = _rs_start(g_up, 1, f"rs_start_f_up{l}")
        dh = _matmul(dup, w_f_up[l], "nt", BF16, f"ffn{l}_up_dx", after=tok, tm_cap=512, tn_cap=512,
                     tk_cap=dup.shape[1], hold_b=True)
        dxin, dxinb, g_norm = _rmsnorm_bwd(xin, f_norm[l:l + 1], dh, dx, f"ffn{l}_norm_bwd")
        return dxin, dxinb, (rs_up, rs_down, g_cw, g_cb, g_norm)

    dx3, dx3b, gf1 = ffn_backward(x3, 1, saved1, dx4, dx4b)
    g_b_out = _matmul(ug, dx3b, "tn", BF16, "mixb_out_dw", **whole_s)
    rs_b_out, tok = _rs_start(g_b_out, 0, "rs_start_b_out")
    dug = _matmul(dx3b, w_b_out, "nt", BF16, "mixb_out_dx", after=tok, tm_cap=512, tn_cap=d, hold_b=True)
    dpre, g_ws, g_bs_wide, g_bvnorm = _mixb_bwd(pre, dug, b_vnorm_full, ws, bs_wide, "mixb_mid_bwd")
    g_b_in = _matmul(h2, dpre, "tn", BF16, "mixb_in_dw", **whole_s)
    rs_b_in, tok = _rs_start(g_b_in, 1, "rs_start_b_in")
    dh2 = _matmul(dpre, w_b_in, "nt", BF16, "mixb_in_dx", after=tok, tk_cap=dpre.shape[1], hold_b=True)
    dx2, dx2b, g_bnorm = _rmsnorm_bwd(x2, b_norm_full, dh2, dx3, "mixb_norm_bwd")
    dx1, dx1b, gf0 = ffn_backward(x1, 0, saved0, dx2, dx2b)
    g_a_out = _matmul(ya, dx1b, "tn", BF16, "mixa_out_dw", **whole_s)
    rs_a_out, tok = _rs_start(g_a_out, 0, "rs_start_a_out")
    dbcx, g_aconv = _mixa_out_dx_fused(dx1b, w_a_out, bcx, cva, a_conv_full, tok, "mixa_out_dx")
    g_a_in = _matmul(h0, dbcx, "tn", BF16, "mixa_in_dw", **whole_s)
    rs_a_in, tok = _rs_start(g_a_in, 1, "rs_start_a_in")
    dh0 = _matmul(dbcx, w_a_in, "nt", BF16, "mixa_in_dx", after=tok, tm_cap=512, tk_cap=dbcx.shape[1], hold_b=True)
    grad_x, _, g_anorm = _rmsnorm_bwd(x0, a_norm, dh0, dx1, "mixa_norm_bwd")

    full_shapes = [(1, LANES), (1, d), (3, d), (1, d), (1, d), (SG_GROUPS, CHUNK, CHUNK), (SG_GROUPS, CHUNK),
                   (n_ffn, d), (n_ffn, 3, f2), (n_ffn, f2), (1, d)]
    parts = [loss_part, g_anorm, g_aconv, g_bnorm, g_bvnorm, g_ws, g_bs_wide[:, :, 0],
             jnp.concatenate([gf0[4], gf1[4]], axis=0), jnp.stack([gf0[2], gf1[2]]),
             jnp.concatenate([gf0[3], gf1[3]], axis=0), g_final]
    small_part = _pack(parts)
    small_state, small_tok = _exchange_start(True, small_part, 0, "ar_start_small", after=grad_x)

    big = {}
    for name, states, axis, w, m, v in (
            ("f_down", (gf0[1], gf1[1]), 0, f_down, m_f_down, v_f_down),
            ("f_up", (gf0[0], gf1[0]), 1, f_up, m_f_up, v_f_up),
            ("b_out", (rs_b_out,), 0, b_out, m_b_out, v_b_out), ("b_in", (rs_b_in,), 1, b_in, m_b_in, v_b_in),
            ("a_out", (rs_a_out,), 0, a_out, m_a_out, v_a_out), ("a_in", (rs_a_in,), 1, a_in, m_a_in, v_a_in)):
        recvs = [_exchange_wait(False, st, axis, small_tok, f"rs_wait_{name}{l}") for l, st in enumerate(states)]
        big[name] = _adamw_sharded(recvs, w, m, v, f"adamw_{name}")

    slots = _exchange_wait(True, small_state, 0, [res[0] for res in big.values()], "ar_wait_small")
    total = _sum_slots(slots.reshape((N_DEV,) + small_part.shape), "ar_sum_small")
    (loss_v, r_anorm, r_aconv, r_bnorm, r_bvnorm, r_ws, r_bs, r_fnorm, r_fcw, r_fcb, r_final) = _unpack(total, full_shapes)
    small_grads = [
        r_anorm,
        lax.dynamic_slice_in_dim(r_aconv, me * cshard, cshard, axis=1).reshape(a_conv.shape),
        lax.dynamic_slice_in_dim(r_bnorm, me * cshard, cshard, axis=1),
        lax.dynamic_slice_in_dim(r_bvnorm, me * cshard, cshard, axis=1),
        r_ws.reshape(b_ws.shape), r_bs.reshape(b_bs.shape), r_fnorm,
        lax.dynamic_slice_in_dim(r_fcw, me * fshard, fshard, axis=2),
        r_fcb, r_final.reshape(final_norm.shape)]
    small_w = [a_norm, a_conv, b_norm, b_vnorm, b_ws, b_bs, f_norm, f_conv_w, f_conv_b, final_norm]
    small_m = [m_a_norm, m_a_conv, m_b_norm, m_b_vnorm, m_b_ws, m_b_bs, m_f_norm, m_f_conv_w, m_f_conv_b, m_final_norm]
    small_v = [v_a_norm, v_a_conv, v_b_norm, v_b_vnorm, v_b_ws, v_b_bs, v_f_norm, v_f_conv_w, v_f_conv_b, v_final_norm]
    shapes = [w.shape for w in small_w]
    packed = _adamw_packed(_pack(small_w), _pack(small_grads), _pack(small_m), _pack(small_v), "adamw_small")
    s_delta, s_m, s_v = (_unpack(p, shapes) for p in packed)
    small_names = ["a_norm", "a_conv", "b_norm", "b_vnorm", "b_ws", "b_bs", "f_norm", "f_conv_w", "f_conv_b", "final_norm"]
    small = {nm: (small_grads[i], s_delta[i], s_m[i], s_v[i]) for i, nm in enumerate(small_names)}

    order = ["a_norm", "a_in", "a_conv", "a_out", "b_norm", "b_in", "b_vnorm", "b_ws", "b_bs", "b_out",
             "f_norm", "f_up", "f_conv_w", "f_conv_b", "f_down", "final_norm"]
    res = {nm: (big[nm] if nm in big else small[nm]) for nm in order}
    outs = [loss_v[0, 0], grad_x.reshape(x.shape)]
    for k in range(4):
        outs += [res[nm][k] for nm in order]
    return tuple(outs)


def _unpack_rows(rows, shapes):
    out, off = [], 0
    for shp in shapes:
        n = math.prod(shp)
        out.append(rows[:, off:off + n].reshape((N_DEV,) + tuple(shp)))
        off += n + (-n) % PACK_GRANULE
    return out
```

```python
import math

import jax
import jax.numpy as jnp
from jax import lax
from jax.experimental import pallas as pl
from jax.experimental.pallas import tpu as pltpu

F32 = jnp.float32
BF16 = jnp.bfloat16
MESH = pl.DeviceIdType.MESH

N_DEV = 8
RMS_EPS = 1e-5
CHUNK = 128
SG_GROUPS = 8
ADAM_LR = 0.001
ADAM_B1 = 0.9
ADAM_B2 = 0.999
ADAM_EPS = 1e-08
ADAM_WD = 0.01
ADAM_STEP = 10

LANES = 128
SLAB = 16
FUSE_STRIP = 256
FUSE_HALO = 8
VMEM_LIMIT = 60 * 1024 * 1024
PACK_GRANULE = 8 * LANES


def _pick(dim, cap, mult):
    best = None
    t = mult
    while t <= min(dim, cap):
        if dim % t == 0:
            best = t
        t += mult
    return dim if best is None else best


def _params(semantics=None):
    return pltpu.CompilerParams(dimension_semantics=semantics, vmem_limit_bytes=VMEM_LIMIT)


_DIMS = {
    "nn": (((1,), (0,)), ((), ())),
    "nt": (((1,), (1,)), ((), ())),
    "tn": (((0,), (0,)), ((), ())),
}


def _matmul(a, b, mode, out_dtype, name, resid=None, after=None, tm_cap=1024, tn_cap=1024, tk_cap=2816,
            hold_b=False, single_b=False):
    if mode == "nn":
        (m, k), n = a.shape, b.shape[1]
    elif mode == "nt":
        (m, k), n = a.shape, b.shape[0]
    else:
        (k, m), n = a.shape, b.shape[1]
    tm, tn, tk = _pick(m, tm_cap, LANES), _pick(n, tn_cap, LANES), _pick(k, tk_cap, LANES)
    nk = k // tk
    n_in = 2 + (resid is not None) + (after is not None)

    def body(*refs):
        a_ref, b_ref = refs[:2]
        r_ref = refs[2] if resid is not None else None
        o_ref = refs[n_in]
        prod = lax.dot_general(a_ref[...], b_ref[...], _DIMS[mode], preferred_element_type=F32)

        def finish(r):
            if r_ref is not None:
                r = r + r_ref[...]
            o_ref[...] = r.astype(out_dtype)

        if nk == 1:
            finish(prod)
            return
        acc_ref = refs[n_in + 1]
        kk = pl.program_id(2)

        @pl.when(kk == 0)
        def _():
            acc_ref[...] = prod

        @pl.when(jnp.logical_and(kk > 0, kk < nk - 1))
        def _():
            acc_ref[...] += prod

        @pl.when(kk == nk - 1)
        def _():
            finish(acc_ref[...] + prod)

    def spec(block, index, depth=None):
        mode_ = None if depth is None else pl.Buffered(depth)
        if hold_b:
            return pl.BlockSpec(block, lambda j, i, kk: index(i, j, kk), pipeline_mode=mode_)
        return pl.BlockSpec(block, index, pipeline_mode=mode_)

    b_depth = 1 if (hold_b and nk == 1 and single_b) else None
    a_spec = (spec((tk, tm), lambda i, j, kk: (kk, i)) if mode == "tn"
              else spec((tm, tk), lambda i, j, kk: (i, kk)))
    b_spec = (spec((tn, tk), lambda i, j, kk: (j, kk), b_depth) if mode == "nt"
              else spec((tk, tn), lambda i, j, kk: (kk, j), b_depth))
    o_spec = spec((tm, tn), lambda i, j, kk: (i, j))
    in_specs = [a_spec, b_spec] + ([o_spec] if resid is not None else [])
    args = (a, b) + ((resid,) if resid is not None else ())
    if after is not None:
        in_specs.append(pl.BlockSpec(memory_space=pl.ANY))
        args += (after,)
    return pl.pallas_call(
        body, name=name, grid=(n // tn, m // tm, nk) if hold_b else (m // tm, n // tn, nk),
        in_specs=in_specs, out_specs=o_spec,
        out_shape=jax.ShapeDtypeStruct((m, n), out_dtype),
        scratch_shapes=[pltpu.VMEM((tm, tn), F32)] if nk > 1 else [],
        compiler_params=_params(("parallel", "parallel", "arbitrary")),
    )(*args)


def _rms_stats(xf):
    inv = lax.rsqrt(jnp.mean(xf * xf, axis=-1, keepdims=True) + RMS_EPS)
    return inv, xf * inv


def _rmsnorm_fwd(x, g, name, after=None):
    s, d = x.shape
    tm = _pick(s, 512, SLAB)
    extra = () if after is None else (after,)

    def body(x_ref, g_ref, *rest):
        _, xhat = _rms_stats(x_ref[...])
        rest[-1][...] = (xhat * g_ref[...]).astype(BF16)

    return pl.pallas_call(
        body, name=name, grid=(s // tm,),
        in_specs=[pl.BlockSpec((tm, d), lambda i: (i, 0)), pl.BlockSpec((1, d), lambda i: (0, 0))]
        + [pl.BlockSpec(memory_space=pl.ANY)] * len(extra),
        out_specs=pl.BlockSpec((tm, d), lambda i: (i, 0)),
        out_shape=jax.ShapeDtypeStruct((s, d), BF16),
        compiler_params=_params(("parallel",)),
    )(x, g, *extra)


def _rmsnorm_bwd(x, g, dh, dx_out, name):
    s, d = x.shape
    tm = _pick(s, 256, SLAB)

    def body(x_ref, g_ref, dh_ref, dxo_ref, dxi_ref, dxib_ref, dg_ref):
        inv, xhat = _rms_stats(x_ref[...])
        dhv = dh_ref[...].astype(F32)
        dxhat = dhv * g_ref[...]
        proj = jnp.mean(dxhat * xhat, axis=-1, keepdims=True)
        dx = dxo_ref[...] + inv * (dxhat - xhat * proj)
        dxi_ref[...] = dx
        dxib_ref[...] = dx.astype(BF16)
        part = jnp.sum(dhv * xhat, axis=0, keepdims=True)

        @pl.when(pl.program_id(0) == 0)
        def _():
            dg_ref[...] = part

        @pl.when(pl.program_id(0) > 0)
        def _():
            dg_ref[...] += part

    row = pl.BlockSpec((tm, d), lambda i: (i, 0))
    vec = pl.BlockSpec((1, d), lambda i: (0, 0))
    return pl.pallas_call(
        body, name=name, grid=(s // tm,),
        in_specs=[row, vec, row, row], out_specs=[row, row, vec],
        out_shape=[jax.ShapeDtypeStruct((s, d), F32), jax.ShapeDtypeStruct((s, d), BF16),
                   jax.ShapeDtypeStruct((1, d), F32)],
        compiler_params=_params(("arbitrary",)),
    )(x, g, dh, dx_out)


def _final_loss(x, g, target, name):
    s, d = x.shape
    tm = _pick(s, 256, SLAB)

    def body(x_ref, g_ref, t_ref, dx_ref, dxb_ref, loss_ref, dg_ref):
        inv, xhat = _rms_stats(x_ref[...])
        gv = g_ref[...]
        err = xhat * gv - t_ref[...]
        loss = 0.5 * jnp.sum(jnp.mean(err * err, axis=-1, keepdims=True), axis=0, keepdims=True)
        dy = err * (1.0 / d)
        dxhat = dy * gv
        proj = jnp.mean(dxhat * xhat, axis=-1, keepdims=True)
        dx = inv * (dxhat - xhat * proj)
        dx_ref[...] = dx
        dxb_ref[...] = dx.astype(BF16)
        part = jnp.sum(dy * xhat, axis=0, keepdims=True)
        loss_row = jnp.broadcast_to(loss, (1, LANES))

        @pl.when(pl.program_id(0) == 0)
        def _():
            dg_ref[...] = part
            loss_ref[...] = loss_row

        @pl.when(pl.program_id(0) > 0)
        def _():
            dg_ref[...] += part
            loss_ref[...] += loss_row

    row = pl.BlockSpec((tm, d), lambda i: (i, 0))
    vec = pl.BlockSpec((1, d), lambda i: (0, 0))
    return pl.pallas_call(
        body, name=name, grid=(s // tm,),
        in_specs=[row, vec, row],
        out_specs=[row, row, pl.BlockSpec((1, LANES), lambda i: (0, 0)), vec],
        out_shape=[jax.ShapeDtypeStruct((s, d), F32), jax.ShapeDtypeStruct((s, d), BF16),
                   jax.ShapeDtypeStruct((1, LANES), F32), jax.ShapeDtypeStruct((1, d), F32)],
        compiler_params=_params(("arbitrary",)),
    )(x, g, target)


def _mixa_in_fused(h, w_in, wc, name):
    s, d = h.shape
    tm = _pick(s, 1024, LANES)
    tn = _pick(d, 512, FUSE_STRIP)
    nj = d // tn

    def body(h_ref, wb_ref, wg_ref, wx_ref, wc_ref, gb_ref, gc_ref, xs_ref, cv_ref, y_ref, carry):
        @pl.when(pl.program_id(1) == 0)
        def _():
            carry[...] = jnp.zeros_like(carry)

        def matmul(st):
            cols = slice(st * FUSE_STRIP, (st + 1) * FUSE_STRIP)
            return tuple(jnp.dot(h_ref[...], w_ref[:, cols], preferred_element_type=F32).astype(BF16)
                         for w_ref in (wb_ref, wg_ref, wx_ref))

        n_strips = tn // FUSE_STRIP
        parts = matmul(0)
        for st in range(n_strips):
            parts_next = matmul(st + 1) if st + 1 < n_strips else None
            cols = slice(st * FUSE_STRIP, (st + 1) * FUSE_STRIP)
            for ref, part in zip((gb_ref, gc_ref, xs_ref), parts):
                ref[:, cols] = part
            p = parts[1].astype(F32) * parts[2].astype(F32)
            ext = jnp.concatenate([carry[:, cols], p], axis=0)
            s1 = pltpu.roll(ext, 1, 0)[FUSE_HALO:, :]
            s2 = pltpu.roll(ext, 2, 0)[FUSE_HALO:, :]
            carry[:, cols] = p[tm - FUSE_HALO:, :]
            cv = wc_ref[0:1, cols] * s2 + wc_ref[1:2, cols] * s1 + wc_ref[2:3, cols] * p
            cv_ref[:, cols] = cv.astype(BF16)
            y_ref[:, cols] = (parts[0].astype(F32) * cv).astype(BF16)
            parts = parts_next

    def cols_of(rows, offset):
        return pl.BlockSpec((rows, tn), lambda j, i: (0, j + offset))

    tile = pl.BlockSpec((tm, tn), lambda j, i: (i, j))
    return pl.pallas_call(
        body, name=name, grid=(nj, s // tm),
        in_specs=[pl.BlockSpec((tm, d), lambda j, i: (i, 0)), cols_of(d, 0), cols_of(d, nj), cols_of(d, 2 * nj),
                  cols_of(3, 0)],
        out_specs=[tile] * 5, out_shape=[jax.ShapeDtypeStruct((s, d), BF16)] * 5,
        scratch_shapes=[pltpu.VMEM((FUSE_HALO, tn), F32)],
        compiler_params=_params(("parallel", "arbitrary")),
    )(h, w_in, w_in, w_in, wc)


def _mixa_out_dx_fused(dxb, w_out, bcx, cv, wc, after, name):
    s, d = dxb.shape
    tm = _pick(s, 1024, LANES)
    tn = _pick(d, 512, FUSE_STRIP)
    nj, ni = d // tn, s // tm
    n_steps = nj * ni
    sub = tm // FUSE_HALO

    def body(dx_ref, w_ref, gb_ref, gc_ref, xs_ref, cv_ref, wc_ref, after_ref,
             dbcx_hbm, dwc_ref, out_buf, out_sem, carry, acc):
        j, i = pl.program_id(0), pl.program_id(1)
        step = j * ni + i
        slot = lax.rem(step, 2)
        row0 = pl.multiple_of((ni - 1 - i) * tm, tm)

        def out_copy(part, slot_=None):
            slot_ = slot if slot_ is None else slot_
            col0 = pl.multiple_of(part * d + j * tn, LANES)
            return pltpu.make_async_copy(out_buf.at[slot_, part], dbcx_hbm.at[pl.ds(row0, tm), pl.ds(col0, tn)],
                                         out_sem.at[slot_, part])

        @pl.when(step >= 2)
        def _():
            for part in range(3):
                out_copy(part).wait()

        @pl.when(i == 0)
        def _():
            carry[...] = jnp.zeros_like(carry)
            acc[...] = jnp.zeros_like(acc)

        for st in range(tn // FUSE_STRIP):
            cols = slice(st * FUSE_STRIP, (st + 1) * FUSE_STRIP)
            dyv = lax.dot_general(dx_ref[...], w_ref[cols, :], _DIMS["nt"], preferred_element_type=F32)
            gc = gc_ref[:, cols].astype(F32)
            xs = xs_ref[:, cols].astype(F32)
            d0 = dyv * gb_ref[:, cols].astype(F32)
            ext = jnp.concatenate([d0, carry[:, cols]], axis=0)
            d1 = pltpu.roll(ext, tm + FUSE_HALO - 1, 0)[:tm, :]
            d2 = pltpu.roll(ext, tm + FUSE_HALO - 2, 0)[:tm, :]
            carry[:, cols] = d0[:FUSE_HALO, :]
            dp = wc_ref[2:3, cols] * d0 + wc_ref[1:2, cols] * d1 + wc_ref[0:1, cols] * d2
            out_buf[slot, 0, :, cols] = (dyv * cv_ref[:, cols].astype(F32)).astype(BF16)
            out_buf[slot, 1, :, cols] = (dp * xs).astype(BF16)
            out_buf[slot, 2, :, cols] = (dp * gc).astype(BF16)
            p = gc * xs
            for k, term in enumerate((d2 * p, d1 * p, d0 * p)):
                acc[k, :, cols] += jnp.sum(term.reshape(sub, FUSE_HALO, FUSE_STRIP), axis=0)

        for part in range(3):
            out_copy(part).start()

        @pl.when(i == ni - 1)
        def _():
            for k in range(3):
                dwc_ref[k:k + 1, :] = jnp.sum(acc[k], axis=0, keepdims=True)

        @pl.when(step == n_steps - 1)
        def _():
            for part in range(3):
                out_copy(part).wait()
                if n_steps > 1:
                    out_copy(part, 1 - slot).wait()

    tile = pl.BlockSpec((tm, tn), lambda j, i: (ni - 1 - i, j))
    return pl.pallas_call(
        body, name=name, grid=(nj, ni),
        in_specs=[pl.BlockSpec((tm, d), lambda j, i: (ni - 1 - i, 0)), pl.BlockSpec((tn, d), lambda j, i: (j, 0)),
                  tile, tile, tile, tile, pl.BlockSpec((3, tn), lambda j, i: (0, j)),
                  pl.BlockSpec(memory_space=pl.ANY)],
        out_specs=[pl.BlockSpec(memory_space=pl.ANY), pl.BlockSpec((3, tn), lambda j, i: (0, j))],
        out_shape=[jax.ShapeDtypeStruct((s, 3 * d), BF16), jax.ShapeDtypeStruct((3, d), F32)],
        scratch_shapes=[pltpu.VMEM((2, 3, tm, tn), BF16), pltpu.SemaphoreType.DMA((2, 3)),
                        pltpu.VMEM((FUSE_HALO, tn), F32), pltpu.VMEM((3, FUSE_HALO, tn), F32)],
        compiler_params=_params(("arbitrary", "arbitrary")),
    )(dxb, w_out, bcx[0], bcx[1], bcx[2], cv, wc, after)


def _sigmoid(z):
    return 0.5 * jnp.tanh(0.5 * z) + 0.5


def _ffn_up_fused(h, w_up, cw, cb, name):
    s, d = h.shape
    f = w_up.shape[1] // 2
    tm = _pick(s, 1024, LANES)
    tn = _pick(f, 512, FUSE_STRIP)
    nj = f // tn

    def body(h_ref, wg_ref, wa_ref, cwg_ref, cwa_ref, cbg_ref, cba_ref,
             upg_ref, upa_ref, cvg_ref, cva_ref, act_ref, carry_g, carry_a):
        @pl.when(pl.program_id(1) == 0)
        def _():
            carry_g[...] = jnp.zeros_like(carry_g)
            carry_a[...] = jnp.zeros_like(carry_a)

        def matmul(st):
            cols = slice(st * FUSE_STRIP, (st + 1) * FUSE_STRIP)
            return tuple(jnp.dot(h_ref[...], w_ref[:, cols], preferred_element_type=F32).astype(BF16)
                         for w_ref in (wg_ref, wa_ref))

        def conv(up, cw_ref, cb_ref, carry, up_ref, cv_ref, cols):
            up_ref[:, cols] = up
            x = up.astype(F32)
            ext = jnp.concatenate([carry[:, cols], x], axis=0)
            s1 = pltpu.roll(ext, 1, 0)[FUSE_HALO:, :]
            s2 = pltpu.roll(ext, 2, 0)[FUSE_HALO:, :]
            carry[:, cols] = x[tm - FUSE_HALO:, :]
            cv = cw_ref[0:1, cols] * s2 + cw_ref[1:2, cols] * s1 + cw_ref[2:3, cols] * x + cb_ref[:, cols]
            cv_ref[:, cols] = cv.astype(BF16)
            return cv

        n_strips = tn // FUSE_STRIP
        ups = matmul(0)
        for st in range(n_strips):
            ups_next = matmul(st + 1) if st + 1 < n_strips else None
            cols = slice(st * FUSE_STRIP, (st + 1) * FUSE_STRIP)
            gcv = conv(ups[0], cwg_ref, cbg_ref, carry_g, upg_ref, cvg_ref, cols)
            acv = conv(ups[1], cwa_ref, cba_ref, carry_a, upa_ref, cva_ref, cols)
            act_ref[:, cols] = (gcv * _sigmoid(gcv) * acv).astype(BF16)
            ups = ups_next

    def cols_of(rows, offset):
        return pl.BlockSpec((rows, tn), lambda j, i: (0, j + offset))

    tile = pl.BlockSpec((tm, tn), lambda j, i: (i, j))
    out = jax.ShapeDtypeStruct((s, f), BF16)
    return pl.pallas_call(
        body, name=name, grid=(nj, s // tm),
        in_specs=[pl.BlockSpec((tm, d), lambda j, i: (i, 0)), cols_of(d, 0), cols_of(d, nj),
                  cols_of(3, 0), cols_of(3, nj), cols_of(1, 0), cols_of(1, nj)],
        out_specs=[tile] * 5, out_shape=[out] * 5,
        scratch_shapes=[pltpu.VMEM((FUSE_HALO, tn), F32)] * 2,
        compiler_params=_params(("parallel", "arbitrary")),
    )(h, w_up, w_up, cw, cw, cb, cb)


def _ffn_down_dx_fused(dxb, w_down, up, cv, cw, after, name):
    s, d = dxb.shape
    f = w_down.shape[0]
    tm = _pick(s, 1024, LANES)
    tn = _pick(f, 512, FUSE_STRIP)
    nj, ni = f // tn, s // tm
    n_steps = nj * ni
    sub = tm // FUSE_HALO

    def body(dx_ref, w_ref, upg_ref, upa_ref, cvg_ref, cva_ref, cwg_ref, cwa_ref, after_ref,
             dup_hbm, dcwg_ref, dcwa_ref, dcbg_ref, dcba_ref, out_buf, out_sem, carry, acc):
        j, i = pl.program_id(0), pl.program_id(1)
        step = j * ni + i
        slot = lax.rem(step, 2)
        row0 = pl.multiple_of((ni - 1 - i) * tm, tm)

        def out_copy(half):
            col0 = pl.multiple_of(half * f + j * tn, LANES)
            return pltpu.make_async_copy(out_buf.at[slot, half], dup_hbm.at[pl.ds(row0, tm), pl.ds(col0, tn)],
                                         out_sem.at[slot, half])

        @pl.when(step >= 2)
        def _():
            for half in range(2):
                out_copy(half).wait()

        @pl.when(i == 0)
        def _():
            carry[...] = jnp.zeros_like(carry)
            acc[...] = jnp.zeros_like(acc)

        for st in range(tn // FUSE_STRIP):
            cols = slice(st * FUSE_STRIP, (st + 1) * FUSE_STRIP)
            dact = lax.dot_general(dx_ref[...], w_ref[cols, :], _DIMS["nt"], preferred_element_type=F32)
            gcv = cvg_ref[:, cols].astype(F32)
            acv = cva_ref[:, cols].astype(F32)
            sg = _sigmoid(gcv)
            dd = (dact * acv * (sg * (1.0 + gcv * (1.0 - sg))), dact * (gcv * sg))
            for half, (up_ref, cw_ref) in enumerate(((upg_ref, cwg_ref), (upa_ref, cwa_ref))):
                x = up_ref[:, cols].astype(F32)
                d0 = dd[half]
                ext = jnp.concatenate([d0, carry[half, :, cols]], axis=0)
                d1 = pltpu.roll(ext, tm + FUSE_HALO - 1, 0)[:tm, :]
                d2 = pltpu.roll(ext, tm + FUSE_HALO - 2, 0)[:tm, :]
                carry[half, :, cols] = d0[:FUSE_HALO, :]
                out_buf[slot, half, :, cols] = (cw_ref[2:3, cols] * d0 + cw_ref[1:2, cols] * d1
                                                + cw_ref[0:1, cols] * d2).astype(BF16)
                for k, term in enumerate((d2 * x, d1 * x, d0 * x, d0)):
                    acc[half, k, :, cols] += jnp.sum(term.reshape(sub, FUSE_HALO, FUSE_STRIP), axis=0)

        for half in range(2):
            out_copy(half).start()

        @pl.when(i == ni - 1)
        def _():
            for half, (dcw_ref, dcb_ref) in enumerate(((dcwg_ref, dcbg_ref), (dcwa_ref, dcba_ref))):
                for k in range(3):
                    dcw_ref[k:k + 1, :] = jnp.sum(acc[half, k], axis=0, keepdims=True)
                dcb_ref[...] = jnp.sum(acc[half, 3], axis=0, keepdims=True)

        @pl.when(step == n_steps - 1)
        def _():
            for half in range(2):
                out_copy(half).wait()
                if n_steps > 1:
                    pltpu.make_async_copy(out_buf.at[1 - slot, half], dup_hbm.at[pl.ds(row0, tm), pl.ds(0, tn)],
                                          out_sem.at[1 - slot, half]).wait()

    tile = pl.BlockSpec((tm, tn), lambda j, i: (ni - 1 - i, j))

    def cols_of(rows, offset):
        return pl.BlockSpec((rows, tn), lambda j, i: (0, j + offset))

    small = pl.BlockSpec((3, tn), lambda j, i: (0, j)), pl.BlockSpec((1, tn), lambda j, i: (0, j))
    return pl.pallas_call(
        body, name=name, grid=(nj, ni),
        in_specs=[pl.BlockSpec((tm, d), lambda j, i: (ni - 1 - i, 0)), pl.BlockSpec((tn, d), lambda j, i: (j, 0)),
                  tile, tile, tile, tile, cols_of(3, 0), cols_of(3, nj), pl.BlockSpec(memory_space=pl.ANY)],
        out_specs=[pl.BlockSpec(memory_space=pl.ANY), small[0], small[0], small[1], small[1]],
        out_shape=[jax.ShapeDtypeStruct((s, 2 * f), BF16), jax.ShapeDtypeStruct((3, f), F32),
                   jax.ShapeDtypeStruct((3, f), F32), jax.ShapeDtypeStruct((1, f), F32),
                   jax.ShapeDtypeStruct((1, f), F32)],
        scratch_shapes=[pltpu.VMEM((2, 2, tm, tn), BF16), pltpu.SemaphoreType.DMA((2, 2)),
                        pltpu.VMEM((2, FUSE_HALO, tn), F32), pltpu.VMEM((2, 4, FUSE_HALO, tn), F32)],
        compiler_params=_params(("arbitrary", "arbitrary")),
    )(dxb, w_down, up[0], up[1], cv[0], cv[1], cw, cw, after)


_GELU_C = math.sqrt(2.0 / math.pi)


def _gelu(x):
    th = jnp.tanh(_GELU_C * (x + 0.044715 * (x * x * x)))
    return x * (0.5 * (1.0 + th)), th


def _gelu_grad(x, th):
    return 0.5 * (1.0 + th) + 0.5 * x * (1.0 - th * th) * (_GELU_C * (1.0 + 3.0 * 0.044715 * (x * x)))


def _masked_ws(ws_ref, h):
    t = lax.broadcasted_iota(jnp.int32, (CHUNK, CHUNK), 0)
    sx = lax.broadcasted_iota(jnp.int32, (CHUNK, CHUNK), 1)
    return jnp.where(sx <= t, ws_ref[h], 0.0)


def _mixb_fwd(pre, gv, ws, bs_wide, name):
    s, w2 = pre.shape
    w = w2 // 2
    gw = w // SG_GROUPS

    def body(pre_ref, gv_ref, ws_ref, bs_ref, o_ref):
        zu, _ = _gelu(pre_ref[:, :w].astype(F32))
        zv, _ = _gelu(pre_ref[:, w:].astype(F32))
        _, vhat = _rms_stats(zv)
        vn = (vhat * gv_ref[...]).astype(BF16)
        for h in range(SG_GROUPS):
            cols = slice(h * gw, (h + 1) * gw)
            wsm = _masked_ws(ws_ref, h).astype(BF16)
            gate = jnp.dot(wsm, vn[:, cols], preferred_element_type=F32)
            gate = gate + jnp.tile(bs_ref[h], (1, gw // LANES))
            o_ref[:, cols] = (zu[:, cols] * gate).astype(BF16)

    return pl.pallas_call(
        body, name=name, grid=(s // CHUNK,),
        in_specs=[pl.BlockSpec((CHUNK, w2), lambda i: (i, 0)), pl.BlockSpec((1, w), lambda i: (0, 0)),
                  pl.BlockSpec((SG_GROUPS, CHUNK, CHUNK), lambda i: (0, 0, 0)),
                  pl.BlockSpec((SG_GROUPS, CHUNK, LANES), lambda i: (0, 0, 0))],
        out_specs=pl.BlockSpec((CHUNK, w), lambda i: (i, 0)),
        out_shape=jax.ShapeDtypeStruct((s, w), BF16),
        compiler_params=_params(("parallel",)),
    )(pre, gv, ws, bs_wide)


def _mixb_bwd(pre, dug, gv, ws, bs_wide, name):
    s, w2 = pre.shape
    w = w2 // 2
    gw = w // SG_GROUPS

    def body(pre_ref, dug_ref, gv_ref, ws_ref, bs_ref, o_ref, dws_ref, dbs_ref, dgv_ref, dvn_ref):
        first = pl.program_id(0) == 0

        @pl.when(first)
        def _():
            dws_ref[...] = jnp.zeros_like(dws_ref)
            dbs_ref[...] = jnp.zeros_like(dbs_ref)

        pu = pre_ref[:, :w].astype(F32)
        pv = pre_ref[:, w:].astype(F32)
        zu, thu = _gelu(pu)
        zv, thv = _gelu(pv)
        inv, vhat = _rms_stats(zv)
        gvv = gv_ref[...]
        vn = (vhat * gvv).astype(BF16)
        for h in range(SG_GROUPS):
            cols = slice(h * gw, (h + 1) * gw)
            wsm = _masked_ws(ws_ref, h).astype(BF16)
            gate = jnp.dot(wsm, vn[:, cols], preferred_element_type=F32)
            gate = gate + jnp.tile(bs_ref[h], (1, gw // LANES))
            dug_h = dug_ref[:, cols].astype(F32)
            dgate = dug_h * zu[:, cols]
            dgate_b = dgate.astype(BF16)
            o_ref[:, cols] = (dug_h * gate * _gelu_grad(pu[:, cols], thu[:, cols])).astype(BF16)
            dbs_ref[h] += jnp.broadcast_to(jnp.sum(dgate, axis=-1, keepdims=True), (CHUNK, LANES))
            dws = lax.dot_general(dgate_b, vn[:, cols], _DIMS["nt"], preferred_element_type=F32)
            t = lax.broadcasted_iota(jnp.int32, (CHUNK, CHUNK), 0)
            sx = lax.broadcasted_iota(jnp.int32, (CHUNK, CHUNK), 1)
            dws_ref[h] += jnp.where(sx <= t, dws, 0.0)
            dvn_ref[:, cols] = lax.dot_general(wsm, dgate_b, _DIMS["tn"], preferred_element_type=F32)
        dvn = dvn_ref[...]
        part = jnp.sum(dvn * vhat, axis=0, keepdims=True)

        @pl.when(first)
        def _():
            dgv_ref[...] = part

        @pl.when(jnp.logical_not(first))
        def _():
            dgv_ref[...] += part

        dvhat = dvn * gvv
        dzv = inv * (dvhat - vhat * jnp.mean(dvhat * vhat, axis=-1, keepdims=True))
        o_ref[:, w:] = (dzv * _gelu_grad(pv, thv)).astype(BF16)

    return pl.pallas_call(
        body, name=name, grid=(s // CHUNK,),
        in_specs=[pl.BlockSpec((CHUNK, w2), lambda i: (i, 0)), pl.BlockSpec((CHUNK, w), lambda i: (i, 0)),
                  pl.BlockSpec((1, w), lambda i: (0, 0)),
                  pl.BlockSpec((SG_GROUPS, CHUNK, CHUNK), lambda i: (0, 0, 0)),
                  pl.BlockSpec((SG_GROUPS, CHUNK, LANES), lambda i: (0, 0, 0))],
        out_specs=[pl.BlockSpec((CHUNK, w2), lambda i: (i, 0)),
                   pl.BlockSpec((SG_GROUPS, CHUNK, CHUNK), lambda i: (0, 0, 0)),
                   pl.BlockSpec((SG_GROUPS, CHUNK, LANES), lambda i: (0, 0, 0)),
                   pl.BlockSpec((1, w), lambda i: (0, 0))],
        out_shape=[jax.ShapeDtypeStruct((s, w2), BF16), jax.ShapeDtypeStruct((SG_GROUPS, CHUNK, CHUNK), F32),
                   jax.ShapeDtypeStruct((SG_GROUPS, CHUNK, LANES), F32), jax.ShapeDtypeStruct((1, w), F32)],
        scratch_shapes=[pltpu.VMEM((CHUNK, w), F32)],
        compiler_params=_params(("arbitrary",)),
    )(pre, dug, gv, ws, bs_wide)


def _cast_layer(w3, layer, name):
    _, r, c = w3.shape
    tr = _pick(r, 256, SLAB)

    def body(w_ref, o_ref):
        o_ref[...] = w_ref[...].astype(BF16)

    return pl.pallas_call(
        body, name=name, grid=(r // tr,),
        in_specs=[pl.BlockSpec((None, tr, c), lambda i: (layer, i, 0))],
        out_specs=pl.BlockSpec((tr, c), lambda i: (i, 0)),
        out_shape=jax.ShapeDtypeStruct((r, c), BF16),
        compiler_params=_params(("parallel",)),
    )(w3)


def _adamw_math(w, g, m, v):
    m = ADAM_B1 * m + (1.0 - ADAM_B1) * g
    v = ADAM_B2 * v + (1.0 - ADAM_B2) * (g * g)
    m_hat = m / (1.0 - ADAM_B1 ** ADAM_STEP)
    v_hat = v / (1.0 - ADAM_B2 ** ADAM_STEP)
    delta = -ADAM_LR * (m_hat / (jnp.sqrt(v_hat) + ADAM_EPS) + ADAM_WD * w)
    return delta, m, v


def _adamw_sharded(recvs, w, m, v, name):
    nl, r, c = w.shape
    tc = _pick(c, 1536, LANES)
    tr = _pick(r, 64, SLAB)

    def body(*refs):
        recv_refs = refs[:nl]
        w_ref, m_ref, v_ref, g_ref, d_ref, nm_ref, nv_ref = refs[nl:]
        for layer, recv_ref in enumerate(recv_refs):
            @pl.when(pl.program_id(0) == layer)
            def _():
                g = recv_ref[0].astype(F32)
                for q in range(1, N_DEV):
                    g = g + recv_ref[q].astype(F32)
                delta, nm, nv = _adamw_math(w_ref[...], g, m_ref[...], v_ref[...])
                g_ref[...] = g
                d_ref[...] = delta
                nm_ref[...] = nm
                nv_ref[...] = nv

    def recv_spec(layer):
        return pl.BlockSpec((N_DEV, tr, tc),
                            lambda l, i, j: (0, jnp.where(l == layer, i, 0), jnp.where(l == layer, j, 0)))

    blk = pl.BlockSpec((None, tr, tc), lambda l, i, j: (l, i, j))
    out = jax.ShapeDtypeStruct((nl, r, c), F32)
    return pl.pallas_call(
        body, name=name, grid=(nl, r // tr, c // tc),
        in_specs=[recv_spec(layer) for layer in range(nl)] + [blk, blk, blk],
        out_specs=[blk] * 4, out_shape=[out] * 4,
        compiler_params=_params(("parallel",) * 3),
    )(*recvs, w, m, v)


def _adamw_packed(w, g, m, v, name):
    r, c = w.shape
    tr = _pick(r, 256, 8)

    def body(w_ref, g_ref, m_ref, v_ref, d_ref, nm_ref, nv_ref):
        delta, nm, nv = _adamw_math(w_ref[...], g_ref[...], m_ref[...], v_ref[...])
        d_ref[...] = delta
        nm_ref[...] = nm
        nv_ref[...] = nv

    blk = pl.BlockSpec((tr, c), lambda i: (i, 0))
    out = jax.ShapeDtypeStruct((r, c), F32)
    return pl.pallas_call(
        body, name=name, grid=(r // tr,), in_specs=[blk] * 4, out_specs=[blk] * 3, out_shape=[out] * 3,
        compiler_params=_params(("parallel",)),
    )(w, g, m, v)


def _pack(arrays):
    parts = []
    for a in arrays:
        flat = a.reshape(-1).astype(F32)
        pad = (-flat.shape[0]) % PACK_GRANULE
        parts.append(jnp.pad(flat, (0, pad)) if pad else flat)
    return jnp.concatenate(parts).reshape(-1, LANES)


def _unpack(buf, shapes):
    flat = buf.reshape(-1)
    out, off = [], 0
    for shp in shapes:
        n = math.prod(shp)
        out.append(flat[off:off + n].reshape(shp))
        off += n + (-n) % PACK_GRANULE
    return out


def _mesh_pos():
    return lax.axis_index("x"), lax.axis_index("y"), lax.axis_index("c")


def _coords(q):
    return q // 4, (q // 2) % 2, q % 2


def _shard_of(ref, q, shard_shape, axis):
    r, c = shard_shape
    if axis == 0:
        return ref.at[pl.ds(pl.multiple_of(q * r, SLAB), r), :]
    return ref.at[:, pl.ds(pl.multiple_of(q * c, LANES), c)]


_HBM = pl.BlockSpec(memory_space=pltpu.HBM)
_SEM = pl.BlockSpec(memory_space=pltpu.SEMAPHORE)
_EFFECT = pltpu.SideEffectType.DATAFLOW_SIDE_EFFECTING


def _exchange_shapes(gather, src_shape, axis):
    r, c = src_shape
    if gather:
        return (r, c), ((r * N_DEV, c) if axis == 0 else (r, c * N_DEV))
    shard = (r // N_DEV, c) if axis == 0 else (r, c // N_DEV)
    return shard, (N_DEV,) + shard


def _exchange_copies(gather, src, land, sems, axis):
    send_sems, recv_sems, own_sem = sems
    x, y, c_ = _mesh_pos()
    me = 4 * x + 2 * y + c_
    shard, _ = _exchange_shapes(gather, src.shape, axis)

    def piece(q):
        return src if gather else _shard_of(src, q, shard, axis)

    def place(q):
        return _shard_of(land, q, shard, axis) if gather else land.at[q]

    own = pltpu.make_async_copy(piece(me), place(me), own_sem.at[0])
    sends, arrivals = [], []
    for step in range(1, N_DEV):
        to = (me + step) % N_DEV
        frm = (me + N_DEV - step) % N_DEV
        sends.append(pltpu.make_async_remote_copy(
            src_ref=piece(to), dst_ref=place(me), send_sem=send_sems.at[step - 1], recv_sem=recv_sems.at[step - 1],
            device_id=_coords(to), device_id_type=MESH))
        arrivals.append(pltpu.make_async_remote_copy(
            src_ref=piece(me), dst_ref=place(frm), send_sem=send_sems.at[step - 1], recv_sem=recv_sems.at[step - 1],
            device_id=_coords(frm), device_id_type=MESH))
    return own, sends, arrivals


def _exchange_start(gather, src, axis, name, after=None):
    _, land_shape = _exchange_shapes(gather, src.shape, axis)
    extra = () if after is None else (after,)

    def body(*refs):
        src_ref, land = refs[:2]
        send_sems, recv_sems, own_sem = refs[2 + len(extra):5 + len(extra)]
        own, sends, _ = _exchange_copies(gather, src_ref, land, (send_sems, recv_sems, own_sem), axis)
        own.start()
        for cp in sends:
            cp.start()
        refs[-1][...] = jnp.zeros_like(refs[-1])

    out = pl.pallas_call(
        body, name=name,
        out_shape=(pltpu.SemaphoreType.DMA((N_DEV - 1,)), pltpu.SemaphoreType.DMA((N_DEV - 1,)),
                   pltpu.SemaphoreType.DMA((1,)), pltpu.HBM(src.shape, src.dtype),
                   pltpu.HBM(land_shape, src.dtype), jax.ShapeDtypeStruct((8, LANES), F32)),
        in_specs=[_HBM, _HBM] + [pl.BlockSpec(memory_space=pl.ANY)] * len(extra),
        out_specs=(_SEM, _SEM, _SEM, _HBM, _HBM, pl.BlockSpec(memory_space=pltpu.VMEM)),
        input_output_aliases={0: 3, 1: 4},
        compiler_params=pltpu.CompilerParams(has_side_effects=_EFFECT),
    )(pltpu.with_memory_space_constraint(src, pltpu.HBM),
      pltpu.with_memory_space_constraint(lax.empty(land_shape, src.dtype), pltpu.HBM), *extra)
    return out[:5], out[5]


def _exchange_wait(gather, state, axis, after, name):
    send_sems, recv_sems, own_sem, src_thru, land_thru = state
    after = tuple(after) if isinstance(after, (tuple, list)) else (after,)

    def body(src, land, send_sems, recv_sems, own_sem, *rest):
        own, sends, arrivals = _exchange_copies(gather, src, land, (send_sems, recv_sems, own_sem), axis)
        for cp in sends:
            cp.wait_send()
        for cp in arrivals:
            cp.wait_recv()
        own.wait()

    return pl.pallas_call(
        body, name=name,
        out_shape=(pltpu.HBM(src_thru.shape, src_thru.dtype), pltpu.HBM(land_thru.shape, land_thru.dtype)),
        in_specs=[_HBM, _HBM, _SEM, _SEM, _SEM] + [pl.BlockSpec(memory_space=pl.ANY)] * len(after),
        out_specs=(_HBM, _HBM),
        input_output_aliases={0: 0, 1: 1},
        compiler_params=pltpu.CompilerParams(has_side_effects=_EFFECT),
    )(src_thru, land_thru, send_sems, recv_sems, own_sem, *after)[1]


def _gather2_copies(shard_ref, land, sems, axis, shard_shape):
    send1, recv1, own_sem, send2, recv2 = sems
    x, y, c = _mesh_pos()
    me, sibling = (x, y, c), (x, y, 1 - c)
    chips = [(1 - x, y), (x, 1 - y), (1 - x, 1 - y)]

    def region(dev):
        px, py, pc = dev
        return _shard_of(land, 4 * px + 2 * py + pc, shard_shape, axis)

    def copy(src, block, to, send, recv):
        return pltpu.make_async_remote_copy(src_ref=src, dst_ref=region(block), send_sem=send, recv_sem=recv,
                                            device_id=to, device_id_type=MESH)

    own = pltpu.make_async_copy(shard_ref, region(me), own_sem.at[0])
    peers = [sibling] + [(*chip, c) for chip in chips]
    sends1 = [copy(shard_ref, me, to, send1.at[k], recv1.at[k]) for k, to in enumerate(peers)]
    arrivals1 = [copy(shard_ref, frm, frm, send1.at[k], recv1.at[k]) for k, frm in enumerate(peers)]
    sends2, arrivals2 = [], []
    if send2 is not None:
        for j, chip in enumerate(chips):
            sends2.append(copy(region((*chip, c)), (*chip, c), sibling, send2.at[j], recv2.at[j]))
            arrivals2.append(copy(region((*chip, 1 - c)), (*chip, 1 - c), sibling, send2.at[j], recv2.at[j]))
    return own, sends1, arrivals1, sends2, arrivals2


def _gather2_start(shard, axis, name, after=None):
    _, land_shape = _exchange_shapes(True, shard.shape, axis)
    extra = () if after is None else (after,)

    def body(*refs):
        src_ref, land = refs[:2]
        send1, recv1, own_sem = refs[2 + len(extra):5 + len(extra)]
        own, sends1, _, _, _ = _gather2_copies(src_ref, land, (send1, recv1, own_sem, None, None), axis, shard.shape)
        own.start()
        for cp in sends1[1:] + sends1[:1]:
            cp.start()
        refs[-1][...] = jnp.zeros_like(refs[-1])

    out = pl.pallas_call(
        body, name=name,
        out_shape=(pltpu.SemaphoreType.DMA((4,)), pltpu.SemaphoreType.DMA((4,)), pltpu.SemaphoreType.DMA((1,)),
                   pltpu.HBM(shard.shape, shard.dtype), pltpu.HBM(land_shape, shard.dtype),
                   jax.ShapeDtypeStruct((8, LANES), F32)),
        in_specs=[_HBM, _HBM] + [pl.BlockSpec(memory_space=pl.ANY)] * len(extra),
        out_specs=(_SEM, _SEM, _SEM, _HBM, _HBM, pl.BlockSpec(memory_space=pltpu.VMEM)),
        input_output_aliases={0: 3, 1: 4},
        compiler_params=pltpu.CompilerParams(has_side_effects=_EFFECT),
    )(pltpu.with_memory_space_constraint(shard, pltpu.HBM),
      pltpu.with_memory_space_constraint(lax.empty(land_shape, shard.dtype), pltpu.HBM), *extra)
    return out[:5], out[5]


def _gather2_pass(state, axis, after, name):
    send1, recv1, own_sem, shard_thru, land_thru = state

    def body(src_ref, land, send1, recv1, own_sem, after_ref, send2, recv2, src_out, land_out, token):
        _, _, arrivals1, sends2, _ = _gather2_copies(src_ref, land, (send1, recv1, own_sem, send2, recv2), axis,
                                                     shard_thru.shape)
        for arrival, fwd in zip(arrivals1[1:], sends2):
            arrival.wait_recv()
            fwd.start()
        token[...] = jnp.zeros_like(token)

    out = pl.pallas_call(
        body, name=name,
        out_shape=(pltpu.SemaphoreType.DMA((3,)), pltpu.SemaphoreType.DMA((3,)),
                   pltpu.HBM(shard_thru.shape, shard_thru.dtype), pltpu.HBM(land_thru.shape, land_thru.dtype),
                   jax.ShapeDtypeStruct((8, LANES), F32)),
        in_specs=[_HBM, _HBM, _SEM, _SEM, _SEM, pl.BlockSpec(memory_space=pl.ANY)],
        out_specs=(_SEM, _SEM, _HBM, _HBM, pl.BlockSpec(memory_space=pltpu.VMEM)),
        input_output_aliases={0: 2, 1: 3},
        compiler_params=pltpu.CompilerParams(has_side_effects=_EFFECT),
    )(shard_thru, land_thru, send1, recv1, own_sem, after)
    return (send1, recv1, own_sem, out[0], out[1], out[2], out[3]), out[4]


def _gather2_wait(state, axis, after, name):
    send1, recv1, own_sem, send2, recv2, shard_thru, land_thru = state

    def body(src_ref, land, send1, recv1, own_sem, send2, recv2, after_ref, src_dead, got):
        own, sends1, arrivals1, sends2, arrivals2 = _gather2_copies(
            src_ref, land, (send1, recv1, own_sem, send2, recv2), axis, shard_thru.shape)
        for cp in sends1 + sends2:
            cp.wait_send()
        for cp in arrivals1[:1] + arrivals2:
            cp.wait_recv()
        own.wait()

    return pl.pallas_call(
        body, name=name,
        out_shape=(pltpu.HBM(shard_thru.shape, shard_thru.dtype), pltpu.HBM(land_thru.shape, land_thru.dtype)),
        in_specs=[_HBM, _HBM] + [_SEM] * 5 + [pl.BlockSpec(memory_space=pl.ANY)],
        out_specs=(_HBM, _HBM),
        input_output_aliases={0: 0, 1: 1},
        compiler_params=pltpu.CompilerParams(has_side_effects=_EFFECT),
    )(shard_thru, land_thru, send1, recv1, own_sem, send2, recv2, after)[1]


def _sum_slots(slots, name):
    _, r, c = slots.shape
    tr = _pick(r, 512, 8)

    def body(s_ref, o_ref):
        total = s_ref[0]
        for q in range(1, N_DEV):
            total = total + s_ref[q]
        o_ref[...] = total

    return pl.pallas_call(
        body, name=name, grid=(r // tr,),
        in_specs=[pl.BlockSpec((N_DEV, tr, c), lambda i: (0, i, 0))],
        out_specs=pl.BlockSpec((tr, c), lambda i: (i, 0)),
        out_shape=jax.ShapeDtypeStruct((r, c), F32),
        compiler_params=_params(("parallel",)),
    )(slots)


def kernel(x, a_norm, a_in, a_conv, a_out, b_norm, b_in, b_vnorm, b_ws, b_bs, b_out, f_norm, f_up, f_conv_w, f_conv_b, f_down, final_norm, loss_target, m_a_norm, m_a_in, m_a_conv, m_a_out, m_b_norm, m_b_in, m_b_vnorm, m_b_ws, m_b_bs, m_b_out, m_f_norm, m_f_up, m_f_conv_w, m_f_conv_b, m_f_down, m_final_norm, v_a_norm, v_a_in, v_a_conv, v_a_out, v_b_norm, v_b_in, v_b_vnorm, v_b_ws, v_b_bs, v_b_out, v_f_norm, v_f_up, v_f_conv_w, v_f_conv_b, v_f_down, v_final_norm):
    s, d = x.shape[1], x.shape[2]
    n_ffn = f_up.shape[0]
    f2 = f_up.shape[2] * N_DEV
    me = 4 * lax.axis_index("x") + 2 * lax.axis_index("y") + lax.axis_index("c")
    x0 = x.reshape(s, d)
    target = loss_target.reshape(s, d)

    wanted = [("a_in", _cast_layer(a_in, 0, "cast_a_in"), 1),
              ("small", _pack([a_conv, b_norm, b_vnorm, f_conv_w]), 0),
              ("a_out", _cast_layer(a_out, 0, "cast_a_out"), 0),
              ("f_up0", _cast_layer(f_up, 0, "cast_f_up0"), 1), ("f_down0", _cast_layer(f_down, 0, "cast_f_down0"), 0),
              ("b_in", _cast_layer(b_in, 0, "cast_b_in"), 1), ("b_out", _cast_layer(b_out, 0, "cast_b_out"), 0),
              ("f_up1", _cast_layer(f_up, 1, "cast_f_up1"), 1), ("f_down1", _cast_layer(f_down, 1, "cast_f_down1"), 0)]
    coming, tok, h0 = {}, None, None
    for n_started, (key, shard, axis) in enumerate(wanted):
        if n_started == 2:
            tok = h0 = _rmsnorm_fwd(x0, a_norm, "mixa_norm", after=tok)
        state, tok = _gather2_start(shard, axis, f"ag_start_{key}", after=tok)
        coming[key] = (state, axis)

    def pass_on(keys, after):
        for key in keys:
            state, axis = coming[key]
            state, after = _gather2_pass(state, axis, after, f"ag_pass_{key}")
            coming[key] = (state, axis)
        return after

    def arrived(key, after):
        state, axis = coming[key]
        return _gather2_wait(state, axis, after, f"ag_wait_{key}")

    cshard = a_conv.shape[2]
    fshard = f_conv_w.shape[2]
    w_a_in = arrived("a_in", pass_on(["a_in", "small"], tok))
    small_full = arrived("small", w_a_in)
    small_rows = small_full.reshape(N_DEV, -1)
    per_dev = _unpack_rows(small_rows, [(3, cshard), (cshard,), (cshard,), (n_ffn, 3, fshard)])
    a_conv_full = per_dev[0].transpose(1, 0, 2).reshape(3, d)
    b_norm_full = per_dev[1].reshape(1, d)
    b_vnorm_full = per_dev[2].reshape(1, d)
    f_conv_w_full = per_dev[3].transpose(1, 2, 0, 3).reshape(n_ffn, 3, f2)
    bs_wide = jnp.broadcast_to(b_bs[0][:, :, None], (SG_GROUPS, CHUNK, LANES))
    ws = b_ws[0]

    w_f_up, w_f_down = {}, {}

    def ffn_forward(xin, l, pass_first, pass_early, pass_late):
        h = _rmsnorm_fwd(xin, f_norm[l:l + 1], f"ffn{l}_norm", after=pass_on(pass_first, xin))
        w_f_up[l] = arrived(f"f_up{l}", pass_on(pass_early, h))
        up_g, up_a, cv_g, cv_a, act = _ffn_up_fused(h, w_f_up[l], f_conv_w_full[l], f_conv_b[l:l + 1], f"ffn{l}_up")
        up, cv = (up_g, up_a), (cv_g, cv_a)
        w_f_down[l] = arrived(f"f_down{l}", pass_on(pass_late, act))
        xout = _matmul(act, w_f_down[l], "nn", F32, f"ffn{l}_down", resid=xin, tm_cap=1024, tk_cap=act.shape[1],
                       hold_b=True, single_b=True)
        return xout, (h, up, act, cv)

    gb, gc, xs, cva, ya = _mixa_in_fused(h0, w_a_in, a_conv_full, "mixa_in")
    bcx = (gb, gc, xs)
    w_a_out = arrived("a_out", pass_on(["a_out"], ya))
    x1 = _matmul(ya, w_a_out, "nn", F32, "mixa_out", resid=x0, tm_cap=512, tn_cap=d, hold_b=True)
    x2, saved0 = ffn_forward(x1, 0, ["f_up0"], ["f_down0"], ["b_in", "b_out", "f_up1", "f_down1"])
    h2 = _rmsnorm_fwd(x2, b_norm_full, "mixb_norm")
    w_b_in = arrived("b_in", h2)
    pre = _matmul(h2, w_b_in, "nn", BF16, "mixb_in")
    ug = _mixb_fwd(pre, b_vnorm_full, ws, bs_wide, "mixb_mid")
    w_b_out = arrived("b_out", ug)
    x3 = _matmul(ug, w_b_out, "nn", F32, "mixb_out", resid=x2, tm_cap=512, tn_cap=d, hold_b=True)
    x4, saved1 = ffn_forward(x3, 1, [], [], [])
    dx4, dx4b, loss_part, g_final = _final_loss(x4, final_norm.reshape(1, d), target, "loss_head")

    def _rs_start(grad, axis, name):
        return _exchange_start(False, grad, axis, name)

    whole_s = dict(tm_cap=512, tk_cap=s, hold_b=True)

    def ffn_backward(xin, l, saved, dx, dxb):
        h, up, act, cv = saved
        g_down = _matmul(act, dxb, "tn", BF16, f"ffn{l}_down_dw", **whole_s)
        rs_down, tok = _rs_start(g_down, 0, f"rs_start_f_down{l}")
        dup, cwg, cwa, cbg, cba = _ffn_down_dx_fused(dxb, w_f_down[l], up, cv, f_conv_w_full[l], tok,
                                                     f"ffn{l}_down_dx")
        g_cw, g_cb = jnp.concatenate([cwg, cwa], axis=1), jnp.concatenate([cbg, cba], axis=1)
        g_up = _matmul(h, dup, "tn", BF16, f"ffn{l}_up_dw", **whole_s)
        rs_up, tok = _rs_start(g_up, 1, f"rs_start_f_up{l}")
        dh = _matmul(dup, w_f_up[l], "nt", BF16, f"ffn{l}_up_dx", after=tok, tm_cap=512, tn_cap=512,
                     tk_cap=dup.shape[1], hold_b=True)
        dxin, dxinb, g_norm = _rmsnorm_bwd(xin, f_norm[l:l + 1], dh, dx, f"ffn{l}_norm_bwd")
        return dxin, dxinb, (rs_up, rs_down, g_cw, g_cb, g_norm)

    dx3, dx3b, gf1 = ffn_backward(x3, 1, saved1, dx4, dx4b)
    g_b_out = _matmul(ug, dx3b, "tn", BF16, "mixb_out_dw", **whole_s)
    rs_b_out, tok = _rs_start(g_b_out, 0, "rs_start_b_out")
    dug = _matmul(dx3b, w_b_out, "nt", BF16, "mixb_out_dx", after=tok, tm_cap=512, tn_cap=d, hold_b=True)
    dpre, g_ws, g_bs_wide, g_bvnorm = _mixb_bwd(pre, dug, b_vnorm_full, ws, bs_wide, "mixb_mid_bwd")
    g_b_in = _matmul(h2, dpre, "tn", BF16, "mixb_in_dw", **whole_s)
    rs_b_in, tok = _rs_start(g_b_in, 1, "rs_start_b_in")
    dh2 = _matmul(dpre, w_b_in, "nt", BF16, "mixb_in_dx", after=tok, tk_cap=dpre.shape[1], hold_b=True)
    dx2, dx2b, g_bnorm = _rmsnorm_bwd(x2, b_norm_full, dh2, dx3, "mixb_norm_bwd")
    dx1, dx1b, gf0 = ffn_backward(x1, 0, saved0, dx2, dx2b)
    g_a_out = _matmul(ya, dx1b, "tn", BF16, "mixa_out_dw", **whole_s)
    rs_a_out, tok = _rs_start(g_a_out, 0, "rs_start_a_out")
    dbcx, g_aconv = _mixa_out_dx_fused(dx1b, w_a_out, bcx, cva, a_conv_full, tok, "mixa_out_dx")
    g_a_in = _matmul(h0, dbcx, "tn", BF16, "mixa_in_dw", **whole_s)
    rs_a_in, tok = _rs_start(g_a_in, 1, "rs_start_a_in")
    dh0 = _matmul(dbcx, w_a_in, "nt", BF16, "mixa_in_dx", after=tok, tm_cap=512, tk_cap=dbcx.shape[1], hold_b=True)
    grad_x, _, g_anorm = _rmsnorm_bwd(x0, a_norm, dh0, dx1, "mixa_norm_bwd")

    full_shapes = [(1, LANES), (1, d), (3, d), (1, d), (1, d), (SG_GROUPS, CHUNK, CHUNK), (SG_GROUPS, CHUNK),
                   (n_ffn, d), (n_ffn, 3, f2), (n_ffn, f2), (1, d)]
    parts = [loss_part, g_anorm, g_aconv, g_bnorm, g_bvnorm, g_ws, g_bs_wide[:, :, 0],
             jnp.concatenate([gf0[4], gf1[4]], axis=0), jnp.stack([gf0[2], gf1[2]]),
             jnp.concatenate([gf0[3], gf1[3]], axis=0), g_final]
    small_part = _pack(parts)
    small_state, small_tok = _exchange_start(True, small_part, 0, "ar_start_small", after=grad_x)

    big = {}
    for name, states, axis, w, m, v in (
            ("f_down", (gf0[1], gf1[1]), 0, f_down, m_f_down, v_f_down),
            ("f_up", (gf0[0], gf1[0]), 1, f_up, m_f_up, v_f_up),
            ("b_out", (rs_b_out,), 0, b_out, m_b_out, v_b_out), ("b_in", (rs_b_in,), 1, b_in, m_b_in, v_b_in),
            ("a_out", (rs_a_out,), 0, a_out, m_a_out, v_a_out), ("a_in", (rs_a_in,), 1, a_in, m_a_in, v_a_in)):
        recvs = [_exchange_wait(False, st, axis, small_tok, f"rs_wait_{name}{l}") for l, st in enumerate(states)]
        big[name] = _adamw_sharded(recvs, w, m, v, f"adamw_{name}")

    slots = _exchange_wait(True, small_state, 0, [res[0] for res in big.values()], "ar_wait_small")
    total = _sum_slots(slots.reshape((N_DEV,) + small_part.shape), "ar_sum_small")
    (loss_v, r_anorm, r_aconv, r_bnorm, r_bvnorm, r_ws, r_bs, r_fnorm, r_fcw, r_fcb, r_final) = _unpack(total, full_shapes)
    small_grads = [
        r_anorm,
        lax.dynamic_slice_in_dim(r_aconv, me * cshard, cshard, axis=1).reshape(a_conv.shape),
        lax.dynamic_slice_in_dim(r_bnorm, me * cshard, cshard, axis=1),
        lax.dynamic_slice_in_dim(r_bvnorm, me * cshard, cshard, axis=1),
        r_ws.reshape(b_ws.shape), r_bs.reshape(b_bs.shape), r_fnorm,
        lax.dynamic_slice_in_dim(r_fcw, me * fshard, fshard, axis=2),
        r_fcb, r_final.reshape(final_norm.shape)]
    small_w = [a_norm, a_conv, b_norm, b_vnorm, b_ws, b_bs, f_norm, f_conv_w, f_conv_b, final_norm]
    small_m = [m_a_norm, m_a_conv, m_b_norm, m_b_vnorm, m_b_ws, m_b_bs, m_f_norm, m_f_conv_w, m_f_conv_b, m_final_norm]
    small_v = [v_a_norm, v_a_conv, v_b_norm, v_b_vnorm, v_b_ws, v_b_bs, v_f_norm, v_f_conv_w, v_f_conv_b, v_final_norm]
    shapes = [w.shape for w in small_w]
    packed = _adamw_packed(_pack(small_w), _pack(small_grads), _pack(small_m), _pack(small_v), "adamw_small")
    s_delta, s_m, s_v = (_unpack(p, shapes) for p in packed)
    small_names = ["a_norm", "a_conv", "b_norm", "b_vnorm", "b_ws", "b_bs", "f_norm", "f_conv_w", "f_conv_b", "final_norm"]
    small = {nm: (small_grads[i], s_delta[i], s_m[i], s_v[i]) for i, nm in enumerate(small_names)}

    order = ["a_norm", "a_in", "a_conv", "a_out", "b_norm", "b_in", "b_vnorm", "b_ws", "b_bs", "b_out",
             "f_norm", "f_up", "f_conv_w", "f_conv_b", "f_down", "final_norm"]
    res = {nm: (big[nm] if nm in big else small[nm]) for nm in order}
    outs = [loss_v[0, 0], grad_x.reshape(x.shape)]
    for k in range(4):
        outs += [res[nm][k] for nm in order]
    return tuple(outs)


def _unpack_rows(rows, shapes):
    out, off = [], 0
    for shp in shapes:
        n = math.prod(shp)
        out.append(rows[:, off:off + n].reshape((N_DEV,) + tuple(shp)))
        off += n + (-n) % PACK_GRANULE
    return out
```
